```python
import math
import jax
import jax.numpy as jnp
from jax import lax
import numpy as np

D_MODEL = 1024
BATCH = 8
SEQ = 4096
DEPTH = 4

A_HEADS = 6
A_KDIM = 128
A_VDIM = 64
A_CHUNK = 64
B_HEADS = 6
B_HDIM = 64
B_BLOCK = 128
C_GROUPS = ((128, 1), (512, 4), (2048, 16))
C_HEADS_PER_GROUP = 4
C_HEADS = C_HEADS_PER_GROUP * len(C_GROUPS)
C_HDIM = 64
C_BLOCK = 128
D_FF = 2816
N_BRANCH = 3
N_NORMS = 6
EPS = 1e-6
NEG_BIG = -1e30
TINY = 1e-30

A_QK = A_HEADS * A_KDIM
A_V = A_HEADS * A_VDIM
B_W = B_HEADS * B_HDIM
C_W = C_HEADS * C_HDIM
C_OUT = C_HEADS_PER_GROUP * C_HDIM
IN_COLS = 2 * A_QK + 2 * A_V + 3 * B_W + 3 * C_W + N_BRANCH * D_MODEL

kernel_name = 'hybrid_hgrn2_stickbreak_dilated_macaron'


def rms_norm(x, g):
    xf = x.astype(jnp.float32)
    y = xf * lax.rsqrt(jnp.mean(xf * xf, axis=-1, keepdims=True) + EPS)
    return (y * g.astype(jnp.float32)).astype(x.dtype)


def swiglu_ffn(h, w_in, w_out):
    a, b = jnp.split(h @ w_in, 2, axis=-1)
    return (jax.nn.silu(a) * b) @ w_out


def alibi_slopes(n):
    def pow2_slopes(m):
        start = 2.0 ** (-8.0 / m)
        return [start ** (i + 1) for i in range(m)]
    if math.log2(n).is_integer():
        s = pow2_slopes(n)
    else:
        c = 2 ** int(math.floor(math.log2(n)))
        s = pow2_slopes(c) + pow2_slopes(2 * c)[0::2][: n - c]
    return sorted(s, reverse=True)


def hgrn2_mixer(q, f_raw, i, g, lb, norm_g):
    f32 = jnp.float32
    bsz, seq, nh, kd = q.shape
    vd = i.shape[-1]
    lb = lb.astype(f32)
    f_raw = f_raw.astype(f32)
    qf = jax.nn.silu(q.astype(f32))
    f = lb + (1.0 - lb) * jax.nn.sigmoid(f_raw)
    log_f = jnp.log(jnp.maximum(f, TINY))
    kf = (1.0 - lb) * jax.nn.sigmoid(-f_raw)
    vf = i.astype(f32)
    n = seq // A_CHUNK

    def to_chunks(t):
        return t.reshape(bsz, n, A_CHUNK, nh, t.shape[-1]).transpose(1, 0, 3, 2, 4)

    causal = jnp.tril(jnp.ones((A_CHUNK, A_CHUNK), dtype=bool))

    def step(state, xs):
        qc, kc, vc, lfc = xs
        b = jnp.cumsum(lfc, axis=2)
        o_inter = jnp.einsum('bhtk,bhkv->bhtv', qc * jnp.exp(b), state)
        diff = b[:, :, :, None, :] - b[:, :, None, :, :]
        decay = jnp.exp(jnp.where(causal[:, :, None], diff, NEG_BIG))
        scores = jnp.einsum('bhtk,bhsk,bhtsk->bhts', qc, kc, decay)
        o_intra = jnp.einsum('bhts,bhsv->bhtv', scores, vc)
        b_end = b[:, :, -1:, :]
        new_state = (jnp.exp(b_end[:, :, 0, :])[..., None] * state
                     + jnp.einsum('bhsk,bhsv->bhkv', kc * jnp.exp(b_end - b), vc))
        return new_state, o_inter + o_intra

    state0 = jnp.zeros((bsz, nh, kd, vd), f32)
    _, o = lax.scan(step, state0, (to_chunks(qf), to_chunks(kf), to_chunks(vf), to_chunks(log_f)))
    o = o.transpose(1, 0, 3, 2, 4).reshape(bsz, seq, nh, vd)
    o = rms_norm(o, norm_g) * jax.nn.silu(g.astype(f32))
    return o.reshape(bsz, seq, nh * vd)


def stick_breaking_mixer(q, k, v):
    f32 = jnp.float32
    bsz, seq, nh, d = q.shape
    qh, kh, vh = (t.astype(f32).transpose(0, 2, 1, 3) for t in (q, k, v))
    nb = seq // B_BLOCK
    q_blocks = qh.reshape(bsz, nh, nb, B_BLOCK, d).transpose(2, 0, 1, 3, 4)
    starts = jnp.arange(nb) * B_BLOCK
    key_pos = jnp.arange(seq)
    scale = d ** -0.5

    def block(args):
        qb, t0 = args
        z = jnp.einsum('bhtd,bhsd->bhts', qb, kh) * scale
        causal = key_pos[None, :] < (t0 + jnp.arange(B_BLOCK))[:, None]
        log_1m = jnp.where(causal, jax.nn.log_sigmoid(-z), 0.0)
        suffix = lax.cumsum(log_1m, axis=3, reverse=True) - log_1m
        a = jnp.where(causal, jnp.exp(jnp.where(causal, jax.nn.log_sigmoid(z) + suffix, NEG_BIG)), 0.0)
        return jnp.einsum('bhts,bhsd->bhtd', a, vh)

    o = lax.map(block, (q_blocks, starts))
    return o.transpose(1, 0, 3, 2, 4).reshape(bsz, seq, nh * d)


def dilated_group_attention(q, k, v, window, dilation, slopes):
    f32 = jnp.float32
    bsz, seq, nh, d = q.shape
    sub_len = seq // dilation
    nb = -(-sub_len // C_BLOCK)
    lp = nb * C_BLOCK
    pad = lp - sub_len

    def strided(t):
        return t.astype(f32).reshape(bsz, sub_len, dilation, nh, d).transpose(0, 2, 3, 1, 4)

    qs = jnp.pad(strided(q), ((0, 0), (0, 0), (0, 0), (0, pad), (0, 0)))
    ks = jnp.pad(strided(k), ((0, 0), (0, 0), (0, 0), (C_BLOCK, pad), (0, 0)))
    vs = jnp.pad(strided(v), ((0, 0), (0, 0), (0, 0), (C_BLOCK, pad), (0, 0)))
    blk = (bsz, dilation, nh, nb, C_BLOCK, d)
    qb = qs.reshape(blk)
    kb = jnp.concatenate([ks[:, :, :, :lp].reshape(blk), ks[:, :, :, C_BLOCK:].reshape(blk)], axis=4)
    vb = jnp.concatenate([vs[:, :, :, :lp].reshape(blk), vs[:, :, :, C_BLOCK:].reshape(blk)], axis=4)
    s = jnp.einsum('brhnid,brhnjd->brhnij', qb, kb) * d ** -0.5
    qi = jnp.arange(C_BLOCK)[:, None]
    kj = jnp.arange(2 * C_BLOCK)[None, :]
    delta = qi + C_BLOCK - kj
    key_idx = (jnp.arange(nb) * C_BLOCK)[:, None, None] + kj[None] - C_BLOCK
    valid = (delta >= 0) & (delta <= window // dilation) & (key_idx >= 0)
    bias = -slopes[:, None, None, None] * (delta * dilation).astype(f32)
    s = jnp.where(valid, s + bias, NEG_BIG)
    m = jnp.max(s, axis=-1, keepdims=True)
    p = jnp.exp(s - m)
    den = jnp.sum(p, axis=-1, keepdims=True)
    o = jnp.einsum('brhnij,brhnjd->brhnid', p, vb) / den
    lse = (m + jnp.log(den))[..., 0]
    o = o.reshape(bsz, dilation, nh, lp, d)[:, :, :, :sub_len].transpose(0, 3, 1, 2, 4).reshape(bsz, seq, nh, d)
    lse = lse.reshape(bsz, dilation, nh, lp)[:, :, :, :sub_len].transpose(0, 3, 1, 2).reshape(bsz, seq, nh)
    return o, lse


def dilated_mixer(q, k, v):
    bsz, seq = q.shape[:2]
    slopes = jnp.asarray(alibi_slopes(C_HEADS), jnp.float32)
    outs, lses = [], []
    for gi, (w, r) in enumerate(C_GROUPS):
        sl = slice(gi * C_HEADS_PER_GROUP, (gi + 1) * C_HEADS_PER_GROUP)
        o, lse = dilated_group_attention(q[:, :, sl], k[:, :, sl], v[:, :, sl], w, r, slopes[sl])
        outs.append(o)
        lses.append(lse)
    wts = jax.nn.softmax(jnp.stack(lses, axis=2), axis=2)
    o = jnp.sum(wts[..., None] * jnp.stack(outs, axis=2), axis=2)
    return o.reshape(bsz, seq, C_OUT)


def token_mixing(h, w_in, lb, a_norm_g, w_br_a, w_br_b, w_br_c, w_out):
    bsz, seq, _ = h.shape
    widths = [A_QK, A_QK, A_V, A_V, B_W, B_W, B_W, C_W, C_W, C_W]
    idx = [int(v) for v in np.cumsum(widths)]
    aq, af, ai, ag, bq, bk, bv, cq, ck, cv, gate_logits = jnp.split(h @ w_in, idx, axis=-1)

    def heads(t, n):
        return t.reshape(bsz, seq, n, -1)

    ya = hgrn2_mixer(heads(aq, A_HEADS), heads(af, A_HEADS), heads(ai, A_HEADS), heads(ag, A_HEADS),
                     lb.reshape(A_HEADS, A_KDIM), a_norm_g)
    yb = stick_breaking_mixer(heads(bq, B_HEADS), heads(bk, B_HEADS), heads(bv, B_HEADS))
    yc = dilated_mixer(heads(cq, C_HEADS), heads(ck, C_HEADS), heads(cv, C_HEADS))
    gates = jax.nn.sigmoid(gate_logits.reshape(bsz, seq, N_BRANCH, D_MODEL).astype(jnp.float32)).astype(h.dtype)
    merged = (gates[:, :, 0] * (ya.astype(h.dtype) @ w_br_a)
              + gates[:, :, 1] * (yb.astype(h.dtype) @ w_br_b)
              + gates[:, :, 2] * (yc.astype(h.dtype) @ w_br_c))
    return merged @ w_out


def modulated_sublayer(x, fn, g_pre, g_post, m, res_w):
    shift, scale, gate = m[:, 0][:, None], m[:, 1][:, None], m[:, 2][:, None]
    h = rms_norm(x, g_pre) * (1.0 + scale) + shift
    return x + res_w * gate * rms_norm(fn(h), g_post)


def _fwd_setup_inputs(seed: int = 0) -> dict:
    key = jax.random.key(seed)
    ks = jax.random.split(key, 18)
    nrm = jax.random.normal
    f32 = jnp.float32
    return {
        'x': nrm(ks[0], (BATCH, SEQ, D_MODEL), f32),
        'c': nrm(ks[1], (BATCH, D_MODEL), f32),
        'w_ada': nrm(ks[2], (DEPTH, D_MODEL, 9 * D_MODEL), f32) * (0.5 * D_MODEL ** -0.5),
        'b_ada': nrm(ks[3], (DEPTH, 9 * D_MODEL), f32) * 0.01,
        'norm_g': 1.0 + 0.05 * nrm(ks[4], (DEPTH, N_NORMS, D_MODEL), f32),
        'ffn1_w_in': nrm(ks[5], (DEPTH, D_MODEL, 2 * D_FF), f32) * D_MODEL ** -0.5,
        'ffn1_w_out': nrm(ks[6], (DEPTH, D_FF, D_MODEL), f32) * D_FF ** -0.5,
        'w_in': nrm(ks[7], (DEPTH, D_MODEL, IN_COLS), f32) * D_MODEL ** -0.5,
        'hgrn_lb_logits': nrm(ks[8], (DEPTH, A_QK), f32),
        'hgrn_norm_g': 1.0 + 0.05 * nrm(ks[9], (DEPTH, A_VDIM), f32),
        'w_branch_a': nrm(ks[10], (DEPTH, A_V, D_MODEL), f32) * A_V ** -0.5,
        'w_branch_b': nrm(ks[11], (DEPTH, B_W, D_MODEL), f32) * B_W ** -0.5,
        'w_branch_c': nrm(ks[12], (DEPTH, C_OUT, D_MODEL), f32) * C_OUT ** -0.5,
        'w_out': nrm(ks[13], (DEPTH, D_MODEL, D_MODEL), f32) * D_MODEL ** -0.5,
        'ffn2_w_in': nrm(ks[14], (DEPTH, D_MODEL, 2 * D_FF), f32) * D_MODEL ** -0.5,
        'ffn2_w_out': nrm(ks[15], (DEPTH, D_FF, D_MODEL), f32) * D_FF ** -0.5,
    }


def _fwd_reference(x, c, w_ada, b_ada, norm_g, ffn1_w_in, ffn1_w_out, w_in, hgrn_lb_logits, hgrn_norm_g,
              w_branch_a, w_branch_b, w_branch_c, w_out, ffn2_w_in, ffn2_w_out):
    bsz = x.shape[0]
    lb_p = jax.nn.softmax(hgrn_lb_logits.astype(jnp.float32), axis=0)
    lb_all = jnp.cumsum(lb_p, axis=0) - lb_p[0:1]
    c_act = jax.nn.silu(c)
    for l in range(DEPTH):
        mod = (c_act @ w_ada[l] + b_ada[l]).reshape(bsz, 3, 3, D_MODEL)
        x = modulated_sublayer(x, lambda h: swiglu_ffn(h, ffn1_w_in[l], ffn1_w_out[l]),
                               norm_g[l, 0], norm_g[l, 1], mod[:, 0], 0.5)
        x = modulated_sublayer(x, lambda h: token_mixing(h, w_in[l], lb_all[l], hgrn_norm_g[l], w_branch_a[l],
                                                         w_branch_b[l], w_branch_c[l], w_out[l]),
                               norm_g[l, 2], norm_g[l, 3], mod[:, 1], 1.0)
        x = modulated_sublayer(x, lambda h: swiglu_ffn(h, ffn2_w_in[l], ffn2_w_out[l]),
                               norm_g[l, 4], norm_g[l, 5], mod[:, 2], 0.5)
    return x


import jax as _jax
import jax.numpy as _jnp

TWIN_FORMAT = 'train_step'
FWD_PARAMS = ['x', 'c', 'w_ada', 'b_ada', 'norm_g', 'ffn1_w_in', 'ffn1_w_out', 'w_in', 'hgrn_lb_logits', 'hgrn_norm_g', 'w_branch_a', 'w_branch_b', 'w_branch_c', 'w_out', 'ffn2_w_in', 'ffn2_w_out']
TWIN_WEIGHTS = ['w_ada', 'b_ada', 'norm_g', 'ffn1_w_in', 'ffn1_w_out', 'w_in', 'hgrn_lb_logits', 'hgrn_norm_g', 'w_branch_a', 'w_branch_b', 'w_branch_c', 'w_out', 'ffn2_w_in', 'ffn2_w_out']
TWIN_DIFF_INPUT = 'x'
TWIN_INPUTS = ['x', 'c', 'w_ada', 'b_ada', 'norm_g', 'ffn1_w_in', 'ffn1_w_out', 'w_in', 'hgrn_lb_logits', 'hgrn_norm_g', 'w_branch_a', 'w_branch_b', 'w_branch_c', 'w_out', 'ffn2_w_in', 'ffn2_w_out', 'loss_target', 'm_w_ada', 'm_b_ada', 'm_norm_g', 'm_ffn1_w_in', 'm_ffn1_w_out', 'm_w_in', 'm_hgrn_lb_logits', 'm_hgrn_norm_g', 'm_w_branch_a', 'm_w_branch_b', 'm_w_branch_c', 'm_w_out', 'm_ffn2_w_in', 'm_ffn2_w_out', 'v_w_ada', 'v_b_ada', 'v_norm_g', 'v_ffn1_w_in', 'v_ffn1_w_out', 'v_w_in', 'v_hgrn_lb_logits', 'v_hgrn_norm_g', 'v_w_branch_a', 'v_w_branch_b', 'v_w_branch_c', 'v_w_out', 'v_ffn2_w_in', 'v_ffn2_w_out']
TWIN_OUTPUTS = ['loss', 'grad_x', 'grad_w_ada', 'grad_b_ada', 'grad_norm_g', 'grad_ffn1_w_in', 'grad_ffn1_w_out', 'grad_w_in', 'grad_hgrn_lb_logits', 'grad_hgrn_norm_g', 'grad_w_branch_a', 'grad_w_branch_b', 'grad_w_branch_c', 'grad_w_out', 'grad_ffn2_w_in', 'grad_ffn2_w_out', 'delta_w_ada', 'delta_b_ada', 'delta_norm_g', 'delta_ffn1_w_in', 'delta_ffn1_w_out', 'delta_w_in', 'delta_hgrn_lb_logits', 'delta_hgrn_norm_g', 'delta_w_branch_a', 'delta_w_branch_b', 'delta_w_branch_c', 'delta_w_out', 'delta_ffn2_w_in', 'delta_ffn2_w_out', 'new_m_w_ada', 'new_m_b_ada', 'new_m_norm_g', 'new_m_ffn1_w_in', 'new_m_ffn1_w_out', 'new_m_w_in', 'new_m_hgrn_lb_logits', 'new_m_hgrn_norm_g', 'new_m_w_branch_a', 'new_m_w_branch_b', 'new_m_w_branch_c', 'new_m_w_out', 'new_m_ffn2_w_in', 'new_m_ffn2_w_out', 'new_v_w_ada', 'new_v_b_ada', 'new_v_norm_g', 'new_v_ffn1_w_in', 'new_v_ffn1_w_out', 'new_v_w_in', 'new_v_hgrn_lb_logits', 'new_v_hgrn_norm_g', 'new_v_w_branch_a', 'new_v_w_branch_b', 'new_v_w_branch_c', 'new_v_w_out', 'new_v_ffn2_w_in', 'new_v_ffn2_w_out']
TWIN_LEAF_KINDS = {'loss': 'loss', 'grad_x': 'grad_x', 'grad_w_ada': 'grad_w', 'grad_b_ada': 'grad_w', 'grad_norm_g': 'grad_w', 'grad_ffn1_w_in': 'grad_w', 'grad_ffn1_w_out': 'grad_w', 'grad_w_in': 'grad_w', 'grad_hgrn_lb_logits': 'grad_w', 'grad_hgrn_norm_g': 'grad_w', 'grad_w_branch_a': 'grad_w', 'grad_w_branch_b': 'grad_w', 'grad_w_branch_c': 'grad_w', 'grad_w_out': 'grad_w', 'grad_ffn2_w_in': 'grad_w', 'grad_ffn2_w_out': 'grad_w', 'delta_w_ada': 'delta_w', 'delta_b_ada': 'delta_w', 'delta_norm_g': 'delta_w', 'delta_ffn1_w_in': 'delta_w', 'delta_ffn1_w_out': 'delta_w', 'delta_w_in': 'delta_w', 'delta_hgrn_lb_logits': 'delta_w', 'delta_hgrn_norm_g': 'delta_w', 'delta_w_branch_a': 'delta_w', 'delta_w_branch_b': 'delta_w', 'delta_w_branch_c': 'delta_w', 'delta_w_out': 'delta_w', 'delta_ffn2_w_in': 'delta_w', 'delta_ffn2_w_out': 'delta_w', 'new_m_w_ada': 'new_m', 'new_m_b_ada': 'new_m', 'new_m_norm_g': 'new_m', 'new_m_ffn1_w_in': 'new_m', 'new_m_ffn1_w_out': 'new_m', 'new_m_w_in': 'new_m', 'new_m_hgrn_lb_logits': 'new_m', 'new_m_hgrn_norm_g': 'new_m', 'new_m_w_branch_a': 'new_m', 'new_m_w_branch_b': 'new_m', 'new_m_w_branch_c': 'new_m', 'new_m_w_out': 'new_m', 'new_m_ffn2_w_in': 'new_m', 'new_m_ffn2_w_out': 'new_m', 'new_v_w_ada': 'new_v', 'new_v_b_ada': 'new_v', 'new_v_norm_g': 'new_v', 'new_v_ffn1_w_in': 'new_v', 'new_v_ffn1_w_out': 'new_v', 'new_v_w_in': 'new_v', 'new_v_hgrn_lb_logits': 'new_v', 'new_v_hgrn_norm_g': 'new_v', 'new_v_w_branch_a': 'new_v', 'new_v_w_branch_b': 'new_v', 'new_v_w_branch_c': 'new_v', 'new_v_w_out': 'new_v', 'new_v_ffn2_w_in': 'new_v', 'new_v_ffn2_w_out': 'new_v'}


def _forward(args):
    return _fwd_reference(*[args[k] for k in FWD_PARAMS])


def _output_shape():
    out = _jax.eval_shape(lambda: _forward(_fwd_setup_inputs(0)))
    return out.shape, out.dtype

N_MICROBATCH = 1
ADAM_LR = 0.001
ADAM_B1 = 0.9
ADAM_B2 = 0.999
ADAM_EPS = 1e-08
ADAM_WD = 0.01
ADAM_STEP = 10
PER_EXAMPLE_BATCH_AXIS = {'x': 0, 'c': 0, 'loss_target': 0}
SHARED_INPUTS = []
_WEIGHT_DTYPES = {'w_ada': _jnp.float32, 'b_ada': _jnp.float32, 'norm_g': _jnp.float32, 'ffn1_w_in': _jnp.float32, 'ffn1_w_out': _jnp.float32, 'w_in': _jnp.float32, 'hgrn_lb_logits': _jnp.float32, 'hgrn_norm_g': _jnp.float32, 'w_branch_a': _jnp.float32, 'w_branch_b': _jnp.float32, 'w_branch_c': _jnp.float32, 'w_out': _jnp.float32, 'ffn2_w_in': _jnp.float32, 'ffn2_w_out': _jnp.float32}
MOMENT_SCALE = {'w_ada': 1.119738e+00, 'b_ada': 2.035108e+00, 'norm_g': 1.582097e+00, 'ffn1_w_in': 4.490506e-02, 'ffn1_w_out': 8.188062e-02, 'w_in': 1.322491e-01, 'hgrn_lb_logits': 5.535552e-03, 'hgrn_norm_g': 6.728893e-01, 'w_branch_a': 1.645473e-01, 'w_branch_b': 3.205007e-01, 'w_branch_c': 1.903293e-01, 'w_out': 4.027531e-01, 'ffn2_w_in': 4.401298e-02, 'ffn2_w_out': 8.141140e-02}


def _to_microbatches(a, axis):
    t = _jnp.moveaxis(a, axis, 0)
    t = t.reshape((N_MICROBATCH, t.shape[0] // N_MICROBATCH) + t.shape[1:])
    return _jnp.moveaxis(t, 1, axis + 1)


def setup_inputs(seed: int = 0) -> dict:
    inp = _fwd_setup_inputs(seed)
    key = _jax.random.fold_in(_jax.random.key(seed), 7919)
    shape, _ = _output_shape()
    out = dict(inp)
    out["loss_target"] = _jax.random.normal(_jax.random.fold_in(key, 0), shape, _jnp.float32)
    for i, name in enumerate(TWIN_WEIGHTS):
        w = inp[name].astype(_jnp.float32)
        if MOMENT_SCALE is None:
            s = _jnp.sqrt(_jnp.mean(_jnp.square(w)) + 1e-30)
        else:
            s = MOMENT_SCALE[name]
        km, kv = _jax.random.split(_jax.random.fold_in(key, i + 1))
        out[name] = w
        out["m_" + name] = s * _jax.random.normal(km, w.shape, _jnp.float32)
        out["v_" + name] = (s * s) * _jax.random.uniform(kv, w.shape, _jnp.float32, 0.5, 1.5)
    if N_MICROBATCH > 1:
        for name, axis in PER_EXAMPLE_BATCH_AXIS.items():
            out[name] = _to_microbatches(out[name], axis)
    return {'x': out['x'], 'c': out['c'], 'w_ada': out['w_ada'], 'b_ada': out['b_ada'], 'norm_g': out['norm_g'], 'ffn1_w_in': out['ffn1_w_in'], 'ffn1_w_out': out['ffn1_w_out'], 'w_in': out['w_in'], 'hgrn_lb_logits': out['hgrn_lb_logits'], 'hgrn_norm_g': out['hgrn_norm_g'], 'w_branch_a': out['w_branch_a'], 'w_branch_b': out['w_branch_b'], 'w_branch_c': out['w_branch_c'], 'w_out': out['w_out'], 'ffn2_w_in': out['ffn2_w_in'], 'ffn2_w_out': out['ffn2_w_out'], 'loss_target': out['loss_target'], 'm_w_ada': out['m_w_ada'], 'm_b_ada': out['m_b_ada'], 'm_norm_g': out['m_norm_g'], 'm_ffn1_w_in': out['m_ffn1_w_in'], 'm_ffn1_w_out': out['m_ffn1_w_out'], 'm_w_in': out['m_w_in'], 'm_hgrn_lb_logits': out['m_hgrn_lb_logits'], 'm_hgrn_norm_g': out['m_hgrn_norm_g'], 'm_w_branch_a': out['m_w_branch_a'], 'm_w_branch_b': out['m_w_branch_b'], 'm_w_branch_c': out['m_w_branch_c'], 'm_w_out': out['m_w_out'], 'm_ffn2_w_in': out['m_ffn2_w_in'], 'm_ffn2_w_out': out['m_ffn2_w_out'], 'v_w_ada': out['v_w_ada'], 'v_b_ada': out['v_b_ada'], 'v_norm_g': out['v_norm_g'], 'v_ffn1_w_in': out['v_ffn1_w_in'], 'v_ffn1_w_out': out['v_ffn1_w_out'], 'v_w_in': out['v_w_in'], 'v_hgrn_lb_logits': out['v_hgrn_lb_logits'], 'v_hgrn_norm_g': out['v_hgrn_norm_g'], 'v_w_branch_a': out['v_w_branch_a'], 'v_w_branch_b': out['v_w_branch_b'], 'v_w_branch_c': out['v_w_branch_c'], 'v_w_out': out['v_w_out'], 'v_ffn2_w_in': out['v_ffn2_w_in'], 'v_ffn2_w_out': out['v_ffn2_w_out']}


def _loss(weights, diff, rest, loss_target):
    with _jax.named_scope("forward"):
        args = {**rest, TWIN_DIFF_INPUT: diff, **{k: w.astype(_WEIGHT_DTYPES[k]) for k, w in weights.items()}}
        y = _forward(args)
    with _jax.named_scope("loss_head"):
        err = _jnp.square(y.astype(_jnp.float32) - loss_target)
        return 0.5 * _jnp.sum(_jnp.mean(err, axis=-1)) if err.ndim else 0.5 * err


def _adamw(w, g, m, v):
    m = ADAM_B1 * m + (1.0 - ADAM_B1) * g
    v = ADAM_B2 * v + (1.0 - ADAM_B2) * _jnp.square(g)
    m_hat = m / (1.0 - ADAM_B1 ** ADAM_STEP)
    v_hat = v / (1.0 - ADAM_B2 ** ADAM_STEP)
    delta = -ADAM_LR * (m_hat / (_jnp.sqrt(v_hat) + ADAM_EPS) + ADAM_WD * w)
    return delta, m, v


def reference(x, c, w_ada, b_ada, norm_g, ffn1_w_in, ffn1_w_out, w_in, hgrn_lb_logits, hgrn_norm_g, w_branch_a, w_branch_b, w_branch_c, w_out, ffn2_w_in, ffn2_w_out, loss_target, m_w_ada, m_b_ada, m_norm_g, m_ffn1_w_in, m_ffn1_w_out, m_w_in, m_hgrn_lb_logits, m_hgrn_norm_g, m_w_branch_a, m_w_branch_b, m_w_branch_c, m_w_out, m_ffn2_w_in, m_ffn2_w_out, v_w_ada, v_b_ada, v_norm_g, v_ffn1_w_in, v_ffn1_w_out, v_w_in, v_hgrn_lb_logits, v_hgrn_norm_g, v_w_branch_a, v_w_branch_b, v_w_branch_c, v_w_out, v_ffn2_w_in, v_ffn2_w_out):
    given = dict(x=x, c=c, w_ada=w_ada, b_ada=b_ada, norm_g=norm_g, ffn1_w_in=ffn1_w_in, ffn1_w_out=ffn1_w_out, w_in=w_in, hgrn_lb_logits=hgrn_lb_logits, hgrn_norm_g=hgrn_norm_g, w_branch_a=w_branch_a, w_branch_b=w_branch_b, w_branch_c=w_branch_c, w_out=w_out, ffn2_w_in=ffn2_w_in, ffn2_w_out=ffn2_w_out, loss_target=loss_target, m_w_ada=m_w_ada, m_b_ada=m_b_ada, m_norm_g=m_norm_g, m_ffn1_w_in=m_ffn1_w_in, m_ffn1_w_out=m_ffn1_w_out, m_w_in=m_w_in, m_hgrn_lb_logits=m_hgrn_lb_logits, m_hgrn_norm_g=m_hgrn_norm_g, m_w_branch_a=m_w_branch_a, m_w_branch_b=m_w_branch_b, m_w_branch_c=m_w_branch_c, m_w_out=m_w_out, m_ffn2_w_in=m_ffn2_w_in, m_ffn2_w_out=m_ffn2_w_out, v_w_ada=v_w_ada, v_b_ada=v_b_ada, v_norm_g=v_norm_g, v_ffn1_w_in=v_ffn1_w_in, v_ffn1_w_out=v_ffn1_w_out, v_w_in=v_w_in, v_hgrn_lb_logits=v_hgrn_lb_logits, v_hgrn_norm_g=v_hgrn_norm_g, v_w_branch_a=v_w_branch_a, v_w_branch_b=v_w_branch_b, v_w_branch_c=v_w_branch_c, v_w_out=v_w_out, v_ffn2_w_in=v_ffn2_w_in, v_ffn2_w_out=v_ffn2_w_out)
    weights = {n: given[n] for n in TWIN_WEIGHTS}
    shared = {n: given[n] for n in SHARED_INPUTS}
    per_example = {n: given[n] for n in ['x', 'c']}
    grad_fn = _jax.value_and_grad(_loss, argnums=(0, 1))

    def one_microbatch(ex, loss_target):
        ex = dict(ex)
        diff = ex.pop(TWIN_DIFF_INPUT)
        return grad_fn(weights, diff, {**shared, **ex}, loss_target)

    if N_MICROBATCH == 1:
        loss, (grad_w, grad_x) = one_microbatch(per_example, given["loss_target"])
    else:
        def body(carry, xs):
            loss_sum, grad_sum = carry
            l_k, (gw_k, gx_k) = one_microbatch(xs[0], xs[1])
            with _jax.named_scope("update"):
                return (loss_sum + l_k, _jax.tree.map(_jnp.add, grad_sum, gw_k)), gx_k

        init = (_jnp.zeros((), _jnp.float32), _jax.tree.map(_jnp.zeros_like, weights))
        (loss, grad_w), grad_x = _jax.lax.scan(body, init, (per_example, given["loss_target"]))
    with _jax.named_scope("update"):
        delta_w, new_m, new_v = {}, {}, {}
        for n in TWIN_WEIGHTS:
            delta_w[n], new_m[n], new_v[n] = _adamw(weights[n], grad_w[n], given["m_" + n], given["v_" + n])
    return (loss, grad_x, *[grad_w[n] for n in TWIN_WEIGHTS], *[delta_w[n] for n in TWIN_WEIGHTS],
            *[new_m[n] for n in TWIN_WEIGHTS], *[new_v[n] for n in TWIN_WEIGHTS])
```

```python
import functools
import math

import jax
import jax.numpy as jnp
from jax import lax
from jax.experimental import pallas as pl
from jax.experimental.pallas import tpu as pltpu

f32, bf16 = jnp.float32, jnp.bfloat16

D = 1024
DEPTH = 4
D_FF = 2816
EPS = 1e-6
NEG_BIG = -1e30
TINY = 1e-30
A_HEADS, A_K, A_V, A_CHUNK = 6, 128, 64, 64
A_SUB = 16
A_CLAMP = 80.0
B_HEADS, HD = 6, 64
C_GROUPS = ((128, 1), (512, 4), (2048, 16))
C_BLK = 128
IN_COLS = 8832
O_AQ, O_AF, O_AI, O_AG = 0, 768, 1536, 1920
O_BQ, O_BK, O_BV = 2304, 2688, 3072
O_CQ, O_CK, O_CV = 3456, 4224, 4992
O_GATE = 5760
LANE = 128
ADAM_LR, ADAM_B1, ADAM_B2, ADAM_EPS, ADAM_WD, ADAM_STEP = 0.001, 0.9, 0.999, 1e-08, 0.01, 10
MESH = pl.DeviceIdType.MESH
VMEM_LIMIT = 56 * 1024 * 1024


def _alibi_slopes(n):
    def pow2(m):
        start = 2.0 ** (-8.0 / m)
        return [start ** (i + 1) for i in range(m)]
    if math.log2(n).is_integer():
        s = pow2(n)
    else:
        c = 2 ** int(math.floor(math.log2(n)))
        s = pow2(c) + pow2(2 * c)[0::2][: n - c]
    return sorted(s, reverse=True)


C_SLOPES = _alibi_slopes(12)


def _tile(n, prefs):
    for p in prefs:
        if n % p == 0:
            return p
    return n


def _cp(sem):
    return pltpu.CompilerParams(dimension_semantics=sem, vmem_limit_bytes=VMEM_LIMIT)


def _sig(x):
    return 1.0 / (1.0 + jnp.exp(-x))


def _dot(a, b, dn, precision=None):
    return lax.dot_general(a, b, (dn, ((), ())), preferred_element_type=f32, precision=precision)


NN = ((1,), (0,))
NT = ((1,), (1,))
TN = ((0,), (0,))


def mm(a, b, *, ta=False, tb=False, out_dtype=f32, name="mm"):
    if ta:
        K, M = a.shape
    else:
        M, K = a.shape
    if tb:
        N, K2 = b.shape
    else:
        K2, N = b.shape
    assert K == K2, (a.shape, b.shape, ta, tb)
    tm = _tile(M, (1024, 512, 384, 256, 128) if ta else (1024, 704, 512, 384, 256, 128))
    tn = _tile(N, (512, 384, 256, 128))
    tk = _tile(K, (1024, 512, 1408, 384, 256, 128))
    nk = K // tk
    dn = (0 if ta else 1,), (1 if tb else 0,)

    def body(a_ref, b_ref, o_ref, acc_ref):
        k = pl.program_id(2)
        p = _dot(a_ref[...].astype(bf16), b_ref[...].astype(bf16), dn)

        @pl.when(k == 0)
        def _():
            acc_ref[...] = p

        @pl.when(k > 0)
        def _():
            acc_ref[...] += p

        @pl.when(k == nk - 1)
        def _():
            o_ref[...] = acc_ref[...].astype(out_dtype)

    a_spec = pl.BlockSpec((tk, tm), lambda i, j, k: (k, i)) if ta else pl.BlockSpec((tm, tk), lambda i, j, k: (i, k))
    b_spec = pl.BlockSpec((tn, tk), lambda i, j, k: (j, k)) if tb else pl.BlockSpec((tk, tn), lambda i, j, k: (k, j))
    return pl.pallas_call(
        body, name=name, grid=(M // tm, N // tn, nk),
        in_specs=[a_spec, b_spec],
        out_specs=pl.BlockSpec((tm, tn), lambda i, j, k: (i, j)),
        out_shape=jax.ShapeDtypeStruct((M, N), out_dtype),
        scratch_shapes=[pltpu.VMEM((tm, tn), f32)],
        compiler_params=_cp(("parallel", "parallel", "arbitrary")),
    )(a, b)


TR = 256


def _row_spec(cols=D):
    return pl.BlockSpec((TR, cols), lambda i: (i, 0))


def _vec_spec(cols=D):
    return pl.BlockSpec((1, cols), lambda i: (0, 0))


def prenorm(x, a_vec, sh_vec):
    S = x.shape[0]

    def body(x_ref, a_ref, s_ref, h_ref):
        xv = x_ref[...]
        rstd = lax.rsqrt(jnp.mean(xv * xv, axis=1, keepdims=True) + EPS)
        h_ref[...] = (xv * rstd * a_ref[...] + s_ref[...]).astype(bf16)

    return pl.pallas_call(
        body, name="prenorm", grid=(S // TR,),
        in_specs=[_row_spec(), _vec_spec(), _vec_spec()], out_specs=_row_spec(),
        out_shape=jax.ShapeDtypeStruct((S, D), bf16), compiler_params=_cp(("parallel",)),
    )(x, a_vec, sh_vec)


def postnorm(x, y, b_vec):
    S = x.shape[0]

    def body(x_ref, y_ref, b_ref, o_ref):
        yv = y_ref[...]
        rstd = lax.rsqrt(jnp.mean(yv * yv, axis=1, keepdims=True) + EPS)
        o_ref[...] = x_ref[...] + b_ref[...] * (yv * rstd)

    return pl.pallas_call(
        body, name="postnorm", grid=(S // TR,),
        in_specs=[_row_spec(), _row_spec(), _vec_spec()], out_specs=_row_spec(),
        out_shape=jax.ShapeDtypeStruct((S, D), f32), compiler_params=_cp(("parallel",)),
    )(x, y, b_vec)


def post_bwd(dout, y, b_vec):
    S = dout.shape[0]

    def body(d_ref, y_ref, b_ref, dy_ref, db_ref):
        i = pl.program_id(0)
        yv, dv = y_ref[...], d_ref[...]
        rstd = lax.rsqrt(jnp.mean(yv * yv, axis=1, keepdims=True) + EPS)
        yh = yv * rstd
        dyh = dv * b_ref[...]
        dy_ref[...] = (rstd * (dyh - yh * jnp.mean(dyh * yh, axis=1, keepdims=True))).astype(bf16)
        part = jnp.sum(dv * yh, axis=0, keepdims=True)

        @pl.when(i == 0)
        def _():
            db_ref[...] = part

        @pl.when(i > 0)
        def _():
            db_ref[...] += part

    return pl.pallas_call(
        body, name="post_bwd", grid=(S // TR,),
        in_specs=[_row_spec(), _row_spec(), _vec_spec()], out_specs=[_row_spec(), _vec_spec()],
        out_shape=[jax.ShapeDtypeStruct((S, D), bf16), jax.ShapeDtypeStruct((1, D), f32)],
        compiler_params=_cp(("arbitrary",)),
    )(dout, y, b_vec)


def pre_bwd(dout, dh, x, a_vec):
    S = dout.shape[0]

    def body(d_ref, dh_ref, x_ref, a_ref, dx_ref, ds_ref, da_ref):
        i = pl.program_id(0)
        xv, dhv = x_ref[...], dh_ref[...]
        rstd = lax.rsqrt(jnp.mean(xv * xv, axis=1, keepdims=True) + EPS)
        n1 = xv * rstd
        dn = dhv * a_ref[...]
        dx_ref[...] = d_ref[...] + rstd * (dn - n1 * jnp.mean(dn * n1, axis=1, keepdims=True))
        p_s = jnp.sum(dhv, axis=0, keepdims=True)
        p_a = jnp.sum(dhv * n1, axis=0, keepdims=True)

        @pl.when(i == 0)
        def _():
            ds_ref[...] = p_s
            da_ref[...] = p_a

        @pl.when(i > 0)
        def _():
            ds_ref[...] += p_s
            da_ref[...] += p_a

    return pl.pallas_call(
        body, name="pre_bwd", grid=(S // TR,),
        in_specs=[_row_spec(), _row_spec(), _row_spec(), _vec_spec()],
        out_specs=[_row_spec(), _vec_spec(), _vec_spec()],
        out_shape=[jax.ShapeDtypeStruct((S, D), f32), jax.ShapeDtypeStruct((1, D), f32), jax.ShapeDtypeStruct((1, D), f32)],
        compiler_params=_cp(("arbitrary",)),
    )(dout, dh, x, a_vec)


def loss_grad(y, tgt):
    S = y.shape[0]

    def body(y_ref, t_ref, dy_ref, l_ref):
        i = pl.program_id(0)
        e = y_ref[...] - t_ref[...]
        dy_ref[...] = e * (1.0 / D)
        part = jnp.sum(jnp.sum(e * e, axis=1, keepdims=True), axis=0, keepdims=True) * (0.5 / D)
        part = jnp.broadcast_to(part, (8, LANE))

        @pl.when(i == 0)
        def _():
            l_ref[...] = part

        @pl.when(i > 0)
        def _():
            l_ref[...] += part

    return pl.pallas_call(
        body, name="loss_grad", grid=(S // TR,),
        in_specs=[_row_spec(), _row_spec()],
        out_specs=[_row_spec(), pl.BlockSpec((8, LANE), lambda i: (0, 0))],
        out_shape=[jax.ShapeDtypeStruct((S, D), f32), jax.ShapeDtypeStruct((8, LANE), f32)],
        compiler_params=_cp(("arbitrary",)),
    )(y, tgt)


FF_T = 256
FF_NB = D_FF // FF_T


def swiglu(u):
    S = u.shape[0]

    def body(a_ref, b_ref, s_ref):
        a = a_ref[...]
        s_ref[...] = (a * _sig(a) * b_ref[...]).astype(bf16)

    return pl.pallas_call(
        body, name="swiglu", grid=(S // 512, FF_NB),
        in_specs=[pl.BlockSpec((512, FF_T), lambda i, j: (i, j)), pl.BlockSpec((512, FF_T), lambda i, j: (i, j + FF_NB))],
        out_specs=pl.BlockSpec((512, FF_T), lambda i, j: (i, j)),
        out_shape=jax.ShapeDtypeStruct((S, D_FF), bf16), compiler_params=_cp(("parallel", "parallel")),
    )(u, u)


def swiglu_bwd(ds, u):
    S = u.shape[0]

    def body(ds_ref, a_ref, b_ref, du_ref):
        j = pl.program_id(1)
        a, b, dsv = a_ref[...], b_ref[...], ds_ref[...]
        sg = _sig(a)
        da = dsv * b * sg * (1.0 + a * (1.0 - sg))
        db = dsv * a * sg
        du_ref[...] = jnp.where(j < FF_NB, da, db).astype(bf16)

    return pl.pallas_call(
        body, name="swiglu_bwd", grid=(S // 512, 2 * FF_NB),
        in_specs=[pl.BlockSpec((512, FF_T), lambda i, j: (i, j % FF_NB)),
                  pl.BlockSpec((512, FF_T), lambda i, j: (i, j % FF_NB)),
                  pl.BlockSpec((512, FF_T), lambda i, j: (i, j % FF_NB + FF_NB))],
        out_specs=pl.BlockSpec((512, FF_T), lambda i, j: (i, j)),
        out_shape=jax.ShapeDtypeStruct((S, 2 * D_FF), bf16), compiler_params=_cp(("parallel", "parallel")),
    )(ds, u, u)


G_NB = D // LANE
G_OFF = O_GATE // LANE


def gate_merge(u, pa, pb, pc):
    S = u.shape[0]

    def body(g0, g1, g2, a, b, c, o_ref):
        o_ref[...] = (_sig(g0[...]) * a[...] + _sig(g1[...]) * b[...] + _sig(g2[...]) * c[...]).astype(bf16)

    gs = [pl.BlockSpec((512, LANE), functools.partial(lambda i, j, k: (i, G_OFF + G_NB * k + j), k=k)) for k in range(3)]
    ps = pl.BlockSpec((512, LANE), lambda i, j: (i, j))
    return pl.pallas_call(
        body, name="gate_merge", grid=(S // 512, G_NB),
        in_specs=gs + [ps, ps, ps], out_specs=ps,
        out_shape=jax.ShapeDtypeStruct((S, D), bf16), compiler_params=_cp(("parallel", "parallel")),
    )(u, u, u, pa, pb, pc)


def gate_bwd(dm, u, pa, pb, pc):
    S = u.shape[0]

    def body(dm_ref, g0, g1, g2, a, b, c, da, db, dc, dg0, dg1, dg2):
        d = dm_ref[...]
        for g, p, dp, dg in ((g0, a, da, dg0), (g1, b, db, dg1), (g2, c, dc, dg2)):
            s = _sig(g[...])
            dp[...] = (d * s).astype(bf16)
            dg[...] = (d * p[...] * s * (1.0 - s)).astype(bf16)

    gs = [pl.BlockSpec((512, LANE), functools.partial(lambda i, j, k: (i, G_OFF + G_NB * k + j), k=k)) for k in range(3)]
    ps = pl.BlockSpec((512, LANE), lambda i, j: (i, j))
    osd = jax.ShapeDtypeStruct((S, D), bf16)
    return pl.pallas_call(
        body, name="gate_bwd", grid=(S // 512, G_NB),
        in_specs=[ps] + gs + [ps, ps, ps], out_specs=[ps] * 6, out_shape=[osd] * 6,
        compiler_params=_cp(("parallel", "parallel")),
    )(dm, u, u, u, pa, pb, pc)


A_TB = 512
A_NCH = A_TB // A_CHUNK
A_NSUB = A_CHUNK // A_SUB


def _hgrn_gates(qr, fr, lbh):
    sq = _sig(qr)
    sig = _sig(fr)
    f = lbh + (1.0 - lbh) * sig
    logf = jnp.log(jnp.maximum(f, TINY))
    k = (1.0 - lbh) * (1.0 - sig)
    return qr * sq, sq, sig, f, logf, k


def _hgrn_intra(qf, k, b, causal):
    qts, kts, eqs, eks, blocks = [], [], [], [], []
    for sb in range(A_NSUB):
        rs = sb * A_SUB
        r = b[rs - 1:rs, :] if sb else jnp.zeros((1, A_K), f32)
        eq = jnp.exp(b[rs:rs + A_SUB, :] - r)
        ek = jnp.exp(jnp.minimum(r - b, A_CLAMP))
        qt = (qf[rs:rs + A_SUB, :] * eq).astype(bf16)
        kt = (k * ek).astype(bf16)
        blocks.append(_dot(qt, kt, NT))
        qts.append(qt), kts.append(kt), eqs.append(eq), eks.append(ek)
    a = jnp.where(causal, jnp.concatenate(blocks, axis=0), 0.0)
    return a, qts, kts, eqs, eks


def _tri():
    r = lax.broadcasted_iota(jnp.int32, (A_CHUNK, A_CHUNK), 0)
    c = lax.broadcasted_iota(jnp.int32, (A_CHUNK, A_CHUNK), 1)
    return r >= c


def _hgrn_in_specs(rev_nb=None):
    def im(col):
        if rev_nb is None:
            return lambda p, i: (i, col + p)
        return lambda p, i: (rev_nb - 1 - i, col + p)
    return [pl.BlockSpec((A_TB, 256), im(O_AQ // 256)), pl.BlockSpec((A_TB, 256), im(O_AF // 256)),
            pl.BlockSpec((A_TB, LANE), im(O_AI // LANE)), pl.BlockSpec((A_TB, LANE), im(O_AG // LANE)),
            pl.BlockSpec((1, 256), lambda p, i: (0, p)), pl.BlockSpec((1, A_V), lambda p, i: (0, 0))]


def hgrn_fwd(u, lb, ng):
    S = u.shape[0]
    nb = S // A_TB

    def body(q_ref, f_ref, i_ref, g_ref, lb_ref, ng_ref, o_ref, ya_ref, st_ref, state):
        @pl.when(pl.program_id(1) == 0)
        def _():
            state[...] = jnp.zeros_like(state)

        causal = _tri()
        tri = causal.astype(f32)

        def chunk(n, carry):
            rows = pl.ds(pl.multiple_of(n * A_CHUNK, A_CHUNK), A_CHUNK)
            o_parts, y_parts = [], []
            for hh in range(2):
                ks = slice(hh * A_K, (hh + 1) * A_K)
                vs = slice(hh * A_V, (hh + 1) * A_V)
                qf, _, _, _, logf, k = _hgrn_gates(q_ref[rows, ks], f_ref[rows, ks], lb_ref[:, ks])
                vi = i_ref[rows, vs].astype(bf16)
                gg = g_ref[rows, vs]
                b = _dot(tri, logf, NN, precision=lax.Precision.HIGHEST)
                s0 = state[hh]
                st_ref[n, hh] = s0
                o = _dot((qf * jnp.exp(b)).astype(bf16), s0.astype(bf16), NT)
                a, _, _, _, _ = _hgrn_intra(qf, k, b, causal)
                o = o + _dot(a.astype(bf16), vi, NN)
                bend = b[A_CHUNK - 1:A_CHUNK, :]
                ke = (k * jnp.exp(bend - b)).astype(bf16)
                state[hh] = s0 * jnp.exp(bend) + _dot(vi, ke, TN)
                rstd = lax.rsqrt(jnp.mean(o * o, axis=1, keepdims=True) + EPS)
                o_parts.append(o)
                y_parts.append(o * rstd * ng_ref[...] * (gg * _sig(gg)))
            o_ref[rows, :] = jnp.concatenate(o_parts, axis=1)
            ya_ref[rows, :] = jnp.concatenate(y_parts, axis=1).astype(bf16)
            return carry

        lax.fori_loop(0, A_NCH, chunk, 0)

    return pl.pallas_call(
        body, name="hgrn_fwd", grid=(3, nb),
        in_specs=_hgrn_in_specs(),
        out_specs=[pl.BlockSpec((A_TB, LANE), lambda p, i: (i, p)), pl.BlockSpec((A_TB, LANE), lambda p, i: (i, p)),
                   pl.BlockSpec((A_NCH, 2, A_V, A_K), lambda p, i: (i, p, 0, 0))],
        out_shape=[jax.ShapeDtypeStruct((S, 384), f32), jax.ShapeDtypeStruct((S, 384), bf16),
                   jax.ShapeDtypeStruct((S // A_CHUNK, A_HEADS, A_V, A_K), f32)],
        scratch_shapes=[pltpu.VMEM((2, A_V, A_K), f32)],
        compiler_params=_cp(("parallel", "arbitrary")),
    )(u, u, u, u, lb, ng)


def hgrn_bwd(u, lb, ng, o, st, dya):
    S = u.shape[0]
    nb = S // A_TB

    def body(q_ref, f_ref, i_ref, g_ref, lb_ref, ng_ref, o_ref, st_ref, dy_ref,
             dq_ref, df_ref, di_ref, dg_ref, dlb_ref, dng_ref, dstate):
        @pl.when(pl.program_id(1) == 0)
        def _():
            dstate[...] = jnp.zeros_like(dstate)
            dlb_ref[...] = jnp.zeros_like(dlb_ref)
            dng_ref[...] = jnp.zeros_like(dng_ref)

        causal = _tri()
        tri = causal.astype(f32)

        def chunk(it, carry):
            n = A_NCH - 1 - it
            rows = pl.ds(pl.multiple_of(n * A_CHUNK, A_CHUNK), A_CHUNK)
            dq_p, df_p, di_p, dg_p, dlb_p = [], [], [], [], []
            dng_acc = jnp.zeros((1, A_V), f32)
            for hh in range(2):
                ks = slice(hh * A_K, (hh + 1) * A_K)
                vs = slice(hh * A_V, (hh + 1) * A_V)
                lbh = lb_ref[:, ks]
                qr = q_ref[rows, ks]
                qf, sq, sig, f, logf, k = _hgrn_gates(qr, f_ref[rows, ks], lbh)
                vi = i_ref[rows, vs].astype(bf16)
                gg = g_ref[rows, vs]
                b = _dot(tri, logf, NN, precision=lax.Precision.HIGHEST)
                eb = jnp.exp(b)
                bend = b[A_CHUNK - 1:A_CHUNK, :]
                eend = jnp.exp(bend)
                ekend = jnp.exp(bend - b)
                qe = (qf * eb).astype(bf16)
                ke = (k * ekend).astype(bf16)
                s0 = st_ref[n, hh]
                dsend = dstate[hh]
                ov = o_ref[rows, vs]
                dy = dy_ref[rows, vs]
                rstd = lax.rsqrt(jnp.mean(ov * ov, axis=1, keepdims=True) + EPS)
                oh = ov * rstd
                sg = _sig(gg)
                d_on = dy * (gg * sg)
                dg_p.append(dy * oh * ng_ref[...] * (sg * (1.0 + gg * (1.0 - sg))))
                dng_acc = dng_acc + jnp.sum(d_on * oh, axis=0, keepdims=True)
                doh = d_on * ng_ref[...]
                do = (rstd * (doh - oh * jnp.mean(doh * oh, axis=1, keepdims=True))).astype(bf16)
                a, qts, kts, eqs, eks = _hgrn_intra(qf, k, b, causal)
                da = jnp.where(causal, _dot(do, vi, NT), 0.0).astype(bf16)
                dsb = dsend.astype(bf16)
                dv = _dot(a.astype(bf16), do, TN) + _dot(ke, dsb, NT)
                dq = _dot(do, s0.astype(bf16), NN) * eb
                dk_state = _dot(vi, dsb, NN) * ekend
                dk = dk_state
                dq_i = []
                for sb in range(A_NSUB):
                    da_sb = da[sb * A_SUB:(sb + 1) * A_SUB, :]
                    dq_i.append(_dot(da_sb, kts[sb], NN) * eqs[sb])
                    dk = dk + _dot(da_sb, qts[sb], TN) * eks[sb]
                dq = dq + jnp.concatenate(dq_i, axis=0)
                db = qf * dq - k * dk
                extra = jnp.sum(k * dk_state, axis=0, keepdims=True) + eend * jnp.sum(s0 * dsend, axis=0, keepdims=True)
                dlogf = _dot(tri, db, TN, precision=lax.Precision.HIGHEST) + extra
                dstate[hh] = _dot(do, qe, TN) + eend * dsend
                d_pre = jnp.where(f > TINY, dlogf / f, 0.0) - dk
                dlb_p.append(jnp.sum((1.0 - sig) * d_pre, axis=0, keepdims=True))
                df_p.append((1.0 - lbh) * d_pre * sig * (1.0 - sig))
                dq_p.append(dq * (sq * (1.0 + qr * (1.0 - sq))))
                di_p.append(dv)
            dq_ref[rows, :] = jnp.concatenate(dq_p, axis=1).astype(bf16)
            df_ref[rows, :] = jnp.concatenate(df_p, axis=1).astype(bf16)
            di_ref[rows, :] = jnp.concatenate(di_p, axis=1).astype(bf16)
            dg_ref[rows, :] = jnp.concatenate(dg_p, axis=1).astype(bf16)
            dlb_ref[...] += jnp.concatenate(dlb_p, axis=1)
            dng_ref[0] += dng_acc
            return carry

        lax.fori_loop(0, A_NCH, chunk, 0)

    rev = lambda p, i: (nb - 1 - i, p)
    return pl.pallas_call(
        body, name="hgrn_bwd", grid=(3, nb),
        in_specs=_hgrn_in_specs(nb) + [pl.BlockSpec((A_TB, LANE), rev),
                                       pl.BlockSpec((A_NCH, 2, A_V, A_K), lambda p, i: (nb - 1 - i, p, 0, 0)),
                                       pl.BlockSpec((A_TB, LANE), rev)],
        out_specs=[pl.BlockSpec((A_TB, 256), rev), pl.BlockSpec((A_TB, 256), rev),
                   pl.BlockSpec((A_TB, LANE), rev), pl.BlockSpec((A_TB, LANE), rev),
                   pl.BlockSpec((1, 256), lambda p, i: (0, p)), pl.BlockSpec((1, 1, A_V), lambda p, i: (p, 0, 0))],
        out_shape=[jax.ShapeDtypeStruct((S, 768), bf16), jax.ShapeDtypeStruct((S, 768), bf16),
                   jax.ShapeDtypeStruct((S, 384), bf16), jax.ShapeDtypeStruct((S, 384), bf16),
                   jax.ShapeDtypeStruct((1, 768), f32), jax.ShapeDtypeStruct((3, 1, A_V), f32)],
        scratch_shapes=[pltpu.VMEM((2, A_V, A_K), f32)],
        compiler_params=_cp(("parallel", "arbitrary")),
    )(u, u, u, u, lb, ng, o, st, dya)


B_TK = 128
SCALE = HD ** -0.5


def _split(x):
    hi = x.astype(bf16)
    return hi, (x - hi.astype(f32)).astype(bf16)


def _dot2(x, m, dn):
    hi, lo = _split(x)
    return _dot(hi, m, dn) + _dot(lo, m, dn)


def _sb_block(qs, kh, mask, m_gt, c):
    z = _dot(qs, kh, NT)
    sp = jnp.maximum(z, 0.0) + jnp.log(1.0 + jnp.exp(-jnp.abs(z)))
    lneg = jnp.where(mask, -sp, 0.0)
    lsz = z - sp
    suf = _dot2(lneg, m_gt, NN) + c
    a = jnp.where(mask, jnp.exp(lsz + suf), 0.0)
    return lneg, lsz, a


def _sb_masks(tq, i, jj):
    t_idx = i * tq + lax.broadcasted_iota(jnp.int32, (tq, B_TK), 0)
    s_idx = jj * B_TK + lax.broadcasted_iota(jnp.int32, (tq, B_TK), 1)
    return s_idx < t_idx


def _sb_tri(strict):
    r = lax.broadcasted_iota(jnp.int32, (B_TK, B_TK), 0)
    c = lax.broadcasted_iota(jnp.int32, (B_TK, B_TK), 1)
    return (r > c if strict else r >= c).astype(bf16)


def sb_fwd(u):
    S = u.shape[0]
    tq = 256

    def body(q_ref, k_ref, v_ref, o_ref):
        i = pl.program_id(1)
        nkb = (i + 1) * (tq // B_TK)
        m_gt = _sb_tri(True)
        qs = [(q_ref[:, hh * HD:(hh + 1) * HD] * SCALE).astype(bf16) for hh in range(2)]

        def step(it, carry):
            jj = nkb - 1 - it
            rows = pl.ds(pl.multiple_of(jj * B_TK, B_TK), B_TK)
            mask = _sb_masks(tq, i, jj)
            kb, vb = k_ref[rows, :], v_ref[rows, :]
            out = []
            for hh in range(2):
                acc, c = carry[hh]
                kh = kb[:, hh * HD:(hh + 1) * HD].astype(bf16)
                vh = vb[:, hh * HD:(hh + 1) * HD].astype(bf16)
                lneg, _, a = _sb_block(qs[hh], kh, mask, m_gt, c)
                out.append((acc + _dot2(a, vh, NN), c + jnp.sum(lneg, axis=1, keepdims=True)))
            return tuple(out)

        z0 = (jnp.zeros((tq, HD), f32), jnp.zeros((tq, 1), f32))
        res = lax.fori_loop(0, nkb, step, (z0, z0))
        o_ref[...] = jnp.concatenate([res[0][0], res[1][0]], axis=1)

    return pl.pallas_call(
        body, name="sb_fwd", grid=(3, S // tq),
        in_specs=[pl.BlockSpec((tq, LANE), lambda p, i: (i, O_BQ // LANE + p)),
                  pl.BlockSpec((S, LANE), lambda p, i: (0, O_BK // LANE + p)),
                  pl.BlockSpec((S, LANE), lambda p, i: (0, O_BV // LANE + p))],
        out_specs=pl.BlockSpec((tq, LANE), lambda p, i: (i, p)),
        out_shape=jax.ShapeDtypeStruct((S, 384), f32),
        compiler_params=_cp(("parallel", "arbitrary")),
    )(u, u, u)


def sb_bwd(u, yb, dyb):
    S = u.shape[0]
    tq = 128

    def body(q_ref, k_ref, v_ref, y_ref, dy_ref, dq_ref, dk_ref, dv_ref):
        i = pl.program_id(1)

        @pl.when(i == 0)
        def _():
            dk_ref[...] = jnp.zeros_like(dk_ref)
            dv_ref[...] = jnp.zeros_like(dv_ref)

        nkb = (i + 1) * (tq // B_TK)
        m_gt = _sb_tri(True)
        m_ge = _sb_tri(False)
        qs, dos, tot = [], [], []
        for hh in range(2):
            hs = slice(hh * HD, (hh + 1) * HD)
            qs.append((q_ref[:, hs] * SCALE).astype(bf16))
            dob = dy_ref[:, hs].astype(bf16)
            dos.append(dob)
            tot.append(jnp.sum(dob.astype(f32) * y_ref[:, hs], axis=1, keepdims=True))

        def step(it, carry):
            jj = nkb - 1 - it
            rows = pl.ds(pl.multiple_of(jj * B_TK, B_TK), B_TK)
            mask = _sb_masks(tq, i, jj)
            kb, vb = k_ref[rows, :], v_ref[rows, :]
            out, dk_p, dv_p = [], [], []
            for hh in range(2):
                dq, c, cg = carry[hh]
                kh = kb[:, hh * HD:(hh + 1) * HD].astype(bf16)
                vh = vb[:, hh * HD:(hh + 1) * HD].astype(bf16)
                lneg, lsz, a = _sb_block(qs[hh], kh, mask, m_gt, c)
                g = a * _dot(dos[hh], vh, NT)
                p = tot[hh] - cg - _dot2(g, m_ge, NN)
                beta = jnp.exp(lsz)
                dz = jnp.where(mask, g * (1.0 - beta) - beta * p, 0.0).astype(bf16)
                dk_p.append(_dot(dz, qs[hh], TN))
                dv_p.append(_dot(a.astype(bf16), dos[hh], TN))
                out.append((dq + _dot(dz, kh, NN), c + jnp.sum(lneg, axis=1, keepdims=True),
                            cg + jnp.sum(g, axis=1, keepdims=True)))
            dk_ref[rows, :] += jnp.concatenate(dk_p, axis=1)
            dv_ref[rows, :] += jnp.concatenate(dv_p, axis=1)
            return tuple(out)

        z0 = (jnp.zeros((tq, HD), f32), jnp.zeros((tq, 1), f32), jnp.zeros((tq, 1), f32))
        res = lax.fori_loop(0, nkb, step, (z0, z0))
        dq_ref[...] = jnp.concatenate([res[0][0], res[1][0]], axis=1) * SCALE

    row = pl.BlockSpec((tq, LANE), lambda p, i: (i, p))
    full = pl.BlockSpec((S, LANE), lambda p, i: (0, p))
    osd = jax.ShapeDtypeStruct((S, 384), f32)
    return pl.pallas_call(
        body, name="sb_bwd", grid=(3, S // tq),
        in_specs=[pl.BlockSpec((tq, LANE), lambda p, i: (i, O_BQ // LANE + p)),
                  pl.BlockSpec((S, LANE), lambda p, i: (0, O_BK // LANE + p)),
                  pl.BlockSpec((S, LANE), lambda p, i: (0, O_BV // LANE + p)), row, row],
        out_specs=[row, full, full], out_shape=[osd, osd, osd],
        compiler_params=_cp(("parallel", "arbitrary")),
    )(u, u, u, yb, dyb)


U_NB = IN_COLS // LANE


def _dil_scores(qs, kc, kp, i, slope_r):
    qi = lax.broadcasted_iota(jnp.int32, (C_BLK, C_BLK), 0)
    kj = lax.broadcasted_iota(jnp.int32, (C_BLK, C_BLK), 1)
    d_c = qi - kj
    d_p = d_c + C_BLK
    ok_c = d_c >= 0
    ok_p = jnp.logical_and(d_c <= 0, i > 0)
    s_c = jnp.where(ok_c, _dot(qs, kc, NT) - slope_r * d_c.astype(f32), NEG_BIG)
    s_p = jnp.where(ok_p, _dot(qs, kp, NT) - slope_r * d_p.astype(f32), NEG_BIG)
    return s_c, s_p, ok_c, ok_p


def _dil_slope(g, r, hh):
    pair = pl.program_id(1)
    return jnp.where(pair == 0, C_SLOPES[4 * g + hh] * r, C_SLOPES[4 * g + 2 + hh] * r).astype(f32)


def _dil_u_specs(g, r, L):
    def im(off):
        return lambda rho, p: (0, rho * U_NB + (off + g * 256) // LANE + p)
    return [pl.BlockSpec((L, LANE), im(O_CQ)), pl.BlockSpec((L, LANE), im(O_CK)), pl.BlockSpec((L, LANE), im(O_CV))]


def dil_fwd(u, g):
    S = u.shape[0]
    r = C_GROUPS[g][1]
    L = S // r
    nbk = L // C_BLK

    def body(q_ref, k_ref, v_ref, o_ref, l_ref):
        def step(i, carry):
            rc = pl.ds(pl.multiple_of(i * C_BLK, C_BLK), C_BLK)
            rp = pl.ds(pl.multiple_of(jnp.maximum(i - 1, 0) * C_BLK, C_BLK), C_BLK)
            o_p, l_p = [], []
            for hh in range(2):
                hs = slice(hh * HD, (hh + 1) * HD)
                qs = (q_ref[rc, hs] * SCALE).astype(bf16)
                kc, kp = k_ref[rc, hs].astype(bf16), k_ref[rp, hs].astype(bf16)
                vc, vp = v_ref[rc, hs].astype(bf16), v_ref[rp, hs].astype(bf16)
                s_c, s_p, _, _ = _dil_scores(qs, kc, kp, i, _dil_slope(g, r, hh))
                m = jnp.maximum(jnp.max(s_c, axis=1, keepdims=True), jnp.max(s_p, axis=1, keepdims=True))
                p_c, p_p = jnp.exp(s_c - m), jnp.exp(s_p - m)
                den = jnp.sum(p_c, axis=1, keepdims=True) + jnp.sum(p_p, axis=1, keepdims=True)
                o_p.append((_dot(p_c.astype(bf16), vc, NN) + _dot(p_p.astype(bf16), vp, NN)) / den)
                l_p.append(jnp.broadcast_to(m + jnp.log(den), (C_BLK, HD)))
            o_ref[rc, :] = jnp.concatenate(o_p, axis=1)
            l_ref[rc, :] = jnp.concatenate(l_p, axis=1)
            return carry

        lax.fori_loop(0, nbk, step, 0)

    ospec = pl.BlockSpec((L, LANE), lambda rho, p: (0, rho * 2 + p))
    osd = jax.ShapeDtypeStruct((L, r * 256), f32)
    u2 = u.reshape(L, r * IN_COLS)
    o, lse = pl.pallas_call(
        body, name=f"dil_fwd{g}", grid=(r, 2),
        in_specs=_dil_u_specs(g, r, L), out_specs=[ospec, ospec], out_shape=[osd, osd],
        compiler_params=_cp(("parallel", "parallel")),
    )(u2, u2, u2)
    return o.reshape(S, 256), lse.reshape(S, 256)


def dil_merge(os_, ls_):
    S = os_[0].shape[0]

    def body(o0, o1, o2, l0, l1, l2, y_ref, lse_ref):
        a, b, c = l0[...], l1[...], l2[...]
        m = jnp.maximum(jnp.maximum(a, b), c)
        ea, eb, ec = jnp.exp(a - m), jnp.exp(b - m), jnp.exp(c - m)
        den = ea + eb + ec
        y_ref[...] = (ea * o0[...] + eb * o1[...] + ec * o2[...]) / den
        lse_ref[...] = m + jnp.log(den)

    spec = pl.BlockSpec((512, 256), lambda i: (i, 0))
    osd = jax.ShapeDtypeStruct((S, 256), f32)
    return pl.pallas_call(
        body, name="dil_merge", grid=(S // 512,), in_specs=[spec] * 6, out_specs=[spec, spec],
        out_shape=[osd, osd], compiler_params=_cp(("parallel",)),
    )(*os_, *ls_)


def dil_bwd(u, g, dyc, yc, lse):
    S = u.shape[0]
    r = C_GROUPS[g][1]
    L = S // r
    nbk = L // C_BLK

    def body(q_ref, k_ref, v_ref, dy_ref, y_ref, l_ref, dq_ref, dk_ref, dv_ref):
        dk_ref[...] = jnp.zeros_like(dk_ref)
        dv_ref[...] = jnp.zeros_like(dv_ref)

        def step(i, carry):
            rc = pl.ds(pl.multiple_of(i * C_BLK, C_BLK), C_BLK)
            rp = pl.ds(pl.multiple_of(jnp.maximum(i - 1, 0) * C_BLK, C_BLK), C_BLK)
            dq_p, dkc_p, dkp_p, dvc_p, dvp_p = [], [], [], [], []
            for hh in range(2):
                hs = slice(hh * HD, (hh + 1) * HD)
                qs = (q_ref[rc, hs] * SCALE).astype(bf16)
                kc, kp = k_ref[rc, hs].astype(bf16), k_ref[rp, hs].astype(bf16)
                vc, vp = v_ref[rc, hs].astype(bf16), v_ref[rp, hs].astype(bf16)
                dy = dy_ref[rc, hs]
                dyb = dy.astype(bf16)
                s_c, s_p, ok_c, ok_p = _dil_scores(qs, kc, kp, i, _dil_slope(g, r, hh))
                lrow = l_ref[rc, hh * HD:hh * HD + 1]
                delta = jnp.sum(dy * y_ref[rc, hs], axis=1, keepdims=True)
                pi_c = jnp.where(ok_c, jnp.exp(s_c - lrow), 0.0)
                pi_p = jnp.where(ok_p, jnp.exp(s_p - lrow), 0.0)
                ds_c = (pi_c * (_dot(dyb, vc, NT) - delta)).astype(bf16)
                ds_p = (pi_p * (_dot(dyb, vp, NT) - delta)).astype(bf16)
                dq_p.append((_dot(ds_c, kc, NN) + _dot(ds_p, kp, NN)) * SCALE)
                dkc_p.append(_dot(ds_c, qs, TN))
                dkp_p.append(_dot(ds_p, qs, TN))
                dvc_p.append(_dot(pi_c.astype(bf16), dyb, TN))
                dvp_p.append(_dot(pi_p.astype(bf16), dyb, TN))
            dq_ref[rc, :] = jnp.concatenate(dq_p, axis=1)
            dk_ref[rc, :] += jnp.concatenate(dkc_p, axis=1)
            dv_ref[rc, :] += jnp.concatenate(dvc_p, axis=1)
            dk_ref[rp, :] += jnp.concatenate(dkp_p, axis=1)
            dv_ref[rp, :] += jnp.concatenate(dvp_p, axis=1)
            return carry

        lax.fori_loop(0, nbk, step, 0)

    ospec = pl.BlockSpec((L, LANE), lambda rho, p: (0, rho * 2 + p))
    osd = jax.ShapeDtypeStruct((L, r * 256), f32)
    u2 = u.reshape(L, r * IN_COLS)
    v2 = lambda t: t.reshape(L, r * 256)
    dq, dk, dv = pl.pallas_call(
        body, name=f"dil_bwd{g}", grid=(r, 2),
        in_specs=_dil_u_specs(g, r, L) + [ospec, ospec, ospec], out_specs=[ospec] * 3, out_shape=[osd] * 3,
        compiler_params=_cp(("parallel", "parallel")),
    )(u2, u2, u2, v2(dyc), v2(yc), v2(lse))
    return dq.reshape(S, 256), dk.reshape(S, 256), dv.reshape(S, 256)


def _rows_tile(rows):
    return _tile(rows, (256, 176, 128, 64, 32, 16, 8))


def cast_bf16(w):
    shape = w.shape
    w2 = w.reshape(-1, shape[-1])
    rows, cols = w2.shape
    tr = _rows_tile(rows)

    def body(x_ref, o_ref):
        o_ref[...] = x_ref[...].astype(bf16)

    spec = pl.BlockSpec((tr, cols), lambda i: (i, 0))
    out = pl.pallas_call(
        body, name="cast_bf16", grid=(rows // tr,), in_specs=[spec], out_specs=spec,
        out_shape=jax.ShapeDtypeStruct((rows, cols), bf16), compiler_params=_cp(("parallel",)),
    )(w2)
    return out.reshape(shape)


BC1 = 1.0 - ADAM_B1 ** ADAM_STEP
BC2 = 1.0 - ADAM_B2 ** ADAM_STEP


def _adam_math(w, g, m, v):
    m2 = ADAM_B1 * m + (1.0 - ADAM_B1) * g
    v2 = ADAM_B2 * v + (1.0 - ADAM_B2) * (g * g)
    delta = -ADAM_LR * ((m2 / BC1) / (jnp.sqrt(v2 / BC2) + ADAM_EPS) + ADAM_WD * w)
    return delta, m2, v2


def adam(w, g, m, v):
    shape = w.shape
    r2 = lambda t: t.reshape(-1, shape[-1])
    rows, cols = r2(w).shape
    tr = _rows_tile(rows)

    def body(w_ref, g_ref, m_ref, v_ref, d_ref, m2_ref, v2_ref):
        d_ref[...], m2_ref[...], v2_ref[...] = _adam_math(w_ref[...], g_ref[...], m_ref[...], v_ref[...])

    spec = pl.BlockSpec((tr, cols), lambda i: (i, 0))
    osd = jax.ShapeDtypeStruct((rows, cols), f32)
    outs = pl.pallas_call(
        body, name="adam", grid=(rows // tr,), in_specs=[spec] * 4, out_specs=[spec] * 3, out_shape=[osd] * 3,
        compiler_params=_cp(("parallel",)),
    )(r2(w), r2(g), r2(m), r2(v))
    return [o.reshape(shape) for o in outs]


ADA_N = 9 * D // 4
ADA_TN = 384


def ada_fwd(c_all, w_ada):
    def body(c_ref, w_ref, o_ref):
        cv = c_ref[...]
        o_ref[0] = _dot((cv * _sig(cv)).astype(bf16), w_ref[0].astype(bf16), NN)

    return pl.pallas_call(
        body, name="ada_fwd", grid=(DEPTH, ADA_N // ADA_TN),
        in_specs=[pl.BlockSpec((8, D), lambda l, j: (0, 0)), pl.BlockSpec((1, D, ADA_TN), lambda l, j: (l, 0, j))],
        out_specs=pl.BlockSpec((1, 8, ADA_TN), lambda l, j: (l, 0, j)),
        out_shape=jax.ShapeDtypeStruct((DEPTH, 8, ADA_N), f32), compiler_params=_cp(("parallel", "parallel")),
    )(c_all, w_ada)


def ada_bwd_adam(c_all, dm, w, m, v):
    tr = 128

    def body(c_ref, dm_ref, w_ref, m_ref, v_ref, g_ref, d_ref, m2_ref, v2_ref):
        cv = c_ref[...]
        g = _dot((cv * _sig(cv)).astype(bf16), dm_ref[0].astype(bf16), TN)
        g_ref[0] = g
        d_ref[0], m2_ref[0], v2_ref[0] = _adam_math(w_ref[0], g, m_ref[0], v_ref[0])

    wspec = pl.BlockSpec((1, tr, ADA_N), lambda l, i: (l, i, 0))
    osd = jax.ShapeDtypeStruct((DEPTH, D, ADA_N), f32)
    return pl.pallas_call(
        body, name="ada_bwd_adam", grid=(DEPTH, D // tr),
        in_specs=[pl.BlockSpec((8, tr), lambda l, i: (0, i)), pl.BlockSpec((1, 8, ADA_N), lambda l, i: (l, 0, 0)),
                  wspec, wspec, wspec],
        out_specs=[wspec] * 4, out_shape=[osd] * 4, compiler_params=_cp(("parallel", "parallel")),
    )(c_all, dm, w, m, v)


def _lb_probs(x):
    mx = jnp.max(x, axis=0, keepdims=True)
    e = jnp.exp(x - mx)
    return e / jnp.sum(e, axis=0, keepdims=True)


def lb_fwd(logits):
    def body(x_ref, o_ref):
        p = _lb_probs(x_ref[...])
        rows = [jnp.zeros((1, 768), f32)]
        for l in range(1, DEPTH):
            rows.append(rows[-1] + p[l:l + 1, :])
        o_ref[...] = jnp.concatenate(rows, axis=0)

    return pl.pallas_call(body, name="lb_fwd", out_shape=jax.ShapeDtypeStruct((DEPTH, 768), f32))(logits)


def lb_bwd(logits, dlb):
    def body(x_ref, d_ref, o_ref):
        p = _lb_probs(x_ref[...])
        d = d_ref[...]
        rows = [jnp.zeros((1, 768), f32)] * DEPTH
        acc = jnp.zeros((1, 768), f32)
        for l in range(DEPTH - 1, 0, -1):
            acc = acc + d[l:l + 1, :]
            rows[l] = acc
        dp = jnp.concatenate(rows, axis=0)
        o_ref[...] = p * (dp - jnp.sum(p * dp, axis=0, keepdims=True))

    return pl.pallas_call(body, name="lb_bwd", out_shape=jax.ShapeDtypeStruct((DEPTH, 768), f32))(logits, dlb)


def sum_slots(x):
    n, rows, cols = x.shape
    tr = _rows_tile(rows)

    def body(x_ref, o_ref):
        acc = x_ref[0]
        for j in range(1, n):
            acc = acc + x_ref[j]
        o_ref[...] = acc

    return pl.pallas_call(
        body, name="sum_slots", grid=(rows // tr,),
        in_specs=[pl.BlockSpec((n, tr, cols), lambda i: (0, i, 0))], out_specs=pl.BlockSpec((tr, cols), lambda i: (i, 0)),
        out_shape=jax.ShapeDtypeStruct((rows, cols), f32), compiler_params=_cp(("parallel",)),
    )(x)


def add2(a, b):
    shape = a.shape
    r2 = lambda t: t.reshape(-1, shape[-1])
    rows, cols = r2(a).shape
    tr = _rows_tile(rows)

    def body(a_ref, b_ref, o_ref):
        o_ref[...] = a_ref[...] + b_ref[...]

    spec = pl.BlockSpec((tr, cols), lambda i: (i, 0))
    return pl.pallas_call(
        body, name="add2", grid=(rows // tr,), in_specs=[spec, spec], out_specs=spec,
        out_shape=jax.ShapeDtypeStruct((rows, cols), f32), compiler_params=_cp(("parallel",)),
    )(r2(a), r2(b)).reshape(shape)


ANY = pl.BlockSpec(memory_space=pl.ANY)
CHIP_FLIPS = ((1, 0), (0, 1), (1, 1))
DEV_FLIPS = tuple((a, b, d) for a in (0, 1) for b in (0, 1) for d in (0, 1))[1:]


def _me():
    return lax.axis_index("x"), lax.axis_index("y"), lax.axis_index("c")


def _flip(v, f):
    return 1 - v if f else v


def _comm_call(body, name, ins, out_shapes, n_remote, n_local):
    return pl.pallas_call(
        body, name=name, in_specs=[ANY] * len(ins), out_specs=[ANY] * len(out_shapes), out_shape=out_shapes,
        scratch_shapes=[pltpu.SemaphoreType.DMA((n_remote,)), pltpu.SemaphoreType.DMA((n_remote,)),
                        pltpu.SemaphoreType.DMA((max(n_local, 1),))],
    )(*ins)


def all_gather_chips(arrs, layer=None, name="ag4"):
    n = len(arrs)
    shapes = [a.shape if layer is None else a.shape[1:] for a in arrs]

    def body(*refs):
        ins, outs, (send, recv, loc) = refs[:n], refs[n:2 * n], refs[2 * n:]
        x, y, c = _me()
        mine = 2 * x + y
        srcs = [r if layer is None else r.at[layer] for r in ins]
        locs = [pltpu.make_async_copy(srcs[a], outs[a].at[mine], loc.at[a]) for a in range(n)]
        for cp in locs:
            cp.start()

        def remote(a, k, slot):
            fx, fy = CHIP_FLIPS[k]
            return pltpu.make_async_remote_copy(srcs[a], outs[a].at[slot], send.at[3 * a + k], recv.at[3 * a + k],
                                                device_id=(_flip(x, fx), _flip(y, fy), c), device_id_type=MESH)

        for a in range(n):
            for k in range(3):
                remote(a, k, mine).start()
        for a in range(n):
            for k, (fx, fy) in enumerate(CHIP_FLIPS):
                cp = remote(a, k, 2 * _flip(x, fx) + _flip(y, fy))
                cp.wait_recv()
                cp.wait_send()
        for cp in locs:
            cp.wait()

    outs = [jax.ShapeDtypeStruct((4,) + tuple(s), a.dtype) for s, a in zip(shapes, arrs)]
    return _comm_call(body, name, arrs, outs, 3 * n, n)


def all_gather_devs(arr, name="ag8"):
    def body(in_ref, out_ref, send, recv, loc):
        x, y, c = _me()
        mine = 4 * x + 2 * y + c
        lc = pltpu.make_async_copy(in_ref, out_ref.at[mine], loc.at[0])
        lc.start()

        def remote(k, slot):
            fx, fy, fc = DEV_FLIPS[k]
            return pltpu.make_async_remote_copy(in_ref, out_ref.at[slot], send.at[k], recv.at[k],
                                                device_id=(_flip(x, fx), _flip(y, fy), _flip(c, fc)), device_id_type=MESH)

        for k in range(7):
            remote(k, mine).start()
        for k, (fx, fy, fc) in enumerate(DEV_FLIPS):
            cp = remote(k, 4 * _flip(x, fx) + 2 * _flip(y, fy) + _flip(c, fc))
            cp.wait_recv()
            cp.wait_send()
        lc.wait()

    return _comm_call(body, name, [arr], [jax.ShapeDtypeStruct((8,) + arr.shape, arr.dtype)], 7, 1)[0]


def _half(r, which):
    rows = r.shape[-2] // 2
    start = pl.multiple_of(which * rows, 8)
    return r.at[:, pl.ds(start, rows), :] if len(r.shape) == 3 else r.at[pl.ds(start, rows), :]


def pair_exchange(parts, name="rs_pair"):
    n = len(parts)

    def body(*refs):
        ins, own, got, (send, recv, loc) = refs[:n], refs[n:2 * n], refs[2 * n:3 * n], refs[3 * n:]
        x, y, c = _me()
        locs = [pltpu.make_async_copy(_half(ins[a], c), own[a], loc.at[a]) for a in range(n)]
        rems = [pltpu.make_async_remote_copy(_half(ins[a], 1 - c), got[a], send.at[a], recv.at[a],
                                             device_id=(x, y, 1 - c), device_id_type=MESH) for a in range(n)]
        for cp in locs + rems:
            cp.start()
        for cp in rems:
            cp.wait_recv()
            cp.wait_send()
        for cp in locs:
            cp.wait()

    hs = [jax.ShapeDtypeStruct((4, p.shape[1] // 2, p.shape[2]), p.dtype) for p in parts]
    res = _comm_call(body, name, parts, hs + hs, n, n)
    return res[:n], res[n:]


def chip_exchange(parts, name="rs_a2a"):
    n = len(parts)

    def body(*refs):
        ins, outs, (send, recv, loc) = refs[:n], refs[n:2 * n], refs[2 * n:]
        x, y, c = _me()
        mine = 2 * x + y
        locs = [pltpu.make_async_copy(ins[a].at[mine], outs[a].at[mine], loc.at[a]) for a in range(n)]
        for cp in locs:
            cp.start()

        def remote(a, k, src_slot, dst_slot):
            fx, fy = CHIP_FLIPS[k]
            return pltpu.make_async_remote_copy(ins[a].at[src_slot], outs[a].at[dst_slot], send.at[3 * a + k], recv.at[3 * a + k],
                                                device_id=(_flip(x, fx), _flip(y, fy), c), device_id_type=MESH)

        peers = [2 * _flip(x, fx) + _flip(y, fy) for fx, fy in CHIP_FLIPS]
        for a in range(n):
            for k in range(3):
                remote(a, k, peers[k], mine).start()
        for a in range(n):
            for k in range(3):
                cp = remote(a, k, peers[k], peers[k])
                cp.wait_recv()
                cp.wait_send()
        for cp in locs:
            cp.wait()

    return _comm_call(body, name, parts, [jax.ShapeDtypeStruct(p.shape, p.dtype) for p in parts], 3 * n, n)


def pair_share(halves, name="rs_share"):
    n = len(halves)

    def body(*refs):
        ins, outs, (send, recv, loc) = refs[:n], refs[n:2 * n], refs[2 * n:]
        x, y, c = _me()
        locs = [pltpu.make_async_copy(ins[a], _half(outs[a], c), loc.at[a]) for a in range(n)]
        sends = [pltpu.make_async_remote_copy(ins[a], _half(outs[a], c), send.at[a], recv.at[a],
                                              device_id=(x, y, 1 - c), device_id_type=MESH) for a in range(n)]
        for cp in locs + sends:
            cp.start()
        for a in range(n):
            cp = pltpu.make_async_remote_copy(ins[a], _half(outs[a], 1 - c), send.at[a], recv.at[a],
                                              device_id=(x, y, 1 - c), device_id_type=MESH)
            cp.wait_recv()
            cp.wait_send()
        for cp in locs:
            cp.wait()

    return _comm_call(body, name, halves, [jax.ShapeDtypeStruct((2 * h.shape[0], h.shape[1]), h.dtype) for h in halves], n, n)


BIG = ("ffn1_w_in", "ffn1_w_out", "w_in", "w_branch_a", "w_branch_b", "w_branch_c", "w_out", "ffn2_w_in", "ffn2_w_out")
ROW_SHARDED = ("ffn1_w_out", "w_out", "ffn2_w_out")
RES_W = (0.5, 1.0, 0.5)


def _full_weight(name, g):
    if name in ROW_SHARDED:
        return g.reshape(4 * g.shape[1], g.shape[2])
    return jnp.concatenate([g[0], g[1], g[2], g[3]], axis=1)


def _by_shard(name, dw):
    if name in ROW_SHARDED:
        return dw.reshape(4, dw.shape[0] // 4, dw.shape[1])
    return dw.reshape(dw.shape[0], 4, dw.shape[1] // 4).transpose(1, 0, 2)


def _ffn_fwd(x, w_in, w_out, a_vec, sh_vec, b_vec):
    h = prenorm(x, a_vec, sh_vec)
    u = mm(h, w_in, name="ffn_in")
    s = swiglu(u)
    y = mm(s, w_out, name="ffn_out")
    return postnorm(x, y, b_vec), (x, h, u, s, y)


def _ffn_bwd(dout, saved, w_in, w_out, a_vec, b_vec):
    x, h, u, s, y = saved
    dy, db = post_bwd(dout, y, b_vec)
    ds = mm(dy, w_out, tb=True, name="ffn_ds")
    dw_out = mm(s, dy, ta=True, name="ffn_dwo")
    du = swiglu_bwd(ds, u)
    dh = mm(du, w_in, tb=True, name="ffn_dh")
    dw_in = mm(h, du, ta=True, name="ffn_dwi")
    dx, dsh, da = pre_bwd(dout, dh, x, a_vec)
    return dx, dw_in, dw_out, dsh, da, db


def _mix_fwd(x, w, lb, ng, a_vec, sh_vec, b_vec):
    h = prenorm(x, a_vec, sh_vec)
    u = mm(h, w["w_in"], name="mix_in")
    o, ya, st = hgrn_fwd(u, lb, ng)
    yb = sb_fwd(u)
    groups = [dil_fwd(u, g) for g in range(3)]
    yc, lse = dil_merge([o_ for o_, _ in groups], [l_ for _, l_ in groups])
    pa = mm(ya, w["w_branch_a"], name="mix_pa")
    pb = mm(yb, w["w_branch_b"], name="mix_pb")
    pc = mm(yc, w["w_branch_c"], name="mix_pc")
    merged = gate_merge(u, pa, pb, pc)
    z = mm(merged, w["w_out"], name="mix_out")
    return postnorm(x, z, b_vec), (x, h, u, o, ya, st, yb, yc, lse, pa, pb, pc, merged, z)


def _mix_bwd(dout, saved, w, lb, ng, a_vec, b_vec):
    x, h, u, o, ya, st, yb, yc, lse, pa, pb, pc, merged, z = saved
    dz, db = post_bwd(dout, z, b_vec)
    dmerged = mm(dz, w["w_out"], tb=True, name="mix_dm")
    dw_out = mm(merged, dz, ta=True, name="mix_dwo")
    dpa, dpb, dpc, dg0, dg1, dg2 = gate_bwd(dmerged, u, pa, pb, pc)
    dya = mm(dpa, w["w_branch_a"], tb=True, name="mix_dya")
    dyb = mm(dpb, w["w_branch_b"], tb=True, name="mix_dyb")
    dyc = mm(dpc, w["w_branch_c"], tb=True, name="mix_dyc")
    dw_a = mm(ya, dpa, ta=True, name="mix_dwa")
    dw_b = mm(yb, dpb, ta=True, name="mix_dwb")
    dw_c = mm(yc, dpc, ta=True, name="mix_dwc")
    daq, daf, dai, dag, dlb, dng = hgrn_bwd(u, lb, ng, o, st, dya)
    dbq, dbk, dbv = sb_bwd(u, yb, dyb)
    dc = [dil_bwd(u, g, dyc, yc, lse) for g in range(3)]
    du = jnp.concatenate(
        [daq, daf, dai, dag] + [t.astype(bf16) for t in (dbq, dbk, dbv)]
        + [dc[g][j].astype(bf16) for j in range(3) for g in range(3)] + [dg0, dg1, dg2], axis=1)
    dh = mm(du, w["w_in"], tb=True, name="mix_dh")
    dw_in = mm(h, du, ta=True, name="mix_dwi")
    dx, dsh, da = pre_bwd(dout, dh, x, a_vec)
    grads = {"w_in": dw_in, "w_out": dw_out, "w_branch_a": dw_a, "w_branch_b": dw_b, "w_branch_c": dw_c}
    return dx, grads, dlb, jnp.sum(dng, axis=0), dsh, da, db


def _reduce_to_shards(names, grads):
    parts = [_by_shard(n, grads[n]) for n in names]
    own, got = pair_exchange(parts)
    pair = [add2(a, b) for a, b in zip(own, got)]
    slots = chip_exchange(pair)
    halves = [sum_slots(s) for s in slots]
    return dict(zip(names, pair_share(halves)))


def kernel(x, c, w_ada, b_ada, norm_g, ffn1_w_in, ffn1_w_out, w_in, hgrn_lb_logits, hgrn_norm_g, w_branch_a, w_branch_b, w_branch_c, w_out, ffn2_w_in, ffn2_w_out, loss_target, m_w_ada, m_b_ada, m_norm_g, m_ffn1_w_in, m_ffn1_w_out, m_w_in, m_hgrn_lb_logits, m_hgrn_norm_g, m_w_branch_a, m_w_branch_b, m_w_branch_c, m_w_out, m_ffn2_w_in, m_ffn2_w_out, v_w_ada, v_b_ada, v_norm_g, v_ffn1_w_in, v_ffn1_w_out, v_w_in, v_hgrn_lb_logits, v_hgrn_norm_g, v_w_branch_a, v_w_branch_b, v_w_branch_c, v_w_out, v_ffn2_w_in, v_ffn2_w_out):
    weights = dict(w_ada=w_ada, b_ada=b_ada, norm_g=norm_g, ffn1_w_in=ffn1_w_in, ffn1_w_out=ffn1_w_out, w_in=w_in,
                   hgrn_lb_logits=hgrn_lb_logits, hgrn_norm_g=hgrn_norm_g, w_branch_a=w_branch_a, w_branch_b=w_branch_b,
                   w_branch_c=w_branch_c, w_out=w_out, ffn2_w_in=ffn2_w_in, ffn2_w_out=ffn2_w_out)
    mom = dict(w_ada=m_w_ada, b_ada=m_b_ada, norm_g=m_norm_g, ffn1_w_in=m_ffn1_w_in, ffn1_w_out=m_ffn1_w_out, w_in=m_w_in,
               hgrn_lb_logits=m_hgrn_lb_logits, hgrn_norm_g=m_hgrn_norm_g, w_branch_a=m_w_branch_a, w_branch_b=m_w_branch_b,
               w_branch_c=m_w_branch_c, w_out=m_w_out, ffn2_w_in=m_ffn2_w_in, ffn2_w_out=m_ffn2_w_out)
    var = dict(w_ada=v_w_ada, b_ada=v_b_ada, norm_g=v_norm_g, ffn1_w_in=v_ffn1_w_in, ffn1_w_out=v_ffn1_w_out, w_in=v_w_in,
               hgrn_lb_logits=v_hgrn_lb_logits, hgrn_norm_g=v_hgrn_norm_g, w_branch_a=v_w_branch_a, w_branch_b=v_w_branch_b,
               w_branch_c=v_w_branch_c, w_out=v_w_out, ffn2_w_in=v_ffn2_w_in, ffn2_w_out=v_ffn2_w_out)
    order = list(weights)
    xi, yi, ci = _me()
    chip = 2 * xi + yi
    dev = 4 * xi + 2 * yi + ci
    xs = x[0]

    c_all = all_gather_devs(c, name="ag8_c").reshape(8, D)
    mod_sh = all_gather_chips([ada_fwd(c_all, w_ada)], name="ag4_mod")[0]
    mod_all = mod_sh.transpose(1, 2, 0, 3).reshape(DEPTH, 8, 9 * D)
    mod = lax.dynamic_index_in_dim(mod_all, dev, axis=1, keepdims=False) + b_ada
    mod = mod.reshape(DEPTH, 3, 3, D)
    ng_all = all_gather_chips([norm_g.reshape(DEPTH * 6, D // 4)], name="ag4_norm")[0]
    ng_all = ng_all.reshape(4, DEPTH, 6, D // 4).transpose(1, 2, 0, 3).reshape(DEPTH, 6, D)
    lb_all = lb_fwd(hgrn_lb_logits)
    w16 = {n: cast_bf16(weights[n]) for n in BIG}

    def vecs(l, i):
        shift, scale, gate = mod[l, i, 0][None], mod[l, i, 1][None], mod[l, i, 2][None]
        g_pre, g_post = ng_all[l, 2 * i][None], ng_all[l, 2 * i + 1][None]
        return g_pre * (1.0 + scale), shift, RES_W[i] * gate * g_post

    saved, full = [], []
    for l in range(DEPTH):
        gathered = all_gather_chips([w16[n] for n in BIG], layer=l, name=f"ag4_w{l}")
        w = {n: _full_weight(n, g) for n, g in zip(BIG, gathered)}
        full.append(w)
        lb, ng = lb_all[l][None], hgrn_norm_g[l][None]
        xs, s1 = _ffn_fwd(xs, w["ffn1_w_in"], w["ffn1_w_out"], *vecs(l, 0))
        xs, s2 = _mix_fwd(xs, w, lb, ng, *vecs(l, 1))
        xs, s3 = _ffn_fwd(xs, w["ffn2_w_in"], w["ffn2_w_out"], *vecs(l, 2))
        saved.append((s1, s2, s3))

    dx, loss_part = loss_grad(xs, loss_target[0])
    loss = lax.psum(loss_part[0, 0], ("x", "y", "c"))

    big_grads = {n: [None] * DEPTH for n in BIG}
    d_mod, d_ng, d_lb, d_hng = [None] * DEPTH, [None] * DEPTH, [None] * DEPTH, [None] * DEPTH
    for l in reversed(range(DEPTH)):
        w = full[l]
        s1, s2, s3 = saved[l]
        lb, ng = lb_all[l][None], hgrn_norm_g[l][None]
        rows_mod, rows_ng = [None] * 9, [None] * 6

        def vec_grads(i, dsh, da, db):
            scale, gate = mod[l, i, 1][None], mod[l, i, 2][None]
            g_pre, g_post = ng_all[l, 2 * i][None], ng_all[l, 2 * i + 1][None]
            rows_mod[3 * i], rows_mod[3 * i + 1], rows_mod[3 * i + 2] = dsh, g_pre * da, RES_W[i] * g_post * db
            rows_ng[2 * i], rows_ng[2 * i + 1] = (1.0 + scale) * da, RES_W[i] * gate * db

        a3, _, b3 = vecs(l, 2)
        dx, dwi, dwo, dsh, da, db = _ffn_bwd(dx, s3, w["ffn2_w_in"], w["ffn2_w_out"], a3, b3)
        vec_grads(2, dsh, da, db)
        grads = {"ffn2_w_in": dwi, "ffn2_w_out": dwo}
        a2, _, b2 = vecs(l, 1)
        dx, gmix, dlb, dhng, dsh, da, db = _mix_bwd(dx, s2, w, lb, ng, a2, b2)
        vec_grads(1, dsh, da, db)
        grads.update(gmix)
        a1, _, b1 = vecs(l, 0)
        dx, dwi, dwo, dsh, da, db = _ffn_bwd(dx, s1, w["ffn1_w_in"], w["ffn1_w_out"], a1, b1)
        vec_grads(0, dsh, da, db)
        grads.update({"ffn1_w_in": dwi, "ffn1_w_out": dwo})
        for n, g in _reduce_to_shards(BIG, grads).items():
            big_grads[n][l] = g
        d_mod[l] = jnp.concatenate(rows_mod, axis=1)
        d_ng[l] = jnp.concatenate(rows_ng, axis=0)
        d_lb[l], d_hng[l] = dlb, dhng

    n_small = 6 * D * DEPTH + 768 * DEPTH + A_V * DEPTH + 9 * D * DEPTH
    pad = -n_small % (512 * LANE)
    flat = jnp.concatenate([jnp.stack(d_ng).reshape(-1), jnp.concatenate(d_lb, axis=0).reshape(-1),
                            jnp.concatenate(d_hng, axis=0).reshape(-1), jnp.concatenate(d_mod, axis=0).reshape(-1),
                            jnp.zeros((pad,), f32)])
    small_all = all_gather_devs(flat.reshape(-1, LANE), name="ag8_small")
    total = sum_slots(small_all).reshape(-1)
    o1 = 6 * D * DEPTH
    o2 = o1 + 768 * DEPTH
    o3 = o2 + A_V * DEPTH
    g_ng_full = total[:o1].reshape(DEPTH, 6, D)
    g_lb_all = total[o1:o2].reshape(DEPTH, 768)
    g_small = {
        "norm_g": lax.dynamic_slice_in_dim(g_ng_full, chip * (D // 4), D // 4, axis=2),
        "hgrn_lb_logits": lb_bwd(hgrn_lb_logits, g_lb_all),
        "hgrn_norm_g": total[o2:o3].reshape(DEPTH, A_V),
        "b_ada": total[o3:n_small].reshape(DEPTH, 9 * D),
    }
    dmod_all = small_all.reshape(8, -1)[:, o3:n_small].reshape(8, DEPTH, 9 * D).transpose(1, 0, 2)
    dm_sh = lax.dynamic_slice_in_dim(dmod_all, chip * ADA_N, ADA_N, axis=2)

    out_g, out_d, out_m, out_v = {}, {}, {}, {}
    out_g["w_ada"], out_d["w_ada"], out_m["w_ada"], out_v["w_ada"] = ada_bwd_adam(c_all, dm_sh, w_ada, m_w_ada, v_w_ada)
    for n in BIG:
        out_g[n] = jnp.stack(big_grads[n])
    out_g.update(g_small)
    for n in order:
        if n != "w_ada":
            out_d[n], out_m[n], out_v[n] = adam(weights[n], out_g[n], mom[n], var[n])
    return (loss, dx[None], *[out_g[n] for n in order], *[out_d[n] for n in order],
            *[out_m[n] for n in order], *[out_v[n] for n in order])
```

```python
import functools
import math

import jax
import jax.numpy as jnp
from jax import lax
from jax.experimental import pallas as pl
from jax.experimental.pallas import tpu as pltpu

f32, bf16 = jnp.float32, jnp.bfloat16

D = 1024
DEPTH = 4
D_FF = 2816
EPS = 1e-6
NEG_BIG = -1e30
TINY = 1e-30
A_HEADS, A_K, A_V, A_CHUNK = 6, 128, 64, 64
A_SUB = 16
A_CLAMP = 80.0
B_HEADS, HD = 6, 64
C_GROUPS = ((128, 1), (512, 4), (2048, 16))
C_BLK = 128
IN_COLS = 8832
O_AQ, O_AF, O_AI, O_AG = 0, 768, 1536, 1920
O_BQ, O_BK, O_BV = 2304, 2688, 3072
O_CQ, O_CK, O_CV = 3456, 4224, 4992
O_GATE = 5760
LANE = 128
ADAM_LR, ADAM_B1, ADAM_B2, ADAM_EPS, ADAM_WD, ADAM_STEP = 0.001, 0.9, 0.999, 1e-08, 0.01, 10
MESH = pl.DeviceIdType.MESH
VMEM_LIMIT = 56 * 1024 * 1024


def _alibi_slopes(n):
    def pow2(m):
        start = 2.0 ** (-8.0 / m)
        return [start ** (i + 1) for i in range(m)]
    if math.log2(n).is_integer():
        s = pow2(n)
    else:
        c = 2 ** int(math.floor(math.log2(n)))
        s = pow2(c) + pow2(2 * c)[0::2][: n - c]
    return sorted(s, reverse=True)


C_SLOPES = _alibi_slopes(12)


def _tile(n, prefs):
    for p in prefs:
        if n % p == 0:
            return p
    return n


def _cp(sem):
    return pltpu.CompilerParams(dimension_semantics=sem, vmem_limit_bytes=VMEM_LIMIT)


def _sig(x):
    return 1.0 / (1.0 + jnp.exp(-x))


def _dot(a, b, dn, precision=None):
    return lax.dot_general(a, b, (dn, ((), ())), preferred_element_type=f32, precision=precision)


NN = ((1,), (0,))
NT = ((1,), (1,))
TN = ((0,), (0,))


def mm(a, b, *, ta=False, tb=False, out_dtype=f32, name="mm"):
    if ta:
        K, M = a.shape
    else:
        M, K = a.shape
    if tb:
        N, K2 = b.shape
    else:
        K2, N = b.shape
    assert K == K2, (a.shape, b.shape, ta, tb)
    tm = _tile(M, (1024, 512, 384, 256, 128) if ta else (1024, 704, 512, 384, 256, 128))
    tn = _tile(N, (512, 384, 256, 128))
    tk = _tile(K, (1024, 512, 1408, 384, 256, 128))
    nk = K // tk
    dn = (0 if ta else 1,), (1 if tb else 0,)

    def body(a_ref, b_ref, o_ref, acc_ref):
        k = pl.program_id(2)
        p = _dot(a_ref[...].astype(bf16), b_ref[...].astype(bf16), dn)

        @pl.when(k == 0)
        def _():
            acc_ref[...] = p

        @pl.when(k > 0)
        def _():
            acc_ref[...] += p

        @pl.when(k == nk - 1)
        def _():
            o_ref[...] = acc_ref[...].astype(out_dtype)

    a_spec = pl.BlockSpec((tk, tm), lambda i, j, k: (k, i)) if ta else pl.BlockSpec((tm, tk), lambda i, j, k: (i, k))
    b_spec = pl.BlockSpec((tn, tk), lambda i, j, k: (j, k)) if tb else pl.BlockSpec((tk, tn), lambda i, j, k: (k, j))
    return pl.pallas_call(
        body, name=name, grid=(M // tm, N // tn, nk),
        in_specs=[a_spec, b_spec],
        out_specs=pl.BlockSpec((tm, tn), lambda i, j, k: (i, j)),
        out_shape=jax.ShapeDtypeStruct((M, N), out_dtype),
        scratch_shapes=[pltpu.VMEM((tm, tn), f32)],
        compiler_params=_cp(("parallel", "parallel", "arbitrary")),
    )(a, b)


TR = 256


def _row_spec(cols=D):
    return pl.BlockSpec((TR, cols), lambda i: (i, 0))


def _vec_spec(cols=D):
    return pl.BlockSpec((1, cols), lambda i: (0, 0))


def prenorm(x, a_vec, sh_vec):
    S = x.shape[0]

    def body(x_ref, a_ref, s_ref, h_ref):
        xv = x_ref[...]
        rstd = lax.rsqrt(jnp.mean(xv * xv, axis=1, keepdims=True) + EPS)
        h_ref[...] = (xv * rstd * a_ref[...] + s_ref[...]).astype(bf16)

    return pl.pallas_call(
        body, name="prenorm", grid=(S // TR,),
        in_specs=[_row_spec(), _vec_spec(), _vec_spec()], out_specs=_row_spec(),
        out_shape=jax.ShapeDtypeStruct((S, D), bf16), compiler_params=_cp(("parallel",)),
    )(x, a_vec, sh_vec)


def postnorm(x, y, b_vec):
    S = x.shape[0]

    def body(x_ref, y_ref, b_ref, o_ref):
        yv = y_ref[...]
        rstd = lax.rsqrt(jnp.mean(yv * yv, axis=1, keepdims=True) + EPS)
        o_ref[...] = x_ref[...] + b_ref[...] * (yv * rstd)

    return pl.pallas_call(
        body, name="postnorm", grid=(S // TR,),
        in_specs=[_row_spec(), _row_spec(), _vec_spec()], out_specs=_row_spec(),
        out_shape=jax.ShapeDtypeStruct((S, D), f32), compiler_params=_cp(("parallel",)),
    )(x, y, b_vec)


def post_bwd(dout, y, b_vec):
    S = dout.shape[0]

    def body(d_ref, y_ref, b_ref, dy_ref, db_ref):
        i = pl.program_id(0)
        yv, dv = y_ref[...], d_ref[...]
        rstd = lax.rsqrt(jnp.mean(yv * yv, axis=1, keepdims=True) + EPS)
        yh = yv * rstd
        dyh = dv * b_ref[...]
        dy_ref[...] = (rstd * (dyh - yh * jnp.mean(dyh * yh, axis=1, keepdims=True))).astype(bf16)
        part = jnp.sum(dv * yh, axis=0, keepdims=True)

        @pl.when(i == 0)
        def _():
            db_ref[...] = part

        @pl.when(i > 0)
        def _():
            db_ref[...] += part

    return pl.pallas_call(
        body, name="post_bwd", grid=(S // TR,),
        in_specs=[_row_spec(), _row_spec(), _vec_spec()], out_specs=[_row_spec(), _vec_spec()],
        out_shape=[jax.ShapeDtypeStruct((S, D), bf16), jax.ShapeDtypeStruct((1, D), f32)],
        compiler_params=_cp(("arbitrary",)),
    )(dout, y, b_vec)


def pre_bwd(dout, dh, x, a_vec):
    S = dout.shape[0]

    def body(d_ref, dh_ref, x_ref, a_ref, dx_ref, ds_ref, da_ref):
        i = pl.program_id(0)
        xv, dhv = x_ref[...], dh_ref[...]
        rstd = lax.rsqrt(jnp.mean(xv * xv, axis=1, keepdims=True) + EPS)
        n1 = xv * rstd
        dn = dhv * a_ref[...]
        dx_ref[...] = d_ref[...] + rstd * (dn - n1 * jnp.mean(dn * n1, axis=1, keepdims=True))
        p_s = jnp.sum(dhv, axis=0, keepdims=True)
        p_a = jnp.sum(dhv * n1, axis=0, keepdims=True)

        @pl.when(i == 0)
        def _():
            ds_ref[...] = p_s
            da_ref[...] = p_a

        @pl.when(i > 0)
        def _():
            ds_ref[...] += p_s
            da_ref[...] += p_a

    return pl.pallas_call(
        body, name="pre_bwd", grid=(S // TR,),
        in_specs=[_row_spec(), _row_spec(), _row_spec(), _vec_spec()],
        out_specs=[_row_spec(), _vec_spec(), _vec_spec()],
        out_shape=[jax.ShapeDtypeStruct((S, D), f32), jax.ShapeDtypeStruct((1, D), f32), jax.ShapeDtypeStruct((1, D), f32)],
        compiler_params=_cp(("arbitrary",)),
    )(dout, dh, x, a_vec)


def loss_grad(y, tgt):
    S = y.shape[0]

    def body(y_ref, t_ref, dy_ref, l_ref):
        i = pl.program_id(0)
        e = y_ref[...] - t_ref[...]
        dy_ref[...] = e * (1.0 / D)
        part = jnp.sum(jnp.sum(e * e, axis=1, keepdims=True), axis=0, keepdims=True) * (0.5 / D)
        part = jnp.broadcast_to(part, (8, LANE))

        @pl.when(i == 0)
        def _():
            l_ref[...] = part

        @pl.when(i > 0)
        def _():
            l_ref[...] += part

    return pl.pallas_call(
        body, name="loss_grad", grid=(S // TR,),
        in_specs=[_row_spec(), _row_spec()],
        out_specs=[_row_spec(), pl.BlockSpec((8, LANE), lambda i: (0, 0))],
        out_shape=[jax.ShapeDtypeStruct((S, D), f32), jax.ShapeDtypeStruct((8, LANE), f32)],
        compiler_params=_cp(("arbitrary",)),
    )(y, tgt)


FF_T = 256
FF_NB = D_FF // FF_T


def swiglu(u):
    S = u.shape[0]

    def body(a_ref, b_ref, s_ref):
        a = a_ref[...]
        s_ref[...] = (a * _sig(a) * b_ref[...]).astype(bf16)

    return pl.pallas_call(
        body, name="swiglu", grid=(S // 512, FF_NB),
        in_specs=[pl.BlockSpec((512, FF_T), lambda i, j: (i, j)), pl.BlockSpec((512, FF_T), lambda i, j: (i, j + FF_NB))],
        out_specs=pl.BlockSpec((512, FF_T), lambda i, j: (i, j)),
        out_shape=jax.ShapeDtypeStruct((S, D_FF), bf16), compiler_params=_cp(("parallel", "parallel")),
    )(u, u)


def swiglu_bwd(ds, u):
    S = u.shape[0]

    def body(ds_ref, a_ref, b_ref, du_ref):
        j = pl.program_id(1)
        a, b, dsv = a_ref[...], b_ref[...], ds_ref[...]
        sg = _sig(a)
        da = dsv * b * sg * (1.0 + a * (1.0 - sg))
        db = dsv * a * sg
        du_ref[...] = jnp.where(j < FF_NB, da, db).astype(bf16)

    return pl.pallas_call(
        body, name="swiglu_bwd", grid=(S // 512, 2 * FF_NB),
        in_specs=[pl.BlockSpec((512, FF_T), lambda i, j: (i, j % FF_NB)),
                  pl.BlockSpec((512, FF_T), lambda i, j: (i, j % FF_NB)),
                  pl.BlockSpec((512, FF_T), lambda i, j: (i, j % FF_NB + FF_NB))],
        out_specs=pl.BlockSpec((512, FF_T), lambda i, j: (i, j)),
        out_shape=jax.ShapeDtypeStruct((S, 2 * D_FF), bf16), compiler_params=_cp(("parallel", "parallel")),
    )(ds, u, u)


G_NB = D // LANE
G_OFF = O_GATE // LANE


def gate_merge(u, pa, pb, pc):
    S = u.shape[0]

    def body(g0, g1, g2, a, b, c, o_ref):
        o_ref[...] = (_sig(g0[...]) * a[...] + _sig(g1[...]) * b[...] + _sig(g2[...]) * c[...]).astype(bf16)

    gs = [pl.BlockSpec((512, LANE), functools.partial(lambda i, j, k: (i, G_OFF + G_NB * k + j), k=k)) for k in range(3)]
    ps = pl.BlockSpec((512, LANE), lambda i, j: (i, j))
    return pl.pallas_call(
        body, name="gate_merge", grid=(S // 512, G_NB),
        in_specs=gs + [ps, ps, ps], out_specs=ps,
        out_shape=jax.ShapeDtypeStruct((S, D), bf16), compiler_params=_cp(("parallel", "parallel")),
    )(u, u, u, pa, pb, pc)


def gate_bwd(dm, u, pa, pb, pc):
    S = u.shape[0]

    def body(dm_ref, g0, g1, g2, a, b, c, da, db, dc, dg0, dg1, dg2):
        d = dm_ref[...]
        for g, p, dp, dg in ((g0, a, da, dg0), (g1, b, db, dg1), (g2, c, dc, dg2)):
            s = _sig(g[...])
            dp[...] = (d * s).astype(bf16)
            dg[...] = (d * p[...] * s * (1.0 - s)).astype(bf16)

    gs = [pl.BlockSpec((512, LANE), functools.partial(lambda i, j, k: (i, G_OFF + G_NB * k + j), k=k)) for k in range(3)]
    ps = pl.BlockSpec((512, LANE), lambda i, j: (i, j))
    osd = jax.ShapeDtypeStruct((S, D), bf16)
    return pl.pallas_call(
        body, name="gate_bwd", grid=(S // 512, G_NB),
        in_specs=[ps] + gs + [ps, ps, ps], out_specs=[ps] * 6, out_shape=[osd] * 6,
        compiler_params=_cp(("parallel", "parallel")),
    )(dm, u, u, u, pa, pb, pc)


A_TB = 512
A_NCH = A_TB // A_CHUNK
A_NSUB = A_CHUNK // A_SUB


def _hgrn_gates(qr, fr, lbh):
    sq = _sig(qr)
    sig = _sig(fr)
    f = lbh + (1.0 - lbh) * sig
    logf = jnp.log(jnp.maximum(f, TINY))
    k = (1.0 - lbh) * (1.0 - sig)
    return qr * sq, sq, sig, f, logf, k


def _hgrn_intra(qf, k, b, causal):
    qts, kts, eqs, eks, blocks = [], [], [], [], []
    for sb in range(A_NSUB):
        rs = sb * A_SUB
        r = b[rs - 1:rs, :] if sb else jnp.zeros((1, A_K), f32)
        eq = jnp.exp(b[rs:rs + A_SUB, :] - r)
        ek = jnp.exp(jnp.minimum(r - b, A_CLAMP))
        qt = (qf[rs:rs + A_SUB, :] * eq).astype(bf16)
        kt = (k * ek).astype(bf16)
        blocks.append(_dot(qt, kt, NT))
        qts.append(qt), kts.append(kt), eqs.append(eq), eks.append(ek)
    a = jnp.where(causal, jnp.concatenate(blocks, axis=0), 0.0)
    return a, qts, kts, eqs, eks


def _tri():
    r = lax.broadcasted_iota(jnp.int32, (A_CHUNK, A_CHUNK), 0)
    c = lax.broadcasted_iota(jnp.int32, (A_CHUNK, A_CHUNK), 1)
    return r >= c


def _hgrn_in_specs(rev_nb=None):
    def im(col):
        if rev_nb is None:
            return lambda p, i: (i, col + p)
        return lambda p, i: (rev_nb - 1 - i, col + p)
    return [pl.BlockSpec((A_TB, 256), im(O_AQ // 256)), pl.BlockSpec((A_TB, 256), im(O_AF // 256)),
            pl.BlockSpec((A_TB, LANE), im(O_AI // LANE)), pl.BlockSpec((A_TB, LANE), im(O_AG // LANE)),
            pl.BlockSpec((1, 256), lambda p, i: (0, p)), pl.BlockSpec((1, A_V), lambda p, i: (0, 0))]


def hgrn_fwd(u, lb, ng):
    S = u.shape[0]
    nb = S // A_TB

    def body(q_ref, f_ref, i_ref, g_ref, lb_ref, ng_ref, o_ref, ya_ref, st_ref, state):
        @pl.when(pl.program_id(1) == 0)
        def _():
            state[...] = jnp.zeros_like(state)

        causal = _tri()
        tri = causal.astype(f32)

        def chunk(n, carry):
            rows = pl.ds(pl.multiple_of(n * A_CHUNK, A_CHUNK), A_CHUNK)
            o_parts, y_parts = [], []
            for hh in range(2):
                ks = slice(hh * A_K, (hh + 1) * A_K)
                vs = slice(hh * A_V, (hh + 1) * A_V)
                qf, _, _, _, logf, k = _hgrn_gates(q_ref[rows, ks], f_ref[rows, ks], lb_ref[:, ks])
                vi = i_ref[rows, vs].astype(bf16)
                gg = g_ref[rows, vs]
                b = _dot(tri, logf, NN, precision=lax.Precision.HIGHEST)
                s0 = state[hh]
                st_ref[n, hh] = s0
                o = _dot((qf * jnp.exp(b)).astype(bf16), s0.astype(bf16), NT)
                a, _, _, _, _ = _hgrn_intra(qf, k, b, causal)
                o = o + _dot(a.astype(bf16), vi, NN)
                bend = b[A_CHUNK - 1:A_CHUNK, :]
                ke = (k * jnp.exp(bend - b)).astype(bf16)
                state[hh] = s0 * jnp.exp(bend) + _dot(vi, ke, TN)
                rstd = lax.rsqrt(jnp.mean(o * o, axis=1, keepdims=True) + EPS)
                o_parts.append(o)
                y_parts.append(o * rstd * ng_ref[...] * (gg * _sig(gg)))
            o_ref[rows, :] = jnp.concatenate(o_parts, axis=1)
            ya_ref[rows, :] = jnp.concatenate(y_parts, axis=1).astype(bf16)
            return carry

        lax.fori_loop(0, A_NCH, chunk, 0)

    return pl.pallas_call(
        body, name="hgrn_fwd", grid=(3, nb),
        in_specs=_hgrn_in_specs(),
        out_specs=[pl.BlockSpec((A_TB, LANE), lambda p, i: (i, p)), pl.BlockSpec((A_TB, LANE), lambda p, i: (i, p)),
                   pl.BlockSpec((A_NCH, 2, A_V, A_K), lambda p, i: (i, p, 0, 0))],
        out_shape=[jax.ShapeDtypeStruct((S, 384), f32), jax.ShapeDtypeStruct((S, 384), bf16),
                   jax.ShapeDtypeStruct((S // A_CHUNK, A_HEADS, A_V, A_K), f32)],
        scratch_shapes=[pltpu.VMEM((2, A_V, A_K), f32)],
        compiler_params=_cp(("parallel", "arbitrary")),
    )(u, u, u, u, lb, ng)


def hgrn_bwd(u, lb, ng, o, st, dya):
    S = u.shape[0]
    nb = S // A_TB

    def body(q_ref, f_ref, i_ref, g_ref, lb_ref, ng_ref, o_ref, st_ref, dy_ref,
             dq_ref, df_ref, di_ref, dg_ref, dlb_ref, dng_ref, dstate):
        @pl.when(pl.program_id(1) == 0)
        def _():
            dstate[...] = jnp.zeros_like(dstate)
            dlb_ref[...] = jnp.zeros_like(dlb_ref)
            dng_ref[...] = jnp.zeros_like(dng_ref)

        causal = _tri()
        tri = causal.astype(f32)

        def chunk(it, carry):
            n = A_NCH - 1 - it
            rows = pl.ds(pl.multiple_of(n * A_CHUNK, A_CHUNK), A_CHUNK)
            dq_p, df_p, di_p, dg_p, dlb_p = [], [], [], [], []
            dng_acc = jnp.zeros((1, A_V), f32)
            for hh in range(2):
                ks = slice(hh * A_K, (hh + 1) * A_K)
                vs = slice(hh * A_V, (hh + 1) * A_V)
                lbh = lb_ref[:, ks]
                qr = q_ref[rows, ks]
                qf, sq, sig, f, logf, k = _hgrn_gates(qr, f_ref[rows, ks], lbh)
                vi = i_ref[rows, vs].astype(bf16)
                gg = g_ref[rows, vs]
                b = _dot(tri, logf, NN, precision=lax.Precision.HIGHEST)
                eb = jnp.exp(b)
                bend = b[A_CHUNK - 1:A_CHUNK, :]
                eend = jnp.exp(bend)
                ekend = jnp.exp(bend - b)
                qe = (qf * eb).astype(bf16)
                ke = (k * ekend).astype(bf16)
                s0 = st_ref[n, hh]
                dsend = dstate[hh]
                ov = o_ref[rows, vs]
                dy = dy_ref[rows, vs]
                rstd = lax.rsqrt(jnp.mean(ov * ov, axis=1, keepdims=True) + EPS)
                oh = ov * rstd
                sg = _sig(gg)
                d_on = dy * (gg * sg)
                dg_p.append(dy * oh * ng_ref[...] * (sg * (1.0 + gg * (1.0 - sg))))
                dng_acc = dng_acc + jnp.sum(d_on * oh, axis=0, keepdims=True)
                doh = d_on * ng_ref[...]
                do = (rstd * (doh - oh * jnp.mean(doh * oh, axis=1, keepdims=True))).astype(bf16)
                a, qts, kts, eqs, eks = _hgrn_intra(qf, k, b, causal)
                da = jnp.where(causal, _dot(do, vi, NT), 0.0).astype(bf16)
                dsb = dsend.astype(bf16)
                dv = _dot(a.astype(bf16), do, TN) + _dot(ke, dsb, NT)
                dq = _dot(do, s0.astype(bf16), NN) * eb
                dk_state = _dot(vi, dsb, NN) * ekend
                dk = dk_state
                dq_i = []
                for sb in range(A_NSUB):
                    da_sb = da[sb * A_SUB:(sb + 1) * A_SUB, :]
                    dq_i.append(_dot(da_sb, kts[sb], NN) * eqs[sb])
                    dk = dk + _dot(da_sb, qts[sb], TN) * eks[sb]
                dq = dq + jnp.concatenate(dq_i, axis=0)
                db = qf * dq - k * dk
                extra = jnp.sum(k * dk_state, axis=0, keepdims=True) + eend * jnp.sum(s0 * dsend, axis=0, keepdims=True)
                dlogf = _dot(tri, db, TN, precision=lax.Precision.HIGHEST) + extra
                dstate[hh] = _dot(do, qe, TN) + eend * dsend
                d_pre = jnp.where(f > TINY, dlogf / f, 0.0) - dk
                dlb_p.append(jnp.sum((1.0 - sig) * d_pre, axis=0, keepdims=True))
                df_p.append((1.0 - lbh) * d_pre * sig * (1.0 - sig))
                dq_p.append(dq * (sq * (1.0 + qr * (1.0 - sq))))
                di_p.append(dv)
            dq_ref[rows, :] = jnp.concatenate(dq_p, axis=1).astype(bf16)
            df_ref[rows, :] = jnp.concatenate(df_p, axis=1).astype(bf16)
            di_ref[rows, :] = jnp.concatenate(di_p, axis=1).astype(bf16)
            dg_ref[rows, :] = jnp.concatenate(dg_p, axis=1).astype(bf16)
            dlb_ref[...] += jnp.concatenate(dlb_p, axis=1)
            dng_ref[0] += dng_acc
            return carry

        lax.fori_loop(0, A_NCH, chunk, 0)

    rev = lambda p, i: (nb - 1 - i, p)
    return pl.pallas_call(
        body, name="hgrn_bwd", grid=(3, nb),
        in_specs=_hgrn_in_specs(nb) + [pl.BlockSpec((A_TB, LANE), rev),
                                       pl.BlockSpec((A_NCH, 2, A_V, A_K), lambda p, i: (nb - 1 - i, p, 0, 0)),
                                       pl.BlockSpec((A_TB, LANE), rev)],
        out_specs=[pl.BlockSpec((A_TB, 256), rev), pl.BlockSpec((A_TB, 256), rev),
                   pl.BlockSpec((A_TB, LANE), rev), pl.BlockSpec((A_TB, LANE), rev),
                   pl.BlockSpec((1, 256), lambda p, i: (0, p)), pl.BlockSpec((1, 1, A_V), lambda p, i: (p, 0, 0))],
        out_shape=[jax.ShapeDtypeStruct((S, 768), bf16), jax.ShapeDtypeStruct((S, 768), bf16),
                   jax.ShapeDtypeStruct((S, 384), bf16), jax.ShapeDtypeStruct((S, 384), bf16),
                   jax.ShapeDtypeStruct((1, 768), f32), jax.ShapeDtypeStruct((3, 1, A_V), f32)],
        scratch_shapes=[pltpu.VMEM((2, A_V, A_K), f32)],
        compiler_params=_cp(("parallel", "arbitrary")),
    )(u, u, u, u, lb, ng, o, st, dya)


B_TK = 128
SCALE = HD ** -0.5


def _split(x):
    hi = x.astype(bf16)
    return hi, (x - hi.astype(f32)).astype(bf16)


def _dot2(x, m, dn):
    hi, lo = _split(x)
    return _dot(hi, m, dn) + _dot(lo, m, dn)


def _sb_block(qs, kh, mask, m_gt, c):
    z = _dot(qs, kh, NT)
    sp = jnp.maximum(z, 0.0) + jnp.log(1.0 + jnp.exp(-jnp.abs(z)))
    lneg = jnp.where(mask, -sp, 0.0)
    lsz = z - sp
    suf = _dot2(lneg, m_gt, NN) + c
    a = jnp.where(mask, jnp.exp(lsz + suf), 0.0)
    return lneg, lsz, a


def _sb_masks(tq, i, jj):
    t_idx = i * tq + lax.broadcasted_iota(jnp.int32, (tq, B_TK), 0)
    s_idx = jj * B_TK + lax.broadcasted_iota(jnp.int32, (tq, B_TK), 1)
    return s_idx < t_idx


def _sb_tri(strict):
    r = lax.broadcasted_iota(jnp.int32, (B_TK, B_TK), 0)
    c = lax.broadcasted_iota(jnp.int32, (B_TK, B_TK), 1)
    return (r > c if strict else r >= c).astype(bf16)


B_DEAD = -110.0


def _sb_walk(nkb, step, init):
    def cond(state):
        it, alive, _ = state
        return jnp.logical_and(it < nkb, alive)

    def body(state):
        it, _, carry = state
        carry = step(it, carry)
        top = jnp.maximum(jnp.max(carry[0][1]), jnp.max(carry[1][1]))
        return it + 1, top > B_DEAD, carry

    return lax.while_loop(cond, body, (jnp.int32(0), jnp.bool_(True), init))[2]


def sb_fwd(u):
    S = u.shape[0]
    tq = 256

    def body(q_ref, k_ref, v_ref, o_ref):
        i = pl.program_id(1)
        nkb = (i + 1) * (tq // B_TK)
        m_gt = _sb_tri(True)
        qs = [(q_ref[:, hh * HD:(hh + 1) * HD] * SCALE).astype(bf16) for hh in range(2)]

        def step(it, carry):
            jj = nkb - 1 - it
            rows = pl.ds(pl.multiple_of(jj * B_TK, B_TK), B_TK)
            mask = _sb_masks(tq, i, jj)
            kb, vb = k_ref[rows, :], v_ref[rows, :]
            out = []
            for hh in range(2):
                acc, c = carry[hh]
                kh = kb[:, hh * HD:(hh + 1) * HD].astype(bf16)
                vh = vb[:, hh * HD:(hh + 1) * HD].astype(bf16)
                lneg, _, a = _sb_block(qs[hh], kh, mask, m_gt, c)
                out.append((acc + _dot2(a, vh, NN), c + jnp.sum(lneg, axis=1, keepdims=True)))
            return tuple(out)

        z0 = (jnp.zeros((tq, HD), f32), jnp.zeros((tq, 1), f32))
        res = _sb_walk(nkb, step, (z0, z0))
        o_ref[...] = jnp.concatenate([res[0][0], res[1][0]], axis=1)

    return pl.pallas_call(
        body, name="sb_fwd", grid=(3, S // tq),
        in_specs=[pl.BlockSpec((tq, LANE), lambda p, i: (i, O_BQ // LANE + p)),
                  pl.BlockSpec((S, LANE), lambda p, i: (0, O_BK // LANE + p)),
                  pl.BlockSpec((S, LANE), lambda p, i: (0, O_BV // LANE + p))],
        out_specs=pl.BlockSpec((tq, LANE), lambda p, i: (i, p)),
        out_shape=jax.ShapeDtypeStruct((S, 384), f32),
        compiler_params=_cp(("parallel", "arbitrary")),
    )(u, u, u)


def sb_bwd(u, yb, dyb):
    S = u.shape[0]
    tq = 128

    def body(q_ref, k_ref, v_ref, y_ref, dy_ref, dq_ref, dk_ref, dv_ref):
        i = pl.program_id(1)

        @pl.when(i == 0)
        def _():
            dk_ref[...] = jnp.zeros_like(dk_ref)
            dv_ref[...] = jnp.zeros_like(dv_ref)

        nkb = (i + 1) * (tq // B_TK)
        m_gt = _sb_tri(True)
        m_ge = _sb_tri(False)
        qs, dos, tot = [], [], []
        for hh in range(2):
            hs = slice(hh * HD, (hh + 1) * HD)
            qs.append((q_ref[:, hs] * SCALE).astype(bf16))
            dob = dy_ref[:, hs].astype(bf16)
            dos.append(dob)
            tot.append(jnp.sum(dob.astype(f32) * y_ref[:, hs], axis=1, keepdims=True))

        def step(it, carry):
            jj = nkb - 1 - it
            rows = pl.ds(pl.multiple_of(jj * B_TK, B_TK), B_TK)
            mask = _sb_masks(tq, i, jj)
            kb, vb = k_ref[rows, :], v_ref[rows, :]
            out, dk_p, dv_p = [], [], []
            for hh in range(2):
                dq, c, cg = carry[hh]
                kh = kb[:, hh * HD:(hh + 1) * HD].astype(bf16)
                vh = vb[:, hh * HD:(hh + 1) * HD].astype(bf16)
                lneg, lsz, a = _sb_block(qs[hh], kh, mask, m_gt, c)
                g = a * _dot(dos[hh], vh, NT)
                p = tot[hh] - cg - _dot2(g, m_ge, NN)
                beta = jnp.exp(lsz)
                dz = jnp.where(mask, g * (1.0 - beta) - beta * p, 0.0).astype(bf16)
                dk_p.append(_dot(dz, qs[hh], TN))
                dv_p.append(_dot(a.astype(bf16), dos[hh], TN))
                out.append((dq + _dot(dz, kh, NN), c + jnp.sum(lneg, axis=1, keepdims=True),
                            cg + jnp.sum(g, axis=1, keepdims=True)))
            dk_ref[rows, :] += jnp.concatenate(dk_p, axis=1)
            dv_ref[rows, :] += jnp.concatenate(dv_p, axis=1)
            return tuple(out)

        z0 = (jnp.zeros((tq, HD), f32), jnp.zeros((tq, 1), f32), jnp.zeros((tq, 1), f32))
        res = _sb_walk(nkb, step, (z0, z0))
        dq_ref[...] = jnp.concatenate([res[0][0], res[1][0]], axis=1) * SCALE

    row = pl.BlockSpec((tq, LANE), lambda p, i: (i, p))
    full = pl.BlockSpec((S, LANE), lambda p, i: (0, p))
    osd = jax.ShapeDtypeStruct((S, 384), f32)
    return pl.pallas_call(
        body, name="sb_bwd", grid=(3, S // tq),
        in_specs=[pl.BlockSpec((tq, LANE), lambda p, i: (i, O_BQ // LANE + p)),
                  pl.BlockSpec((S, LANE), lambda p, i: (0, O_BK // LANE + p)),
                  pl.BlockSpec((S, LANE), lambda p, i: (0, O_BV // LANE + p)), row, row],
        out_specs=[row, full, full], out_shape=[osd, osd, osd],
        compiler_params=_cp(("parallel", "arbitrary")),
    )(u, u, u, yb, dyb)


U_NB = IN_COLS // LANE


def _dil_scores(qs, kc, kp, i, slope_r):
    qi = lax.broadcasted_iota(jnp.int32, (C_BLK, C_BLK), 0)
    kj = lax.broadcasted_iota(jnp.int32, (C_BLK, C_BLK), 1)
    d_c = qi - kj
    d_p = d_c + C_BLK
    ok_c = d_c >= 0
    ok_p = jnp.logical_and(d_c <= 0, i > 0)
    s_c = jnp.where(ok_c, _dot(qs, kc, NT) - slope_r * d_c.astype(f32), NEG_BIG)
    s_p = jnp.where(ok_p, _dot(qs, kp, NT) - slope_r * d_p.astype(f32), NEG_BIG)
    return s_c, s_p, ok_c, ok_p


def _dil_slope(g, r, hh):
    pair = pl.program_id(1)
    return jnp.where(pair == 0, C_SLOPES[4 * g + hh] * r, C_SLOPES[4 * g + 2 + hh] * r).astype(f32)


def _dil_u_specs(g, r, L):
    def im(off):
        return lambda rho, p: (0, rho * U_NB + (off + g * 256) // LANE + p)
    return [pl.BlockSpec((L, LANE), im(O_CQ)), pl.BlockSpec((L, LANE), im(O_CK)), pl.BlockSpec((L, LANE), im(O_CV))]


def dil_fwd(u, g):
    S = u.shape[0]
    r = C_GROUPS[g][1]
    L = S // r
    nbk = L // C_BLK

    def body(q_ref, k_ref, v_ref, o_ref, l_ref):
        def step(i, carry):
            rc = pl.ds(pl.multiple_of(i * C_BLK, C_BLK), C_BLK)
            rp = pl.ds(pl.multiple_of(jnp.maximum(i - 1, 0) * C_BLK, C_BLK), C_BLK)
            o_p, l_p = [], []
            for hh in range(2):
                hs = slice(hh * HD, (hh + 1) * HD)
                qs = (q_ref[rc, hs] * SCALE).astype(bf16)
                kc, kp = k_ref[rc, hs].astype(bf16), k_ref[rp, hs].astype(bf16)
                vc, vp = v_ref[rc, hs].astype(bf16), v_ref[rp, hs].astype(bf16)
                s_c, s_p, _, _ = _dil_scores(qs, kc, kp, i, _dil_slope(g, r, hh))
                m = jnp.maximum(jnp.max(s_c, axis=1, keepdims=True), jnp.max(s_p, axis=1, keepdims=True))
                p_c, p_p = jnp.exp(s_c - m), jnp.exp(s_p - m)
                den = jnp.sum(p_c, axis=1, keepdims=True) + jnp.sum(p_p, axis=1, keepdims=True)
                o_p.append((_dot(p_c.astype(bf16), vc, NN) + _dot(p_p.astype(bf16), vp, NN)) / den)
                l_p.append(jnp.broadcast_to(m + jnp.log(den), (C_BLK, HD)))
            o_ref[rc, :] = jnp.concatenate(o_p, axis=1)
            l_ref[rc, :] = jnp.concatenate(l_p, axis=1)
            return carry

        lax.fori_loop(0, nbk, step, 0)

    ospec = pl.BlockSpec((L, LANE), lambda rho, p: (0, rho * 2 + p))
    osd = jax.ShapeDtypeStruct((L, r * 256), f32)
    u2 = u.reshape(L, r * IN_COLS)
    o, lse = pl.pallas_call(
        body, name=f"dil_fwd{g}", grid=(r, 2),
        in_specs=_dil_u_specs(g, r, L), out_specs=[ospec, ospec], out_shape=[osd, osd],
        compiler_params=_cp(("parallel", "parallel")),
    )(u2, u2, u2)
    return o.reshape(S, 256), lse.reshape(S, 256)


def dil_merge(os_, ls_):
    S = os_[0].shape[0]

    def body(o0, o1, o2, l0, l1, l2, y_ref, lse_ref):
        a, b, c = l0[...], l1[...], l2[...]
        m = jnp.maximum(jnp.maximum(a, b), c)
        ea, eb, ec = jnp.exp(a - m), jnp.exp(b - m), jnp.exp(c - m)
        den = ea + eb + ec
        y_ref[...] = (ea * o0[...] + eb * o1[...] + ec * o2[...]) / den
        lse_ref[...] = m + jnp.log(den)

    spec = pl.BlockSpec((512, 256), lambda i: (i, 0))
    osd = jax.ShapeDtypeStruct((S, 256), f32)
    return pl.pallas_call(
        body, name="dil_merge", grid=(S // 512,), in_specs=[spec] * 6, out_specs=[spec, spec],
        out_shape=[osd, osd], compiler_params=_cp(("parallel",)),
    )(*os_, *ls_)


def dil_bwd(u, g, dyc, yc, lse):
    S = u.shape[0]
    r = C_GROUPS[g][1]
    L = S // r
    nbk = L // C_BLK

    def body(q_ref, k_ref, v_ref, dy_ref, y_ref, l_ref, dq_ref, dk_ref, dv_ref):
        dk_ref[...] = jnp.zeros_like(dk_ref)
        dv_ref[...] = jnp.zeros_like(dv_ref)

        def step(i, carry):
            rc = pl.ds(pl.multiple_of(i * C_BLK, C_BLK), C_BLK)
            rp = pl.ds(pl.multiple_of(jnp.maximum(i - 1, 0) * C_BLK, C_BLK), C_BLK)
            dq_p, dkc_p, dkp_p, dvc_p, dvp_p = [], [], [], [], []
            for hh in range(2):
                hs = slice(hh * HD, (hh + 1) * HD)
                qs = (q_ref[rc, hs] * SCALE).astype(bf16)
                kc, kp = k_ref[rc, hs].astype(bf16), k_ref[rp, hs].astype(bf16)
                vc, vp = v_ref[rc, hs].astype(bf16), v_ref[rp, hs].astype(bf16)
                dy = dy_ref[rc, hs]
                dyb = dy.astype(bf16)
                s_c, s_p, ok_c, ok_p = _dil_scores(qs, kc, kp, i, _dil_slope(g, r, hh))
                lrow = l_ref[rc, hh * HD:hh * HD + 1]
                delta = jnp.sum(dy * y_ref[rc, hs], axis=1, keepdims=True)
                pi_c = jnp.where(ok_c, jnp.exp(s_c - lrow), 0.0)
                pi_p = jnp.where(ok_p, jnp.exp(s_p - lrow), 0.0)
                ds_c = (pi_c * (_dot(dyb, vc, NT) - delta)).astype(bf16)
                ds_p = (pi_p * (_dot(dyb, vp, NT) - delta)).astype(bf16)
                dq_p.append((_dot(ds_c, kc, NN) + _dot(ds_p, kp, NN)) * SCALE)
                dkc_p.append(_dot(ds_c, qs, TN))
                dkp_p.append(_dot(ds_p, qs, TN))
                dvc_p.append(_dot(pi_c.astype(bf16), dyb, TN))
                dvp_p.append(_dot(pi_p.astype(bf16), dyb, TN))
            dq_ref[rc, :] = jnp.concatenate(dq_p, axis=1)
            dk_ref[rc, :] += jnp.concatenate(dkc_p, axis=1)
            dv_ref[rc, :] += jnp.concatenate(dvc_p, axis=1)
            dk_ref[rp, :] += jnp.concatenate(dkp_p, axis=1)
            dv_ref[rp, :] += jnp.concatenate(dvp_p, axis=1)
            return carry

        lax.fori_loop(0, nbk, step, 0)

    ospec = pl.BlockSpec((L, LANE), lambda rho, p: (0, rho * 2 + p))
    osd = jax.ShapeDtypeStruct((L, r * 256), f32)
    u2 = u.reshape(L, r * IN_COLS)
    v2 = lambda t: t.reshape(L, r * 256)
    dq, dk, dv = pl.pallas_call(
        body, name=f"dil_bwd{g}", grid=(r, 2),
        in_specs=_dil_u_specs(g, r, L) + [ospec, ospec, ospec], out_specs=[ospec] * 3, out_shape=[osd] * 3,
        compiler_params=_cp(("parallel", "parallel")),
    )(u2, u2, u2, v2(dyc), v2(yc), v2(lse))
    return dq.reshape(S, 256), dk.reshape(S, 256), dv.reshape(S, 256)


def _rows_tile(rows):
    return _tile(rows, (256, 176, 128, 64, 32, 16, 8))


def cast_bf16(w):
    shape = w.shape
    w2 = w.reshape(-1, shape[-1])
    rows, cols = w2.shape
    tr = _rows_tile(rows)

    def body(x_ref, o_ref):
        o_ref[...] = x_ref[...].astype(bf16)

    spec = pl.BlockSpec((tr, cols), lambda i: (i, 0))
    out = pl.pallas_call(
        body, name="cast_bf16", grid=(rows // tr,), in_specs=[spec], out_specs=spec,
        out_shape=jax.ShapeDtypeStruct((rows, cols), bf16), compiler_params=_cp(("parallel",)),
    )(w2)
    return out.reshape(shape)


BC1 = 1.0 - ADAM_B1 ** ADAM_STEP
BC2 = 1.0 - ADAM_B2 ** ADAM_STEP


def _adam_math(w, g, m, v):
    m2 = ADAM_B1 * m + (1.0 - ADAM_B1) * g
    v2 = ADAM_B2 * v + (1.0 - ADAM_B2) * (g * g)
    delta = -ADAM_LR * ((m2 / BC1) / (jnp.sqrt(v2 / BC2) + ADAM_EPS) + ADAM_WD * w)
    return delta, m2, v2


def adam(w, g, m, v):
    shape = w.shape
    r2 = lambda t: t.reshape(-1, shape[-1])
    rows, cols = r2(w).shape
    tr = _rows_tile(rows)

    def body(w_ref, g_ref, m_ref, v_ref, d_ref, m2_ref, v2_ref):
        d_ref[...], m2_ref[...], v2_ref[...] = _adam_math(w_ref[...], g_ref[...], m_ref[...], v_ref[...])

    spec = pl.BlockSpec((tr, cols), lambda i: (i, 0))
    osd = jax.ShapeDtypeStruct((rows, cols), f32)
    outs = pl.pallas_call(
        body, name="adam", grid=(rows // tr,), in_specs=[spec] * 4, out_specs=[spec] * 3, out_shape=[osd] * 3,
        compiler_params=_cp(("parallel",)),
    )(r2(w), r2(g), r2(m), r2(v))
    return [o.reshape(shape) for o in outs]


ADA_N = 9 * D // 4
ADA_TN = 384


def ada_fwd(c_all, w_ada):
    def body(c_ref, w_ref, o_ref):
        cv = c_ref[...]
        o_ref[0] = _dot((cv * _sig(cv)).astype(bf16), w_ref[0].astype(bf16), NN)

    return pl.pallas_call(
        body, name="ada_fwd", grid=(DEPTH, ADA_N // ADA_TN),
        in_specs=[pl.BlockSpec((8, D), lambda l, j: (0, 0)), pl.BlockSpec((1, D, ADA_TN), lambda l, j: (l, 0, j))],
        out_specs=pl.BlockSpec((1, 8, ADA_TN), lambda l, j: (l, 0, j)),
        out_shape=jax.ShapeDtypeStruct((DEPTH, 8, ADA_N), f32), compiler_params=_cp(("parallel", "parallel")),
    )(c_all, w_ada)


def ada_bwd_adam(c_all, dm, w, m, v):
    tr = 128

    def body(c_ref, dm_ref, w_ref, m_ref, v_ref, g_ref, d_ref, m2_ref, v2_ref):
        cv = c_ref[...]
        g = _dot((cv * _sig(cv)).astype(bf16), dm_ref[0].astype(bf16), TN)
        g_ref[0] = g
        d_ref[0], m2_ref[0], v2_ref[0] = _adam_math(w_ref[0], g, m_ref[0], v_ref[0])

    wspec = pl.BlockSpec((1, tr, ADA_N), lambda l, i: (l, i, 0))
    osd = jax.ShapeDtypeStruct((DEPTH, D, ADA_N), f32)
    return pl.pallas_call(
        body, name="ada_bwd_adam", grid=(DEPTH, D // tr),
        in_specs=[pl.BlockSpec((8, tr), lambda l, i: (0, i)), pl.BlockSpec((1, 8, ADA_N), lambda l, i: (l, 0, 0)),
                  wspec, wspec, wspec],
        out_specs=[wspec] * 4, out_shape=[osd] * 4, compiler_params=_cp(("parallel", "parallel")),
    )(c_all, dm, w, m, v)


def _lb_probs(x):
    mx = jnp.max(x, axis=0, keepdims=True)
    e = jnp.exp(x - mx)
    return e / jnp.sum(e, axis=0, keepdims=True)


def lb_fwd(logits):
    def body(x_ref, o_ref):
        p = _lb_probs(x_ref[...])
        rows = [jnp.zeros((1, 768), f32)]
        for l in range(1, DEPTH):
            rows.append(rows[-1] + p[l:l + 1, :])
        o_ref[...] = jnp.concatenate(rows, axis=0)

    return pl.pallas_call(body, name="lb_fwd", out_shape=jax.ShapeDtypeStruct((DEPTH, 768), f32))(logits)


def lb_bwd(logits, dlb):
    def body(x_ref, d_ref, o_ref):
        p = _lb_probs(x_ref[...])
        d = d_ref[...]
        rows = [jnp.zeros((1, 768), f32)] * DEPTH
        acc = jnp.zeros((1, 768), f32)
        for l in range(DEPTH - 1, 0, -1):
            acc = acc + d[l:l + 1, :]
            rows[l] = acc
        dp = jnp.concatenate(rows, axis=0)
        o_ref[...] = p * (dp - jnp.sum(p * dp, axis=0, keepdims=True))

    return pl.pallas_call(body, name="lb_bwd", out_shape=jax.ShapeDtypeStruct((DEPTH, 768), f32))(logits, dlb)


def sum_slots(x):
    n, rows, cols = x.shape
    tr = _rows_tile(rows)

    def body(x_ref, o_ref):
        acc = x_ref[0]
        for j in range(1, n):
            acc = acc + x_ref[j]
        o_ref[...] = acc

    return pl.pallas_call(
        body, name="sum_slots", grid=(rows // tr,),
        in_specs=[pl.BlockSpec((n, tr, cols), lambda i: (0, i, 0))], out_specs=pl.BlockSpec((tr, cols), lambda i: (i, 0)),
        out_shape=jax.ShapeDtypeStruct((rows, cols), f32), compiler_params=_cp(("parallel",)),
    )(x)


ANY = pl.BlockSpec(memory_space=pl.ANY)
CHIP_FLIPS = ((1, 0), (0, 1), (1, 1))
DEV_FLIPS = tuple((a, b, d) for a in (0, 1) for b in (0, 1) for d in (0, 1))[1:]


def _me():
    return lax.axis_index("x"), lax.axis_index("y"), lax.axis_index("c")


def _flip(v, f):
    return 1 - v if f else v


def _comm_call(body, name, ins, out_shapes, n_remote, n_local):
    return pl.pallas_call(
        body, name=name, in_specs=[ANY] * len(ins), out_specs=[ANY] * len(out_shapes), out_shape=out_shapes,
        scratch_shapes=[pltpu.SemaphoreType.DMA((n_remote,)), pltpu.SemaphoreType.DMA((n_remote,)),
                        pltpu.SemaphoreType.DMA((max(n_local, 1),))],
    )(*ins)


def all_gather_chips(arrs, layer=None, name="ag4"):
    n = len(arrs)
    shapes = [a.shape if layer is None else a.shape[1:] for a in arrs]

    def body(*refs):
        ins, outs, (send, recv, loc) = refs[:n], refs[n:2 * n], refs[2 * n:]
        x, y, c = _me()
        mine = 2 * x + y
        srcs = [r if layer is None else r.at[layer] for r in ins]
        locs = [pltpu.make_async_copy(srcs[a], outs[a].at[mine], loc.at[a]) for a in range(n)]
        for cp in locs:
            cp.start()

        def remote(a, k, slot):
            fx, fy = CHIP_FLIPS[k]
            return pltpu.make_async_remote_copy(srcs[a], outs[a].at[slot], send.at[3 * a + k], recv.at[3 * a + k],
                                                device_id=(_flip(x, fx), _flip(y, fy), c), device_id_type=MESH)

        for a in range(n):
            for k in range(3):
                remote(a, k, mine).start()
        for a in range(n):
            for k, (fx, fy) in enumerate(CHIP_FLIPS):
                cp = remote(a, k, 2 * _flip(x, fx) + _flip(y, fy))
                cp.wait_recv()
                cp.wait_send()
        for cp in locs:
            cp.wait()

    outs = [jax.ShapeDtypeStruct((4,) + tuple(s), a.dtype) for s, a in zip(shapes, arrs)]
    return _comm_call(body, name, arrs, outs, 3 * n, n)


def all_gather_devs(arr, name="ag8"):
    def body(in_ref, out_ref, send, recv, loc):
        x, y, c = _me()
        mine = 4 * x + 2 * y + c
        lc = pltpu.make_async_copy(in_ref, out_ref.at[mine], loc.at[0])
        lc.start()

        def remote(k, slot):
            fx, fy, fc = DEV_FLIPS[k]
            return pltpu.make_async_remote_copy(in_ref, out_ref.at[slot], send.at[k], recv.at[k],
                                                device_id=(_flip(x, fx), _flip(y, fy), _flip(c, fc)), device_id_type=MESH)

        for k in range(7):
            remote(k, mine).start()
        for k, (fx, fy, fc) in enumerate(DEV_FLIPS):
            cp = remote(k, 4 * _flip(x, fx) + 2 * _flip(y, fy) + _flip(c, fc))
            cp.wait_recv()
            cp.wait_send()
        lc.wait()

    return _comm_call(body, name, [arr], [jax.ShapeDtypeStruct((8,) + arr.shape, arr.dtype)], 7, 1)[0]


def _rows_of(which, rows):
    return pl.ds(pl.multiple_of(which * rows, 16), rows)


def dev_exchange(parts, name="rs_x8"):
    n = len(parts)

    def body(*refs):
        ins, outs, (send, recv, loc) = refs[:n], refs[n:2 * n], refs[2 * n:]
        x, y, c = _me()
        mine = 4 * x + 2 * y + c

        def piece(a, px, py, pc):
            rows = ins[a].shape[1] // 2
            return ins[a].at[2 * px + py, _rows_of(pc, rows), :]

        locs = [pltpu.make_async_copy(piece(a, x, y, c), outs[a].at[mine], loc.at[a]) for a in range(n)]
        for cp in locs:
            cp.start()

        def remote(a, k, slot):
            fx, fy, fc = DEV_FLIPS[k]
            px, py, pc = _flip(x, fx), _flip(y, fy), _flip(c, fc)
            return pltpu.make_async_remote_copy(piece(a, px, py, pc), outs[a].at[slot], send.at[7 * a + k], recv.at[7 * a + k],
                                                device_id=(px, py, pc), device_id_type=MESH)

        for a in range(n):
            for k in range(7):
                remote(a, k, mine).start()
        for a in range(n):
            for k, (fx, fy, fc) in enumerate(DEV_FLIPS):
                cp = remote(a, k, 4 * _flip(x, fx) + 2 * _flip(y, fy) + _flip(c, fc))
                cp.wait_recv()
                cp.wait_send()
        for cp in locs:
            cp.wait()

    outs = [jax.ShapeDtypeStruct((8, p.shape[1] // 2, p.shape[2]), p.dtype) for p in parts]
    return _comm_call(body, name, parts, outs, 7 * n, n)


def sum_share(slots, name="rs_sum"):
    n, r, cols = slots.shape
    tr = _tile(r, (128, 176, 64))
    steps = r // tr

    def body(s_ref, g_ref, buf, send, loc, recv):
        i = pl.program_id(0)
        x, y, c = _me()
        slot = i % 2

        def copies(step, sl):
            rows = pl.ds(pl.multiple_of(c * r + step * tr, 8), tr)
            rem = pltpu.make_async_remote_copy(buf.at[sl], g_ref.at[rows, :], send.at[sl], recv.at[0],
                                               device_id=(x, y, 1 - c), device_id_type=MESH)
            return rem, pltpu.make_async_copy(buf.at[sl], g_ref.at[rows, :], loc.at[sl])

        @pl.when(i >= 2)
        def _():
            rem, lc = copies(i - 2, slot)
            rem.wait_send()
            lc.wait()

        acc = s_ref[0].astype(f32)
        for j in range(1, n):
            acc = acc + s_ref[j].astype(f32)
        buf[slot] = acc
        rem, lc = copies(i, slot)
        rem.start()
        lc.start()

        @pl.when(i == steps - 1)
        def _():
            for back in range(min(2, steps)):
                rem, lc = copies(i - back, (i - back) % 2)
                rem.wait_send()
                lc.wait()
            other = g_ref.at[pl.ds(pl.multiple_of((1 - c) * r, 8), r), :]
            pltpu.make_async_remote_copy(other, other, send.at[0], recv.at[0],
                                         device_id=(x, y, 1 - c), device_id_type=MESH).wait_recv()

    return pl.pallas_call(
        body, name=name, grid=(steps,),
        in_specs=[pl.BlockSpec((n, tr, cols), lambda i: (0, i, 0))], out_specs=ANY,
        out_shape=jax.ShapeDtypeStruct((2 * r, cols), f32),
        scratch_shapes=[pltpu.VMEM((2, tr, cols), f32), pltpu.SemaphoreType.DMA((2,)), pltpu.SemaphoreType.DMA((2,)),
                        pltpu.SemaphoreType.DMA((1,))],
        compiler_params=_cp(("arbitrary",)),
    )(slots)


BIG = ("ffn1_w_in", "ffn1_w_out", "w_in", "w_branch_a", "w_branch_b", "w_branch_c", "w_out", "ffn2_w_in", "ffn2_w_out")
ROW_SHARDED = ("ffn1_w_out", "w_out", "ffn2_w_out")
RES_W = (0.5, 1.0, 0.5)


def _full_weight(name, g):
    if name in ROW_SHARDED:
        return g.reshape(4 * g.shape[1], g.shape[2])
    return jnp.concatenate([g[0], g[1], g[2], g[3]], axis=1)


def _by_shard(name, dw):
    if name in ROW_SHARDED:
        return dw.reshape(4, dw.shape[0] // 4, dw.shape[1])
    return dw.reshape(dw.shape[0], 4, dw.shape[1] // 4).transpose(1, 0, 2)


def _ffn_fwd(x, w_in, w_out, a_vec, sh_vec, b_vec):
    h = prenorm(x, a_vec, sh_vec)
    u = mm(h, w_in, name="ffn_in")
    s = swiglu(u)
    y = mm(s, w_out, name="ffn_out")
    return postnorm(x, y, b_vec), (x, h, u, s, y)


def _ffn_bwd(dout, saved, w_in, w_out, a_vec, b_vec):
    x, h, u, s, y = saved
    dy, db = post_bwd(dout, y, b_vec)
    ds = mm(dy, w_out, tb=True, name="ffn_ds")
    dw_out = mm(s, dy, ta=True, out_dtype=bf16, name="ffn_dwo")
    du = swiglu_bwd(ds, u)
    dh = mm(du, w_in, tb=True, name="ffn_dh")
    dw_in = mm(h, du, ta=True, out_dtype=bf16, name="ffn_dwi")
    dx, dsh, da = pre_bwd(dout, dh, x, a_vec)
    return dx, dw_in, dw_out, dsh, da, db


def _mix_fwd(x, w, lb, ng, a_vec, sh_vec, b_vec):
    h = prenorm(x, a_vec, sh_vec)
    u = mm(h, w["w_in"], name="mix_in")
    o, ya, st = hgrn_fwd(u, lb, ng)
    yb = sb_fwd(u)
    groups = [dil_fwd(u, g) for g in range(3)]
    yc, lse = dil_merge([o_ for o_, _ in groups], [l_ for _, l_ in groups])
    pa = mm(ya, w["w_branch_a"], name="mix_pa")
    pb = mm(yb, w["w_branch_b"], name="mix_pb")
    pc = mm(yc, w["w_branch_c"], name="mix_pc")
    merged = gate_merge(u, pa, pb, pc)
    z = mm(merged, w["w_out"], name="mix_out")
    return postnorm(x, z, b_vec), (x, h, u, o, ya, st, yb, yc, lse, pa, pb, pc, merged, z)


def _mix_bwd(dout, saved, w, lb, ng, a_vec, b_vec):
    x, h, u, o, ya, st, yb, yc, lse, pa, pb, pc, merged, z = saved
    dz, db = post_bwd(dout, z, b_vec)
    dmerged = mm(dz, w["w_out"], tb=True, name="mix_dm")
    dw_out = mm(merged, dz, ta=True, out_dtype=bf16, name="mix_dwo")
    dpa, dpb, dpc, dg0, dg1, dg2 = gate_bwd(dmerged, u, pa, pb, pc)
    dya = mm(dpa, w["w_branch_a"], tb=True, name="mix_dya")
    dyb = mm(dpb, w["w_branch_b"], tb=True, name="mix_dyb")
    dyc = mm(dpc, w["w_branch_c"], tb=True, name="mix_dyc")
    dw_a = mm(ya, dpa, ta=True, out_dtype=bf16, name="mix_dwa")
    dw_b = mm(yb, dpb, ta=True, out_dtype=bf16, name="mix_dwb")
    dw_c = mm(yc, dpc, ta=True, out_dtype=bf16, name="mix_dwc")
    daq, daf, dai, dag, dlb, dng = hgrn_bwd(u, lb, ng, o, st, dya)
    dbq, dbk, dbv = sb_bwd(u, yb, dyb)
    dc = [dil_bwd(u, g, dyc, yc, lse) for g in range(3)]
    du = jnp.concatenate(
        [daq, daf, dai, dag] + [t.astype(bf16) for t in (dbq, dbk, dbv)]
        + [dc[g][j].astype(bf16) for j in range(3) for g in range(3)] + [dg0, dg1, dg2], axis=1)
    dh = mm(du, w["w_in"], tb=True, name="mix_dh")
    dw_in = mm(h, du, ta=True, out_dtype=bf16, name="mix_dwi")
    dx, dsh, da = pre_bwd(dout, dh, x, a_vec)
    grads = {"w_in": dw_in, "w_out": dw_out, "w_branch_a": dw_a, "w_branch_b": dw_b, "w_branch_c": dw_c}
    return dx, grads, dlb, jnp.sum(dng, axis=0), dsh, da, db


def _reduce_to_shards(names, grads):
    parts = [_by_shard(n, grads[n]) for n in names]
    return {n: sum_share(s) for n, s in zip(names, dev_exchange(parts))}


def kernel(x, c, w_ada, b_ada, norm_g, ffn1_w_in, ffn1_w_out, w_in, hgrn_lb_logits, hgrn_norm_g, w_branch_a, w_branch_b, w_branch_c, w_out, ffn2_w_in, ffn2_w_out, loss_target, m_w_ada, m_b_ada, m_norm_g, m_ffn1_w_in, m_ffn1_w_out, m_w_in, m_hgrn_lb_logits, m_hgrn_norm_g, m_w_branch_a, m_w_branch_b, m_w_branch_c, m_w_out, m_ffn2_w_in, m_ffn2_w_out, v_w_ada, v_b_ada, v_norm_g, v_ffn1_w_in, v_ffn1_w_out, v_w_in, v_hgrn_lb_logits, v_hgrn_norm_g, v_w_branch_a, v_w_branch_b, v_w_branch_c, v_w_out, v_ffn2_w_in, v_ffn2_w_out):
    weights = dict(w_ada=w_ada, b_ada=b_ada, norm_g=norm_g, ffn1_w_in=ffn1_w_in, ffn1_w_out=ffn1_w_out, w_in=w_in,
                   hgrn_lb_logits=hgrn_lb_logits, hgrn_norm_g=hgrn_norm_g, w_branch_a=w_branch_a, w_branch_b=w_branch_b,
                   w_branch_c=w_branch_c, w_out=w_out, ffn2_w_in=ffn2_w_in, ffn2_w_out=ffn2_w_out)
    mom = dict(w_ada=m_w_ada, b_ada=m_b_ada, norm_g=m_norm_g, ffn1_w_in=m_ffn1_w_in, ffn1_w_out=m_ffn1_w_out, w_in=m_w_in,
               hgrn_lb_logits=m_hgrn_lb_logits, hgrn_norm_g=m_hgrn_norm_g, w_branch_a=m_w_branch_a, w_branch_b=m_w_branch_b,
               w_branch_c=m_w_branch_c, w_out=m_w_out, ffn2_w_in=m_ffn2_w_in, ffn2_w_out=m_ffn2_w_out)
    var = dict(w_ada=v_w_ada, b_ada=v_b_ada, norm_g=v_norm_g, ffn1_w_in=v_ffn1_w_in, ffn1_w_out=v_ffn1_w_out, w_in=v_w_in,
               hgrn_lb_logits=v_hgrn_lb_logits, hgrn_norm_g=v_hgrn_norm_g, w_branch_a=v_w_branch_a, w_branch_b=v_w_branch_b,
               w_branch_c=v_w_branch_c, w_out=v_w_out, ffn2_w_in=v_ffn2_w_in, ffn2_w_out=v_ffn2_w_out)
    order = list(weights)
    xi, yi, ci = _me()
    chip = 2 * xi + yi
    dev = 4 * xi + 2 * yi + ci
    xs = x[0]

    c_all = all_gather_devs(c, name="ag8_c").reshape(8, D)
    mod_sh = all_gather_chips([ada_fwd(c_all, w_ada)], name="ag4_mod")[0]
    mod_all = mod_sh.transpose(1, 2, 0, 3).reshape(DEPTH, 8, 9 * D)
    mod = lax.dynamic_index_in_dim(mod_all, dev, axis=1, keepdims=False) + b_ada
    mod = mod.reshape(DEPTH, 3, 3, D)
    ng_all = all_gather_chips([norm_g.reshape(DEPTH * 6, D // 4)], name="ag4_norm")[0]
    ng_all = ng_all.reshape(4, DEPTH, 6, D // 4).transpose(1, 2, 0, 3).reshape(DEPTH, 6, D)
    lb_all = lb_fwd(hgrn_lb_logits)
    w16 = {n: cast_bf16(weights[n]) for n in BIG}

    def vecs(l, i):
        shift, scale, gate = mod[l, i, 0][None], mod[l, i, 1][None], mod[l, i, 2][None]
        g_pre, g_post = ng_all[l, 2 * i][None], ng_all[l, 2 * i + 1][None]
        return g_pre * (1.0 + scale), shift, RES_W[i] * gate * g_post

    saved, full = [], []
    for l in range(DEPTH):
        gathered = all_gather_chips([w16[n] for n in BIG], layer=l, name=f"ag4_w{l}")
        w = {n: _full_weight(n, g) for n, g in zip(BIG, gathered)}
        full.append(w)
        lb, ng = lb_all[l][None], hgrn_norm_g[l][None]
        xs, s1 = _ffn_fwd(xs, w["ffn1_w_in"], w["ffn1_w_out"], *vecs(l, 0))
        xs, s2 = _mix_fwd(xs, w, lb, ng, *vecs(l, 1))
        xs, s3 = _ffn_fwd(xs, w["ffn2_w_in"], w["ffn2_w_out"], *vecs(l, 2))
        saved.append((s1, s2, s3))

    dx, loss_part = loss_grad(xs, loss_target[0])
    loss = lax.psum(loss_part[0, 0], ("x", "y", "c"))

    big_grads = {n: [None] * DEPTH for n in BIG}
    d_mod, d_ng, d_lb, d_hng = [None] * DEPTH, [None] * DEPTH, [None] * DEPTH, [None] * DEPTH
    for l in reversed(range(DEPTH)):
        w = full[l]
        s1, s2, s3 = saved[l]
        lb, ng = lb_all[l][None], hgrn_norm_g[l][None]
        rows_mod, rows_ng = [None] * 9, [None] * 6

        def vec_grads(i, dsh, da, db):
            scale, gate = mod[l, i, 1][None], mod[l, i, 2][None]
            g_pre, g_post = ng_all[l, 2 * i][None], ng_all[l, 2 * i + 1][None]
            rows_mod[3 * i], rows_mod[3 * i + 1], rows_mod[3 * i + 2] = dsh, g_pre * da, RES_W[i] * g_post * db
            rows_ng[2 * i], rows_ng[2 * i + 1] = (1.0 + scale) * da, RES_W[i] * gate * db

        a3, _, b3 = vecs(l, 2)
        dx, dwi, dwo, dsh, da, db = _ffn_bwd(dx, s3, w["ffn2_w_in"], w["ffn2_w_out"], a3, b3)
        vec_grads(2, dsh, da, db)
        grads = {"ffn2_w_in": dwi, "ffn2_w_out": dwo}
        a2, _, b2 = vecs(l, 1)
        dx, gmix, dlb, dhng, dsh, da, db = _mix_bwd(dx, s2, w, lb, ng, a2, b2)
        vec_grads(1, dsh, da, db)
        grads.update(gmix)
        a1, _, b1 = vecs(l, 0)
        dx, dwi, dwo, dsh, da, db = _ffn_bwd(dx, s1, w["ffn1_w_in"], w["ffn1_w_out"], a1, b1)
        vec_grads(0, dsh, da, db)
        grads.update({"ffn1_w_in": dwi, "ffn1_w_out": dwo})
        for n, g in _reduce_to_shards(BIG, grads).items():
            big_grads[n][l] = g
        d_mod[l] = jnp.concatenate(rows_mod, axis=1)
        d_ng[l] = jnp.concatenate(rows_ng, axis=0)
        d_lb[l], d_hng[l] = dlb, dhng

    n_small = 6 * D * DEPTH + 768 * DEPTH + A_V * DEPTH + 9 * D * DEPTH
    pad = -n_small % (512 * LANE)
    flat = jnp.concatenate([jnp.stack(d_ng).reshape(-1), jnp.concatenate(d_lb, axis=0).reshape(-1),
                            jnp.concatenate(d_hng, axis=0).reshape(-1), jnp.concatenate(d_mod, axis=0).reshape(-1),
                            jnp.zeros((pad,), f32)])
    small_all = all_gather_devs(flat.reshape(-1, LANE), name="ag8_small")
    total = sum_slots(small_all).reshape(-1)
    o1 = 6 * D * DEPTH
    o2 = o1 + 768 * DEPTH
    o3 = o2 + A_V * DEPTH
    g_ng_full = total[:o1].reshape(DEPTH, 6, D)
    g_lb_all = total[o1:o2].reshape(DEPTH, 768)
    g_small = {
        "norm_g": lax.dynamic_slice_in_dim(g_ng_full, chip * (D // 4), D // 4, axis=2),
        "hgrn_lb_logits": lb_bwd(hgrn_lb_logits, g_lb_all),
        "hgrn_norm_g": total[o2:o3].reshape(DEPTH, A_V),
        "b_ada": total[o3:n_small].reshape(DEPTH, 9 * D),
    }
    dmod_all = small_all.reshape(8, -1)[:, o3:n_small].reshape(8, DEPTH, 9 * D).transpose(1, 0, 2)
    dm_sh = lax.dynamic_slice_in_dim(dmod_all, chip * ADA_N, ADA_N, axis=2)

    out_g, out_d, out_m, out_v = {}, {}, {}, {}
    out_g["w_ada"], out_d["w_ada"], out_m["w_ada"], out_v["w_ada"] = ada_bwd_adam(c_all, dm_sh, w_ada, m_w_ada, v_w_ada)
    for n in BIG:
        out_g[n] = jnp.stack(big_grads[n])
    out_g.update(g_small)
    for n in order:
        if n != "w_ada":
            out_d[n], out_m[n], out_v[n] = adam(weights[n], out_g[n], mom[n], var[n])
    return (loss, dx[None], *[out_g[n] for n in order], *[out_d[n] for n in order],
            *[out_m[n] for n in order], *[out_v[n] for n in order])
```

```python
import functools
import math

import jax
import jax.numpy as jnp
from jax import lax
from jax.experimental import pallas as pl
from jax.experimental.pallas import tpu as pltpu

f32, bf16 = jnp.float32, jnp.bfloat16

D = 1024
DEPTH = 4
D_FF = 2816
EPS = 1e-6
NEG_BIG = -1e30
TINY = 1e-30
A_HEADS, A_K, A_V, A_CHUNK = 6, 128, 64, 64
A_SUB = 16
A_CLAMP = 80.0
B_HEADS, HD = 6, 64
C_GROUPS = ((128, 1), (512, 4), (2048, 16))
C_BLK = 128
IN_COLS = 8832
O_AQ, O_AF, O_AI, O_AG = 0, 768, 1536, 1920
O_BQ, O_BK, O_BV = 2304, 2688, 3072
O_CQ, O_CK, O_CV = 3456, 4224, 4992
O_GATE = 5760
LANE = 128
ADAM_LR, ADAM_B1, ADAM_B2, ADAM_EPS, ADAM_WD, ADAM_STEP = 0.001, 0.9, 0.999, 1e-08, 0.01, 10
MESH = pl.DeviceIdType.MESH
VMEM_LIMIT = 56 * 1024 * 1024


def _alibi_slopes(n):
    def pow2(m):
        start = 2.0 ** (-8.0 / m)
        return [start ** (i + 1) for i in range(m)]
    if math.log2(n).is_integer():
        s = pow2(n)
    else:
        c = 2 ** int(math.floor(math.log2(n)))
        s = pow2(c) + pow2(2 * c)[0::2][: n - c]
    return sorted(s, reverse=True)


C_SLOPES = _alibi_slopes(12)


def _tile(n, prefs):
    for p in prefs:
        if n % p == 0:
            return p
    return n


def _cp(sem):
    return pltpu.CompilerParams(dimension_semantics=sem, vmem_limit_bytes=VMEM_LIMIT)


def _sig(x):
    return 1.0 / (1.0 + jnp.exp(-x))


def _dot(a, b, dn, precision=None):
    return lax.dot_general(a, b, (dn, ((), ())), preferred_element_type=f32, precision=precision)


NN = ((1,), (0,))
NT = ((1,), (1,))
TN = ((0,), (0,))


MM_TILES = {
    (4096, 1024, 2816): (1024, 512, 2816),
    (4096, 1024, 5632): (512, 512, 5632),
    (1024, 5632, 4096): (512, 512, 4096),
    (2816, 1024, 4096): (704, 512, 4096),
    (4096, 8832, 1024): (512, 2944, 1024),
    (4096, 1024, 8832): (1024, 512, 2944),
    (1024, 8832, 4096): (512, 2944, 1024),
    (1024, 1024, 4096): (512, 512, 4096),
    (384, 1024, 4096): (384, 512, 4096),
    (256, 1024, 4096): (256, 512, 4096),
}


def mm(a, b, *, out_dtype=f32, name="mm"):
    M, K = a.shape
    K2, N = b.shape
    assert K == K2, (a.shape, b.shape)
    tm, tn, tk = MM_TILES.get((M, N, K), (_tile(M, (1024, 704, 512, 384, 256, 128)), _tile(N, (512, 384, 256, 128)),
                                          _tile(K, (1024, 512, 1408, 384, 256, 128))))
    nk = K // tk

    def body(a_ref, b_ref, o_ref, *acc):
        p = _dot(a_ref[...].astype(bf16), b_ref[...].astype(bf16), NN)
        if nk == 1:
            o_ref[...] = p.astype(out_dtype)
            return
        acc_ref, = acc
        k = pl.program_id(2)

        @pl.when(k == 0)
        def _():
            acc_ref[...] = p

        @pl.when(k > 0)
        def _():
            acc_ref[...] += p

        @pl.when(k == nk - 1)
        def _():
            o_ref[...] = acc_ref[...].astype(out_dtype)

    return pl.pallas_call(
        body, name=name, grid=(M // tm, N // tn, nk),
        in_specs=[pl.BlockSpec((tm, tk), lambda i, j, k: (i, k)), pl.BlockSpec((tk, tn), lambda i, j, k: (k, j))],
        out_specs=pl.BlockSpec((tm, tn), lambda i, j, k: (i, j)),
        out_shape=jax.ShapeDtypeStruct((M, N), out_dtype),
        scratch_shapes=[pltpu.VMEM((tm, tn), f32)] if nk > 1 else [],
        compiler_params=_cp(("parallel", "parallel", "arbitrary")),
    )(a, b)


FF_TM = 1024
FF_T = 256
FF_NB = D_FF // FF_T


def ffn_in_swiglu(h, w_in):
    S = h.shape[0]

    def body(h_ref, wa_ref, wb_ref, a_ref, b_ref, s_ref):
        hv = h_ref[...]
        a = _dot(hv, wa_ref[...], NN)
        b = _dot(hv, wb_ref[...], NN)
        a_ref[...] = a.astype(bf16)
        b_ref[...] = b.astype(bf16)
        s_ref[...] = (a * _sig(a) * b).astype(bf16)

    ospec = pl.BlockSpec((FF_TM, FF_T), lambda i, j: (i, j))
    osd = jax.ShapeDtypeStruct((S, D_FF), bf16)
    return pl.pallas_call(
        body, name="ffn_in", grid=(S // FF_TM, FF_NB),
        in_specs=[pl.BlockSpec((FF_TM, D), lambda i, j: (i, 0)), pl.BlockSpec((D, FF_T), lambda i, j: (0, j)),
                  pl.BlockSpec((D, FF_T), lambda i, j: (0, j + FF_NB))],
        out_specs=[ospec] * 3, out_shape=[osd] * 3, compiler_params=_cp(("parallel", "parallel")),
    )(h, w_in, w_in)


def ffn_du(dy, w_out_t, ua, ub):
    S = dy.shape[0]

    def body(dy_ref, w_ref, a_ref, b_ref, du_ref):
        j = pl.program_id(1)
        ds = _dot(dy_ref[...], w_ref[...], NN)
        a, b = a_ref[...].astype(f32), b_ref[...].astype(f32)
        sg = _sig(a)
        da = ds * b * sg * (1.0 + a * (1.0 - sg))
        db = ds * a * sg
        du_ref[...] = jnp.where(j < FF_NB, da, db).astype(bf16)

    half = pl.BlockSpec((FF_TM, FF_T), lambda i, j: (i, j % FF_NB))
    return pl.pallas_call(
        body, name="ffn_du", grid=(S // FF_TM, 2 * FF_NB),
        in_specs=[pl.BlockSpec((FF_TM, D), lambda i, j: (i, 0)), pl.BlockSpec((D, FF_T), lambda i, j: (0, j % FF_NB)),
                  half, half],
        out_specs=pl.BlockSpec((FF_TM, FF_T), lambda i, j: (i, j)),
        out_shape=jax.ShapeDtypeStruct((S, 2 * D_FF), bf16), compiler_params=_cp(("parallel", "parallel")),
    )(dy, w_out_t, ua, ub)


TR = 256


def _row_spec(cols=D):
    return pl.BlockSpec((TR, cols), lambda i: (i, 0))


def _vec_spec(cols=D):
    return pl.BlockSpec((1, cols), lambda i: (0, 0))


def prenorm(x, a_vec, sh_vec):
    S = x.shape[0]

    def body(x_ref, a_ref, s_ref, h_ref):
        xv = x_ref[...]
        rstd = lax.rsqrt(jnp.mean(xv * xv, axis=1, keepdims=True) + EPS)
        h_ref[...] = (xv * rstd * a_ref[...] + s_ref[...]).astype(bf16)

    return pl.pallas_call(
        body, name="prenorm", grid=(S // TR,),
        in_specs=[_row_spec(), _vec_spec(), _vec_spec()], out_specs=_row_spec(),
        out_shape=jax.ShapeDtypeStruct((S, D), bf16), compiler_params=_cp(("parallel",)),
    )(x, a_vec, sh_vec)


def postnorm(x, y, b_vec):
    S = x.shape[0]

    def body(x_ref, y_ref, b_ref, o_ref):
        yv = y_ref[...]
        rstd = lax.rsqrt(jnp.mean(yv * yv, axis=1, keepdims=True) + EPS)
        o_ref[...] = x_ref[...] + b_ref[...] * (yv * rstd)

    return pl.pallas_call(
        body, name="postnorm", grid=(S // TR,),
        in_specs=[_row_spec(), _row_spec(), _vec_spec()], out_specs=_row_spec(),
        out_shape=jax.ShapeDtypeStruct((S, D), f32), compiler_params=_cp(("parallel",)),
    )(x, y, b_vec)


def post_bwd(dout, y, b_vec):
    S = dout.shape[0]

    def body(d_ref, y_ref, b_ref, dy_ref, db_ref):
        i = pl.program_id(0)
        yv, dv = y_ref[...], d_ref[...]
        rstd = lax.rsqrt(jnp.mean(yv * yv, axis=1, keepdims=True) + EPS)
        yh = yv * rstd
        dyh = dv * b_ref[...]
        dy_ref[...] = (rstd * (dyh - yh * jnp.mean(dyh * yh, axis=1, keepdims=True))).astype(bf16)
        part = jnp.sum(dv * yh, axis=0, keepdims=True)

        @pl.when(i == 0)
        def _():
            db_ref[...] = part

        @pl.when(i > 0)
        def _():
            db_ref[...] += part

    return pl.pallas_call(
        body, name="post_bwd", grid=(S // TR,),
        in_specs=[_row_spec(), _row_spec(), _vec_spec()], out_specs=[_row_spec(), _vec_spec()],
        out_shape=[jax.ShapeDtypeStruct((S, D), bf16), jax.ShapeDtypeStruct((1, D), f32)],
        compiler_params=_cp(("arbitrary",)),
    )(dout, y, b_vec)


def pre_bwd(dout, dh, x, a_vec):
    S = dout.shape[0]

    def body(d_ref, dh_ref, x_ref, a_ref, dx_ref, ds_ref, da_ref):
        i = pl.program_id(0)
        xv, dhv = x_ref[...], dh_ref[...]
        rstd = lax.rsqrt(jnp.mean(xv * xv, axis=1, keepdims=True) + EPS)
        n1 = xv * rstd
        dn = dhv * a_ref[...]
        dx_ref[...] = d_ref[...] + rstd * (dn - n1 * jnp.mean(dn * n1, axis=1, keepdims=True))
        p_s = jnp.sum(dhv, axis=0, keepdims=True)
        p_a = jnp.sum(dhv * n1, axis=0, keepdims=True)

        @pl.when(i == 0)
        def _():
            ds_ref[...] = p_s
            da_ref[...] = p_a

        @pl.when(i > 0)
        def _():
            ds_ref[...] += p_s
            da_ref[...] += p_a

    return pl.pallas_call(
        body, name="pre_bwd", grid=(S // TR,),
        in_specs=[_row_spec(), _row_spec(), _row_spec(), _vec_spec()],
        out_specs=[_row_spec(), _vec_spec(), _vec_spec()],
        out_shape=[jax.ShapeDtypeStruct((S, D), f32), jax.ShapeDtypeStruct((1, D), f32), jax.ShapeDtypeStruct((1, D), f32)],
        compiler_params=_cp(("arbitrary",)),
    )(dout, dh, x, a_vec)


def loss_grad(y, tgt):
    S = y.shape[0]

    def body(y_ref, t_ref, dy_ref, l_ref):
        i = pl.program_id(0)
        e = y_ref[...] - t_ref[...]
        dy_ref[...] = e * (1.0 / D)
        part = jnp.sum(jnp.sum(e * e, axis=1, keepdims=True), axis=0, keepdims=True) * (0.5 / D)
        part = jnp.broadcast_to(part, (8, LANE))

        @pl.when(i == 0)
        def _():
            l_ref[...] = part

        @pl.when(i > 0)
        def _():
            l_ref[...] += part

    return pl.pallas_call(
        body, name="loss_grad", grid=(S // TR,),
        in_specs=[_row_spec(), _row_spec()],
        out_specs=[_row_spec(), pl.BlockSpec((8, LANE), lambda i: (0, 0))],
        out_shape=[jax.ShapeDtypeStruct((S, D), f32), jax.ShapeDtypeStruct((8, LANE), f32)],
        compiler_params=_cp(("arbitrary",)),
    )(y, tgt)


G_NB = D // LANE
G_OFF = O_GATE // LANE


def gate_merge(u, pa, pb, pc):
    S = u.shape[0]

    def body(g0, g1, g2, a, b, c, o_ref):
        o_ref[...] = (_sig(g0[...]) * a[...] + _sig(g1[...]) * b[...] + _sig(g2[...]) * c[...]).astype(bf16)

    gs = [pl.BlockSpec((512, LANE), functools.partial(lambda i, j, k: (i, G_OFF + G_NB * k + j), k=k)) for k in range(3)]
    ps = pl.BlockSpec((512, LANE), lambda i, j: (i, j))
    return pl.pallas_call(
        body, name="gate_merge", grid=(S // 512, G_NB),
        in_specs=gs + [ps, ps, ps], out_specs=ps,
        out_shape=jax.ShapeDtypeStruct((S, D), bf16), compiler_params=_cp(("parallel", "parallel")),
    )(u, u, u, pa, pb, pc)


def gate_bwd(dm, u, pa, pb, pc):
    S = u.shape[0]

    def body(dm_ref, g0, g1, g2, a, b, c, da, db, dc, dg0, dg1, dg2):
        d = dm_ref[...]
        for g, p, dp, dg in ((g0, a, da, dg0), (g1, b, db, dg1), (g2, c, dc, dg2)):
            s = _sig(g[...])
            dp[...] = (d * s).astype(bf16)
            dg[...] = (d * p[...] * s * (1.0 - s)).astype(bf16)

    gs = [pl.BlockSpec((512, LANE), functools.partial(lambda i, j, k: (i, G_OFF + G_NB * k + j), k=k)) for k in range(3)]
    ps = pl.BlockSpec((512, LANE), lambda i, j: (i, j))
    osd = jax.ShapeDtypeStruct((S, D), bf16)
    return pl.pallas_call(
        body, name="gate_bwd", grid=(S // 512, G_NB),
        in_specs=[ps] + gs + [ps, ps, ps], out_specs=[ps] * 6, out_shape=[osd] * 6,
        compiler_params=_cp(("parallel", "parallel")),
    )(dm, u, u, u, pa, pb, pc)


A_TB = 512
A_NCH = A_TB // A_CHUNK
A_NSUB = A_CHUNK // A_SUB


def _hgrn_gates(qr, fr, lbh):
    sq = _sig(qr)
    sig = _sig(fr)
    f = lbh + (1.0 - lbh) * sig
    logf = jnp.log(jnp.maximum(f, TINY))
    k = (1.0 - lbh) * (1.0 - sig)
    return qr * sq, sq, sig, f, logf, k


def _hgrn_intra(qf, k, b, causal):
    qts, kts, eqs, eks, blocks = [], [], [], [], []
    for sb in range(A_NSUB):
        rs = sb * A_SUB
        r = b[rs - 1:rs, :] if sb else jnp.zeros((1, A_K), f32)
        eq = jnp.exp(b[rs:rs + A_SUB, :] - r)
        ek = jnp.exp(jnp.minimum(r - b, A_CLAMP))
        qt = (qf[rs:rs + A_SUB, :] * eq).astype(bf16)
        kt = (k * ek).astype(bf16)
        blocks.append(_dot(qt, kt, NT))
        qts.append(qt), kts.append(kt), eqs.append(eq), eks.append(ek)
    a = jnp.where(causal, jnp.concatenate(blocks, axis=0), 0.0)
    return a, qts, kts, eqs, eks


def _tri():
    r = lax.broadcasted_iota(jnp.int32, (A_CHUNK, A_CHUNK), 0)
    c = lax.broadcasted_iota(jnp.int32, (A_CHUNK, A_CHUNK), 1)
    return r >= c


def _hgrn_in_specs(rev_nb=None):
    def im(col):
        if rev_nb is None:
            return lambda p, i: (i, col + p)
        return lambda p, i: (rev_nb - 1 - i, col + p)
    return [pl.BlockSpec((A_TB, 256), im(O_AQ // 256)), pl.BlockSpec((A_TB, 256), im(O_AF // 256)),
            pl.BlockSpec((A_TB, LANE), im(O_AI // LANE)), pl.BlockSpec((A_TB, LANE), im(O_AG // LANE)),
            pl.BlockSpec((1, 256), lambda p, i: (0, p)), pl.BlockSpec((1, A_V), lambda p, i: (0, 0))]


def hgrn_fwd(u, lb, ng):
    S = u.shape[0]
    nb = S // A_TB

    def body(q_ref, f_ref, i_ref, g_ref, lb_ref, ng_ref, o_ref, ya_ref, st_ref, state):
        @pl.when(pl.program_id(1) == 0)
        def _():
            state[...] = jnp.zeros_like(state)

        causal = _tri()
        tri = causal.astype(f32)

        def chunk(n, carry):
            rows = pl.ds(pl.multiple_of(n * A_CHUNK, A_CHUNK), A_CHUNK)
            o_parts, y_parts = [], []
            for hh in range(2):
                ks = slice(hh * A_K, (hh + 1) * A_K)
                vs = slice(hh * A_V, (hh + 1) * A_V)
                qf, _, _, _, logf, k = _hgrn_gates(q_ref[rows, ks], f_ref[rows, ks], lb_ref[:, ks])
                vi = i_ref[rows, vs].astype(bf16)
                gg = g_ref[rows, vs]
                b = _dot(tri, logf, NN, precision=lax.Precision.HIGHEST)
                s0 = state[hh]
                st_ref[n, hh] = s0
                o = _dot((qf * jnp.exp(b)).astype(bf16), s0.astype(bf16), NT)
                a, _, _, _, _ = _hgrn_intra(qf, k, b, causal)
                o = o + _dot(a.astype(bf16), vi, NN)
                bend = b[A_CHUNK - 1:A_CHUNK, :]
                ke = (k * jnp.exp(bend - b)).astype(bf16)
                state[hh] = s0 * jnp.exp(bend) + _dot(vi, ke, TN)
                rstd = lax.rsqrt(jnp.mean(o * o, axis=1, keepdims=True) + EPS)
                o_parts.append(o)
                y_parts.append(o * rstd * ng_ref[...] * (gg * _sig(gg)))
            o_ref[rows, :] = jnp.concatenate(o_parts, axis=1)
            ya_ref[rows, :] = jnp.concatenate(y_parts, axis=1).astype(bf16)
            return carry

        lax.fori_loop(0, A_NCH, chunk, 0)

    return pl.pallas_call(
        body, name="hgrn_fwd", grid=(3, nb),
        in_specs=_hgrn_in_specs(),
        out_specs=[pl.BlockSpec((A_TB, LANE), lambda p, i: (i, p)), pl.BlockSpec((A_TB, LANE), lambda p, i: (i, p)),
                   pl.BlockSpec((A_NCH, 2, A_V, A_K), lambda p, i: (i, p, 0, 0))],
        out_shape=[jax.ShapeDtypeStruct((S, 384), f32), jax.ShapeDtypeStruct((S, 384), bf16),
                   jax.ShapeDtypeStruct((S // A_CHUNK, A_HEADS, A_V, A_K), f32)],
        scratch_shapes=[pltpu.VMEM((2, A_V, A_K), f32)],
        compiler_params=_cp(("parallel", "arbitrary")),
    )(u, u, u, u, lb, ng)


def hgrn_bwd(u, lb, ng, o, st, dya):
    S = u.shape[0]
    nb = S // A_TB

    def body(q_ref, f_ref, i_ref, g_ref, lb_ref, ng_ref, o_ref, st_ref, dy_ref,
             dq_ref, df_ref, di_ref, dg_ref, dlb_ref, dng_ref, dstate):
        @pl.when(pl.program_id(1) == 0)
        def _():
            dstate[...] = jnp.zeros_like(dstate)
            dlb_ref[...] = jnp.zeros_like(dlb_ref)
            dng_ref[...] = jnp.zeros_like(dng_ref)

        causal = _tri()
        tri = causal.astype(f32)

        def chunk(it, carry):
            n = A_NCH - 1 - it
            rows = pl.ds(pl.multiple_of(n * A_CHUNK, A_CHUNK), A_CHUNK)
            dq_p, df_p, di_p, dg_p, dlb_p = [], [], [], [], []
            dng_acc = jnp.zeros((1, A_V), f32)
            for hh in range(2):
                ks = slice(hh * A_K, (hh + 1) * A_K)
                vs = slice(hh * A_V, (hh + 1) * A_V)
                lbh = lb_ref[:, ks]
                qr = q_ref[rows, ks]
                qf, sq, sig, f, logf, k = _hgrn_gates(qr, f_ref[rows, ks], lbh)
                vi = i_ref[rows, vs].astype(bf16)
                gg = g_ref[rows, vs]
                b = _dot(tri, logf, NN, precision=lax.Precision.HIGHEST)
                eb = jnp.exp(b)
                bend = b[A_CHUNK - 1:A_CHUNK, :]
                eend = jnp.exp(bend)
                ekend = jnp.exp(bend - b)
                qe = (qf * eb).astype(bf16)
                ke = (k * ekend).astype(bf16)
                s0 = st_ref[n, hh]
                dsend = dstate[hh]
                ov = o_ref[rows, vs]
                dy = dy_ref[rows, vs]
                rstd = lax.rsqrt(jnp.mean(ov * ov, axis=1, keepdims=True) + EPS)
                oh = ov * rstd
                sg = _sig(gg)
                d_on = dy * (gg * sg)
                dg_p.append(dy * oh * ng_ref[...] * (sg * (1.0 + gg * (1.0 - sg))))
                dng_acc = dng_acc + jnp.sum(d_on * oh, axis=0, keepdims=True)
                doh = d_on * ng_ref[...]
                do = (rstd * (doh - oh * jnp.mean(doh * oh, axis=1, keepdims=True))).astype(bf16)
                a, qts, kts, eqs, eks = _hgrn_intra(qf, k, b, causal)
                da = jnp.where(causal, _dot(do, vi, NT), 0.0).astype(bf16)
                dsb = dsend.astype(bf16)
                dv = _dot(a.astype(bf16), do, TN) + _dot(ke, dsb, NT)
                dq = _dot(do, s0.astype(bf16), NN) * eb
                dk_state = _dot(vi, dsb, NN) * ekend
                dk = dk_state
                dq_i = []
                for sb in range(A_NSUB):
                    da_sb = da[sb * A_SUB:(sb + 1) * A_SUB, :]
                    dq_i.append(_dot(da_sb, kts[sb], NN) * eqs[sb])
                    dk = dk + _dot(da_sb, qts[sb], TN) * eks[sb]
                dq = dq + jnp.concatenate(dq_i, axis=0)
                db = qf * dq - k * dk
                extra = jnp.sum(k * dk_state, axis=0, keepdims=True) + eend * jnp.sum(s0 * dsend, axis=0, keepdims=True)
                dlogf = _dot(tri, db, TN, precision=lax.Precision.HIGHEST) + extra
                dstate[hh] = _dot(do, qe, TN) + eend * dsend
                d_pre = jnp.where(f > TINY, dlogf / f, 0.0) - dk
                dlb_p.append(jnp.sum((1.0 - sig) * d_pre, axis=0, keepdims=True))
                df_p.append((1.0 - lbh) * d_pre * sig * (1.0 - sig))
                dq_p.append(dq * (sq * (1.0 + qr * (1.0 - sq))))
                di_p.append(dv)
            dq_ref[rows, :] = jnp.concatenate(dq_p, axis=1).astype(bf16)
            df_ref[rows, :] = jnp.concatenate(df_p, axis=1).astype(bf16)
            di_ref[rows, :] = jnp.concatenate(di_p, axis=1).astype(bf16)
            dg_ref[rows, :] = jnp.concatenate(dg_p, axis=1).astype(bf16)
            dlb_ref[...] += jnp.concatenate(dlb_p, axis=1)
            dng_ref[0] += dng_acc
            return carry

        lax.fori_loop(0, A_NCH, chunk, 0)

    rev = lambda p, i: (nb - 1 - i, p)
    return pl.pallas_call(
        body, name="hgrn_bwd", grid=(3, nb),
        in_specs=_hgrn_in_specs(nb) + [pl.BlockSpec((A_TB, LANE), rev),
                                       pl.BlockSpec((A_NCH, 2, A_V, A_K), lambda p, i: (nb - 1 - i, p, 0, 0)),
                                       pl.BlockSpec((A_TB, LANE), rev)],
        out_specs=[pl.BlockSpec((A_TB, 256), rev), pl.BlockSpec((A_TB, 256), rev),
                   pl.BlockSpec((A_TB, LANE), rev), pl.BlockSpec((A_TB, LANE), rev),
                   pl.BlockSpec((1, 256), lambda p, i: (0, p)), pl.BlockSpec((1, 1, A_V), lambda p, i: (p, 0, 0))],
        out_shape=[jax.ShapeDtypeStruct((S, 768), bf16), jax.ShapeDtypeStruct((S, 768), bf16),
                   jax.ShapeDtypeStruct((S, 384), bf16), jax.ShapeDtypeStruct((S, 384), bf16),
                   jax.ShapeDtypeStruct((1, 768), f32), jax.ShapeDtypeStruct((3, 1, A_V), f32)],
        scratch_shapes=[pltpu.VMEM((2, A_V, A_K), f32)],
        compiler_params=_cp(("parallel", "arbitrary")),
    )(u, u, u, u, lb, ng, o, st, dya)


B_TK = 128
SCALE = HD ** -0.5


def _split(x):
    hi = x.astype(bf16)
    return hi, (x - hi.astype(f32)).astype(bf16)


def _dot2(x, m, dn):
    hi, lo = _split(x)
    return _dot(hi, m, dn) + _dot(lo, m, dn)


def _sb_block(qs, kh, mask, m_gt, c):
    z = _dot(qs, kh, NT)
    sp = jnp.maximum(z, 0.0) + jnp.log(1.0 + jnp.exp(-jnp.abs(z)))
    lneg = jnp.where(mask, -sp, 0.0)
    lsz = z - sp
    suf = _dot2(lneg, m_gt, NN) + c
    a = jnp.where(mask, jnp.exp(lsz + suf), 0.0)
    return lneg, lsz, a


def _sb_masks(tq, i, jj):
    t_idx = i * tq + lax.broadcasted_iota(jnp.int32, (tq, B_TK), 0)
    s_idx = jj * B_TK + lax.broadcasted_iota(jnp.int32, (tq, B_TK), 1)
    return s_idx < t_idx


def _sb_tri(strict):
    r = lax.broadcasted_iota(jnp.int32, (B_TK, B_TK), 0)
    c = lax.broadcasted_iota(jnp.int32, (B_TK, B_TK), 1)
    return (r > c if strict else r >= c).astype(bf16)


B_DEAD = -110.0


def _sb_walk(nkb, step, init):
    def cond(state):
        it, alive, _ = state
        return jnp.logical_and(it < nkb, alive)

    def body(state):
        it, _, carry = state
        carry = step(it, carry)
        top = jnp.maximum(jnp.max(carry[0][1]), jnp.max(carry[1][1]))
        return it + 1, top > B_DEAD, carry

    return lax.while_loop(cond, body, (jnp.int32(0), jnp.bool_(True), init))[2]


def sb_fwd(u):
    S = u.shape[0]
    tq = 256

    def body(q_ref, k_ref, v_ref, o_ref):
        i = pl.program_id(1)
        nkb = (i + 1) * (tq // B_TK)
        m_gt = _sb_tri(True)
        qs = [(q_ref[:, hh * HD:(hh + 1) * HD] * SCALE).astype(bf16) for hh in range(2)]

        def step(it, carry):
            jj = nkb - 1 - it
            rows = pl.ds(pl.multiple_of(jj * B_TK, B_TK), B_TK)
            mask = _sb_masks(tq, i, jj)
            kb, vb = k_ref[rows, :], v_ref[rows, :]
            out = []
            for hh in range(2):
                acc, c = carry[hh]
                kh = kb[:, hh * HD:(hh + 1) * HD].astype(bf16)
                vh = vb[:, hh * HD:(hh + 1) * HD].astype(bf16)
                lneg, _, a = _sb_block(qs[hh], kh, mask, m_gt, c)
                out.append((acc + _dot2(a, vh, NN), c + jnp.sum(lneg, axis=1, keepdims=True)))
            return tuple(out)

        z0 = (jnp.zeros((tq, HD), f32), jnp.zeros((tq, 1), f32))
        res = _sb_walk(nkb, step, (z0, z0))
        o_ref[...] = jnp.concatenate([res[0][0], res[1][0]], axis=1)

    return pl.pallas_call(
        body, name="sb_fwd", grid=(3, S // tq),
        in_specs=[pl.BlockSpec((tq, LANE), lambda p, i: (i, O_BQ // LANE + p)),
                  pl.BlockSpec((S, LANE), lambda p, i: (0, O_BK // LANE + p)),
                  pl.BlockSpec((S, LANE), lambda p, i: (0, O_BV // LANE + p))],
        out_specs=pl.BlockSpec((tq, LANE), lambda p, i: (i, p)),
        out_shape=jax.ShapeDtypeStruct((S, 384), f32),
        compiler_params=_cp(("parallel", "arbitrary")),
    )(u, u, u)


def sb_bwd(u, yb, dyb):
    S = u.shape[0]
    tq = 128

    def body(q_ref, k_ref, v_ref, y_ref, dy_ref, dq_ref, dk_ref, dv_ref):
        i = pl.program_id(1)

        @pl.when(i == 0)
        def _():
            dk_ref[...] = jnp.zeros_like(dk_ref)
            dv_ref[...] = jnp.zeros_like(dv_ref)

        nkb = (i + 1) * (tq // B_TK)
        m_gt = _sb_tri(True)
        m_ge = _sb_tri(False)
        qs, dos, tot = [], [], []
        for hh in range(2):
            hs = slice(hh * HD, (hh + 1) * HD)
            qs.append((q_ref[:, hs] * SCALE).astype(bf16))
            dob = dy_ref[:, hs].astype(bf16)
            dos.append(dob)
            tot.append(jnp.sum(dob.astype(f32) * y_ref[:, hs], axis=1, keepdims=True))

        def step(it, carry):
            jj = nkb - 1 - it
            rows = pl.ds(pl.multiple_of(jj * B_TK, B_TK), B_TK)
            mask = _sb_masks(tq, i, jj)
            kb, vb = k_ref[rows, :], v_ref[rows, :]
            out, dk_p, dv_p = [], [], []
            for hh in range(2):
                dq, c, cg = carry[hh]
                kh = kb[:, hh * HD:(hh + 1) * HD].astype(bf16)
                vh = vb[:, hh * HD:(hh + 1) * HD].astype(bf16)
                lneg, lsz, a = _sb_block(qs[hh], kh, mask, m_gt, c)
                g = a * _dot(dos[hh], vh, NT)
                p = tot[hh] - cg - _dot2(g, m_ge, NN)
                beta = jnp.exp(lsz)
                dz = jnp.where(mask, g * (1.0 - beta) - beta * p, 0.0).astype(bf16)
                dk_p.append(_dot(dz, qs[hh], TN))
                dv_p.append(_dot(a.astype(bf16), dos[hh], TN))
                out.append((dq + _dot(dz, kh, NN), c + jnp.sum(lneg, axis=1, keepdims=True),
                            cg + jnp.sum(g, axis=1, keepdims=True)))
            dk_ref[rows, :] += jnp.concatenate(dk_p, axis=1)
            dv_ref[rows, :] += jnp.concatenate(dv_p, axis=1)
            return tuple(out)

        z0 = (jnp.zeros((tq, HD), f32), jnp.zeros((tq, 1), f32), jnp.zeros((tq, 1), f32))
        res = _sb_walk(nkb, step, (z0, z0))
        dq_ref[...] = jnp.concatenate([res[0][0], res[1][0]], axis=1) * SCALE

    row = pl.BlockSpec((tq, LANE), lambda p, i: (i, p))
    full = pl.BlockSpec((S, LANE), lambda p, i: (0, p))
    osd = jax.ShapeDtypeStruct((S, 384), f32)
    return pl.pallas_call(
        body, name="sb_bwd", grid=(3, S // tq),
        in_specs=[pl.BlockSpec((tq, LANE), lambda p, i: (i, O_BQ // LANE + p)),
                  pl.BlockSpec((S, LANE), lambda p, i: (0, O_BK // LANE + p)),
                  pl.BlockSpec((S, LANE), lambda p, i: (0, O_BV // LANE + p)), row, row],
        out_specs=[row, full, full], out_shape=[osd, osd, osd],
        compiler_params=_cp(("parallel", "arbitrary")),
    )(u, u, u, yb, dyb)


def _dil_rows(i, rho, r):
    if r == 1:
        return pl.ds(pl.multiple_of(i * C_BLK, C_BLK), C_BLK)
    return pl.ds(i * (C_BLK * r) + rho, C_BLK, stride=r)


def _dil_scores(qs, kc, kp, i, slope_r):
    qi = lax.broadcasted_iota(jnp.int32, (C_BLK, C_BLK), 0)
    kj = lax.broadcasted_iota(jnp.int32, (C_BLK, C_BLK), 1)
    d_c = qi - kj
    d_p = d_c + C_BLK
    ok_c = d_c >= 0
    ok_p = jnp.logical_and(d_c <= 0, i > 0)
    s_c = jnp.where(ok_c, _dot(qs, kc, NT) - slope_r * d_c.astype(f32), NEG_BIG)
    s_p = jnp.where(ok_p, _dot(qs, kp, NT) - slope_r * d_p.astype(f32), NEG_BIG)
    return s_c, s_p, ok_c, ok_p


def _dil_slope(g, r, hh):
    pair = pl.program_id(0)
    return jnp.where(pair == 0, C_SLOPES[4 * g + hh] * r, C_SLOPES[4 * g + 2 + hh] * r).astype(f32)


def _dil_u_specs(g, S):
    def im(off):
        return lambda p, rho: (0, (off + g * 256) // LANE + p)
    return [pl.BlockSpec((S, LANE), im(O_CQ)), pl.BlockSpec((S, LANE), im(O_CK)), pl.BlockSpec((S, LANE), im(O_CV))]


def dil_fwd(u, g):
    S = u.shape[0]
    r = C_GROUPS[g][1]
    nbk = S // r // C_BLK

    def body(q_ref, k_ref, v_ref, o_ref, l_ref):
        rho = pl.program_id(1)

        def step(i, carry):
            rc = _dil_rows(i, rho, r)
            rp = _dil_rows(jnp.maximum(i - 1, 0), rho, r)
            q2, kc2, kp2, vc2, vp2 = q_ref[rc, :], k_ref[rc, :], k_ref[rp, :], v_ref[rc, :], v_ref[rp, :]
            o_p, l_p = [], []
            for hh in range(2):
                hs = slice(hh * HD, (hh + 1) * HD)
                qs = (q2[:, hs] * SCALE).astype(bf16)
                kc, kp = kc2[:, hs].astype(bf16), kp2[:, hs].astype(bf16)
                vc, vp = vc2[:, hs].astype(bf16), vp2[:, hs].astype(bf16)
                s_c, s_p, _, _ = _dil_scores(qs, kc, kp, i, _dil_slope(g, r, hh))
                m = jnp.maximum(jnp.max(s_c, axis=1, keepdims=True), jnp.max(s_p, axis=1, keepdims=True))
                p_c, p_p = jnp.exp(s_c - m), jnp.exp(s_p - m)
                den = jnp.sum(p_c, axis=1, keepdims=True) + jnp.sum(p_p, axis=1, keepdims=True)
                o_p.append((_dot(p_c.astype(bf16), vc, NN) + _dot(p_p.astype(bf16), vp, NN)) / den)
                l_p.append(jnp.broadcast_to(m + jnp.log(den), (C_BLK, HD)))
            o_ref[rc, :] = jnp.concatenate(o_p, axis=1)
            l_ref[rc, :] = jnp.concatenate(l_p, axis=1)
            return carry

        lax.fori_loop(0, nbk, step, 0)

    ospec = pl.BlockSpec((S, LANE), lambda p, rho: (0, p))
    osd = jax.ShapeDtypeStruct((S, 256), f32)
    return pl.pallas_call(
        body, name=f"dil_fwd{g}", grid=(2, r),
        in_specs=_dil_u_specs(g, S), out_specs=[ospec, ospec], out_shape=[osd, osd],
        compiler_params=_cp(("parallel", "arbitrary")),
    )(u, u, u)


def dil_merge(os_, ls_):
    S = os_[0].shape[0]

    def body(o0, o1, o2, l0, l1, l2, y_ref, lse_ref):
        a, b, c = l0[...], l1[...], l2[...]
        m = jnp.maximum(jnp.maximum(a, b), c)
        ea, eb, ec = jnp.exp(a - m), jnp.exp(b - m), jnp.exp(c - m)
        den = ea + eb + ec
        y_ref[...] = (ea * o0[...] + eb * o1[...] + ec * o2[...]) / den
        lse_ref[...] = m + jnp.log(den)

    spec = pl.BlockSpec((512, 256), lambda i: (i, 0))
    osd = jax.ShapeDtypeStruct((S, 256), f32)
    return pl.pallas_call(
        body, name="dil_merge", grid=(S // 512,), in_specs=[spec] * 6, out_specs=[spec, spec],
        out_shape=[osd, osd], compiler_params=_cp(("parallel",)),
    )(*os_, *ls_)


def dil_bwd(u, g, dyc, yc, lse):
    S = u.shape[0]
    r = C_GROUPS[g][1]
    nbk = S // r // C_BLK

    def body(q_ref, k_ref, v_ref, dy_ref, y_ref, l_ref, dq_ref, dk_ref, dv_ref):
        rho = pl.program_id(1)

        @pl.when(rho == 0)
        def _():
            dk_ref[...] = jnp.zeros_like(dk_ref)
            dv_ref[...] = jnp.zeros_like(dv_ref)

        def step(i, carry):
            rc = _dil_rows(i, rho, r)
            rp = _dil_rows(jnp.maximum(i - 1, 0), rho, r)
            q2, kc2, kp2, vc2, vp2 = q_ref[rc, :], k_ref[rc, :], k_ref[rp, :], v_ref[rc, :], v_ref[rp, :]
            dy2, y2, l2 = dy_ref[rc, :], y_ref[rc, :], l_ref[rc, :]
            dq_p, dkc_p, dkp_p, dvc_p, dvp_p = [], [], [], [], []
            for hh in range(2):
                hs = slice(hh * HD, (hh + 1) * HD)
                qs = (q2[:, hs] * SCALE).astype(bf16)
                kc, kp = kc2[:, hs].astype(bf16), kp2[:, hs].astype(bf16)
                vc, vp = vc2[:, hs].astype(bf16), vp2[:, hs].astype(bf16)
                dy = dy2[:, hs]
                dyb = dy.astype(bf16)
                s_c, s_p, ok_c, ok_p = _dil_scores(qs, kc, kp, i, _dil_slope(g, r, hh))
                lrow = l2[:, hh * HD:hh * HD + 1]
                delta = jnp.sum(dy * y2[:, hs], axis=1, keepdims=True)
                pi_c = jnp.where(ok_c, jnp.exp(s_c - lrow), 0.0)
                pi_p = jnp.where(ok_p, jnp.exp(s_p - lrow), 0.0)
                ds_c = (pi_c * (_dot(dyb, vc, NT) - delta)).astype(bf16)
                ds_p = (pi_p * (_dot(dyb, vp, NT) - delta)).astype(bf16)
                dq_p.append((_dot(ds_c, kc, NN) + _dot(ds_p, kp, NN)) * SCALE)
                dkc_p.append(_dot(ds_c, qs, TN))
                dkp_p.append(_dot(ds_p, qs, TN))
                dvc_p.append(_dot(pi_c.astype(bf16), dyb, TN))
                dvp_p.append(_dot(pi_p.astype(bf16), dyb, TN))
            dq_ref[rc, :] = jnp.concatenate(dq_p, axis=1)
            dk_ref[rc, :] += jnp.concatenate(dkc_p, axis=1)
            dv_ref[rc, :] += jnp.concatenate(dvc_p, axis=1)
            dk_ref[rp, :] += jnp.concatenate(dkp_p, axis=1)
            dv_ref[rp, :] += jnp.concatenate(dvp_p, axis=1)
            return carry

        lax.fori_loop(0, nbk, step, 0)

    ospec = pl.BlockSpec((S, LANE), lambda p, rho: (0, p))
    osd = jax.ShapeDtypeStruct((S, 256), f32)
    return pl.pallas_call(
        body, name=f"dil_bwd{g}", grid=(2, r),
        in_specs=_dil_u_specs(g, S) + [ospec, ospec, ospec], out_specs=[ospec] * 3, out_shape=[osd] * 3,
        compiler_params=_cp(("parallel", "arbitrary")),
    )(u, u, u, dyc, yc, lse)


def _rows_tile(rows):
    return _tile(rows, (256, 176, 128, 64, 32, 16, 8))


def cast_bf16(w):
    shape = w.shape
    w2 = w.reshape(-1, shape[-1])
    rows, cols = w2.shape
    tr = _rows_tile(rows)

    def body(x_ref, o_ref):
        o_ref[...] = x_ref[...].astype(bf16)

    spec = pl.BlockSpec((tr, cols), lambda i: (i, 0))
    out = pl.pallas_call(
        body, name="cast_bf16", grid=(rows // tr,), in_specs=[spec], out_specs=spec,
        out_shape=jax.ShapeDtypeStruct((rows, cols), bf16), compiler_params=_cp(("parallel",)),
    )(w2)
    return out.reshape(shape)


BC1 = 1.0 - ADAM_B1 ** ADAM_STEP
BC2 = 1.0 - ADAM_B2 ** ADAM_STEP


def _adam_math(w, g, m, v):
    m2 = ADAM_B1 * m + (1.0 - ADAM_B1) * g
    v2 = ADAM_B2 * v + (1.0 - ADAM_B2) * (g * g)
    delta = -ADAM_LR * ((m2 / BC1) / (jnp.sqrt(v2 / BC2) + ADAM_EPS) + ADAM_WD * w)
    return delta, m2, v2


def adam(w, g, m, v):
    shape = w.shape
    r2 = lambda t: t.reshape(-1, shape[-1])
    rows, cols = r2(w).shape
    tr = _rows_tile(rows)

    def body(w_ref, g_ref, m_ref, v_ref, d_ref, m2_ref, v2_ref):
        d_ref[...], m2_ref[...], v2_ref[...] = _adam_math(w_ref[...], g_ref[...], m_ref[...], v_ref[...])

    spec = pl.BlockSpec((tr, cols), lambda i: (i, 0))
    osd = jax.ShapeDtypeStruct((rows, cols), f32)
    outs = pl.pallas_call(
        body, name="adam", grid=(rows // tr,), in_specs=[spec] * 4, out_specs=[spec] * 3, out_shape=[osd] * 3,
        compiler_params=_cp(("parallel",)),
    )(r2(w), r2(g), r2(m), r2(v))
    return [o.reshape(shape) for o in outs]


ADA_N = 9 * D // 4
ADA_TN = 384


def ada_fwd(c_all, w_ada):
    def body(c_ref, w_ref, o_ref):
        cv = c_ref[...]
        o_ref[0] = _dot((cv * _sig(cv)).astype(bf16), w_ref[0].astype(bf16), NN)

    return pl.pallas_call(
        body, name="ada_fwd", grid=(DEPTH, ADA_N // ADA_TN),
        in_specs=[pl.BlockSpec((8, D), lambda l, j: (0, 0)), pl.BlockSpec((1, D, ADA_TN), lambda l, j: (l, 0, j))],
        out_specs=pl.BlockSpec((1, 8, ADA_TN), lambda l, j: (l, 0, j)),
        out_shape=jax.ShapeDtypeStruct((DEPTH, 8, ADA_N), f32), compiler_params=_cp(("parallel", "parallel")),
    )(c_all, w_ada)


def ada_bwd_adam(c_all, dm, w, m, v):
    tr = 128

    def body(c_ref, dm_ref, w_ref, m_ref, v_ref, g_ref, d_ref, m2_ref, v2_ref):
        cv = c_ref[...]
        g = _dot((cv * _sig(cv)).astype(bf16), dm_ref[0].astype(bf16), TN)
        g_ref[0] = g
        d_ref[0], m2_ref[0], v2_ref[0] = _adam_math(w_ref[0], g, m_ref[0], v_ref[0])

    wspec = pl.BlockSpec((1, tr, ADA_N), lambda l, i: (l, i, 0))
    osd = jax.ShapeDtypeStruct((DEPTH, D, ADA_N), f32)
    return pl.pallas_call(
        body, name="ada_bwd_adam", grid=(DEPTH, D // tr),
        in_specs=[pl.BlockSpec((8, tr), lambda l, i: (0, i)), pl.BlockSpec((1, 8, ADA_N), lambda l, i: (l, 0, 0)),
                  wspec, wspec, wspec],
        out_specs=[wspec] * 4, out_shape=[osd] * 4, compiler_params=_cp(("parallel", "parallel")),
    )(c_all, dm, w, m, v)


def _lb_probs(x):
    mx = jnp.max(x, axis=0, keepdims=True)
    e = jnp.exp(x - mx)
    return e / jnp.sum(e, axis=0, keepdims=True)


def lb_fwd(logits):
    def body(x_ref, o_ref):
        p = _lb_probs(x_ref[...])
        rows = [jnp.zeros((1, 768), f32)]
        for l in range(1, DEPTH):
            rows.append(rows[-1] + p[l:l + 1, :])
        o_ref[...] = jnp.concatenate(rows, axis=0)

    return pl.pallas_call(body, name="lb_fwd", out_shape=jax.ShapeDtypeStruct((DEPTH, 768), f32))(logits)


def lb_bwd(logits, dlb):
    def body(x_ref, d_ref, o_ref):
        p = _lb_probs(x_ref[...])
        d = d_ref[...]
        rows = [jnp.zeros((1, 768), f32)] * DEPTH
        acc = jnp.zeros((1, 768), f32)
        for l in range(DEPTH - 1, 0, -1):
            acc = acc + d[l:l + 1, :]
            rows[l] = acc
        dp = jnp.concatenate(rows, axis=0)
        o_ref[...] = p * (dp - jnp.sum(p * dp, axis=0, keepdims=True))

    return pl.pallas_call(body, name="lb_bwd", out_shape=jax.ShapeDtypeStruct((DEPTH, 768), f32))(logits, dlb)


def sum_slots(x):
    n, rows, cols = x.shape
    tr = _rows_tile(rows)

    def body(x_ref, o_ref):
        acc = x_ref[0]
        for j in range(1, n):
            acc = acc + x_ref[j]
        o_ref[...] = acc

    return pl.pallas_call(
        body, name="sum_slots", grid=(rows // tr,),
        in_specs=[pl.BlockSpec((n, tr, cols), lambda i: (0, i, 0))], out_specs=pl.BlockSpec((tr, cols), lambda i: (i, 0)),
        out_shape=jax.ShapeDtypeStruct((rows, cols), f32), compiler_params=_cp(("parallel",)),
    )(x)


ANY = pl.BlockSpec(memory_space=pl.ANY)
CHIP_FLIPS = ((1, 0), (0, 1), (1, 1))
DEV_FLIPS = tuple((a, b, d) for a in (0, 1) for b in (0, 1) for d in (0, 1))[1:]


def _me():
    return lax.axis_index("x"), lax.axis_index("y"), lax.axis_index("c")


def _flip(v, f):
    return 1 - v if f else v


def _comm_call(body, name, ins, out_shapes, n_remote, n_local):
    return pl.pallas_call(
        body, name=name, in_specs=[ANY] * len(ins), out_specs=[ANY] * len(out_shapes), out_shape=out_shapes,
        scratch_shapes=[pltpu.SemaphoreType.DMA((n_remote,)), pltpu.SemaphoreType.DMA((n_remote,)),
                        pltpu.SemaphoreType.DMA((max(n_local, 1),))],
    )(*ins)


def all_gather_chips(arrs, layer=None, name="ag4"):
    n = len(arrs)
    shapes = [a.shape if layer is None else a.shape[1:] for a in arrs]

    def body(*refs):
        ins, outs, (send, recv, loc) = refs[:n], refs[n:2 * n], refs[2 * n:]
        x, y, c = _me()
        mine = 2 * x + y
        srcs = [r if layer is None else r.at[layer] for r in ins]
        locs = [pltpu.make_async_copy(srcs[a], outs[a].at[mine], loc.at[a]) for a in range(n)]
        for cp in locs:
            cp.start()

        def remote(a, k, slot):
            fx, fy = CHIP_FLIPS[k]
            return pltpu.make_async_remote_copy(srcs[a], outs[a].at[slot], send.at[3 * a + k], recv.at[3 * a + k],
                                                device_id=(_flip(x, fx), _flip(y, fy), c), device_id_type=MESH)

        for a in range(n):
            for k in range(3):
                remote(a, k, mine).start()
        for a in range(n):
            for k, (fx, fy) in enumerate(CHIP_FLIPS):
                cp = remote(a, k, 2 * _flip(x, fx) + _flip(y, fy))
                cp.wait_recv()
                cp.wait_send()
        for cp in locs:
            cp.wait()

    outs = [jax.ShapeDtypeStruct((4,) + tuple(s), a.dtype) for s, a in zip(shapes, arrs)]
    return _comm_call(body, name, arrs, outs, 3 * n, n)


def all_gather_devs(arr, name="ag8"):
    def body(in_ref, out_ref, send, recv, loc):
        x, y, c = _me()
        mine = 4 * x + 2 * y + c
        lc = pltpu.make_async_copy(in_ref, out_ref.at[mine], loc.at[0])
        lc.start()

        def remote(k, slot):
            fx, fy, fc = DEV_FLIPS[k]
            return pltpu.make_async_remote_copy(in_ref, out_ref.at[slot], send.at[k], recv.at[k],
                                                device_id=(_flip(x, fx), _flip(y, fy), _flip(c, fc)), device_id_type=MESH)

        for k in range(7):
            remote(k, mine).start()
        for k, (fx, fy, fc) in enumerate(DEV_FLIPS):
            cp = remote(k, 4 * _flip(x, fx) + 2 * _flip(y, fy) + _flip(c, fc))
            cp.wait_recv()
            cp.wait_send()
        lc.wait()

    return _comm_call(body, name, [arr], [jax.ShapeDtypeStruct((8,) + arr.shape, arr.dtype)], 7, 1)[0]


def _rows_of(which, rows):
    return pl.ds(pl.multiple_of(which * rows, 16), rows)


def dev_exchange(parts, name="rs_x8"):
    n = len(parts)

    def body(*refs):
        ins, outs, (send, recv, loc) = refs[:n], refs[n:2 * n], refs[2 * n:]
        x, y, c = _me()
        mine = 4 * x + 2 * y + c

        def piece(a, px, py, pc):
            rows = ins[a].shape[1] // 2
            return ins[a].at[2 * px + py, _rows_of(pc, rows), :]

        locs = [pltpu.make_async_copy(piece(a, x, y, c), outs[a].at[mine], loc.at[a]) for a in range(n)]
        for cp in locs:
            cp.start()

        def remote(a, k, slot):
            fx, fy, fc = DEV_FLIPS[k]
            px, py, pc = _flip(x, fx), _flip(y, fy), _flip(c, fc)
            return pltpu.make_async_remote_copy(piece(a, px, py, pc), outs[a].at[slot], send.at[7 * a + k], recv.at[7 * a + k],
                                                device_id=(px, py, pc), device_id_type=MESH)

        for a in range(n):
            for k in range(7):
                remote(a, k, mine).start()
        for a in range(n):
            for k, (fx, fy, fc) in enumerate(DEV_FLIPS):
                cp = remote(a, k, 4 * _flip(x, fx) + 2 * _flip(y, fy) + _flip(c, fc))
                cp.wait_recv()
                cp.wait_send()
        for cp in locs:
            cp.wait()

    outs = [jax.ShapeDtypeStruct((8, p.shape[1] // 2, p.shape[2]), p.dtype) for p in parts]
    return _comm_call(body, name, parts, outs, 7 * n, n)


def sum_share(slots, name="rs_sum"):
    n, r, cols = slots.shape
    tr = _tile(r, (128, 176, 64))
    steps = r // tr

    def body(s_ref, g_ref, buf, send, loc, recv):
        i = pl.program_id(0)
        x, y, c = _me()
        slot = i % 2

        def copies(step, sl):
            rows = pl.ds(pl.multiple_of(c * r + step * tr, 8), tr)
            rem = pltpu.make_async_remote_copy(buf.at[sl], g_ref.at[rows, :], send.at[sl], recv.at[0],
                                               device_id=(x, y, 1 - c), device_id_type=MESH)
            return rem, pltpu.make_async_copy(buf.at[sl], g_ref.at[rows, :], loc.at[sl])

        @pl.when(i >= 2)
        def _():
            rem, lc = copies(i - 2, slot)
            rem.wait_send()
            lc.wait()

        acc = s_ref[0].astype(f32)
        for j in range(1, n):
            acc = acc + s_ref[j].astype(f32)
        buf[slot] = acc
        rem, lc = copies(i, slot)
        rem.start()
        lc.start()

        @pl.when(i == steps - 1)
        def _():
            for back in range(min(2, steps)):
                rem, lc = copies(i - back, (i - back) % 2)
                rem.wait_send()
                lc.wait()
            other = g_ref.at[pl.ds(pl.multiple_of((1 - c) * r, 8), r), :]
            pltpu.make_async_remote_copy(other, other, send.at[0], recv.at[0],
                                         device_id=(x, y, 1 - c), device_id_type=MESH).wait_recv()

    return pl.pallas_call(
        body, name=name, grid=(steps,),
        in_specs=[pl.BlockSpec((n, tr, cols), lambda i: (0, i, 0))], out_specs=ANY,
        out_shape=jax.ShapeDtypeStruct((2 * r, cols), f32),
        scratch_shapes=[pltpu.VMEM((2, tr, cols), f32), pltpu.SemaphoreType.DMA((2,)), pltpu.SemaphoreType.DMA((2,)),
                        pltpu.SemaphoreType.DMA((1,))],
        compiler_params=_cp(("arbitrary",)),
    )(slots)


BIG = ("ffn1_w_in", "ffn1_w_out", "w_in", "w_branch_a", "w_branch_b", "w_branch_c", "w_out", "ffn2_w_in", "ffn2_w_out")
ROW_SHARDED = ("ffn1_w_out", "w_out", "ffn2_w_out")
RES_W = (0.5, 1.0, 0.5)


def _full_weight(name, g):
    if name in ROW_SHARDED:
        return g.reshape(4 * g.shape[1], g.shape[2])
    return jnp.concatenate([g[0], g[1], g[2], g[3]], axis=1)


def _by_shard(name, dw):
    if name in ROW_SHARDED:
        return dw.reshape(4, dw.shape[0] // 4, dw.shape[1])
    return dw.reshape(dw.shape[0], 4, dw.shape[1] // 4).transpose(1, 0, 2)


def _full_weight_t(name, g):
    if name in ROW_SHARDED:
        return g.reshape(4 * g.shape[1], g.shape[2]).T
    return g.transpose(0, 2, 1).reshape(4 * g.shape[2], g.shape[1])


def _ffn_fwd(x, w_in, w_out, a_vec, sh_vec, b_vec):
    h = prenorm(x, a_vec, sh_vec)
    ua, ub, s = ffn_in_swiglu(h, w_in)
    y = mm(s, w_out, name="ffn_out")
    return postnorm(x, y, b_vec), (x, h, ua, ub, s, y)


def _ffn_bwd(dout, saved, w_in_t, w_out_t, a_vec, b_vec):
    x, h, ua, ub, s, y = saved
    dy, db = post_bwd(dout, y, b_vec)
    dw_out = mm(s.T, dy, out_dtype=bf16, name="ffn_dwo")
    du = ffn_du(dy, w_out_t, ua, ub)
    dh = mm(du, w_in_t, name="ffn_dh")
    dw_in = mm(h.T, du, out_dtype=bf16, name="ffn_dwi")
    dx, dsh, da = pre_bwd(dout, dh, x, a_vec)
    return dx, dw_in, dw_out, dsh, da, db


def _mix_fwd(x, w, lb, ng, a_vec, sh_vec, b_vec):
    h = prenorm(x, a_vec, sh_vec)
    u = mm(h, w["w_in"], name="mix_in")
    o, ya, st = hgrn_fwd(u, lb, ng)
    yb = sb_fwd(u)
    groups = [dil_fwd(u, g) for g in range(3)]
    yc, lse = dil_merge([o_ for o_, _ in groups], [l_ for _, l_ in groups])
    pa = mm(ya, w["w_branch_a"], name="mix_pa")
    pb = mm(yb, w["w_branch_b"], name="mix_pb")
    pc = mm(yc, w["w_branch_c"], name="mix_pc")
    merged = gate_merge(u, pa, pb, pc)
    z = mm(merged, w["w_out"], name="mix_out")
    return postnorm(x, z, b_vec), (x, h, u, o, ya, st, yb, yc, lse, pa, pb, pc, merged, z)


def _mix_bwd(dout, saved, wt, lb, ng, a_vec, b_vec):
    x, h, u, o, ya, st, yb, yc, lse, pa, pb, pc, merged, z = saved
    dz, db = post_bwd(dout, z, b_vec)
    dmerged = mm(dz, wt["w_out"], name="mix_dm")
    dw_out = mm(merged.T, dz, out_dtype=bf16, name="mix_dwo")
    dpa, dpb, dpc, dg0, dg1, dg2 = gate_bwd(dmerged, u, pa, pb, pc)
    dya = mm(dpa, wt["w_branch_a"], name="mix_dya")
    dyb = mm(dpb, wt["w_branch_b"], name="mix_dyb")
    dyc = mm(dpc, wt["w_branch_c"], name="mix_dyc")
    dw_a = mm(ya.T, dpa, out_dtype=bf16, name="mix_dwa")
    dw_b = mm(yb.astype(bf16).T, dpb, out_dtype=bf16, name="mix_dwb")
    dw_c = mm(yc.astype(bf16).T, dpc, out_dtype=bf16, name="mix_dwc")
    daq, daf, dai, dag, dlb, dng = hgrn_bwd(u, lb, ng, o, st, dya)
    dbq, dbk, dbv = sb_bwd(u, yb, dyb)
    dc = [dil_bwd(u, g, dyc, yc, lse) for g in range(3)]
    du = jnp.concatenate(
        [daq, daf, dai, dag] + [t.astype(bf16) for t in (dbq, dbk, dbv)]
        + [dc[g][j].astype(bf16) for j in range(3) for g in range(3)] + [dg0, dg1, dg2], axis=1)
    dh = mm(du, wt["w_in"], name="mix_dh")
    dw_in = mm(h.T, du, out_dtype=bf16, name="mix_dwi")
    dx, dsh, da = pre_bwd(dout, dh, x, a_vec)
    grads = {"w_in": dw_in, "w_out": dw_out, "w_branch_a": dw_a, "w_branch_b": dw_b, "w_branch_c": dw_c}
    return dx, grads, dlb, jnp.sum(dng, axis=0), dsh, da, db


def _reduce_to_shards(names, grads):
    parts = [_by_shard(n, grads[n]) for n in names]
    return {n: sum_share(s) for n, s in zip(names, dev_exchange(parts))}


def kernel(x, c, w_ada, b_ada, norm_g, ffn1_w_in, ffn1_w_out, w_in, hgrn_lb_logits, hgrn_norm_g, w_branch_a, w_branch_b, w_branch_c, w_out, ffn2_w_in, ffn2_w_out, loss_target, m_w_ada, m_b_ada, m_norm_g, m_ffn1_w_in, m_ffn1_w_out, m_w_in, m_hgrn_lb_logits, m_hgrn_norm_g, m_w_branch_a, m_w_branch_b, m_w_branch_c, m_w_out, m_ffn2_w_in, m_ffn2_w_out, v_w_ada, v_b_ada, v_norm_g, v_ffn1_w_in, v_ffn1_w_out, v_w_in, v_hgrn_lb_logits, v_hgrn_norm_g, v_w_branch_a, v_w_branch_b, v_w_branch_c, v_w_out, v_ffn2_w_in, v_ffn2_w_out):
    weights = dict(w_ada=w_ada, b_ada=b_ada, norm_g=norm_g, ffn1_w_in=ffn1_w_in, ffn1_w_out=ffn1_w_out, w_in=w_in,
                   hgrn_lb_logits=hgrn_lb_logits, hgrn_norm_g=hgrn_norm_g, w_branch_a=w_branch_a, w_branch_b=w_branch_b,
                   w_branch_c=w_branch_c, w_out=w_out, ffn2_w_in=ffn2_w_in, ffn2_w_out=ffn2_w_out)
    mom = dict(w_ada=m_w_ada, b_ada=m_b_ada, norm_g=m_norm_g, ffn1_w_in=m_ffn1_w_in, ffn1_w_out=m_ffn1_w_out, w_in=m_w_in,
               hgrn_lb_logits=m_hgrn_lb_logits, hgrn_norm_g=m_hgrn_norm_g, w_branch_a=m_w_branch_a, w_branch_b=m_w_branch_b,
               w_branch_c=m_w_branch_c, w_out=m_w_out, ffn2_w_in=m_ffn2_w_in, ffn2_w_out=m_ffn2_w_out)
    var = dict(w_ada=v_w_ada, b_ada=v_b_ada, norm_g=v_norm_g, ffn1_w_in=v_ffn1_w_in, ffn1_w_out=v_ffn1_w_out, w_in=v_w_in,
               hgrn_lb_logits=v_hgrn_lb_logits, hgrn_norm_g=v_hgrn_norm_g, w_branch_a=v_w_branch_a, w_branch_b=v_w_branch_b,
               w_branch_c=v_w_branch_c, w_out=v_w_out, ffn2_w_in=v_ffn2_w_in, ffn2_w_out=v_ffn2_w_out)
    order = list(weights)
    xi, yi, ci = _me()
    chip = 2 * xi + yi
    dev = 4 * xi + 2 * yi + ci
    xs = x[0]

    c_all = all_gather_devs(c, name="ag8_c").reshape(8, D)
    mod_sh = all_gather_chips([ada_fwd(c_all, w_ada)], name="ag4_mod")[0]
    mod_all = mod_sh.transpose(1, 2, 0, 3).reshape(DEPTH, 8, 9 * D)
    mod = lax.dynamic_index_in_dim(mod_all, dev, axis=1, keepdims=False) + b_ada
    mod = mod.reshape(DEPTH, 3, 3, D)
    ng_all = all_gather_chips([norm_g.reshape(DEPTH * 6, D // 4)], name="ag4_norm")[0]
    ng_all = ng_all.reshape(4, DEPTH, 6, D // 4).transpose(1, 2, 0, 3).reshape(DEPTH, 6, D)
    lb_all = lb_fwd(hgrn_lb_logits)
    w16 = {n: cast_bf16(weights[n]) for n in BIG}

    def vecs(l, i):
        shift, scale, gate = mod[l, i, 0][None], mod[l, i, 1][None], mod[l, i, 2][None]
        g_pre, g_post = ng_all[l, 2 * i][None], ng_all[l, 2 * i + 1][None]
        return g_pre * (1.0 + scale), shift, RES_W[i] * gate * g_post

    saved, full = [], []
    for l in range(DEPTH):
        gathered = all_gather_chips([w16[n] for n in BIG], layer=l, name=f"ag4_w{l}")
        w = {n: _full_weight(n, g) for n, g in zip(BIG, gathered)}
        full.append({n: _full_weight_t(n, g) for n, g in zip(BIG, gathered)})
        lb, ng = lb_all[l][None], hgrn_norm_g[l][None]
        xs, s1 = _ffn_fwd(xs, w["ffn1_w_in"], w["ffn1_w_out"], *vecs(l, 0))
        xs, s2 = _mix_fwd(xs, w, lb, ng, *vecs(l, 1))
        xs, s3 = _ffn_fwd(xs, w["ffn2_w_in"], w["ffn2_w_out"], *vecs(l, 2))
        saved.append((s1, s2, s3))

    dx, loss_part = loss_grad(xs, loss_target[0])
    loss = lax.psum(loss_part[0, 0], ("x", "y", "c"))

    big_grads = {n: [None] * DEPTH for n in BIG}
    d_mod, d_ng, d_lb, d_hng = [None] * DEPTH, [None] * DEPTH, [None] * DEPTH, [None] * DEPTH
    for l in reversed(range(DEPTH)):
        wt = full[l]
        s1, s2, s3 = saved[l]
        lb, ng = lb_all[l][None], hgrn_norm_g[l][None]
        rows_mod, rows_ng = [None] * 9, [None] * 6

        def vec_grads(i, dsh, da, db):
            scale, gate = mod[l, i, 1][None], mod[l, i, 2][None]
            g_pre, g_post = ng_all[l, 2 * i][None], ng_all[l, 2 * i + 1][None]
            rows_mod[3 * i], rows_mod[3 * i + 1], rows_mod[3 * i + 2] = dsh, g_pre * da, RES_W[i] * g_post * db
            rows_ng[2 * i], rows_ng[2 * i + 1] = (1.0 + scale) * da, RES_W[i] * gate * db

        a3, _, b3 = vecs(l, 2)
        dx, dwi, dwo, dsh, da, db = _ffn_bwd(dx, s3, wt["ffn2_w_in"], wt["ffn2_w_out"], a3, b3)
        vec_grads(2, dsh, da, db)
        grads = {"ffn2_w_in": dwi, "ffn2_w_out": dwo}
        a2, _, b2 = vecs(l, 1)
        dx, gmix, dlb, dhng, dsh, da, db = _mix_bwd(dx, s2, wt, lb, ng, a2, b2)
        vec_grads(1, dsh, da, db)
        grads.update(gmix)
        a1, _, b1 = vecs(l, 0)
        dx, dwi, dwo, dsh, da, db = _ffn_bwd(dx, s1, wt["ffn1_w_in"], wt["ffn1_w_out"], a1, b1)
        vec_grads(0, dsh, da, db)
        grads.update({"ffn1_w_in": dwi, "ffn1_w_out": dwo})
        for n, g in _reduce_to_shards(BIG, grads).items():
            big_grads[n][l] = g
        d_mod[l] = jnp.concatenate(rows_mod, axis=1)
        d_ng[l] = jnp.concatenate(rows_ng, axis=0)
        d_lb[l], d_hng[l] = dlb, dhng

    n_small = 6 * D * DEPTH + 768 * DEPTH + A_V * DEPTH + 9 * D * DEPTH
    pad = -n_small % (512 * LANE)
    flat = jnp.concatenate([jnp.stack(d_ng).reshape(-1), jnp.concatenate(d_lb, axis=0).reshape(-1),
                            jnp.concatenate(d_hng, axis=0).reshape(-1), jnp.concatenate(d_mod, axis=0).reshape(-1),
                            jnp.zeros((pad,), f32)])
    small_all = all_gather_devs(flat.reshape(-1, LANE), name="ag8_small")
    total = sum_slots(small_all).reshape(-1)
    o1 = 6 * D * DEPTH
    o2 = o1 + 768 * DEPTH
    o3 = o2 + A_V * DEPTH
    g_ng_full = total[:o1].reshape(DEPTH, 6, D)
    g_lb_all = total[o1:o2].reshape(DEPTH, 768)
    g_small = {
        "norm_g": lax.dynamic_slice_in_dim(g_ng_full, chip * (D // 4), D // 4, axis=2),
        "hgrn_lb_logits": lb_bwd(hgrn_lb_logits, g_lb_all),
        "hgrn_norm_g": total[o2:o3].reshape(DEPTH, A_V),
        "b_ada": total[o3:n_small].reshape(DEPTH, 9 * D),
    }
    dmod_all = small_all.reshape(8, -1)[:, o3:n_small].reshape(8, DEPTH, 9 * D).transpose(1, 0, 2)
    dm_sh = lax.dynamic_slice_in_dim(dmod_all, chip * ADA_N, ADA_N, axis=2)

    out_g, out_d, out_m, out_v = {}, {}, {}, {}
    out_g["w_ada"], out_d["w_ada"], out_m["w_ada"], out_v["w_ada"] = ada_bwd_adam(c_all, dm_sh, w_ada, m_w_ada, v_w_ada)
    for n in BIG:
        out_g[n] = jnp.stack(big_grads[n])
    out_g.update(g_small)
    for n in order:
        if n != "w_ada":
            out_d[n], out_m[n], out_v[n] = adam(weights[n], out_g[n], mom[n], var[n])
    return (loss, dx[None], *[out_g[n] for n in order], *[out_d[n] for n in order],
            *[out_m[n] for n in order], *[out_v[n] for n in order])
```

```python
import functools
import math

import jax
import jax.numpy as jnp
from jax import lax
from jax.experimental import pallas as pl
from jax.experimental.pallas import tpu as pltpu

f32, bf16 = jnp.float32, jnp.bfloat16

D = 1024
DEPTH = 4
D_FF = 2816
EPS = 1e-6
NEG_BIG = -1e30
TINY = 1e-30
A_HEADS, A_K, A_V, A_CHUNK = 6, 128, 64, 64
A_SUB = 16
A_CLAMP = 80.0
B_HEADS, HD = 6, 64
C_GROUPS = ((128, 1), (512, 4), (2048, 16))
C_BLK = 128
IN_COLS = 8832
O_AQ, O_AF, O_AI, O_AG = 0, 768, 1536, 1920
O_BQ, O_BK, O_BV = 2304, 2688, 3072
O_CQ, O_CK, O_CV = 3456, 4224, 4992
O_GATE = 5760
LANE = 128
ADAM_LR, ADAM_B1, ADAM_B2, ADAM_EPS, ADAM_WD, ADAM_STEP = 0.001, 0.9, 0.999, 1e-08, 0.01, 10
MESH = pl.DeviceIdType.MESH
VMEM_LIMIT = 56 * 1024 * 1024


def _alibi_slopes(n):
    def pow2(m):
        start = 2.0 ** (-8.0 / m)
        return [start ** (i + 1) for i in range(m)]
    if math.log2(n).is_integer():
        s = pow2(n)
    else:
        c = 2 ** int(math.floor(math.log2(n)))
        s = pow2(c) + pow2(2 * c)[0::2][: n - c]
    return sorted(s, reverse=True)


C_SLOPES = _alibi_slopes(12)


def _tile(n, prefs):
    for p in prefs:
        if n % p == 0:
            return p
    return n


def _cp(sem):
    return pltpu.CompilerParams(dimension_semantics=sem, vmem_limit_bytes=VMEM_LIMIT)


def _sig(x):
    return 1.0 / (1.0 + jnp.exp(-x))


def _dot(a, b, dn, precision=None):
    return lax.dot_general(a, b, (dn, ((), ())), preferred_element_type=f32, precision=precision)


NN = ((1,), (0,))
NT = ((1,), (1,))
TN = ((0,), (0,))


class Plan:
    def __init__(self, ins, outs, n_remote, n_local, start, wait):
        self.ins, self.outs, self.n_remote, self.n_local, self.start, self.wait = ins, outs, n_remote, n_local, start, wait

    def sems(self):
        return [pltpu.SemaphoreType.DMA((self.n_remote,)), pltpu.SemaphoreType.DMA((self.n_remote,)),
                pltpu.SemaphoreType.DMA((max(self.n_local, 1),))]


def _call(body, *, name, grid, in_specs, out_specs, out_shape, sem, args, scratch_shapes=(), side=None):
    if side is None:
        return pl.pallas_call(body, name=name, grid=grid, in_specs=in_specs, out_specs=out_specs, out_shape=out_shape,
                              scratch_shapes=list(scratch_shapes), compiler_params=_cp(sem))(*args), None
    any_spec = pl.BlockSpec(memory_space=pl.ANY)
    n_in, n_out, n_scr = len(in_specs), len(out_specs), len(scratch_shapes)
    s_in, s_out = len(side.ins), len(side.outs)

    def hosted(*refs):
        ins, rest = refs[:n_in], refs[n_in:]
        sins, rest = rest[:s_in], rest[s_in:]
        outs, rest = rest[:n_out], rest[n_out:]
        souts, rest = rest[:s_out], rest[s_out:]
        scr, sems = rest[:n_scr], rest[n_scr:]
        pids = [pl.program_id(d) for d in range(len(grid))]
        first = functools.reduce(jnp.logical_and, [p == 0 for p in pids])
        last = functools.reduce(jnp.logical_and, [p == g - 1 for p, g in zip(pids, grid)])

        @pl.when(first)
        def _():
            side.start(sins, souts, *sems)

        body(*ins, *outs, *scr)

        @pl.when(last)
        def _():
            side.wait(sins, souts, *sems)

    res = pl.pallas_call(
        hosted, name=name, grid=grid, in_specs=list(in_specs) + [any_spec] * s_in,
        out_specs=list(out_specs) + [any_spec] * s_out, out_shape=list(out_shape) + list(side.outs),
        scratch_shapes=list(scratch_shapes) + side.sems(), compiler_params=_cp(("arbitrary",) * len(grid)),
    )(*args, *side.ins)
    return res[:n_out], res[n_out:]


MM_TILES = {
    (4096, 1024, 2816): (1024, 512, 2816),
    (4096, 1024, 5632): (512, 512, 5632),
    (1024, 5632, 4096): (512, 512, 4096),
    (2816, 1024, 4096): (704, 512, 4096),
    (4096, 8832, 1024): (512, 2944, 1024),
    (4096, 1024, 8832): (1024, 512, 2944),
    (1024, 8832, 4096): (512, 2944, 1024),
    (1024, 1024, 4096): (512, 512, 4096),
    (384, 1024, 4096): (384, 512, 4096),
    (256, 1024, 4096): (256, 512, 4096),
}


def mm(a, b, *, out_dtype=f32, name="mm", side=None):
    M, K = a.shape
    K2, N = b.shape
    assert K == K2, (a.shape, b.shape)
    tm, tn, tk = MM_TILES.get((M, N, K), (_tile(M, (1024, 704, 512, 384, 256, 128)), _tile(N, (512, 384, 256, 128)),
                                          _tile(K, (1024, 512, 1408, 384, 256, 128))))
    nk = K // tk

    def body(a_ref, b_ref, o_ref, *acc):
        p = _dot(a_ref[...].astype(bf16), b_ref[...].astype(bf16), NN)
        if nk == 1:
            o_ref[...] = p.astype(out_dtype)
            return
        acc_ref, = acc
        k = pl.program_id(2)

        @pl.when(k == 0)
        def _():
            acc_ref[...] = p

        @pl.when(k > 0)
        def _():
            acc_ref[...] += p

        @pl.when(k == nk - 1)
        def _():
            o_ref[...] = acc_ref[...].astype(out_dtype)

    outs, souts = _call(
        body, name=name, grid=(M // tm, N // tn, nk),
        in_specs=[pl.BlockSpec((tm, tk), lambda i, j, k: (i, k)), pl.BlockSpec((tk, tn), lambda i, j, k: (k, j))],
        out_specs=[pl.BlockSpec((tm, tn), lambda i, j, k: (i, j))],
        out_shape=[jax.ShapeDtypeStruct((M, N), out_dtype)],
        scratch_shapes=[pltpu.VMEM((tm, tn), f32)] if nk > 1 else [],
        sem=("parallel", "parallel", "arbitrary"), args=(a, b), side=side)
    return outs[0] if side is None else (outs[0], souts)


FF_TM = 1024
FF_T = 256
FF_NB = D_FF // FF_T


def ffn_in_swiglu(h, w_in):
    S = h.shape[0]

    def body(h_ref, wa_ref, wb_ref, a_ref, b_ref, s_ref):
        hv = h_ref[...]
        a = _dot(hv, wa_ref[...], NN)
        b = _dot(hv, wb_ref[...], NN)
        a_ref[...] = a.astype(bf16)
        b_ref[...] = b.astype(bf16)
        s_ref[...] = (a * _sig(a) * b).astype(bf16)

    ospec = pl.BlockSpec((FF_TM, FF_T), lambda i, j: (i, j))
    osd = jax.ShapeDtypeStruct((S, D_FF), bf16)
    return pl.pallas_call(
        body, name="ffn_in", grid=(S // FF_TM, FF_NB),
        in_specs=[pl.BlockSpec((FF_TM, D), lambda i, j: (i, 0)), pl.BlockSpec((D, FF_T), lambda i, j: (0, j)),
                  pl.BlockSpec((D, FF_T), lambda i, j: (0, j + FF_NB))],
        out_specs=[ospec] * 3, out_shape=[osd] * 3, compiler_params=_cp(("parallel", "parallel")),
    )(h, w_in, w_in)


def ffn_du(dy, w_out_t, ua, ub):
    S = dy.shape[0]

    def body(dy_ref, w_ref, a_ref, b_ref, du_ref):
        j = pl.program_id(1)
        ds = _dot(dy_ref[...], w_ref[...], NN)
        a, b = a_ref[...].astype(f32), b_ref[...].astype(f32)
        sg = _sig(a)
        da = ds * b * sg * (1.0 + a * (1.0 - sg))
        db = ds * a * sg
        du_ref[...] = jnp.where(j < FF_NB, da, db).astype(bf16)

    half = pl.BlockSpec((FF_TM, FF_T), lambda i, j: (i, j % FF_NB))
    return pl.pallas_call(
        body, name="ffn_du", grid=(S // FF_TM, 2 * FF_NB),
        in_specs=[pl.BlockSpec((FF_TM, D), lambda i, j: (i, 0)), pl.BlockSpec((D, FF_T), lambda i, j: (0, j % FF_NB)),
                  half, half],
        out_specs=pl.BlockSpec((FF_TM, FF_T), lambda i, j: (i, j)),
        out_shape=jax.ShapeDtypeStruct((S, 2 * D_FF), bf16), compiler_params=_cp(("parallel", "parallel")),
    )(dy, w_out_t, ua, ub)


TR = 256


def _row_spec(cols=D):
    return pl.BlockSpec((TR, cols), lambda i: (i, 0))


def _vec_spec(cols=D):
    return pl.BlockSpec((1, cols), lambda i: (0, 0))


def prenorm(x, a_vec, sh_vec):
    S = x.shape[0]

    def body(x_ref, a_ref, s_ref, h_ref):
        xv = x_ref[...]
        rstd = lax.rsqrt(jnp.mean(xv * xv, axis=1, keepdims=True) + EPS)
        h_ref[...] = (xv * rstd * a_ref[...] + s_ref[...]).astype(bf16)

    return pl.pallas_call(
        body, name="prenorm", grid=(S // TR,),
        in_specs=[_row_spec(), _vec_spec(), _vec_spec()], out_specs=_row_spec(),
        out_shape=jax.ShapeDtypeStruct((S, D), bf16), compiler_params=_cp(("parallel",)),
    )(x, a_vec, sh_vec)


def postnorm(x, y, b_vec):
    S = x.shape[0]

    def body(x_ref, y_ref, b_ref, o_ref):
        yv = y_ref[...]
        rstd = lax.rsqrt(jnp.mean(yv * yv, axis=1, keepdims=True) + EPS)
        o_ref[...] = x_ref[...] + b_ref[...] * (yv * rstd)

    return pl.pallas_call(
        body, name="postnorm", grid=(S // TR,),
        in_specs=[_row_spec(), _row_spec(), _vec_spec()], out_specs=_row_spec(),
        out_shape=jax.ShapeDtypeStruct((S, D), f32), compiler_params=_cp(("parallel",)),
    )(x, y, b_vec)


def post_bwd(dout, y, b_vec):
    S = dout.shape[0]

    def body(d_ref, y_ref, b_ref, dy_ref, db_ref):
        i = pl.program_id(0)
        yv, dv = y_ref[...], d_ref[...]
        rstd = lax.rsqrt(jnp.mean(yv * yv, axis=1, keepdims=True) + EPS)
        yh = yv * rstd
        dyh = dv * b_ref[...]
        dy_ref[...] = (rstd * (dyh - yh * jnp.mean(dyh * yh, axis=1, keepdims=True))).astype(bf16)
        part = jnp.sum(dv * yh, axis=0, keepdims=True)

        @pl.when(i == 0)
        def _():
            db_ref[...] = part

        @pl.when(i > 0)
        def _():
            db_ref[...] += part

    return pl.pallas_call(
        body, name="post_bwd", grid=(S // TR,),
        in_specs=[_row_spec(), _row_spec(), _vec_spec()], out_specs=[_row_spec(), _vec_spec()],
        out_shape=[jax.ShapeDtypeStruct((S, D), bf16), jax.ShapeDtypeStruct((1, D), f32)],
        compiler_params=_cp(("arbitrary",)),
    )(dout, y, b_vec)


def pre_bwd(dout, dh, x, a_vec):
    S = dout.shape[0]

    def body(d_ref, dh_ref, x_ref, a_ref, dx_ref, ds_ref, da_ref):
        i = pl.program_id(0)
        xv, dhv = x_ref[...], dh_ref[...]
        rstd = lax.rsqrt(jnp.mean(xv * xv, axis=1, keepdims=True) + EPS)
        n1 = xv * rstd
        dn = dhv * a_ref[...]
        dx_ref[...] = d_ref[...] + rstd * (dn - n1 * jnp.mean(dn * n1, axis=1, keepdims=True))
        p_s = jnp.sum(dhv, axis=0, keepdims=True)
        p_a = jnp.sum(dhv * n1, axis=0, keepdims=True)

        @pl.when(i == 0)
        def _():
            ds_ref[...] = p_s
            da_ref[...] = p_a

        @pl.when(i > 0)
        def _():
            ds_ref[...] += p_s
            da_ref[...] += p_a

    return pl.pallas_call(
        body, name="pre_bwd", grid=(S // TR,),
        in_specs=[_row_spec(), _row_spec(), _row_spec(), _vec_spec()],
        out_specs=[_row_spec(), _vec_spec(), _vec_spec()],
        out_shape=[jax.ShapeDtypeStruct((S, D), f32), jax.ShapeDtypeStruct((1, D), f32), jax.ShapeDtypeStruct((1, D), f32)],
        compiler_params=_cp(("arbitrary",)),
    )(dout, dh, x, a_vec)


def loss_grad(y, tgt):
    S = y.shape[0]

    def body(y_ref, t_ref, dy_ref, l_ref):
        i = pl.program_id(0)
        e = y_ref[...] - t_ref[...]
        dy_ref[...] = e * (1.0 / D)
        part = jnp.sum(jnp.sum(e * e, axis=1, keepdims=True), axis=0, keepdims=True) * (0.5 / D)
        part = jnp.broadcast_to(part, (8, LANE))

        @pl.when(i == 0)
        def _():
            l_ref[...] = part

        @pl.when(i > 0)
        def _():
            l_ref[...] += part

    return pl.pallas_call(
        body, name="loss_grad", grid=(S // TR,),
        in_specs=[_row_spec(), _row_spec()],
        out_specs=[_row_spec(), pl.BlockSpec((8, LANE), lambda i: (0, 0))],
        out_shape=[jax.ShapeDtypeStruct((S, D), f32), jax.ShapeDtypeStruct((8, LANE), f32)],
        compiler_params=_cp(("arbitrary",)),
    )(y, tgt)


G_NB = D // LANE
G_OFF = O_GATE // LANE


def gate_merge(u, pa, pb, pc):
    S = u.shape[0]

    def body(g0, g1, g2, a, b, c, o_ref):
        o_ref[...] = (_sig(g0[...]) * a[...] + _sig(g1[...]) * b[...] + _sig(g2[...]) * c[...]).astype(bf16)

    gs = [pl.BlockSpec((512, LANE), functools.partial(lambda i, j, k: (i, G_OFF + G_NB * k + j), k=k)) for k in range(3)]
    ps = pl.BlockSpec((512, LANE), lambda i, j: (i, j))
    return pl.pallas_call(
        body, name="gate_merge", grid=(S // 512, G_NB),
        in_specs=gs + [ps, ps, ps], out_specs=ps,
        out_shape=jax.ShapeDtypeStruct((S, D), bf16), compiler_params=_cp(("parallel", "parallel")),
    )(u, u, u, pa, pb, pc)


def gate_bwd(dm, u, pa, pb, pc):
    S = u.shape[0]

    def body(dm_ref, g0, g1, g2, a, b, c, da, db, dc, dg0, dg1, dg2):
        d = dm_ref[...]
        for g, p, dp, dg in ((g0, a, da, dg0), (g1, b, db, dg1), (g2, c, dc, dg2)):
            s = _sig(g[...])
            dp[...] = (d * s).astype(bf16)
            dg[...] = (d * p[...] * s * (1.0 - s)).astype(bf16)

    gs = [pl.BlockSpec((512, LANE), functools.partial(lambda i, j, k: (i, G_OFF + G_NB * k + j), k=k)) for k in range(3)]
    ps = pl.BlockSpec((512, LANE), lambda i, j: (i, j))
    osd = jax.ShapeDtypeStruct((S, D), bf16)
    return pl.pallas_call(
        body, name="gate_bwd", grid=(S // 512, G_NB),
        in_specs=[ps] + gs + [ps, ps, ps], out_specs=[ps] * 6, out_shape=[osd] * 6,
        compiler_params=_cp(("parallel", "parallel")),
    )(dm, u, u, u, pa, pb, pc)


A_TB = 512
A_NCH = A_TB // A_CHUNK
A_NSUB = A_CHUNK // A_SUB


def _hgrn_gates(qr, fr, lbh):
    sq = _sig(qr)
    sig = _sig(fr)
    f = lbh + (1.0 - lbh) * sig
    logf = jnp.log(jnp.maximum(f, TINY))
    k = (1.0 - lbh) * (1.0 - sig)
    return qr * sq, sq, sig, f, logf, k


def _hgrn_intra(qf, k, b, causal):
    qts, kts, eqs, eks, blocks = [], [], [], [], []
    for sb in range(A_NSUB):
        rs = sb * A_SUB
        r = b[rs - 1:rs, :] if sb else jnp.zeros((1, A_K), f32)
        eq = jnp.exp(b[rs:rs + A_SUB, :] - r)
        ek = jnp.exp(jnp.minimum(r - b, A_CLAMP))
        qt = (qf[rs:rs + A_SUB, :] * eq).astype(bf16)
        kt = (k * ek).astype(bf16)
        blocks.append(_dot(qt, kt, NT))
        qts.append(qt), kts.append(kt), eqs.append(eq), eks.append(ek)
    a = jnp.where(causal, jnp.concatenate(blocks, axis=0), 0.0)
    return a, qts, kts, eqs, eks


def _tri():
    r = lax.broadcasted_iota(jnp.int32, (A_CHUNK, A_CHUNK), 0)
    c = lax.broadcasted_iota(jnp.int32, (A_CHUNK, A_CHUNK), 1)
    return r >= c


def _hgrn_in_specs(rev_nb=None):
    def im(col):
        if rev_nb is None:
            return lambda p, i: (i, col + p)
        return lambda p, i: (rev_nb - 1 - i, col + p)
    return [pl.BlockSpec((A_TB, 256), im(O_AQ // 256)), pl.BlockSpec((A_TB, 256), im(O_AF // 256)),
            pl.BlockSpec((A_TB, LANE), im(O_AI // LANE)), pl.BlockSpec((A_TB, LANE), im(O_AG // LANE)),
            pl.BlockSpec((1, 256), lambda p, i: (0, p)), pl.BlockSpec((1, A_V), lambda p, i: (0, 0))]


def hgrn_fwd(u, lb, ng, side=None):
    S = u.shape[0]
    nb = S // A_TB

    def body(q_ref, f_ref, i_ref, g_ref, lb_ref, ng_ref, o_ref, ya_ref, st_ref, state):
        @pl.when(pl.program_id(1) == 0)
        def _():
            state[...] = jnp.zeros_like(state)

        causal = _tri()
        tri = causal.astype(f32)

        def chunk(n, carry):
            rows = pl.ds(pl.multiple_of(n * A_CHUNK, A_CHUNK), A_CHUNK)
            o_parts, y_parts = [], []
            for hh in range(2):
                ks = slice(hh * A_K, (hh + 1) * A_K)
                vs = slice(hh * A_V, (hh + 1) * A_V)
                qf, _, _, _, logf, k = _hgrn_gates(q_ref[rows, ks], f_ref[rows, ks], lb_ref[:, ks])
                vi = i_ref[rows, vs].astype(bf16)
                gg = g_ref[rows, vs]
                b = _dot(tri, logf, NN, precision=lax.Precision.HIGHEST)
                s0 = state[hh]
                st_ref[n, hh] = s0
                o = _dot((qf * jnp.exp(b)).astype(bf16), s0.astype(bf16), NT)
                a, _, _, _, _ = _hgrn_intra(qf, k, b, causal)
                o = o + _dot(a.astype(bf16), vi, NN)
                bend = b[A_CHUNK - 1:A_CHUNK, :]
                ke = (k * jnp.exp(bend - b)).astype(bf16)
                state[hh] = s0 * jnp.exp(bend) + _dot(vi, ke, TN)
                rstd = lax.rsqrt(jnp.mean(o * o, axis=1, keepdims=True) + EPS)
                o_parts.append(o)
                y_parts.append(o * rstd * ng_ref[...] * (gg * _sig(gg)))
            o_ref[rows, :] = jnp.concatenate(o_parts, axis=1)
            ya_ref[rows, :] = jnp.concatenate(y_parts, axis=1).astype(bf16)
            return carry

        lax.fori_loop(0, A_NCH, chunk, 0)

    return _call(
        body, name="hgrn_fwd", grid=(3, nb),
        in_specs=_hgrn_in_specs(),
        out_specs=[pl.BlockSpec((A_TB, LANE), lambda p, i: (i, p)), pl.BlockSpec((A_TB, LANE), lambda p, i: (i, p)),
                   pl.BlockSpec((A_NCH, 2, A_V, A_K), lambda p, i: (i, p, 0, 0))],
        out_shape=[jax.ShapeDtypeStruct((S, 384), f32), jax.ShapeDtypeStruct((S, 384), bf16),
                   jax.ShapeDtypeStruct((S // A_CHUNK, A_HEADS, A_V, A_K), f32)],
        scratch_shapes=[pltpu.VMEM((2, A_V, A_K), f32)],
        sem=("parallel", "arbitrary"), args=(u, u, u, u, lb, ng), side=side)


def hgrn_bwd(u, lb, ng, o, st, dya, side=None):
    S = u.shape[0]
    nb = S // A_TB

    def body(q_ref, f_ref, i_ref, g_ref, lb_ref, ng_ref, o_ref, st_ref, dy_ref,
             dq_ref, df_ref, di_ref, dg_ref, dlb_ref, dng_ref, dstate):
        @pl.when(pl.program_id(1) == 0)
        def _():
            dstate[...] = jnp.zeros_like(dstate)
            dlb_ref[...] = jnp.zeros_like(dlb_ref)
            dng_ref[...] = jnp.zeros_like(dng_ref)

        causal = _tri()
        tri = causal.astype(f32)

        def chunk(it, carry):
            n = A_NCH - 1 - it
            rows = pl.ds(pl.multiple_of(n * A_CHUNK, A_CHUNK), A_CHUNK)
            dq_p, df_p, di_p, dg_p, dlb_p = [], [], [], [], []
            dng_acc = jnp.zeros((1, A_V), f32)
            for hh in range(2):
                ks = slice(hh * A_K, (hh + 1) * A_K)
                vs = slice(hh * A_V, (hh + 1) * A_V)
                lbh = lb_ref[:, ks]
                qr = q_ref[rows, ks]
                qf, sq, sig, f, logf, k = _hgrn_gates(qr, f_ref[rows, ks], lbh)
                vi = i_ref[rows, vs].astype(bf16)
                gg = g_ref[rows, vs]
                b = _dot(tri, logf, NN, precision=lax.Precision.HIGHEST)
                eb = jnp.exp(b)
                bend = b[A_CHUNK - 1:A_CHUNK, :]
                eend = jnp.exp(bend)
                ekend = jnp.exp(bend - b)
                qe = (qf * eb).astype(bf16)
                ke = (k * ekend).astype(bf16)
                s0 = st_ref[n, hh]
                dsend = dstate[hh]
                ov = o_ref[rows, vs]
                dy = dy_ref[rows, vs]
                rstd = lax.rsqrt(jnp.mean(ov * ov, axis=1, keepdims=True) + EPS)
                oh = ov * rstd
                sg = _sig(gg)
                d_on = dy * (gg * sg)
                dg_p.append(dy * oh * ng_ref[...] * (sg * (1.0 + gg * (1.0 - sg))))
                dng_acc = dng_acc + jnp.sum(d_on * oh, axis=0, keepdims=True)
                doh = d_on * ng_ref[...]
                do = (rstd * (doh - oh * jnp.mean(doh * oh, axis=1, keepdims=True))).astype(bf16)
                a, qts, kts, eqs, eks = _hgrn_intra(qf, k, b, causal)
                da = jnp.where(causal, _dot(do, vi, NT), 0.0).astype(bf16)
                dsb = dsend.astype(bf16)
                dv = _dot(a.astype(bf16), do, TN) + _dot(ke, dsb, NT)
                dq = _dot(do, s0.astype(bf16), NN) * eb
                dk_state = _dot(vi, dsb, NN) * ekend
                dk = dk_state
                dq_i = []
                for sb in range(A_NSUB):
                    da_sb = da[sb * A_SUB:(sb + 1) * A_SUB, :]
                    dq_i.append(_dot(da_sb, kts[sb], NN) * eqs[sb])
                    dk = dk + _dot(da_sb, qts[sb], TN) * eks[sb]
                dq = dq + jnp.concatenate(dq_i, axis=0)
                db = qf * dq - k * dk
                extra = jnp.sum(k * dk_state, axis=0, keepdims=True) + eend * jnp.sum(s0 * dsend, axis=0, keepdims=True)
                dlogf = _dot(tri, db, TN, precision=lax.Precision.HIGHEST) + extra
                dstate[hh] = _dot(do, qe, TN) + eend * dsend
                d_pre = jnp.where(f > TINY, dlogf / f, 0.0) - dk
                dlb_p.append(jnp.sum((1.0 - sig) * d_pre, axis=0, keepdims=True))
                df_p.append((1.0 - lbh) * d_pre * sig * (1.0 - sig))
                dq_p.append(dq * (sq * (1.0 + qr * (1.0 - sq))))
                di_p.append(dv)
            dq_ref[rows, :] = jnp.concatenate(dq_p, axis=1).astype(bf16)
            df_ref[rows, :] = jnp.concatenate(df_p, axis=1).astype(bf16)
            di_ref[rows, :] = jnp.concatenate(di_p, axis=1).astype(bf16)
            dg_ref[rows, :] = jnp.concatenate(dg_p, axis=1).astype(bf16)
            dlb_ref[...] += jnp.concatenate(dlb_p, axis=1)
            dng_ref[0] += dng_acc
            return carry

        lax.fori_loop(0, A_NCH, chunk, 0)

    rev = lambda p, i: (nb - 1 - i, p)
    return _call(
        body, name="hgrn_bwd", grid=(3, nb),
        in_specs=_hgrn_in_specs(nb) + [pl.BlockSpec((A_TB, LANE), rev),
                                       pl.BlockSpec((A_NCH, 2, A_V, A_K), lambda p, i: (nb - 1 - i, p, 0, 0)),
                                       pl.BlockSpec((A_TB, LANE), rev)],
        out_specs=[pl.BlockSpec((A_TB, 256), rev), pl.BlockSpec((A_TB, 256), rev),
                   pl.BlockSpec((A_TB, LANE), rev), pl.BlockSpec((A_TB, LANE), rev),
                   pl.BlockSpec((1, 256), lambda p, i: (0, p)), pl.BlockSpec((1, 1, A_V), lambda p, i: (p, 0, 0))],
        out_shape=[jax.ShapeDtypeStruct((S, 768), bf16), jax.ShapeDtypeStruct((S, 768), bf16),
                   jax.ShapeDtypeStruct((S, 384), bf16), jax.ShapeDtypeStruct((S, 384), bf16),
                   jax.ShapeDtypeStruct((1, 768), f32), jax.ShapeDtypeStruct((3, 1, A_V), f32)],
        scratch_shapes=[pltpu.VMEM((2, A_V, A_K), f32)],
        sem=("parallel", "arbitrary"), args=(u, u, u, u, lb, ng, o, st, dya), side=side)


B_TK = 128
SCALE = HD ** -0.5


def _split(x):
    hi = x.astype(bf16)
    return hi, (x - hi.astype(f32)).astype(bf16)


def _dot2(x, m, dn):
    hi, lo = _split(x)
    return _dot(hi, m, dn) + _dot(lo, m, dn)


def _sb_block(qs, kh, mask, m_gt, c):
    z = _dot(qs, kh, NT)
    sp = jnp.maximum(z, 0.0) + jnp.log(1.0 + jnp.exp(-jnp.abs(z)))
    lneg = jnp.where(mask, -sp, 0.0)
    lsz = z - sp
    suf = _dot2(lneg, m_gt, NN) + c
    a = jnp.where(mask, jnp.exp(lsz + suf), 0.0)
    return lneg, lsz, a


def _sb_masks(tq, i, jj):
    t_idx = i * tq + lax.broadcasted_iota(jnp.int32, (tq, B_TK), 0)
    s_idx = jj * B_TK + lax.broadcasted_iota(jnp.int32, (tq, B_TK), 1)
    return s_idx < t_idx


def _sb_tri(strict):
    r = lax.broadcasted_iota(jnp.int32, (B_TK, B_TK), 0)
    c = lax.broadcasted_iota(jnp.int32, (B_TK, B_TK), 1)
    return (r > c if strict else r >= c).astype(bf16)


B_DEAD = -110.0


def _sb_walk(nkb, step, init):
    def cond(state):
        it, alive, _ = state
        return jnp.logical_and(it < nkb, alive)

    def body(state):
        it, _, carry = state
        carry = step(it, carry)
        top = jnp.maximum(jnp.max(carry[0][1]), jnp.max(carry[1][1]))
        return it + 1, top > B_DEAD, carry

    return lax.while_loop(cond, body, (jnp.int32(0), jnp.bool_(True), init))[2]


def sb_fwd(u, side=None):
    S = u.shape[0]
    tq = 256

    def body(q_ref, k_ref, v_ref, o_ref):
        i = pl.program_id(1)
        nkb = (i + 1) * (tq // B_TK)
        m_gt = _sb_tri(True)
        qs = [(q_ref[:, hh * HD:(hh + 1) * HD] * SCALE).astype(bf16) for hh in range(2)]

        def step(it, carry):
            jj = nkb - 1 - it
            rows = pl.ds(pl.multiple_of(jj * B_TK, B_TK), B_TK)
            mask = _sb_masks(tq, i, jj)
            kb, vb = k_ref[rows, :], v_ref[rows, :]
            out = []
            for hh in range(2):
                acc, c = carry[hh]
                kh = kb[:, hh * HD:(hh + 1) * HD].astype(bf16)
                vh = vb[:, hh * HD:(hh + 1) * HD].astype(bf16)
                lneg, _, a = _sb_block(qs[hh], kh, mask, m_gt, c)
                out.append((acc + _dot2(a, vh, NN), c + jnp.sum(lneg, axis=1, keepdims=True)))
            return tuple(out)

        z0 = (jnp.zeros((tq, HD), f32), jnp.zeros((tq, 1), f32))
        res = _sb_walk(nkb, step, (z0, z0))
        o_ref[...] = jnp.concatenate([res[0][0], res[1][0]], axis=1)

    outs, souts = _call(
        body, name="sb_fwd", grid=(3, S // tq),
        in_specs=[pl.BlockSpec((tq, LANE), lambda p, i: (i, O_BQ // LANE + p)),
                  pl.BlockSpec((S, LANE), lambda p, i: (0, O_BK // LANE + p)),
                  pl.BlockSpec((S, LANE), lambda p, i: (0, O_BV // LANE + p))],
        out_specs=[pl.BlockSpec((tq, LANE), lambda p, i: (i, p))],
        out_shape=[jax.ShapeDtypeStruct((S, 384), f32)],
        sem=("parallel", "arbitrary"), args=(u, u, u), side=side)
    return outs[0], souts


def sb_bwd(u, yb, dyb, side=None):
    S = u.shape[0]
    tq = 128

    def body(q_ref, k_ref, v_ref, y_ref, dy_ref, dq_ref, dk_ref, dv_ref):
        i = pl.program_id(1)

        @pl.when(i == 0)
        def _():
            dk_ref[...] = jnp.zeros_like(dk_ref)
            dv_ref[...] = jnp.zeros_like(dv_ref)

        nkb = (i + 1) * (tq // B_TK)
        m_gt = _sb_tri(True)
        m_ge = _sb_tri(False)
        qs, dos, tot = [], [], []
        for hh in range(2):
            hs = slice(hh * HD, (hh + 1) * HD)
            qs.append((q_ref[:, hs] * SCALE).astype(bf16))
            dob = dy_ref[:, hs].astype(bf16)
            dos.append(dob)
            tot.append(jnp.sum(dob.astype(f32) * y_ref[:, hs], axis=1, keepdims=True))

        def step(it, carry):
            jj = nkb - 1 - it
            rows = pl.ds(pl.multiple_of(jj * B_TK, B_TK), B_TK)
            mask = _sb_masks(tq, i, jj)
            kb, vb = k_ref[rows, :], v_ref[rows, :]
            out, dk_p, dv_p = [], [], []
            for hh in range(2):
                dq, c, cg = carry[hh]
                kh = kb[:, hh * HD:(hh + 1) * HD].astype(bf16)
                vh = vb[:, hh * HD:(hh + 1) * HD].astype(bf16)
                lneg, lsz, a = _sb_block(qs[hh], kh, mask, m_gt, c)
                g = a * _dot(dos[hh], vh, NT)
                p = tot[hh] - cg - _dot2(g, m_ge, NN)
                beta = jnp.exp(lsz)
                dz = jnp.where(mask, g * (1.0 - beta) - beta * p, 0.0).astype(bf16)
                dk_p.append(_dot(dz, qs[hh], TN))
                dv_p.append(_dot(a.astype(bf16), dos[hh], TN))
                out.append((dq + _dot(dz, kh, NN), c + jnp.sum(lneg, axis=1, keepdims=True),
                            cg + jnp.sum(g, axis=1, keepdims=True)))
            dk_ref[rows, :] += jnp.concatenate(dk_p, axis=1)
            dv_ref[rows, :] += jnp.concatenate(dv_p, axis=1)
            return tuple(out)

        z0 = (jnp.zeros((tq, HD), f32), jnp.zeros((tq, 1), f32), jnp.zeros((tq, 1), f32))
        res = _sb_walk(nkb, step, (z0, z0))
        dq_ref[...] = jnp.concatenate([res[0][0], res[1][0]], axis=1) * SCALE

    row = pl.BlockSpec((tq, LANE), lambda p, i: (i, p))
    full = pl.BlockSpec((S, LANE), lambda p, i: (0, p))
    osd = jax.ShapeDtypeStruct((S, 384), f32)
    return _call(
        body, name="sb_bwd", grid=(3, S // tq),
        in_specs=[pl.BlockSpec((tq, LANE), lambda p, i: (i, O_BQ // LANE + p)),
                  pl.BlockSpec((S, LANE), lambda p, i: (0, O_BK // LANE + p)),
                  pl.BlockSpec((S, LANE), lambda p, i: (0, O_BV // LANE + p)), row, row],
        out_specs=[row, full, full], out_shape=[osd, osd, osd],
        sem=("parallel", "arbitrary"), args=(u, u, u, yb, dyb), side=side)


def _dil_rows(i, rho, r):
    if r == 1:
        return pl.ds(pl.multiple_of(i * C_BLK, C_BLK), C_BLK)
    return pl.ds(i * (C_BLK * r) + rho, C_BLK, stride=r)


def _dil_scores(qs, kc, kp, i, slope_r):
    qi = lax.broadcasted_iota(jnp.int32, (C_BLK, C_BLK), 0)
    kj = lax.broadcasted_iota(jnp.int32, (C_BLK, C_BLK), 1)
    d_c = qi - kj
    d_p = d_c + C_BLK
    ok_c = d_c >= 0
    ok_p = jnp.logical_and(d_c <= 0, i > 0)
    s_c = jnp.where(ok_c, _dot(qs, kc, NT) - slope_r * d_c.astype(f32), NEG_BIG)
    s_p = jnp.where(ok_p, _dot(qs, kp, NT) - slope_r * d_p.astype(f32), NEG_BIG)
    return s_c, s_p, ok_c, ok_p


def _dil_slope(g, r, hh):
    pair = pl.program_id(0)
    return jnp.where(pair == 0, C_SLOPES[4 * g + hh] * r, C_SLOPES[4 * g + 2 + hh] * r).astype(f32)


def _dil_u_specs(g, S):
    def im(off):
        return lambda p, rho: (0, (off + g * 256) // LANE + p)
    return [pl.BlockSpec((S, LANE), im(O_CQ)), pl.BlockSpec((S, LANE), im(O_CK)), pl.BlockSpec((S, LANE), im(O_CV))]


def dil_fwd(u, g):
    S = u.shape[0]
    r = C_GROUPS[g][1]
    nbk = S // r // C_BLK

    def body(q_ref, k_ref, v_ref, o_ref, l_ref):
        rho = pl.program_id(1)

        def step(i, carry):
            rc = _dil_rows(i, rho, r)
            rp = _dil_rows(jnp.maximum(i - 1, 0), rho, r)
            q2, kc2, kp2, vc2, vp2 = q_ref[rc, :], k_ref[rc, :], k_ref[rp, :], v_ref[rc, :], v_ref[rp, :]
            o_p, l_p = [], []
            for hh in range(2):
                hs = slice(hh * HD, (hh + 1) * HD)
                qs = (q2[:, hs] * SCALE).astype(bf16)
                kc, kp = kc2[:, hs].astype(bf16), kp2[:, hs].astype(bf16)
                vc, vp = vc2[:, hs].astype(bf16), vp2[:, hs].astype(bf16)
                s_c, s_p, _, _ = _dil_scores(qs, kc, kp, i, _dil_slope(g, r, hh))
                m = jnp.maximum(jnp.max(s_c, axis=1, keepdims=True), jnp.max(s_p, axis=1, keepdims=True))
                p_c, p_p = jnp.exp(s_c - m), jnp.exp(s_p - m)
                den = jnp.sum(p_c, axis=1, keepdims=True) + jnp.sum(p_p, axis=1, keepdims=True)
                o_p.append((_dot(p_c.astype(bf16), vc, NN) + _dot(p_p.astype(bf16), vp, NN)) / den)
                l_p.append(jnp.broadcast_to(m + jnp.log(den), (C_BLK, HD)))
            o_ref[rc, :] = jnp.concatenate(o_p, axis=1)
            l_ref[rc, :] = jnp.concatenate(l_p, axis=1)
            return carry

        lax.fori_loop(0, nbk, step, 0)

    ospec = pl.BlockSpec((S, LANE), lambda p, rho: (0, p))
    osd = jax.ShapeDtypeStruct((S, 256), f32)
    return pl.pallas_call(
        body, name=f"dil_fwd{g}", grid=(2, r),
        in_specs=_dil_u_specs(g, S), out_specs=[ospec, ospec], out_shape=[osd, osd],
        compiler_params=_cp(("parallel", "arbitrary")),
    )(u, u, u)


def dil_merge(os_, ls_):
    S = os_[0].shape[0]

    def body(o0, o1, o2, l0, l1, l2, y_ref, lse_ref):
        a, b, c = l0[...], l1[...], l2[...]
        m = jnp.maximum(jnp.maximum(a, b), c)
        ea, eb, ec = jnp.exp(a - m), jnp.exp(b - m), jnp.exp(c - m)
        den = ea + eb + ec
        y_ref[...] = (ea * o0[...] + eb * o1[...] + ec * o2[...]) / den
        lse_ref[...] = m + jnp.log(den)

    spec = pl.BlockSpec((512, 256), lambda i: (i, 0))
    osd = jax.ShapeDtypeStruct((S, 256), f32)
    return pl.pallas_call(
        body, name="dil_merge", grid=(S // 512,), in_specs=[spec] * 6, out_specs=[spec, spec],
        out_shape=[osd, osd], compiler_params=_cp(("parallel",)),
    )(*os_, *ls_)


def dil_bwd(u, g, dyc, yc, lse):
    S = u.shape[0]
    r = C_GROUPS[g][1]
    nbk = S // r // C_BLK

    def body(q_ref, k_ref, v_ref, dy_ref, y_ref, l_ref, dq_ref, dk_ref, dv_ref):
        rho = pl.program_id(1)

        @pl.when(rho == 0)
        def _():
            dk_ref[...] = jnp.zeros_like(dk_ref)
            dv_ref[...] = jnp.zeros_like(dv_ref)

        def step(i, carry):
            rc = _dil_rows(i, rho, r)
            rp = _dil_rows(jnp.maximum(i - 1, 0), rho, r)
            q2, kc2, kp2, vc2, vp2 = q_ref[rc, :], k_ref[rc, :], k_ref[rp, :], v_ref[rc, :], v_ref[rp, :]
            dy2, y2, l2 = dy_ref[rc, :], y_ref[rc, :], l_ref[rc, :]
            dq_p, dkc_p, dkp_p, dvc_p, dvp_p = [], [], [], [], []
            for hh in range(2):
                hs = slice(hh * HD, (hh + 1) * HD)
                qs = (q2[:, hs] * SCALE).astype(bf16)
                kc, kp = kc2[:, hs].astype(bf16), kp2[:, hs].astype(bf16)
                vc, vp = vc2[:, hs].astype(bf16), vp2[:, hs].astype(bf16)
                dy = dy2[:, hs]
                dyb = dy.astype(bf16)
                s_c, s_p, ok_c, ok_p = _dil_scores(qs, kc, kp, i, _dil_slope(g, r, hh))
                lrow = l2[:, hh * HD:hh * HD + 1]
                delta = jnp.sum(dy * y2[:, hs], axis=1, keepdims=True)
                pi_c = jnp.where(ok_c, jnp.exp(s_c - lrow), 0.0)
                pi_p = jnp.where(ok_p, jnp.exp(s_p - lrow), 0.0)
                ds_c = (pi_c * (_dot(dyb, vc, NT) - delta)).astype(bf16)
                ds_p = (pi_p * (_dot(dyb, vp, NT) - delta)).astype(bf16)
                dq_p.append((_dot(ds_c, kc, NN) + _dot(ds_p, kp, NN)) * SCALE)
                dkc_p.append(_dot(ds_c, qs, TN))
                dkp_p.append(_dot(ds_p, qs, TN))
                dvc_p.append(_dot(pi_c.astype(bf16), dyb, TN))
                dvp_p.append(_dot(pi_p.astype(bf16), dyb, TN))
            dq_ref[rc, :] = jnp.concatenate(dq_p, axis=1)
            dk_ref[rc, :] += jnp.concatenate(dkc_p, axis=1)
            dv_ref[rc, :] += jnp.concatenate(dvc_p, axis=1)
            dk_ref[rp, :] += jnp.concatenate(dkp_p, axis=1)
            dv_ref[rp, :] += jnp.concatenate(dvp_p, axis=1)
            return carry

        lax.fori_loop(0, nbk, step, 0)

    ospec = pl.BlockSpec((S, LANE), lambda p, rho: (0, p))
    osd = jax.ShapeDtypeStruct((S, 256), f32)
    return pl.pallas_call(
        body, name=f"dil_bwd{g}", grid=(2, r),
        in_specs=_dil_u_specs(g, S) + [ospec, ospec, ospec], out_specs=[ospec] * 3, out_shape=[osd] * 3,
        compiler_params=_cp(("parallel", "arbitrary")),
    )(u, u, u, dyc, yc, lse)


def _rows_tile(rows):
    return _tile(rows, (256, 176, 128, 64, 32, 16, 8))


def cast_bf16(w):
    shape = w.shape
    w2 = w.reshape(-1, shape[-1])
    rows, cols = w2.shape
    tr = _rows_tile(rows)

    def body(x_ref, o_ref):
        o_ref[...] = x_ref[...].astype(bf16)

    spec = pl.BlockSpec((tr, cols), lambda i: (i, 0))
    out = pl.pallas_call(
        body, name="cast_bf16", grid=(rows // tr,), in_specs=[spec], out_specs=spec,
        out_shape=jax.ShapeDtypeStruct((rows, cols), bf16), compiler_params=_cp(("parallel",)),
    )(w2)
    return out.reshape(shape)


BC1 = 1.0 - ADAM_B1 ** ADAM_STEP
BC2 = 1.0 - ADAM_B2 ** ADAM_STEP


def _adam_math(w, g, m, v):
    m2 = ADAM_B1 * m + (1.0 - ADAM_B1) * g
    v2 = ADAM_B2 * v + (1.0 - ADAM_B2) * (g * g)
    delta = -ADAM_LR * ((m2 / BC1) / (jnp.sqrt(v2 / BC2) + ADAM_EPS) + ADAM_WD * w)
    return delta, m2, v2


def adam(w, g, m, v):
    shape = w.shape
    r2 = lambda t: t.reshape(-1, shape[-1])
    rows, cols = r2(w).shape
    tr = _rows_tile(rows)

    def body(w_ref, g_ref, m_ref, v_ref, d_ref, m2_ref, v2_ref):
        d_ref[...], m2_ref[...], v2_ref[...] = _adam_math(w_ref[...], g_ref[...], m_ref[...], v_ref[...])

    spec = pl.BlockSpec((tr, cols), lambda i: (i, 0))
    osd = jax.ShapeDtypeStruct((rows, cols), f32)
    outs = pl.pallas_call(
        body, name="adam", grid=(rows // tr,), in_specs=[spec] * 4, out_specs=[spec] * 3, out_shape=[osd] * 3,
        compiler_params=_cp(("parallel",)),
    )(r2(w), r2(g), r2(m), r2(v))
    return [o.reshape(shape) for o in outs]


ADA_N = 9 * D // 4
ADA_TN = 384


def ada_fwd(c_all, w_ada):
    def body(c_ref, w_ref, o_ref):
        cv = c_ref[...]
        o_ref[0] = _dot((cv * _sig(cv)).astype(bf16), w_ref[0].astype(bf16), NN)

    return pl.pallas_call(
        body, name="ada_fwd", grid=(DEPTH, ADA_N // ADA_TN),
        in_specs=[pl.BlockSpec((8, D), lambda l, j: (0, 0)), pl.BlockSpec((1, D, ADA_TN), lambda l, j: (l, 0, j))],
        out_specs=pl.BlockSpec((1, 8, ADA_TN), lambda l, j: (l, 0, j)),
        out_shape=jax.ShapeDtypeStruct((DEPTH, 8, ADA_N), f32), compiler_params=_cp(("parallel", "parallel")),
    )(c_all, w_ada)


def ada_bwd_adam(c_all, dm, w, m, v):
    tr = 128

    def body(c_ref, dm_ref, w_ref, m_ref, v_ref, g_ref, d_ref, m2_ref, v2_ref):
        cv = c_ref[...]
        g = _dot((cv * _sig(cv)).astype(bf16), dm_ref[0].astype(bf16), TN)
        g_ref[0] = g
        d_ref[0], m2_ref[0], v2_ref[0] = _adam_math(w_ref[0], g, m_ref[0], v_ref[0])

    wspec = pl.BlockSpec((1, tr, ADA_N), lambda l, i: (l, i, 0))
    osd = jax.ShapeDtypeStruct((DEPTH, D, ADA_N), f32)
    return pl.pallas_call(
        body, name="ada_bwd_adam", grid=(DEPTH, D // tr),
        in_specs=[pl.BlockSpec((8, tr), lambda l, i: (0, i)), pl.BlockSpec((1, 8, ADA_N), lambda l, i: (l, 0, 0)),
                  wspec, wspec, wspec],
        out_specs=[wspec] * 4, out_shape=[osd] * 4, compiler_params=_cp(("parallel", "parallel")),
    )(c_all, dm, w, m, v)


def _lb_probs(x):
    mx = jnp.max(x, axis=0, keepdims=True)
    e = jnp.exp(x - mx)
    return e / jnp.sum(e, axis=0, keepdims=True)


def lb_fwd(logits):
    def body(x_ref, o_ref):
        p = _lb_probs(x_ref[...])
        rows = [jnp.zeros((1, 768), f32)]
        for l in range(1, DEPTH):
            rows.append(rows[-1] + p[l:l + 1, :])
        o_ref[...] = jnp.concatenate(rows, axis=0)

    return pl.pallas_call(body, name="lb_fwd", out_shape=jax.ShapeDtypeStruct((DEPTH, 768), f32))(logits)


def lb_bwd(logits, dlb):
    def body(x_ref, d_ref, o_ref):
        p = _lb_probs(x_ref[...])
        d = d_ref[...]
        rows = [jnp.zeros((1, 768), f32)] * DEPTH
        acc = jnp.zeros((1, 768), f32)
        for l in range(DEPTH - 1, 0, -1):
            acc = acc + d[l:l + 1, :]
            rows[l] = acc
        dp = jnp.concatenate(rows, axis=0)
        o_ref[...] = p * (dp - jnp.sum(p * dp, axis=0, keepdims=True))

    return pl.pallas_call(body, name="lb_bwd", out_shape=jax.ShapeDtypeStruct((DEPTH, 768), f32))(logits, dlb)


def sum_slots(x):
    n, rows, cols = x.shape
    tr = _rows_tile(rows)

    def body(x_ref, o_ref):
        acc = x_ref[0]
        for j in range(1, n):
            acc = acc + x_ref[j]
        o_ref[...] = acc

    return pl.pallas_call(
        body, name="sum_slots", grid=(rows // tr,),
        in_specs=[pl.BlockSpec((n, tr, cols), lambda i: (0, i, 0))], out_specs=pl.BlockSpec((tr, cols), lambda i: (i, 0)),
        out_shape=jax.ShapeDtypeStruct((rows, cols), f32), compiler_params=_cp(("parallel",)),
    )(x)


ANY = pl.BlockSpec(memory_space=pl.ANY)
CHIP_FLIPS = ((1, 0), (0, 1), (1, 1))
DEV_FLIPS = tuple((a, b, d) for a in (0, 1) for b in (0, 1) for d in (0, 1))[1:]


def _me():
    return lax.axis_index("x"), lax.axis_index("y"), lax.axis_index("c")


def _flip(v, f):
    return 1 - v if f else v


def _comm_call(body, name, ins, out_shapes, n_remote, n_local):
    return pl.pallas_call(
        body, name=name, in_specs=[ANY] * len(ins), out_specs=[ANY] * len(out_shapes), out_shape=out_shapes,
        scratch_shapes=[pltpu.SemaphoreType.DMA((n_remote,)), pltpu.SemaphoreType.DMA((n_remote,)),
                        pltpu.SemaphoreType.DMA((max(n_local, 1),))],
    )(*ins)


def run_plan(plan, name):
    ni, no = len(plan.ins), len(plan.outs)

    def body(*refs):
        ins, outs, sems = refs[:ni], refs[ni:ni + no], refs[ni + no:]
        plan.start(ins, outs, *sems)
        plan.wait(ins, outs, *sems)

    return pl.pallas_call(body, name=name, in_specs=[ANY] * ni, out_specs=[ANY] * no, out_shape=list(plan.outs),
                          scratch_shapes=plan.sems())(*plan.ins)


def gather_chips_plan(arrs, layer=None):
    n = len(arrs)
    shapes = [a.shape if layer is None else a.shape[1:] for a in arrs]

    def copies(ins, outs, send, recv, loc):
        x, y, c = _me()
        mine = 2 * x + y
        srcs = [r if layer is None else r.at[layer] for r in ins]
        locs = [pltpu.make_async_copy(srcs[a], outs[a].at[mine], loc.at[a]) for a in range(n)]

        def remote(a, k, slot):
            fx, fy = CHIP_FLIPS[k]
            return pltpu.make_async_remote_copy(srcs[a], outs[a].at[slot], send.at[3 * a + k], recv.at[3 * a + k],
                                                device_id=(_flip(x, fx), _flip(y, fy), c), device_id_type=MESH)

        peers = [2 * _flip(x, fx) + _flip(y, fy) for fx, fy in CHIP_FLIPS]
        return locs, remote, mine, peers

    def start(ins, outs, send, recv, loc):
        locs, remote, mine, _ = copies(ins, outs, send, recv, loc)
        for cp in locs:
            cp.start()
        for a in range(n):
            for k in range(3):
                remote(a, k, mine).start()

    def wait(ins, outs, send, recv, loc):
        locs, remote, _, peers = copies(ins, outs, send, recv, loc)
        for a in range(n):
            for k in range(3):
                cp = remote(a, k, peers[k])
                cp.wait_recv()
                cp.wait_send()
        for cp in locs:
            cp.wait()

    outs = [jax.ShapeDtypeStruct((4,) + tuple(s), a.dtype) for s, a in zip(shapes, arrs)]
    return Plan(list(arrs), outs, 3 * n, n, start, wait)


def all_gather_chips(arrs, layer=None, name="ag4"):
    return run_plan(gather_chips_plan(arrs, layer), name)


def all_gather_devs(arr, name="ag8"):
    def body(in_ref, out_ref, send, recv, loc):
        x, y, c = _me()
        mine = 4 * x + 2 * y + c
        lc = pltpu.make_async_copy(in_ref, out_ref.at[mine], loc.at[0])
        lc.start()

        def remote(k, slot):
            fx, fy, fc = DEV_FLIPS[k]
            return pltpu.make_async_remote_copy(in_ref, out_ref.at[slot], send.at[k], recv.at[k],
                                                device_id=(_flip(x, fx), _flip(y, fy), _flip(c, fc)), device_id_type=MESH)

        for k in range(7):
            remote(k, mine).start()
        for k, (fx, fy, fc) in enumerate(DEV_FLIPS):
            cp = remote(k, 4 * _flip(x, fx) + 2 * _flip(y, fy) + _flip(c, fc))
            cp.wait_recv()
            cp.wait_send()
        lc.wait()

    return _comm_call(body, name, [arr], [jax.ShapeDtypeStruct((8,) + arr.shape, arr.dtype)], 7, 1)[0]


def _rows_of(which, rows):
    return pl.ds(pl.multiple_of(which * rows, 16), rows)


def dev_exchange_plan(parts):
    n = len(parts)

    def copies(ins, outs, send, recv, loc):
        x, y, c = _me()
        mine = 4 * x + 2 * y + c

        def piece(a, px, py, pc):
            rows = ins[a].shape[1] // 2
            return ins[a].at[2 * px + py, _rows_of(pc, rows), :]

        locs = [pltpu.make_async_copy(piece(a, x, y, c), outs[a].at[mine], loc.at[a]) for a in range(n)]

        def remote(a, k, slot):
            fx, fy, fc = DEV_FLIPS[k]
            px, py, pc = _flip(x, fx), _flip(y, fy), _flip(c, fc)
            return pltpu.make_async_remote_copy(piece(a, px, py, pc), outs[a].at[slot], send.at[7 * a + k], recv.at[7 * a + k],
                                                device_id=(px, py, pc), device_id_type=MESH)

        peers = [4 * _flip(x, fx) + 2 * _flip(y, fy) + _flip(c, fc) for fx, fy, fc in DEV_FLIPS]
        return locs, remote, mine, peers

    def start(ins, outs, send, recv, loc):
        locs, remote, mine, _ = copies(ins, outs, send, recv, loc)
        for cp in locs:
            cp.start()
        for a in range(n):
            for k in range(7):
                remote(a, k, mine).start()

    def wait(ins, outs, send, recv, loc):
        locs, remote, _, peers = copies(ins, outs, send, recv, loc)
        for a in range(n):
            for k in range(7):
                cp = remote(a, k, peers[k])
                cp.wait_recv()
                cp.wait_send()
        for cp in locs:
            cp.wait()

    outs = [jax.ShapeDtypeStruct((8, p.shape[1] // 2, p.shape[2]), p.dtype) for p in parts]
    return Plan(list(parts), outs, 7 * n, n, start, wait)


def sum_share(slots, name="rs_sum"):
    n, r, cols = slots.shape
    tr = _tile(r, (128, 176, 64))
    steps = r // tr

    def body(s_ref, g_ref, buf, send, loc, recv):
        i = pl.program_id(0)
        x, y, c = _me()
        slot = i % 2

        def copies(step, sl):
            rows = pl.ds(pl.multiple_of(c * r + step * tr, 8), tr)
            rem = pltpu.make_async_remote_copy(buf.at[sl], g_ref.at[rows, :], send.at[sl], recv.at[0],
                                               device_id=(x, y, 1 - c), device_id_type=MESH)
            return rem, pltpu.make_async_copy(buf.at[sl], g_ref.at[rows, :], loc.at[sl])

        @pl.when(i >= 2)
        def _():
            rem, lc = copies(i - 2, slot)
            rem.wait_send()
            lc.wait()

        acc = s_ref[0].astype(f32)
        for j in range(1, n):
            acc = acc + s_ref[j].astype(f32)
        buf[slot] = acc
        rem, lc = copies(i, slot)
        rem.start()
        lc.start()

        @pl.when(i == steps - 1)
        def _():
            for back in range(min(2, steps)):
                rem, lc = copies(i - back, (i - back) % 2)
                rem.wait_send()
                lc.wait()
            other = g_ref.at[pl.ds(pl.multiple_of((1 - c) * r, 8), r), :]
            pltpu.make_async_remote_copy(other, other, send.at[0], recv.at[0],
                                         device_id=(x, y, 1 - c), device_id_type=MESH).wait_recv()

    return pl.pallas_call(
        body, name=name, grid=(steps,),
        in_specs=[pl.BlockSpec((n, tr, cols), lambda i: (0, i, 0))], out_specs=ANY,
        out_shape=jax.ShapeDtypeStruct((2 * r, cols), f32),
        scratch_shapes=[pltpu.VMEM((2, tr, cols), f32), pltpu.SemaphoreType.DMA((2,)), pltpu.SemaphoreType.DMA((2,)),
                        pltpu.SemaphoreType.DMA((1,))],
        compiler_params=_cp(("arbitrary",)),
    )(slots)


BIG = ("ffn1_w_in", "ffn1_w_out", "w_in", "w_branch_a", "w_branch_b", "w_branch_c", "w_out", "ffn2_w_in", "ffn2_w_out")
ROW_SHARDED = ("ffn1_w_out", "w_out", "ffn2_w_out")
RES_W = (0.5, 1.0, 0.5)


def _full_weight(name, g):
    if name in ROW_SHARDED:
        return g.reshape(4 * g.shape[1], g.shape[2])
    return jnp.concatenate([g[0], g[1], g[2], g[3]], axis=1)


def _by_shard(name, dw):
    if name in ROW_SHARDED:
        return dw.reshape(4, dw.shape[0] // 4, dw.shape[1])
    return dw.reshape(dw.shape[0], 4, dw.shape[1] // 4).transpose(1, 0, 2)


def _full_weight_t(name, g):
    if name in ROW_SHARDED:
        return g.reshape(4 * g.shape[1], g.shape[2]).T
    return g.transpose(0, 2, 1).reshape(4 * g.shape[2], g.shape[1])


def _ffn_fwd(x, w_in, w_out, a_vec, sh_vec, b_vec):
    h = prenorm(x, a_vec, sh_vec)
    ua, ub, s = ffn_in_swiglu(h, w_in)
    y = mm(s, w_out, name="ffn_out")
    return postnorm(x, y, b_vec), (x, h, ua, ub, s, y)


def _ffn_bwd(dout, saved, w_in_t, w_out_t, a_vec, b_vec):
    x, h, ua, ub, s, y = saved
    dy, db = post_bwd(dout, y, b_vec)
    dw_out = mm(s.T, dy, out_dtype=bf16, name="ffn_dwo")
    du = ffn_du(dy, w_out_t, ua, ub)
    dh = mm(du, w_in_t, name="ffn_dh")
    dw_in = mm(h.T, du, out_dtype=bf16, name="ffn_dwi")
    dx, dsh, da = pre_bwd(dout, dh, x, a_vec)
    return dx, dw_in, dw_out, dsh, da, db


def _mix_fwd(x, w, lb, ng, a_vec, sh_vec, b_vec, plans=(None, None, None)):
    h = prenorm(x, a_vec, sh_vec)
    u = mm(h, w["w_in"], name="mix_in", side=plans[0])
    side0 = None
    if plans[0] is not None:
        u, side0 = u
    (o, ya, st), side1 = hgrn_fwd(u, lb, ng, side=plans[1])
    yb, side2 = sb_fwd(u, side=plans[2])
    groups = [dil_fwd(u, g) for g in range(3)]
    yc, lse = dil_merge([o_ for o_, _ in groups], [l_ for _, l_ in groups])
    pa = mm(ya, w["w_branch_a"], name="mix_pa")
    pb = mm(yb, w["w_branch_b"], name="mix_pb")
    pc = mm(yc, w["w_branch_c"], name="mix_pc")
    merged = gate_merge(u, pa, pb, pc)
    z = mm(merged, w["w_out"], name="mix_out")
    return postnorm(x, z, b_vec), (x, h, u, o, ya, st, yb, yc, lse, pa, pb, pc, merged, z), (side0, side1, side2)


def _mix_bwd(dout, saved, wt, lb, ng, a_vec, b_vec, plans=(None, None)):
    x, h, u, o, ya, st, yb, yc, lse, pa, pb, pc, merged, z = saved
    dz, db = post_bwd(dout, z, b_vec)
    dmerged = mm(dz, wt["w_out"], name="mix_dm")
    dw_out = mm(merged.T, dz, out_dtype=bf16, name="mix_dwo")
    dpa, dpb, dpc, dg0, dg1, dg2 = gate_bwd(dmerged, u, pa, pb, pc)
    dya = mm(dpa, wt["w_branch_a"], name="mix_dya")
    dyb = mm(dpb, wt["w_branch_b"], name="mix_dyb")
    dyc = mm(dpc, wt["w_branch_c"], name="mix_dyc")
    dw_a = mm(ya.T, dpa, out_dtype=bf16, name="mix_dwa")
    dw_b = mm(yb.astype(bf16).T, dpb, out_dtype=bf16, name="mix_dwb")
    dw_c = mm(yc.astype(bf16).T, dpc, out_dtype=bf16, name="mix_dwc")
    (daq, daf, dai, dag, dlb, dng), side0 = hgrn_bwd(u, lb, ng, o, st, dya, side=plans[0])
    (dbq, dbk, dbv), side1 = sb_bwd(u, yb, dyb, side=plans[1])
    dc = [dil_bwd(u, g, dyc, yc, lse) for g in range(3)]
    du = jnp.concatenate(
        [daq, daf, dai, dag] + [t.astype(bf16) for t in (dbq, dbk, dbv)]
        + [dc[g][j].astype(bf16) for j in range(3) for g in range(3)] + [dg0, dg1, dg2], axis=1)
    dh = mm(du, wt["w_in"], name="mix_dh")
    dw_in = mm(h.T, du, out_dtype=bf16, name="mix_dwi")
    dx, dsh, da = pre_bwd(dout, dh, x, a_vec)
    grads = {"w_in": dw_in, "w_out": dw_out, "w_branch_a": dw_a, "w_branch_b": dw_b, "w_branch_c": dw_c}
    return dx, grads, dlb, jnp.sum(dng, axis=0), dsh, da, db, (side0, side1)


FWD_RIDERS = (("ffn1_w_in", "w_out"), ("w_in", "ffn1_w_out"),
              ("ffn2_w_in", "ffn2_w_out", "w_branch_a", "w_branch_b", "w_branch_c"))
BWD_RIDERS = (("ffn2_w_in", "ffn1_w_out", "ffn2_w_out"), ("w_in", "ffn1_w_in", "w_out", "w_branch_a", "w_branch_b", "w_branch_c"))


def _reduce_to_shards(names, grads):
    slots = run_plan(dev_exchange_plan([_by_shard(n, grads[n]) for n in names]), "rs_x8")
    return {n: sum_share(s) for n, s in zip(names, slots)}


def kernel(x, c, w_ada, b_ada, norm_g, ffn1_w_in, ffn1_w_out, w_in, hgrn_lb_logits, hgrn_norm_g, w_branch_a, w_branch_b, w_branch_c, w_out, ffn2_w_in, ffn2_w_out, loss_target, m_w_ada, m_b_ada, m_norm_g, m_ffn1_w_in, m_ffn1_w_out, m_w_in, m_hgrn_lb_logits, m_hgrn_norm_g, m_w_branch_a, m_w_branch_b, m_w_branch_c, m_w_out, m_ffn2_w_in, m_ffn2_w_out, v_w_ada, v_b_ada, v_norm_g, v_ffn1_w_in, v_ffn1_w_out, v_w_in, v_hgrn_lb_logits, v_hgrn_norm_g, v_w_branch_a, v_w_branch_b, v_w_branch_c, v_w_out, v_ffn2_w_in, v_ffn2_w_out):
    weights = dict(w_ada=w_ada, b_ada=b_ada, norm_g=norm_g, ffn1_w_in=ffn1_w_in, ffn1_w_out=ffn1_w_out, w_in=w_in,
                   hgrn_lb_logits=hgrn_lb_logits, hgrn_norm_g=hgrn_norm_g, w_branch_a=w_branch_a, w_branch_b=w_branch_b,
                   w_branch_c=w_branch_c, w_out=w_out, ffn2_w_in=ffn2_w_in, ffn2_w_out=ffn2_w_out)
    mom = dict(w_ada=m_w_ada, b_ada=m_b_ada, norm_g=m_norm_g, ffn1_w_in=m_ffn1_w_in, ffn1_w_out=m_ffn1_w_out, w_in=m_w_in,
               hgrn_lb_logits=m_hgrn_lb_logits, hgrn_norm_g=m_hgrn_norm_g, w_branch_a=m_w_branch_a, w_branch_b=m_w_branch_b,
               w_branch_c=m_w_branch_c, w_out=m_w_out, ffn2_w_in=m_ffn2_w_in, ffn2_w_out=m_ffn2_w_out)
    var = dict(w_ada=v_w_ada, b_ada=v_b_ada, norm_g=v_norm_g, ffn1_w_in=v_ffn1_w_in, ffn1_w_out=v_ffn1_w_out, w_in=v_w_in,
               hgrn_lb_logits=v_hgrn_lb_logits, hgrn_norm_g=v_hgrn_norm_g, w_branch_a=v_w_branch_a, w_branch_b=v_w_branch_b,
               w_branch_c=v_w_branch_c, w_out=v_w_out, ffn2_w_in=v_ffn2_w_in, ffn2_w_out=v_ffn2_w_out)
    order = list(weights)
    xi, yi, ci = _me()
    chip = 2 * xi + yi
    dev = 4 * xi + 2 * yi + ci
    xs = x[0]

    c_all = all_gather_devs(c, name="ag8_c").reshape(8, D)
    mod_sh = all_gather_chips([ada_fwd(c_all, w_ada)], name="ag4_mod")[0]
    mod_all = mod_sh.transpose(1, 2, 0, 3).reshape(DEPTH, 8, 9 * D)
    mod = lax.dynamic_index_in_dim(mod_all, dev, axis=1, keepdims=False) + b_ada
    mod = mod.reshape(DEPTH, 3, 3, D)
    ng_all = all_gather_chips([norm_g.reshape(DEPTH * 6, D // 4)], name="ag4_norm")[0]
    ng_all = ng_all.reshape(4, DEPTH, 6, D // 4).transpose(1, 2, 0, 3).reshape(DEPTH, 6, D)
    lb_all = lb_fwd(hgrn_lb_logits)
    w16 = {n: cast_bf16(weights[n]) for n in BIG}

    def vecs(l, i):
        shift, scale, gate = mod[l, i, 0][None], mod[l, i, 1][None], mod[l, i, 2][None]
        g_pre, g_post = ng_all[l, 2 * i][None], ng_all[l, 2 * i + 1][None]
        return g_pre * (1.0 + scale), shift, RES_W[i] * gate * g_post

    saved, full = [], []
    gathered = dict(zip(BIG, all_gather_chips([w16[n] for n in BIG], layer=0, name="ag4_w0")))
    for l in range(DEPTH):
        w = {n: _full_weight(n, gathered[n]) for n in BIG}
        full.append({n: _full_weight_t(n, gathered[n]) for n in BIG})
        lb, ng = lb_all[l][None], hgrn_norm_g[l][None]
        plans = (None, None, None)
        if l + 1 < DEPTH:
            plans = tuple(gather_chips_plan([w16[n] for n in names], layer=l + 1) for names in FWD_RIDERS)
        xs, s1 = _ffn_fwd(xs, w["ffn1_w_in"], w["ffn1_w_out"], *vecs(l, 0))
        xs, s2, sides = _mix_fwd(xs, w, lb, ng, *vecs(l, 1), plans=plans)
        xs, s3 = _ffn_fwd(xs, w["ffn2_w_in"], w["ffn2_w_out"], *vecs(l, 2))
        saved.append((s1, s2, s3))
        if l + 1 < DEPTH:
            gathered = {n: g for names, outs in zip(FWD_RIDERS, sides) for n, g in zip(names, outs)}

    dx, loss_part = loss_grad(xs, loss_target[0])
    loss = lax.psum(loss_part[0, 0], ("x", "y", "c"))

    big_grads = {n: [None] * DEPTH for n in BIG}
    d_mod, d_ng, d_lb, d_hng = [None] * DEPTH, [None] * DEPTH, [None] * DEPTH, [None] * DEPTH
    pending = None
    for l in reversed(range(DEPTH)):
        wt = full[l]
        s1, s2, s3 = saved[l]
        lb, ng = lb_all[l][None], hgrn_norm_g[l][None]
        rows_mod, rows_ng = [None] * 9, [None] * 6

        def vec_grads(i, dsh, da, db):
            scale, gate = mod[l, i, 1][None], mod[l, i, 2][None]
            g_pre, g_post = ng_all[l, 2 * i][None], ng_all[l, 2 * i + 1][None]
            rows_mod[3 * i], rows_mod[3 * i + 1], rows_mod[3 * i + 2] = dsh, g_pre * da, RES_W[i] * g_post * db
            rows_ng[2 * i], rows_ng[2 * i + 1] = (1.0 + scale) * da, RES_W[i] * gate * db

        a3, _, b3 = vecs(l, 2)
        dx, dwi, dwo, dsh, da, db = _ffn_bwd(dx, s3, wt["ffn2_w_in"], wt["ffn2_w_out"], a3, b3)
        vec_grads(2, dsh, da, db)
        grads = {"ffn2_w_in": dwi, "ffn2_w_out": dwo}
        a2, _, b2 = vecs(l, 1)
        plans = (None, None)
        if pending is not None:
            plans = tuple(dev_exchange_plan([pending[n] for n in names]) for names in BWD_RIDERS)
        dx, gmix, dlb, dhng, dsh, da, db, sides = _mix_bwd(dx, s2, wt, lb, ng, a2, b2, plans=plans)
        if pending is not None:
            for names, outs in zip(BWD_RIDERS, sides):
                for n, slots in zip(names, outs):
                    big_grads[n][l + 1] = sum_share(slots)
        vec_grads(1, dsh, da, db)
        grads.update(gmix)
        a1, _, b1 = vecs(l, 0)
        dx, dwi, dwo, dsh, da, db = _ffn_bwd(dx, s1, wt["ffn1_w_in"], wt["ffn1_w_out"], a1, b1)
        vec_grads(0, dsh, da, db)
        grads.update({"ffn1_w_in": dwi, "ffn1_w_out": dwo})
        pending = {n: _by_shard(n, grads[n]) for n in BIG}
        d_mod[l] = jnp.concatenate(rows_mod, axis=1)
        d_ng[l] = jnp.concatenate(rows_ng, axis=0)
        d_lb[l], d_hng[l] = dlb, dhng

    slots = run_plan(dev_exchange_plan([pending[n] for n in BIG]), "rs_x8")
    for n, s in zip(BIG, slots):
        big_grads[n][0] = sum_share(s)

    n_small = 6 * D * DEPTH + 768 * DEPTH + A_V * DEPTH + 9 * D * DEPTH
    pad = -n_small % (512 * LANE)
    flat = jnp.concatenate([jnp.stack(d_ng).reshape(-1), jnp.concatenate(d_lb, axis=0).reshape(-1),
                            jnp.concatenate(d_hng, axis=0).reshape(-1), jnp.concatenate(d_mod, axis=0).reshape(-1),
                            jnp.zeros((pad,), f32)])
    small_all = all_gather_devs(flat.reshape(-1, LANE), name="ag8_small")
    total = sum_slots(small_all).reshape(-1)
    o1 = 6 * D * DEPTH
    o2 = o1 + 768 * DEPTH
    o3 = o2 + A_V * DEPTH
    g_ng_full = total[:o1].reshape(DEPTH, 6, D)
    g_lb_all = total[o1:o2].reshape(DEPTH, 768)
    g_small = {
        "norm_g": lax.dynamic_slice_in_dim(g_ng_full, chip * (D // 4), D // 4, axis=2),
        "hgrn_lb_logits": lb_bwd(hgrn_lb_logits, g_lb_all),
        "hgrn_norm_g": total[o2:o3].reshape(DEPTH, A_V),
        "b_ada": total[o3:n_small].reshape(DEPTH, 9 * D),
    }
    dmod_all = small_all.reshape(8, -1)[:, o3:n_small].reshape(8, DEPTH, 9 * D).transpose(1, 0, 2)
    dm_sh = lax.dynamic_slice_in_dim(dmod_all, chip * ADA_N, ADA_N, axis=2)

    out_g, out_d, out_m, out_v = {}, {}, {}, {}
    out_g["w_ada"], out_d["w_ada"], out_m["w_ada"], out_v["w_ada"] = ada_bwd_adam(c_all, dm_sh, w_ada, m_w_ada, v_w_ada)
    for n in BIG:
        out_g[n] = jnp.stack(big_grads[n])
    out_g.update(g_small)
    for n in order:
        if n != "w_ada":
            out_d[n], out_m[n], out_v[n] = adam(weights[n], out_g[n], mom[n], var[n])
    return (loss, dx[None], *[out_g[n] for n in order], *[out_d[n] for n in order],
            *[out_m[n] for n in order], *[out_v[n] for n in order])
```

```python
import functools
import math

import jax
import jax.numpy as jnp
from jax import lax
from jax.experimental import pallas as pl
from jax.experimental.pallas import tpu as pltpu

f32, bf16 = jnp.float32, jnp.bfloat16

D = 1024
DEPTH = 4
D_FF = 2816
EPS = 1e-6
NEG_BIG = -1e30
TINY = 1e-30
A_HEADS, A_K, A_V, A_CHUNK = 6, 128, 64, 64
A_SUB = 16
A_CLAMP = 80.0
B_HEADS, HD = 6, 64
C_GROUPS = ((128, 1), (512, 4), (2048, 16))
C_BLK = 128
IN_COLS = 8832
O_AQ, O_AF, O_AI, O_AG = 0, 768, 1536, 1920
O_BQ, O_BK, O_BV = 2304, 2688, 3072
O_CQ, O_CK, O_CV = 3456, 4224, 4992
O_GATE = 5760
LANE = 128
ADAM_LR, ADAM_B1, ADAM_B2, ADAM_EPS, ADAM_WD, ADAM_STEP = 0.001, 0.9, 0.999, 1e-08, 0.01, 10
MESH = pl.DeviceIdType.MESH
VMEM_LIMIT = 56 * 1024 * 1024


def _alibi_slopes(n):
    def pow2(m):
        start = 2.0 ** (-8.0 / m)
        return [start ** (i + 1) for i in range(m)]
    if math.log2(n).is_integer():
        s = pow2(n)
    else:
        c = 2 ** int(math.floor(math.log2(n)))
        s = pow2(c) + pow2(2 * c)[0::2][: n - c]
    return sorted(s, reverse=True)


C_SLOPES = _alibi_slopes(12)


def _tile(n, prefs):
    for p in prefs:
        if n % p == 0:
            return p
    return n


def _cp(sem):
    return pltpu.CompilerParams(dimension_semantics=sem, vmem_limit_bytes=VMEM_LIMIT)


def _sig(x):
    return 1.0 / (1.0 + jnp.exp(-x))


def _dot(a, b, dn, precision=None):
    return lax.dot_general(a, b, (dn, ((), ())), preferred_element_type=f32, precision=precision)


NN = ((1,), (0,))
NT = ((1,), (1,))
TN = ((0,), (0,))


class Plan:
    def __init__(self, ins, outs, n_remote, n_local, start, wait):
        self.ins, self.outs, self.n_remote, self.n_local, self.start, self.wait = ins, outs, n_remote, n_local, start, wait

    def sems(self):
        return [pltpu.SemaphoreType.DMA((self.n_remote,)), pltpu.SemaphoreType.DMA((self.n_remote,)),
                pltpu.SemaphoreType.DMA((max(self.n_local, 1),))]


def _call(body, *, name, grid, in_specs, out_specs, out_shape, sem, args, scratch_shapes=(), side=None):
    if side is None:
        return pl.pallas_call(body, name=name, grid=grid, in_specs=in_specs, out_specs=out_specs, out_shape=out_shape,
                              scratch_shapes=list(scratch_shapes), compiler_params=_cp(sem))(*args), None
    any_spec = pl.BlockSpec(memory_space=pl.ANY)
    n_in, n_out, n_scr = len(in_specs), len(out_specs), len(scratch_shapes)
    s_in, s_out = len(side.ins), len(side.outs)

    def hosted(*refs):
        ins, rest = refs[:n_in], refs[n_in:]
        sins, rest = rest[:s_in], rest[s_in:]
        outs, rest = rest[:n_out], rest[n_out:]
        souts, rest = rest[:s_out], rest[s_out:]
        scr, sems = rest[:n_scr], rest[n_scr:]
        pids = [pl.program_id(d) for d in range(len(grid))]
        first = functools.reduce(jnp.logical_and, [p == 0 for p in pids])
        last = functools.reduce(jnp.logical_and, [p == g - 1 for p, g in zip(pids, grid)])

        @pl.when(first)
        def _():
            side.start(sins, souts, *sems)

        body(*ins, *outs, *scr)

        @pl.when(last)
        def _():
            side.wait(sins, souts, *sems)

    res = pl.pallas_call(
        hosted, name=name, grid=grid, in_specs=list(in_specs) + [any_spec] * s_in,
        out_specs=list(out_specs) + [any_spec] * s_out, out_shape=list(out_shape) + list(side.outs),
        scratch_shapes=list(scratch_shapes) + side.sems(), compiler_params=_cp(("arbitrary",) * len(grid)),
    )(*args, *side.ins)
    return res[:n_out], res[n_out:]


MM_TILES = {
    (4096, 1024, 2816): (1024, 512, 2816),
    (4096, 1024, 5632): (512, 512, 5632),
    (1024, 5632, 4096): (512, 512, 4096),
    (2816, 1024, 4096): (704, 512, 4096),
    (4096, 8832, 1024): (512, 2944, 1024),
    (4096, 1024, 8832): (1024, 512, 2944),
    (1024, 8832, 4096): (512, 2944, 1024),
    (1024, 1024, 4096): (512, 512, 4096),
    (384, 1024, 4096): (384, 512, 4096),
    (256, 1024, 4096): (256, 512, 4096),
}


def mm(a, b, *, out_dtype=f32, name="mm", side=None):
    M, K = a.shape
    K2, N = b.shape
    assert K == K2, (a.shape, b.shape)
    tm, tn, tk = MM_TILES.get((M, N, K), (_tile(M, (1024, 704, 512, 384, 256, 128)), _tile(N, (512, 384, 256, 128)),
                                          _tile(K, (1024, 512, 1408, 384, 256, 128))))
    nk = K // tk

    def body(a_ref, b_ref, o_ref, *acc):
        p = _dot(a_ref[...].astype(bf16), b_ref[...].astype(bf16), NN)
        if nk == 1:
            o_ref[...] = p.astype(out_dtype)
            return
        acc_ref, = acc
        k = pl.program_id(2)

        @pl.when(k == 0)
        def _():
            acc_ref[...] = p

        @pl.when(k > 0)
        def _():
            acc_ref[...] += p

        @pl.when(k == nk - 1)
        def _():
            o_ref[...] = acc_ref[...].astype(out_dtype)

    outs, souts = _call(
        body, name=name, grid=(M // tm, N // tn, nk),
        in_specs=[pl.BlockSpec((tm, tk), lambda i, j, k: (i, k)), pl.BlockSpec((tk, tn), lambda i, j, k: (k, j))],
        out_specs=[pl.BlockSpec((tm, tn), lambda i, j, k: (i, j))],
        out_shape=[jax.ShapeDtypeStruct((M, N), out_dtype)],
        scratch_shapes=[pltpu.VMEM((tm, tn), f32)] if nk > 1 else [],
        sem=("parallel", "parallel", "arbitrary"), args=(a, b), side=side)
    return outs[0] if side is None else (outs[0], souts)


FF_TM = 1024
FF_T = 256
FF_NB = D_FF // FF_T


def ffn_in_swiglu(h, w_in):
    S = h.shape[0]

    def body(h_ref, wa_ref, wb_ref, a_ref, b_ref, s_ref):
        hv = h_ref[...]
        a = _dot(hv, wa_ref[...], NN)
        b = _dot(hv, wb_ref[...], NN)
        a_ref[...] = a.astype(bf16)
        b_ref[...] = b.astype(bf16)
        s_ref[...] = (a * _sig(a) * b).astype(bf16)

    ospec = pl.BlockSpec((FF_TM, FF_T), lambda i, j: (i, j))
    osd = jax.ShapeDtypeStruct((S, D_FF), bf16)
    return pl.pallas_call(
        body, name="ffn_in", grid=(S // FF_TM, FF_NB),
        in_specs=[pl.BlockSpec((FF_TM, D), lambda i, j: (i, 0)), pl.BlockSpec((D, FF_T), lambda i, j: (0, j)),
                  pl.BlockSpec((D, FF_T), lambda i, j: (0, j + FF_NB))],
        out_specs=[ospec] * 3, out_shape=[osd] * 3, compiler_params=_cp(("parallel", "parallel")),
    )(h, w_in, w_in)


def ffn_du(dy, w_out_t, ua, ub):
    S = dy.shape[0]
    tm = 512

    def body(dy_ref, w_ref, a_ref, b_ref, du_ref):
        dyv = dy_ref[...]
        for j in range(FF_NB):
            cols = slice(j * FF_T, (j + 1) * FF_T)
            ds = _dot(dyv, w_ref[:, cols], NN)
            a, b = a_ref[:, cols].astype(f32), b_ref[:, cols].astype(f32)
            sg = _sig(a)
            du_ref[:, cols] = (ds * b * sg * (1.0 + a * (1.0 - sg))).astype(bf16)
            du_ref[:, D_FF + j * FF_T:D_FF + (j + 1) * FF_T] = (ds * a * sg).astype(bf16)

    half = pl.BlockSpec((tm, D_FF), lambda i: (i, 0))
    return pl.pallas_call(
        body, name="ffn_du", grid=(S // tm,),
        in_specs=[pl.BlockSpec((tm, D), lambda i: (i, 0)), pl.BlockSpec((D, D_FF), lambda i: (0, 0)), half, half],
        out_specs=pl.BlockSpec((tm, 2 * D_FF), lambda i: (i, 0)),
        out_shape=jax.ShapeDtypeStruct((S, 2 * D_FF), bf16), compiler_params=_cp(("parallel",)),
    )(dy, w_out_t, ua, ub)


TR = 256


def _row_spec(cols=D):
    return pl.BlockSpec((TR, cols), lambda i: (i, 0))


def _vec_spec(cols=D):
    return pl.BlockSpec((1, cols), lambda i: (0, 0))


def prenorm(x, a_vec, sh_vec):
    S = x.shape[0]

    def body(x_ref, a_ref, s_ref, h_ref):
        xv = x_ref[...]
        rstd = lax.rsqrt(jnp.mean(xv * xv, axis=1, keepdims=True) + EPS)
        h_ref[...] = (xv * rstd * a_ref[...] + s_ref[...]).astype(bf16)

    return pl.pallas_call(
        body, name="prenorm", grid=(S // TR,),
        in_specs=[_row_spec(), _vec_spec(), _vec_spec()], out_specs=_row_spec(),
        out_shape=jax.ShapeDtypeStruct((S, D), bf16), compiler_params=_cp(("parallel",)),
    )(x, a_vec, sh_vec)


def postnorm(x, y, b_vec):
    S = x.shape[0]

    def body(x_ref, y_ref, b_ref, o_ref):
        yv = y_ref[...]
        rstd = lax.rsqrt(jnp.mean(yv * yv, axis=1, keepdims=True) + EPS)
        o_ref[...] = x_ref[...] + b_ref[...] * (yv * rstd)

    return pl.pallas_call(
        body, name="postnorm", grid=(S // TR,),
        in_specs=[_row_spec(), _row_spec(), _vec_spec()], out_specs=_row_spec(),
        out_shape=jax.ShapeDtypeStruct((S, D), f32), compiler_params=_cp(("parallel",)),
    )(x, y, b_vec)


def post_bwd(dout, y, b_vec):
    S = dout.shape[0]

    def body(d_ref, y_ref, b_ref, dy_ref, db_ref):
        i = pl.program_id(0)
        yv, dv = y_ref[...], d_ref[...]
        rstd = lax.rsqrt(jnp.mean(yv * yv, axis=1, keepdims=True) + EPS)
        yh = yv * rstd
        dyh = dv * b_ref[...]
        dy_ref[...] = (rstd * (dyh - yh * jnp.mean(dyh * yh, axis=1, keepdims=True))).astype(bf16)
        part = jnp.sum(dv * yh, axis=0, keepdims=True)

        @pl.when(i == 0)
        def _():
            db_ref[...] = part

        @pl.when(i > 0)
        def _():
            db_ref[...] += part

    return pl.pallas_call(
        body, name="post_bwd", grid=(S // TR,),
        in_specs=[_row_spec(), _row_spec(), _vec_spec()], out_specs=[_row_spec(), _vec_spec()],
        out_shape=[jax.ShapeDtypeStruct((S, D), bf16), jax.ShapeDtypeStruct((1, D), f32)],
        compiler_params=_cp(("arbitrary",)),
    )(dout, y, b_vec)


def pre_bwd(dout, dh, x, a_vec):
    S = dout.shape[0]

    def body(d_ref, dh_ref, x_ref, a_ref, dx_ref, ds_ref, da_ref):
        i = pl.program_id(0)
        xv, dhv = x_ref[...], dh_ref[...]
        rstd = lax.rsqrt(jnp.mean(xv * xv, axis=1, keepdims=True) + EPS)
        n1 = xv * rstd
        dn = dhv * a_ref[...]
        dx_ref[...] = d_ref[...] + rstd * (dn - n1 * jnp.mean(dn * n1, axis=1, keepdims=True))
        p_s = jnp.sum(dhv, axis=0, keepdims=True)
        p_a = jnp.sum(dhv * n1, axis=0, keepdims=True)

        @pl.when(i == 0)
        def _():
            ds_ref[...] = p_s
            da_ref[...] = p_a

        @pl.when(i > 0)
        def _():
            ds_ref[...] += p_s
            da_ref[...] += p_a

    return pl.pallas_call(
        body, name="pre_bwd", grid=(S // TR,),
        in_specs=[_row_spec(), _row_spec(), _row_spec(), _vec_spec()],
        out_specs=[_row_spec(), _vec_spec(), _vec_spec()],
        out_shape=[jax.ShapeDtypeStruct((S, D), f32), jax.ShapeDtypeStruct((1, D), f32), jax.ShapeDtypeStruct((1, D), f32)],
        compiler_params=_cp(("arbitrary",)),
    )(dout, dh, x, a_vec)


def loss_grad(y, tgt):
    S = y.shape[0]

    def body(y_ref, t_ref, dy_ref, l_ref):
        i = pl.program_id(0)
        e = y_ref[...] - t_ref[...]
        dy_ref[...] = e * (1.0 / D)
        part = jnp.sum(jnp.sum(e * e, axis=1, keepdims=True), axis=0, keepdims=True) * (0.5 / D)
        part = jnp.broadcast_to(part, (8, LANE))

        @pl.when(i == 0)
        def _():
            l_ref[...] = part

        @pl.when(i > 0)
        def _():
            l_ref[...] += part

    return pl.pallas_call(
        body, name="loss_grad", grid=(S // TR,),
        in_specs=[_row_spec(), _row_spec()],
        out_specs=[_row_spec(), pl.BlockSpec((8, LANE), lambda i: (0, 0))],
        out_shape=[jax.ShapeDtypeStruct((S, D), f32), jax.ShapeDtypeStruct((8, LANE), f32)],
        compiler_params=_cp(("arbitrary",)),
    )(y, tgt)


G_NB = D // LANE
G_TR = 2048
G_OFF = O_GATE // LANE


def gate_merge(u, pa, pb, pc):
    S = u.shape[0]

    def body(g0, g1, g2, a, b, c, o_ref):
        o_ref[...] = (_sig(g0[...]) * a[...] + _sig(g1[...]) * b[...] + _sig(g2[...]) * c[...]).astype(bf16)

    gs = [pl.BlockSpec((G_TR, LANE), functools.partial(lambda i, j, k: (i, G_OFF + G_NB * k + j), k=k)) for k in range(3)]
    ps = pl.BlockSpec((G_TR, LANE), lambda i, j: (i, j))
    return pl.pallas_call(
        body, name="gate_merge", grid=(S // G_TR, G_NB),
        in_specs=gs + [ps, ps, ps], out_specs=ps,
        out_shape=jax.ShapeDtypeStruct((S, D), bf16), compiler_params=_cp(("parallel", "parallel")),
    )(u, u, u, pa, pb, pc)


def gate_bwd(dm, u, pa, pb, pc):
    S = u.shape[0]

    def body(dm_ref, g0, g1, g2, a, b, c, da, db, dc, dg0, dg1, dg2):
        d = dm_ref[...]
        for g, p, dp, dg in ((g0, a, da, dg0), (g1, b, db, dg1), (g2, c, dc, dg2)):
            s = _sig(g[...])
            dp[...] = (d * s).astype(bf16)
            dg[...] = (d * p[...] * s * (1.0 - s)).astype(bf16)

    gs = [pl.BlockSpec((G_TR, LANE), functools.partial(lambda i, j, k: (i, G_OFF + G_NB * k + j), k=k)) for k in range(3)]
    ps = pl.BlockSpec((G_TR, LANE), lambda i, j: (i, j))
    osd = jax.ShapeDtypeStruct((S, D), bf16)
    return pl.pallas_call(
        body, name="gate_bwd", grid=(S // G_TR, G_NB),
        in_specs=[ps] + gs + [ps, ps, ps], out_specs=[ps] * 6, out_shape=[osd] * 6,
        compiler_params=_cp(("parallel", "parallel")),
    )(dm, u, u, u, pa, pb, pc)


A_TB = 512
A_NCH = A_TB // A_CHUNK
A_NSUB = A_CHUNK // A_SUB
A_HP = 6
A_KW, A_VW = A_HP * A_K, A_HP * A_V


def _hgrn_gates(qr, fr, lbh):
    sq = _sig(qr)
    sig = _sig(fr)
    f = lbh + (1.0 - lbh) * sig
    logf = jnp.log(jnp.maximum(f, TINY))
    k = (1.0 - lbh) * (1.0 - sig)
    return qr * sq, sq, sig, f, logf, k


def _hgrn_intra(qf, k, b, causal):
    qts, kts, eqs, eks, blocks = [], [], [], [], []
    for sb in range(A_NSUB):
        rs = sb * A_SUB
        r = b[rs - 1:rs, :] if sb else jnp.zeros((1, A_K), f32)
        eq = jnp.exp(b[rs:rs + A_SUB, :] - r)
        ek = jnp.exp(jnp.minimum(r - b, A_CLAMP))
        qt = (qf[rs:rs + A_SUB, :] * eq).astype(bf16)
        kt = (k * ek).astype(bf16)
        blocks.append(_dot(qt, kt, NT))
        qts.append(qt), kts.append(kt), eqs.append(eq), eks.append(ek)
    a = jnp.where(causal, jnp.concatenate(blocks, axis=0), 0.0)
    return a, qts, kts, eqs, eks


def _tri():
    r = lax.broadcasted_iota(jnp.int32, (A_CHUNK, A_CHUNK), 0)
    c = lax.broadcasted_iota(jnp.int32, (A_CHUNK, A_CHUNK), 1)
    return r >= c


def _hgrn_in_specs(rev_nb=None):
    def im(col):
        if rev_nb is None:
            return lambda p, i: (i, col + p)
        return lambda p, i: (rev_nb - 1 - i, col + p)
    return [pl.BlockSpec((A_TB, A_KW), im(O_AQ // A_KW)), pl.BlockSpec((A_TB, A_KW), im(O_AF // A_KW)),
            pl.BlockSpec((A_TB, A_VW), im(O_AI // A_VW)), pl.BlockSpec((A_TB, A_VW), im(O_AG // A_VW)),
            pl.BlockSpec((1, A_KW), lambda p, i: (0, p)), pl.BlockSpec((1, A_V), lambda p, i: (0, 0))]


def hgrn_fwd(u, lb, ng, side=None):
    S = u.shape[0]
    nb = S // A_TB

    def body(q_ref, f_ref, i_ref, g_ref, lb_ref, ng_ref, o_ref, ya_ref, st_ref, state):
        @pl.when(pl.program_id(1) == 0)
        def _():
            state[...] = jnp.zeros_like(state)

        causal = _tri()
        tri = causal.astype(f32)

        def chunk(n, carry):
            rows = pl.ds(pl.multiple_of(n * A_CHUNK, A_CHUNK), A_CHUNK)
            o_parts, y_parts = [], []
            for hh in range(A_HP):
                ks = slice(hh * A_K, (hh + 1) * A_K)
                vs = slice(hh * A_V, (hh + 1) * A_V)
                qf, _, _, _, logf, k = _hgrn_gates(q_ref[rows, ks], f_ref[rows, ks], lb_ref[:, ks])
                vi = i_ref[rows, vs].astype(bf16)
                gg = g_ref[rows, vs]
                b = _dot(tri, logf, NN, precision=lax.Precision.HIGHEST)
                s0 = state[hh]
                st_ref[n, hh] = s0
                o = _dot((qf * jnp.exp(b)).astype(bf16), s0.astype(bf16), NT)
                a, _, _, _, _ = _hgrn_intra(qf, k, b, causal)
                o = o + _dot(a.astype(bf16), vi, NN)
                bend = b[A_CHUNK - 1:A_CHUNK, :]
                ke = (k * jnp.exp(bend - b)).astype(bf16)
                state[hh] = s0 * jnp.exp(bend) + _dot(vi, ke, TN)
                rstd = lax.rsqrt(jnp.mean(o * o, axis=1, keepdims=True) + EPS)
                o_parts.append(o)
                y_parts.append(o * rstd * ng_ref[...] * (gg * _sig(gg)))
            o_ref[rows, :] = jnp.concatenate(o_parts, axis=1)
            ya_ref[rows, :] = jnp.concatenate(y_parts, axis=1).astype(bf16)
            return carry

        lax.fori_loop(0, A_NCH, chunk, 0)

    return _call(
        body, name="hgrn_fwd", grid=(A_HEADS // A_HP, nb),
        in_specs=_hgrn_in_specs(),
        out_specs=[pl.BlockSpec((A_TB, A_VW), lambda p, i: (i, p)), pl.BlockSpec((A_TB, A_VW), lambda p, i: (i, p)),
                   pl.BlockSpec((A_NCH, A_HP, A_V, A_K), lambda p, i: (i, p, 0, 0))],
        out_shape=[jax.ShapeDtypeStruct((S, 384), f32), jax.ShapeDtypeStruct((S, 384), bf16),
                   jax.ShapeDtypeStruct((S // A_CHUNK, A_HEADS, A_V, A_K), f32)],
        scratch_shapes=[pltpu.VMEM((A_HP, A_V, A_K), f32)],
        sem=("parallel", "arbitrary"), args=(u, u, u, u, lb, ng), side=side)


def hgrn_bwd(u, lb, ng, o, st, dya, side=None):
    S = u.shape[0]
    nb = S // A_TB

    def body(q_ref, f_ref, i_ref, g_ref, lb_ref, ng_ref, o_ref, st_ref, dy_ref,
             dq_ref, df_ref, di_ref, dg_ref, dlb_ref, dng_ref, dstate):
        @pl.when(pl.program_id(1) == 0)
        def _():
            dstate[...] = jnp.zeros_like(dstate)
            dlb_ref[...] = jnp.zeros_like(dlb_ref)
            dng_ref[...] = jnp.zeros_like(dng_ref)

        causal = _tri()
        tri = causal.astype(f32)

        def chunk(it, carry):
            n = A_NCH - 1 - it
            rows = pl.ds(pl.multiple_of(n * A_CHUNK, A_CHUNK), A_CHUNK)
            dq_p, df_p, di_p, dg_p, dlb_p = [], [], [], [], []
            dng_acc = jnp.zeros((1, A_V), f32)
            for hh in range(A_HP):
                ks = slice(hh * A_K, (hh + 1) * A_K)
                vs = slice(hh * A_V, (hh + 1) * A_V)
                lbh = lb_ref[:, ks]
                qr = q_ref[rows, ks]
                qf, sq, sig, f, logf, k = _hgrn_gates(qr, f_ref[rows, ks], lbh)
                vi = i_ref[rows, vs].astype(bf16)
                gg = g_ref[rows, vs]
                b = _dot(tri, logf, NN, precision=lax.Precision.HIGHEST)
                eb = jnp.exp(b)
                bend = b[A_CHUNK - 1:A_CHUNK, :]
                eend = jnp.exp(bend)
                ekend = jnp.exp(bend - b)
                qe = (qf * eb).astype(bf16)
                ke = (k * ekend).astype(bf16)
                s0 = st_ref[n, hh]
                dsend = dstate[hh]
                ov = o_ref[rows, vs]
                dy = dy_ref[rows, vs]
                rstd = lax.rsqrt(jnp.mean(ov * ov, axis=1, keepdims=True) + EPS)
                oh = ov * rstd
                sg = _sig(gg)
                d_on = dy * (gg * sg)
                dg_p.append(dy * oh * ng_ref[...] * (sg * (1.0 + gg * (1.0 - sg))))
                dng_acc = dng_acc + jnp.sum(d_on * oh, axis=0, keepdims=True)
                doh = d_on * ng_ref[...]
                do = (rstd * (doh - oh * jnp.mean(doh * oh, axis=1, keepdims=True))).astype(bf16)
                a, qts, kts, eqs, eks = _hgrn_intra(qf, k, b, causal)
                da = jnp.where(causal, _dot(do, vi, NT), 0.0).astype(bf16)
                dsb = dsend.astype(bf16)
                dv = _dot(a.astype(bf16), do, TN) + _dot(ke, dsb, NT)
                dq = _dot(do, s0.astype(bf16), NN) * eb
                dk_state = _dot(vi, dsb, NN) * ekend
                dk = dk_state
                dq_i = []
                for sb in range(A_NSUB):
                    da_sb = da[sb * A_SUB:(sb + 1) * A_SUB, :]
                    dq_i.append(_dot(da_sb, kts[sb], NN) * eqs[sb])
                    dk = dk + _dot(da_sb, qts[sb], TN) * eks[sb]
                dq = dq + jnp.concatenate(dq_i, axis=0)
                db = qf * dq - k * dk
                extra = jnp.sum(k * dk_state, axis=0, keepdims=True) + eend * jnp.sum(s0 * dsend, axis=0, keepdims=True)
                dlogf = _dot(tri, db, TN, precision=lax.Precision.HIGHEST) + extra
                dstate[hh] = _dot(do, qe, TN) + eend * dsend
                d_pre = jnp.where(f > TINY, dlogf / f, 0.0) - dk
                dlb_p.append(jnp.sum((1.0 - sig) * d_pre, axis=0, keepdims=True))
                df_p.append((1.0 - lbh) * d_pre * sig * (1.0 - sig))
                dq_p.append(dq * (sq * (1.0 + qr * (1.0 - sq))))
                di_p.append(dv)
            dq_ref[rows, :] = jnp.concatenate(dq_p, axis=1).astype(bf16)
            df_ref[rows, :] = jnp.concatenate(df_p, axis=1).astype(bf16)
            di_ref[rows, :] = jnp.concatenate(di_p, axis=1).astype(bf16)
            dg_ref[rows, :] = jnp.concatenate(dg_p, axis=1).astype(bf16)
            dlb_ref[...] += jnp.concatenate(dlb_p, axis=1)
            dng_ref[0] += dng_acc
            return carry

        lax.fori_loop(0, A_NCH, chunk, 0)

    rev = lambda p, i: (nb - 1 - i, p)
    return _call(
        body, name="hgrn_bwd", grid=(A_HEADS // A_HP, nb),
        in_specs=_hgrn_in_specs(nb) + [pl.BlockSpec((A_TB, A_VW), rev),
                                       pl.BlockSpec((A_NCH, A_HP, A_V, A_K), lambda p, i: (nb - 1 - i, p, 0, 0)),
                                       pl.BlockSpec((A_TB, A_VW), rev)],
        out_specs=[pl.BlockSpec((A_TB, A_KW), rev), pl.BlockSpec((A_TB, A_KW), rev),
                   pl.BlockSpec((A_TB, A_VW), rev), pl.BlockSpec((A_TB, A_VW), rev),
                   pl.BlockSpec((1, A_KW), lambda p, i: (0, p)), pl.BlockSpec((1, 1, A_V), lambda p, i: (p, 0, 0))],
        out_shape=[jax.ShapeDtypeStruct((S, 768), bf16), jax.ShapeDtypeStruct((S, 768), bf16),
                   jax.ShapeDtypeStruct((S, 384), bf16), jax.ShapeDtypeStruct((S, 384), bf16),
                   jax.ShapeDtypeStruct((1, 768), f32), jax.ShapeDtypeStruct((A_HEADS // A_HP, 1, A_V), f32)],
        scratch_shapes=[pltpu.VMEM((A_HP, A_V, A_K), f32)],
        sem=("parallel", "arbitrary"), args=(u, u, u, u, lb, ng, o, st, dya), side=side)


B_TK = 128
SCALE = HD ** -0.5


def _split(x):
    hi = x.astype(bf16)
    return hi, (x - hi.astype(f32)).astype(bf16)


def _dot2(x, m, dn):
    hi, lo = _split(x)
    return _dot(hi, m, dn) + _dot(lo, m, dn)


def _sb_block(qs, kh, mask, m_gt, c):
    z = _dot(qs, kh, NT)
    sp = jnp.maximum(z, 0.0) + jnp.log(1.0 + jnp.exp(-jnp.abs(z)))
    lneg = jnp.where(mask, -sp, 0.0)
    lsz = z - sp
    suf = _dot2(lneg, m_gt, NN) + c
    a = jnp.where(mask, jnp.exp(lsz + suf), 0.0)
    return lneg, lsz, a


def _sb_masks(tq, i, jj):
    t_idx = i * tq + lax.broadcasted_iota(jnp.int32, (tq, B_TK), 0)
    s_idx = jj * B_TK + lax.broadcasted_iota(jnp.int32, (tq, B_TK), 1)
    return s_idx < t_idx


def _sb_tri(strict):
    r = lax.broadcasted_iota(jnp.int32, (B_TK, B_TK), 0)
    c = lax.broadcasted_iota(jnp.int32, (B_TK, B_TK), 1)
    return (r > c if strict else r >= c).astype(bf16)


B_DEAD = -88.0


def _sb_walk(nkb, step, init):
    def cond(state):
        it, alive, _ = state
        return jnp.logical_and(it < nkb, alive)

    def body(state):
        it, _, carry = state
        carry = step(it, carry)
        top = jnp.max(functools.reduce(jnp.maximum, [h[1] for h in carry]))
        return it + 1, top > B_DEAD, carry

    return lax.while_loop(cond, body, (jnp.int32(0), jnp.bool_(True), init))[2]


B_HP = 6
B_W = B_HP * HD


def sb_fwd(u, side=None):
    S = u.shape[0]
    tq = 128

    def body(q_ref, k_ref, v_ref, o_ref):
        i = pl.program_id(1)
        nkb = (i + 1) * (tq // B_TK)
        m_gt = _sb_tri(True)
        qs = [(q_ref[:, hh * HD:(hh + 1) * HD] * SCALE).astype(bf16) for hh in range(B_HP)]

        def step(it, carry):
            jj = nkb - 1 - it
            rows = pl.ds(pl.multiple_of(jj * B_TK, B_TK), B_TK)
            mask = _sb_masks(tq, i, jj)
            kb, vb = k_ref[rows, :], v_ref[rows, :]
            out = []
            for hh in range(B_HP):
                acc, c = carry[hh]
                kh = kb[:, hh * HD:(hh + 1) * HD].astype(bf16)
                vh = vb[:, hh * HD:(hh + 1) * HD].astype(bf16)
                lneg, _, a = _sb_block(qs[hh], kh, mask, m_gt, c)
                out.append((acc + _dot2(a, vh, NN), c + jnp.sum(lneg, axis=1, keepdims=True)))
            return tuple(out)

        z0 = (jnp.zeros((tq, HD), f32), jnp.zeros((tq, 1), f32))
        res = _sb_walk(nkb, step, (z0,) * B_HP)
        o_ref[...] = jnp.concatenate([r[0] for r in res], axis=1)

    outs, souts = _call(
        body, name="sb_fwd", grid=(B_HEADS // B_HP, S // tq),
        in_specs=[pl.BlockSpec((tq, B_W), lambda p, i: (i, O_BQ // B_W + p)),
                  pl.BlockSpec((S, B_W), lambda p, i: (0, O_BK // B_W + p)),
                  pl.BlockSpec((S, B_W), lambda p, i: (0, O_BV // B_W + p))],
        out_specs=[pl.BlockSpec((tq, B_W), lambda p, i: (i, p))],
        out_shape=[jax.ShapeDtypeStruct((S, 384), f32)],
        sem=("parallel", "arbitrary"), args=(u, u, u), side=side)
    return outs[0], souts


def sb_bwd(u, yb, dyb, side=None):
    S = u.shape[0]
    tq = 128
    nq = S // tq

    def body(q_ref, k_ref, v_ref, y_ref, dy_ref, dq_ref, dk_out, dv_out, dk_ref, dv_ref, out_sem):
        p, i = pl.program_id(0), pl.program_id(1)

        @pl.when(i == 0)
        def _():
            dk_ref[...] = jnp.zeros_like(dk_ref)
            dv_ref[...] = jnp.zeros_like(dv_ref)

        nkb = (i + 1) * (tq // B_TK)
        m_gt = _sb_tri(True)
        m_ge = _sb_tri(False)
        qs, dos, tot = [], [], []
        for hh in range(B_HP):
            hs = slice(hh * HD, (hh + 1) * HD)
            qs.append((q_ref[:, hs] * SCALE).astype(bf16))
            dob = dy_ref[:, hs].astype(bf16)
            dos.append(dob)
            tot.append(jnp.sum(dob.astype(f32) * y_ref[:, hs], axis=1, keepdims=True))

        def step(it, carry):
            jj = nkb - 1 - it
            rows = pl.ds(pl.multiple_of(jj * B_TK, B_TK), B_TK)
            mask = _sb_masks(tq, i, jj)
            kb, vb = k_ref[rows, :], v_ref[rows, :]
            out, dk_p, dv_p = [], [], []
            for hh in range(B_HP):
                dq, c, cg = carry[hh]
                kh = kb[:, hh * HD:(hh + 1) * HD].astype(bf16)
                vh = vb[:, hh * HD:(hh + 1) * HD].astype(bf16)
                lneg, lsz, a = _sb_block(qs[hh], kh, mask, m_gt, c)
                g = a * _dot(dos[hh], vh, NT)
                pre = tot[hh] - cg - _dot2(g, m_ge, NN)
                beta = jnp.exp(lsz)
                dz = jnp.where(mask, g * (1.0 - beta) - beta * pre, 0.0).astype(bf16)
                dk_p.append(_dot(dz, qs[hh], TN))
                dv_p.append(_dot(a.astype(bf16), dos[hh], TN))
                out.append((dq + _dot(dz, kh, NN), c + jnp.sum(lneg, axis=1, keepdims=True),
                            cg + jnp.sum(g, axis=1, keepdims=True)))
            dk_ref[rows, :] += jnp.concatenate(dk_p, axis=1)
            dv_ref[rows, :] += jnp.concatenate(dv_p, axis=1)
            return tuple(out)

        z0 = (jnp.zeros((tq, HD), f32), jnp.zeros((tq, 1), f32), jnp.zeros((tq, 1), f32))
        res = _sb_walk(nkb, step, (z0,) * B_HP)
        dq_ref[...] = jnp.concatenate([r[0] for r in res], axis=1) * SCALE

        @pl.when(i == nq - 1)
        def _():
            cols = pl.ds(pl.multiple_of(p * B_W, LANE), B_W)
            ck = pltpu.make_async_copy(dk_ref, dk_out.at[:, cols], out_sem.at[0])
            cv = pltpu.make_async_copy(dv_ref, dv_out.at[:, cols], out_sem.at[1])
            ck.start()
            cv.start()
            ck.wait()
            cv.wait()

    row = pl.BlockSpec((tq, B_W), lambda p, i: (i, p))
    hbm = pl.BlockSpec(memory_space=pl.ANY)
    osd = jax.ShapeDtypeStruct((S, 384), f32)
    return _call(
        body, name="sb_bwd", grid=(B_HEADS // B_HP, nq),
        in_specs=[pl.BlockSpec((tq, B_W), lambda p, i: (i, O_BQ // B_W + p)),
                  pl.BlockSpec((S, B_W), lambda p, i: (0, O_BK // B_W + p)),
                  pl.BlockSpec((S, B_W), lambda p, i: (0, O_BV // B_W + p)), row, row],
        out_specs=[row, hbm, hbm], out_shape=[osd, osd, osd],
        scratch_shapes=[pltpu.VMEM((S, B_W), f32), pltpu.VMEM((S, B_W), f32), pltpu.SemaphoreType.DMA((2,))],
        sem=("parallel", "arbitrary"), args=(u, u, u, yb, dyb), side=side)


def _dil_rows(i, rho, r):
    if r == 1:
        return pl.ds(pl.multiple_of(i * C_BLK, C_BLK), C_BLK)
    return pl.ds(i * (C_BLK * r) + rho, C_BLK, stride=r)


def _dil_scores(qs, kc, kp, i, slope_r):
    qi = lax.broadcasted_iota(jnp.int32, (C_BLK, C_BLK), 0)
    kj = lax.broadcasted_iota(jnp.int32, (C_BLK, C_BLK), 1)
    d_c = qi - kj
    d_p = d_c + C_BLK
    ok_c = d_c >= 0
    ok_p = jnp.logical_and(d_c <= 0, i > 0)
    s_c = jnp.where(ok_c, _dot(qs, kc, NT) - slope_r * d_c.astype(f32), NEG_BIG)
    s_p = jnp.where(ok_p, _dot(qs, kp, NT) - slope_r * d_p.astype(f32), NEG_BIG)
    return s_c, s_p, ok_c, ok_p


def _dil_slope(g, r, hh):
    pair = pl.program_id(0)
    return jnp.where(pair == 0, C_SLOPES[4 * g + hh] * r, C_SLOPES[4 * g + 2 + hh] * r).astype(f32)


def _dil_u_specs(g, S):
    def im(off):
        return lambda p, rho: (0, (off + g * 256) // LANE + p)
    return [pl.BlockSpec((S, LANE), im(O_CQ)), pl.BlockSpec((S, LANE), im(O_CK)), pl.BlockSpec((S, LANE), im(O_CV))]


def dil_fwd(u, g):
    S = u.shape[0]
    r = C_GROUPS[g][1]
    nbk = S // r // C_BLK

    def body(q_ref, k_ref, v_ref, o_ref, l_ref):
        rho = pl.program_id(1)

        def step(i, carry):
            rc = _dil_rows(i, rho, r)
            rp = _dil_rows(jnp.maximum(i - 1, 0), rho, r)
            q2, kc2, kp2, vc2, vp2 = q_ref[rc, :], k_ref[rc, :], k_ref[rp, :], v_ref[rc, :], v_ref[rp, :]
            o_p, l_p = [], []
            for hh in range(2):
                hs = slice(hh * HD, (hh + 1) * HD)
                qs = (q2[:, hs] * SCALE).astype(bf16)
                kc, kp = kc2[:, hs].astype(bf16), kp2[:, hs].astype(bf16)
                vc, vp = vc2[:, hs].astype(bf16), vp2[:, hs].astype(bf16)
                s_c, s_p, _, _ = _dil_scores(qs, kc, kp, i, _dil_slope(g, r, hh))
                m = jnp.maximum(jnp.max(s_c, axis=1, keepdims=True), jnp.max(s_p, axis=1, keepdims=True))
                p_c, p_p = jnp.exp(s_c - m), jnp.exp(s_p - m)
                den = jnp.sum(p_c, axis=1, keepdims=True) + jnp.sum(p_p, axis=1, keepdims=True)
                o_p.append((_dot(p_c.astype(bf16), vc, NN) + _dot(p_p.astype(bf16), vp, NN)) / den)
                l_p.append(jnp.broadcast_to(m + jnp.log(den), (C_BLK, HD)))
            o_ref[rc, :] = jnp.concatenate(o_p, axis=1)
            l_ref[rc, :] = jnp.concatenate(l_p, axis=1)
            return carry

        lax.fori_loop(0, nbk, step, 0)

    ospec = pl.BlockSpec((S, LANE), lambda p, rho: (0, p))
    osd = jax.ShapeDtypeStruct((S, 256), f32)
    return pl.pallas_call(
        body, name=f"dil_fwd{g}", grid=(2, r),
        in_specs=_dil_u_specs(g, S), out_specs=[ospec, ospec], out_shape=[osd, osd],
        compiler_params=_cp(("parallel", "arbitrary")),
    )(u, u, u)


def dil_merge(os_, ls_):
    S = os_[0].shape[0]

    def body(o0, o1, o2, l0, l1, l2, y_ref, lse_ref):
        a, b, c = l0[...], l1[...], l2[...]
        m = jnp.maximum(jnp.maximum(a, b), c)
        ea, eb, ec = jnp.exp(a - m), jnp.exp(b - m), jnp.exp(c - m)
        den = ea + eb + ec
        y_ref[...] = (ea * o0[...] + eb * o1[...] + ec * o2[...]) / den
        lse_ref[...] = m + jnp.log(den)

    spec = pl.BlockSpec((512, 256), lambda i: (i, 0))
    osd = jax.ShapeDtypeStruct((S, 256), f32)
    return pl.pallas_call(
        body, name="dil_merge", grid=(S // 512,), in_specs=[spec] * 6, out_specs=[spec, spec],
        out_shape=[osd, osd], compiler_params=_cp(("parallel",)),
    )(*os_, *ls_)


def dil_bwd(u, g, dyc, yc, lse):
    S = u.shape[0]
    r = C_GROUPS[g][1]
    nbk = S // r // C_BLK

    def body(q_ref, k_ref, v_ref, dy_ref, y_ref, l_ref, dq_ref, dk_ref, dv_ref):
        rho = pl.program_id(1)

        @pl.when(rho == 0)
        def _():
            dk_ref[...] = jnp.zeros_like(dk_ref)
            dv_ref[...] = jnp.zeros_like(dv_ref)

        def step(i, carry):
            rc = _dil_rows(i, rho, r)
            rp = _dil_rows(jnp.maximum(i - 1, 0), rho, r)
            q2, kc2, kp2, vc2, vp2 = q_ref[rc, :], k_ref[rc, :], k_ref[rp, :], v_ref[rc, :], v_ref[rp, :]
            dy2, y2, l2 = dy_ref[rc, :], y_ref[rc, :], l_ref[rc, :]
            dq_p, dkc_p, dkp_p, dvc_p, dvp_p = [], [], [], [], []
            for hh in range(2):
                hs = slice(hh * HD, (hh + 1) * HD)
                qs = (q2[:, hs] * SCALE).astype(bf16)
                kc, kp = kc2[:, hs].astype(bf16), kp2[:, hs].astype(bf16)
                vc, vp = vc2[:, hs].astype(bf16), vp2[:, hs].astype(bf16)
                dy = dy2[:, hs]
                dyb = dy.astype(bf16)
                s_c, s_p, ok_c, ok_p = _dil_scores(qs, kc, kp, i, _dil_slope(g, r, hh))
                lrow = l2[:, hh * HD:hh * HD + 1]
                delta = jnp.sum(dy * y2[:, hs], axis=1, keepdims=True)
                pi_c = jnp.where(ok_c, jnp.exp(s_c - lrow), 0.0)
                pi_p = jnp.where(ok_p, jnp.exp(s_p - lrow), 0.0)
                ds_c = (pi_c * (_dot(dyb, vc, NT) - delta)).astype(bf16)
                ds_p = (pi_p * (_dot(dyb, vp, NT) - delta)).astype(bf16)
                dq_p.append((_dot(ds_c, kc, NN) + _dot(ds_p, kp, NN)) * SCALE)
                dkc_p.append(_dot(ds_c, qs, TN))
                dkp_p.append(_dot(ds_p, qs, TN))
                dvc_p.append(_dot(pi_c.astype(bf16), dyb, TN))
                dvp_p.append(_dot(pi_p.astype(bf16), dyb, TN))
            dq_ref[rc, :] = jnp.concatenate(dq_p, axis=1)
            dk_ref[rc, :] += jnp.concatenate(dkc_p, axis=1)
            dv_ref[rc, :] += jnp.concatenate(dvc_p, axis=1)
            dk_ref[rp, :] += jnp.concatenate(dkp_p, axis=1)
            dv_ref[rp, :] += jnp.concatenate(dvp_p, axis=1)
            return carry

        lax.fori_loop(0, nbk, step, 0)

    ospec = pl.BlockSpec((S, LANE), lambda p, rho: (0, p))
    osd = jax.ShapeDtypeStruct((S, 256), f32)
    return pl.pallas_call(
        body, name=f"dil_bwd{g}", grid=(2, r),
        in_specs=_dil_u_specs(g, S) + [ospec, ospec, ospec], out_specs=[ospec] * 3, out_shape=[osd] * 3,
        compiler_params=_cp(("parallel", "arbitrary")),
    )(u, u, u, dyc, yc, lse)


def _rows_tile(rows):
    return _tile(rows, (256, 176, 128, 64, 32, 16, 8))


def cast_bf16(w):
    shape = w.shape
    w2 = w.reshape(-1, shape[-1])
    rows, cols = w2.shape
    tr = _rows_tile(rows)

    def body(x_ref, o_ref):
        o_ref[...] = x_ref[...].astype(bf16)

    spec = pl.BlockSpec((tr, cols), lambda i: (i, 0))
    out = pl.pallas_call(
        body, name="cast_bf16", grid=(rows // tr,), in_specs=[spec], out_specs=spec,
        out_shape=jax.ShapeDtypeStruct((rows, cols), bf16), compiler_params=_cp(("parallel",)),
    )(w2)
    return out.reshape(shape)


BC1 = 1.0 - ADAM_B1 ** ADAM_STEP
BC2 = 1.0 - ADAM_B2 ** ADAM_STEP


def _adam_math(w, g, m, v):
    m2 = ADAM_B1 * m + (1.0 - ADAM_B1) * g
    v2 = ADAM_B2 * v + (1.0 - ADAM_B2) * (g * g)
    delta = -ADAM_LR * ((m2 / BC1) / (jnp.sqrt(v2 / BC2) + ADAM_EPS) + ADAM_WD * w)
    return delta, m2, v2


def adam(w, g, m, v):
    shape = w.shape
    r2 = lambda t: t.reshape(-1, shape[-1])
    rows, cols = r2(w).shape
    tr = _rows_tile(rows)

    def body(w_ref, g_ref, m_ref, v_ref, d_ref, m2_ref, v2_ref):
        d_ref[...], m2_ref[...], v2_ref[...] = _adam_math(w_ref[...], g_ref[...], m_ref[...], v_ref[...])

    spec = pl.BlockSpec((tr, cols), lambda i: (i, 0))
    osd = jax.ShapeDtypeStruct((rows, cols), f32)
    outs = pl.pallas_call(
        body, name="adam", grid=(rows // tr,), in_specs=[spec] * 4, out_specs=[spec] * 3, out_shape=[osd] * 3,
        compiler_params=_cp(("parallel",)),
    )(r2(w), r2(g), r2(m), r2(v))
    return [o.reshape(shape) for o in outs]


ADA_N = 9 * D // 4
ADA_TN = 384


def ada_fwd(c_all, w_ada):
    def body(c_ref, w_ref, o_ref):
        cv = c_ref[...]
        o_ref[0] = _dot((cv * _sig(cv)).astype(bf16), w_ref[0].astype(bf16), NN)

    return pl.pallas_call(
        body, name="ada_fwd", grid=(DEPTH, ADA_N // ADA_TN),
        in_specs=[pl.BlockSpec((8, D), lambda l, j: (0, 0)), pl.BlockSpec((1, D, ADA_TN), lambda l, j: (l, 0, j))],
        out_specs=pl.BlockSpec((1, 8, ADA_TN), lambda l, j: (l, 0, j)),
        out_shape=jax.ShapeDtypeStruct((DEPTH, 8, ADA_N), f32), compiler_params=_cp(("parallel", "parallel")),
    )(c_all, w_ada)


def ada_bwd_adam(c_all, dm, w, m, v):
    tr = 128

    def body(c_ref, dm_ref, w_ref, m_ref, v_ref, g_ref, d_ref, m2_ref, v2_ref):
        cv = c_ref[...]
        g = _dot((cv * _sig(cv)).astype(bf16), dm_ref[0].astype(bf16), TN)
        g_ref[0] = g
        d_ref[0], m2_ref[0], v2_ref[0] = _adam_math(w_ref[0], g, m_ref[0], v_ref[0])

    wspec = pl.BlockSpec((1, tr, ADA_N), lambda l, i: (l, i, 0))
    osd = jax.ShapeDtypeStruct((DEPTH, D, ADA_N), f32)
    return pl.pallas_call(
        body, name="ada_bwd_adam", grid=(DEPTH, D // tr),
        in_specs=[pl.BlockSpec((8, tr), lambda l, i: (0, i)), pl.BlockSpec((1, 8, ADA_N), lambda l, i: (l, 0, 0)),
                  wspec, wspec, wspec],
        out_specs=[wspec] * 4, out_shape=[osd] * 4, compiler_params=_cp(("parallel", "parallel")),
    )(c_all, dm, w, m, v)


def _lb_probs(x):
    mx = jnp.max(x, axis=0, keepdims=True)
    e = jnp.exp(x - mx)
    return e / jnp.sum(e, axis=0, keepdims=True)


def lb_fwd(logits):
    def body(x_ref, o_ref):
        p = _lb_probs(x_ref[...])
        rows = [jnp.zeros((1, 768), f32)]
        for l in range(1, DEPTH):
            rows.append(rows[-1] + p[l:l + 1, :])
        o_ref[...] = jnp.concatenate(rows, axis=0)

    return pl.pallas_call(body, name="lb_fwd", out_shape=jax.ShapeDtypeStruct((DEPTH, 768), f32))(logits)


def lb_bwd(logits, dlb):
    def body(x_ref, d_ref, o_ref):
        p = _lb_probs(x_ref[...])
        d = d_ref[...]
        rows = [jnp.zeros((1, 768), f32)] * DEPTH
        acc = jnp.zeros((1, 768), f32)
        for l in range(DEPTH - 1, 0, -1):
            acc = acc + d[l:l + 1, :]
            rows[l] = acc
        dp = jnp.concatenate(rows, axis=0)
        o_ref[...] = p * (dp - jnp.sum(p * dp, axis=0, keepdims=True))

    return pl.pallas_call(body, name="lb_bwd", out_shape=jax.ShapeDtypeStruct((DEPTH, 768), f32))(logits, dlb)


def sum_slots(x):
    n, rows, cols = x.shape
    tr = _rows_tile(rows)

    def body(x_ref, o_ref):
        acc = x_ref[0]
        for j in range(1, n):
            acc = acc + x_ref[j]
        o_ref[...] = acc

    return pl.pallas_call(
        body, name="sum_slots", grid=(rows // tr,),
        in_specs=[pl.BlockSpec((n, tr, cols), lambda i: (0, i, 0))], out_specs=pl.BlockSpec((tr, cols), lambda i: (i, 0)),
        out_shape=jax.ShapeDtypeStruct((rows, cols), f32), compiler_params=_cp(("parallel",)),
    )(x)


ANY = pl.BlockSpec(memory_space=pl.ANY)
CHIP_FLIPS = ((1, 0), (0, 1), (1, 1))
DEV_FLIPS = tuple((a, b, d) for a in (0, 1) for b in (0, 1) for d in (0, 1))[1:]


def _me():
    return lax.axis_index("x"), lax.axis_index("y"), lax.axis_index("c")


def _flip(v, f):
    return 1 - v if f else v


def _comm_call(body, name, ins, out_shapes, n_remote, n_local):
    return pl.pallas_call(
        body, name=name, in_specs=[ANY] * len(ins), out_specs=[ANY] * len(out_shapes), out_shape=out_shapes,
        scratch_shapes=[pltpu.SemaphoreType.DMA((n_remote,)), pltpu.SemaphoreType.DMA((n_remote,)),
                        pltpu.SemaphoreType.DMA((max(n_local, 1),))],
    )(*ins)


def run_plan(plan, name):
    ni, no = len(plan.ins), len(plan.outs)

    def body(*refs):
        ins, outs, sems = refs[:ni], refs[ni:ni + no], refs[ni + no:]
        plan.start(ins, outs, *sems)
        plan.wait(ins, outs, *sems)

    return pl.pallas_call(body, name=name, in_specs=[ANY] * ni, out_specs=[ANY] * no, out_shape=list(plan.outs),
                          scratch_shapes=plan.sems())(*plan.ins)


def gather_chips_plan(arrs, layer=None):
    n = len(arrs)
    shapes = [a.shape if layer is None else a.shape[1:] for a in arrs]

    def copies(ins, outs, send, recv, loc):
        x, y, c = _me()
        mine = 2 * x + y
        srcs = [r if layer is None else r.at[layer] for r in ins]
        locs = [pltpu.make_async_copy(srcs[a], outs[a].at[mine], loc.at[a]) for a in range(n)]

        def remote(a, k, slot):
            fx, fy = CHIP_FLIPS[k]
            return pltpu.make_async_remote_copy(srcs[a], outs[a].at[slot], send.at[3 * a + k], recv.at[3 * a + k],
                                                device_id=(_flip(x, fx), _flip(y, fy), c), device_id_type=MESH)

        peers = [2 * _flip(x, fx) + _flip(y, fy) for fx, fy in CHIP_FLIPS]
        return locs, remote, mine, peers

    def start(ins, outs, send, recv, loc):
        locs, remote, mine, _ = copies(ins, outs, send, recv, loc)
        for cp in locs:
            cp.start()
        for a in range(n):
            for k in range(3):
                remote(a, k, mine).start()

    def wait(ins, outs, send, recv, loc):
        locs, remote, _, peers = copies(ins, outs, send, recv, loc)
        for a in range(n):
            for k in range(3):
                cp = remote(a, k, peers[k])
                cp.wait_recv()
                cp.wait_send()
        for cp in locs:
            cp.wait()

    outs = [jax.ShapeDtypeStruct((4,) + tuple(s), a.dtype) for s, a in zip(shapes, arrs)]
    return Plan(list(arrs), outs, 3 * n, n, start, wait)


def all_gather_chips(arrs, layer=None, name="ag4"):
    return run_plan(gather_chips_plan(arrs, layer), name)


def all_gather_devs(arr, name="ag8"):
    def body(in_ref, out_ref, send, recv, loc):
        x, y, c = _me()
        mine = 4 * x + 2 * y + c
        lc = pltpu.make_async_copy(in_ref, out_ref.at[mine], loc.at[0])
        lc.start()

        def remote(k, slot):
            fx, fy, fc = DEV_FLIPS[k]
            return pltpu.make_async_remote_copy(in_ref, out_ref.at[slot], send.at[k], recv.at[k],
                                                device_id=(_flip(x, fx), _flip(y, fy), _flip(c, fc)), device_id_type=MESH)

        for k in range(7):
            remote(k, mine).start()
        for k, (fx, fy, fc) in enumerate(DEV_FLIPS):
            cp = remote(k, 4 * _flip(x, fx) + 2 * _flip(y, fy) + _flip(c, fc))
            cp.wait_recv()
            cp.wait_send()
        lc.wait()

    return _comm_call(body, name, [arr], [jax.ShapeDtypeStruct((8,) + arr.shape, arr.dtype)], 7, 1)[0]


def _rows_of(which, rows):
    return pl.ds(pl.multiple_of(which * rows, 16), rows)


def dev_exchange_plan(parts):
    n = len(parts)

    def copies(ins, outs, send, recv, loc):
        x, y, c = _me()
        mine = 4 * x + 2 * y + c

        def piece(a, px, py, pc):
            rows = ins[a].shape[1] // 2
            return ins[a].at[2 * px + py, _rows_of(pc, rows), :]

        locs = [pltpu.make_async_copy(piece(a, x, y, c), outs[a].at[mine], loc.at[a]) for a in range(n)]

        def remote(a, k, slot):
            fx, fy, fc = DEV_FLIPS[k]
            px, py, pc = _flip(x, fx), _flip(y, fy), _flip(c, fc)
            return pltpu.make_async_remote_copy(piece(a, px, py, pc), outs[a].at[slot], send.at[7 * a + k], recv.at[7 * a + k],
                                                device_id=(px, py, pc), device_id_type=MESH)

        peers = [4 * _flip(x, fx) + 2 * _flip(y, fy) + _flip(c, fc) for fx, fy, fc in DEV_FLIPS]
        return locs, remote, mine, peers

    def start(ins, outs, send, recv, loc):
        locs, remote, mine, _ = copies(ins, outs, send, recv, loc)
        for cp in locs:
            cp.start()
        for a in range(n):
            for k in range(7):
                remote(a, k, mine).start()

    def wait(ins, outs, send, recv, loc):
        locs, remote, _, peers = copies(ins, outs, send, recv, loc)
        for a in range(n):
            for k in range(7):
                cp = remote(a, k, peers[k])
                cp.wait_recv()
                cp.wait_send()
        for cp in locs:
            cp.wait()

    outs = [jax.ShapeDtypeStruct((8, p.shape[1] // 2, p.shape[2]), p.dtype) for p in parts]
    return Plan(list(parts), outs, 7 * n, n, start, wait)


def sum_share(slots, name="rs_sum"):
    n, r, cols = slots.shape
    tr = _tile(r, (128, 176, 64))
    steps = r // tr

    def body(s_ref, g_ref, buf, send, loc, recv):
        i = pl.program_id(0)
        x, y, c = _me()
        slot = i % 2

        def copies(step, sl):
            rows = pl.ds(pl.multiple_of(c * r + step * tr, 8), tr)
            rem = pltpu.make_async_remote_copy(buf.at[sl], g_ref.at[rows, :], send.at[sl], recv.at[0],
                                               device_id=(x, y, 1 - c), device_id_type=MESH)
            return rem, pltpu.make_async_copy(buf.at[sl], g_ref.at[rows, :], loc.at[sl])

        @pl.when(i >= 2)
        def _():
            rem, lc = copies(i - 2, slot)
            rem.wait_send()
            lc.wait()

        acc = s_ref[0].astype(f32)
        for j in range(1, n):
            acc = acc + s_ref[j].astype(f32)
        buf[slot] = acc
        rem, lc = copies(i, slot)
        rem.start()
        lc.start()

        @pl.when(i == steps - 1)
        def _():
            for back in range(min(2, steps)):
                rem, lc = copies(i - back, (i - back) % 2)
                rem.wait_send()
                lc.wait()
            other = g_ref.at[pl.ds(pl.multiple_of((1 - c) * r, 8), r), :]
            pltpu.make_async_remote_copy(other, other, send.at[0], recv.at[0],
                                         device_id=(x, y, 1 - c), device_id_type=MESH).wait_recv()

    return pl.pallas_call(
        body, name=name, grid=(steps,),
        in_specs=[pl.BlockSpec((n, tr, cols), lambda i: (0, i, 0))], out_specs=ANY,
        out_shape=jax.ShapeDtypeStruct((2 * r, cols), f32),
        scratch_shapes=[pltpu.VMEM((2, tr, cols), f32), pltpu.SemaphoreType.DMA((2,)), pltpu.SemaphoreType.DMA((2,)),
                        pltpu.SemaphoreType.DMA((1,))],
        compiler_params=_cp(("arbitrary",)),
    )(slots)


BIG = ("ffn1_w_in", "ffn1_w_out", "w_in", "w_branch_a", "w_branch_b", "w_branch_c", "w_out", "ffn2_w_in", "ffn2_w_out")
ROW_SHARDED = ("ffn1_w_out", "w_out", "ffn2_w_out")
RES_W = (0.5, 1.0, 0.5)


def _full_weight(name, g):
    if name in ROW_SHARDED:
        return g.reshape(4 * g.shape[1], g.shape[2])
    return jnp.concatenate([g[0], g[1], g[2], g[3]], axis=1)


def _by_shard(name, dw):
    if name in ROW_SHARDED:
        return dw.reshape(4, dw.shape[0] // 4, dw.shape[1])
    return dw.reshape(dw.shape[0], 4, dw.shape[1] // 4).transpose(1, 0, 2)


def _full_weight_t(name, g):
    if name in ROW_SHARDED:
        return g.reshape(4 * g.shape[1], g.shape[2]).T
    return g.transpose(0, 2, 1).reshape(4 * g.shape[2], g.shape[1])


def _ffn_fwd(x, w_in, w_out, a_vec, sh_vec, b_vec):
    h = prenorm(x, a_vec, sh_vec)
    ua, ub, s = ffn_in_swiglu(h, w_in)
    y = mm(s, w_out, name="ffn_out")
    return postnorm(x, y, b_vec), (x, h, ua, ub, s, y)


def _ffn_bwd(dout, saved, w_in_t, w_out_t, a_vec, b_vec):
    x, h, ua, ub, s, y = saved
    dy, db = post_bwd(dout, y, b_vec)
    dw_out = mm(s.T, dy, out_dtype=bf16, name="ffn_dwo")
    du = ffn_du(dy, w_out_t, ua, ub)
    dh = mm(du, w_in_t, name="ffn_dh")
    dw_in = mm(h.T, du, out_dtype=bf16, name="ffn_dwi")
    dx, dsh, da = pre_bwd(dout, dh, x, a_vec)
    return dx, dw_in, dw_out, dsh, da, db


def _mix_fwd(x, w, lb, ng, a_vec, sh_vec, b_vec, plans=(None, None, None)):
    h = prenorm(x, a_vec, sh_vec)
    u = mm(h, w["w_in"], name="mix_in", side=plans[0])
    side0 = None
    if plans[0] is not None:
        u, side0 = u
    (o, ya, st), side1 = hgrn_fwd(u, lb, ng, side=plans[1])
    yb, side2 = sb_fwd(u, side=plans[2])
    groups = [dil_fwd(u, g) for g in range(3)]
    yc, lse = dil_merge([o_ for o_, _ in groups], [l_ for _, l_ in groups])
    pa = mm(ya, w["w_branch_a"], name="mix_pa")
    pb = mm(yb, w["w_branch_b"], name="mix_pb")
    pc = mm(yc, w["w_branch_c"], name="mix_pc")
    merged = gate_merge(u, pa, pb, pc)
    z = mm(merged, w["w_out"], name="mix_out")
    return postnorm(x, z, b_vec), (x, h, u, o, ya, st, yb, yc, lse, pa, pb, pc, merged, z), (side0, side1, side2)


def _mix_bwd(dout, saved, wt, lb, ng, a_vec, b_vec, plans=(None, None)):
    x, h, u, o, ya, st, yb, yc, lse, pa, pb, pc, merged, z = saved
    dz, db = post_bwd(dout, z, b_vec)
    dmerged = mm(dz, wt["w_out"], name="mix_dm")
    dw_out = mm(merged.T, dz, out_dtype=bf16, name="mix_dwo")
    dpa, dpb, dpc, dg0, dg1, dg2 = gate_bwd(dmerged, u, pa, pb, pc)
    dya = mm(dpa, wt["w_branch_a"], name="mix_dya")
    dyb = mm(dpb, wt["w_branch_b"], name="mix_dyb")
    dyc = mm(dpc, wt["w_branch_c"], name="mix_dyc")
    dw_a = mm(ya.T, dpa, out_dtype=bf16, name="mix_dwa")
    dw_b = mm(yb.astype(bf16).T, dpb, out_dtype=bf16, name="mix_dwb")
    dw_c = mm(yc.astype(bf16).T, dpc, out_dtype=bf16, name="mix_dwc")
    (daq, daf, dai, dag, dlb, dng), side0 = hgrn_bwd(u, lb, ng, o, st, dya, side=plans[0])
    (dbq, dbk, dbv), side1 = sb_bwd(u, yb, dyb, side=plans[1])
    dc = [dil_bwd(u, g, dyc, yc, lse) for g in range(3)]
    du = jnp.concatenate(
        [daq, daf, dai, dag] + [t.astype(bf16) for t in (dbq, dbk, dbv)]
        + [dc[g][j].astype(bf16) for j in range(3) for g in range(3)] + [dg0, dg1, dg2], axis=1)
    dh = mm(du, wt["w_in"], name="mix_dh")
    dw_in = mm(h.T, du, out_dtype=bf16, name="mix_dwi")
    dx, dsh, da = pre_bwd(dout, dh, x, a_vec)
    grads = {"w_in": dw_in, "w_out": dw_out, "w_branch_a": dw_a, "w_branch_b": dw_b, "w_branch_c": dw_c}
    return dx, grads, dlb, jnp.sum(dng, axis=0), dsh, da, db, (side0, side1)


FWD_RIDERS = (("ffn1_w_in", "w_out"), ("w_in", "ffn1_w_out"),
              ("ffn2_w_in", "ffn2_w_out", "w_branch_a", "w_branch_b", "w_branch_c"))
BWD_RIDERS = (("ffn2_w_in", "ffn1_w_out", "ffn2_w_out"), ("w_in", "ffn1_w_in", "w_out", "w_branch_a", "w_branch_b", "w_branch_c"))


def _reduce_to_shards(names, grads):
    slots = run_plan(dev_exchange_plan([_by_shard(n, grads[n]) for n in names]), "rs_x8")
    return {n: sum_share(s) for n, s in zip(names, slots)}


def kernel(x, c, w_ada, b_ada, norm_g, ffn1_w_in, ffn1_w_out, w_in, hgrn_lb_logits, hgrn_norm_g, w_branch_a, w_branch_b, w_branch_c, w_out, ffn2_w_in, ffn2_w_out, loss_target, m_w_ada, m_b_ada, m_norm_g, m_ffn1_w_in, m_ffn1_w_out, m_w_in, m_hgrn_lb_logits, m_hgrn_norm_g, m_w_branch_a, m_w_branch_b, m_w_branch_c, m_w_out, m_ffn2_w_in, m_ffn2_w_out, v_w_ada, v_b_ada, v_norm_g, v_ffn1_w_in, v_ffn1_w_out, v_w_in, v_hgrn_lb_logits, v_hgrn_norm_g, v_w_branch_a, v_w_branch_b, v_w_branch_c, v_w_out, v_ffn2_w_in, v_ffn2_w_out):
    weights = dict(w_ada=w_ada, b_ada=b_ada, norm_g=norm_g, ffn1_w_in=ffn1_w_in, ffn1_w_out=ffn1_w_out, w_in=w_in,
                   hgrn_lb_logits=hgrn_lb_logits, hgrn_norm_g=hgrn_norm_g, w_branch_a=w_branch_a, w_branch_b=w_branch_b,
                   w_branch_c=w_branch_c, w_out=w_out, ffn2_w_in=ffn2_w_in, ffn2_w_out=ffn2_w_out)
    mom = dict(w_ada=m_w_ada, b_ada=m_b_ada, norm_g=m_norm_g, ffn1_w_in=m_ffn1_w_in, ffn1_w_out=m_ffn1_w_out, w_in=m_w_in,
               hgrn_lb_logits=m_hgrn_lb_logits, hgrn_norm_g=m_hgrn_norm_g, w_branch_a=m_w_branch_a, w_branch_b=m_w_branch_b,
               w_branch_c=m_w_branch_c, w_out=m_w_out, ffn2_w_in=m_ffn2_w_in, ffn2_w_out=m_ffn2_w_out)
    var = dict(w_ada=v_w_ada, b_ada=v_b_ada, norm_g=v_norm_g, ffn1_w_in=v_ffn1_w_in, ffn1_w_out=v_ffn1_w_out, w_in=v_w_in,
               hgrn_lb_logits=v_hgrn_lb_logits, hgrn_norm_g=v_hgrn_norm_g, w_branch_a=v_w_branch_a, w_branch_b=v_w_branch_b,
               w_branch_c=v_w_branch_c, w_out=v_w_out, ffn2_w_in=v_ffn2_w_in, ffn2_w_out=v_ffn2_w_out)
    order = list(weights)
    xi, yi, ci = _me()
    chip = 2 * xi + yi
    dev = 4 * xi + 2 * yi + ci
    xs = x[0]

    c_all = all_gather_devs(c, name="ag8_c").reshape(8, D)
    mod_sh = all_gather_chips([ada_fwd(c_all, w_ada)], name="ag4_mod")[0]
    mod_all = mod_sh.transpose(1, 2, 0, 3).reshape(DEPTH, 8, 9 * D)
    mod = lax.dynamic_index_in_dim(mod_all, dev, axis=1, keepdims=False) + b_ada
    mod = mod.reshape(DEPTH, 3, 3, D)
    ng_all = all_gather_chips([norm_g.reshape(DEPTH * 6, D // 4)], name="ag4_norm")[0]
    ng_all = ng_all.reshape(4, DEPTH, 6, D // 4).transpose(1, 2, 0, 3).reshape(DEPTH, 6, D)
    lb_all = lb_fwd(hgrn_lb_logits)
    w16 = {n: cast_bf16(weights[n]) for n in BIG}

    def vecs(l, i):
        shift, scale, gate = mod[l, i, 0][None], mod[l, i, 1][None], mod[l, i, 2][None]
        g_pre, g_post = ng_all[l, 2 * i][None], ng_all[l, 2 * i + 1][None]
        return g_pre * (1.0 + scale), shift, RES_W[i] * gate * g_post

    saved, full = [], []
    gathered = dict(zip(BIG, all_gather_chips([w16[n] for n in BIG], layer=0, name="ag4_w0")))
    for l in range(DEPTH):
        w = {n: _full_weight(n, gathered[n]) for n in BIG}
        full.append({n: _full_weight_t(n, gathered[n]) for n in BIG})
        lb, ng = lb_all[l][None], hgrn_norm_g[l][None]
        plans = (None, None, None)
        if l + 1 < DEPTH:
            plans = tuple(gather_chips_plan([w16[n] for n in names], layer=l + 1) for names in FWD_RIDERS)
        xs, s1 = _ffn_fwd(xs, w["ffn1_w_in"], w["ffn1_w_out"], *vecs(l, 0))
        xs, s2, sides = _mix_fwd(xs, w, lb, ng, *vecs(l, 1), plans=plans)
        xs, s3 = _ffn_fwd(xs, w["ffn2_w_in"], w["ffn2_w_out"], *vecs(l, 2))
        saved.append((s1, s2, s3))
        if l + 1 < DEPTH:
            gathered = {n: g for names, outs in zip(FWD_RIDERS, sides) for n, g in zip(names, outs)}

    dx, loss_part = loss_grad(xs, loss_target[0])
    loss = lax.psum(loss_part[0, 0], ("x", "y", "c"))

    big_grads = {n: [None] * DEPTH for n in BIG}
    d_mod, d_ng, d_lb, d_hng = [None] * DEPTH, [None] * DEPTH, [None] * DEPTH, [None] * DEPTH
    pending = None
    for l in reversed(range(DEPTH)):
        wt = full[l]
        s1, s2, s3 = saved[l]
        lb, ng = lb_all[l][None], hgrn_norm_g[l][None]
        rows_mod, rows_ng = [None] * 9, [None] * 6

        def vec_grads(i, dsh, da, db):
            scale, gate = mod[l, i, 1][None], mod[l, i, 2][None]
            g_pre, g_post = ng_all[l, 2 * i][None], ng_all[l, 2 * i + 1][None]
            rows_mod[3 * i], rows_mod[3 * i + 1], rows_mod[3 * i + 2] = dsh, g_pre * da, RES_W[i] * g_post * db
            rows_ng[2 * i], rows_ng[2 * i + 1] = (1.0 + scale) * da, RES_W[i] * gate * db

        a3, _, b3 = vecs(l, 2)
        dx, dwi, dwo, dsh, da, db = _ffn_bwd(dx, s3, wt["ffn2_w_in"], wt["ffn2_w_out"], a3, b3)
        vec_grads(2, dsh, da, db)
        grads = {"ffn2_w_in": dwi, "ffn2_w_out": dwo}
        a2, _, b2 = vecs(l, 1)
        plans = (None, None)
        if pending is not None:
            plans = tuple(dev_exchange_plan([pending[n] for n in names]) for names in BWD_RIDERS)
        dx, gmix, dlb, dhng, dsh, da, db, sides = _mix_bwd(dx, s2, wt, lb, ng, a2, b2, plans=plans)
        if pending is not None:
            for names, outs in zip(BWD_RIDERS, sides):
                for n, slots in zip(names, outs):
                    big_grads[n][l + 1] = sum_share(slots)
        vec_grads(1, dsh, da, db)
        grads.update(gmix)
        a1, _, b1 = vecs(l, 0)
        dx, dwi, dwo, dsh, da, db = _ffn_bwd(dx, s1, wt["ffn1_w_in"], wt["ffn1_w_out"], a1, b1)
        vec_grads(0, dsh, da, db)
        grads.update({"ffn1_w_in": dwi, "ffn1_w_out": dwo})
        pending = {n: _by_shard(n, grads[n]) for n in BIG}
        d_mod[l] = jnp.concatenate(rows_mod, axis=1)
        d_ng[l] = jnp.concatenate(rows_ng, axis=0)
        d_lb[l], d_hng[l] = dlb, dhng

    slots = run_plan(dev_exchange_plan([pending[n] for n in BIG]), "rs_x8")
    for n, s in zip(BIG, slots):
        big_grads[n][0] = sum_share(s)

    n_small = 6 * D * DEPTH + 768 * DEPTH + A_V * DEPTH + 9 * D * DEPTH
    pad = -n_small % (512 * LANE)
    flat = jnp.concatenate([jnp.stack(d_ng).reshape(-1), jnp.concatenate(d_lb, axis=0).reshape(-1),
                            jnp.concatenate(d_hng, axis=0).reshape(-1), jnp.concatenate(d_mod, axis=0).reshape(-1),
                            jnp.zeros((pad,), f32)])
    small_all = all_gather_devs(flat.reshape(-1, LANE), name="ag8_small")
    total = sum_slots(small_all).reshape(-1)
    o1 = 6 * D * DEPTH
    o2 = o1 + 768 * DEPTH
    o3 = o2 + A_V * DEPTH
    g_ng_full = total[:o1].reshape(DEPTH, 6, D)
    g_lb_all = total[o1:o2].reshape(DEPTH, 768)
    g_small = {
        "norm_g": lax.dynamic_slice_in_dim(g_ng_full, chip * (D // 4), D // 4, axis=2),
        "hgrn_lb_logits": lb_bwd(hgrn_lb_logits, g_lb_all),
        "hgrn_norm_g": total[o2:o3].reshape(DEPTH, A_V),
        "b_ada": total[o3:n_small].reshape(DEPTH, 9 * D),
    }
    dmod_all = small_all.reshape(8, -1)[:, o3:n_small].reshape(8, DEPTH, 9 * D).transpose(1, 0, 2)
    dm_sh = lax.dynamic_slice_in_dim(dmod_all, chip * ADA_N, ADA_N, axis=2)

    out_g, out_d, out_m, out_v = {}, {}, {}, {}
    out_g["w_ada"], out_d["w_ada"], out_m["w_ada"], out_v["w_ada"] = ada_bwd_adam(c_all, dm_sh, w_ada, m_w_ada, v_w_ada)
    for n in BIG:
        out_g[n] = jnp.stack(big_grads[n])
    out_g.update(g_small)
    for n in order:
        if n != "w_ada":
            out_d[n], out_m[n], out_v[n] = adam(weights[n], out_g[n], mom[n], var[n])
    return (loss, dx[None], *[out_g[n] for n in order], *[out_d[n] for n in order],
            *[out_m[n] for n in order], *[out_v[n] for n in order])
```

```python
import functools
import math

import jax
import jax.numpy as jnp
from jax import lax
from jax.experimental import pallas as pl
from jax.experimental.pallas import tpu as pltpu

f32, bf16 = jnp.float32, jnp.bfloat16

D = 1024
DEPTH = 4
D_FF = 2816
EPS = 1e-6
NEG_BIG = -1e30
TINY = 1e-30
A_HEADS, A_K, A_V, A_CHUNK = 6, 128, 64, 64
A_SUB = 16
A_CLAMP = 80.0
B_HEADS, HD = 6, 64
C_GROUPS = ((128, 1), (512, 4), (2048, 16))
C_BLK = 128
IN_COLS = 8832
O_AQ, O_AF, O_AI, O_AG = 0, 768, 1536, 1920
O_BQ, O_BK, O_BV = 2304, 2688, 3072
O_CQ, O_CK, O_CV = 3456, 4224, 4992
O_GATE = 5760
LANE = 128
ADAM_LR, ADAM_B1, ADAM_B2, ADAM_EPS, ADAM_WD, ADAM_STEP = 0.001, 0.9, 0.999, 1e-08, 0.01, 10
MESH = pl.DeviceIdType.MESH
VMEM_LIMIT = 56 * 1024 * 1024


def _alibi_slopes(n):
    def pow2(m):
        start = 2.0 ** (-8.0 / m)
        return [start ** (i + 1) for i in range(m)]
    if math.log2(n).is_integer():
        s = pow2(n)
    else:
        c = 2 ** int(math.floor(math.log2(n)))
        s = pow2(c) + pow2(2 * c)[0::2][: n - c]
    return sorted(s, reverse=True)


C_SLOPES = _alibi_slopes(12)


def _tile(n, prefs):
    for p in prefs:
        if n % p == 0:
            return p
    return n


def _cp(sem):
    return pltpu.CompilerParams(dimension_semantics=sem, vmem_limit_bytes=VMEM_LIMIT)


def _sig(x):
    return 1.0 / (1.0 + jnp.exp(-x))


def _dot(a, b, dn, precision=None):
    return lax.dot_general(a, b, (dn, ((), ())), preferred_element_type=f32, precision=precision)


NN = ((1,), (0,))
NT = ((1,), (1,))
TN = ((0,), (0,))


class Plan:
    def __init__(self, ins, outs, n_remote, n_local, start, wait):
        self.ins, self.outs, self.n_remote, self.n_local, self.start, self.wait = ins, outs, n_remote, n_local, start, wait

    def sems(self):
        return [pltpu.SemaphoreType.DMA((self.n_remote,)), pltpu.SemaphoreType.DMA((self.n_remote,)),
                pltpu.SemaphoreType.DMA((max(self.n_local, 1),))]


def _call(body, *, name, grid, in_specs, out_specs, out_shape, sem, args, scratch_shapes=(), side=None):
    if side is None:
        return pl.pallas_call(body, name=name, grid=grid, in_specs=in_specs, out_specs=out_specs, out_shape=out_shape,
                              scratch_shapes=list(scratch_shapes), compiler_params=_cp(sem))(*args), None
    any_spec = pl.BlockSpec(memory_space=pl.ANY)
    n_in, n_out, n_scr = len(in_specs), len(out_specs), len(scratch_shapes)
    s_in, s_out = len(side.ins), len(side.outs)

    def hosted(*refs):
        ins, rest = refs[:n_in], refs[n_in:]
        sins, rest = rest[:s_in], rest[s_in:]
        outs, rest = rest[:n_out], rest[n_out:]
        souts, rest = rest[:s_out], rest[s_out:]
        scr, sems = rest[:n_scr], rest[n_scr:]
        pids = [pl.program_id(d) for d in range(len(grid))]
        first = functools.reduce(jnp.logical_and, [p == 0 for p in pids])
        last = functools.reduce(jnp.logical_and, [p == g - 1 for p, g in zip(pids, grid)])

        @pl.when(first)
        def _():
            side.start(sins, souts, *sems)

        body(*ins, *outs, *scr)

        @pl.when(last)
        def _():
            side.wait(sins, souts, *sems)

    res = pl.pallas_call(
        hosted, name=name, grid=grid, in_specs=list(in_specs) + [any_spec] * s_in,
        out_specs=list(out_specs) + [any_spec] * s_out, out_shape=list(out_shape) + list(side.outs),
        scratch_shapes=list(scratch_shapes) + side.sems(), compiler_params=_cp(("arbitrary",) * len(grid)),
    )(*args, *side.ins)
    return res[:n_out], res[n_out:]


MM_TILES = {
    (4096, 1024, 2816): (1024, 512, 2816),
    (4096, 1024, 5632): (512, 512, 5632),
    (1024, 5632, 4096): (512, 512, 4096),
    (2816, 1024, 4096): (704, 512, 4096),
    (4096, 8832, 1024): (512, 2944, 1024),
    (4096, 1024, 8832): (1024, 512, 2944),
    (1024, 8832, 4096): (512, 2944, 1024),
    (1024, 1024, 4096): (512, 512, 4096),
    (384, 1024, 4096): (384, 512, 4096),
    (256, 1024, 4096): (256, 512, 4096),
}


def mm(a, b, *, out_dtype=f32, name="mm", side=None):
    M, K = a.shape
    K2, N = b.shape
    assert K == K2, (a.shape, b.shape)
    tm, tn, tk = MM_TILES.get((M, N, K), (_tile(M, (1024, 704, 512, 384, 256, 128)), _tile(N, (512, 384, 256, 128)),
                                          _tile(K, (1024, 512, 1408, 384, 256, 128))))
    nk = K // tk

    def body(a_ref, b_ref, o_ref, *acc):
        p = _dot(a_ref[...].astype(bf16), b_ref[...].astype(bf16), NN)
        if nk == 1:
            o_ref[...] = p.astype(out_dtype)
            return
        acc_ref, = acc
        k = pl.program_id(2)

        @pl.when(k == 0)
        def _():
            acc_ref[...] = p

        @pl.when(k > 0)
        def _():
            acc_ref[...] += p

        @pl.when(k == nk - 1)
        def _():
            o_ref[...] = acc_ref[...].astype(out_dtype)

    outs, souts = _call(
        body, name=name, grid=(M // tm, N // tn, nk),
        in_specs=[pl.BlockSpec((tm, tk), lambda i, j, k: (i, k)), pl.BlockSpec((tk, tn), lambda i, j, k: (k, j))],
        out_specs=[pl.BlockSpec((tm, tn), lambda i, j, k: (i, j))],
        out_shape=[jax.ShapeDtypeStruct((M, N), out_dtype)],
        scratch_shapes=[pltpu.VMEM((tm, tn), f32)] if nk > 1 else [],
        sem=("parallel", "parallel", "arbitrary"), args=(a, b), side=side)
    return outs[0] if side is None else (outs[0], souts)


FF_TM = 1024
FF_T = 256
FF_NB = D_FF // FF_T


def ffn_in_swiglu(h, w_in):
    S = h.shape[0]

    def body(h_ref, wa_ref, wb_ref, a_ref, b_ref, s_ref):
        hv = h_ref[...]
        a = _dot(hv, wa_ref[...], NN)
        b = _dot(hv, wb_ref[...], NN)
        a_ref[...] = a.astype(bf16)
        b_ref[...] = b.astype(bf16)
        s_ref[...] = (a * _sig(a) * b).astype(bf16)

    ospec = pl.BlockSpec((FF_TM, FF_T), lambda i, j: (i, j))
    osd = jax.ShapeDtypeStruct((S, D_FF), bf16)
    return pl.pallas_call(
        body, name="ffn_in", grid=(S // FF_TM, FF_NB),
        in_specs=[pl.BlockSpec((FF_TM, D), lambda i, j: (i, 0)), pl.BlockSpec((D, FF_T), lambda i, j: (0, j)),
                  pl.BlockSpec((D, FF_T), lambda i, j: (0, j + FF_NB))],
        out_specs=[ospec] * 3, out_shape=[osd] * 3, compiler_params=_cp(("parallel", "parallel")),
    )(h, w_in, w_in)


def ffn_du(dy, w_out_t, ua, ub):
    S = dy.shape[0]
    tm = 512

    def body(dy_ref, w_ref, a_ref, b_ref, du_ref):
        dyv = dy_ref[...]
        for j in range(FF_NB):
            cols = slice(j * FF_T, (j + 1) * FF_T)
            ds = _dot(dyv, w_ref[:, cols], NN)
            a, b = a_ref[:, cols].astype(f32), b_ref[:, cols].astype(f32)
            sg = _sig(a)
            du_ref[:, cols] = (ds * b * sg * (1.0 + a * (1.0 - sg))).astype(bf16)
            du_ref[:, D_FF + j * FF_T:D_FF + (j + 1) * FF_T] = (ds * a * sg).astype(bf16)

    half = pl.BlockSpec((tm, D_FF), lambda i: (i, 0))
    return pl.pallas_call(
        body, name="ffn_du", grid=(S // tm,),
        in_specs=[pl.BlockSpec((tm, D), lambda i: (i, 0)), pl.BlockSpec((D, D_FF), lambda i: (0, 0)), half, half],
        out_specs=pl.BlockSpec((tm, 2 * D_FF), lambda i: (i, 0)),
        out_shape=jax.ShapeDtypeStruct((S, 2 * D_FF), bf16), compiler_params=_cp(("parallel",)),
    )(dy, w_out_t, ua, ub)


TR = 256


def _row_spec(cols=D):
    return pl.BlockSpec((TR, cols), lambda i: (i, 0))


def _vec_spec(cols=D):
    return pl.BlockSpec((1, cols), lambda i: (0, 0))


def prenorm(x, a_vec, sh_vec):
    S = x.shape[0]

    def body(x_ref, a_ref, s_ref, h_ref):
        xv = x_ref[...]
        rstd = lax.rsqrt(jnp.mean(xv * xv, axis=1, keepdims=True) + EPS)
        h_ref[...] = (xv * rstd * a_ref[...] + s_ref[...]).astype(bf16)

    return pl.pallas_call(
        body, name="prenorm", grid=(S // TR,),
        in_specs=[_row_spec(), _vec_spec(), _vec_spec()], out_specs=_row_spec(),
        out_shape=jax.ShapeDtypeStruct((S, D), bf16), compiler_params=_cp(("parallel",)),
    )(x, a_vec, sh_vec)


def postnorm(x, y, b_vec):
    S = x.shape[0]

    def body(x_ref, y_ref, b_ref, o_ref):
        yv = y_ref[...]
        rstd = lax.rsqrt(jnp.mean(yv * yv, axis=1, keepdims=True) + EPS)
        o_ref[...] = x_ref[...] + b_ref[...] * (yv * rstd)

    return pl.pallas_call(
        body, name="postnorm", grid=(S // TR,),
        in_specs=[_row_spec(), _row_spec(), _vec_spec()], out_specs=_row_spec(),
        out_shape=jax.ShapeDtypeStruct((S, D), f32), compiler_params=_cp(("parallel",)),
    )(x, y, b_vec)


def post_bwd(dout, y, b_vec):
    S = dout.shape[0]

    def body(d_ref, y_ref, b_ref, dy_ref, db_ref):
        i = pl.program_id(0)
        yv, dv = y_ref[...], d_ref[...]
        rstd = lax.rsqrt(jnp.mean(yv * yv, axis=1, keepdims=True) + EPS)
        yh = yv * rstd
        dyh = dv * b_ref[...]
        dy_ref[...] = (rstd * (dyh - yh * jnp.mean(dyh * yh, axis=1, keepdims=True))).astype(bf16)
        part = jnp.sum(dv * yh, axis=0, keepdims=True)

        @pl.when(i == 0)
        def _():
            db_ref[...] = part

        @pl.when(i > 0)
        def _():
            db_ref[...] += part

    return pl.pallas_call(
        body, name="post_bwd", grid=(S // TR,),
        in_specs=[_row_spec(), _row_spec(), _vec_spec()], out_specs=[_row_spec(), _vec_spec()],
        out_shape=[jax.ShapeDtypeStruct((S, D), bf16), jax.ShapeDtypeStruct((1, D), f32)],
        compiler_params=_cp(("arbitrary",)),
    )(dout, y, b_vec)


def pre_bwd(dout, dh, x, a_vec):
    S = dout.shape[0]

    def body(d_ref, dh_ref, x_ref, a_ref, dx_ref, ds_ref, da_ref):
        i = pl.program_id(0)
        xv, dhv = x_ref[...], dh_ref[...]
        rstd = lax.rsqrt(jnp.mean(xv * xv, axis=1, keepdims=True) + EPS)
        n1 = xv * rstd
        dn = dhv * a_ref[...]
        dx_ref[...] = d_ref[...] + rstd * (dn - n1 * jnp.mean(dn * n1, axis=1, keepdims=True))
        p_s = jnp.sum(dhv, axis=0, keepdims=True)
        p_a = jnp.sum(dhv * n1, axis=0, keepdims=True)

        @pl.when(i == 0)
        def _():
            ds_ref[...] = p_s
            da_ref[...] = p_a

        @pl.when(i > 0)
        def _():
            ds_ref[...] += p_s
            da_ref[...] += p_a

    return pl.pallas_call(
        body, name="pre_bwd", grid=(S // TR,),
        in_specs=[_row_spec(), _row_spec(), _row_spec(), _vec_spec()],
        out_specs=[_row_spec(), _vec_spec(), _vec_spec()],
        out_shape=[jax.ShapeDtypeStruct((S, D), f32), jax.ShapeDtypeStruct((1, D), f32), jax.ShapeDtypeStruct((1, D), f32)],
        compiler_params=_cp(("arbitrary",)),
    )(dout, dh, x, a_vec)


def loss_grad(y, tgt):
    S = y.shape[0]

    def body(y_ref, t_ref, dy_ref, l_ref):
        i = pl.program_id(0)
        e = y_ref[...] - t_ref[...]
        dy_ref[...] = e * (1.0 / D)
        part = jnp.sum(jnp.sum(e * e, axis=1, keepdims=True), axis=0, keepdims=True) * (0.5 / D)
        part = jnp.broadcast_to(part, (8, LANE))

        @pl.when(i == 0)
        def _():
            l_ref[...] = part

        @pl.when(i > 0)
        def _():
            l_ref[...] += part

    return pl.pallas_call(
        body, name="loss_grad", grid=(S // TR,),
        in_specs=[_row_spec(), _row_spec()],
        out_specs=[_row_spec(), pl.BlockSpec((8, LANE), lambda i: (0, 0))],
        out_shape=[jax.ShapeDtypeStruct((S, D), f32), jax.ShapeDtypeStruct((8, LANE), f32)],
        compiler_params=_cp(("arbitrary",)),
    )(y, tgt)


G_NB = D // LANE
G_TR = 2048
G_OFF = O_GATE // LANE


def gate_merge(u, pa, pb, pc):
    S = u.shape[0]

    def body(g0, g1, g2, a, b, c, o_ref):
        o_ref[...] = (_sig(g0[...]) * a[...] + _sig(g1[...]) * b[...] + _sig(g2[...]) * c[...]).astype(bf16)

    gs = [pl.BlockSpec((G_TR, LANE), functools.partial(lambda i, j, k: (i, G_OFF + G_NB * k + j), k=k)) for k in range(3)]
    ps = pl.BlockSpec((G_TR, LANE), lambda i, j: (i, j))
    return pl.pallas_call(
        body, name="gate_merge", grid=(S // G_TR, G_NB),
        in_specs=gs + [ps, ps, ps], out_specs=ps,
        out_shape=jax.ShapeDtypeStruct((S, D), bf16), compiler_params=_cp(("parallel", "parallel")),
    )(u, u, u, pa, pb, pc)


def gate_bwd(dm, u, pa, pb, pc):
    S = u.shape[0]

    def body(dm_ref, g0, g1, g2, a, b, c, da, db, dc, dg0, dg1, dg2):
        d = dm_ref[...]
        for g, p, dp, dg in ((g0, a, da, dg0), (g1, b, db, dg1), (g2, c, dc, dg2)):
            s = _sig(g[...])
            dp[...] = (d * s).astype(bf16)
            dg[...] = (d * p[...] * s * (1.0 - s)).astype(bf16)

    gs = [pl.BlockSpec((G_TR, LANE), functools.partial(lambda i, j, k: (i, G_OFF + G_NB * k + j), k=k)) for k in range(3)]
    ps = pl.BlockSpec((G_TR, LANE), lambda i, j: (i, j))
    osd = jax.ShapeDtypeStruct((S, D), bf16)
    return pl.pallas_call(
        body, name="gate_bwd", grid=(S // G_TR, G_NB),
        in_specs=[ps] + gs + [ps, ps, ps], out_specs=[ps] * 6, out_shape=[osd] * 6,
        compiler_params=_cp(("parallel", "parallel")),
    )(dm, u, u, u, pa, pb, pc)


A_TB = 512
A_NCH = A_TB // A_CHUNK
A_NSUB = A_CHUNK // A_SUB
A_HP = 6
A_KW, A_VW = A_HP * A_K, A_HP * A_V


def _hgrn_gates(qr, fr, lbh):
    sq = _sig(qr)
    sig = _sig(fr)
    f = lbh + (1.0 - lbh) * sig
    logf = jnp.log(jnp.maximum(f, TINY))
    k = (1.0 - lbh) * (1.0 - sig)
    return qr * sq, sq, sig, f, logf, k


def _hgrn_intra(qf, k, b, causal):
    qts, kts, eqs, eks, blocks = [], [], [], [], []
    for sb in range(A_NSUB):
        rs = sb * A_SUB
        r = b[rs - 1:rs, :] if sb else jnp.zeros((1, A_K), f32)
        eq = jnp.exp(b[rs:rs + A_SUB, :] - r)
        ek = jnp.exp(jnp.minimum(r - b, A_CLAMP))
        qt = (qf[rs:rs + A_SUB, :] * eq).astype(bf16)
        kt = (k * ek).astype(bf16)
        blocks.append(_dot(qt, kt, NT))
        qts.append(qt), kts.append(kt), eqs.append(eq), eks.append(ek)
    a = jnp.where(causal, jnp.concatenate(blocks, axis=0), 0.0)
    return a, qts, kts, eqs, eks


def _tri():
    r = lax.broadcasted_iota(jnp.int32, (A_CHUNK, A_CHUNK), 0)
    c = lax.broadcasted_iota(jnp.int32, (A_CHUNK, A_CHUNK), 1)
    return r >= c


def _hgrn_in_specs(rev_nb=None):
    def im(col):
        if rev_nb is None:
            return lambda p, i: (i, col + p)
        return lambda p, i: (rev_nb - 1 - i, col + p)
    return [pl.BlockSpec((A_TB, A_KW), im(O_AQ // A_KW)), pl.BlockSpec((A_TB, A_KW), im(O_AF // A_KW)),
            pl.BlockSpec((A_TB, A_VW), im(O_AI // A_VW)), pl.BlockSpec((A_TB, A_VW), im(O_AG // A_VW)),
            pl.BlockSpec((1, A_KW), lambda p, i: (0, p)), pl.BlockSpec((1, A_V), lambda p, i: (0, 0))]


def hgrn_fwd(u, lb, ng, side=None):
    S = u.shape[0]
    nb = S // A_TB

    def body(q_ref, f_ref, i_ref, g_ref, lb_ref, ng_ref, o_ref, ya_ref, st_ref, state):
        @pl.when(pl.program_id(1) == 0)
        def _():
            state[...] = jnp.zeros_like(state)

        causal = _tri()
        tri = causal.astype(f32)

        def chunk(n, carry):
            rows = pl.ds(pl.multiple_of(n * A_CHUNK, A_CHUNK), A_CHUNK)
            o_parts, y_parts = [], []
            for hh in range(A_HP):
                ks = slice(hh * A_K, (hh + 1) * A_K)
                vs = slice(hh * A_V, (hh + 1) * A_V)
                qf, _, _, _, logf, k = _hgrn_gates(q_ref[rows, ks], f_ref[rows, ks], lb_ref[:, ks])
                vi = i_ref[rows, vs].astype(bf16)
                gg = g_ref[rows, vs]
                b = _dot(tri, logf, NN, precision=lax.Precision.HIGHEST)
                s0 = state[hh]
                st_ref[n, hh] = s0
                o = _dot((qf * jnp.exp(b)).astype(bf16), s0.astype(bf16), NT)
                a, _, _, _, _ = _hgrn_intra(qf, k, b, causal)
                o = o + _dot(a.astype(bf16), vi, NN)
                bend = b[A_CHUNK - 1:A_CHUNK, :]
                ke = (k * jnp.exp(bend - b)).astype(bf16)
                state[hh] = s0 * jnp.exp(bend) + _dot(vi, ke, TN)
                rstd = lax.rsqrt(jnp.mean(o * o, axis=1, keepdims=True) + EPS)
                o_parts.append(o)
                y_parts.append(o * rstd * ng_ref[...] * (gg * _sig(gg)))
            o_ref[rows, :] = jnp.concatenate(o_parts, axis=1)
            ya_ref[rows, :] = jnp.concatenate(y_parts, axis=1).astype(bf16)
            return carry

        lax.fori_loop(0, A_NCH, chunk, 0)

    return _call(
        body, name="hgrn_fwd", grid=(A_HEADS // A_HP, nb),
        in_specs=_hgrn_in_specs(),
        out_specs=[pl.BlockSpec((A_TB, A_VW), lambda p, i: (i, p)), pl.BlockSpec((A_TB, A_VW), lambda p, i: (i, p)),
                   pl.BlockSpec((A_NCH, A_HP, A_V, A_K), lambda p, i: (i, p, 0, 0))],
        out_shape=[jax.ShapeDtypeStruct((S, 384), f32), jax.ShapeDtypeStruct((S, 384), bf16),
                   jax.ShapeDtypeStruct((S // A_CHUNK, A_HEADS, A_V, A_K), f32)],
        scratch_shapes=[pltpu.VMEM((A_HP, A_V, A_K), f32)],
        sem=("parallel", "arbitrary"), args=(u, u, u, u, lb, ng), side=side)


def hgrn_bwd(u, lb, ng, o, st, dya, side=None):
    S = u.shape[0]
    nb = S // A_TB

    def body(q_ref, f_ref, i_ref, g_ref, lb_ref, ng_ref, o_ref, st_ref, dy_ref,
             dq_ref, df_ref, di_ref, dg_ref, dlb_ref, dng_ref, dstate):
        @pl.when(pl.program_id(1) == 0)
        def _():
            dstate[...] = jnp.zeros_like(dstate)
            dlb_ref[...] = jnp.zeros_like(dlb_ref)
            dng_ref[...] = jnp.zeros_like(dng_ref)

        causal = _tri()
        tri = causal.astype(f32)

        def chunk(it, carry):
            n = A_NCH - 1 - it
            rows = pl.ds(pl.multiple_of(n * A_CHUNK, A_CHUNK), A_CHUNK)
            dq_p, df_p, di_p, dg_p, dlb_p = [], [], [], [], []
            dng_acc = jnp.zeros((1, A_V), f32)
            for hh in range(A_HP):
                ks = slice(hh * A_K, (hh + 1) * A_K)
                vs = slice(hh * A_V, (hh + 1) * A_V)
                lbh = lb_ref[:, ks]
                qr = q_ref[rows, ks]
                qf, sq, sig, f, logf, k = _hgrn_gates(qr, f_ref[rows, ks], lbh)
                vi = i_ref[rows, vs].astype(bf16)
                gg = g_ref[rows, vs]
                b = _dot(tri, logf, NN, precision=lax.Precision.HIGHEST)
                eb = jnp.exp(b)
                bend = b[A_CHUNK - 1:A_CHUNK, :]
                eend = jnp.exp(bend)
                ekend = jnp.exp(bend - b)
                qe = (qf * eb).astype(bf16)
                ke = (k * ekend).astype(bf16)
                s0 = st_ref[n, hh]
                dsend = dstate[hh]
                ov = o_ref[rows, vs]
                dy = dy_ref[rows, vs]
                rstd = lax.rsqrt(jnp.mean(ov * ov, axis=1, keepdims=True) + EPS)
                oh = ov * rstd
                sg = _sig(gg)
                d_on = dy * (gg * sg)
                dg_p.append(dy * oh * ng_ref[...] * (sg * (1.0 + gg * (1.0 - sg))))
                dng_acc = dng_acc + jnp.sum(d_on * oh, axis=0, keepdims=True)
                doh = d_on * ng_ref[...]
                do = (rstd * (doh - oh * jnp.mean(doh * oh, axis=1, keepdims=True))).astype(bf16)
                a, qts, kts, eqs, eks = _hgrn_intra(qf, k, b, causal)
                da = jnp.where(causal, _dot(do, vi, NT), 0.0).astype(bf16)
                dsb = dsend.astype(bf16)
                dv = _dot(a.astype(bf16), do, TN) + _dot(ke, dsb, NT)
                dq = _dot(do, s0.astype(bf16), NN) * eb
                dk_state = _dot(vi, dsb, NN) * ekend
                dk = dk_state
                dq_i = []
                for sb in range(A_NSUB):
                    da_sb = da[sb * A_SUB:(sb + 1) * A_SUB, :]
                    dq_i.append(_dot(da_sb, kts[sb], NN) * eqs[sb])
                    dk = dk + _dot(da_sb, qts[sb], TN) * eks[sb]
                dq = dq + jnp.concatenate(dq_i, axis=0)
                db = qf * dq - k * dk
                extra = jnp.sum(k * dk_state, axis=0, keepdims=True) + eend * jnp.sum(s0 * dsend, axis=0, keepdims=True)
                dlogf = _dot(tri, db, TN, precision=lax.Precision.HIGHEST) + extra
                dstate[hh] = _dot(do, qe, TN) + eend * dsend
                d_pre = jnp.where(f > TINY, dlogf / f, 0.0) - dk
                dlb_p.append(jnp.sum((1.0 - sig) * d_pre, axis=0, keepdims=True))
                df_p.append((1.0 - lbh) * d_pre * sig * (1.0 - sig))
                dq_p.append(dq * (sq * (1.0 + qr * (1.0 - sq))))
                di_p.append(dv)
            dq_ref[rows, :] = jnp.concatenate(dq_p, axis=1).astype(bf16)
            df_ref[rows, :] = jnp.concatenate(df_p, axis=1).astype(bf16)
            di_ref[rows, :] = jnp.concatenate(di_p, axis=1).astype(bf16)
            dg_ref[rows, :] = jnp.concatenate(dg_p, axis=1).astype(bf16)
            dlb_ref[...] += jnp.concatenate(dlb_p, axis=1)
            dng_ref[0] += dng_acc
            return carry

        lax.fori_loop(0, A_NCH, chunk, 0)

    rev = lambda p, i: (nb - 1 - i, p)
    return _call(
        body, name="hgrn_bwd", grid=(A_HEADS // A_HP, nb),
        in_specs=_hgrn_in_specs(nb) + [pl.BlockSpec((A_TB, A_VW), rev),
                                       pl.BlockSpec((A_NCH, A_HP, A_V, A_K), lambda p, i: (nb - 1 - i, p, 0, 0)),
                                       pl.BlockSpec((A_TB, A_VW), rev)],
        out_specs=[pl.BlockSpec((A_TB, A_KW), rev), pl.BlockSpec((A_TB, A_KW), rev),
                   pl.BlockSpec((A_TB, A_VW), rev), pl.BlockSpec((A_TB, A_VW), rev),
                   pl.BlockSpec((1, A_KW), lambda p, i: (0, p)), pl.BlockSpec((1, 1, A_V), lambda p, i: (p, 0, 0))],
        out_shape=[jax.ShapeDtypeStruct((S, 768), bf16), jax.ShapeDtypeStruct((S, 768), bf16),
                   jax.ShapeDtypeStruct((S, 384), bf16), jax.ShapeDtypeStruct((S, 384), bf16),
                   jax.ShapeDtypeStruct((1, 768), f32), jax.ShapeDtypeStruct((A_HEADS // A_HP, 1, A_V), f32)],
        scratch_shapes=[pltpu.VMEM((A_HP, A_V, A_K), f32)],
        sem=("parallel", "arbitrary"), args=(u, u, u, u, lb, ng, o, st, dya), side=side)


B_TK = 128
SCALE = HD ** -0.5


def _split(x):
    hi = x.astype(bf16)
    return hi, (x - hi.astype(f32)).astype(bf16)


def _dot2(x, m, dn):
    hi, lo = _split(x)
    return _dot(hi, m, dn) + _dot(lo, m, dn)


def _sb_block(qs, kh, mask, m_gt, c):
    z = _dot(qs, kh, NT)
    sp = jnp.maximum(z, 0.0) + jnp.log(1.0 + jnp.exp(-jnp.abs(z)))
    lneg = jnp.where(mask, -sp, 0.0)
    lsz = z - sp
    suf = _dot2(lneg, m_gt, NN) + c
    a = jnp.where(mask, jnp.exp(lsz + suf), 0.0)
    return lneg, lsz, a


def _sb_masks(tq, i, jj):
    t_idx = i * tq + lax.broadcasted_iota(jnp.int32, (tq, B_TK), 0)
    s_idx = jj * B_TK + lax.broadcasted_iota(jnp.int32, (tq, B_TK), 1)
    return s_idx < t_idx


def _sb_tri(strict):
    r = lax.broadcasted_iota(jnp.int32, (B_TK, B_TK), 0)
    c = lax.broadcasted_iota(jnp.int32, (B_TK, B_TK), 1)
    return (r > c if strict else r >= c).astype(bf16)


B_DEAD = -88.0


def _sb_walk(nkb, step, init):
    def cond(state):
        it, alive, _ = state
        return jnp.logical_and(it < nkb, alive)

    def body(state):
        it, _, carry = state
        carry = step(it, carry)
        top = jnp.max(functools.reduce(jnp.maximum, [h[1] for h in carry]))
        return it + 1, top > B_DEAD, carry

    return lax.while_loop(cond, body, (jnp.int32(0), jnp.bool_(True), init))[2]


B_HP = 6
B_W = B_HP * HD


def sb_fwd(u, side=None):
    S = u.shape[0]
    tq = 128

    def body(q_ref, k_ref, v_ref, o_ref):
        i = pl.program_id(1)
        nkb = (i + 1) * (tq // B_TK)
        m_gt = _sb_tri(True)
        qs = [(q_ref[:, hh * HD:(hh + 1) * HD] * SCALE).astype(bf16) for hh in range(B_HP)]

        def step(it, carry):
            jj = nkb - 1 - it
            rows = pl.ds(pl.multiple_of(jj * B_TK, B_TK), B_TK)
            mask = _sb_masks(tq, i, jj)
            kb, vb = k_ref[rows, :], v_ref[rows, :]
            out = []
            for hh in range(B_HP):
                acc, c = carry[hh]
                kh = kb[:, hh * HD:(hh + 1) * HD].astype(bf16)
                vh = vb[:, hh * HD:(hh + 1) * HD].astype(bf16)
                lneg, _, a = _sb_block(qs[hh], kh, mask, m_gt, c)
                out.append((acc + _dot2(a, vh, NN), c + jnp.sum(lneg, axis=1, keepdims=True)))
            return tuple(out)

        z0 = (jnp.zeros((tq, HD), f32), jnp.zeros((tq, 1), f32))
        res = _sb_walk(nkb, step, (z0,) * B_HP)
        o_ref[...] = jnp.concatenate([r[0] for r in res], axis=1)

    outs, souts = _call(
        body, name="sb_fwd", grid=(B_HEADS // B_HP, S // tq),
        in_specs=[pl.BlockSpec((tq, B_W), lambda p, i: (i, O_BQ // B_W + p)),
                  pl.BlockSpec((S, B_W), lambda p, i: (0, O_BK // B_W + p)),
                  pl.BlockSpec((S, B_W), lambda p, i: (0, O_BV // B_W + p))],
        out_specs=[pl.BlockSpec((tq, B_W), lambda p, i: (i, p))],
        out_shape=[jax.ShapeDtypeStruct((S, 384), f32)],
        sem=("parallel", "arbitrary"), args=(u, u, u), side=side)
    return outs[0], souts


def sb_bwd(u, yb, dyb, side=None):
    S = u.shape[0]
    tq = 128
    nq = S // tq

    def body(q_ref, k_ref, v_ref, y_ref, dy_ref, dq_ref, dk_out, dv_out, dk_ref, dv_ref, out_sem):
        p, i = pl.program_id(0), pl.program_id(1)

        @pl.when(i == 0)
        def _():
            dk_ref[...] = jnp.zeros_like(dk_ref)
            dv_ref[...] = jnp.zeros_like(dv_ref)

        nkb = (i + 1) * (tq // B_TK)
        m_gt = _sb_tri(True)
        m_ge = _sb_tri(False)
        qs, dos, tot = [], [], []
        for hh in range(B_HP):
            hs = slice(hh * HD, (hh + 1) * HD)
            qs.append((q_ref[:, hs] * SCALE).astype(bf16))
            dob = dy_ref[:, hs].astype(bf16)
            dos.append(dob)
            tot.append(jnp.sum(dob.astype(f32) * y_ref[:, hs], axis=1, keepdims=True))

        def step(it, carry):
            jj = nkb - 1 - it
            rows = pl.ds(pl.multiple_of(jj * B_TK, B_TK), B_TK)
            mask = _sb_masks(tq, i, jj)
            kb, vb = k_ref[rows, :], v_ref[rows, :]
            out, dk_p, dv_p = [], [], []
            for hh in range(B_HP):
                dq, c, cg = carry[hh]
                kh = kb[:, hh * HD:(hh + 1) * HD].astype(bf16)
                vh = vb[:, hh * HD:(hh + 1) * HD].astype(bf16)
                lneg, lsz, a = _sb_block(qs[hh], kh, mask, m_gt, c)
                g = a * _dot(dos[hh], vh, NT)
                pre = tot[hh] - cg - _dot2(g, m_ge, NN)
                beta = jnp.exp(lsz)
                dz = jnp.where(mask, g * (1.0 - beta) - beta * pre, 0.0).astype(bf16)
                dk_p.append(_dot(dz, qs[hh], TN))
                dv_p.append(_dot(a.astype(bf16), dos[hh], TN))
                out.append((dq + _dot(dz, kh, NN), c + jnp.sum(lneg, axis=1, keepdims=True),
                            cg + jnp.sum(g, axis=1, keepdims=True)))
            dk_ref[rows, :] += jnp.concatenate(dk_p, axis=1)
            dv_ref[rows, :] += jnp.concatenate(dv_p, axis=1)
            return tuple(out)

        z0 = (jnp.zeros((tq, HD), f32), jnp.zeros((tq, 1), f32), jnp.zeros((tq, 1), f32))
        res = _sb_walk(nkb, step, (z0,) * B_HP)
        dq_ref[...] = jnp.concatenate([r[0] for r in res], axis=1) * SCALE

        @pl.when(i == nq - 1)
        def _():
            cols = pl.ds(pl.multiple_of(p * B_W, LANE), B_W)
            ck = pltpu.make_async_copy(dk_ref, dk_out.at[:, cols], out_sem.at[0])
            cv = pltpu.make_async_copy(dv_ref, dv_out.at[:, cols], out_sem.at[1])
            ck.start()
            cv.start()
            ck.wait()
            cv.wait()

    row = pl.BlockSpec((tq, B_W), lambda p, i: (i, p))
    hbm = pl.BlockSpec(memory_space=pl.ANY)
    osd = jax.ShapeDtypeStruct((S, 384), f32)
    return _call(
        body, name="sb_bwd", grid=(B_HEADS // B_HP, nq),
        in_specs=[pl.BlockSpec((tq, B_W), lambda p, i: (i, O_BQ // B_W + p)),
                  pl.BlockSpec((S, B_W), lambda p, i: (0, O_BK // B_W + p)),
                  pl.BlockSpec((S, B_W), lambda p, i: (0, O_BV // B_W + p)), row, row],
        out_specs=[row, hbm, hbm], out_shape=[osd, osd, osd],
        scratch_shapes=[pltpu.VMEM((S, B_W), f32), pltpu.VMEM((S, B_W), f32), pltpu.SemaphoreType.DMA((2,))],
        sem=("parallel", "arbitrary"), args=(u, u, u, yb, dyb), side=side)


def _dil_rows(i, rho, r):
    if r == 1:
        return pl.ds(pl.multiple_of(i * C_BLK, C_BLK), C_BLK)
    return pl.ds(i * (C_BLK * r) + rho, C_BLK, stride=r)


def _dil_scores(qs, kc, kp, i, slope_r):
    qi = lax.broadcasted_iota(jnp.int32, (C_BLK, C_BLK), 0)
    kj = lax.broadcasted_iota(jnp.int32, (C_BLK, C_BLK), 1)
    d_c = qi - kj
    d_p = d_c + C_BLK
    ok_c = d_c >= 0
    ok_p = jnp.logical_and(d_c <= 0, i > 0)
    s_c = jnp.where(ok_c, _dot(qs, kc, NT) - slope_r * d_c.astype(f32), NEG_BIG)
    s_p = jnp.where(ok_p, _dot(qs, kp, NT) - slope_r * d_p.astype(f32), NEG_BIG)
    return s_c, s_p, ok_c, ok_p


def _dil_slope(g, r, hh):
    pair = pl.program_id(0)
    return jnp.where(pair == 0, C_SLOPES[4 * g + hh] * r, C_SLOPES[4 * g + 2 + hh] * r).astype(f32)


def _dil_u_specs(g, S):
    def im(off):
        return lambda p, rho: (0, (off + g * 256) // LANE + p)
    return [pl.BlockSpec((S, LANE), im(O_CQ)), pl.BlockSpec((S, LANE), im(O_CK)), pl.BlockSpec((S, LANE), im(O_CV))]


def dil_fwd(u, g, side=None):
    S = u.shape[0]
    r = C_GROUPS[g][1]
    nbk = S // r // C_BLK

    def body(q_ref, k_ref, v_ref, o_ref, l_ref):
        rho = pl.program_id(1)

        def step(i, carry):
            rc = _dil_rows(i, rho, r)
            rp = _dil_rows(jnp.maximum(i - 1, 0), rho, r)
            q2, kc2, kp2, vc2, vp2 = q_ref[rc, :], k_ref[rc, :], k_ref[rp, :], v_ref[rc, :], v_ref[rp, :]
            o_p, l_p = [], []
            for hh in range(2):
                hs = slice(hh * HD, (hh + 1) * HD)
                qs = (q2[:, hs] * SCALE).astype(bf16)
                kc, kp = kc2[:, hs].astype(bf16), kp2[:, hs].astype(bf16)
                vc, vp = vc2[:, hs].astype(bf16), vp2[:, hs].astype(bf16)
                s_c, s_p, _, _ = _dil_scores(qs, kc, kp, i, _dil_slope(g, r, hh))
                m = jnp.maximum(jnp.max(s_c, axis=1, keepdims=True), jnp.max(s_p, axis=1, keepdims=True))
                p_c, p_p = jnp.exp(s_c - m), jnp.exp(s_p - m)
                den = jnp.sum(p_c, axis=1, keepdims=True) + jnp.sum(p_p, axis=1, keepdims=True)
                o_p.append((_dot(p_c.astype(bf16), vc, NN) + _dot(p_p.astype(bf16), vp, NN)) / den)
                l_p.append(jnp.broadcast_to(m + jnp.log(den), (C_BLK, HD)))
            o_ref[rc, :] = jnp.concatenate(o_p, axis=1)
            l_ref[rc, :] = jnp.concatenate(l_p, axis=1)
            return carry

        lax.fori_loop(0, nbk, step, 0)

    ospec = pl.BlockSpec((S, LANE), lambda p, rho: (0, p))
    osd = jax.ShapeDtypeStruct((S, 256), f32)
    return _call(
        body, name=f"dil_fwd{g}", grid=(2, r),
        in_specs=_dil_u_specs(g, S), out_specs=[ospec, ospec], out_shape=[osd, osd],
        sem=("parallel", "arbitrary"), args=(u, u, u), side=side)


def dil_merge(os_, ls_):
    S = os_[0].shape[0]

    def body(o0, o1, o2, l0, l1, l2, y_ref, lse_ref):
        a, b, c = l0[...], l1[...], l2[...]
        m = jnp.maximum(jnp.maximum(a, b), c)
        ea, eb, ec = jnp.exp(a - m), jnp.exp(b - m), jnp.exp(c - m)
        den = ea + eb + ec
        y_ref[...] = (ea * o0[...] + eb * o1[...] + ec * o2[...]) / den
        lse_ref[...] = m + jnp.log(den)

    spec = pl.BlockSpec((512, 256), lambda i: (i, 0))
    osd = jax.ShapeDtypeStruct((S, 256), f32)
    return pl.pallas_call(
        body, name="dil_merge", grid=(S // 512,), in_specs=[spec] * 6, out_specs=[spec, spec],
        out_shape=[osd, osd], compiler_params=_cp(("parallel",)),
    )(*os_, *ls_)


def dil_bwd(u, g, dyc, yc, lse):
    S = u.shape[0]
    r = C_GROUPS[g][1]
    nbk = S // r // C_BLK

    def body(q_ref, k_ref, v_ref, dy_ref, y_ref, l_ref, dq_ref, dk_ref, dv_ref):
        rho = pl.program_id(1)

        @pl.when(rho == 0)
        def _():
            dk_ref[...] = jnp.zeros_like(dk_ref)
            dv_ref[...] = jnp.zeros_like(dv_ref)

        def step(i, carry):
            rc = _dil_rows(i, rho, r)
            rp = _dil_rows(jnp.maximum(i - 1, 0), rho, r)
            q2, kc2, kp2, vc2, vp2 = q_ref[rc, :], k_ref[rc, :], k_ref[rp, :], v_ref[rc, :], v_ref[rp, :]
            dy2, y2, l2 = dy_ref[rc, :], y_ref[rc, :], l_ref[rc, :]
            dq_p, dkc_p, dkp_p, dvc_p, dvp_p = [], [], [], [], []
            for hh in range(2):
                hs = slice(hh * HD, (hh + 1) * HD)
                qs = (q2[:, hs] * SCALE).astype(bf16)
                kc, kp = kc2[:, hs].astype(bf16), kp2[:, hs].astype(bf16)
                vc, vp = vc2[:, hs].astype(bf16), vp2[:, hs].astype(bf16)
                dy = dy2[:, hs]
                dyb = dy.astype(bf16)
                s_c, s_p, ok_c, ok_p = _dil_scores(qs, kc, kp, i, _dil_slope(g, r, hh))
                lrow = l2[:, hh * HD:hh * HD + 1]
                delta = jnp.sum(dy * y2[:, hs], axis=1, keepdims=True)
                pi_c = jnp.where(ok_c, jnp.exp(s_c - lrow), 0.0)
                pi_p = jnp.where(ok_p, jnp.exp(s_p - lrow), 0.0)
                ds_c = (pi_c * (_dot(dyb, vc, NT) - delta)).astype(bf16)
                ds_p = (pi_p * (_dot(dyb, vp, NT) - delta)).astype(bf16)
                dq_p.append((_dot(ds_c, kc, NN) + _dot(ds_p, kp, NN)) * SCALE)
                dkc_p.append(_dot(ds_c, qs, TN))
                dkp_p.append(_dot(ds_p, qs, TN))
                dvc_p.append(_dot(pi_c.astype(bf16), dyb, TN))
                dvp_p.append(_dot(pi_p.astype(bf16), dyb, TN))
            dq_ref[rc, :] = jnp.concatenate(dq_p, axis=1)
            dk_ref[rc, :] += jnp.concatenate(dkc_p, axis=1)
            dv_ref[rc, :] += jnp.concatenate(dvc_p, axis=1)
            dk_ref[rp, :] += jnp.concatenate(dkp_p, axis=1)
            dv_ref[rp, :] += jnp.concatenate(dvp_p, axis=1)
            return carry

        lax.fori_loop(0, nbk, step, 0)

    ospec = pl.BlockSpec((S, LANE), lambda p, rho: (0, p))
    osd = jax.ShapeDtypeStruct((S, 256), f32)
    return pl.pallas_call(
        body, name=f"dil_bwd{g}", grid=(2, r),
        in_specs=_dil_u_specs(g, S) + [ospec, ospec, ospec], out_specs=[ospec] * 3, out_shape=[osd] * 3,
        compiler_params=_cp(("parallel", "arbitrary")),
    )(u, u, u, dyc, yc, lse)


def _rows_tile(rows):
    return _tile(rows, (256, 176, 128, 64, 32, 16, 8))


def cast_bf16(w):
    shape = w.shape
    w2 = w.reshape(-1, shape[-1])
    rows, cols = w2.shape
    tr = _rows_tile(rows)

    def body(x_ref, o_ref):
        o_ref[...] = x_ref[...].astype(bf16)

    spec = pl.BlockSpec((tr, cols), lambda i: (i, 0))
    out = pl.pallas_call(
        body, name="cast_bf16", grid=(rows // tr,), in_specs=[spec], out_specs=spec,
        out_shape=jax.ShapeDtypeStruct((rows, cols), bf16), compiler_params=_cp(("parallel",)),
    )(w2)
    return out.reshape(shape)


BC1 = 1.0 - ADAM_B1 ** ADAM_STEP
BC2 = 1.0 - ADAM_B2 ** ADAM_STEP


def _adam_math(w, g, m, v):
    m2 = ADAM_B1 * m + (1.0 - ADAM_B1) * g
    v2 = ADAM_B2 * v + (1.0 - ADAM_B2) * (g * g)
    delta = -ADAM_LR * ((m2 / BC1) / (jnp.sqrt(v2 / BC2) + ADAM_EPS) + ADAM_WD * w)
    return delta, m2, v2


def adam(w, g, m, v):
    shape = w.shape
    r2 = lambda t: t.reshape(-1, shape[-1])
    rows, cols = r2(w).shape
    tr = _rows_tile(rows)

    def body(w_ref, g_ref, m_ref, v_ref, d_ref, m2_ref, v2_ref):
        d_ref[...], m2_ref[...], v2_ref[...] = _adam_math(w_ref[...], g_ref[...], m_ref[...], v_ref[...])

    spec = pl.BlockSpec((tr, cols), lambda i: (i, 0))
    osd = jax.ShapeDtypeStruct((rows, cols), f32)
    outs = pl.pallas_call(
        body, name="adam", grid=(rows // tr,), in_specs=[spec] * 4, out_specs=[spec] * 3, out_shape=[osd] * 3,
        compiler_params=_cp(("parallel",)),
    )(r2(w), r2(g), r2(m), r2(v))
    return [o.reshape(shape) for o in outs]


ADA_N = 9 * D // 4
ADA_TN = 384


def ada_fwd(c_all, w_ada):
    def body(c_ref, w_ref, o_ref):
        cv = c_ref[...]
        o_ref[0] = _dot((cv * _sig(cv)).astype(bf16), w_ref[0].astype(bf16), NN)

    return pl.pallas_call(
        body, name="ada_fwd", grid=(DEPTH, ADA_N // ADA_TN),
        in_specs=[pl.BlockSpec((8, D), lambda l, j: (0, 0)), pl.BlockSpec((1, D, ADA_TN), lambda l, j: (l, 0, j))],
        out_specs=pl.BlockSpec((1, 8, ADA_TN), lambda l, j: (l, 0, j)),
        out_shape=jax.ShapeDtypeStruct((DEPTH, 8, ADA_N), f32), compiler_params=_cp(("parallel", "parallel")),
    )(c_all, w_ada)


def ada_bwd_adam(c_all, dm, w, m, v):
    tr = 128

    def body(c_ref, dm_ref, w_ref, m_ref, v_ref, g_ref, d_ref, m2_ref, v2_ref):
        cv = c_ref[...]
        g = _dot((cv * _sig(cv)).astype(bf16), dm_ref[0].astype(bf16), TN)
        g_ref[0] = g
        d_ref[0], m2_ref[0], v2_ref[0] = _adam_math(w_ref[0], g, m_ref[0], v_ref[0])

    wspec = pl.BlockSpec((1, tr, ADA_N), lambda l, i: (l, i, 0))
    osd = jax.ShapeDtypeStruct((DEPTH, D, ADA_N), f32)
    return pl.pallas_call(
        body, name="ada_bwd_adam", grid=(DEPTH, D // tr),
        in_specs=[pl.BlockSpec((8, tr), lambda l, i: (0, i)), pl.BlockSpec((1, 8, ADA_N), lambda l, i: (l, 0, 0)),
                  wspec, wspec, wspec],
        out_specs=[wspec] * 4, out_shape=[osd] * 4, compiler_params=_cp(("parallel", "parallel")),
    )(c_all, dm, w, m, v)


def _lb_probs(x):
    mx = jnp.max(x, axis=0, keepdims=True)
    e = jnp.exp(x - mx)
    return e / jnp.sum(e, axis=0, keepdims=True)


def lb_fwd(logits):
    def body(x_ref, o_ref):
        p = _lb_probs(x_ref[...])
        rows = [jnp.zeros((1, 768), f32)]
        for l in range(1, DEPTH):
            rows.append(rows[-1] + p[l:l + 1, :])
        o_ref[...] = jnp.concatenate(rows, axis=0)

    return pl.pallas_call(body, name="lb_fwd", out_shape=jax.ShapeDtypeStruct((DEPTH, 768), f32))(logits)


def lb_bwd(logits, dlb):
    def body(x_ref, d_ref, o_ref):
        p = _lb_probs(x_ref[...])
        d = d_ref[...]
        rows = [jnp.zeros((1, 768), f32)] * DEPTH
        acc = jnp.zeros((1, 768), f32)
        for l in range(DEPTH - 1, 0, -1):
            acc = acc + d[l:l + 1, :]
            rows[l] = acc
        dp = jnp.concatenate(rows, axis=0)
        o_ref[...] = p * (dp - jnp.sum(p * dp, axis=0, keepdims=True))

    return pl.pallas_call(body, name="lb_bwd", out_shape=jax.ShapeDtypeStruct((DEPTH, 768), f32))(logits, dlb)


def sum_slots(x):
    n, rows, cols = x.shape
    tr = _rows_tile(rows)

    def body(x_ref, o_ref):
        acc = x_ref[0]
        for j in range(1, n):
            acc = acc + x_ref[j]
        o_ref[...] = acc

    return pl.pallas_call(
        body, name="sum_slots", grid=(rows // tr,),
        in_specs=[pl.BlockSpec((n, tr, cols), lambda i: (0, i, 0))], out_specs=pl.BlockSpec((tr, cols), lambda i: (i, 0)),
        out_shape=jax.ShapeDtypeStruct((rows, cols), f32), compiler_params=_cp(("parallel",)),
    )(x)


ANY = pl.BlockSpec(memory_space=pl.ANY)
CHIP_FLIPS = ((1, 0), (0, 1), (1, 1))
DEV_FLIPS = tuple((a, b, d) for a in (0, 1) for b in (0, 1) for d in (0, 1))[1:]


def _me():
    return lax.axis_index("x"), lax.axis_index("y"), lax.axis_index("c")


def _flip(v, f):
    return 1 - v if f else v


def _comm_call(body, name, ins, out_shapes, n_remote, n_local):
    return pl.pallas_call(
        body, name=name, in_specs=[ANY] * len(ins), out_specs=[ANY] * len(out_shapes), out_shape=out_shapes,
        scratch_shapes=[pltpu.SemaphoreType.DMA((n_remote,)), pltpu.SemaphoreType.DMA((n_remote,)),
                        pltpu.SemaphoreType.DMA((max(n_local, 1),))],
    )(*ins)


def run_plan(plan, name):
    ni, no = len(plan.ins), len(plan.outs)

    def body(*refs):
        ins, outs, sems = refs[:ni], refs[ni:ni + no], refs[ni + no:]
        plan.start(ins, outs, *sems)
        plan.wait(ins, outs, *sems)

    return pl.pallas_call(body, name=name, in_specs=[ANY] * ni, out_specs=[ANY] * no, out_shape=list(plan.outs),
                          scratch_shapes=plan.sems())(*plan.ins)


def gather_chips_plan(arrs, layer=None):
    n = len(arrs)
    shapes = [a.shape if layer is None else a.shape[1:] for a in arrs]

    def copies(ins, outs, send, recv, loc):
        x, y, c = _me()
        mine = 2 * x + y
        srcs = [r if layer is None else r.at[layer] for r in ins]
        locs = [pltpu.make_async_copy(srcs[a], outs[a].at[mine], loc.at[a]) for a in range(n)]

        def remote(a, k, slot):
            fx, fy = CHIP_FLIPS[k]
            return pltpu.make_async_remote_copy(srcs[a], outs[a].at[slot], send.at[3 * a + k], recv.at[3 * a + k],
                                                device_id=(_flip(x, fx), _flip(y, fy), c), device_id_type=MESH)

        peers = [2 * _flip(x, fx) + _flip(y, fy) for fx, fy in CHIP_FLIPS]
        return locs, remote, mine, peers

    def start(ins, outs, send, recv, loc):
        locs, remote, mine, _ = copies(ins, outs, send, recv, loc)
        for cp in locs:
            cp.start()
        for a in range(n):
            for k in range(3):
                remote(a, k, mine).start()

    def wait(ins, outs, send, recv, loc):
        locs, remote, _, peers = copies(ins, outs, send, recv, loc)
        for a in range(n):
            for k in range(3):
                cp = remote(a, k, peers[k])
                cp.wait_recv()
                cp.wait_send()
        for cp in locs:
            cp.wait()

    outs = [jax.ShapeDtypeStruct((4,) + tuple(s), a.dtype) for s, a in zip(shapes, arrs)]
    return Plan(list(arrs), outs, 3 * n, n, start, wait)


def all_gather_chips(arrs, layer=None, name="ag4"):
    return run_plan(gather_chips_plan(arrs, layer), name)


def all_gather_devs(arr, name="ag8"):
    def body(in_ref, out_ref, send, recv, loc):
        x, y, c = _me()
        mine = 4 * x + 2 * y + c
        lc = pltpu.make_async_copy(in_ref, out_ref.at[mine], loc.at[0])
        lc.start()

        def remote(k, slot):
            fx, fy, fc = DEV_FLIPS[k]
            return pltpu.make_async_remote_copy(in_ref, out_ref.at[slot], send.at[k], recv.at[k],
                                                device_id=(_flip(x, fx), _flip(y, fy), _flip(c, fc)), device_id_type=MESH)

        for k in range(7):
            remote(k, mine).start()
        for k, (fx, fy, fc) in enumerate(DEV_FLIPS):
            cp = remote(k, 4 * _flip(x, fx) + 2 * _flip(y, fy) + _flip(c, fc))
            cp.wait_recv()
            cp.wait_send()
        lc.wait()

    return _comm_call(body, name, [arr], [jax.ShapeDtypeStruct((8,) + arr.shape, arr.dtype)], 7, 1)[0]


def _rows_of(which, rows):
    return pl.ds(pl.multiple_of(which * rows, 16), rows)


def dev_exchange_plan(parts):
    n = len(parts)

    def copies(ins, outs, send, recv, loc):
        x, y, c = _me()
        mine = 4 * x + 2 * y + c

        def piece(a, px, py, pc):
            rows = ins[a].shape[1] // 2
            return ins[a].at[2 * px + py, _rows_of(pc, rows), :]

        locs = [pltpu.make_async_copy(piece(a, x, y, c), outs[a].at[mine], loc.at[a]) for a in range(n)]

        def remote(a, k, slot):
            fx, fy, fc = DEV_FLIPS[k]
            px, py, pc = _flip(x, fx), _flip(y, fy), _flip(c, fc)
            return pltpu.make_async_remote_copy(piece(a, px, py, pc), outs[a].at[slot], send.at[7 * a + k], recv.at[7 * a + k],
                                                device_id=(px, py, pc), device_id_type=MESH)

        peers = [4 * _flip(x, fx) + 2 * _flip(y, fy) + _flip(c, fc) for fx, fy, fc in DEV_FLIPS]
        return locs, remote, mine, peers

    def start(ins, outs, send, recv, loc):
        locs, remote, mine, _ = copies(ins, outs, send, recv, loc)
        for cp in locs:
            cp.start()
        for a in range(n):
            for k in range(7):
                remote(a, k, mine).start()

    def wait(ins, outs, send, recv, loc):
        locs, remote, _, peers = copies(ins, outs, send, recv, loc)
        for a in range(n):
            for k in range(7):
                cp = remote(a, k, peers[k])
                cp.wait_recv()
                cp.wait_send()
        for cp in locs:
            cp.wait()

    outs = [jax.ShapeDtypeStruct((8, p.shape[1] // 2, p.shape[2]), p.dtype) for p in parts]
    return Plan(list(parts), outs, 7 * n, n, start, wait)


def sum_share(slots, name="rs_sum"):
    n, r, cols = slots.shape
    tr = _tile(r, (128, 176, 64))
    steps = r // tr

    def body(s_ref, g_ref, buf, send, loc, recv):
        i = pl.program_id(0)
        x, y, c = _me()
        slot = i % 2

        def copies(step, sl):
            rows = pl.ds(pl.multiple_of(c * r + step * tr, 8), tr)
            rem = pltpu.make_async_remote_copy(buf.at[sl], g_ref.at[rows, :], send.at[sl], recv.at[0],
                                               device_id=(x, y, 1 - c), device_id_type=MESH)
            return rem, pltpu.make_async_copy(buf.at[sl], g_ref.at[rows, :], loc.at[sl])

        @pl.when(i >= 2)
        def _():
            rem, lc = copies(i - 2, slot)
            rem.wait_send()
            lc.wait()

        acc = s_ref[0].astype(f32)
        for j in range(1, n):
            acc = acc + s_ref[j].astype(f32)
        buf[slot] = acc
        rem, lc = copies(i, slot)
        rem.start()
        lc.start()

        @pl.when(i == steps - 1)
        def _():
            for back in range(min(2, steps)):
                rem, lc = copies(i - back, (i - back) % 2)
                rem.wait_send()
                lc.wait()
            other = g_ref.at[pl.ds(pl.multiple_of((1 - c) * r, 8), r), :]
            pltpu.make_async_remote_copy(other, other, send.at[0], recv.at[0],
                                         device_id=(x, y, 1 - c), device_id_type=MESH).wait_recv()

    return pl.pallas_call(
        body, name=name, grid=(steps,),
        in_specs=[pl.BlockSpec((n, tr, cols), lambda i: (0, i, 0))], out_specs=ANY,
        out_shape=jax.ShapeDtypeStruct((2 * r, cols), f32),
        scratch_shapes=[pltpu.VMEM((2, tr, cols), f32), pltpu.SemaphoreType.DMA((2,)), pltpu.SemaphoreType.DMA((2,)),
                        pltpu.SemaphoreType.DMA((1,))],
        compiler_params=_cp(("arbitrary",)),
    )(slots)


BIG = ("ffn1_w_in", "ffn1_w_out", "w_in", "w_branch_a", "w_branch_b", "w_branch_c", "w_out", "ffn2_w_in", "ffn2_w_out")
ROW_SHARDED = ("ffn1_w_out", "w_out", "ffn2_w_out")
RES_W = (0.5, 1.0, 0.5)


def _full_weight(name, g):
    if name in ROW_SHARDED:
        return g.reshape(4 * g.shape[1], g.shape[2])
    return jnp.concatenate([g[0], g[1], g[2], g[3]], axis=1)


def _by_shard(name, dw):
    if name in ROW_SHARDED:
        return dw.reshape(4, dw.shape[0] // 4, dw.shape[1])
    return dw.reshape(dw.shape[0], 4, dw.shape[1] // 4).transpose(1, 0, 2)


def _full_weight_t(name, g):
    if name in ROW_SHARDED:
        return g.reshape(4 * g.shape[1], g.shape[2]).T
    return g.transpose(0, 2, 1).reshape(4 * g.shape[2], g.shape[1])


def _ffn_fwd(x, w_in, w_out, a_vec, sh_vec, b_vec):
    h = prenorm(x, a_vec, sh_vec)
    ua, ub, s = ffn_in_swiglu(h, w_in)
    y = mm(s, w_out, name="ffn_out")
    return postnorm(x, y, b_vec), (x, h, ua, ub, s, y)


def _ffn_bwd(dout, saved, w_in_t, w_out_t, a_vec, b_vec):
    x, h, ua, ub, s, y = saved
    dy, db = post_bwd(dout, y, b_vec)
    dw_out = mm(s.T, dy, out_dtype=bf16, name="ffn_dwo")
    du = ffn_du(dy, w_out_t, ua, ub)
    dh = mm(du, w_in_t, name="ffn_dh")
    dw_in = mm(h.T, du, out_dtype=bf16, name="ffn_dwi")
    dx, dsh, da = pre_bwd(dout, dh, x, a_vec)
    return dx, dw_in, dw_out, dsh, da, db


def _mix_fwd(x, w, lb, ng, a_vec, sh_vec, b_vec, plans=(None,) * 6):
    h = prenorm(x, a_vec, sh_vec)
    u = mm(h, w["w_in"], name="mix_in", side=plans[0])
    side0 = None
    if plans[0] is not None:
        u, side0 = u
    (o, ya, st), side1 = hgrn_fwd(u, lb, ng, side=plans[1])
    yb, side2 = sb_fwd(u, side=plans[2])
    groups, dil_sides = [], []
    for g in range(3):
        og, sg = dil_fwd(u, g, side=plans[3 + g])
        groups.append(og)
        dil_sides.append(sg)
    yc, lse = dil_merge([o_ for o_, _ in groups], [l_ for _, l_ in groups])
    pa = mm(ya, w["w_branch_a"], name="mix_pa")
    pb = mm(yb, w["w_branch_b"], name="mix_pb")
    pc = mm(yc, w["w_branch_c"], name="mix_pc")
    merged = gate_merge(u, pa, pb, pc)
    z = mm(merged, w["w_out"], name="mix_out")
    return (postnorm(x, z, b_vec), (x, h, u, o, ya, st, yb, yc, lse, pa, pb, pc, merged, z),
            (side0, side1, side2, *dil_sides))


def _mix_bwd(dout, saved, wt, lb, ng, a_vec, b_vec, plans=(None,) * 3):
    x, h, u, o, ya, st, yb, yc, lse, pa, pb, pc, merged, z = saved
    dz, db = post_bwd(dout, z, b_vec)
    dmerged = mm(dz, wt["w_out"], name="mix_dm")
    dw_out = mm(merged.T, dz, out_dtype=bf16, name="mix_dwo")
    dpa, dpb, dpc, dg0, dg1, dg2 = gate_bwd(dmerged, u, pa, pb, pc)
    dya = mm(dpa, wt["w_branch_a"], name="mix_dya")
    dyb = mm(dpb, wt["w_branch_b"], name="mix_dyb")
    dyc = mm(dpc, wt["w_branch_c"], name="mix_dyc")
    dw_a = mm(ya.T, dpa, out_dtype=bf16, name="mix_dwa")
    dw_b = mm(yb.astype(bf16).T, dpb, out_dtype=bf16, name="mix_dwb")
    dw_c = mm(yc.astype(bf16).T, dpc, out_dtype=bf16, name="mix_dwc")
    (daq, daf, dai, dag, dlb, dng), side0 = hgrn_bwd(u, lb, ng, o, st, dya, side=plans[0])
    (dbq, dbk, dbv), side1 = sb_bwd(u, yb, dyb, side=plans[1])
    dc = [dil_bwd(u, g, dyc, yc, lse) for g in range(3)]
    du = jnp.concatenate(
        [daq, daf, dai, dag] + [t.astype(bf16) for t in (dbq, dbk, dbv)]
        + [dc[g][j].astype(bf16) for j in range(3) for g in range(3)] + [dg0, dg1, dg2], axis=1)
    dh = mm(du, wt["w_in"], name="mix_dh")
    dw_in = mm(h.T, du, out_dtype=bf16, name="mix_dwi", side=plans[2])
    side2 = None
    if plans[2] is not None:
        dw_in, side2 = dw_in
    dx, dsh, da = pre_bwd(dout, dh, x, a_vec)
    grads = {"w_in": dw_in, "w_out": dw_out, "w_branch_a": dw_a, "w_branch_b": dw_b, "w_branch_c": dw_c}
    return dx, grads, dlb, jnp.sum(dng, axis=0), dsh, da, db, (side0, side1, side2)


FWD_RIDERS = (("ffn1_w_in",), ("ffn2_w_in",), ("w_in",), ("ffn1_w_out", "w_out"), ("ffn2_w_out", "w_branch_a"),
              ("w_branch_b", "w_branch_c"))
BWD_RIDERS = (("ffn1_w_in", "ffn2_w_out", "w_branch_a", "w_branch_b", "w_branch_c"), ("w_in", "ffn1_w_out", "w_out"),
              ("ffn2_w_in",))


def _reduce_to_shards(names, grads):
    slots = run_plan(dev_exchange_plan([_by_shard(n, grads[n]) for n in names]), "rs_x8")
    return {n: sum_share(s) for n, s in zip(names, slots)}


def kernel(x, c, w_ada, b_ada, norm_g, ffn1_w_in, ffn1_w_out, w_in, hgrn_lb_logits, hgrn_norm_g, w_branch_a, w_branch_b, w_branch_c, w_out, ffn2_w_in, ffn2_w_out, loss_target, m_w_ada, m_b_ada, m_norm_g, m_ffn1_w_in, m_ffn1_w_out, m_w_in, m_hgrn_lb_logits, m_hgrn_norm_g, m_w_branch_a, m_w_branch_b, m_w_branch_c, m_w_out, m_ffn2_w_in, m_ffn2_w_out, v_w_ada, v_b_ada, v_norm_g, v_ffn1_w_in, v_ffn1_w_out, v_w_in, v_hgrn_lb_logits, v_hgrn_norm_g, v_w_branch_a, v_w_branch_b, v_w_branch_c, v_w_out, v_ffn2_w_in, v_ffn2_w_out):
    weights = dict(w_ada=w_ada, b_ada=b_ada, norm_g=norm_g, ffn1_w_in=ffn1_w_in, ffn1_w_out=ffn1_w_out, w_in=w_in,
                   hgrn_lb_logits=hgrn_lb_logits, hgrn_norm_g=hgrn_norm_g, w_branch_a=w_branch_a, w_branch_b=w_branch_b,
                   w_branch_c=w_branch_c, w_out=w_out, ffn2_w_in=ffn2_w_in, ffn2_w_out=ffn2_w_out)
    mom = dict(w_ada=m_w_ada, b_ada=m_b_ada, norm_g=m_norm_g, ffn1_w_in=m_ffn1_w_in, ffn1_w_out=m_ffn1_w_out, w_in=m_w_in,
               hgrn_lb_logits=m_hgrn_lb_logits, hgrn_norm_g=m_hgrn_norm_g, w_branch_a=m_w_branch_a, w_branch_b=m_w_branch_b,
               w_branch_c=m_w_branch_c, w_out=m_w_out, ffn2_w_in=m_ffn2_w_in, ffn2_w_out=m_ffn2_w_out)
    var = dict(w_ada=v_w_ada, b_ada=v_b_ada, norm_g=v_norm_g, ffn1_w_in=v_ffn1_w_in, ffn1_w_out=v_ffn1_w_out, w_in=v_w_in,
               hgrn_lb_logits=v_hgrn_lb_logits, hgrn_norm_g=v_hgrn_norm_g, w_branch_a=v_w_branch_a, w_branch_b=v_w_branch_b,
               w_branch_c=v_w_branch_c, w_out=v_w_out, ffn2_w_in=v_ffn2_w_in, ffn2_w_out=v_ffn2_w_out)
    order = list(weights)
    xi, yi, ci = _me()
    chip = 2 * xi + yi
    dev = 4 * xi + 2 * yi + ci
    xs = x[0]

    c_all = all_gather_devs(c, name="ag8_c").reshape(8, D)
    mod_sh = all_gather_chips([ada_fwd(c_all, w_ada)], name="ag4_mod")[0]
    mod_all = mod_sh.transpose(1, 2, 0, 3).reshape(DEPTH, 8, 9 * D)
    mod = lax.dynamic_index_in_dim(mod_all, dev, axis=1, keepdims=False) + b_ada
    mod = mod.reshape(DEPTH, 3, 3, D)
    ng_all = all_gather_chips([norm_g.reshape(DEPTH * 6, D // 4)], name="ag4_norm")[0]
    ng_all = ng_all.reshape(4, DEPTH, 6, D // 4).transpose(1, 2, 0, 3).reshape(DEPTH, 6, D)
    lb_all = lb_fwd(hgrn_lb_logits)
    w16 = {n: cast_bf16(weights[n]) for n in BIG}

    def vecs(l, i):
        shift, scale, gate = mod[l, i, 0][None], mod[l, i, 1][None], mod[l, i, 2][None]
        g_pre, g_post = ng_all[l, 2 * i][None], ng_all[l, 2 * i + 1][None]
        return g_pre * (1.0 + scale), shift, RES_W[i] * gate * g_post

    saved, full = [], []
    gathered = dict(zip(BIG, all_gather_chips([w16[n] for n in BIG], layer=0, name="ag4_w0")))
    for l in range(DEPTH):
        w = {n: _full_weight(n, gathered[n]) for n in BIG}
        full.append({n: _full_weight_t(n, gathered[n]) for n in BIG})
        lb, ng = lb_all[l][None], hgrn_norm_g[l][None]
        plans = (None,) * len(FWD_RIDERS)
        if l + 1 < DEPTH:
            plans = tuple(gather_chips_plan([w16[n] for n in names], layer=l + 1) for names in FWD_RIDERS)
        xs, s1 = _ffn_fwd(xs, w["ffn1_w_in"], w["ffn1_w_out"], *vecs(l, 0))
        xs, s2, sides = _mix_fwd(xs, w, lb, ng, *vecs(l, 1), plans=plans)
        xs, s3 = _ffn_fwd(xs, w["ffn2_w_in"], w["ffn2_w_out"], *vecs(l, 2))
        saved.append((s1, s2, s3))
        if l + 1 < DEPTH:
            gathered = {n: g for names, outs in zip(FWD_RIDERS, sides) for n, g in zip(names, outs)}

    dx, loss_part = loss_grad(xs, loss_target[0])
    loss = lax.psum(loss_part[0, 0], ("x", "y", "c"))

    big_grads = {n: [None] * DEPTH for n in BIG}
    d_mod, d_ng, d_lb, d_hng = [None] * DEPTH, [None] * DEPTH, [None] * DEPTH, [None] * DEPTH
    pending = None
    for l in reversed(range(DEPTH)):
        wt = full[l]
        s1, s2, s3 = saved[l]
        lb, ng = lb_all[l][None], hgrn_norm_g[l][None]
        rows_mod, rows_ng = [None] * 9, [None] * 6

        def vec_grads(i, dsh, da, db):
            scale, gate = mod[l, i, 1][None], mod[l, i, 2][None]
            g_pre, g_post = ng_all[l, 2 * i][None], ng_all[l, 2 * i + 1][None]
            rows_mod[3 * i], rows_mod[3 * i + 1], rows_mod[3 * i + 2] = dsh, g_pre * da, RES_W[i] * g_post * db
            rows_ng[2 * i], rows_ng[2 * i + 1] = (1.0 + scale) * da, RES_W[i] * gate * db

        a3, _, b3 = vecs(l, 2)
        dx, dwi, dwo, dsh, da, db = _ffn_bwd(dx, s3, wt["ffn2_w_in"], wt["ffn2_w_out"], a3, b3)
        vec_grads(2, dsh, da, db)
        grads = {"ffn2_w_in": dwi, "ffn2_w_out": dwo}
        a2, _, b2 = vecs(l, 1)
        plans = (None,) * len(BWD_RIDERS)
        if pending is not None:
            plans = tuple(dev_exchange_plan([pending[n] for n in names]) for names in BWD_RIDERS)
        dx, gmix, dlb, dhng, dsh, da, db, sides = _mix_bwd(dx, s2, wt, lb, ng, a2, b2, plans=plans)
        if pending is not None:
            for names, outs in zip(BWD_RIDERS, sides):
                for n, slots in zip(names, outs):
                    big_grads[n][l + 1] = sum_share(slots)
        vec_grads(1, dsh, da, db)
        grads.update(gmix)
        a1, _, b1 = vecs(l, 0)
        dx, dwi, dwo, dsh, da, db = _ffn_bwd(dx, s1, wt["ffn1_w_in"], wt["ffn1_w_out"], a1, b1)
        vec_grads(0, dsh, da, db)
        grads.update({"ffn1_w_in": dwi, "ffn1_w_out": dwo})
        pending = {n: _by_shard(n, grads[n]) for n in BIG}
        d_mod[l] = jnp.concatenate(rows_mod, axis=1)
        d_ng[l] = jnp.concatenate(rows_ng, axis=0)
        d_lb[l], d_hng[l] = dlb, dhng

    slots = run_plan(dev_exchange_plan([pending[n] for n in BIG]), "rs_x8")
    for n, s in zip(BIG, slots):
        big_grads[n][0] = sum_share(s)

    n_small = 6 * D * DEPTH + 768 * DEPTH + A_V * DEPTH + 9 * D * DEPTH
    pad = -n_small % (512 * LANE)
    flat = jnp.concatenate([jnp.stack(d_ng).reshape(-1), jnp.concatenate(d_lb, axis=0).reshape(-1),
                            jnp.concatenate(d_hng, axis=0).reshape(-1), jnp.concatenate(d_mod, axis=0).reshape(-1),
                            jnp.zeros((pad,), f32)])
    small_all = all_gather_devs(flat.reshape(-1, LANE), name="ag8_small")
    total = sum_slots(small_all).reshape(-1)
    o1 = 6 * D * DEPTH
    o2 = o1 + 768 * DEPTH
    o3 = o2 + A_V * DEPTH
    g_ng_full = total[:o1].reshape(DEPTH, 6, D)
    g_lb_all = total[o1:o2].reshape(DEPTH, 768)
    g_small = {
        "norm_g": lax.dynamic_slice_in_dim(g_ng_full, chip * (D // 4), D // 4, axis=2),
        "hgrn_lb_logits": lb_bwd(hgrn_lb_logits, g_lb_all),
        "hgrn_norm_g": total[o2:o3].reshape(DEPTH, A_V),
        "b_ada": total[o3:n_small].reshape(DEPTH, 9 * D),
    }
    dmod_all = small_all.reshape(8, -1)[:, o3:n_small].reshape(8, DEPTH, 9 * D).transpose(1, 0, 2)
    dm_sh = lax.dynamic_slice_in_dim(dmod_all, chip * ADA_N, ADA_N, axis=2)

    out_g, out_d, out_m, out_v = {}, {}, {}, {}
    out_g["w_ada"], out_d["w_ada"], out_m["w_ada"], out_v["w_ada"] = ada_bwd_adam(c_all, dm_sh, w_ada, m_w_ada, v_w_ada)
    for n in BIG:
        out_g[n] = jnp.stack(big_grads[n])
    out_g.update(g_small)
    for n in order:
        if n != "w_ada":
            out_d[n], out_m[n], out_v[n] = adam(weights[n], out_g[n], mom[n], var[n])
    return (loss, dx[None], *[out_g[n] for n in order], *[out_d[n] for n in order],
            *[out_m[n] for n in order], *[out_v[n] for n in order])
```

```python
import functools
import math

import jax
import jax.numpy as jnp
from jax import lax
from jax.experimental import pallas as pl
from jax.experimental.pallas import tpu as pltpu

f32, bf16 = jnp.float32, jnp.bfloat16

D = 1024
DEPTH = 4
D_FF = 2816
EPS = 1e-6
NEG_BIG = -1e30
TINY = 1e-30
A_HEADS, A_K, A_V, A_CHUNK = 6, 128, 64, 64
A_SUB = 16
A_CLAMP = 80.0
B_HEADS, HD = 6, 64
C_GROUPS = ((128, 1), (512, 4), (2048, 16))
C_BLK = 128
IN_COLS = 8832
O_AQ, O_AF, O_AI, O_AG = 0, 768, 1536, 1920
O_BQ, O_BK, O_BV = 2304, 2688, 3072
O_CQ, O_CK, O_CV = 3456, 4224, 4992
O_GATE = 5760
LANE = 128
ADAM_LR, ADAM_B1, ADAM_B2, ADAM_EPS, ADAM_WD, ADAM_STEP = 0.001, 0.9, 0.999, 1e-08, 0.01, 10
MESH = pl.DeviceIdType.MESH
VMEM_LIMIT = 56 * 1024 * 1024


def _alibi_slopes(n):
    def pow2(m):
        start = 2.0 ** (-8.0 / m)
        return [start ** (i + 1) for i in range(m)]
    if math.log2(n).is_integer():
        s = pow2(n)
    else:
        c = 2 ** int(math.floor(math.log2(n)))
        s = pow2(c) + pow2(2 * c)[0::2][: n - c]
    return sorted(s, reverse=True)


C_SLOPES = _alibi_slopes(12)


def _tile(n, prefs):
    for p in prefs:
        if n % p == 0:
            return p
    return n


def _cp(sem):
    return pltpu.CompilerParams(dimension_semantics=sem, vmem_limit_bytes=VMEM_LIMIT)


def _sig(x):
    return 1.0 / (1.0 + jnp.exp(-x))


def _dot(a, b, dn, precision=None):
    return lax.dot_general(a, b, (dn, ((), ())), preferred_element_type=f32, precision=precision)


NN = ((1,), (0,))
NT = ((1,), (1,))
TN = ((0,), (0,))


class Plan:
    def __init__(self, ins, outs, n_remote, n_local, start, wait):
        self.ins, self.outs, self.n_remote, self.n_local, self.start, self.wait = ins, outs, n_remote, n_local, start, wait

    def sems(self):
        return [pltpu.SemaphoreType.DMA((self.n_remote,)), pltpu.SemaphoreType.DMA((self.n_remote,)),
                pltpu.SemaphoreType.DMA((max(self.n_local, 1),))]


def _call(body, *, name, grid, in_specs, out_specs, out_shape, sem, args, scratch_shapes=(), side=None):
    if side is None:
        return pl.pallas_call(body, name=name, grid=grid, in_specs=in_specs, out_specs=out_specs, out_shape=out_shape,
                              scratch_shapes=list(scratch_shapes), compiler_params=_cp(sem))(*args), None
    any_spec = pl.BlockSpec(memory_space=pl.ANY)
    n_in, n_out, n_scr = len(in_specs), len(out_specs), len(scratch_shapes)
    s_in, s_out = len(side.ins), len(side.outs)

    def hosted(*refs):
        ins, rest = refs[:n_in], refs[n_in:]
        sins, rest = rest[:s_in], rest[s_in:]
        outs, rest = rest[:n_out], rest[n_out:]
        souts, rest = rest[:s_out], rest[s_out:]
        scr, sems = rest[:n_scr], rest[n_scr:]
        pids = [pl.program_id(d) for d in range(len(grid))]
        first = functools.reduce(jnp.logical_and, [p == 0 for p in pids])
        last = functools.reduce(jnp.logical_and, [p == g - 1 for p, g in zip(pids, grid)])

        @pl.when(first)
        def _():
            side.start(sins, souts, *sems)

        body(*ins, *outs, *scr)

        @pl.when(last)
        def _():
            side.wait(sins, souts, *sems)

    res = pl.pallas_call(
        hosted, name=name, grid=grid, in_specs=list(in_specs) + [any_spec] * s_in,
        out_specs=list(out_specs) + [any_spec] * s_out, out_shape=list(out_shape) + list(side.outs),
        scratch_shapes=list(scratch_shapes) + side.sems(), compiler_params=_cp(("arbitrary",) * len(grid)),
    )(*args, *side.ins)
    return res[:n_out], res[n_out:]


MM_TILES = {
    (4096, 1024, 2816): (1024, 512, 2816),
    (4096, 1024, 5632): (512, 512, 5632),
    (1024, 5632, 4096): (512, 512, 4096),
    (2816, 1024, 4096): (704, 512, 4096),
    (4096, 8832, 1024): (512, 2944, 1024),
    (4096, 1024, 8832): (1024, 512, 2944),
    (1024, 8832, 4096): (512, 2944, 1024),
    (1024, 1024, 4096): (512, 512, 4096),
    (384, 1024, 4096): (384, 512, 4096),
    (256, 1024, 4096): (256, 512, 4096),
}


def mm(a, b, *, out_dtype=f32, name="mm", side=None):
    M, K = a.shape
    K2, N = b.shape
    assert K == K2, (a.shape, b.shape)
    tm, tn, tk = MM_TILES.get((M, N, K), (_tile(M, (1024, 704, 512, 384, 256, 128)), _tile(N, (512, 384, 256, 128)),
                                          _tile(K, (1024, 512, 1408, 384, 256, 128))))
    nk = K // tk

    def body(a_ref, b_ref, o_ref, *acc):
        p = _dot(a_ref[...].astype(bf16), b_ref[...].astype(bf16), NN)
        if nk == 1:
            o_ref[...] = p.astype(out_dtype)
            return
        acc_ref, = acc
        k = pl.program_id(2)

        @pl.when(k == 0)
        def _():
            acc_ref[...] = p

        @pl.when(k > 0)
        def _():
            acc_ref[...] += p

        @pl.when(k == nk - 1)
        def _():
            o_ref[...] = acc_ref[...].astype(out_dtype)

    outs, souts = _call(
        body, name=name, grid=(M // tm, N // tn, nk),
        in_specs=[pl.BlockSpec((tm, tk), lambda i, j, k: (i, k)), pl.BlockSpec((tk, tn), lambda i, j, k: (k, j))],
        out_specs=[pl.BlockSpec((tm, tn), lambda i, j, k: (i, j))],
        out_shape=[jax.ShapeDtypeStruct((M, N), out_dtype)],
        scratch_shapes=[pltpu.VMEM((tm, tn), f32)] if nk > 1 else [],
        sem=("parallel", "parallel", "arbitrary"), args=(a, b), side=side)
    return outs[0] if side is None else (outs[0], souts)


FF_TM = 1024
FF_T = 256
FF_NB = D_FF // FF_T


def ffn_in_swiglu(h, w_in):
    S = h.shape[0]

    def body(h_ref, wa_ref, wb_ref, a_ref, b_ref, s_ref):
        hv = h_ref[...]
        a = _dot(hv, wa_ref[...], NN)
        b = _dot(hv, wb_ref[...], NN)
        a_ref[...] = a.astype(bf16)
        b_ref[...] = b.astype(bf16)
        s_ref[...] = (a * _sig(a) * b).astype(bf16)

    ospec = pl.BlockSpec((FF_TM, FF_T), lambda i, j: (i, j))
    osd = jax.ShapeDtypeStruct((S, D_FF), bf16)
    return pl.pallas_call(
        body, name="ffn_in", grid=(S // FF_TM, FF_NB),
        in_specs=[pl.BlockSpec((FF_TM, D), lambda i, j: (i, 0)), pl.BlockSpec((D, FF_T), lambda i, j: (0, j)),
                  pl.BlockSpec((D, FF_T), lambda i, j: (0, j + FF_NB))],
        out_specs=[ospec] * 3, out_shape=[osd] * 3, compiler_params=_cp(("parallel", "parallel")),
    )(h, w_in, w_in)


def ffn_du(dy, w_out_t, ua, ub):
    S = dy.shape[0]
    tm = 512

    def body(dy_ref, w_ref, a_ref, b_ref, du_ref):
        dyv = dy_ref[...]
        for j in range(FF_NB):
            cols = slice(j * FF_T, (j + 1) * FF_T)
            ds = _dot(dyv, w_ref[:, cols], NN)
            a, b = a_ref[:, cols].astype(f32), b_ref[:, cols].astype(f32)
            sg = _sig(a)
            du_ref[:, cols] = (ds * b * sg * (1.0 + a * (1.0 - sg))).astype(bf16)
            du_ref[:, D_FF + j * FF_T:D_FF + (j + 1) * FF_T] = (ds * a * sg).astype(bf16)

    half = pl.BlockSpec((tm, D_FF), lambda i: (i, 0))
    return pl.pallas_call(
        body, name="ffn_du", grid=(S // tm,),
        in_specs=[pl.BlockSpec((tm, D), lambda i: (i, 0)), pl.BlockSpec((D, D_FF), lambda i: (0, 0)), half, half],
        out_specs=pl.BlockSpec((tm, 2 * D_FF), lambda i: (i, 0)),
        out_shape=jax.ShapeDtypeStruct((S, 2 * D_FF), bf16), compiler_params=_cp(("parallel",)),
    )(dy, w_out_t, ua, ub)


TR = 512


def _row_spec(cols=D):
    return pl.BlockSpec((TR, cols), lambda i: (i, 0))


def _vec_spec(cols=D):
    return pl.BlockSpec((1, cols), lambda i: (0, 0))


def prenorm(x, a_vec, sh_vec):
    S = x.shape[0]

    def body(x_ref, a_ref, s_ref, h_ref):
        xv = x_ref[...]
        rstd = lax.rsqrt(jnp.mean(xv * xv, axis=1, keepdims=True) + EPS)
        h_ref[...] = (xv * rstd * a_ref[...] + s_ref[...]).astype(bf16)

    return pl.pallas_call(
        body, name="prenorm", grid=(S // TR,),
        in_specs=[_row_spec(), _vec_spec(), _vec_spec()], out_specs=_row_spec(),
        out_shape=jax.ShapeDtypeStruct((S, D), bf16), compiler_params=_cp(("parallel",)),
    )(x, a_vec, sh_vec)


def postnorm(x, y, b_vec):
    S = x.shape[0]

    def body(x_ref, y_ref, b_ref, o_ref):
        yv = y_ref[...]
        rstd = lax.rsqrt(jnp.mean(yv * yv, axis=1, keepdims=True) + EPS)
        o_ref[...] = x_ref[...] + b_ref[...] * (yv * rstd)

    return pl.pallas_call(
        body, name="postnorm", grid=(S // TR,),
        in_specs=[_row_spec(), _row_spec(), _vec_spec()], out_specs=_row_spec(),
        out_shape=jax.ShapeDtypeStruct((S, D), f32), compiler_params=_cp(("parallel",)),
    )(x, y, b_vec)


def post_bwd(dout, y, b_vec):
    S = dout.shape[0]

    def body(d_ref, y_ref, b_ref, dy_ref, db_ref):
        i = pl.program_id(0)
        yv, dv = y_ref[...], d_ref[...]
        rstd = lax.rsqrt(jnp.mean(yv * yv, axis=1, keepdims=True) + EPS)
        yh = yv * rstd
        dyh = dv * b_ref[...]
        dy_ref[...] = (rstd * (dyh - yh * jnp.mean(dyh * yh, axis=1, keepdims=True))).astype(bf16)
        part = jnp.sum(dv * yh, axis=0, keepdims=True)

        @pl.when(i == 0)
        def _():
            db_ref[...] = part

        @pl.when(i > 0)
        def _():
            db_ref[...] += part

    return pl.pallas_call(
        body, name="post_bwd", grid=(S // TR,),
        in_specs=[_row_spec(), _row_spec(), _vec_spec()], out_specs=[_row_spec(), _vec_spec()],
        out_shape=[jax.ShapeDtypeStruct((S, D), bf16), jax.ShapeDtypeStruct((1, D), f32)],
        compiler_params=_cp(("arbitrary",)),
    )(dout, y, b_vec)


def pre_bwd(dout, dh, x, a_vec):
    S = dout.shape[0]

    def body(d_ref, dh_ref, x_ref, a_ref, dx_ref, ds_ref, da_ref):
        i = pl.program_id(0)
        xv, dhv = x_ref[...], dh_ref[...]
        rstd = lax.rsqrt(jnp.mean(xv * xv, axis=1, keepdims=True) + EPS)
        n1 = xv * rstd
        dn = dhv * a_ref[...]
        dx_ref[...] = d_ref[...] + rstd * (dn - n1 * jnp.mean(dn * n1, axis=1, keepdims=True))
        p_s = jnp.sum(dhv, axis=0, keepdims=True)
        p_a = jnp.sum(dhv * n1, axis=0, keepdims=True)

        @pl.when(i == 0)
        def _():
            ds_ref[...] = p_s
            da_ref[...] = p_a

        @pl.when(i > 0)
        def _():
            ds_ref[...] += p_s
            da_ref[...] += p_a

    return pl.pallas_call(
        body, name="pre_bwd", grid=(S // TR,),
        in_specs=[_row_spec(), _row_spec(), _row_spec(), _vec_spec()],
        out_specs=[_row_spec(), _vec_spec(), _vec_spec()],
        out_shape=[jax.ShapeDtypeStruct((S, D), f32), jax.ShapeDtypeStruct((1, D), f32), jax.ShapeDtypeStruct((1, D), f32)],
        compiler_params=_cp(("arbitrary",)),
    )(dout, dh, x, a_vec)


def loss_grad(y, tgt):
    S = y.shape[0]

    def body(y_ref, t_ref, dy_ref, l_ref):
        i = pl.program_id(0)
        e = y_ref[...] - t_ref[...]
        dy_ref[...] = e * (1.0 / D)
        part = jnp.sum(jnp.sum(e * e, axis=1, keepdims=True), axis=0, keepdims=True) * (0.5 / D)
        part = jnp.broadcast_to(part, (8, LANE))

        @pl.when(i == 0)
        def _():
            l_ref[...] = part

        @pl.when(i > 0)
        def _():
            l_ref[...] += part

    return pl.pallas_call(
        body, name="loss_grad", grid=(S // TR,),
        in_specs=[_row_spec(), _row_spec()],
        out_specs=[_row_spec(), pl.BlockSpec((8, LANE), lambda i: (0, 0))],
        out_shape=[jax.ShapeDtypeStruct((S, D), f32), jax.ShapeDtypeStruct((8, LANE), f32)],
        compiler_params=_cp(("arbitrary",)),
    )(y, tgt)


G_NB = D // LANE
G_TR = 2048
G_OFF = O_GATE // LANE


def gate_merge(u, pa, pb, pc):
    S = u.shape[0]

    def body(g0, g1, g2, a, b, c, o_ref):
        o_ref[...] = (_sig(g0[...]) * a[...] + _sig(g1[...]) * b[...] + _sig(g2[...]) * c[...]).astype(bf16)

    gs = [pl.BlockSpec((G_TR, LANE), functools.partial(lambda i, j, k: (i, G_OFF + G_NB * k + j), k=k)) for k in range(3)]
    ps = pl.BlockSpec((G_TR, LANE), lambda i, j: (i, j))
    return pl.pallas_call(
        body, name="gate_merge", grid=(S // G_TR, G_NB),
        in_specs=gs + [ps, ps, ps], out_specs=ps,
        out_shape=jax.ShapeDtypeStruct((S, D), bf16), compiler_params=_cp(("parallel", "parallel")),
    )(u, u, u, pa, pb, pc)


def gate_bwd(dm, u, pa, pb, pc):
    S = u.shape[0]

    def body(dm_ref, g0, g1, g2, a, b, c, da, db, dc, dg0, dg1, dg2):
        d = dm_ref[...]
        for g, p, dp, dg in ((g0, a, da, dg0), (g1, b, db, dg1), (g2, c, dc, dg2)):
            s = _sig(g[...])
            dp[...] = (d * s).astype(bf16)
            dg[...] = (d * p[...] * s * (1.0 - s)).astype(bf16)

    gs = [pl.BlockSpec((G_TR, LANE), functools.partial(lambda i, j, k: (i, G_OFF + G_NB * k + j), k=k)) for k in range(3)]
    ps = pl.BlockSpec((G_TR, LANE), lambda i, j: (i, j))
    osd = jax.ShapeDtypeStruct((S, D), bf16)
    return pl.pallas_call(
        body, name="gate_bwd", grid=(S // G_TR, G_NB),
        in_specs=[ps] + gs + [ps, ps, ps], out_specs=[ps] * 6, out_shape=[osd] * 6,
        compiler_params=_cp(("parallel", "parallel")),
    )(dm, u, u, u, pa, pb, pc)


A_TB = 512
A_NCH = A_TB // A_CHUNK
A_NSUB = A_CHUNK // A_SUB
A_HP = 6
A_KW, A_VW = A_HP * A_K, A_HP * A_V


def _hgrn_gates(qr, fr, lbh):
    sq = _sig(qr)
    sig = _sig(fr)
    f = lbh + (1.0 - lbh) * sig
    logf = jnp.log(jnp.maximum(f, TINY))
    k = (1.0 - lbh) * (1.0 - sig)
    return qr * sq, sq, sig, f, logf, k


def _hgrn_intra(qf, k, b, causal):
    qts, kts, eqs, eks, blocks = [], [], [], [], []
    for sb in range(A_NSUB):
        rs = sb * A_SUB
        r = b[rs - 1:rs, :] if sb else jnp.zeros((1, A_K), f32)
        eq = jnp.exp(b[rs:rs + A_SUB, :] - r)
        ek = jnp.exp(jnp.minimum(r - b, A_CLAMP))
        qt = (qf[rs:rs + A_SUB, :] * eq).astype(bf16)
        kt = (k * ek).astype(bf16)
        blocks.append(_dot(qt, kt, NT))
        qts.append(qt), kts.append(kt), eqs.append(eq), eks.append(ek)
    a = jnp.where(causal, jnp.concatenate(blocks, axis=0), 0.0)
    return a, qts, kts, eqs, eks


def _tri():
    r = lax.broadcasted_iota(jnp.int32, (A_CHUNK, A_CHUNK), 0)
    c = lax.broadcasted_iota(jnp.int32, (A_CHUNK, A_CHUNK), 1)
    return r >= c


def _hgrn_in_specs(rev_nb=None):
    def im(col):
        if rev_nb is None:
            return lambda p, i: (i, col + p)
        return lambda p, i: (rev_nb - 1 - i, col + p)
    return [pl.BlockSpec((A_TB, A_KW), im(O_AQ // A_KW)), pl.BlockSpec((A_TB, A_KW), im(O_AF // A_KW)),
            pl.BlockSpec((A_TB, A_VW), im(O_AI // A_VW)), pl.BlockSpec((A_TB, A_VW), im(O_AG // A_VW)),
            pl.BlockSpec((1, A_KW), lambda p, i: (0, p)), pl.BlockSpec((1, A_V), lambda p, i: (0, 0))]


def hgrn_fwd(u, lb, ng, side=None):
    S = u.shape[0]
    nb = S // A_TB

    def body(q_ref, f_ref, i_ref, g_ref, lb_ref, ng_ref, o_ref, ya_ref, st_ref, state):
        @pl.when(pl.program_id(1) == 0)
        def _():
            state[...] = jnp.zeros_like(state)

        causal = _tri()
        tri = causal.astype(f32)

        def chunk(n, carry):
            rows = pl.ds(pl.multiple_of(n * A_CHUNK, A_CHUNK), A_CHUNK)
            o_parts, y_parts = [], []
            for hh in range(A_HP):
                ks = slice(hh * A_K, (hh + 1) * A_K)
                vs = slice(hh * A_V, (hh + 1) * A_V)
                qf, _, _, _, logf, k = _hgrn_gates(q_ref[rows, ks], f_ref[rows, ks], lb_ref[:, ks])
                vi = i_ref[rows, vs].astype(bf16)
                gg = g_ref[rows, vs]
                b = _dot(tri, logf, NN, precision=lax.Precision.HIGHEST)
                s0 = state[hh]
                st_ref[n, hh] = s0
                o = _dot((qf * jnp.exp(b)).astype(bf16), s0.astype(bf16), NT)
                a, _, _, _, _ = _hgrn_intra(qf, k, b, causal)
                o = o + _dot(a.astype(bf16), vi, NN)
                bend = b[A_CHUNK - 1:A_CHUNK, :]
                ke = (k * jnp.exp(bend - b)).astype(bf16)
                state[hh] = s0 * jnp.exp(bend) + _dot(vi, ke, TN)
                rstd = lax.rsqrt(jnp.mean(o * o, axis=1, keepdims=True) + EPS)
                o_parts.append(o)
                y_parts.append(o * rstd * ng_ref[...] * (gg * _sig(gg)))
            o_ref[rows, :] = jnp.concatenate(o_parts, axis=1)
            ya_ref[rows, :] = jnp.concatenate(y_parts, axis=1).astype(bf16)
            return carry

        lax.fori_loop(0, A_NCH, chunk, 0)

    return _call(
        body, name="hgrn_fwd", grid=(A_HEADS // A_HP, nb),
        in_specs=_hgrn_in_specs(),
        out_specs=[pl.BlockSpec((A_TB, A_VW), lambda p, i: (i, p)), pl.BlockSpec((A_TB, A_VW), lambda p, i: (i, p)),
                   pl.BlockSpec((A_NCH, A_HP, A_V, A_K), lambda p, i: (i, p, 0, 0))],
        out_shape=[jax.ShapeDtypeStruct((S, 384), f32), jax.ShapeDtypeStruct((S, 384), bf16),
                   jax.ShapeDtypeStruct((S // A_CHUNK, A_HEADS, A_V, A_K), f32)],
        scratch_shapes=[pltpu.VMEM((A_HP, A_V, A_K), f32)],
        sem=("parallel", "arbitrary"), args=(u, u, u, u, lb, ng), side=side)


def hgrn_bwd(u, lb, ng, o, st, dya, side=None):
    S = u.shape[0]
    nb = S // A_TB

    def body(q_ref, f_ref, i_ref, g_ref, lb_ref, ng_ref, o_ref, st_ref, dy_ref,
             dq_ref, df_ref, di_ref, dg_ref, dlb_ref, dng_ref, dstate):
        @pl.when(pl.program_id(1) == 0)
        def _():
            dstate[...] = jnp.zeros_like(dstate)
            dlb_ref[...] = jnp.zeros_like(dlb_ref)
            dng_ref[...] = jnp.zeros_like(dng_ref)

        causal = _tri()
        tri = causal.astype(f32)

        def chunk(it, carry):
            n = A_NCH - 1 - it
            rows = pl.ds(pl.multiple_of(n * A_CHUNK, A_CHUNK), A_CHUNK)
            dq_p, df_p, di_p, dg_p, dlb_p = [], [], [], [], []
            dng_acc = jnp.zeros((1, A_V), f32)
            for hh in range(A_HP):
                ks = slice(hh * A_K, (hh + 1) * A_K)
                vs = slice(hh * A_V, (hh + 1) * A_V)
                lbh = lb_ref[:, ks]
                qr = q_ref[rows, ks]
                qf, sq, sig, f, logf, k = _hgrn_gates(qr, f_ref[rows, ks], lbh)
                vi = i_ref[rows, vs].astype(bf16)
                gg = g_ref[rows, vs]
                b = _dot(tri, logf, NN, precision=lax.Precision.HIGHEST)
                eb = jnp.exp(b)
                bend = b[A_CHUNK - 1:A_CHUNK, :]
                eend = jnp.exp(bend)
                ekend = jnp.exp(bend - b)
                qe = (qf * eb).astype(bf16)
                ke = (k * ekend).astype(bf16)
                s0 = st_ref[n, hh]
                dsend = dstate[hh]
                ov = o_ref[rows, vs]
                dy = dy_ref[rows, vs]
                rstd = lax.rsqrt(jnp.mean(ov * ov, axis=1, keepdims=True) + EPS)
                oh = ov * rstd
                sg = _sig(gg)
                d_on = dy * (gg * sg)
                dg_p.append(dy * oh * ng_ref[...] * (sg * (1.0 + gg * (1.0 - sg))))
                dng_acc = dng_acc + jnp.sum(d_on * oh, axis=0, keepdims=True)
                doh = d_on * ng_ref[...]
                do = (rstd * (doh - oh * jnp.mean(doh * oh, axis=1, keepdims=True))).astype(bf16)
                a, qts, kts, eqs, eks = _hgrn_intra(qf, k, b, causal)
                da = jnp.where(causal, _dot(do, vi, NT), 0.0).astype(bf16)
                dsb = dsend.astype(bf16)
                dv = _dot(a.astype(bf16), do, TN) + _dot(ke, dsb, NT)
                dq = _dot(do, s0.astype(bf16), NN) * eb
                dk_state = _dot(vi, dsb, NN) * ekend
                dk = dk_state
                dq_i = []
                for sb in range(A_NSUB):
                    da_sb = da[sb * A_SUB:(sb + 1) * A_SUB, :]
                    dq_i.append(_dot(da_sb, kts[sb], NN) * eqs[sb])
                    dk = dk + _dot(da_sb, qts[sb], TN) * eks[sb]
                dq = dq + jnp.concatenate(dq_i, axis=0)
                db = qf * dq - k * dk
                extra = jnp.sum(k * dk_state, axis=0, keepdims=True) + eend * jnp.sum(s0 * dsend, axis=0, keepdims=True)
                dlogf = _dot(tri, db, TN, precision=lax.Precision.HIGHEST) + extra
                dstate[hh] = _dot(do, qe, TN) + eend * dsend
                d_pre = jnp.where(f > TINY, dlogf / f, 0.0) - dk
                dlb_p.append(jnp.sum((1.0 - sig) * d_pre, axis=0, keepdims=True))
                df_p.append((1.0 - lbh) * d_pre * sig * (1.0 - sig))
                dq_p.append(dq * (sq * (1.0 + qr * (1.0 - sq))))
                di_p.append(dv)
            dq_ref[rows, :] = jnp.concatenate(dq_p, axis=1).astype(bf16)
            df_ref[rows, :] = jnp.concatenate(df_p, axis=1).astype(bf16)
            di_ref[rows, :] = jnp.concatenate(di_p, axis=1).astype(bf16)
            dg_ref[rows, :] = jnp.concatenate(dg_p, axis=1).astype(bf16)
            dlb_ref[...] += jnp.concatenate(dlb_p, axis=1)
            dng_ref[0] += dng_acc
            return carry

        lax.fori_loop(0, A_NCH, chunk, 0)

    rev = lambda p, i: (nb - 1 - i, p)
    return _call(
        body, name="hgrn_bwd", grid=(A_HEADS // A_HP, nb),
        in_specs=_hgrn_in_specs(nb) + [pl.BlockSpec((A_TB, A_VW), rev),
                                       pl.BlockSpec((A_NCH, A_HP, A_V, A_K), lambda p, i: (nb - 1 - i, p, 0, 0)),
                                       pl.BlockSpec((A_TB, A_VW), rev)],
        out_specs=[pl.BlockSpec((A_TB, A_KW), rev), pl.BlockSpec((A_TB, A_KW), rev),
                   pl.BlockSpec((A_TB, A_VW), rev), pl.BlockSpec((A_TB, A_VW), rev),
                   pl.BlockSpec((1, A_KW), lambda p, i: (0, p)), pl.BlockSpec((1, 1, A_V), lambda p, i: (p, 0, 0))],
        out_shape=[jax.ShapeDtypeStruct((S, 768), bf16), jax.ShapeDtypeStruct((S, 768), bf16),
                   jax.ShapeDtypeStruct((S, 384), bf16), jax.ShapeDtypeStruct((S, 384), bf16),
                   jax.ShapeDtypeStruct((1, 768), f32), jax.ShapeDtypeStruct((A_HEADS // A_HP, 1, A_V), f32)],
        scratch_shapes=[pltpu.VMEM((A_HP, A_V, A_K), f32)],
        sem=("parallel", "arbitrary"), args=(u, u, u, u, lb, ng, o, st, dya), side=side)


B_TK = 128
SCALE = HD ** -0.5


def _split(x):
    hi = x.astype(bf16)
    return hi, (x - hi.astype(f32)).astype(bf16)


def _dot2(x, m, dn):
    hi, lo = _split(x)
    return _dot(hi, m, dn) + _dot(lo, m, dn)


def _sb_block(qs, kh, mask, m_gt, c):
    z = _dot(qs, kh, NT)
    sp = jnp.maximum(z, 0.0) + jnp.log(1.0 + jnp.exp(-jnp.abs(z)))
    lneg = jnp.where(mask, -sp, 0.0)
    lsz = z - sp
    suf = _dot2(lneg, m_gt, NN) + c
    a = jnp.where(mask, jnp.exp(lsz + suf), 0.0)
    return lneg, lsz, a


def _sb_masks(tq, i, jj):
    t_idx = i * tq + lax.broadcasted_iota(jnp.int32, (tq, B_TK), 0)
    s_idx = jj * B_TK + lax.broadcasted_iota(jnp.int32, (tq, B_TK), 1)
    return s_idx < t_idx


def _sb_tri(strict):
    r = lax.broadcasted_iota(jnp.int32, (B_TK, B_TK), 0)
    c = lax.broadcasted_iota(jnp.int32, (B_TK, B_TK), 1)
    return (r > c if strict else r >= c).astype(bf16)


B_DEAD = -88.0


def _sb_walk(nkb, step, init):
    def cond(state):
        it, alive, _ = state
        return jnp.logical_and(it < nkb, alive)

    def body(state):
        it, _, carry = state
        carry = step(it, carry)
        top = jnp.max(functools.reduce(jnp.maximum, [h[1] for h in carry]))
        return it + 1, top > B_DEAD, carry

    return lax.while_loop(cond, body, (jnp.int32(0), jnp.bool_(True), init))[2]


B_HP = 6
B_W = B_HP * HD


def sb_fwd(u, side=None):
    S = u.shape[0]
    tq = 128

    def body(q_ref, k_ref, v_ref, o_ref):
        i = pl.program_id(1)
        nkb = (i + 1) * (tq // B_TK)
        m_gt = _sb_tri(True)
        qs = [(q_ref[:, hh * HD:(hh + 1) * HD] * SCALE).astype(bf16) for hh in range(B_HP)]

        def step(it, carry):
            jj = nkb - 1 - it
            rows = pl.ds(pl.multiple_of(jj * B_TK, B_TK), B_TK)
            mask = _sb_masks(tq, i, jj)
            kb, vb = k_ref[rows, :], v_ref[rows, :]
            out = []
            for hh in range(B_HP):
                acc, c = carry[hh]
                kh = kb[:, hh * HD:(hh + 1) * HD].astype(bf16)
                vh = vb[:, hh * HD:(hh + 1) * HD].astype(bf16)
                lneg, _, a = _sb_block(qs[hh], kh, mask, m_gt, c)
                out.append((acc + _dot2(a, vh, NN), c + jnp.sum(lneg, axis=1, keepdims=True)))
            return tuple(out)

        z0 = (jnp.zeros((tq, HD), f32), jnp.zeros((tq, 1), f32))
        res = _sb_walk(nkb, step, (z0,) * B_HP)
        o_ref[...] = jnp.concatenate([r[0] for r in res], axis=1)

    outs, souts = _call(
        body, name="sb_fwd", grid=(B_HEADS // B_HP, S // tq),
        in_specs=[pl.BlockSpec((tq, B_W), lambda p, i: (i, O_BQ // B_W + p)),
                  pl.BlockSpec((S, B_W), lambda p, i: (0, O_BK // B_W + p)),
                  pl.BlockSpec((S, B_W), lambda p, i: (0, O_BV // B_W + p))],
        out_specs=[pl.BlockSpec((tq, B_W), lambda p, i: (i, p))],
        out_shape=[jax.ShapeDtypeStruct((S, 384), f32)],
        sem=("parallel", "arbitrary"), args=(u, u, u), side=side)
    return outs[0], souts


def sb_bwd(u, yb, dyb, side=None):
    S = u.shape[0]
    tq = 128
    nq = S // tq

    def body(q_ref, k_ref, v_ref, y_ref, dy_ref, dq_ref, dk_out, dv_out, dk_ref, dv_ref, out_sem):
        p, i = pl.program_id(0), pl.program_id(1)

        @pl.when(i == 0)
        def _():
            dk_ref[...] = jnp.zeros_like(dk_ref)
            dv_ref[...] = jnp.zeros_like(dv_ref)

        nkb = (i + 1) * (tq // B_TK)
        m_gt = _sb_tri(True)
        m_ge = _sb_tri(False)
        qs, dos, tot = [], [], []
        for hh in range(B_HP):
            hs = slice(hh * HD, (hh + 1) * HD)
            qs.append((q_ref[:, hs] * SCALE).astype(bf16))
            dob = dy_ref[:, hs].astype(bf16)
            dos.append(dob)
            tot.append(jnp.sum(dob.astype(f32) * y_ref[:, hs], axis=1, keepdims=True))

        def step(it, carry):
            jj = nkb - 1 - it
            rows = pl.ds(pl.multiple_of(jj * B_TK, B_TK), B_TK)
            mask = _sb_masks(tq, i, jj)
            kb, vb = k_ref[rows, :], v_ref[rows, :]
            out, dk_p, dv_p = [], [], []
            for hh in range(B_HP):
                dq, c, cg = carry[hh]
                kh = kb[:, hh * HD:(hh + 1) * HD].astype(bf16)
                vh = vb[:, hh * HD:(hh + 1) * HD].astype(bf16)
                lneg, lsz, a = _sb_block(qs[hh], kh, mask, m_gt, c)
                g = a * _dot(dos[hh], vh, NT)
                pre = tot[hh] - cg - _dot2(g, m_ge, NN)
                beta = jnp.exp(lsz)
                dz = jnp.where(mask, g * (1.0 - beta) - beta * pre, 0.0).astype(bf16)
                dk_p.append(_dot(dz, qs[hh], TN))
                dv_p.append(_dot(a.astype(bf16), dos[hh], TN))
                out.append((dq + _dot(dz, kh, NN), c + jnp.sum(lneg, axis=1, keepdims=True),
                            cg + jnp.sum(g, axis=1, keepdims=True)))
            dk_ref[rows, :] += jnp.concatenate(dk_p, axis=1)
            dv_ref[rows, :] += jnp.concatenate(dv_p, axis=1)
            return tuple(out)

        z0 = (jnp.zeros((tq, HD), f32), jnp.zeros((tq, 1), f32), jnp.zeros((tq, 1), f32))
        res = _sb_walk(nkb, step, (z0,) * B_HP)
        dq_ref[...] = jnp.concatenate([r[0] for r in res], axis=1) * SCALE

        @pl.when(i == nq - 1)
        def _():
            cols = pl.ds(pl.multiple_of(p * B_W, LANE), B_W)
            ck = pltpu.make_async_copy(dk_ref, dk_out.at[:, cols], out_sem.at[0])
            cv = pltpu.make_async_copy(dv_ref, dv_out.at[:, cols], out_sem.at[1])
            ck.start()
            cv.start()
            ck.wait()
            cv.wait()

    row = pl.BlockSpec((tq, B_W), lambda p, i: (i, p))
    hbm = pl.BlockSpec(memory_space=pl.ANY)
    osd = jax.ShapeDtypeStruct((S, 384), f32)
    return _call(
        body, name="sb_bwd", grid=(B_HEADS // B_HP, nq),
        in_specs=[pl.BlockSpec((tq, B_W), lambda p, i: (i, O_BQ // B_W + p)),
                  pl.BlockSpec((S, B_W), lambda p, i: (0, O_BK // B_W + p)),
                  pl.BlockSpec((S, B_W), lambda p, i: (0, O_BV // B_W + p)), row, row],
        out_specs=[row, hbm, hbm], out_shape=[osd, osd, osd],
        scratch_shapes=[pltpu.VMEM((S, B_W), f32), pltpu.VMEM((S, B_W), f32), pltpu.SemaphoreType.DMA((2,))],
        sem=("parallel", "arbitrary"), args=(u, u, u, yb, dyb), side=side)


def _dil_rows(i, rho, r):
    if r == 1:
        return pl.ds(pl.multiple_of(i * C_BLK, C_BLK), C_BLK)
    return pl.ds(i * (C_BLK * r) + rho, C_BLK, stride=r)


def _dil_scores(qs, kc, kp, i, slope_r):
    qi = lax.broadcasted_iota(jnp.int32, (C_BLK, C_BLK), 0)
    kj = lax.broadcasted_iota(jnp.int32, (C_BLK, C_BLK), 1)
    d_c = qi - kj
    d_p = d_c + C_BLK
    ok_c = d_c >= 0
    ok_p = jnp.logical_and(d_c <= 0, i > 0)
    s_c = jnp.where(ok_c, _dot(qs, kc, NT) - slope_r * d_c.astype(f32), NEG_BIG)
    s_p = jnp.where(ok_p, _dot(qs, kp, NT) - slope_r * d_p.astype(f32), NEG_BIG)
    return s_c, s_p, ok_c, ok_p


def _dil_slope(g, r, hh):
    pair = pl.program_id(0)
    return jnp.where(pair == 0, C_SLOPES[4 * g + hh] * r, C_SLOPES[4 * g + 2 + hh] * r).astype(f32)


def _dil_u_specs(g, S):
    def im(off):
        return lambda p, rho: (0, (off + g * 256) // LANE + p)
    return [pl.BlockSpec((S, LANE), im(O_CQ)), pl.BlockSpec((S, LANE), im(O_CK)), pl.BlockSpec((S, LANE), im(O_CV))]


def dil_fwd(u, g, side=None):
    S = u.shape[0]
    r = C_GROUPS[g][1]
    nbk = S // r // C_BLK

    def body(q_ref, k_ref, v_ref, o_ref, l_ref):
        rho = pl.program_id(1)

        def step(i, carry):
            rc = _dil_rows(i, rho, r)
            rp = _dil_rows(jnp.maximum(i - 1, 0), rho, r)
            q2, kc2, kp2, vc2, vp2 = q_ref[rc, :], k_ref[rc, :], k_ref[rp, :], v_ref[rc, :], v_ref[rp, :]
            o_p, l_p = [], []
            for hh in range(2):
                hs = slice(hh * HD, (hh + 1) * HD)
                qs = (q2[:, hs] * SCALE).astype(bf16)
                kc, kp = kc2[:, hs].astype(bf16), kp2[:, hs].astype(bf16)
                vc, vp = vc2[:, hs].astype(bf16), vp2[:, hs].astype(bf16)
                s_c, s_p, _, _ = _dil_scores(qs, kc, kp, i, _dil_slope(g, r, hh))
                m = jnp.maximum(jnp.max(s_c, axis=1, keepdims=True), jnp.max(s_p, axis=1, keepdims=True))
                p_c, p_p = jnp.exp(s_c - m), jnp.exp(s_p - m)
                den = jnp.sum(p_c, axis=1, keepdims=True) + jnp.sum(p_p, axis=1, keepdims=True)
                o_p.append((_dot(p_c.astype(bf16), vc, NN) + _dot(p_p.astype(bf16), vp, NN)) / den)
                l_p.append(jnp.broadcast_to(m + jnp.log(den), (C_BLK, HD)))
            o_ref[rc, :] = jnp.concatenate(o_p, axis=1)
            l_ref[rc, :] = jnp.concatenate(l_p, axis=1)
            return carry

        lax.fori_loop(0, nbk, step, 0)

    ospec = pl.BlockSpec((S, LANE), lambda p, rho: (0, p))
    osd = jax.ShapeDtypeStruct((S, 256), f32)
    return _call(
        body, name=f"dil_fwd{g}", grid=(2, r),
        in_specs=_dil_u_specs(g, S), out_specs=[ospec, ospec], out_shape=[osd, osd],
        sem=("parallel", "arbitrary"), args=(u, u, u), side=side)


def dil_merge(os_, ls_):
    S = os_[0].shape[0]

    def body(o0, o1, o2, l0, l1, l2, y_ref, lse_ref):
        a, b, c = l0[...], l1[...], l2[...]
        m = jnp.maximum(jnp.maximum(a, b), c)
        ea, eb, ec = jnp.exp(a - m), jnp.exp(b - m), jnp.exp(c - m)
        den = ea + eb + ec
        y_ref[...] = (ea * o0[...] + eb * o1[...] + ec * o2[...]) / den
        lse_ref[...] = m + jnp.log(den)

    spec = pl.BlockSpec((512, 256), lambda i: (i, 0))
    osd = jax.ShapeDtypeStruct((S, 256), f32)
    return pl.pallas_call(
        body, name="dil_merge", grid=(S // 512,), in_specs=[spec] * 6, out_specs=[spec, spec],
        out_shape=[osd, osd], compiler_params=_cp(("parallel",)),
    )(*os_, *ls_)


def dil_bwd(u, g, dyc, yc, lse):
    S = u.shape[0]
    r = C_GROUPS[g][1]
    nbk = S // r // C_BLK

    def body(q_ref, k_ref, v_ref, dy_ref, y_ref, l_ref, dq_ref, dk_ref, dv_ref):
        rho = pl.program_id(1)

        @pl.when(rho == 0)
        def _():
            dk_ref[...] = jnp.zeros_like(dk_ref)
            dv_ref[...] = jnp.zeros_like(dv_ref)

        def step(i, carry):
            rc = _dil_rows(i, rho, r)
            rp = _dil_rows(jnp.maximum(i - 1, 0), rho, r)
            q2, kc2, kp2, vc2, vp2 = q_ref[rc, :], k_ref[rc, :], k_ref[rp, :], v_ref[rc, :], v_ref[rp, :]
            dy2, y2, l2 = dy_ref[rc, :], y_ref[rc, :], l_ref[rc, :]
            dq_p, dkc_p, dkp_p, dvc_p, dvp_p = [], [], [], [], []
            for hh in range(2):
                hs = slice(hh * HD, (hh + 1) * HD)
                qs = (q2[:, hs] * SCALE).astype(bf16)
                kc, kp = kc2[:, hs].astype(bf16), kp2[:, hs].astype(bf16)
                vc, vp = vc2[:, hs].astype(bf16), vp2[:, hs].astype(bf16)
                dy = dy2[:, hs]
                dyb = dy.astype(bf16)
                s_c, s_p, ok_c, ok_p = _dil_scores(qs, kc, kp, i, _dil_slope(g, r, hh))
                lrow = l2[:, hh * HD:hh * HD + 1]
                delta = jnp.sum(dy * y2[:, hs], axis=1, keepdims=True)
                pi_c = jnp.where(ok_c, jnp.exp(s_c - lrow), 0.0)
                pi_p = jnp.where(ok_p, jnp.exp(s_p - lrow), 0.0)
                ds_c = (pi_c * (_dot(dyb, vc, NT) - delta)).astype(bf16)
                ds_p = (pi_p * (_dot(dyb, vp, NT) - delta)).astype(bf16)
                dq_p.append((_dot(ds_c, kc, NN) + _dot(ds_p, kp, NN)) * SCALE)
                dkc_p.append(_dot(ds_c, qs, TN))
                dkp_p.append(_dot(ds_p, qs, TN))
                dvc_p.append(_dot(pi_c.astype(bf16), dyb, TN))
                dvp_p.append(_dot(pi_p.astype(bf16), dyb, TN))
            dq_ref[rc, :] = jnp.concatenate(dq_p, axis=1)
            dk_ref[rc, :] += jnp.concatenate(dkc_p, axis=1)
            dv_ref[rc, :] += jnp.concatenate(dvc_p, axis=1)
            dk_ref[rp, :] += jnp.concatenate(dkp_p, axis=1)
            dv_ref[rp, :] += jnp.concatenate(dvp_p, axis=1)
            return carry

        lax.fori_loop(0, nbk, step, 0)

    ospec = pl.BlockSpec((S, LANE), lambda p, rho: (0, p))
    osd = jax.ShapeDtypeStruct((S, 256), f32)
    return pl.pallas_call(
        body, name=f"dil_bwd{g}", grid=(2, r),
        in_specs=_dil_u_specs(g, S) + [ospec, ospec, ospec], out_specs=[ospec] * 3, out_shape=[osd] * 3,
        compiler_params=_cp(("parallel", "arbitrary")),
    )(u, u, u, dyc, yc, lse)


def _rows_tile(rows):
    return _tile(rows, (256, 176, 128, 64, 32, 16, 8))


def cast_bf16(w):
    shape = w.shape
    w2 = w.reshape(-1, shape[-1])
    rows, cols = w2.shape
    tr = _rows_tile(rows)

    def body(x_ref, o_ref):
        o_ref[...] = x_ref[...].astype(bf16)

    spec = pl.BlockSpec((tr, cols), lambda i: (i, 0))
    out = pl.pallas_call(
        body, name="cast_bf16", grid=(rows // tr,), in_specs=[spec], out_specs=spec,
        out_shape=jax.ShapeDtypeStruct((rows, cols), bf16), compiler_params=_cp(("parallel",)),
    )(w2)
    return out.reshape(shape)


BC1 = 1.0 - ADAM_B1 ** ADAM_STEP
BC2 = 1.0 - ADAM_B2 ** ADAM_STEP


def _adam_math(w, g, m, v):
    m2 = ADAM_B1 * m + (1.0 - ADAM_B1) * g
    v2 = ADAM_B2 * v + (1.0 - ADAM_B2) * (g * g)
    delta = -ADAM_LR * ((m2 / BC1) / (jnp.sqrt(v2 / BC2) + ADAM_EPS) + ADAM_WD * w)
    return delta, m2, v2


def adam(w, g, m, v):
    shape = w.shape
    r2 = lambda t: t.reshape(-1, shape[-1])
    rows, cols = r2(w).shape
    tr = _rows_tile(rows)

    def body(w_ref, g_ref, m_ref, v_ref, d_ref, m2_ref, v2_ref):
        d_ref[...], m2_ref[...], v2_ref[...] = _adam_math(w_ref[...], g_ref[...], m_ref[...], v_ref[...])

    spec = pl.BlockSpec((tr, cols), lambda i: (i, 0))
    osd = jax.ShapeDtypeStruct((rows, cols), f32)
    outs = pl.pallas_call(
        body, name="adam", grid=(rows // tr,), in_specs=[spec] * 4, out_specs=[spec] * 3, out_shape=[osd] * 3,
        compiler_params=_cp(("parallel",)),
    )(r2(w), r2(g), r2(m), r2(v))
    return [o.reshape(shape) for o in outs]


ADA_N = 9 * D // 4
ADA_TN = 384


def ada_fwd(c_all, w_ada):
    def body(c_ref, w_ref, o_ref):
        cv = c_ref[...]
        o_ref[0] = _dot((cv * _sig(cv)).astype(bf16), w_ref[0].astype(bf16), NN)

    return pl.pallas_call(
        body, name="ada_fwd", grid=(DEPTH, ADA_N // ADA_TN),
        in_specs=[pl.BlockSpec((8, D), lambda l, j: (0, 0)), pl.BlockSpec((1, D, ADA_TN), lambda l, j: (l, 0, j))],
        out_specs=pl.BlockSpec((1, 8, ADA_TN), lambda l, j: (l, 0, j)),
        out_shape=jax.ShapeDtypeStruct((DEPTH, 8, ADA_N), f32), compiler_params=_cp(("parallel", "parallel")),
    )(c_all, w_ada)


def ada_bwd_adam(c_all, dm, w, m, v, side=None):
    tr = 128

    def body(c_ref, dm_ref, w_ref, m_ref, v_ref, g_ref, d_ref, m2_ref, v2_ref):
        cv = c_ref[...]
        g = _dot((cv * _sig(cv)).astype(bf16), dm_ref[0].astype(bf16), TN)
        g_ref[0] = g
        d_ref[0], m2_ref[0], v2_ref[0] = _adam_math(w_ref[0], g, m_ref[0], v_ref[0])

    wspec = pl.BlockSpec((1, tr, ADA_N), lambda l, i: (l, i, 0))
    osd = jax.ShapeDtypeStruct((DEPTH, D, ADA_N), f32)
    return _call(
        body, name="ada_bwd_adam", grid=(DEPTH, D // tr),
        in_specs=[pl.BlockSpec((8, tr), lambda l, i: (0, i)), pl.BlockSpec((1, 8, ADA_N), lambda l, i: (l, 0, 0)),
                  wspec, wspec, wspec],
        out_specs=[wspec] * 4, out_shape=[osd] * 4, sem=("parallel", "parallel"), args=(c_all, dm, w, m, v), side=side)


def _lb_probs(x):
    mx = jnp.max(x, axis=0, keepdims=True)
    e = jnp.exp(x - mx)
    return e / jnp.sum(e, axis=0, keepdims=True)


def lb_fwd(logits):
    def body(x_ref, o_ref):
        p = _lb_probs(x_ref[...])
        rows = [jnp.zeros((1, 768), f32)]
        for l in range(1, DEPTH):
            rows.append(rows[-1] + p[l:l + 1, :])
        o_ref[...] = jnp.concatenate(rows, axis=0)

    return pl.pallas_call(body, name="lb_fwd", out_shape=jax.ShapeDtypeStruct((DEPTH, 768), f32))(logits)


def lb_bwd(logits, dlb):
    def body(x_ref, d_ref, o_ref):
        p = _lb_probs(x_ref[...])
        d = d_ref[...]
        rows = [jnp.zeros((1, 768), f32)] * DEPTH
        acc = jnp.zeros((1, 768), f32)
        for l in range(DEPTH - 1, 0, -1):
            acc = acc + d[l:l + 1, :]
            rows[l] = acc
        dp = jnp.concatenate(rows, axis=0)
        o_ref[...] = p * (dp - jnp.sum(p * dp, axis=0, keepdims=True))

    return pl.pallas_call(body, name="lb_bwd", out_shape=jax.ShapeDtypeStruct((DEPTH, 768), f32))(logits, dlb)


def sum_slots(x):
    n, rows, cols = x.shape
    tr = _rows_tile(rows)

    def body(x_ref, o_ref):
        acc = x_ref[0]
        for j in range(1, n):
            acc = acc + x_ref[j]
        o_ref[...] = acc

    return pl.pallas_call(
        body, name="sum_slots", grid=(rows // tr,),
        in_specs=[pl.BlockSpec((n, tr, cols), lambda i: (0, i, 0))], out_specs=pl.BlockSpec((tr, cols), lambda i: (i, 0)),
        out_shape=jax.ShapeDtypeStruct((rows, cols), f32), compiler_params=_cp(("parallel",)),
    )(x)


ANY = pl.BlockSpec(memory_space=pl.ANY)
CHIP_FLIPS = ((1, 0), (0, 1), (1, 1))
DEV_FLIPS = tuple((a, b, d) for a in (0, 1) for b in (0, 1) for d in (0, 1))[1:]


def _me():
    return lax.axis_index("x"), lax.axis_index("y"), lax.axis_index("c")


def _flip(v, f):
    return 1 - v if f else v


def _comm_call(body, name, ins, out_shapes, n_remote, n_local):
    return pl.pallas_call(
        body, name=name, in_specs=[ANY] * len(ins), out_specs=[ANY] * len(out_shapes), out_shape=out_shapes,
        scratch_shapes=[pltpu.SemaphoreType.DMA((n_remote,)), pltpu.SemaphoreType.DMA((n_remote,)),
                        pltpu.SemaphoreType.DMA((max(n_local, 1),))],
    )(*ins)


def run_plan(plan, name):
    ni, no = len(plan.ins), len(plan.outs)

    def body(*refs):
        ins, outs, sems = refs[:ni], refs[ni:ni + no], refs[ni + no:]
        plan.start(ins, outs, *sems)
        plan.wait(ins, outs, *sems)

    return pl.pallas_call(body, name=name, in_specs=[ANY] * ni, out_specs=[ANY] * no, out_shape=list(plan.outs),
                          scratch_shapes=plan.sems())(*plan.ins)


def gather_chips_plan(arrs, layer=None):
    n = len(arrs)
    shapes = [a.shape if layer is None else a.shape[1:] for a in arrs]

    def copies(ins, outs, send, recv, loc):
        x, y, c = _me()
        mine = 2 * x + y
        srcs = [r if layer is None else r.at[layer] for r in ins]
        locs = [pltpu.make_async_copy(srcs[a], outs[a].at[mine], loc.at[a]) for a in range(n)]

        def remote(a, k, slot):
            fx, fy = CHIP_FLIPS[k]
            return pltpu.make_async_remote_copy(srcs[a], outs[a].at[slot], send.at[3 * a + k], recv.at[3 * a + k],
                                                device_id=(_flip(x, fx), _flip(y, fy), c), device_id_type=MESH)

        peers = [2 * _flip(x, fx) + _flip(y, fy) for fx, fy in CHIP_FLIPS]
        return locs, remote, mine, peers

    def start(ins, outs, send, recv, loc):
        locs, remote, mine, _ = copies(ins, outs, send, recv, loc)
        for cp in locs:
            cp.start()
        for a in range(n):
            for k in range(3):
                remote(a, k, mine).start()

    def wait(ins, outs, send, recv, loc):
        locs, remote, _, peers = copies(ins, outs, send, recv, loc)
        for a in range(n):
            for k in range(3):
                cp = remote(a, k, peers[k])
                cp.wait_recv()
                cp.wait_send()
        for cp in locs:
            cp.wait()

    outs = [jax.ShapeDtypeStruct((4,) + tuple(s), a.dtype) for s, a in zip(shapes, arrs)]
    return Plan(list(arrs), outs, 3 * n, n, start, wait)


def all_gather_chips(arrs, layer=None, name="ag4"):
    return run_plan(gather_chips_plan(arrs, layer), name)


def all_gather_devs(arr, name="ag8"):
    def body(in_ref, out_ref, send, recv, loc):
        x, y, c = _me()
        mine = 4 * x + 2 * y + c
        lc = pltpu.make_async_copy(in_ref, out_ref.at[mine], loc.at[0])
        lc.start()

        def remote(k, slot):
            fx, fy, fc = DEV_FLIPS[k]
            return pltpu.make_async_remote_copy(in_ref, out_ref.at[slot], send.at[k], recv.at[k],
                                                device_id=(_flip(x, fx), _flip(y, fy), _flip(c, fc)), device_id_type=MESH)

        for k in range(7):
            remote(k, mine).start()
        for k, (fx, fy, fc) in enumerate(DEV_FLIPS):
            cp = remote(k, 4 * _flip(x, fx) + 2 * _flip(y, fy) + _flip(c, fc))
            cp.wait_recv()
            cp.wait_send()
        lc.wait()

    return _comm_call(body, name, [arr], [jax.ShapeDtypeStruct((8,) + arr.shape, arr.dtype)], 7, 1)[0]


def _rows_of(which, rows):
    return pl.ds(pl.multiple_of(which * rows, 16), rows)


def dev_exchange_plan(parts):
    n = len(parts)

    def copies(ins, outs, send, recv, loc):
        x, y, c = _me()
        mine = 4 * x + 2 * y + c

        def piece(a, px, py, pc):
            rows = ins[a].shape[1] // 2
            return ins[a].at[2 * px + py, _rows_of(pc, rows), :]

        locs = [pltpu.make_async_copy(piece(a, x, y, c), outs[a].at[mine], loc.at[a]) for a in range(n)]

        def remote(a, k, slot):
            fx, fy, fc = DEV_FLIPS[k]
            px, py, pc = _flip(x, fx), _flip(y, fy), _flip(c, fc)
            return pltpu.make_async_remote_copy(piece(a, px, py, pc), outs[a].at[slot], send.at[7 * a + k], recv.at[7 * a + k],
                                                device_id=(px, py, pc), device_id_type=MESH)

        peers = [4 * _flip(x, fx) + 2 * _flip(y, fy) + _flip(c, fc) for fx, fy, fc in DEV_FLIPS]
        return locs, remote, mine, peers

    def start(ins, outs, send, recv, loc):
        locs, remote, mine, _ = copies(ins, outs, send, recv, loc)
        for cp in locs:
            cp.start()
        for a in range(n):
            for k in range(7):
                remote(a, k, mine).start()

    def wait(ins, outs, send, recv, loc):
        locs, remote, _, peers = copies(ins, outs, send, recv, loc)
        for a in range(n):
            for k in range(7):
                cp = remote(a, k, peers[k])
                cp.wait_recv()
                cp.wait_send()
        for cp in locs:
            cp.wait()

    outs = [jax.ShapeDtypeStruct((8, p.shape[1] // 2, p.shape[2]), p.dtype) for p in parts]
    return Plan(list(parts), outs, 7 * n, n, start, wait)


def sum_share(slots, name="rs_sum"):
    n, r, cols = slots.shape
    tr = _tile(r, (128, 176, 64))
    steps = r // tr

    def body(s_ref, g_ref, buf, send, loc, recv):
        i = pl.program_id(0)
        x, y, c = _me()
        slot = i % 2

        def copies(step, sl):
            rows = pl.ds(pl.multiple_of(c * r + step * tr, 8), tr)
            rem = pltpu.make_async_remote_copy(buf.at[sl], g_ref.at[rows, :], send.at[sl], recv.at[0],
                                               device_id=(x, y, 1 - c), device_id_type=MESH)
            return rem, pltpu.make_async_copy(buf.at[sl], g_ref.at[rows, :], loc.at[sl])

        @pl.when(i >= 2)
        def _():
            rem, lc = copies(i - 2, slot)
            rem.wait_send()
            lc.wait()

        acc = s_ref[0].astype(f32)
        for j in range(1, n):
            acc = acc + s_ref[j].astype(f32)
        buf[slot] = acc
        rem, lc = copies(i, slot)
        rem.start()
        lc.start()

        @pl.when(i == steps - 1)
        def _():
            for back in range(min(2, steps)):
                rem, lc = copies(i - back, (i - back) % 2)
                rem.wait_send()
                lc.wait()
            other = g_ref.at[pl.ds(pl.multiple_of((1 - c) * r, 8), r), :]
            pltpu.make_async_remote_copy(other, other, send.at[0], recv.at[0],
                                         device_id=(x, y, 1 - c), device_id_type=MESH).wait_recv()

    return pl.pallas_call(
        body, name=name, grid=(steps,),
        in_specs=[pl.BlockSpec((n, tr, cols), lambda i: (0, i, 0))], out_specs=ANY,
        out_shape=jax.ShapeDtypeStruct((2 * r, cols), f32),
        scratch_shapes=[pltpu.VMEM((2, tr, cols), f32), pltpu.SemaphoreType.DMA((2,)), pltpu.SemaphoreType.DMA((2,)),
                        pltpu.SemaphoreType.DMA((1,))],
        compiler_params=_cp(("arbitrary",)),
    )(slots)


BIG = ("ffn1_w_in", "ffn1_w_out", "w_in", "w_branch_a", "w_branch_b", "w_branch_c", "w_out", "ffn2_w_in", "ffn2_w_out")
ROW_SHARDED = ("ffn1_w_out", "w_out", "ffn2_w_out")
RES_W = (0.5, 1.0, 0.5)


def _full_weight(name, g):
    if name in ROW_SHARDED:
        return g.reshape(4 * g.shape[1], g.shape[2])
    return jnp.concatenate([g[0], g[1], g[2], g[3]], axis=1)


def _by_shard(name, dw):
    if name in ROW_SHARDED:
        return dw.reshape(4, dw.shape[0] // 4, dw.shape[1])
    return dw.reshape(dw.shape[0], 4, dw.shape[1] // 4).transpose(1, 0, 2)


def _full_weight_t(name, g):
    if name in ROW_SHARDED:
        return g.reshape(4 * g.shape[1], g.shape[2]).T
    return g.transpose(0, 2, 1).reshape(4 * g.shape[2], g.shape[1])


def _ffn_fwd(x, w_in, w_out, a_vec, sh_vec, b_vec):
    h = prenorm(x, a_vec, sh_vec)
    ua, ub, s = ffn_in_swiglu(h, w_in)
    y = mm(s, w_out, name="ffn_out")
    return postnorm(x, y, b_vec), (x, h, ua, ub, s, y)


def _ffn_bwd(dout, saved, w_in_t, w_out_t, a_vec, b_vec, plans=None):
    x, h, ua, ub, s, y = saved
    riders = plans or (None, None, None)
    dy, db = post_bwd(dout, y, b_vec)
    dw_out = mm(s.T, dy, out_dtype=bf16, name="ffn_dwo", side=riders[0])
    du = ffn_du(dy, w_out_t, ua, ub)
    dh = mm(du, w_in_t, name="ffn_dh", side=riders[1])
    sides = []
    if plans:
        (dw_out, s0), (dh, s1) = dw_out, dh
        sides = [s0, s1]
    dw_in = mm(h.T, du, out_dtype=bf16, name="ffn_dwi", side=riders[2])
    if plans:
        dw_in, s2 = dw_in
        sides.append(s2)
    dx, dsh, da = pre_bwd(dout, dh, x, a_vec)
    return (dx, dw_in, dw_out, dsh, da, db) + ((sides,) if plans else ())


def _mix_fwd(x, w, lb, ng, a_vec, sh_vec, b_vec, plans=(None,) * 6):
    h = prenorm(x, a_vec, sh_vec)
    u = mm(h, w["w_in"], name="mix_in", side=plans[0])
    side0 = None
    if plans[0] is not None:
        u, side0 = u
    (o, ya, st), side1 = hgrn_fwd(u, lb, ng, side=plans[1])
    yb, side2 = sb_fwd(u, side=plans[2])
    groups, dil_sides = [], []
    for g in range(3):
        og, sg = dil_fwd(u, g, side=plans[3 + g])
        groups.append(og)
        dil_sides.append(sg)
    yc, lse = dil_merge([o_ for o_, _ in groups], [l_ for _, l_ in groups])
    pa = mm(ya, w["w_branch_a"], name="mix_pa")
    pb = mm(yb, w["w_branch_b"], name="mix_pb")
    pc = mm(yc, w["w_branch_c"], name="mix_pc")
    merged = gate_merge(u, pa, pb, pc)
    z = mm(merged, w["w_out"], name="mix_out")
    return (postnorm(x, z, b_vec), (x, h, u, o, ya, st, yb, yc, lse, pa, pb, pc, merged, z),
            (side0, side1, side2, *dil_sides))


def _mix_bwd(dout, saved, wt, lb, ng, a_vec, b_vec, plans=(None,) * 3):
    x, h, u, o, ya, st, yb, yc, lse, pa, pb, pc, merged, z = saved
    dz, db = post_bwd(dout, z, b_vec)
    dmerged = mm(dz, wt["w_out"], name="mix_dm")
    dw_out = mm(merged.T, dz, out_dtype=bf16, name="mix_dwo")
    dpa, dpb, dpc, dg0, dg1, dg2 = gate_bwd(dmerged, u, pa, pb, pc)
    dya = mm(dpa, wt["w_branch_a"], name="mix_dya")
    dyb = mm(dpb, wt["w_branch_b"], name="mix_dyb")
    dyc = mm(dpc, wt["w_branch_c"], name="mix_dyc")
    dw_a = mm(ya.T, dpa, out_dtype=bf16, name="mix_dwa")
    dw_b = mm(yb.astype(bf16).T, dpb, out_dtype=bf16, name="mix_dwb")
    dw_c = mm(yc.astype(bf16).T, dpc, out_dtype=bf16, name="mix_dwc")
    (daq, daf, dai, dag, dlb, dng), side0 = hgrn_bwd(u, lb, ng, o, st, dya, side=plans[0])
    (dbq, dbk, dbv), side1 = sb_bwd(u, yb, dyb, side=plans[1])
    dc = [dil_bwd(u, g, dyc, yc, lse) for g in range(3)]
    du = jnp.concatenate(
        [daq, daf, dai, dag] + [t.astype(bf16) for t in (dbq, dbk, dbv)]
        + [dc[g][j].astype(bf16) for j in range(3) for g in range(3)] + [dg0, dg1, dg2], axis=1)
    dh = mm(du, wt["w_in"], name="mix_dh")
    dw_in = mm(h.T, du, out_dtype=bf16, name="mix_dwi", side=plans[2])
    side2 = None
    if plans[2] is not None:
        dw_in, side2 = dw_in
    dx, dsh, da = pre_bwd(dout, dh, x, a_vec)
    grads = {"w_in": dw_in, "w_out": dw_out, "w_branch_a": dw_a, "w_branch_b": dw_b, "w_branch_c": dw_c}
    return dx, grads, dlb, jnp.sum(dng, axis=0), dsh, da, db, (side0, side1, side2)


FWD_RIDERS = (("ffn1_w_in",), ("ffn2_w_in",), ("w_in",), ("ffn1_w_out", "w_out"), ("ffn2_w_out", "w_branch_a"),
              ("w_branch_b", "w_branch_c"))
BWD_RIDERS = (("ffn1_w_in", "ffn2_w_out", "w_branch_a", "w_branch_b", "w_branch_c"), ("w_in", "ffn1_w_out", "w_out"),
              ("ffn2_w_in",))
LAST_RIDERS = (("ffn2_w_out", "w_out", "w_branch_a", "w_branch_b", "w_branch_c"), ("ffn2_w_in",), ("w_in",))
TAIL = ("ffn1_w_in", "ffn1_w_out")


def _reduce_to_shards(names, grads):
    slots = run_plan(dev_exchange_plan([_by_shard(n, grads[n]) for n in names]), "rs_x8")
    return {n: sum_share(s) for n, s in zip(names, slots)}


def kernel(x, c, w_ada, b_ada, norm_g, ffn1_w_in, ffn1_w_out, w_in, hgrn_lb_logits, hgrn_norm_g, w_branch_a, w_branch_b, w_branch_c, w_out, ffn2_w_in, ffn2_w_out, loss_target, m_w_ada, m_b_ada, m_norm_g, m_ffn1_w_in, m_ffn1_w_out, m_w_in, m_hgrn_lb_logits, m_hgrn_norm_g, m_w_branch_a, m_w_branch_b, m_w_branch_c, m_w_out, m_ffn2_w_in, m_ffn2_w_out, v_w_ada, v_b_ada, v_norm_g, v_ffn1_w_in, v_ffn1_w_out, v_w_in, v_hgrn_lb_logits, v_hgrn_norm_g, v_w_branch_a, v_w_branch_b, v_w_branch_c, v_w_out, v_ffn2_w_in, v_ffn2_w_out):
    weights = dict(w_ada=w_ada, b_ada=b_ada, norm_g=norm_g, ffn1_w_in=ffn1_w_in, ffn1_w_out=ffn1_w_out, w_in=w_in,
                   hgrn_lb_logits=hgrn_lb_logits, hgrn_norm_g=hgrn_norm_g, w_branch_a=w_branch_a, w_branch_b=w_branch_b,
                   w_branch_c=w_branch_c, w_out=w_out, ffn2_w_in=ffn2_w_in, ffn2_w_out=ffn2_w_out)
    mom = dict(w_ada=m_w_ada, b_ada=m_b_ada, norm_g=m_norm_g, ffn1_w_in=m_ffn1_w_in, ffn1_w_out=m_ffn1_w_out, w_in=m_w_in,
               hgrn_lb_logits=m_hgrn_lb_logits, hgrn_norm_g=m_hgrn_norm_g, w_branch_a=m_w_branch_a, w_branch_b=m_w_branch_b,
               w_branch_c=m_w_branch_c, w_out=m_w_out, ffn2_w_in=m_ffn2_w_in, ffn2_w_out=m_ffn2_w_out)
    var = dict(w_ada=v_w_ada, b_ada=v_b_ada, norm_g=v_norm_g, ffn1_w_in=v_ffn1_w_in, ffn1_w_out=v_ffn1_w_out, w_in=v_w_in,
               hgrn_lb_logits=v_hgrn_lb_logits, hgrn_norm_g=v_hgrn_norm_g, w_branch_a=v_w_branch_a, w_branch_b=v_w_branch_b,
               w_branch_c=v_w_branch_c, w_out=v_w_out, ffn2_w_in=v_ffn2_w_in, ffn2_w_out=v_ffn2_w_out)
    order = list(weights)
    xi, yi, ci = _me()
    chip = 2 * xi + yi
    dev = 4 * xi + 2 * yi + ci
    xs = x[0]

    c_all = all_gather_devs(c, name="ag8_c").reshape(8, D)
    mod_sh = all_gather_chips([ada_fwd(c_all, w_ada)], name="ag4_mod")[0]
    mod_all = mod_sh.transpose(1, 2, 0, 3).reshape(DEPTH, 8, 9 * D)
    mod = lax.dynamic_index_in_dim(mod_all, dev, axis=1, keepdims=False) + b_ada
    mod = mod.reshape(DEPTH, 3, 3, D)
    ng_all = all_gather_chips([norm_g.reshape(DEPTH * 6, D // 4)], name="ag4_norm")[0]
    ng_all = ng_all.reshape(4, DEPTH, 6, D // 4).transpose(1, 2, 0, 3).reshape(DEPTH, 6, D)
    lb_all = lb_fwd(hgrn_lb_logits)
    w16 = {n: cast_bf16(weights[n]) for n in BIG}

    def vecs(l, i):
        shift, scale, gate = mod[l, i, 0][None], mod[l, i, 1][None], mod[l, i, 2][None]
        g_pre, g_post = ng_all[l, 2 * i][None], ng_all[l, 2 * i + 1][None]
        return g_pre * (1.0 + scale), shift, RES_W[i] * gate * g_post

    saved, full = [], []
    gathered = dict(zip(BIG, all_gather_chips([w16[n] for n in BIG], layer=0, name="ag4_w0")))
    for l in range(DEPTH):
        w = {n: _full_weight(n, gathered[n]) for n in BIG}
        full.append({n: _full_weight_t(n, gathered[n]) for n in BIG})
        lb, ng = lb_all[l][None], hgrn_norm_g[l][None]
        plans = (None,) * len(FWD_RIDERS)
        if l + 1 < DEPTH:
            plans = tuple(gather_chips_plan([w16[n] for n in names], layer=l + 1) for names in FWD_RIDERS)
        xs, s1 = _ffn_fwd(xs, w["ffn1_w_in"], w["ffn1_w_out"], *vecs(l, 0))
        xs, s2, sides = _mix_fwd(xs, w, lb, ng, *vecs(l, 1), plans=plans)
        xs, s3 = _ffn_fwd(xs, w["ffn2_w_in"], w["ffn2_w_out"], *vecs(l, 2))
        saved.append((s1, s2, s3))
        if l + 1 < DEPTH:
            gathered = {n: g for names, outs in zip(FWD_RIDERS, sides) for n, g in zip(names, outs)}

    dx, loss_part = loss_grad(xs, loss_target[0])
    loss = lax.psum(loss_part[0, 0], ("x", "y", "c"))

    big_grads = {n: [None] * DEPTH for n in BIG}
    d_mod, d_ng, d_lb, d_hng = [None] * DEPTH, [None] * DEPTH, [None] * DEPTH, [None] * DEPTH
    pending = None
    for l in reversed(range(DEPTH)):
        wt = full[l]
        s1, s2, s3 = saved[l]
        lb, ng = lb_all[l][None], hgrn_norm_g[l][None]
        rows_mod, rows_ng = [None] * 9, [None] * 6

        def vec_grads(i, dsh, da, db):
            scale, gate = mod[l, i, 1][None], mod[l, i, 2][None]
            g_pre, g_post = ng_all[l, 2 * i][None], ng_all[l, 2 * i + 1][None]
            rows_mod[3 * i], rows_mod[3 * i + 1], rows_mod[3 * i + 2] = dsh, g_pre * da, RES_W[i] * g_post * db
            rows_ng[2 * i], rows_ng[2 * i + 1] = (1.0 + scale) * da, RES_W[i] * gate * db

        a3, _, b3 = vecs(l, 2)
        dx, dwi, dwo, dsh, da, db = _ffn_bwd(dx, s3, wt["ffn2_w_in"], wt["ffn2_w_out"], a3, b3)
        vec_grads(2, dsh, da, db)
        grads = {"ffn2_w_in": dwi, "ffn2_w_out": dwo}
        a2, _, b2 = vecs(l, 1)
        plans = (None,) * len(BWD_RIDERS)
        if pending is not None:
            plans = tuple(dev_exchange_plan([pending[n] for n in names]) for names in BWD_RIDERS)
        dx, gmix, dlb, dhng, dsh, da, db, sides = _mix_bwd(dx, s2, wt, lb, ng, a2, b2, plans=plans)
        if pending is not None:
            for names, outs in zip(BWD_RIDERS, sides):
                for n, slots in zip(names, outs):
                    big_grads[n][l + 1] = sum_share(slots)
        vec_grads(1, dsh, da, db)
        grads.update(gmix)
        a1, _, b1 = vecs(l, 0)
        if l > 0:
            dx, dwi, dwo, dsh, da, db = _ffn_bwd(dx, s1, wt["ffn1_w_in"], wt["ffn1_w_out"], a1, b1)
        else:
            ready = {n: _by_shard(n, grads[n]) for names in LAST_RIDERS for n in names}
            plans = tuple(dev_exchange_plan([ready[n] for n in names]) for names in LAST_RIDERS)
            dx, dwi, dwo, dsh, da, db, sides = _ffn_bwd(dx, s1, wt["ffn1_w_in"], wt["ffn1_w_out"], a1, b1, plans=plans)
            for names, outs in zip(LAST_RIDERS, sides):
                for n, slots in zip(names, outs):
                    big_grads[n][0] = sum_share(slots)
        vec_grads(0, dsh, da, db)
        grads.update({"ffn1_w_in": dwi, "ffn1_w_out": dwo})
        pending = {n: _by_shard(n, grads[n]) for n in (BIG if l > 0 else TAIL)}
        d_mod[l] = jnp.concatenate(rows_mod, axis=1)
        d_ng[l] = jnp.concatenate(rows_ng, axis=0)
        d_lb[l], d_hng[l] = dlb, dhng

    n_small = 6 * D * DEPTH + 768 * DEPTH + A_V * DEPTH + 9 * D * DEPTH
    pad = -n_small % (512 * LANE)
    flat = jnp.concatenate([jnp.stack(d_ng).reshape(-1), jnp.concatenate(d_lb, axis=0).reshape(-1),
                            jnp.concatenate(d_hng, axis=0).reshape(-1), jnp.concatenate(d_mod, axis=0).reshape(-1),
                            jnp.zeros((pad,), f32)])
    small_all = all_gather_devs(flat.reshape(-1, LANE), name="ag8_small")
    total = sum_slots(small_all).reshape(-1)
    o1 = 6 * D * DEPTH
    o2 = o1 + 768 * DEPTH
    o3 = o2 + A_V * DEPTH
    g_ng_full = total[:o1].reshape(DEPTH, 6, D)
    g_lb_all = total[o1:o2].reshape(DEPTH, 768)
    g_small = {
        "norm_g": lax.dynamic_slice_in_dim(g_ng_full, chip * (D // 4), D // 4, axis=2),
        "hgrn_lb_logits": lb_bwd(hgrn_lb_logits, g_lb_all),
        "hgrn_norm_g": total[o2:o3].reshape(DEPTH, A_V),
        "b_ada": total[o3:n_small].reshape(DEPTH, 9 * D),
    }
    dmod_all = small_all.reshape(8, -1)[:, o3:n_small].reshape(8, DEPTH, 9 * D).transpose(1, 0, 2)
    dm_sh = lax.dynamic_slice_in_dim(dmod_all, chip * ADA_N, ADA_N, axis=2)

    out_g, out_d, out_m, out_v = {}, {}, {}, {}
    ada_outs, slots = ada_bwd_adam(c_all, dm_sh, w_ada, m_w_ada, v_w_ada, side=dev_exchange_plan([pending[n] for n in TAIL]))
    out_g["w_ada"], out_d["w_ada"], out_m["w_ada"], out_v["w_ada"] = ada_outs
    for n, s in zip(TAIL, slots):
        big_grads[n][0] = sum_share(s)
    for n in BIG:
        out_g[n] = jnp.stack(big_grads[n])
    out_g.update(g_small)
    for n in order:
        if n != "w_ada":
            out_d[n], out_m[n], out_v[n] = adam(weights[n], out_g[n], mom[n], var[n])
    return (loss, dx[None], *[out_g[n] for n in order], *[out_d[n] for n in order],
            *[out_m[n] for n in order], *[out_v[n] for n in order])
```

```python
import functools
import math

import jax
import jax.numpy as jnp
from jax import lax
from jax.experimental import pallas as pl
from jax.experimental.pallas import tpu as pltpu

f32, bf16 = jnp.float32, jnp.bfloat16

D = 1024
DEPTH = 4
D_FF = 2816
EPS = 1e-6
NEG_BIG = -1e30
TINY = 1e-30
A_HEADS, A_K, A_V, A_CHUNK = 6, 128, 64, 64
A_SUB = 16
A_CLAMP = 80.0
B_HEADS, HD = 6, 64
C_GROUPS = ((128, 1), (512, 4), (2048, 16))
C_BLK = 128
IN_COLS = 8832
O_AQ, O_AF, O_AI, O_AG = 0, 768, 1536, 1920
O_BQ, O_BK, O_BV = 2304, 2688, 3072
O_CQ, O_CK, O_CV = 3456, 4224, 4992
O_GATE = 5760
LANE = 128
ADAM_LR, ADAM_B1, ADAM_B2, ADAM_EPS, ADAM_WD, ADAM_STEP = 0.001, 0.9, 0.999, 1e-08, 0.01, 10
MESH = pl.DeviceIdType.MESH
VMEM_LIMIT = 56 * 1024 * 1024


def _alibi_slopes(n):
    def pow2(m):
        start = 2.0 ** (-8.0 / m)
        return [start ** (i + 1) for i in range(m)]
    if math.log2(n).is_integer():
        s = pow2(n)
    else:
        c = 2 ** int(math.floor(math.log2(n)))
        s = pow2(c) + pow2(2 * c)[0::2][: n - c]
    return sorted(s, reverse=True)


C_SLOPES = _alibi_slopes(12)


def _tile(n, prefs):
    for p in prefs:
        if n % p == 0:
            return p
    return n


def _cp(sem):
    return pltpu.CompilerParams(dimension_semantics=sem, vmem_limit_bytes=VMEM_LIMIT)


def _sig(x):
    return 1.0 / (1.0 + jnp.exp(-x))


def _dot(a, b, dn, precision=None):
    return lax.dot_general(a, b, (dn, ((), ())), preferred_element_type=f32, precision=precision)


NN = ((1,), (0,))
NT = ((1,), (1,))
TN = ((0,), (0,))


class Plan:
    def __init__(self, ins, outs, n_remote, n_local, start, wait):
        self.ins, self.outs, self.n_remote, self.n_local, self.start, self.wait = ins, outs, n_remote, n_local, start, wait

    def sems(self):
        return [pltpu.SemaphoreType.DMA((self.n_remote,)), pltpu.SemaphoreType.DMA((self.n_remote,)),
                pltpu.SemaphoreType.DMA((max(self.n_local, 1),))]


def _call(body, *, name, grid, in_specs, out_specs, out_shape, sem, args, scratch_shapes=(), side=None):
    if side is None:
        return pl.pallas_call(body, name=name, grid=grid, in_specs=in_specs, out_specs=out_specs, out_shape=out_shape,
                              scratch_shapes=list(scratch_shapes), compiler_params=_cp(sem))(*args), None
    any_spec = pl.BlockSpec(memory_space=pl.ANY)
    n_in, n_out, n_scr = len(in_specs), len(out_specs), len(scratch_shapes)
    s_in, s_out = len(side.ins), len(side.outs)

    def hosted(*refs):
        ins, rest = refs[:n_in], refs[n_in:]
        sins, rest = rest[:s_in], rest[s_in:]
        outs, rest = rest[:n_out], rest[n_out:]
        souts, rest = rest[:s_out], rest[s_out:]
        scr, sems = rest[:n_scr], rest[n_scr:]
        pids = [pl.program_id(d) for d in range(len(grid))]
        first = functools.reduce(jnp.logical_and, [p == 0 for p in pids])
        last = functools.reduce(jnp.logical_and, [p == g - 1 for p, g in zip(pids, grid)])

        @pl.when(first)
        def _():
            side.start(sins, souts, *sems)

        body(*ins, *outs, *scr)

        @pl.when(last)
        def _():
            side.wait(sins, souts, *sems)

    res = pl.pallas_call(
        hosted, name=name, grid=grid, in_specs=list(in_specs) + [any_spec] * s_in,
        out_specs=list(out_specs) + [any_spec] * s_out, out_shape=list(out_shape) + list(side.outs),
        scratch_shapes=list(scratch_shapes) + side.sems(), compiler_params=_cp(("arbitrary",) * len(grid)),
    )(*args, *side.ins)
    return res[:n_out], res[n_out:]


MM_TILES = {
    (4096, 1024, 2816): (1024, 512, 2816),
    (4096, 1024, 5632): (512, 512, 5632),
    (1024, 5632, 4096): (512, 512, 4096),
    (2816, 1024, 4096): (704, 512, 4096),
    (4096, 8832, 1024): (512, 2944, 1024),
    (4096, 1024, 8832): (1024, 512, 2944),
    (1024, 8832, 4096): (512, 2944, 1024),
    (1024, 1024, 4096): (512, 512, 4096),
    (384, 1024, 4096): (384, 512, 4096),
    (256, 1024, 4096): (256, 512, 4096),
}


def mm(a, b, *, out_dtype=f32, name="mm", side=None):
    M, K = a.shape
    K2, N = b.shape
    assert K == K2, (a.shape, b.shape)
    tm, tn, tk = MM_TILES.get((M, N, K), (_tile(M, (1024, 704, 512, 384, 256, 128)), _tile(N, (512, 384, 256, 128)),
                                          _tile(K, (1024, 512, 1408, 384, 256, 128))))
    nk = K // tk

    def body(a_ref, b_ref, o_ref, *acc):
        p = _dot(a_ref[...].astype(bf16), b_ref[...].astype(bf16), NN)
        if nk == 1:
            o_ref[...] = p.astype(out_dtype)
            return
        acc_ref, = acc
        k = pl.program_id(2)

        @pl.when(k == 0)
        def _():
            acc_ref[...] = p

        @pl.when(k > 0)
        def _():
            acc_ref[...] += p

        @pl.when(k == nk - 1)
        def _():
            o_ref[...] = acc_ref[...].astype(out_dtype)

    outs, souts = _call(
        body, name=name, grid=(M // tm, N // tn, nk),
        in_specs=[pl.BlockSpec((tm, tk), lambda i, j, k: (i, k)), pl.BlockSpec((tk, tn), lambda i, j, k: (k, j))],
        out_specs=[pl.BlockSpec((tm, tn), lambda i, j, k: (i, j))],
        out_shape=[jax.ShapeDtypeStruct((M, N), out_dtype)],
        scratch_shapes=[pltpu.VMEM((tm, tn), f32)] if nk > 1 else [],
        sem=("parallel", "parallel", "arbitrary"), args=(a, b), side=side)
    return outs[0] if side is None else (outs[0], souts)


FF_TM = 1024
FF_T = 256
FF_NB = D_FF // FF_T


def ffn_in_swiglu(h, w_in):
    S = h.shape[0]

    def body(h_ref, wa_ref, wb_ref, a_ref, b_ref, s_ref, st_ref):
        hv = h_ref[...]
        a = _dot(hv, wa_ref[...], NN)
        b = _dot(hv, wb_ref[...], NN)
        a_ref[...] = a.astype(bf16)
        b_ref[...] = b.astype(bf16)
        s = (a * _sig(a) * b).astype(bf16)
        s_ref[...] = s
        st_ref[...] = s.T

    ospec = pl.BlockSpec((FF_TM, FF_T), lambda i, j: (i, j))
    osd = jax.ShapeDtypeStruct((S, D_FF), bf16)
    return pl.pallas_call(
        body, name="ffn_in", grid=(S // FF_TM, FF_NB),
        in_specs=[pl.BlockSpec((FF_TM, D), lambda i, j: (i, 0)), pl.BlockSpec((D, FF_T), lambda i, j: (0, j)),
                  pl.BlockSpec((D, FF_T), lambda i, j: (0, j + FF_NB))],
        out_specs=[ospec] * 3 + [pl.BlockSpec((FF_T, FF_TM), lambda i, j: (j, i))],
        out_shape=[osd] * 3 + [jax.ShapeDtypeStruct((D_FF, S), bf16)], compiler_params=_cp(("parallel", "parallel")),
    )(h, w_in, w_in)


def ffn_du(dy, w_out_t, ua, ub):
    S = dy.shape[0]
    tm = 512

    def body(dy_ref, w_ref, a_ref, b_ref, du_ref):
        dyv = dy_ref[...]
        for j in range(FF_NB):
            cols = slice(j * FF_T, (j + 1) * FF_T)
            ds = _dot(dyv, w_ref[:, cols], NN)
            a, b = a_ref[:, cols].astype(f32), b_ref[:, cols].astype(f32)
            sg = _sig(a)
            du_ref[:, cols] = (ds * b * sg * (1.0 + a * (1.0 - sg))).astype(bf16)
            du_ref[:, D_FF + j * FF_T:D_FF + (j + 1) * FF_T] = (ds * a * sg).astype(bf16)

    half = pl.BlockSpec((tm, D_FF), lambda i: (i, 0))
    return pl.pallas_call(
        body, name="ffn_du", grid=(S // tm,),
        in_specs=[pl.BlockSpec((tm, D), lambda i: (i, 0)), pl.BlockSpec((D, D_FF), lambda i: (0, 0)), half, half],
        out_specs=pl.BlockSpec((tm, 2 * D_FF), lambda i: (i, 0)),
        out_shape=jax.ShapeDtypeStruct((S, 2 * D_FF), bf16), compiler_params=_cp(("parallel",)),
    )(dy, w_out_t, ua, ub)


TR = 512


def _row_spec(cols=D):
    return pl.BlockSpec((TR, cols), lambda i: (i, 0))


def _vec_spec(cols=D):
    return pl.BlockSpec((1, cols), lambda i: (0, 0))


def prenorm(x, a_vec, sh_vec):
    S = x.shape[0]

    def body(x_ref, a_ref, s_ref, h_ref, ht_ref):
        xv = x_ref[...]
        rstd = lax.rsqrt(jnp.mean(xv * xv, axis=1, keepdims=True) + EPS)
        h = (xv * rstd * a_ref[...] + s_ref[...]).astype(bf16)
        h_ref[...] = h
        ht_ref[...] = h.T

    return pl.pallas_call(
        body, name="prenorm", grid=(S // TR,),
        in_specs=[_row_spec(), _vec_spec(), _vec_spec()],
        out_specs=[_row_spec(), pl.BlockSpec((D, TR), lambda i: (0, i))],
        out_shape=[jax.ShapeDtypeStruct((S, D), bf16), jax.ShapeDtypeStruct((D, S), bf16)],
        compiler_params=_cp(("parallel",)),
    )(x, a_vec, sh_vec)


def postnorm(x, y, b_vec):
    S = x.shape[0]

    def body(x_ref, y_ref, b_ref, o_ref):
        yv = y_ref[...]
        rstd = lax.rsqrt(jnp.mean(yv * yv, axis=1, keepdims=True) + EPS)
        o_ref[...] = x_ref[...] + b_ref[...] * (yv * rstd)

    return pl.pallas_call(
        body, name="postnorm", grid=(S // TR,),
        in_specs=[_row_spec(), _row_spec(), _vec_spec()], out_specs=_row_spec(),
        out_shape=jax.ShapeDtypeStruct((S, D), f32), compiler_params=_cp(("parallel",)),
    )(x, y, b_vec)


def post_bwd(dout, y, b_vec):
    S = dout.shape[0]

    def body(d_ref, y_ref, b_ref, dy_ref, db_ref):
        i = pl.program_id(0)
        yv, dv = y_ref[...], d_ref[...]
        rstd = lax.rsqrt(jnp.mean(yv * yv, axis=1, keepdims=True) + EPS)
        yh = yv * rstd
        dyh = dv * b_ref[...]
        dy_ref[...] = (rstd * (dyh - yh * jnp.mean(dyh * yh, axis=1, keepdims=True))).astype(bf16)
        part = jnp.sum(dv * yh, axis=0, keepdims=True)

        @pl.when(i == 0)
        def _():
            db_ref[...] = part

        @pl.when(i > 0)
        def _():
            db_ref[...] += part

    return pl.pallas_call(
        body, name="post_bwd", grid=(S // TR,),
        in_specs=[_row_spec(), _row_spec(), _vec_spec()], out_specs=[_row_spec(), _vec_spec()],
        out_shape=[jax.ShapeDtypeStruct((S, D), bf16), jax.ShapeDtypeStruct((1, D), f32)],
        compiler_params=_cp(("arbitrary",)),
    )(dout, y, b_vec)


def pre_bwd(dout, dh, x, a_vec):
    S = dout.shape[0]

    def body(d_ref, dh_ref, x_ref, a_ref, dx_ref, ds_ref, da_ref):
        i = pl.program_id(0)
        xv, dhv = x_ref[...], dh_ref[...]
        rstd = lax.rsqrt(jnp.mean(xv * xv, axis=1, keepdims=True) + EPS)
        n1 = xv * rstd
        dn = dhv * a_ref[...]
        dx_ref[...] = d_ref[...] + rstd * (dn - n1 * jnp.mean(dn * n1, axis=1, keepdims=True))
        p_s = jnp.sum(dhv, axis=0, keepdims=True)
        p_a = jnp.sum(dhv * n1, axis=0, keepdims=True)

        @pl.when(i == 0)
        def _():
            ds_ref[...] = p_s
            da_ref[...] = p_a

        @pl.when(i > 0)
        def _():
            ds_ref[...] += p_s
            da_ref[...] += p_a

    return pl.pallas_call(
        body, name="pre_bwd", grid=(S // TR,),
        in_specs=[_row_spec(), _row_spec(), _row_spec(), _vec_spec()],
        out_specs=[_row_spec(), _vec_spec(), _vec_spec()],
        out_shape=[jax.ShapeDtypeStruct((S, D), f32), jax.ShapeDtypeStruct((1, D), f32), jax.ShapeDtypeStruct((1, D), f32)],
        compiler_params=_cp(("arbitrary",)),
    )(dout, dh, x, a_vec)


def loss_grad(y, tgt):
    S = y.shape[0]

    def body(y_ref, t_ref, dy_ref, l_ref):
        i = pl.program_id(0)
        e = y_ref[...] - t_ref[...]
        dy_ref[...] = e * (1.0 / D)
        part = jnp.sum(jnp.sum(e * e, axis=1, keepdims=True), axis=0, keepdims=True) * (0.5 / D)
        part = jnp.broadcast_to(part, (8, LANE))

        @pl.when(i == 0)
        def _():
            l_ref[...] = part

        @pl.when(i > 0)
        def _():
            l_ref[...] += part

    return pl.pallas_call(
        body, name="loss_grad", grid=(S // TR,),
        in_specs=[_row_spec(), _row_spec()],
        out_specs=[_row_spec(), pl.BlockSpec((8, LANE), lambda i: (0, 0))],
        out_shape=[jax.ShapeDtypeStruct((S, D), f32), jax.ShapeDtypeStruct((8, LANE), f32)],
        compiler_params=_cp(("arbitrary",)),
    )(y, tgt)


G_NB = D // LANE
G_TR = 2048
G_OFF = O_GATE // LANE


def gate_merge(u, pa, pb, pc):
    S = u.shape[0]

    def body(g0, g1, g2, a, b, c, o_ref, ot_ref):
        m = (_sig(g0[...]) * a[...] + _sig(g1[...]) * b[...] + _sig(g2[...]) * c[...]).astype(bf16)
        o_ref[...] = m
        ot_ref[...] = m.T

    gs = [pl.BlockSpec((G_TR, LANE), functools.partial(lambda i, j, k: (i, G_OFF + G_NB * k + j), k=k)) for k in range(3)]
    ps = pl.BlockSpec((G_TR, LANE), lambda i, j: (i, j))
    return pl.pallas_call(
        body, name="gate_merge", grid=(S // G_TR, G_NB),
        in_specs=gs + [ps, ps, ps], out_specs=[ps, pl.BlockSpec((LANE, G_TR), lambda i, j: (j, i))],
        out_shape=[jax.ShapeDtypeStruct((S, D), bf16), jax.ShapeDtypeStruct((D, S), bf16)],
        compiler_params=_cp(("parallel", "parallel")),
    )(u, u, u, pa, pb, pc)


def gate_bwd(dm, u, pa, pb, pc):
    S = u.shape[0]

    def body(dm_ref, g0, g1, g2, a, b, c, da, db, dc, dg0, dg1, dg2):
        d = dm_ref[...]
        for g, p, dp, dg in ((g0, a, da, dg0), (g1, b, db, dg1), (g2, c, dc, dg2)):
            s = _sig(g[...])
            dp[...] = (d * s).astype(bf16)
            dg[...] = (d * p[...] * s * (1.0 - s)).astype(bf16)

    gs = [pl.BlockSpec((G_TR, LANE), functools.partial(lambda i, j, k: (i, G_OFF + G_NB * k + j), k=k)) for k in range(3)]
    ps = pl.BlockSpec((G_TR, LANE), lambda i, j: (i, j))
    osd = jax.ShapeDtypeStruct((S, D), bf16)
    return pl.pallas_call(
        body, name="gate_bwd", grid=(S // G_TR, G_NB),
        in_specs=[ps] + gs + [ps, ps, ps], out_specs=[ps] * 6, out_shape=[osd] * 6,
        compiler_params=_cp(("parallel", "parallel")),
    )(dm, u, u, u, pa, pb, pc)


A_TB = 512
A_NCH = A_TB // A_CHUNK
A_NSUB = A_CHUNK // A_SUB
A_HP = 6
A_KW, A_VW = A_HP * A_K, A_HP * A_V


def _hgrn_gates(qr, fr, lbh):
    sq = _sig(qr)
    sig = _sig(fr)
    f = lbh + (1.0 - lbh) * sig
    logf = jnp.log(jnp.maximum(f, TINY))
    k = (1.0 - lbh) * (1.0 - sig)
    return qr * sq, sq, sig, f, logf, k


def _hgrn_intra(qf, k, b, causal):
    qts, kts, eqs, eks, blocks = [], [], [], [], []
    for sb in range(A_NSUB):
        rs = sb * A_SUB
        r = b[rs - 1:rs, :] if sb else jnp.zeros((1, A_K), f32)
        eq = jnp.exp(b[rs:rs + A_SUB, :] - r)
        ek = jnp.exp(jnp.minimum(r - b, A_CLAMP))
        qt = (qf[rs:rs + A_SUB, :] * eq).astype(bf16)
        kt = (k * ek).astype(bf16)
        blocks.append(_dot(qt, kt, NT))
        qts.append(qt), kts.append(kt), eqs.append(eq), eks.append(ek)
    a = jnp.where(causal, jnp.concatenate(blocks, axis=0), 0.0)
    return a, qts, kts, eqs, eks


def _tri():
    r = lax.broadcasted_iota(jnp.int32, (A_CHUNK, A_CHUNK), 0)
    c = lax.broadcasted_iota(jnp.int32, (A_CHUNK, A_CHUNK), 1)
    return r >= c


def _hgrn_in_specs(rev_nb=None):
    def im(col):
        if rev_nb is None:
            return lambda p, i: (i, col + p)
        return lambda p, i: (rev_nb - 1 - i, col + p)
    return [pl.BlockSpec((A_TB, A_KW), im(O_AQ // A_KW)), pl.BlockSpec((A_TB, A_KW), im(O_AF // A_KW)),
            pl.BlockSpec((A_TB, A_VW), im(O_AI // A_VW)), pl.BlockSpec((A_TB, A_VW), im(O_AG // A_VW)),
            pl.BlockSpec((1, A_KW), lambda p, i: (0, p)), pl.BlockSpec((1, A_V), lambda p, i: (0, 0))]


def hgrn_fwd(u, lb, ng, side=None):
    S = u.shape[0]
    nb = S // A_TB

    def body(q_ref, f_ref, i_ref, g_ref, lb_ref, ng_ref, o_ref, ya_ref, st_ref, state):
        @pl.when(pl.program_id(1) == 0)
        def _():
            state[...] = jnp.zeros_like(state)

        causal = _tri()
        tri = causal.astype(f32)

        def chunk(n, carry):
            rows = pl.ds(pl.multiple_of(n * A_CHUNK, A_CHUNK), A_CHUNK)
            o_parts, y_parts = [], []
            for hh in range(A_HP):
                ks = slice(hh * A_K, (hh + 1) * A_K)
                vs = slice(hh * A_V, (hh + 1) * A_V)
                qf, _, _, _, logf, k = _hgrn_gates(q_ref[rows, ks], f_ref[rows, ks], lb_ref[:, ks])
                vi = i_ref[rows, vs].astype(bf16)
                gg = g_ref[rows, vs]
                b = _dot(tri, logf, NN, precision=lax.Precision.HIGHEST)
                s0 = state[hh]
                st_ref[n, hh] = s0
                o = _dot((qf * jnp.exp(b)).astype(bf16), s0.astype(bf16), NT)
                a, _, _, _, _ = _hgrn_intra(qf, k, b, causal)
                o = o + _dot(a.astype(bf16), vi, NN)
                bend = b[A_CHUNK - 1:A_CHUNK, :]
                ke = (k * jnp.exp(bend - b)).astype(bf16)
                state[hh] = s0 * jnp.exp(bend) + _dot(vi, ke, TN)
                rstd = lax.rsqrt(jnp.mean(o * o, axis=1, keepdims=True) + EPS)
                o_parts.append(o)
                y_parts.append(o * rstd * ng_ref[...] * (gg * _sig(gg)))
            o_ref[rows, :] = jnp.concatenate(o_parts, axis=1)
            ya_ref[rows, :] = jnp.concatenate(y_parts, axis=1).astype(bf16)
            return carry

        lax.fori_loop(0, A_NCH, chunk, 0)

    return _call(
        body, name="hgrn_fwd", grid=(A_HEADS // A_HP, nb),
        in_specs=_hgrn_in_specs(),
        out_specs=[pl.BlockSpec((A_TB, A_VW), lambda p, i: (i, p)), pl.BlockSpec((A_TB, A_VW), lambda p, i: (i, p)),
                   pl.BlockSpec((A_NCH, A_HP, A_V, A_K), lambda p, i: (i, p, 0, 0))],
        out_shape=[jax.ShapeDtypeStruct((S, 384), f32), jax.ShapeDtypeStruct((S, 384), bf16),
                   jax.ShapeDtypeStruct((S // A_CHUNK, A_HEADS, A_V, A_K), f32)],
        scratch_shapes=[pltpu.VMEM((A_HP, A_V, A_K), f32)],
        sem=("parallel", "arbitrary"), args=(u, u, u, u, lb, ng), side=side)


def hgrn_bwd(u, lb, ng, o, st, dya, side=None):
    S = u.shape[0]
    nb = S // A_TB

    def body(q_ref, f_ref, i_ref, g_ref, lb_ref, ng_ref, o_ref, st_ref, dy_ref,
             dq_ref, df_ref, di_ref, dg_ref, dlb_ref, dng_ref, dstate):
        @pl.when(pl.program_id(1) == 0)
        def _():
            dstate[...] = jnp.zeros_like(dstate)
            dlb_ref[...] = jnp.zeros_like(dlb_ref)
            dng_ref[...] = jnp.zeros_like(dng_ref)

        causal = _tri()
        tri = causal.astype(f32)

        def chunk(it, carry):
            n = A_NCH - 1 - it
            rows = pl.ds(pl.multiple_of(n * A_CHUNK, A_CHUNK), A_CHUNK)
            dq_p, df_p, di_p, dg_p, dlb_p = [], [], [], [], []
            dng_acc = jnp.zeros((1, A_V), f32)
            for hh in range(A_HP):
                ks = slice(hh * A_K, (hh + 1) * A_K)
                vs = slice(hh * A_V, (hh + 1) * A_V)
                lbh = lb_ref[:, ks]
                qr = q_ref[rows, ks]
                qf, sq, sig, f, logf, k = _hgrn_gates(qr, f_ref[rows, ks], lbh)
                vi = i_ref[rows, vs].astype(bf16)
                gg = g_ref[rows, vs]
                b = _dot(tri, logf, NN, precision=lax.Precision.HIGHEST)
                eb = jnp.exp(b)
                bend = b[A_CHUNK - 1:A_CHUNK, :]
                eend = jnp.exp(bend)
                ekend = jnp.exp(bend - b)
                qe = (qf * eb).astype(bf16)
                ke = (k * ekend).astype(bf16)
                s0 = st_ref[n, hh]
                dsend = dstate[hh]
                ov = o_ref[rows, vs]
                dy = dy_ref[rows, vs]
                rstd = lax.rsqrt(jnp.mean(ov * ov, axis=1, keepdims=True) + EPS)
                oh = ov * rstd
                sg = _sig(gg)
                d_on = dy * (gg * sg)
                dg_p.append(dy * oh * ng_ref[...] * (sg * (1.0 + gg * (1.0 - sg))))
                dng_acc = dng_acc + jnp.sum(d_on * oh, axis=0, keepdims=True)
                doh = d_on * ng_ref[...]
                do = (rstd * (doh - oh * jnp.mean(doh * oh, axis=1, keepdims=True))).astype(bf16)
                a, qts, kts, eqs, eks = _hgrn_intra(qf, k, b, causal)
                da = jnp.where(causal, _dot(do, vi, NT), 0.0).astype(bf16)
                dsb = dsend.astype(bf16)
                dv = _dot(a.astype(bf16), do, TN) + _dot(ke, dsb, NT)
                dq = _dot(do, s0.astype(bf16), NN) * eb
                dk_state = _dot(vi, dsb, NN) * ekend
                dk = dk_state
                dq_i = []
                for sb in range(A_NSUB):
                    da_sb = da[sb * A_SUB:(sb + 1) * A_SUB, :]
                    dq_i.append(_dot(da_sb, kts[sb], NN) * eqs[sb])
                    dk = dk + _dot(da_sb, qts[sb], TN) * eks[sb]
                dq = dq + jnp.concatenate(dq_i, axis=0)
                db = qf * dq - k * dk
                extra = jnp.sum(k * dk_state, axis=0, keepdims=True) + eend * jnp.sum(s0 * dsend, axis=0, keepdims=True)
                dlogf = _dot(tri, db, TN, precision=lax.Precision.HIGHEST) + extra
                dstate[hh] = _dot(do, qe, TN) + eend * dsend
                d_pre = jnp.where(f > TINY, dlogf / f, 0.0) - dk
                dlb_p.append(jnp.sum((1.0 - sig) * d_pre, axis=0, keepdims=True))
                df_p.append((1.0 - lbh) * d_pre * sig * (1.0 - sig))
                dq_p.append(dq * (sq * (1.0 + qr * (1.0 - sq))))
                di_p.append(dv)
            dq_ref[rows, :] = jnp.concatenate(dq_p, axis=1).astype(bf16)
            df_ref[rows, :] = jnp.concatenate(df_p, axis=1).astype(bf16)
            di_ref[rows, :] = jnp.concatenate(di_p, axis=1).astype(bf16)
            dg_ref[rows, :] = jnp.concatenate(dg_p, axis=1).astype(bf16)
            dlb_ref[...] += jnp.concatenate(dlb_p, axis=1)
            dng_ref[0] += dng_acc
            return carry

        lax.fori_loop(0, A_NCH, chunk, 0)

    rev = lambda p, i: (nb - 1 - i, p)
    return _call(
        body, name="hgrn_bwd", grid=(A_HEADS // A_HP, nb),
        in_specs=_hgrn_in_specs(nb) + [pl.BlockSpec((A_TB, A_VW), rev),
                                       pl.BlockSpec((A_NCH, A_HP, A_V, A_K), lambda p, i: (nb - 1 - i, p, 0, 0)),
                                       pl.BlockSpec((A_TB, A_VW), rev)],
        out_specs=[pl.BlockSpec((A_TB, A_KW), rev), pl.BlockSpec((A_TB, A_KW), rev),
                   pl.BlockSpec((A_TB, A_VW), rev), pl.BlockSpec((A_TB, A_VW), rev),
                   pl.BlockSpec((1, A_KW), lambda p, i: (0, p)), pl.BlockSpec((1, 1, A_V), lambda p, i: (p, 0, 0))],
        out_shape=[jax.ShapeDtypeStruct((S, 768), bf16), jax.ShapeDtypeStruct((S, 768), bf16),
                   jax.ShapeDtypeStruct((S, 384), bf16), jax.ShapeDtypeStruct((S, 384), bf16),
                   jax.ShapeDtypeStruct((1, 768), f32), jax.ShapeDtypeStruct((A_HEADS // A_HP, 1, A_V), f32)],
        scratch_shapes=[pltpu.VMEM((A_HP, A_V, A_K), f32)],
        sem=("parallel", "arbitrary"), args=(u, u, u, u, lb, ng, o, st, dya), side=side)


B_TK = 128
SCALE = HD ** -0.5


def _split(x):
    hi = x.astype(bf16)
    return hi, (x - hi.astype(f32)).astype(bf16)


def _dot2(x, m, dn):
    hi, lo = _split(x)
    return _dot(hi, m, dn) + _dot(lo, m, dn)


def _sb_block(qs, kh, mask, m_gt, c):
    z = _dot(qs, kh, NT)
    sp = jnp.maximum(z, 0.0) + jnp.log(1.0 + jnp.exp(-jnp.abs(z)))
    lneg = jnp.where(mask, -sp, 0.0)
    lsz = z - sp
    suf = _dot2(lneg, m_gt, NN) + c
    a = jnp.where(mask, jnp.exp(lsz + suf), 0.0)
    return lneg, lsz, a


def _sb_masks(tq, i, jj):
    t_idx = i * tq + lax.broadcasted_iota(jnp.int32, (tq, B_TK), 0)
    s_idx = jj * B_TK + lax.broadcasted_iota(jnp.int32, (tq, B_TK), 1)
    return s_idx < t_idx


def _sb_tri(strict):
    r = lax.broadcasted_iota(jnp.int32, (B_TK, B_TK), 0)
    c = lax.broadcasted_iota(jnp.int32, (B_TK, B_TK), 1)
    return (r > c if strict else r >= c).astype(bf16)


B_DEAD = -88.0


def _sb_walk(nkb, step, init):
    def cond(state):
        it, alive, _ = state
        return jnp.logical_and(it < nkb, alive)

    def body(state):
        it, _, carry = state
        carry = step(it, carry)
        top = jnp.max(functools.reduce(jnp.maximum, [h[1] for h in carry]))
        return it + 1, top > B_DEAD, carry

    return lax.while_loop(cond, body, (jnp.int32(0), jnp.bool_(True), init))[2]


B_HP = 6
B_W = B_HP * HD


def sb_fwd(u, side=None):
    S = u.shape[0]
    tq = 128

    def body(q_ref, k_ref, v_ref, o_ref):
        i = pl.program_id(1)
        nkb = (i + 1) * (tq // B_TK)
        m_gt = _sb_tri(True)
        qs = [(q_ref[:, hh * HD:(hh + 1) * HD] * SCALE).astype(bf16) for hh in range(B_HP)]

        def step(it, carry):
            jj = nkb - 1 - it
            rows = pl.ds(pl.multiple_of(jj * B_TK, B_TK), B_TK)
            mask = _sb_masks(tq, i, jj)
            kb, vb = k_ref[rows, :], v_ref[rows, :]
            out = []
            for hh in range(B_HP):
                acc, c = carry[hh]
                kh = kb[:, hh * HD:(hh + 1) * HD].astype(bf16)
                vh = vb[:, hh * HD:(hh + 1) * HD].astype(bf16)
                lneg, _, a = _sb_block(qs[hh], kh, mask, m_gt, c)
                out.append((acc + _dot2(a, vh, NN), c + jnp.sum(lneg, axis=1, keepdims=True)))
            return tuple(out)

        z0 = (jnp.zeros((tq, HD), f32), jnp.zeros((tq, 1), f32))
        res = _sb_walk(nkb, step, (z0,) * B_HP)
        o_ref[...] = jnp.concatenate([r[0] for r in res], axis=1)

    outs, souts = _call(
        body, name="sb_fwd", grid=(B_HEADS // B_HP, S // tq),
        in_specs=[pl.BlockSpec((tq, B_W), lambda p, i: (i, O_BQ // B_W + p)),
                  pl.BlockSpec((S, B_W), lambda p, i: (0, O_BK // B_W + p)),
                  pl.BlockSpec((S, B_W), lambda p, i: (0, O_BV // B_W + p))],
        out_specs=[pl.BlockSpec((tq, B_W), lambda p, i: (i, p))],
        out_shape=[jax.ShapeDtypeStruct((S, 384), f32)],
        sem=("parallel", "arbitrary"), args=(u, u, u), side=side)
    return outs[0], souts


def sb_bwd(u, yb, dyb, side=None):
    S = u.shape[0]
    tq = 128
    nq = S // tq

    def body(q_ref, k_ref, v_ref, y_ref, dy_ref, dq_ref, dk_out, dv_out, dk_ref, dv_ref, out_sem):
        p, i = pl.program_id(0), pl.program_id(1)

        @pl.when(i == 0)
        def _():
            dk_ref[...] = jnp.zeros_like(dk_ref)
            dv_ref[...] = jnp.zeros_like(dv_ref)

        nkb = (i + 1) * (tq // B_TK)
        m_gt = _sb_tri(True)
        m_ge = _sb_tri(False)
        qs, dos, tot = [], [], []
        for hh in range(B_HP):
            hs = slice(hh * HD, (hh + 1) * HD)
            qs.append((q_ref[:, hs] * SCALE).astype(bf16))
            dob = dy_ref[:, hs].astype(bf16)
            dos.append(dob)
            tot.append(jnp.sum(dob.astype(f32) * y_ref[:, hs], axis=1, keepdims=True))

        def step(it, carry):
            jj = nkb - 1 - it
            rows = pl.ds(pl.multiple_of(jj * B_TK, B_TK), B_TK)
            mask = _sb_masks(tq, i, jj)
            kb, vb = k_ref[rows, :], v_ref[rows, :]
            out, dk_p, dv_p = [], [], []
            for hh in range(B_HP):
                dq, c, cg = carry[hh]
                kh = kb[:, hh * HD:(hh + 1) * HD].astype(bf16)
                vh = vb[:, hh * HD:(hh + 1) * HD].astype(bf16)
                lneg, lsz, a = _sb_block(qs[hh], kh, mask, m_gt, c)
                g = a * _dot(dos[hh], vh, NT)
                pre = tot[hh] - cg - _dot2(g, m_ge, NN)
                beta = jnp.exp(lsz)
                dz = jnp.where(mask, g * (1.0 - beta) - beta * pre, 0.0).astype(bf16)
                dk_p.append(_dot(dz, qs[hh], TN))
                dv_p.append(_dot(a.astype(bf16), dos[hh], TN))
                out.append((dq + _dot(dz, kh, NN), c + jnp.sum(lneg, axis=1, keepdims=True),
                            cg + jnp.sum(g, axis=1, keepdims=True)))
            dk_ref[rows, :] += jnp.concatenate(dk_p, axis=1)
            dv_ref[rows, :] += jnp.concatenate(dv_p, axis=1)
            return tuple(out)

        z0 = (jnp.zeros((tq, HD), f32), jnp.zeros((tq, 1), f32), jnp.zeros((tq, 1), f32))
        res = _sb_walk(nkb, step, (z0,) * B_HP)
        dq_ref[...] = jnp.concatenate([r[0] for r in res], axis=1) * SCALE

        @pl.when(i == nq - 1)
        def _():
            cols = pl.ds(pl.multiple_of(p * B_W, LANE), B_W)
            ck = pltpu.make_async_copy(dk_ref, dk_out.at[:, cols], out_sem.at[0])
            cv = pltpu.make_async_copy(dv_ref, dv_out.at[:, cols], out_sem.at[1])
            ck.start()
            cv.start()
            ck.wait()
            cv.wait()

    row = pl.BlockSpec((tq, B_W), lambda p, i: (i, p))
    hbm = pl.BlockSpec(memory_space=pl.ANY)
    osd = jax.ShapeDtypeStruct((S, 384), f32)
    return _call(
        body, name="sb_bwd", grid=(B_HEADS // B_HP, nq),
        in_specs=[pl.BlockSpec((tq, B_W), lambda p, i: (i, O_BQ // B_W + p)),
                  pl.BlockSpec((S, B_W), lambda p, i: (0, O_BK // B_W + p)),
                  pl.BlockSpec((S, B_W), lambda p, i: (0, O_BV // B_W + p)), row, row],
        out_specs=[row, hbm, hbm], out_shape=[osd, osd, osd],
        scratch_shapes=[pltpu.VMEM((S, B_W), f32), pltpu.VMEM((S, B_W), f32), pltpu.SemaphoreType.DMA((2,))],
        sem=("parallel", "arbitrary"), args=(u, u, u, yb, dyb), side=side)


def _dil_rows(i, rho, r):
    if r == 1:
        return pl.ds(pl.multiple_of(i * C_BLK, C_BLK), C_BLK)
    return pl.ds(i * (C_BLK * r) + rho, C_BLK, stride=r)


def _dil_scores(qs, kc, kp, i, slope_r):
    qi = lax.broadcasted_iota(jnp.int32, (C_BLK, C_BLK), 0)
    kj = lax.broadcasted_iota(jnp.int32, (C_BLK, C_BLK), 1)
    d_c = qi - kj
    d_p = d_c + C_BLK
    ok_c = d_c >= 0
    ok_p = jnp.logical_and(d_c <= 0, i > 0)
    s_c = jnp.where(ok_c, _dot(qs, kc, NT) - slope_r * d_c.astype(f32), NEG_BIG)
    s_p = jnp.where(ok_p, _dot(qs, kp, NT) - slope_r * d_p.astype(f32), NEG_BIG)
    return s_c, s_p, ok_c, ok_p


def _dil_slope(g, r, hh):
    pair = pl.program_id(0)
    return jnp.where(pair == 0, C_SLOPES[4 * g + hh] * r, C_SLOPES[4 * g + 2 + hh] * r).astype(f32)


def _dil_u_specs(g, S):
    def im(off):
        return lambda p, rho: (0, (off + g * 256) // LANE + p)
    return [pl.BlockSpec((S, LANE), im(O_CQ)), pl.BlockSpec((S, LANE), im(O_CK)), pl.BlockSpec((S, LANE), im(O_CV))]


def dil_fwd(u, g, side=None):
    S = u.shape[0]
    r = C_GROUPS[g][1]
    nbk = S // r // C_BLK

    def body(q_ref, k_ref, v_ref, o_ref, l_ref):
        rho = pl.program_id(1)

        def step(i, carry):
            rc = _dil_rows(i, rho, r)
            rp = _dil_rows(jnp.maximum(i - 1, 0), rho, r)
            q2, kc2, kp2, vc2, vp2 = q_ref[rc, :], k_ref[rc, :], k_ref[rp, :], v_ref[rc, :], v_ref[rp, :]
            o_p, l_p = [], []
            for hh in range(2):
                hs = slice(hh * HD, (hh + 1) * HD)
                qs = (q2[:, hs] * SCALE).astype(bf16)
                kc, kp = kc2[:, hs].astype(bf16), kp2[:, hs].astype(bf16)
                vc, vp = vc2[:, hs].astype(bf16), vp2[:, hs].astype(bf16)
                s_c, s_p, _, _ = _dil_scores(qs, kc, kp, i, _dil_slope(g, r, hh))
                m = jnp.maximum(jnp.max(s_c, axis=1, keepdims=True), jnp.max(s_p, axis=1, keepdims=True))
                p_c, p_p = jnp.exp(s_c - m), jnp.exp(s_p - m)
                den = jnp.sum(p_c, axis=1, keepdims=True) + jnp.sum(p_p, axis=1, keepdims=True)
                o_p.append((_dot(p_c.astype(bf16), vc, NN) + _dot(p_p.astype(bf16), vp, NN)) / den)
                l_p.append(jnp.broadcast_to(m + jnp.log(den), (C_BLK, HD)))
            o_ref[rc, :] = jnp.concatenate(o_p, axis=1)
            l_ref[rc, :] = jnp.concatenate(l_p, axis=1)
            return carry

        lax.fori_loop(0, nbk, step, 0)

    ospec = pl.BlockSpec((S, LANE), lambda p, rho: (0, p))
    osd = jax.ShapeDtypeStruct((S, 256), f32)
    return _call(
        body, name=f"dil_fwd{g}", grid=(2, r),
        in_specs=_dil_u_specs(g, S), out_specs=[ospec, ospec], out_shape=[osd, osd],
        sem=("parallel", "arbitrary"), args=(u, u, u), side=side)


def dil_merge(os_, ls_):
    S = os_[0].shape[0]

    def body(o0, o1, o2, l0, l1, l2, y_ref, lse_ref):
        a, b, c = l0[...], l1[...], l2[...]
        m = jnp.maximum(jnp.maximum(a, b), c)
        ea, eb, ec = jnp.exp(a - m), jnp.exp(b - m), jnp.exp(c - m)
        den = ea + eb + ec
        y_ref[...] = (ea * o0[...] + eb * o1[...] + ec * o2[...]) / den
        lse_ref[...] = m + jnp.log(den)

    spec = pl.BlockSpec((512, 256), lambda i: (i, 0))
    osd = jax.ShapeDtypeStruct((S, 256), f32)
    return pl.pallas_call(
        body, name="dil_merge", grid=(S // 512,), in_specs=[spec] * 6, out_specs=[spec, spec],
        out_shape=[osd, osd], compiler_params=_cp(("parallel",)),
    )(*os_, *ls_)


def dil_bwd(u, g, dyc, yc, lse):
    S = u.shape[0]
    r = C_GROUPS[g][1]
    nbk = S // r // C_BLK

    def body(q_ref, k_ref, v_ref, dy_ref, y_ref, l_ref, dq_ref, dk_ref, dv_ref):
        rho = pl.program_id(1)

        @pl.when(rho == 0)
        def _():
            dk_ref[...] = jnp.zeros_like(dk_ref)
            dv_ref[...] = jnp.zeros_like(dv_ref)

        def step(i, carry):
            rc = _dil_rows(i, rho, r)
            rp = _dil_rows(jnp.maximum(i - 1, 0), rho, r)
            q2, kc2, kp2, vc2, vp2 = q_ref[rc, :], k_ref[rc, :], k_ref[rp, :], v_ref[rc, :], v_ref[rp, :]
            dy2, y2, l2 = dy_ref[rc, :], y_ref[rc, :], l_ref[rc, :]
            dq_p, dkc_p, dkp_p, dvc_p, dvp_p = [], [], [], [], []
            for hh in range(2):
                hs = slice(hh * HD, (hh + 1) * HD)
                qs = (q2[:, hs] * SCALE).astype(bf16)
                kc, kp = kc2[:, hs].astype(bf16), kp2[:, hs].astype(bf16)
                vc, vp = vc2[:, hs].astype(bf16), vp2[:, hs].astype(bf16)
                dy = dy2[:, hs]
                dyb = dy.astype(bf16)
                s_c, s_p, ok_c, ok_p = _dil_scores(qs, kc, kp, i, _dil_slope(g, r, hh))
                lrow = l2[:, hh * HD:hh * HD + 1]
                delta = jnp.sum(dy * y2[:, hs], axis=1, keepdims=True)
                pi_c = jnp.where(ok_c, jnp.exp(s_c - lrow), 0.0)
                pi_p = jnp.where(ok_p, jnp.exp(s_p - lrow), 0.0)
                ds_c = (pi_c * (_dot(dyb, vc, NT) - delta)).astype(bf16)
                ds_p = (pi_p * (_dot(dyb, vp, NT) - delta)).astype(bf16)
                dq_p.append((_dot(ds_c, kc, NN) + _dot(ds_p, kp, NN)) * SCALE)
                dkc_p.append(_dot(ds_c, qs, TN))
                dkp_p.append(_dot(ds_p, qs, TN))
                dvc_p.append(_dot(pi_c.astype(bf16), dyb, TN))
                dvp_p.append(_dot(pi_p.astype(bf16), dyb, TN))
            dq_ref[rc, :] = jnp.concatenate(dq_p, axis=1)
            dk_ref[rc, :] += jnp.concatenate(dkc_p, axis=1)
            dv_ref[rc, :] += jnp.concatenate(dvc_p, axis=1)
            dk_ref[rp, :] += jnp.concatenate(dkp_p, axis=1)
            dv_ref[rp, :] += jnp.concatenate(dvp_p, axis=1)
            return carry

        lax.fori_loop(0, nbk, step, 0)

    ospec = pl.BlockSpec((S, LANE), lambda p, rho: (0, p))
    osd = jax.ShapeDtypeStruct((S, 256), f32)
    return pl.pallas_call(
        body, name=f"dil_bwd{g}", grid=(2, r),
        in_specs=_dil_u_specs(g, S) + [ospec, ospec, ospec], out_specs=[ospec] * 3, out_shape=[osd] * 3,
        compiler_params=_cp(("parallel", "arbitrary")),
    )(u, u, u, dyc, yc, lse)


def _rows_tile(rows):
    return _tile(rows, (256, 176, 128, 64, 32, 16, 8))


def cast_bf16(w):
    shape = w.shape
    w2 = w.reshape(-1, shape[-1])
    rows, cols = w2.shape
    tr = _rows_tile(rows)

    def body(x_ref, o_ref):
        o_ref[...] = x_ref[...].astype(bf16)

    spec = pl.BlockSpec((tr, cols), lambda i: (i, 0))
    out = pl.pallas_call(
        body, name="cast_bf16", grid=(rows // tr,), in_specs=[spec], out_specs=spec,
        out_shape=jax.ShapeDtypeStruct((rows, cols), bf16), compiler_params=_cp(("parallel",)),
    )(w2)
    return out.reshape(shape)


BC1 = 1.0 - ADAM_B1 ** ADAM_STEP
BC2 = 1.0 - ADAM_B2 ** ADAM_STEP


def _adam_math(w, g, m, v):
    m2 = ADAM_B1 * m + (1.0 - ADAM_B1) * g
    v2 = ADAM_B2 * v + (1.0 - ADAM_B2) * (g * g)
    delta = -ADAM_LR * ((m2 / BC1) / (jnp.sqrt(v2 / BC2) + ADAM_EPS) + ADAM_WD * w)
    return delta, m2, v2


def adam(w, g, m, v):
    shape = w.shape
    r2 = lambda t: t.reshape(-1, shape[-1])
    rows, cols = r2(w).shape
    tr = _rows_tile(rows)

    def body(w_ref, g_ref, m_ref, v_ref, d_ref, m2_ref, v2_ref):
        d_ref[...], m2_ref[...], v2_ref[...] = _adam_math(w_ref[...], g_ref[...], m_ref[...], v_ref[...])

    spec = pl.BlockSpec((tr, cols), lambda i: (i, 0))
    osd = jax.ShapeDtypeStruct((rows, cols), f32)
    outs = pl.pallas_call(
        body, name="adam", grid=(rows // tr,), in_specs=[spec] * 4, out_specs=[spec] * 3, out_shape=[osd] * 3,
        compiler_params=_cp(("parallel",)),
    )(r2(w), r2(g), r2(m), r2(v))
    return [o.reshape(shape) for o in outs]


def adam_layers(w, gs, m, v):
    depth, rows, cols = w.shape
    tr = _tile(rows, (128, 64, 32, 16, 8))
    nb = rows // tr

    def body(w_ref, m_ref, v_ref, *rest):
        g_refs, (g_out, d_ref, m2_ref, v2_ref) = rest[:depth], rest[depth:]
        for k in range(depth):
            @pl.when(pl.program_id(0) == k)
            def _(k=k):
                g = g_refs[k][...]
                g_out[0] = g
                d_ref[0], m2_ref[0], v2_ref[0] = _adam_math(w_ref[0], g, m_ref[0], v_ref[0])

    def g_spec(k):
        return pl.BlockSpec((tr, cols), lambda l, i: (jnp.where(l < k, 0, jnp.where(l > k, nb - 1, i)), 0))

    wspec = pl.BlockSpec((1, tr, cols), lambda l, i: (l, i, 0))
    osd = jax.ShapeDtypeStruct(w.shape, f32)
    return pl.pallas_call(
        body, name="adam_layers", grid=(depth, nb), in_specs=[wspec] * 3 + [g_spec(k) for k in range(depth)],
        out_specs=[wspec] * 4, out_shape=[osd] * 4, compiler_params=_cp(("arbitrary", "arbitrary")),
    )(w, m, v, *gs)


ADA_N = 9 * D // 4
ADA_TN = 384


def ada_fwd(c_all, w_ada):
    def body(c_ref, w_ref, o_ref):
        cv = c_ref[...]
        o_ref[0] = _dot((cv * _sig(cv)).astype(bf16), w_ref[0].astype(bf16), NN)

    return pl.pallas_call(
        body, name="ada_fwd", grid=(DEPTH, ADA_N // ADA_TN),
        in_specs=[pl.BlockSpec((8, D), lambda l, j: (0, 0)), pl.BlockSpec((1, D, ADA_TN), lambda l, j: (l, 0, j))],
        out_specs=pl.BlockSpec((1, 8, ADA_TN), lambda l, j: (l, 0, j)),
        out_shape=jax.ShapeDtypeStruct((DEPTH, 8, ADA_N), f32), compiler_params=_cp(("parallel", "parallel")),
    )(c_all, w_ada)


def ada_bwd_adam(c_all, dm, w, m, v, side=None):
    tr = 128

    def body(c_ref, dm_ref, w_ref, m_ref, v_ref, g_ref, d_ref, m2_ref, v2_ref):
        cv = c_ref[...]
        g = _dot((cv * _sig(cv)).astype(bf16), dm_ref[0].astype(bf16), TN)
        g_ref[0] = g
        d_ref[0], m2_ref[0], v2_ref[0] = _adam_math(w_ref[0], g, m_ref[0], v_ref[0])

    wspec = pl.BlockSpec((1, tr, ADA_N), lambda l, i: (l, i, 0))
    osd = jax.ShapeDtypeStruct((DEPTH, D, ADA_N), f32)
    return _call(
        body, name="ada_bwd_adam", grid=(DEPTH, D // tr),
        in_specs=[pl.BlockSpec((8, tr), lambda l, i: (0, i)), pl.BlockSpec((1, 8, ADA_N), lambda l, i: (l, 0, 0)),
                  wspec, wspec, wspec],
        out_specs=[wspec] * 4, out_shape=[osd] * 4, sem=("parallel", "parallel"), args=(c_all, dm, w, m, v), side=side)


def _lb_probs(x):
    mx = jnp.max(x, axis=0, keepdims=True)
    e = jnp.exp(x - mx)
    return e / jnp.sum(e, axis=0, keepdims=True)


def lb_fwd(logits):
    def body(x_ref, o_ref):
        p = _lb_probs(x_ref[...])
        rows = [jnp.zeros((1, 768), f32)]
        for l in range(1, DEPTH):
            rows.append(rows[-1] + p[l:l + 1, :])
        o_ref[...] = jnp.concatenate(rows, axis=0)

    return pl.pallas_call(body, name="lb_fwd", out_shape=jax.ShapeDtypeStruct((DEPTH, 768), f32))(logits)


def lb_bwd(logits, dlb):
    def body(x_ref, d_ref, o_ref):
        p = _lb_probs(x_ref[...])
        d = d_ref[...]
        rows = [jnp.zeros((1, 768), f32)] * DEPTH
        acc = jnp.zeros((1, 768), f32)
        for l in range(DEPTH - 1, 0, -1):
            acc = acc + d[l:l + 1, :]
            rows[l] = acc
        dp = jnp.concatenate(rows, axis=0)
        o_ref[...] = p * (dp - jnp.sum(p * dp, axis=0, keepdims=True))

    return pl.pallas_call(body, name="lb_bwd", out_shape=jax.ShapeDtypeStruct((DEPTH, 768), f32))(logits, dlb)


def sum_slots(x):
    n, rows, cols = x.shape
    tr = _rows_tile(rows)

    def body(x_ref, o_ref):
        acc = x_ref[0]
        for j in range(1, n):
            acc = acc + x_ref[j]
        o_ref[...] = acc

    return pl.pallas_call(
        body, name="sum_slots", grid=(rows // tr,),
        in_specs=[pl.BlockSpec((n, tr, cols), lambda i: (0, i, 0))], out_specs=pl.BlockSpec((tr, cols), lambda i: (i, 0)),
        out_shape=jax.ShapeDtypeStruct((rows, cols), f32), compiler_params=_cp(("parallel",)),
    )(x)


ANY = pl.BlockSpec(memory_space=pl.ANY)
CHIP_FLIPS = ((1, 0), (0, 1), (1, 1))
DEV_FLIPS = tuple((a, b, d) for a in (0, 1) for b in (0, 1) for d in (0, 1))[1:]


def _me():
    return lax.axis_index("x"), lax.axis_index("y"), lax.axis_index("c")


def _flip(v, f):
    return 1 - v if f else v


def _comm_call(body, name, ins, out_shapes, n_remote, n_local):
    return pl.pallas_call(
        body, name=name, in_specs=[ANY] * len(ins), out_specs=[ANY] * len(out_shapes), out_shape=out_shapes,
        scratch_shapes=[pltpu.SemaphoreType.DMA((n_remote,)), pltpu.SemaphoreType.DMA((n_remote,)),
                        pltpu.SemaphoreType.DMA((max(n_local, 1),))],
    )(*ins)


def run_plan(plan, name):
    ni, no = len(plan.ins), len(plan.outs)

    def body(*refs):
        ins, outs, sems = refs[:ni], refs[ni:ni + no], refs[ni + no:]
        plan.start(ins, outs, *sems)
        plan.wait(ins, outs, *sems)

    return pl.pallas_call(body, name=name, in_specs=[ANY] * ni, out_specs=[ANY] * no, out_shape=list(plan.outs),
                          scratch_shapes=plan.sems())(*plan.ins)


def gather_chips_plan(arrs, layer=None):
    n = len(arrs)
    shapes = [a.shape if layer is None else a.shape[1:] for a in arrs]

    def copies(ins, outs, send, recv, loc):
        x, y, c = _me()
        mine = 2 * x + y
        srcs = [r if layer is None else r.at[layer] for r in ins]
        locs = [pltpu.make_async_copy(srcs[a], outs[a].at[mine], loc.at[a]) for a in range(n)]

        def remote(a, k, slot):
            fx, fy = CHIP_FLIPS[k]
            return pltpu.make_async_remote_copy(srcs[a], outs[a].at[slot], send.at[3 * a + k], recv.at[3 * a + k],
                                                device_id=(_flip(x, fx), _flip(y, fy), c), device_id_type=MESH)

        peers = [2 * _flip(x, fx) + _flip(y, fy) for fx, fy in CHIP_FLIPS]
        return locs, remote, mine, peers

    def start(ins, outs, send, recv, loc):
        locs, remote, mine, _ = copies(ins, outs, send, recv, loc)
        for cp in locs:
            cp.start()
        for a in range(n):
            for k in range(3):
                remote(a, k, mine).start()

    def wait(ins, outs, send, recv, loc):
        locs, remote, _, peers = copies(ins, outs, send, recv, loc)
        for a in range(n):
            for k in range(3):
                cp = remote(a, k, peers[k])
                cp.wait_recv()
                cp.wait_send()
        for cp in locs:
            cp.wait()

    outs = [jax.ShapeDtypeStruct((4,) + tuple(s), a.dtype) for s, a in zip(shapes, arrs)]
    return Plan(list(arrs), outs, 3 * n, n, start, wait)


def all_gather_chips(arrs, layer=None, name="ag4"):
    return run_plan(gather_chips_plan(arrs, layer), name)


def all_gather_devs(arr, name="ag8"):
    def body(in_ref, out_ref, send, recv, loc):
        x, y, c = _me()
        mine = 4 * x + 2 * y + c
        lc = pltpu.make_async_copy(in_ref, out_ref.at[mine], loc.at[0])
        lc.start()

        def remote(k, slot):
            fx, fy, fc = DEV_FLIPS[k]
            return pltpu.make_async_remote_copy(in_ref, out_ref.at[slot], send.at[k], recv.at[k],
                                                device_id=(_flip(x, fx), _flip(y, fy), _flip(c, fc)), device_id_type=MESH)

        for k in range(7):
            remote(k, mine).start()
        for k, (fx, fy, fc) in enumerate(DEV_FLIPS):
            cp = remote(k, 4 * _flip(x, fx) + 2 * _flip(y, fy) + _flip(c, fc))
            cp.wait_recv()
            cp.wait_send()
        lc.wait()

    return _comm_call(body, name, [arr], [jax.ShapeDtypeStruct((8,) + arr.shape, arr.dtype)], 7, 1)[0]


def _rows_of(which, rows):
    return pl.ds(pl.multiple_of(which * rows, 16), rows)


def dev_exchange_plan(parts):
    n = len(parts)

    def copies(ins, outs, send, recv, loc):
        x, y, c = _me()
        mine = 4 * x + 2 * y + c

        def piece(a, px, py, pc):
            rows = ins[a].shape[1] // 2
            return ins[a].at[2 * px + py, _rows_of(pc, rows), :]

        locs = [pltpu.make_async_copy(piece(a, x, y, c), outs[a].at[mine], loc.at[a]) for a in range(n)]

        def remote(a, k, slot):
            fx, fy, fc = DEV_FLIPS[k]
            px, py, pc = _flip(x, fx), _flip(y, fy), _flip(c, fc)
            return pltpu.make_async_remote_copy(piece(a, px, py, pc), outs[a].at[slot], send.at[7 * a + k], recv.at[7 * a + k],
                                                device_id=(px, py, pc), device_id_type=MESH)

        peers = [4 * _flip(x, fx) + 2 * _flip(y, fy) + _flip(c, fc) for fx, fy, fc in DEV_FLIPS]
        return locs, remote, mine, peers

    def start(ins, outs, send, recv, loc):
        locs, remote, mine, _ = copies(ins, outs, send, recv, loc)
        for cp in locs:
            cp.start()
        for a in range(n):
            for k in range(7):
                remote(a, k, mine).start()

    def wait(ins, outs, send, recv, loc):
        locs, remote, _, peers = copies(ins, outs, send, recv, loc)
        for a in range(n):
            for k in range(7):
                cp = remote(a, k, peers[k])
                cp.wait_recv()
                cp.wait_send()
        for cp in locs:
            cp.wait()

    outs = [jax.ShapeDtypeStruct((8, p.shape[1] // 2, p.shape[2]), p.dtype) for p in parts]
    return Plan(list(parts), outs, 7 * n, n, start, wait)


def sum_share(slots, name="rs_sum"):
    n, r, cols = slots.shape
    tr = _tile(r, (128, 176, 64))
    steps = r // tr

    def body(s_ref, g_ref, buf, send, loc, recv):
        i = pl.program_id(0)
        x, y, c = _me()
        slot = i % 2

        def copies(step, sl):
            rows = pl.ds(pl.multiple_of(c * r + step * tr, 8), tr)
            rem = pltpu.make_async_remote_copy(buf.at[sl], g_ref.at[rows, :], send.at[sl], recv.at[0],
                                               device_id=(x, y, 1 - c), device_id_type=MESH)
            return rem, pltpu.make_async_copy(buf.at[sl], g_ref.at[rows, :], loc.at[sl])

        @pl.when(i >= 2)
        def _():
            rem, lc = copies(i - 2, slot)
            rem.wait_send()
            lc.wait()

        acc = s_ref[0].astype(f32)
        for j in range(1, n):
            acc = acc + s_ref[j].astype(f32)
        buf[slot] = acc
        rem, lc = copies(i, slot)
        rem.start()
        lc.start()

        @pl.when(i == steps - 1)
        def _():
            for back in range(min(2, steps)):
                rem, lc = copies(i - back, (i - back) % 2)
                rem.wait_send()
                lc.wait()
            other = g_ref.at[pl.ds(pl.multiple_of((1 - c) * r, 8), r), :]
            pltpu.make_async_remote_copy(other, other, send.at[0], recv.at[0],
                                         device_id=(x, y, 1 - c), device_id_type=MESH).wait_recv()

    return pl.pallas_call(
        body, name=name, grid=(steps,),
        in_specs=[pl.BlockSpec((n, tr, cols), lambda i: (0, i, 0))], out_specs=ANY,
        out_shape=jax.ShapeDtypeStruct((2 * r, cols), f32),
        scratch_shapes=[pltpu.VMEM((2, tr, cols), f32), pltpu.SemaphoreType.DMA((2,)), pltpu.SemaphoreType.DMA((2,)),
                        pltpu.SemaphoreType.DMA((1,))],
        compiler_params=_cp(("arbitrary",)),
    )(slots)


BIG = ("ffn1_w_in", "ffn1_w_out", "w_in", "w_branch_a", "w_branch_b", "w_branch_c", "w_out", "ffn2_w_in", "ffn2_w_out")
ROW_SHARDED = ("ffn1_w_out", "w_out", "ffn2_w_out")
RES_W = (0.5, 1.0, 0.5)


def _full_weight(name, g):
    if name in ROW_SHARDED:
        return g.reshape(4 * g.shape[1], g.shape[2])
    return jnp.concatenate([g[0], g[1], g[2], g[3]], axis=1)


def _by_shard(name, dw):
    if name in ROW_SHARDED:
        return dw.reshape(4, dw.shape[0] // 4, dw.shape[1])
    return dw.reshape(dw.shape[0], 4, dw.shape[1] // 4).transpose(1, 0, 2)


def _full_weight_t(name, g):
    if name in ROW_SHARDED:
        return g.reshape(4 * g.shape[1], g.shape[2]).T
    return g.transpose(0, 2, 1).reshape(4 * g.shape[2], g.shape[1])


def _ffn_fwd(x, w_in, w_out, a_vec, sh_vec, b_vec):
    h, h_t = prenorm(x, a_vec, sh_vec)
    ua, ub, s, s_t = ffn_in_swiglu(h, w_in)
    y = mm(s, w_out, name="ffn_out")
    return postnorm(x, y, b_vec), (x, h_t, ua, ub, s_t, y)


def _ffn_bwd(dout, saved, w_in_t, w_out_t, a_vec, b_vec, plans=None):
    x, h_t, ua, ub, s_t, y = saved
    riders = plans or (None, None, None)
    dy, db = post_bwd(dout, y, b_vec)
    dw_out = mm(s_t, dy, out_dtype=bf16, name="ffn_dwo", side=riders[0])
    du = ffn_du(dy, w_out_t, ua, ub)
    dh = mm(du, w_in_t, name="ffn_dh", side=riders[1])
    sides = []
    if plans:
        (dw_out, s0), (dh, s1) = dw_out, dh
        sides = [s0, s1]
    dw_in = mm(h_t, du, out_dtype=bf16, name="ffn_dwi", side=riders[2])
    if plans:
        dw_in, s2 = dw_in
        sides.append(s2)
    dx, dsh, da = pre_bwd(dout, dh, x, a_vec)
    return (dx, dw_in, dw_out, dsh, da, db) + ((sides,) if plans else ())


def _mix_fwd(x, w, lb, ng, a_vec, sh_vec, b_vec, plans=(None,) * 6):
    h, h_t = prenorm(x, a_vec, sh_vec)
    u = mm(h, w["w_in"], name="mix_in", side=plans[0])
    side0 = None
    if plans[0] is not None:
        u, side0 = u
    (o, ya, st), side1 = hgrn_fwd(u, lb, ng, side=plans[1])
    yb, side2 = sb_fwd(u, side=plans[2])
    groups, dil_sides = [], []
    for g in range(3):
        og, sg = dil_fwd(u, g, side=plans[3 + g])
        groups.append(og)
        dil_sides.append(sg)
    yc, lse = dil_merge([o_ for o_, _ in groups], [l_ for _, l_ in groups])
    pa = mm(ya, w["w_branch_a"], name="mix_pa")
    pb = mm(yb, w["w_branch_b"], name="mix_pb")
    pc = mm(yc, w["w_branch_c"], name="mix_pc")
    merged, merged_t = gate_merge(u, pa, pb, pc)
    z = mm(merged, w["w_out"], name="mix_out")
    return (postnorm(x, z, b_vec), (x, h_t, u, o, ya, st, yb, yc, lse, pa, pb, pc, merged_t, z),
            (side0, side1, side2, *dil_sides))


def _mix_bwd(dout, saved, wt, lb, ng, a_vec, b_vec, plans=(None,) * 3):
    x, h_t, u, o, ya, st, yb, yc, lse, pa, pb, pc, merged_t, z = saved
    dz, db = post_bwd(dout, z, b_vec)
    dmerged = mm(dz, wt["w_out"], name="mix_dm")
    dw_out = mm(merged_t, dz, out_dtype=bf16, name="mix_dwo")
    dpa, dpb, dpc, dg0, dg1, dg2 = gate_bwd(dmerged, u, pa, pb, pc)
    dya = mm(dpa, wt["w_branch_a"], name="mix_dya")
    dyb = mm(dpb, wt["w_branch_b"], name="mix_dyb")
    dyc = mm(dpc, wt["w_branch_c"], name="mix_dyc")
    dw_a = mm(ya.T, dpa, out_dtype=bf16, name="mix_dwa")
    dw_b = mm(yb.astype(bf16).T, dpb, out_dtype=bf16, name="mix_dwb")
    dw_c = mm(yc.astype(bf16).T, dpc, out_dtype=bf16, name="mix_dwc")
    (daq, daf, dai, dag, dlb, dng), side0 = hgrn_bwd(u, lb, ng, o, st, dya, side=plans[0])
    (dbq, dbk, dbv), side1 = sb_bwd(u, yb, dyb, side=plans[1])
    dc = [dil_bwd(u, g, dyc, yc, lse) for g in range(3)]
    du = jnp.concatenate(
        [daq, daf, dai, dag] + [t.astype(bf16) for t in (dbq, dbk, dbv)]
        + [dc[g][j].astype(bf16) for j in range(3) for g in range(3)] + [dg0, dg1, dg2], axis=1)
    dh = mm(du, wt["w_in"], name="mix_dh")
    dw_in = mm(h_t, du, out_dtype=bf16, name="mix_dwi", side=plans[2])
    side2 = None
    if plans[2] is not None:
        dw_in, side2 = dw_in
    dx, dsh, da = pre_bwd(dout, dh, x, a_vec)
    grads = {"w_in": dw_in, "w_out": dw_out, "w_branch_a": dw_a, "w_branch_b": dw_b, "w_branch_c": dw_c}
    return dx, grads, dlb, jnp.sum(dng, axis=0), dsh, da, db, (side0, side1, side2)


FWD_RIDERS = (("ffn1_w_in",), ("ffn2_w_in",), ("w_in",), ("ffn1_w_out", "w_out"), ("ffn2_w_out", "w_branch_a"),
              ("w_branch_b", "w_branch_c"))
BWD_RIDERS = (("ffn1_w_in", "ffn2_w_out", "w_branch_a", "w_branch_b", "w_branch_c"), ("w_in", "ffn1_w_out", "w_out"),
              ("ffn2_w_in",))
LAST_RIDERS = (("ffn2_w_out", "w_out", "w_branch_a", "w_branch_b", "w_branch_c"), ("ffn2_w_in",), ("w_in",))
TAIL = ("ffn1_w_in", "ffn1_w_out")


def _reduce_to_shards(names, grads):
    slots = run_plan(dev_exchange_plan([_by_shard(n, grads[n]) for n in names]), "rs_x8")
    return {n: sum_share(s) for n, s in zip(names, slots)}


def kernel(x, c, w_ada, b_ada, norm_g, ffn1_w_in, ffn1_w_out, w_in, hgrn_lb_logits, hgrn_norm_g, w_branch_a, w_branch_b, w_branch_c, w_out, ffn2_w_in, ffn2_w_out, loss_target, m_w_ada, m_b_ada, m_norm_g, m_ffn1_w_in, m_ffn1_w_out, m_w_in, m_hgrn_lb_logits, m_hgrn_norm_g, m_w_branch_a, m_w_branch_b, m_w_branch_c, m_w_out, m_ffn2_w_in, m_ffn2_w_out, v_w_ada, v_b_ada, v_norm_g, v_ffn1_w_in, v_ffn1_w_out, v_w_in, v_hgrn_lb_logits, v_hgrn_norm_g, v_w_branch_a, v_w_branch_b, v_w_branch_c, v_w_out, v_ffn2_w_in, v_ffn2_w_out):
    weights = dict(w_ada=w_ada, b_ada=b_ada, norm_g=norm_g, ffn1_w_in=ffn1_w_in, ffn1_w_out=ffn1_w_out, w_in=w_in,
                   hgrn_lb_logits=hgrn_lb_logits, hgrn_norm_g=hgrn_norm_g, w_branch_a=w_branch_a, w_branch_b=w_branch_b,
                   w_branch_c=w_branch_c, w_out=w_out, ffn2_w_in=ffn2_w_in, ffn2_w_out=ffn2_w_out)
    mom = dict(w_ada=m_w_ada, b_ada=m_b_ada, norm_g=m_norm_g, ffn1_w_in=m_ffn1_w_in, ffn1_w_out=m_ffn1_w_out, w_in=m_w_in,
               hgrn_lb_logits=m_hgrn_lb_logits, hgrn_norm_g=m_hgrn_norm_g, w_branch_a=m_w_branch_a, w_branch_b=m_w_branch_b,
               w_branch_c=m_w_branch_c, w_out=m_w_out, ffn2_w_in=m_ffn2_w_in, ffn2_w_out=m_ffn2_w_out)
    var = dict(w_ada=v_w_ada, b_ada=v_b_ada, norm_g=v_norm_g, ffn1_w_in=v_ffn1_w_in, ffn1_w_out=v_ffn1_w_out, w_in=v_w_in,
               hgrn_lb_logits=v_hgrn_lb_logits, hgrn_norm_g=v_hgrn_norm_g, w_branch_a=v_w_branch_a, w_branch_b=v_w_branch_b,
               w_branch_c=v_w_branch_c, w_out=v_w_out, ffn2_w_in=v_ffn2_w_in, ffn2_w_out=v_ffn2_w_out)
    order = list(weights)
    xi, yi, ci = _me()
    chip = 2 * xi + yi
    dev = 4 * xi + 2 * yi + ci
    xs = x[0]

    c_all = all_gather_devs(c, name="ag8_c").reshape(8, D)
    mod_sh = all_gather_chips([ada_fwd(c_all, w_ada)], name="ag4_mod")[0]
    mod_all = mod_sh.transpose(1, 2, 0, 3).reshape(DEPTH, 8, 9 * D)
    mod = lax.dynamic_index_in_dim(mod_all, dev, axis=1, keepdims=False) + b_ada
    mod = mod.reshape(DEPTH, 3, 3, D)
    ng_all = all_gather_chips([norm_g.reshape(DEPTH * 6, D // 4)], name="ag4_norm")[0]
    ng_all = ng_all.reshape(4, DEPTH, 6, D // 4).transpose(1, 2, 0, 3).reshape(DEPTH, 6, D)
    lb_all = lb_fwd(hgrn_lb_logits)
    w16 = {n: cast_bf16(weights[n]) for n in BIG}

    def vecs(l, i):
        shift, scale, gate = mod[l, i, 0][None], mod[l, i, 1][None], mod[l, i, 2][None]
        g_pre, g_post = ng_all[l, 2 * i][None], ng_all[l, 2 * i + 1][None]
        return g_pre * (1.0 + scale), shift, RES_W[i] * gate * g_post

    saved, full = [], []
    gathered = dict(zip(BIG, all_gather_chips([w16[n] for n in BIG], layer=0, name="ag4_w0")))
    for l in range(DEPTH):
        w = {n: _full_weight(n, gathered[n]) for n in BIG}
        full.append({n: _full_weight_t(n, gathered[n]) for n in BIG})
        lb, ng = lb_all[l][None], hgrn_norm_g[l][None]
        plans = (None,) * len(FWD_RIDERS)
        if l + 1 < DEPTH:
            plans = tuple(gather_chips_plan([w16[n] for n in names], layer=l + 1) for names in FWD_RIDERS)
        xs, s1 = _ffn_fwd(xs, w["ffn1_w_in"], w["ffn1_w_out"], *vecs(l, 0))
        xs, s2, sides = _mix_fwd(xs, w, lb, ng, *vecs(l, 1), plans=plans)
        xs, s3 = _ffn_fwd(xs, w["ffn2_w_in"], w["ffn2_w_out"], *vecs(l, 2))
        saved.append((s1, s2, s3))
        if l + 1 < DEPTH:
            gathered = {n: g for names, outs in zip(FWD_RIDERS, sides) for n, g in zip(names, outs)}

    dx, loss_part = loss_grad(xs, loss_target[0])
    loss = lax.psum(loss_part[0, 0], ("x", "y", "c"))

    big_grads = {n: [None] * DEPTH for n in BIG}
    d_mod, d_ng, d_lb, d_hng = [None] * DEPTH, [None] * DEPTH, [None] * DEPTH, [None] * DEPTH
    pending = None
    for l in reversed(range(DEPTH)):
        wt = full[l]
        s1, s2, s3 = saved[l]
        lb, ng = lb_all[l][None], hgrn_norm_g[l][None]
        rows_mod, rows_ng = [None] * 9, [None] * 6

        def vec_grads(i, dsh, da, db):
            scale, gate = mod[l, i, 1][None], mod[l, i, 2][None]
            g_pre, g_post = ng_all[l, 2 * i][None], ng_all[l, 2 * i + 1][None]
            rows_mod[3 * i], rows_mod[3 * i + 1], rows_mod[3 * i + 2] = dsh, g_pre * da, RES_W[i] * g_post * db
            rows_ng[2 * i], rows_ng[2 * i + 1] = (1.0 + scale) * da, RES_W[i] * gate * db

        a3, _, b3 = vecs(l, 2)
        dx, dwi, dwo, dsh, da, db = _ffn_bwd(dx, s3, wt["ffn2_w_in"], wt["ffn2_w_out"], a3, b3)
        vec_grads(2, dsh, da, db)
        grads = {"ffn2_w_in": dwi, "ffn2_w_out": dwo}
        a2, _, b2 = vecs(l, 1)
        plans = (None,) * len(BWD_RIDERS)
        if pending is not None:
            plans = tuple(dev_exchange_plan([pending[n] for n in names]) for names in BWD_RIDERS)
        dx, gmix, dlb, dhng, dsh, da, db, sides = _mix_bwd(dx, s2, wt, lb, ng, a2, b2, plans=plans)
        if pending is not None:
            for names, outs in zip(BWD_RIDERS, sides):
                for n, slots in zip(names, outs):
                    big_grads[n][l + 1] = sum_share(slots)
        vec_grads(1, dsh, da, db)
        grads.update(gmix)
        a1, _, b1 = vecs(l, 0)
        if l > 0:
            dx, dwi, dwo, dsh, da, db = _ffn_bwd(dx, s1, wt["ffn1_w_in"], wt["ffn1_w_out"], a1, b1)
        else:
            ready = {n: _by_shard(n, grads[n]) for names in LAST_RIDERS for n in names}
            plans = tuple(dev_exchange_plan([ready[n] for n in names]) for names in LAST_RIDERS)
            dx, dwi, dwo, dsh, da, db, sides = _ffn_bwd(dx, s1, wt["ffn1_w_in"], wt["ffn1_w_out"], a1, b1, plans=plans)
            for names, outs in zip(LAST_RIDERS, sides):
                for n, slots in zip(names, outs):
                    big_grads[n][0] = sum_share(slots)
        vec_grads(0, dsh, da, db)
        grads.update({"ffn1_w_in": dwi, "ffn1_w_out": dwo})
        pending = {n: _by_shard(n, grads[n]) for n in (BIG if l > 0 else TAIL)}
        d_mod[l] = jnp.concatenate(rows_mod, axis=1)
        d_ng[l] = jnp.concatenate(rows_ng, axis=0)
        d_lb[l], d_hng[l] = dlb, dhng

    n_small = 6 * D * DEPTH + 768 * DEPTH + A_V * DEPTH + 9 * D * DEPTH
    pad = -n_small % (512 * LANE)
    flat = jnp.concatenate([jnp.stack(d_ng).reshape(-1), jnp.concatenate(d_lb, axis=0).reshape(-1),
                            jnp.concatenate(d_hng, axis=0).reshape(-1), jnp.concatenate(d_mod, axis=0).reshape(-1),
                            jnp.zeros((pad,), f32)])
    small_all = all_gather_devs(flat.reshape(-1, LANE), name="ag8_small")
    total = sum_slots(small_all).reshape(-1)
    o1 = 6 * D * DEPTH
    o2 = o1 + 768 * DEPTH
    o3 = o2 + A_V * DEPTH
    g_ng_full = total[:o1].reshape(DEPTH, 6, D)
    g_lb_all = total[o1:o2].reshape(DEPTH, 768)
    g_small = {
        "norm_g": lax.dynamic_slice_in_dim(g_ng_full, chip * (D // 4), D // 4, axis=2),
        "hgrn_lb_logits": lb_bwd(hgrn_lb_logits, g_lb_all),
        "hgrn_norm_g": total[o2:o3].reshape(DEPTH, A_V),
        "b_ada": total[o3:n_small].reshape(DEPTH, 9 * D),
    }
    dmod_all = small_all.reshape(8, -1)[:, o3:n_small].reshape(8, DEPTH, 9 * D).transpose(1, 0, 2)
    dm_sh = lax.dynamic_slice_in_dim(dmod_all, chip * ADA_N, ADA_N, axis=2)

    out_g, out_d, out_m, out_v = {}, {}, {}, {}
    ada_outs, slots = ada_bwd_adam(c_all, dm_sh, w_ada, m_w_ada, v_w_ada, side=dev_exchange_plan([pending[n] for n in TAIL]))
    out_g["w_ada"], out_d["w_ada"], out_m["w_ada"], out_v["w_ada"] = ada_outs
    for n, s in zip(TAIL, slots):
        big_grads[n][0] = sum_share(s)
    for n in BIG:
        out_g[n], out_d[n], out_m[n], out_v[n] = adam_layers(weights[n], big_grads[n], mom[n], var[n])
    out_g.update(g_small)
    for n in g_small:
        out_d[n], out_m[n], out_v[n] = adam(weights[n], out_g[n], mom[n], var[n])
    return (loss, dx[None], *[out_g[n] for n in order], *[out_d[n] for n in order],
            *[out_m[n] for n in order], *[out_v[n] for n in order])
```

```python
import functools
import math

import jax
import jax.numpy as jnp
from jax import lax
from jax.experimental import pallas as pl
from jax.experimental.pallas import tpu as pltpu

f32, bf16 = jnp.float32, jnp.bfloat16

D = 1024
DEPTH = 4
D_FF = 2816
EPS = 1e-6
NEG_BIG = -1e30
TINY = 1e-30
A_HEADS, A_K, A_V, A_CHUNK = 6, 128, 64, 64
A_SUB = 16
A_CLAMP = 80.0
B_HEADS, HD = 6, 64
C_GROUPS = ((128, 1), (512, 4), (2048, 16))
C_BLK = 128
IN_COLS = 8832
O_AQ, O_AF, O_AI, O_AG = 0, 768, 1536, 1920
O_BQ, O_BK, O_BV = 2304, 2688, 3072
O_CQ, O_CK, O_CV = 3456, 4224, 4992
O_GATE = 5760
LANE = 128
ADAM_LR, ADAM_B1, ADAM_B2, ADAM_EPS, ADAM_WD, ADAM_STEP = 0.001, 0.9, 0.999, 1e-08, 0.01, 10
MESH = pl.DeviceIdType.MESH
VMEM_LIMIT = 56 * 1024 * 1024


def _alibi_slopes(n):
    def pow2(m):
        start = 2.0 ** (-8.0 / m)
        return [start ** (i + 1) for i in range(m)]
    if math.log2(n).is_integer():
        s = pow2(n)
    else:
        c = 2 ** int(math.floor(math.log2(n)))
        s = pow2(c) + pow2(2 * c)[0::2][: n - c]
    return sorted(s, reverse=True)


C_SLOPES = _alibi_slopes(12)


def _tile(n, prefs):
    for p in prefs:
        if n % p == 0:
            return p
    return n


def _cp(sem):
    return pltpu.CompilerParams(dimension_semantics=sem, vmem_limit_bytes=VMEM_LIMIT)


def _sig(x):
    return 1.0 / (1.0 + jnp.exp(-x))


def _dot(a, b, dn, precision=None):
    return lax.dot_general(a, b, (dn, ((), ())), preferred_element_type=f32, precision=precision)


NN = ((1,), (0,))
NT = ((1,), (1,))
TN = ((0,), (0,))


class Plan:
    def __init__(self, ins, outs, n_remote, n_local, start, wait):
        self.ins, self.outs, self.n_remote, self.n_local, self.start, self.wait = ins, outs, n_remote, n_local, start, wait

    def sems(self):
        return [pltpu.SemaphoreType.DMA((self.n_remote,)), pltpu.SemaphoreType.DMA((self.n_remote,)),
                pltpu.SemaphoreType.DMA((max(self.n_local, 1),))]


def _call(body, *, name, grid, in_specs, out_specs, out_shape, sem, args, scratch_shapes=(), side=None):
    if side is None:
        return pl.pallas_call(body, name=name, grid=grid, in_specs=in_specs, out_specs=out_specs, out_shape=out_shape,
                              scratch_shapes=list(scratch_shapes), compiler_params=_cp(sem))(*args), None
    any_spec = pl.BlockSpec(memory_space=pl.ANY)
    n_in, n_out, n_scr = len(in_specs), len(out_specs), len(scratch_shapes)
    s_in, s_out = len(side.ins), len(side.outs)

    def hosted(*refs):
        ins, rest = refs[:n_in], refs[n_in:]
        sins, rest = rest[:s_in], rest[s_in:]
        outs, rest = rest[:n_out], rest[n_out:]
        souts, rest = rest[:s_out], rest[s_out:]
        scr, sems = rest[:n_scr], rest[n_scr:]
        pids = [pl.program_id(d) for d in range(len(grid))]
        first = functools.reduce(jnp.logical_and, [p == 0 for p in pids])
        last = functools.reduce(jnp.logical_and, [p == g - 1 for p, g in zip(pids, grid)])

        @pl.when(first)
        def _():
            side.start(sins, souts, *sems)

        body(*ins, *outs, *scr)

        @pl.when(last)
        def _():
            side.wait(sins, souts, *sems)

    res = pl.pallas_call(
        hosted, name=name, grid=grid, in_specs=list(in_specs) + [any_spec] * s_in,
        out_specs=list(out_specs) + [any_spec] * s_out, out_shape=list(out_shape) + list(side.outs),
        scratch_shapes=list(scratch_shapes) + side.sems(), compiler_params=_cp(("arbitrary",) * len(grid)),
    )(*args, *side.ins)
    return res[:n_out], res[n_out:]


MM_TILES = {
    (4096, 1024, 2816): (1024, 512, 2816),
    (4096, 1024, 5632): (512, 512, 5632),
    (1024, 5632, 4096): (512, 512, 4096),
    (2816, 1024, 4096): (704, 512, 4096),
    (4096, 8832, 1024): (512, 2944, 1024),
    (4096, 1024, 8832): (1024, 512, 2944),
    (1024, 8832, 4096): (512, 2944, 1024),
    (1024, 1024, 4096): (512, 512, 4096),
    (384, 1024, 4096): (384, 512, 4096),
    (256, 1024, 4096): (256, 512, 4096),
}


def mm(a, b, *, out_dtype=f32, name="mm", side=None):
    M, K = a.shape
    K2, N = b.shape
    assert K == K2, (a.shape, b.shape)
    tm, tn, tk = MM_TILES.get((M, N, K), (_tile(M, (1024, 704, 512, 384, 256, 128)), _tile(N, (512, 384, 256, 128)),
                                          _tile(K, (1024, 512, 1408, 384, 256, 128))))
    nk = K // tk

    def body(a_ref, b_ref, o_ref, *acc):
        p = _dot(a_ref[...].astype(bf16), b_ref[...].astype(bf16), NN)
        if nk == 1:
            o_ref[...] = p.astype(out_dtype)
            return
        acc_ref, = acc
        k = pl.program_id(2)

        @pl.when(k == 0)
        def _():
            acc_ref[...] = p

        @pl.when(k > 0)
        def _():
            acc_ref[...] += p

        @pl.when(k == nk - 1)
        def _():
            o_ref[...] = acc_ref[...].astype(out_dtype)

    outs, souts = _call(
        body, name=name, grid=(M // tm, N // tn, nk),
        in_specs=[pl.BlockSpec((tm, tk), lambda i, j, k: (i, k)), pl.BlockSpec((tk, tn), lambda i, j, k: (k, j))],
        out_specs=[pl.BlockSpec((tm, tn), lambda i, j, k: (i, j))],
        out_shape=[jax.ShapeDtypeStruct((M, N), out_dtype)],
        scratch_shapes=[pltpu.VMEM((tm, tn), f32)] if nk > 1 else [],
        sem=("parallel", "parallel", "arbitrary"), args=(a, b), side=side)
    return outs[0] if side is None else (outs[0], souts)


FF_TM = 1024
FF_T = 256
FF_NB = D_FF // FF_T


def ffn_in_swiglu(h, w_in):
    S = h.shape[0]

    def body(h_ref, wa_ref, wb_ref, a_ref, b_ref, s_ref, st_ref):
        hv = h_ref[...]
        a = _dot(hv, wa_ref[...], NN)
        b = _dot(hv, wb_ref[...], NN)
        a_ref[...] = a.astype(bf16)
        b_ref[...] = b.astype(bf16)
        s = (a * _sig(a) * b).astype(bf16)
        s_ref[...] = s
        st_ref[...] = s.T

    ospec = pl.BlockSpec((FF_TM, FF_T), lambda i, j: (i, j))
    osd = jax.ShapeDtypeStruct((S, D_FF), bf16)
    return pl.pallas_call(
        body, name="ffn_in", grid=(S // FF_TM, FF_NB),
        in_specs=[pl.BlockSpec((FF_TM, D), lambda i, j: (i, 0)), pl.BlockSpec((D, FF_T), lambda i, j: (0, j)),
                  pl.BlockSpec((D, FF_T), lambda i, j: (0, j + FF_NB))],
        out_specs=[ospec] * 3 + [pl.BlockSpec((FF_T, FF_TM), lambda i, j: (j, i))],
        out_shape=[osd] * 3 + [jax.ShapeDtypeStruct((D_FF, S), bf16)], compiler_params=_cp(("parallel", "parallel")),
    )(h, w_in, w_in)


def ffn_du(dy, w_out_t, ua, ub):
    S = dy.shape[0]
    tm = 512

    def body(dy_ref, w_ref, a_ref, b_ref, du_ref):
        dyv = dy_ref[...]
        for j in range(FF_NB):
            cols = slice(j * FF_T, (j + 1) * FF_T)
            ds = _dot(dyv, w_ref[:, cols], NN)
            a, b = a_ref[:, cols].astype(f32), b_ref[:, cols].astype(f32)
            sg = _sig(a)
            du_ref[:, cols] = (ds * b * sg * (1.0 + a * (1.0 - sg))).astype(bf16)
            du_ref[:, D_FF + j * FF_T:D_FF + (j + 1) * FF_T] = (ds * a * sg).astype(bf16)

    half = pl.BlockSpec((tm, D_FF), lambda i: (i, 0))
    return pl.pallas_call(
        body, name="ffn_du", grid=(S // tm,),
        in_specs=[pl.BlockSpec((tm, D), lambda i: (i, 0)), pl.BlockSpec((D, D_FF), lambda i: (0, 0)), half, half],
        out_specs=pl.BlockSpec((tm, 2 * D_FF), lambda i: (i, 0)),
        out_shape=jax.ShapeDtypeStruct((S, 2 * D_FF), bf16), compiler_params=_cp(("parallel",)),
    )(dy, w_out_t, ua, ub)


TR = 512


def _row_spec(cols=D):
    return pl.BlockSpec((TR, cols), lambda i: (i, 0))


def _vec_spec(cols=D):
    return pl.BlockSpec((1, cols), lambda i: (0, 0))


def prenorm(x, a_vec, sh_vec):
    S = x.shape[0]

    def body(x_ref, a_ref, s_ref, h_ref, ht_ref):
        xv = x_ref[...]
        rstd = lax.rsqrt(jnp.mean(xv * xv, axis=1, keepdims=True) + EPS)
        h = (xv * rstd * a_ref[...] + s_ref[...]).astype(bf16)
        h_ref[...] = h
        ht_ref[...] = h.T

    return pl.pallas_call(
        body, name="prenorm", grid=(S // TR,),
        in_specs=[_row_spec(), _vec_spec(), _vec_spec()],
        out_specs=[_row_spec(), pl.BlockSpec((D, TR), lambda i: (0, i))],
        out_shape=[jax.ShapeDtypeStruct((S, D), bf16), jax.ShapeDtypeStruct((D, S), bf16)],
        compiler_params=_cp(("parallel",)),
    )(x, a_vec, sh_vec)


def postnorm(x, y, b_vec):
    S = x.shape[0]

    def body(x_ref, y_ref, b_ref, o_ref):
        yv = y_ref[...]
        rstd = lax.rsqrt(jnp.mean(yv * yv, axis=1, keepdims=True) + EPS)
        o_ref[...] = x_ref[...] + b_ref[...] * (yv * rstd)

    return pl.pallas_call(
        body, name="postnorm", grid=(S // TR,),
        in_specs=[_row_spec(), _row_spec(), _vec_spec()], out_specs=_row_spec(),
        out_shape=jax.ShapeDtypeStruct((S, D), f32), compiler_params=_cp(("parallel",)),
    )(x, y, b_vec)


def post_bwd(dout, y, b_vec):
    S = dout.shape[0]

    def body(d_ref, y_ref, b_ref, dy_ref, db_ref):
        i = pl.program_id(0)
        yv, dv = y_ref[...], d_ref[...]
        rstd = lax.rsqrt(jnp.mean(yv * yv, axis=1, keepdims=True) + EPS)
        yh = yv * rstd
        dyh = dv * b_ref[...]
        dy_ref[...] = (rstd * (dyh - yh * jnp.mean(dyh * yh, axis=1, keepdims=True))).astype(bf16)
        part = jnp.sum(dv * yh, axis=0, keepdims=True)

        @pl.when(i == 0)
        def _():
            db_ref[...] = part

        @pl.when(i > 0)
        def _():
            db_ref[...] += part

    return pl.pallas_call(
        body, name="post_bwd", grid=(S // TR,),
        in_specs=[_row_spec(), _row_spec(), _vec_spec()], out_specs=[_row_spec(), _vec_spec()],
        out_shape=[jax.ShapeDtypeStruct((S, D), bf16), jax.ShapeDtypeStruct((1, D), f32)],
        compiler_params=_cp(("arbitrary",)),
    )(dout, y, b_vec)


def pre_bwd(dout, dh, x, a_vec):
    S = dout.shape[0]

    def body(d_ref, dh_ref, x_ref, a_ref, dx_ref, ds_ref, da_ref):
        i = pl.program_id(0)
        xv, dhv = x_ref[...], dh_ref[...]
        rstd = lax.rsqrt(jnp.mean(xv * xv, axis=1, keepdims=True) + EPS)
        n1 = xv * rstd
        dn = dhv * a_ref[...]
        dx_ref[...] = d_ref[...] + rstd * (dn - n1 * jnp.mean(dn * n1, axis=1, keepdims=True))
        p_s = jnp.sum(dhv, axis=0, keepdims=True)
        p_a = jnp.sum(dhv * n1, axis=0, keepdims=True)

        @pl.when(i == 0)
        def _():
            ds_ref[...] = p_s
            da_ref[...] = p_a

        @pl.when(i > 0)
        def _():
            ds_ref[...] += p_s
            da_ref[...] += p_a

    return pl.pallas_call(
        body, name="pre_bwd", grid=(S // TR,),
        in_specs=[_row_spec(), _row_spec(), _row_spec(), _vec_spec()],
        out_specs=[_row_spec(), _vec_spec(), _vec_spec()],
        out_shape=[jax.ShapeDtypeStruct((S, D), f32), jax.ShapeDtypeStruct((1, D), f32), jax.ShapeDtypeStruct((1, D), f32)],
        compiler_params=_cp(("arbitrary",)),
    )(dout, dh, x, a_vec)


def loss_grad(y, tgt):
    S = y.shape[0]

    def body(y_ref, t_ref, dy_ref, l_ref):
        i = pl.program_id(0)
        e = y_ref[...] - t_ref[...]
        dy_ref[...] = e * (1.0 / D)
        part = jnp.sum(jnp.sum(e * e, axis=1, keepdims=True), axis=0, keepdims=True) * (0.5 / D)
        part = jnp.broadcast_to(part, (8, LANE))

        @pl.when(i == 0)
        def _():
            l_ref[...] = part

        @pl.when(i > 0)
        def _():
            l_ref[...] += part

    return pl.pallas_call(
        body, name="loss_grad", grid=(S // TR,),
        in_specs=[_row_spec(), _row_spec()],
        out_specs=[_row_spec(), pl.BlockSpec((8, LANE), lambda i: (0, 0))],
        out_shape=[jax.ShapeDtypeStruct((S, D), f32), jax.ShapeDtypeStruct((8, LANE), f32)],
        compiler_params=_cp(("arbitrary",)),
    )(y, tgt)


G_NB = D // LANE
G_TR = 2048
G_OFF = O_GATE // LANE


def gate_merge(u, pa, pb, pc):
    S = u.shape[0]

    def body(g0, g1, g2, a, b, c, o_ref, ot_ref):
        m = (_sig(g0[...]) * a[...] + _sig(g1[...]) * b[...] + _sig(g2[...]) * c[...]).astype(bf16)
        o_ref[...] = m
        ot_ref[...] = m.T

    gs = [pl.BlockSpec((G_TR, LANE), functools.partial(lambda i, j, k: (i, G_OFF + G_NB * k + j), k=k)) for k in range(3)]
    ps = pl.BlockSpec((G_TR, LANE), lambda i, j: (i, j))
    return pl.pallas_call(
        body, name="gate_merge", grid=(S // G_TR, G_NB),
        in_specs=gs + [ps, ps, ps], out_specs=[ps, pl.BlockSpec((LANE, G_TR), lambda i, j: (j, i))],
        out_shape=[jax.ShapeDtypeStruct((S, D), bf16), jax.ShapeDtypeStruct((D, S), bf16)],
        compiler_params=_cp(("parallel", "parallel")),
    )(u, u, u, pa, pb, pc)


def gate_bwd(dm, u, pa, pb, pc):
    S = u.shape[0]

    def body(dm_ref, g0, g1, g2, a, b, c, da, db, dc, dg0, dg1, dg2):
        d = dm_ref[...]
        for g, p, dp, dg in ((g0, a, da, dg0), (g1, b, db, dg1), (g2, c, dc, dg2)):
            s = _sig(g[...])
            dp[...] = (d * s).astype(bf16)
            dg[...] = (d * p[...] * s * (1.0 - s)).astype(bf16)

    gs = [pl.BlockSpec((G_TR, LANE), functools.partial(lambda i, j, k: (i, G_OFF + G_NB * k + j), k=k)) for k in range(3)]
    ps = pl.BlockSpec((G_TR, LANE), lambda i, j: (i, j))
    osd = jax.ShapeDtypeStruct((S, D), bf16)
    return pl.pallas_call(
        body, name="gate_bwd", grid=(S // G_TR, G_NB),
        in_specs=[ps] + gs + [ps, ps, ps], out_specs=[ps] * 6, out_shape=[osd] * 6,
        compiler_params=_cp(("parallel", "parallel")),
    )(dm, u, u, u, pa, pb, pc)


A_TB = 512
A_NCH = A_TB // A_CHUNK
A_NSUB = A_CHUNK // A_SUB
A_HP = 6
A_KW, A_VW = A_HP * A_K, A_HP * A_V


def _hgrn_gates(qr, fr, lbh):
    sq = _sig(qr)
    sig = _sig(fr)
    f = lbh + (1.0 - lbh) * sig
    logf = jnp.log(jnp.maximum(f, TINY))
    k = (1.0 - lbh) * (1.0 - sig)
    return qr * sq, sq, sig, f, logf, k


def _hgrn_intra(qf, k, b, causal):
    qts, kts, eqs, eks, blocks = [], [], [], [], []
    for sb in range(A_NSUB):
        rs = sb * A_SUB
        r = b[rs - 1:rs, :] if sb else jnp.zeros((1, A_K), f32)
        eq = jnp.exp(b[rs:rs + A_SUB, :] - r)
        ek = jnp.exp(jnp.minimum(r - b, A_CLAMP))
        qt = (qf[rs:rs + A_SUB, :] * eq).astype(bf16)
        kt = (k * ek).astype(bf16)
        blocks.append(_dot(qt, kt, NT))
        qts.append(qt), kts.append(kt), eqs.append(eq), eks.append(ek)
    a = jnp.where(causal, jnp.concatenate(blocks, axis=0), 0.0)
    return a, qts, kts, eqs, eks


def _tri():
    r = lax.broadcasted_iota(jnp.int32, (A_CHUNK, A_CHUNK), 0)
    c = lax.broadcasted_iota(jnp.int32, (A_CHUNK, A_CHUNK), 1)
    return r >= c


def _hgrn_in_specs(rev_nb=None):
    def im(col):
        if rev_nb is None:
            return lambda p, i: (i, col + p)
        return lambda p, i: (rev_nb - 1 - i, col + p)
    return [pl.BlockSpec((A_TB, A_KW), im(O_AQ // A_KW)), pl.BlockSpec((A_TB, A_KW), im(O_AF // A_KW)),
            pl.BlockSpec((A_TB, A_VW), im(O_AI // A_VW)), pl.BlockSpec((A_TB, A_VW), im(O_AG // A_VW)),
            pl.BlockSpec((1, A_KW), lambda p, i: (0, p)), pl.BlockSpec((1, A_V), lambda p, i: (0, 0))]


def hgrn_fwd(u, lb, ng, side=None):
    S = u.shape[0]
    nb = S // A_TB

    def body(q_ref, f_ref, i_ref, g_ref, lb_ref, ng_ref, o_ref, ya_ref, st_ref, state):
        @pl.when(pl.program_id(1) == 0)
        def _():
            state[...] = jnp.zeros_like(state)

        causal = _tri()
        tri = causal.astype(f32)

        def chunk(n, carry):
            rows = pl.ds(pl.multiple_of(n * A_CHUNK, A_CHUNK), A_CHUNK)
            o_parts, y_parts = [], []
            for hh in range(A_HP):
                ks = slice(hh * A_K, (hh + 1) * A_K)
                vs = slice(hh * A_V, (hh + 1) * A_V)
                qf, _, _, _, logf, k = _hgrn_gates(q_ref[rows, ks], f_ref[rows, ks], lb_ref[:, ks])
                vi = i_ref[rows, vs].astype(bf16)
                gg = g_ref[rows, vs]
                b = _dot(tri, logf, NN, precision=lax.Precision.HIGHEST)
                s0 = state[hh]
                st_ref[n, hh] = s0
                o = _dot((qf * jnp.exp(b)).astype(bf16), s0.astype(bf16), NT)
                a, _, _, _, _ = _hgrn_intra(qf, k, b, causal)
                o = o + _dot(a.astype(bf16), vi, NN)
                bend = b[A_CHUNK - 1:A_CHUNK, :]
                ke = (k * jnp.exp(bend - b)).astype(bf16)
                state[hh] = s0 * jnp.exp(bend) + _dot(vi, ke, TN)
                rstd = lax.rsqrt(jnp.mean(o * o, axis=1, keepdims=True) + EPS)
                o_parts.append(o)
                y_parts.append(o * rstd * ng_ref[...] * (gg * _sig(gg)))
            o_ref[rows, :] = jnp.concatenate(o_parts, axis=1)
            ya_ref[rows, :] = jnp.concatenate(y_parts, axis=1).astype(bf16)
            return carry

        lax.fori_loop(0, A_NCH, chunk, 0)

    return _call(
        body, name="hgrn_fwd", grid=(A_HEADS // A_HP, nb),
        in_specs=_hgrn_in_specs(),
        out_specs=[pl.BlockSpec((A_TB, A_VW), lambda p, i: (i, p)), pl.BlockSpec((A_TB, A_VW), lambda p, i: (i, p)),
                   pl.BlockSpec((A_NCH, A_HP, A_V, A_K), lambda p, i: (i, p, 0, 0))],
        out_shape=[jax.ShapeDtypeStruct((S, 384), f32), jax.ShapeDtypeStruct((S, 384), bf16),
                   jax.ShapeDtypeStruct((S // A_CHUNK, A_HEADS, A_V, A_K), f32)],
        scratch_shapes=[pltpu.VMEM((A_HP, A_V, A_K), f32)],
        sem=("parallel", "arbitrary"), args=(u, u, u, u, lb, ng), side=side)


def hgrn_bwd(u, lb, ng, o, st, dya, side=None):
    S = u.shape[0]
    nb = S // A_TB

    def body(q_ref, f_ref, i_ref, g_ref, lb_ref, ng_ref, o_ref, st_ref, dy_ref,
             dq_ref, df_ref, di_ref, dg_ref, dlb_ref, dng_ref, dstate):
        @pl.when(pl.program_id(1) == 0)
        def _():
            dstate[...] = jnp.zeros_like(dstate)
            dlb_ref[...] = jnp.zeros_like(dlb_ref)
            dng_ref[...] = jnp.zeros_like(dng_ref)

        causal = _tri()
        tri = causal.astype(f32)

        def chunk(it, carry):
            n = A_NCH - 1 - it
            rows = pl.ds(pl.multiple_of(n * A_CHUNK, A_CHUNK), A_CHUNK)
            dq_p, df_p, di_p, dg_p, dlb_p = [], [], [], [], []
            dng_acc = jnp.zeros((1, A_V), f32)
            for hh in range(A_HP):
                ks = slice(hh * A_K, (hh + 1) * A_K)
                vs = slice(hh * A_V, (hh + 1) * A_V)
                lbh = lb_ref[:, ks]
                qr = q_ref[rows, ks]
                qf, sq, sig, f, logf, k = _hgrn_gates(qr, f_ref[rows, ks], lbh)
                vi = i_ref[rows, vs].astype(bf16)
                gg = g_ref[rows, vs]
                b = _dot(tri, logf, NN, precision=lax.Precision.HIGHEST)
                eb = jnp.exp(b)
                bend = b[A_CHUNK - 1:A_CHUNK, :]
                eend = jnp.exp(bend)
                ekend = jnp.exp(bend - b)
                qe = (qf * eb).astype(bf16)
                ke = (k * ekend).astype(bf16)
                s0 = st_ref[n, hh]
                dsend = dstate[hh]
                ov = o_ref[rows, vs]
                dy = dy_ref[rows, vs]
                rstd = lax.rsqrt(jnp.mean(ov * ov, axis=1, keepdims=True) + EPS)
                oh = ov * rstd
                sg = _sig(gg)
                d_on = dy * (gg * sg)
                dg_p.append(dy * oh * ng_ref[...] * (sg * (1.0 + gg * (1.0 - sg))))
                dng_acc = dng_acc + jnp.sum(d_on * oh, axis=0, keepdims=True)
                doh = d_on * ng_ref[...]
                do = (rstd * (doh - oh * jnp.mean(doh * oh, axis=1, keepdims=True))).astype(bf16)
                a, qts, kts, eqs, eks = _hgrn_intra(qf, k, b, causal)
                da = jnp.where(causal, _dot(do, vi, NT), 0.0).astype(bf16)
                dsb = dsend.astype(bf16)
                dv = _dot(a.astype(bf16), do, TN) + _dot(ke, dsb, NT)
                dq = _dot(do, s0.astype(bf16), NN) * eb
                dk_state = _dot(vi, dsb, NN) * ekend
                dk = dk_state
                dq_i = []
                for sb in range(A_NSUB):
                    da_sb = da[sb * A_SUB:(sb + 1) * A_SUB, :]
                    dq_i.append(_dot(da_sb, kts[sb], NN) * eqs[sb])
                    dk = dk + _dot(da_sb, qts[sb], TN) * eks[sb]
                dq = dq + jnp.concatenate(dq_i, axis=0)
                db = qf * dq - k * dk
                extra = jnp.sum(k * dk_state, axis=0, keepdims=True) + eend * jnp.sum(s0 * dsend, axis=0, keepdims=True)
                dlogf = _dot(tri, db, TN, precision=lax.Precision.HIGHEST) + extra
                dstate[hh] = _dot(do, qe, TN) + eend * dsend
                d_pre = jnp.where(f > TINY, dlogf / f, 0.0) - dk
                dlb_p.append(jnp.sum((1.0 - sig) * d_pre, axis=0, keepdims=True))
                df_p.append((1.0 - lbh) * d_pre * sig * (1.0 - sig))
                dq_p.append(dq * (sq * (1.0 + qr * (1.0 - sq))))
                di_p.append(dv)
            dq_ref[rows, :] = jnp.concatenate(dq_p, axis=1).astype(bf16)
            df_ref[rows, :] = jnp.concatenate(df_p, axis=1).astype(bf16)
            di_ref[rows, :] = jnp.concatenate(di_p, axis=1).astype(bf16)
            dg_ref[rows, :] = jnp.concatenate(dg_p, axis=1).astype(bf16)
            dlb_ref[...] += jnp.concatenate(dlb_p, axis=1)
            dng_ref[0] += dng_acc
            return carry

        lax.fori_loop(0, A_NCH, chunk, 0)

    rev = lambda p, i: (nb - 1 - i, p)
    return _call(
        body, name="hgrn_bwd", grid=(A_HEADS // A_HP, nb),
        in_specs=_hgrn_in_specs(nb) + [pl.BlockSpec((A_TB, A_VW), rev),
                                       pl.BlockSpec((A_NCH, A_HP, A_V, A_K), lambda p, i: (nb - 1 - i, p, 0, 0)),
                                       pl.BlockSpec((A_TB, A_VW), rev)],
        out_specs=[pl.BlockSpec((A_TB, A_KW), rev), pl.BlockSpec((A_TB, A_KW), rev),
                   pl.BlockSpec((A_TB, A_VW), rev), pl.BlockSpec((A_TB, A_VW), rev),
                   pl.BlockSpec((1, A_KW), lambda p, i: (0, p)), pl.BlockSpec((1, 1, A_V), lambda p, i: (p, 0, 0))],
        out_shape=[jax.ShapeDtypeStruct((S, 768), bf16), jax.ShapeDtypeStruct((S, 768), bf16),
                   jax.ShapeDtypeStruct((S, 384), bf16), jax.ShapeDtypeStruct((S, 384), bf16),
                   jax.ShapeDtypeStruct((1, 768), f32), jax.ShapeDtypeStruct((A_HEADS // A_HP, 1, A_V), f32)],
        scratch_shapes=[pltpu.VMEM((A_HP, A_V, A_K), f32)],
        sem=("parallel", "arbitrary"), args=(u, u, u, u, lb, ng, o, st, dya), side=side)


B_TK = 128
SCALE = HD ** -0.5


def _split(x):
    hi = x.astype(bf16)
    return hi, (x - hi.astype(f32)).astype(bf16)


def _dot2(x, m, dn):
    hi, lo = _split(x)
    return _dot(hi, m, dn) + _dot(lo, m, dn)


def _sb_block(qs, kh, mask, m_gt, c):
    z = _dot(qs, kh, NT)
    sp = jnp.maximum(z, 0.0) + jnp.log(1.0 + jnp.exp(-jnp.abs(z)))
    lneg = jnp.where(mask, -sp, 0.0)
    lsz = z - sp
    suf = _dot2(lneg, m_gt, NN) + c
    a = jnp.where(mask, jnp.exp(lsz + suf), 0.0)
    return lneg, lsz, a


def _sb_masks(tq, i, jj):
    t_idx = i * tq + lax.broadcasted_iota(jnp.int32, (tq, B_TK), 0)
    s_idx = jj * B_TK + lax.broadcasted_iota(jnp.int32, (tq, B_TK), 1)
    return s_idx < t_idx


def _sb_tri(strict):
    r = lax.broadcasted_iota(jnp.int32, (B_TK, B_TK), 0)
    c = lax.broadcasted_iota(jnp.int32, (B_TK, B_TK), 1)
    return (r > c if strict else r >= c).astype(bf16)


B_DEAD = -88.0


def _sb_walk(nkb, step, init):
    def cond(state):
        it, alive, _ = state
        return jnp.logical_and(it < nkb, alive)

    def body(state):
        it, _, carry = state
        carry = step(it, carry)
        top = jnp.max(functools.reduce(jnp.maximum, [h[1] for h in carry]))
        return it + 1, top > B_DEAD, carry

    return lax.while_loop(cond, body, (jnp.int32(0), jnp.bool_(True), init))[2]


B_HP = 6
B_W = B_HP * HD


def sb_fwd(u, side=None):
    S = u.shape[0]
    tq = 128

    def body(q_ref, k_ref, v_ref, o_ref):
        i = pl.program_id(1)
        nkb = (i + 1) * (tq // B_TK)
        m_gt = _sb_tri(True)
        qs = [(q_ref[:, hh * HD:(hh + 1) * HD] * SCALE).astype(bf16) for hh in range(B_HP)]

        def step(it, carry):
            jj = nkb - 1 - it
            rows = pl.ds(pl.multiple_of(jj * B_TK, B_TK), B_TK)
            mask = _sb_masks(tq, i, jj)
            kb, vb = k_ref[rows, :], v_ref[rows, :]
            out = []
            for hh in range(B_HP):
                acc, c = carry[hh]
                kh = kb[:, hh * HD:(hh + 1) * HD].astype(bf16)
                vh = vb[:, hh * HD:(hh + 1) * HD].astype(bf16)
                lneg, _, a = _sb_block(qs[hh], kh, mask, m_gt, c)
                out.append((acc + _dot2(a, vh, NN), c + jnp.sum(lneg, axis=1, keepdims=True)))
            return tuple(out)

        z0 = (jnp.zeros((tq, HD), f32), jnp.zeros((tq, 1), f32))
        res = _sb_walk(nkb, step, (z0,) * B_HP)
        o_ref[...] = jnp.concatenate([r[0] for r in res], axis=1)

    outs, souts = _call(
        body, name="sb_fwd", grid=(B_HEADS // B_HP, S // tq),
        in_specs=[pl.BlockSpec((tq, B_W), lambda p, i: (i, O_BQ // B_W + p)),
                  pl.BlockSpec((S, B_W), lambda p, i: (0, O_BK // B_W + p)),
                  pl.BlockSpec((S, B_W), lambda p, i: (0, O_BV // B_W + p))],
        out_specs=[pl.BlockSpec((tq, B_W), lambda p, i: (i, p))],
        out_shape=[jax.ShapeDtypeStruct((S, 384), f32)],
        sem=("parallel", "arbitrary"), args=(u, u, u), side=side)
    return outs[0], souts


def sb_bwd(u, yb, dyb, side=None):
    S = u.shape[0]
    tq = 128
    nq = S // tq

    def body(q_ref, k_ref, v_ref, y_ref, dy_ref, dq_ref, dk_out, dv_out, dk_ref, dv_ref, out_sem):
        p, i = pl.program_id(0), pl.program_id(1)

        @pl.when(i == 0)
        def _():
            dk_ref[...] = jnp.zeros_like(dk_ref)
            dv_ref[...] = jnp.zeros_like(dv_ref)

        nkb = (i + 1) * (tq // B_TK)
        m_gt = _sb_tri(True)
        m_ge = _sb_tri(False)
        qs, dos, tot = [], [], []
        for hh in range(B_HP):
            hs = slice(hh * HD, (hh + 1) * HD)
            qs.append((q_ref[:, hs] * SCALE).astype(bf16))
            dob = dy_ref[:, hs].astype(bf16)
            dos.append(dob)
            tot.append(jnp.sum(dob.astype(f32) * y_ref[:, hs], axis=1, keepdims=True))

        def step(it, carry):
            jj = nkb - 1 - it
            rows = pl.ds(pl.multiple_of(jj * B_TK, B_TK), B_TK)
            mask = _sb_masks(tq, i, jj)
            kb, vb = k_ref[rows, :], v_ref[rows, :]
            out, dk_p, dv_p = [], [], []
            for hh in range(B_HP):
                dq, c, cg = carry[hh]
                kh = kb[:, hh * HD:(hh + 1) * HD].astype(bf16)
                vh = vb[:, hh * HD:(hh + 1) * HD].astype(bf16)
                lneg, lsz, a = _sb_block(qs[hh], kh, mask, m_gt, c)
                g = a * _dot(dos[hh], vh, NT)
                pre = tot[hh] - cg - _dot2(g, m_ge, NN)
                beta = jnp.exp(lsz)
                dz = jnp.where(mask, g * (1.0 - beta) - beta * pre, 0.0).astype(bf16)
                dk_p.append(_dot(dz, qs[hh], TN))
                dv_p.append(_dot(a.astype(bf16), dos[hh], TN))
                out.append((dq + _dot(dz, kh, NN), c + jnp.sum(lneg, axis=1, keepdims=True),
                            cg + jnp.sum(g, axis=1, keepdims=True)))
            dk_ref[rows, :] += jnp.concatenate(dk_p, axis=1)
            dv_ref[rows, :] += jnp.concatenate(dv_p, axis=1)
            return tuple(out)

        z0 = (jnp.zeros((tq, HD), f32), jnp.zeros((tq, 1), f32), jnp.zeros((tq, 1), f32))
        res = _sb_walk(nkb, step, (z0,) * B_HP)
        dq_ref[...] = jnp.concatenate([r[0] for r in res], axis=1) * SCALE

        @pl.when(i == nq - 1)
        def _():
            cols = pl.ds(pl.multiple_of(p * B_W, LANE), B_W)
            ck = pltpu.make_async_copy(dk_ref, dk_out.at[:, cols], out_sem.at[0])
            cv = pltpu.make_async_copy(dv_ref, dv_out.at[:, cols], out_sem.at[1])
            ck.start()
            cv.start()
            ck.wait()
            cv.wait()

    row = pl.BlockSpec((tq, B_W), lambda p, i: (i, p))
    hbm = pl.BlockSpec(memory_space=pl.ANY)
    osd = jax.ShapeDtypeStruct((S, 384), f32)
    return _call(
        body, name="sb_bwd", grid=(B_HEADS // B_HP, nq),
        in_specs=[pl.BlockSpec((tq, B_W), lambda p, i: (i, O_BQ // B_W + p)),
                  pl.BlockSpec((S, B_W), lambda p, i: (0, O_BK // B_W + p)),
                  pl.BlockSpec((S, B_W), lambda p, i: (0, O_BV // B_W + p)), row, row],
        out_specs=[row, hbm, hbm], out_shape=[osd, osd, osd],
        scratch_shapes=[pltpu.VMEM((S, B_W), f32), pltpu.VMEM((S, B_W), f32), pltpu.SemaphoreType.DMA((2,))],
        sem=("parallel", "arbitrary"), args=(u, u, u, yb, dyb), side=side)


def _dil_rows(i, rho, r):
    if r == 1:
        return pl.ds(pl.multiple_of(i * C_BLK, C_BLK), C_BLK)
    return pl.ds(i * (C_BLK * r) + rho, C_BLK, stride=r)


def _dil_scores(qs, kc, kp, i, slope_r):
    qi = lax.broadcasted_iota(jnp.int32, (C_BLK, C_BLK), 0)
    kj = lax.broadcasted_iota(jnp.int32, (C_BLK, C_BLK), 1)
    d_c = qi - kj
    d_p = d_c + C_BLK
    ok_c = d_c >= 0
    ok_p = jnp.logical_and(d_c <= 0, i > 0)
    s_c = jnp.where(ok_c, _dot(qs, kc, NT) - slope_r * d_c.astype(f32), NEG_BIG)
    s_p = jnp.where(ok_p, _dot(qs, kp, NT) - slope_r * d_p.astype(f32), NEG_BIG)
    return s_c, s_p, ok_c, ok_p


def _dil_slope(g, r, hh):
    pair = pl.program_id(0)
    return jnp.where(pair == 0, C_SLOPES[4 * g + hh] * r, C_SLOPES[4 * g + 2 + hh] * r).astype(f32)


def _dil_u_specs(g, S):
    def im(off):
        return lambda p, rho: (0, (off + g * 256) // LANE + p)
    return [pl.BlockSpec((S, LANE), im(O_CQ)), pl.BlockSpec((S, LANE), im(O_CK)), pl.BlockSpec((S, LANE), im(O_CV))]


def dil_fwd(u, g, side=None):
    S = u.shape[0]
    r = C_GROUPS[g][1]
    nbk = S // r // C_BLK

    def body(q_ref, k_ref, v_ref, o_ref, l_ref):
        rho = pl.program_id(1)

        def step(i, carry):
            rc = _dil_rows(i, rho, r)
            rp = _dil_rows(jnp.maximum(i - 1, 0), rho, r)
            q2, kc2, kp2, vc2, vp2 = q_ref[rc, :], k_ref[rc, :], k_ref[rp, :], v_ref[rc, :], v_ref[rp, :]
            o_p, l_p = [], []
            for hh in range(2):
                hs = slice(hh * HD, (hh + 1) * HD)
                qs = (q2[:, hs] * SCALE).astype(bf16)
                kc, kp = kc2[:, hs].astype(bf16), kp2[:, hs].astype(bf16)
                vc, vp = vc2[:, hs].astype(bf16), vp2[:, hs].astype(bf16)
                s_c, s_p, _, _ = _dil_scores(qs, kc, kp, i, _dil_slope(g, r, hh))
                m = jnp.maximum(jnp.max(s_c, axis=1, keepdims=True), jnp.max(s_p, axis=1, keepdims=True))
                p_c, p_p = jnp.exp(s_c - m), jnp.exp(s_p - m)
                den = jnp.sum(p_c, axis=1, keepdims=True) + jnp.sum(p_p, axis=1, keepdims=True)
                o_p.append((_dot(p_c.astype(bf16), vc, NN) + _dot(p_p.astype(bf16), vp, NN)) / den)
                l_p.append(jnp.broadcast_to(m + jnp.log(den), (C_BLK, HD)))
            o_ref[rc, :] = jnp.concatenate(o_p, axis=1)
            l_ref[rc, :] = jnp.concatenate(l_p, axis=1)
            return carry

        lax.fori_loop(0, nbk, step, 0, unroll=2)

    ospec = pl.BlockSpec((S, LANE), lambda p, rho: (0, p))
    osd = jax.ShapeDtypeStruct((S, 256), f32)
    return _call(
        body, name=f"dil_fwd{g}", grid=(2, r),
        in_specs=_dil_u_specs(g, S), out_specs=[ospec, ospec], out_shape=[osd, osd],
        sem=("parallel", "arbitrary"), args=(u, u, u), side=side)


def dil_merge(os_, ls_):
    S = os_[0].shape[0]

    def body(o0, o1, o2, l0, l1, l2, y_ref, lse_ref):
        a, b, c = l0[...], l1[...], l2[...]
        m = jnp.maximum(jnp.maximum(a, b), c)
        ea, eb, ec = jnp.exp(a - m), jnp.exp(b - m), jnp.exp(c - m)
        den = ea + eb + ec
        y_ref[...] = (ea * o0[...] + eb * o1[...] + ec * o2[...]) / den
        lse_ref[...] = m + jnp.log(den)

    spec = pl.BlockSpec((512, 256), lambda i: (i, 0))
    osd = jax.ShapeDtypeStruct((S, 256), f32)
    return pl.pallas_call(
        body, name="dil_merge", grid=(S // 512,), in_specs=[spec] * 6, out_specs=[spec, spec],
        out_shape=[osd, osd], compiler_params=_cp(("parallel",)),
    )(*os_, *ls_)


def dil_bwd(u, g, dyc, yc, lse):
    S = u.shape[0]
    r = C_GROUPS[g][1]
    nbk = S // r // C_BLK

    def body(q_ref, k_ref, v_ref, dy_ref, y_ref, l_ref, dq_ref, dk_ref, dv_ref):
        rho = pl.program_id(1)

        @pl.when(rho == 0)
        def _():
            dk_ref[...] = jnp.zeros_like(dk_ref)
            dv_ref[...] = jnp.zeros_like(dv_ref)

        def step(i, carry):
            rc = _dil_rows(i, rho, r)
            rp = _dil_rows(jnp.maximum(i - 1, 0), rho, r)
            q2, kc2, kp2, vc2, vp2 = q_ref[rc, :], k_ref[rc, :], k_ref[rp, :], v_ref[rc, :], v_ref[rp, :]
            dy2, y2, l2 = dy_ref[rc, :], y_ref[rc, :], l_ref[rc, :]
            dq_p, dkc_p, dkp_p, dvc_p, dvp_p = [], [], [], [], []
            for hh in range(2):
                hs = slice(hh * HD, (hh + 1) * HD)
                qs = (q2[:, hs] * SCALE).astype(bf16)
                kc, kp = kc2[:, hs].astype(bf16), kp2[:, hs].astype(bf16)
                vc, vp = vc2[:, hs].astype(bf16), vp2[:, hs].astype(bf16)
                dy = dy2[:, hs]
                dyb = dy.astype(bf16)
                s_c, s_p, ok_c, ok_p = _dil_scores(qs, kc, kp, i, _dil_slope(g, r, hh))
                lrow = l2[:, hh * HD:hh * HD + 1]
                delta = jnp.sum(dy * y2[:, hs], axis=1, keepdims=True)
                pi_c = jnp.where(ok_c, jnp.exp(s_c - lrow), 0.0)
                pi_p = jnp.where(ok_p, jnp.exp(s_p - lrow), 0.0)
                ds_c = (pi_c * (_dot(dyb, vc, NT) - delta)).astype(bf16)
                ds_p = (pi_p * (_dot(dyb, vp, NT) - delta)).astype(bf16)
                dq_p.append((_dot(ds_c, kc, NN) + _dot(ds_p, kp, NN)) * SCALE)
                dkc_p.append(_dot(ds_c, qs, TN))
                dkp_p.append(_dot(ds_p, qs, TN))
                dvc_p.append(_dot(pi_c.astype(bf16), dyb, TN))
                dvp_p.append(_dot(pi_p.astype(bf16), dyb, TN))
            dq_ref[rc, :] = jnp.concatenate(dq_p, axis=1)
            dk_ref[rc, :] += jnp.concatenate(dkc_p, axis=1)
            dv_ref[rc, :] += jnp.concatenate(dvc_p, axis=1)
            dk_ref[rp, :] += jnp.concatenate(dkp_p, axis=1)
            dv_ref[rp, :] += jnp.concatenate(dvp_p, axis=1)
            return carry

        lax.fori_loop(0, nbk, step, 0, unroll=2)

    ospec = pl.BlockSpec((S, LANE), lambda p, rho: (0, p))
    osd = jax.ShapeDtypeStruct((S, 256), f32)
    return pl.pallas_call(
        body, name=f"dil_bwd{g}", grid=(2, r),
        in_specs=_dil_u_specs(g, S) + [ospec, ospec, ospec], out_specs=[ospec] * 3, out_shape=[osd] * 3,
        compiler_params=_cp(("parallel", "arbitrary")),
    )(u, u, u, dyc, yc, lse)


def _rows_tile(rows):
    return _tile(rows, (256, 176, 128, 64, 32, 16, 8))


def cast_bf16(w):
    shape = w.shape
    w2 = w.reshape(-1, shape[-1])
    rows, cols = w2.shape
    tr = _rows_tile(rows)

    def body(x_ref, o_ref):
        o_ref[...] = x_ref[...].astype(bf16)

    spec = pl.BlockSpec((tr, cols), lambda i: (i, 0))
    out = pl.pallas_call(
        body, name="cast_bf16", grid=(rows // tr,), in_specs=[spec], out_specs=spec,
        out_shape=jax.ShapeDtypeStruct((rows, cols), bf16), compiler_params=_cp(("parallel",)),
    )(w2)
    return out.reshape(shape)


BC1 = 1.0 - ADAM_B1 ** ADAM_STEP
BC2 = 1.0 - ADAM_B2 ** ADAM_STEP


def _adam_math(w, g, m, v):
    m2 = ADAM_B1 * m + (1.0 - ADAM_B1) * g
    v2 = ADAM_B2 * v + (1.0 - ADAM_B2) * (g * g)
    delta = -ADAM_LR * ((m2 / BC1) / (jnp.sqrt(v2 / BC2) + ADAM_EPS) + ADAM_WD * w)
    return delta, m2, v2


def adam(w, g, m, v):
    shape = w.shape
    r2 = lambda t: t.reshape(-1, shape[-1])
    rows, cols = r2(w).shape
    tr = _rows_tile(rows)

    def body(w_ref, g_ref, m_ref, v_ref, d_ref, m2_ref, v2_ref):
        d_ref[...], m2_ref[...], v2_ref[...] = _adam_math(w_ref[...], g_ref[...], m_ref[...], v_ref[...])

    spec = pl.BlockSpec((tr, cols), lambda i: (i, 0))
    osd = jax.ShapeDtypeStruct((rows, cols), f32)
    outs = pl.pallas_call(
        body, name="adam", grid=(rows // tr,), in_specs=[spec] * 4, out_specs=[spec] * 3, out_shape=[osd] * 3,
        compiler_params=_cp(("parallel",)),
    )(r2(w), r2(g), r2(m), r2(v))
    return [o.reshape(shape) for o in outs]


def adam_layers(w, gs, m, v):
    depth, rows, cols = w.shape
    tr = _tile(rows, (128, 64, 32, 16, 8))
    nb = rows // tr

    def body(w_ref, m_ref, v_ref, *rest):
        g_refs, (g_out, d_ref, m2_ref, v2_ref) = rest[:depth], rest[depth:]
        for k in range(depth):
            @pl.when(pl.program_id(0) == k)
            def _(k=k):
                g = g_refs[k][...]
                g_out[0] = g
                d_ref[0], m2_ref[0], v2_ref[0] = _adam_math(w_ref[0], g, m_ref[0], v_ref[0])

    def g_spec(k):
        return pl.BlockSpec((tr, cols), lambda l, i: (jnp.where(l < k, 0, jnp.where(l > k, nb - 1, i)), 0))

    wspec = pl.BlockSpec((1, tr, cols), lambda l, i: (l, i, 0))
    osd = jax.ShapeDtypeStruct(w.shape, f32)
    return pl.pallas_call(
        body, name="adam_layers", grid=(depth, nb), in_specs=[wspec] * 3 + [g_spec(k) for k in range(depth)],
        out_specs=[wspec] * 4, out_shape=[osd] * 4, compiler_params=_cp(("arbitrary", "arbitrary")),
    )(w, m, v, *gs)


ADA_N = 9 * D // 4
ADA_TN = 384


def ada_fwd(c_all, w_ada):
    def body(c_ref, w_ref, o_ref):
        cv = c_ref[...]
        o_ref[0] = _dot((cv * _sig(cv)).astype(bf16), w_ref[0].astype(bf16), NN)

    return pl.pallas_call(
        body, name="ada_fwd", grid=(DEPTH, ADA_N // ADA_TN),
        in_specs=[pl.BlockSpec((8, D), lambda l, j: (0, 0)), pl.BlockSpec((1, D, ADA_TN), lambda l, j: (l, 0, j))],
        out_specs=pl.BlockSpec((1, 8, ADA_TN), lambda l, j: (l, 0, j)),
        out_shape=jax.ShapeDtypeStruct((DEPTH, 8, ADA_N), f32), compiler_params=_cp(("parallel", "parallel")),
    )(c_all, w_ada)


def ada_bwd_adam(c_all, dm, w, m, v, side=None):
    tr = 128

    def body(c_ref, dm_ref, w_ref, m_ref, v_ref, g_ref, d_ref, m2_ref, v2_ref):
        cv = c_ref[...]
        g = _dot((cv * _sig(cv)).astype(bf16), dm_ref[0].astype(bf16), TN)
        g_ref[0] = g
        d_ref[0], m2_ref[0], v2_ref[0] = _adam_math(w_ref[0], g, m_ref[0], v_ref[0])

    wspec = pl.BlockSpec((1, tr, ADA_N), lambda l, i: (l, i, 0))
    osd = jax.ShapeDtypeStruct((DEPTH, D, ADA_N), f32)
    return _call(
        body, name="ada_bwd_adam", grid=(DEPTH, D // tr),
        in_specs=[pl.BlockSpec((8, tr), lambda l, i: (0, i)), pl.BlockSpec((1, 8, ADA_N), lambda l, i: (l, 0, 0)),
                  wspec, wspec, wspec],
        out_specs=[wspec] * 4, out_shape=[osd] * 4, sem=("parallel", "parallel"), args=(c_all, dm, w, m, v), side=side)


def _lb_probs(x):
    mx = jnp.max(x, axis=0, keepdims=True)
    e = jnp.exp(x - mx)
    return e / jnp.sum(e, axis=0, keepdims=True)


def lb_fwd(logits):
    def body(x_ref, o_ref):
        p = _lb_probs(x_ref[...])
        rows = [jnp.zeros((1, 768), f32)]
        for l in range(1, DEPTH):
            rows.append(rows[-1] + p[l:l + 1, :])
        o_ref[...] = jnp.concatenate(rows, axis=0)

    return pl.pallas_call(body, name="lb_fwd", out_shape=jax.ShapeDtypeStruct((DEPTH, 768), f32))(logits)


def lb_bwd(logits, dlb):
    def body(x_ref, d_ref, o_ref):
        p = _lb_probs(x_ref[...])
        d = d_ref[...]
        rows = [jnp.zeros((1, 768), f32)] * DEPTH
        acc = jnp.zeros((1, 768), f32)
        for l in range(DEPTH - 1, 0, -1):
            acc = acc + d[l:l + 1, :]
            rows[l] = acc
        dp = jnp.concatenate(rows, axis=0)
        o_ref[...] = p * (dp - jnp.sum(p * dp, axis=0, keepdims=True))

    return pl.pallas_call(body, name="lb_bwd", out_shape=jax.ShapeDtypeStruct((DEPTH, 768), f32))(logits, dlb)


def sum_slots(x):
    n, rows, cols = x.shape
    tr = _rows_tile(rows)

    def body(x_ref, o_ref):
        acc = x_ref[0]
        for j in range(1, n):
            acc = acc + x_ref[j]
        o_ref[...] = acc

    return pl.pallas_call(
        body, name="sum_slots", grid=(rows // tr,),
        in_specs=[pl.BlockSpec((n, tr, cols), lambda i: (0, i, 0))], out_specs=pl.BlockSpec((tr, cols), lambda i: (i, 0)),
        out_shape=jax.ShapeDtypeStruct((rows, cols), f32), compiler_params=_cp(("parallel",)),
    )(x)


ANY = pl.BlockSpec(memory_space=pl.ANY)
CHIP_FLIPS = ((1, 0), (0, 1), (1, 1))
DEV_FLIPS = tuple((a, b, d) for a in (0, 1) for b in (0, 1) for d in (0, 1))[1:]


def _me():
    return lax.axis_index("x"), lax.axis_index("y"), lax.axis_index("c")


def _flip(v, f):
    return 1 - v if f else v


def _comm_call(body, name, ins, out_shapes, n_remote, n_local):
    return pl.pallas_call(
        body, name=name, in_specs=[ANY] * len(ins), out_specs=[ANY] * len(out_shapes), out_shape=out_shapes,
        scratch_shapes=[pltpu.SemaphoreType.DMA((n_remote,)), pltpu.SemaphoreType.DMA((n_remote,)),
                        pltpu.SemaphoreType.DMA((max(n_local, 1),))],
    )(*ins)


def run_plan(plan, name):
    ni, no = len(plan.ins), len(plan.outs)

    def body(*refs):
        ins, outs, sems = refs[:ni], refs[ni:ni + no], refs[ni + no:]
        plan.start(ins, outs, *sems)
        plan.wait(ins, outs, *sems)

    return pl.pallas_call(body, name=name, in_specs=[ANY] * ni, out_specs=[ANY] * no, out_shape=list(plan.outs),
                          scratch_shapes=plan.sems())(*plan.ins)


def gather_chips_plan(arrs, layer=None):
    n = len(arrs)
    shapes = [a.shape if layer is None else a.shape[1:] for a in arrs]

    def copies(ins, outs, send, recv, loc):
        x, y, c = _me()
        mine = 2 * x + y
        srcs = [r if layer is None else r.at[layer] for r in ins]
        locs = [pltpu.make_async_copy(srcs[a], outs[a].at[mine], loc.at[a]) for a in range(n)]

        def remote(a, k, slot):
            fx, fy = CHIP_FLIPS[k]
            return pltpu.make_async_remote_copy(srcs[a], outs[a].at[slot], send.at[3 * a + k], recv.at[3 * a + k],
                                                device_id=(_flip(x, fx), _flip(y, fy), c), device_id_type=MESH)

        peers = [2 * _flip(x, fx) + _flip(y, fy) for fx, fy in CHIP_FLIPS]
        return locs, remote, mine, peers

    def start(ins, outs, send, recv, loc):
        locs, remote, mine, _ = copies(ins, outs, send, recv, loc)
        for cp in locs:
            cp.start()
        for a in range(n):
            for k in range(3):
                remote(a, k, mine).start()

    def wait(ins, outs, send, recv, loc):
        locs, remote, _, peers = copies(ins, outs, send, recv, loc)
        for a in range(n):
            for k in range(3):
                cp = remote(a, k, peers[k])
                cp.wait_recv()
                cp.wait_send()
        for cp in locs:
            cp.wait()

    outs = [jax.ShapeDtypeStruct((4,) + tuple(s), a.dtype) for s, a in zip(shapes, arrs)]
    return Plan(list(arrs), outs, 3 * n, n, start, wait)


def all_gather_chips(arrs, layer=None, name="ag4"):
    return run_plan(gather_chips_plan(arrs, layer), name)


def all_gather_devs(arr, name="ag8"):
    def body(in_ref, out_ref, send, recv, loc):
        x, y, c = _me()
        mine = 4 * x + 2 * y + c
        lc = pltpu.make_async_copy(in_ref, out_ref.at[mine], loc.at[0])
        lc.start()

        def remote(k, slot):
            fx, fy, fc = DEV_FLIPS[k]
            return pltpu.make_async_remote_copy(in_ref, out_ref.at[slot], send.at[k], recv.at[k],
                                                device_id=(_flip(x, fx), _flip(y, fy), _flip(c, fc)), device_id_type=MESH)

        for k in range(7):
            remote(k, mine).start()
        for k, (fx, fy, fc) in enumerate(DEV_FLIPS):
            cp = remote(k, 4 * _flip(x, fx) + 2 * _flip(y, fy) + _flip(c, fc))
            cp.wait_recv()
            cp.wait_send()
        lc.wait()

    return _comm_call(body, name, [arr], [jax.ShapeDtypeStruct((8,) + arr.shape, arr.dtype)], 7, 1)[0]


def _rows_of(which, rows):
    return pl.ds(pl.multiple_of(which * rows, 16), rows)


def dev_exchange_plan(parts):
    n = len(parts)

    def copies(ins, outs, send, recv, loc):
        x, y, c = _me()
        mine = 4 * x + 2 * y + c

        def piece(a, px, py, pc):
            rows = ins[a].shape[1] // 2
            return ins[a].at[2 * px + py, _rows_of(pc, rows), :]

        locs = [pltpu.make_async_copy(piece(a, x, y, c), outs[a].at[mine], loc.at[a]) for a in range(n)]

        def remote(a, k, slot):
            fx, fy, fc = DEV_FLIPS[k]
            px, py, pc = _flip(x, fx), _flip(y, fy), _flip(c, fc)
            return pltpu.make_async_remote_copy(piece(a, px, py, pc), outs[a].at[slot], send.at[7 * a + k], recv.at[7 * a + k],
                                                device_id=(px, py, pc), device_id_type=MESH)

        peers = [4 * _flip(x, fx) + 2 * _flip(y, fy) + _flip(c, fc) for fx, fy, fc in DEV_FLIPS]
        return locs, remote, mine, peers

    def start(ins, outs, send, recv, loc):
        locs, remote, mine, _ = copies(ins, outs, send, recv, loc)
        for cp in locs:
            cp.start()
        for a in range(n):
            for k in range(7):
                remote(a, k, mine).start()

    def wait(ins, outs, send, recv, loc):
        locs, remote, _, peers = copies(ins, outs, send, recv, loc)
        for a in range(n):
            for k in range(7):
                cp = remote(a, k, peers[k])
                cp.wait_recv()
                cp.wait_send()
        for cp in locs:
            cp.wait()

    outs = [jax.ShapeDtypeStruct((8, p.shape[1] // 2, p.shape[2]), p.dtype) for p in parts]
    return Plan(list(parts), outs, 7 * n, n, start, wait)


def sum_share(slots, name="rs_sum"):
    n, r, cols = slots.shape
    tr = _tile(r, (128, 176, 64))
    steps = r // tr

    def body(s_ref, g_ref, buf, send, loc, recv):
        i = pl.program_id(0)
        x, y, c = _me()
        slot = i % 2

        def copies(step, sl):
            rows = pl.ds(pl.multiple_of(c * r + step * tr, 8), tr)
            rem = pltpu.make_async_remote_copy(buf.at[sl], g_ref.at[rows, :], send.at[sl], recv.at[0],
                                               device_id=(x, y, 1 - c), device_id_type=MESH)
            return rem, pltpu.make_async_copy(buf.at[sl], g_ref.at[rows, :], loc.at[sl])

        @pl.when(i >= 2)
        def _():
            rem, lc = copies(i - 2, slot)
            rem.wait_send()
            lc.wait()

        acc = s_ref[0].astype(f32)
        for j in range(1, n):
            acc = acc + s_ref[j].astype(f32)
        buf[slot] = acc
        rem, lc = copies(i, slot)
        rem.start()
        lc.start()

        @pl.when(i == steps - 1)
        def _():
            for back in range(min(2, steps)):
                rem, lc = copies(i - back, (i - back) % 2)
                rem.wait_send()
                lc.wait()
            other = g_ref.at[pl.ds(pl.multiple_of((1 - c) * r, 8), r), :]
            pltpu.make_async_remote_copy(other, other, send.at[0], recv.at[0],
                                         device_id=(x, y, 1 - c), device_id_type=MESH).wait_recv()

    return pl.pallas_call(
        body, name=name, grid=(steps,),
        in_specs=[pl.BlockSpec((n, tr, cols), lambda i: (0, i, 0))], out_specs=ANY,
        out_shape=jax.ShapeDtypeStruct((2 * r, cols), f32),
        scratch_shapes=[pltpu.VMEM((2, tr, cols), f32), pltpu.SemaphoreType.DMA((2,)), pltpu.SemaphoreType.DMA((2,)),
                        pltpu.SemaphoreType.DMA((1,))],
        compiler_params=_cp(("arbitrary",)),
    )(slots)


BIG = ("ffn1_w_in", "ffn1_w_out", "w_in", "w_branch_a", "w_branch_b", "w_branch_c", "w_out", "ffn2_w_in", "ffn2_w_out")
ROW_SHARDED = ("ffn1_w_out", "w_out", "ffn2_w_out")
RES_W = (0.5, 1.0, 0.5)


def _full_weight(name, g):
    if name in ROW_SHARDED:
        return g.reshape(4 * g.shape[1], g.shape[2])
    return jnp.concatenate([g[0], g[1], g[2], g[3]], axis=1)


def _by_shard(name, dw):
    if name in ROW_SHARDED:
        return dw.reshape(4, dw.shape[0] // 4, dw.shape[1])
    return dw.reshape(dw.shape[0], 4, dw.shape[1] // 4).transpose(1, 0, 2)


def _full_weight_t(name, g):
    if name in ROW_SHARDED:
        return g.reshape(4 * g.shape[1], g.shape[2]).T
    return g.transpose(0, 2, 1).reshape(4 * g.shape[2], g.shape[1])


def _ffn_fwd(x, w_in, w_out, a_vec, sh_vec, b_vec):
    h, h_t = prenorm(x, a_vec, sh_vec)
    ua, ub, s, s_t = ffn_in_swiglu(h, w_in)
    y = mm(s, w_out, name="ffn_out")
    return postnorm(x, y, b_vec), (x, h_t, ua, ub, s_t, y)


def _ffn_bwd(dout, saved, w_in_t, w_out_t, a_vec, b_vec, plans=None):
    x, h_t, ua, ub, s_t, y = saved
    riders = plans or (None, None, None)
    dy, db = post_bwd(dout, y, b_vec)
    dw_out = mm(s_t, dy, out_dtype=bf16, name="ffn_dwo", side=riders[0])
    du = ffn_du(dy, w_out_t, ua, ub)
    dh = mm(du, w_in_t, name="ffn_dh", side=riders[1])
    sides = []
    if plans:
        (dw_out, s0), (dh, s1) = dw_out, dh
        sides = [s0, s1]
    dw_in = mm(h_t, du, out_dtype=bf16, name="ffn_dwi", side=riders[2])
    if plans:
        dw_in, s2 = dw_in
        sides.append(s2)
    dx, dsh, da = pre_bwd(dout, dh, x, a_vec)
    return (dx, dw_in, dw_out, dsh, da, db) + ((sides,) if plans else ())


def _mix_fwd(x, w, lb, ng, a_vec, sh_vec, b_vec, plans=(None,) * 6):
    h, h_t = prenorm(x, a_vec, sh_vec)
    u = mm(h, w["w_in"], name="mix_in", side=plans[0])
    side0 = None
    if plans[0] is not None:
        u, side0 = u
    (o, ya, st), side1 = hgrn_fwd(u, lb, ng, side=plans[1])
    yb, side2 = sb_fwd(u, side=plans[2])
    groups, dil_sides = [], []
    for g in range(3):
        og, sg = dil_fwd(u, g, side=plans[3 + g])
        groups.append(og)
        dil_sides.append(sg)
    yc, lse = dil_merge([o_ for o_, _ in groups], [l_ for _, l_ in groups])
    pa = mm(ya, w["w_branch_a"], name="mix_pa")
    pb = mm(yb, w["w_branch_b"], name="mix_pb")
    pc = mm(yc, w["w_branch_c"], name="mix_pc")
    merged, merged_t = gate_merge(u, pa, pb, pc)
    z = mm(merged, w["w_out"], name="mix_out")
    return (postnorm(x, z, b_vec), (x, h_t, u, o, ya, st, yb, yc, lse, pa, pb, pc, merged_t, z),
            (side0, side1, side2, *dil_sides))


def _mix_bwd(dout, saved, wt, lb, ng, a_vec, b_vec, plans=(None,) * 3):
    x, h_t, u, o, ya, st, yb, yc, lse, pa, pb, pc, merged_t, z = saved
    dz, db = post_bwd(dout, z, b_vec)
    dmerged = mm(dz, wt["w_out"], name="mix_dm")
    dw_out = mm(merged_t, dz, out_dtype=bf16, name="mix_dwo")
    dpa, dpb, dpc, dg0, dg1, dg2 = gate_bwd(dmerged, u, pa, pb, pc)
    dya = mm(dpa, wt["w_branch_a"], name="mix_dya")
    dyb = mm(dpb, wt["w_branch_b"], name="mix_dyb")
    dyc = mm(dpc, wt["w_branch_c"], name="mix_dyc")
    dw_a = mm(ya.T, dpa, out_dtype=bf16, name="mix_dwa")
    dw_b = mm(yb.astype(bf16).T, dpb, out_dtype=bf16, name="mix_dwb")
    dw_c = mm(yc.astype(bf16).T, dpc, out_dtype=bf16, name="mix_dwc")
    (daq, daf, dai, dag, dlb, dng), side0 = hgrn_bwd(u, lb, ng, o, st, dya, side=plans[0])
    (dbq, dbk, dbv), side1 = sb_bwd(u, yb, dyb, side=plans[1])
    dc = [dil_bwd(u, g, dyc, yc, lse) for g in range(3)]
    du = jnp.concatenate(
        [daq, daf, dai, dag] + [t.astype(bf16) for t in (dbq, dbk, dbv)]
        + [dc[g][j].astype(bf16) for j in range(3) for g in range(3)] + [dg0, dg1, dg2], axis=1)
    dh = mm(du, wt["w_in"], name="mix_dh")
    dw_in = mm(h_t, du, out_dtype=bf16, name="mix_dwi", side=plans[2])
    side2 = None
    if plans[2] is not None:
        dw_in, side2 = dw_in
    dx, dsh, da = pre_bwd(dout, dh, x, a_vec)
    grads = {"w_in": dw_in, "w_out": dw_out, "w_branch_a": dw_a, "w_branch_b": dw_b, "w_branch_c": dw_c}
    return dx, grads, dlb, jnp.sum(dng, axis=0), dsh, da, db, (side0, side1, side2)


FWD_RIDERS = (("ffn1_w_in",), ("ffn2_w_in",), ("w_in",), ("ffn1_w_out", "w_out"), ("ffn2_w_out", "w_branch_a"),
              ("w_branch_b", "w_branch_c"))
BWD_RIDERS = (("ffn1_w_in", "ffn2_w_out", "w_branch_a", "w_branch_b", "w_branch_c"), ("w_in", "ffn1_w_out", "w_out"),
              ("ffn2_w_in",))
LAST_RIDERS = (("ffn2_w_out", "w_out", "w_branch_a", "w_branch_b", "w_branch_c"), ("ffn2_w_in",), ("w_in",))
TAIL = ("ffn1_w_in", "ffn1_w_out")


def _reduce_to_shards(names, grads):
    slots = run_plan(dev_exchange_plan([_by_shard(n, grads[n]) for n in names]), "rs_x8")
    return {n: sum_share(s) for n, s in zip(names, slots)}


def kernel(x, c, w_ada, b_ada, norm_g, ffn1_w_in, ffn1_w_out, w_in, hgrn_lb_logits, hgrn_norm_g, w_branch_a, w_branch_b, w_branch_c, w_out, ffn2_w_in, ffn2_w_out, loss_target, m_w_ada, m_b_ada, m_norm_g, m_ffn1_w_in, m_ffn1_w_out, m_w_in, m_hgrn_lb_logits, m_hgrn_norm_g, m_w_branch_a, m_w_branch_b, m_w_branch_c, m_w_out, m_ffn2_w_in, m_ffn2_w_out, v_w_ada, v_b_ada, v_norm_g, v_ffn1_w_in, v_ffn1_w_out, v_w_in, v_hgrn_lb_logits, v_hgrn_norm_g, v_w_branch_a, v_w_branch_b, v_w_branch_c, v_w_out, v_ffn2_w_in, v_ffn2_w_out):
    weights = dict(w_ada=w_ada, b_ada=b_ada, norm_g=norm_g, ffn1_w_in=ffn1_w_in, ffn1_w_out=ffn1_w_out, w_in=w_in,
                   hgrn_lb_logits=hgrn_lb_logits, hgrn_norm_g=hgrn_norm_g, w_branch_a=w_branch_a, w_branch_b=w_branch_b,
                   w_branch_c=w_branch_c, w_out=w_out, ffn2_w_in=ffn2_w_in, ffn2_w_out=ffn2_w_out)
    mom = dict(w_ada=m_w_ada, b_ada=m_b_ada, norm_g=m_norm_g, ffn1_w_in=m_ffn1_w_in, ffn1_w_out=m_ffn1_w_out, w_in=m_w_in,
               hgrn_lb_logits=m_hgrn_lb_logits, hgrn_norm_g=m_hgrn_norm_g, w_branch_a=m_w_branch_a, w_branch_b=m_w_branch_b,
               w_branch_c=m_w_branch_c, w_out=m_w_out, ffn2_w_in=m_ffn2_w_in, ffn2_w_out=m_ffn2_w_out)
    var = dict(w_ada=v_w_ada, b_ada=v_b_ada, norm_g=v_norm_g, ffn1_w_in=v_ffn1_w_in, ffn1_w_out=v_ffn1_w_out, w_in=v_w_in,
               hgrn_lb_logits=v_hgrn_lb_logits, hgrn_norm_g=v_hgrn_norm_g, w_branch_a=v_w_branch_a, w_branch_b=v_w_branch_b,
               w_branch_c=v_w_branch_c, w_out=v_w_out, ffn2_w_in=v_ffn2_w_in, ffn2_w_out=v_ffn2_w_out)
    order = list(weights)
    xi, yi, ci = _me()
    chip = 2 * xi + yi
    dev = 4 * xi + 2 * yi + ci
    xs = x[0]

    c_all = all_gather_devs(c, name="ag8_c").reshape(8, D)
    mod_sh = all_gather_chips([ada_fwd(c_all, w_ada)], name="ag4_mod")[0]
    mod_all = mod_sh.transpose(1, 2, 0, 3).reshape(DEPTH, 8, 9 * D)
    mod = lax.dynamic_index_in_dim(mod_all, dev, axis=1, keepdims=False) + b_ada
    mod = mod.reshape(DEPTH, 3, 3, D)
    ng_all = all_gather_chips([norm_g.reshape(DEPTH * 6, D // 4)], name="ag4_norm")[0]
    ng_all = ng_all.reshape(4, DEPTH, 6, D // 4).transpose(1, 2, 0, 3).reshape(DEPTH, 6, D)
    lb_all = lb_fwd(hgrn_lb_logits)
    w16 = {n: cast_bf16(weights[n]) for n in BIG}

    def vecs(l, i):
        shift, scale, gate = mod[l, i, 0][None], mod[l, i, 1][None], mod[l, i, 2][None]
        g_pre, g_post = ng_all[l, 2 * i][None], ng_all[l, 2 * i + 1][None]
        return g_pre * (1.0 + scale), shift, RES_W[i] * gate * g_post

    saved, full = [], []
    gathered = dict(zip(BIG, all_gather_chips([w16[n] for n in BIG], layer=0, name="ag4_w0")))
    for l in range(DEPTH):
        w = {n: _full_weight(n, gathered[n]) for n in BIG}
        full.append({n: _full_weight_t(n, gathered[n]) for n in BIG})
        lb, ng = lb_all[l][None], hgrn_norm_g[l][None]
        plans = (None,) * len(FWD_RIDERS)
        if l + 1 < DEPTH:
            plans = tuple(gather_chips_plan([w16[n] for n in names], layer=l + 1) for names in FWD_RIDERS)
        xs, s1 = _ffn_fwd(xs, w["ffn1_w_in"], w["ffn1_w_out"], *vecs(l, 0))
        xs, s2, sides = _mix_fwd(xs, w, lb, ng, *vecs(l, 1), plans=plans)
        xs, s3 = _ffn_fwd(xs, w["ffn2_w_in"], w["ffn2_w_out"], *vecs(l, 2))
        saved.append((s1, s2, s3))
        if l + 1 < DEPTH:
            gathered = {n: g for names, outs in zip(FWD_RIDERS, sides) for n, g in zip(names, outs)}

    dx, loss_part = loss_grad(xs, loss_target[0])
    loss = lax.psum(loss_part[0, 0], ("x", "y", "c"))

    big_grads = {n: [None] * DEPTH for n in BIG}
    d_mod, d_ng, d_lb, d_hng = [None] * DEPTH, [None] * DEPTH, [None] * DEPTH, [None] * DEPTH
    pending = None
    for l in reversed(range(DEPTH)):
        wt = full[l]
        s1, s2, s3 = saved[l]
        lb, ng = lb_all[l][None], hgrn_norm_g[l][None]
        rows_mod, rows_ng = [None] * 9, [None] * 6

        def vec_grads(i, dsh, da, db):
            scale, gate = mod[l, i, 1][None], mod[l, i, 2][None]
            g_pre, g_post = ng_all[l, 2 * i][None], ng_all[l, 2 * i + 1][None]
            rows_mod[3 * i], rows_mod[3 * i + 1], rows_mod[3 * i + 2] = dsh, g_pre * da, RES_W[i] * g_post * db
            rows_ng[2 * i], rows_ng[2 * i + 1] = (1.0 + scale) * da, RES_W[i] * gate * db

        a3, _, b3 = vecs(l, 2)
        dx, dwi, dwo, dsh, da, db = _ffn_bwd(dx, s3, wt["ffn2_w_in"], wt["ffn2_w_out"], a3, b3)
        vec_grads(2, dsh, da, db)
        grads = {"ffn2_w_in": dwi, "ffn2_w_out": dwo}
        a2, _, b2 = vecs(l, 1)
        plans = (None,) * len(BWD_RIDERS)
        if pending is not None:
            plans = tuple(dev_exchange_plan([pending[n] for n in names]) for names in BWD_RIDERS)
        dx, gmix, dlb, dhng, dsh, da, db, sides = _mix_bwd(dx, s2, wt, lb, ng, a2, b2, plans=plans)
        if pending is not None:
            for names, outs in zip(BWD_RIDERS, sides):
                for n, slots in zip(names, outs):
                    big_grads[n][l + 1] = sum_share(slots)
        vec_grads(1, dsh, da, db)
        grads.update(gmix)
        a1, _, b1 = vecs(l, 0)
        if l > 0:
            dx, dwi, dwo, dsh, da, db = _ffn_bwd(dx, s1, wt["ffn1_w_in"], wt["ffn1_w_out"], a1, b1)
        else:
            ready = {n: _by_shard(n, grads[n]) for names in LAST_RIDERS for n in names}
            plans = tuple(dev_exchange_plan([ready[n] for n in names]) for names in LAST_RIDERS)
            dx, dwi, dwo, dsh, da, db, sides = _ffn_bwd(dx, s1, wt["ffn1_w_in"], wt["ffn1_w_out"], a1, b1, plans=plans)
            for names, outs in zip(LAST_RIDERS, sides):
                for n, slots in zip(names, outs):
                    big_grads[n][0] = sum_share(slots)
        vec_grads(0, dsh, da, db)
        grads.update({"ffn1_w_in": dwi, "ffn1_w_out": dwo})
        pending = {n: _by_shard(n, grads[n]) for n in (BIG if l > 0 else TAIL)}
        d_mod[l] = jnp.concatenate(rows_mod, axis=1)
        d_ng[l] = jnp.concatenate(rows_ng, axis=0)
        d_lb[l], d_hng[l] = dlb, dhng

    n_small = 6 * D * DEPTH + 768 * DEPTH + A_V * DEPTH + 9 * D * DEPTH
    pad = -n_small % (512 * LANE)
    flat = jnp.concatenate([jnp.stack(d_ng).reshape(-1), jnp.concatenate(d_lb, axis=0).reshape(-1),
                            jnp.concatenate(d_hng, axis=0).reshape(-1), jnp.concatenate(d_mod, axis=0).reshape(-1),
                            jnp.zeros((pad,), f32)])
    small_all = all_gather_devs(flat.reshape(-1, LANE), name="ag8_small")
    total = sum_slots(small_all).reshape(-1)
    o1 = 6 * D * DEPTH
    o2 = o1 + 768 * DEPTH
    o3 = o2 + A_V * DEPTH
    g_ng_full = total[:o1].reshape(DEPTH, 6, D)
    g_lb_all = total[o1:o2].reshape(DEPTH, 768)
    g_small = {
        "norm_g": lax.dynamic_slice_in_dim(g_ng_full, chip * (D // 4), D // 4, axis=2),
        "hgrn_lb_logits": lb_bwd(hgrn_lb_logits, g_lb_all),
        "hgrn_norm_g": total[o2:o3].reshape(DEPTH, A_V),
        "b_ada": total[o3:n_small].reshape(DEPTH, 9 * D),
    }
    dmod_all = small_all.reshape(8, -1)[:, o3:n_small].reshape(8, DEPTH, 9 * D).transpose(1, 0, 2)
    dm_sh = lax.dynamic_slice_in_dim(dmod_all, chip * ADA_N, ADA_N, axis=2)

    out_g, out_d, out_m, out_v = {}, {}, {}, {}
    ada_outs, slots = ada_bwd_adam(c_all, dm_sh, w_ada, m_w_ada, v_w_ada, side=dev_exchange_plan([pending[n] for n in TAIL]))
    out_g["w_ada"], out_d["w_ada"], out_m["w_ada"], out_v["w_ada"] = ada_outs
    for n, s in zip(TAIL, slots):
        big_grads[n][0] = sum_share(s)
    for n in BIG:
        out_g[n], out_d[n], out_m[n], out_v[n] = adam_layers(weights[n], big_grads[n], mom[n], var[n])
    out_g.update(g_small)
    for n in g_small:
        out_d[n], out_m[n], out_v[n] = adam(weights[n], out_g[n], mom[n], var[n])
    return (loss, dx[None], *[out_g[n] for n in order], *[out_d[n] for n in order],
            *[out_m[n] for n in order], *[out_v[n] for n in order])
```

```python
import functools
import math

import jax
import jax.numpy as jnp
from jax import lax
from jax.experimental import pallas as pl
from jax.experimental.pallas import tpu as pltpu

f32, bf16 = jnp.float32, jnp.bfloat16

D = 1024
DEPTH = 4
D_FF = 2816
EPS = 1e-6
NEG_BIG = -1e30
TINY = 1e-30
A_HEADS, A_K, A_V, A_CHUNK = 6, 128, 64, 64
A_SUB = 16
A_CLAMP = 80.0
B_HEADS, HD = 6, 64
C_GROUPS = ((128, 1), (512, 4), (2048, 16))
C_BLK = 128
IN_COLS = 8832
O_AQ, O_AF, O_AI, O_AG = 0, 768, 1536, 1920
O_BQ, O_BK, O_BV = 2304, 2688, 3072
O_CQ, O_CK, O_CV = 3456, 4224, 4992
O_GATE = 5760
LANE = 128
ADAM_LR, ADAM_B1, ADAM_B2, ADAM_EPS, ADAM_WD, ADAM_STEP = 0.001, 0.9, 0.999, 1e-08, 0.01, 10
MESH = pl.DeviceIdType.MESH
VMEM_LIMIT = 56 * 1024 * 1024


def _alibi_slopes(n):
    def pow2(m):
        start = 2.0 ** (-8.0 / m)
        return [start ** (i + 1) for i in range(m)]
    if math.log2(n).is_integer():
        s = pow2(n)
    else:
        c = 2 ** int(math.floor(math.log2(n)))
        s = pow2(c) + pow2(2 * c)[0::2][: n - c]
    return sorted(s, reverse=True)


C_SLOPES = _alibi_slopes(12)


def _tile(n, prefs):
    for p in prefs:
        if n % p == 0:
            return p
    return n


def _cp(sem):
    return pltpu.CompilerParams(dimension_semantics=sem, vmem_limit_bytes=VMEM_LIMIT)


def _sig(x):
    return 1.0 / (1.0 + jnp.exp(-x))


def _dot(a, b, dn, precision=None):
    return lax.dot_general(a, b, (dn, ((), ())), preferred_element_type=f32, precision=precision)


NN = ((1,), (0,))
NT = ((1,), (1,))
TN = ((0,), (0,))


class Plan:
    def __init__(self, ins, outs, n_remote, n_local, start, wait):
        self.ins, self.outs, self.n_remote, self.n_local, self.start, self.wait = ins, outs, n_remote, n_local, start, wait

    def sems(self):
        return [pltpu.SemaphoreType.DMA((self.n_remote,)), pltpu.SemaphoreType.DMA((self.n_remote,)),
                pltpu.SemaphoreType.DMA((max(self.n_local, 1),))]


def _call(body, *, name, grid, in_specs, out_specs, out_shape, sem, args, scratch_shapes=(), side=None):
    if side is None:
        return pl.pallas_call(body, name=name, grid=grid, in_specs=in_specs, out_specs=out_specs, out_shape=out_shape,
                              scratch_shapes=list(scratch_shapes), compiler_params=_cp(sem))(*args), None
    any_spec = pl.BlockSpec(memory_space=pl.ANY)
    n_in, n_out, n_scr = len(in_specs), len(out_specs), len(scratch_shapes)
    s_in, s_out = len(side.ins), len(side.outs)

    def hosted(*refs):
        ins, rest = refs[:n_in], refs[n_in:]
        sins, rest = rest[:s_in], rest[s_in:]
        outs, rest = rest[:n_out], rest[n_out:]
        souts, rest = rest[:s_out], rest[s_out:]
        scr, sems = rest[:n_scr], rest[n_scr:]
        pids = [pl.program_id(d) for d in range(len(grid))]
        first = functools.reduce(jnp.logical_and, [p == 0 for p in pids])
        last = functools.reduce(jnp.logical_and, [p == g - 1 for p, g in zip(pids, grid)])

        @pl.when(first)
        def _():
            side.start(sins, souts, *sems)

        body(*ins, *outs, *scr)

        @pl.when(last)
        def _():
            side.wait(sins, souts, *sems)

    res = pl.pallas_call(
        hosted, name=name, grid=grid, in_specs=list(in_specs) + [any_spec] * s_in,
        out_specs=list(out_specs) + [any_spec] * s_out, out_shape=list(out_shape) + list(side.outs),
        scratch_shapes=list(scratch_shapes) + side.sems(), compiler_params=_cp(("arbitrary",) * len(grid)),
    )(*args, *side.ins)
    return res[:n_out], res[n_out:]


MM_TILES = {
    (4096, 1024, 2816): (1024, 512, 2816),
    (4096, 1024, 5632): (512, 512, 5632),
    (1024, 5632, 4096): (512, 512, 4096),
    (2816, 1024, 4096): (704, 512, 4096),
    (4096, 8832, 1024): (512, 2944, 1024),
    (4096, 1024, 8832): (1024, 512, 2944),
    (1024, 8832, 4096): (512, 2944, 1024),
    (1024, 1024, 4096): (512, 512, 4096),
    (384, 1024, 4096): (384, 512, 4096),
    (256, 1024, 4096): (256, 512, 4096),
}


def mm(a, b, *, out_dtype=f32, name="mm", side=None):
    M, K = a.shape
    K2, N = b.shape
    assert K == K2, (a.shape, b.shape)
    tm, tn, tk = MM_TILES.get((M, N, K), (_tile(M, (1024, 704, 512, 384, 256, 128)), _tile(N, (512, 384, 256, 128)),
                                          _tile(K, (1024, 512, 1408, 384, 256, 128))))
    nk = K // tk

    def body(a_ref, b_ref, o_ref, *acc):
        p = _dot(a_ref[...].astype(bf16), b_ref[...].astype(bf16), NN)
        if nk == 1:
            o_ref[...] = p.astype(out_dtype)
            return
        acc_ref, = acc
        k = pl.program_id(2)

        @pl.when(k == 0)
        def _():
            acc_ref[...] = p

        @pl.when(k > 0)
        def _():
            acc_ref[...] += p

        @pl.when(k == nk - 1)
        def _():
            o_ref[...] = acc_ref[...].astype(out_dtype)

    outs, souts = _call(
        body, name=name, grid=(M // tm, N // tn, nk),
        in_specs=[pl.BlockSpec((tm, tk), lambda i, j, k: (i, k)), pl.BlockSpec((tk, tn), lambda i, j, k: (k, j))],
        out_specs=[pl.BlockSpec((tm, tn), lambda i, j, k: (i, j))],
        out_shape=[jax.ShapeDtypeStruct((M, N), out_dtype)],
        scratch_shapes=[pltpu.VMEM((tm, tn), f32)] if nk > 1 else [],
        sem=("parallel", "parallel", "arbitrary"), args=(a, b), side=side)
    return outs[0] if side is None else (outs[0], souts)


FF_TM = 1024
FF_T = 256
FF_NB = D_FF // FF_T


def ffn_in_swiglu(h, w_in, side=None):
    S = h.shape[0]

    def body(h_ref, wa_ref, wb_ref, a_ref, b_ref, s_ref, st_ref):
        hv = h_ref[...]
        a = _dot(hv, wa_ref[...], NN)
        b = _dot(hv, wb_ref[...], NN)
        a_ref[...] = a.astype(bf16)
        b_ref[...] = b.astype(bf16)
        s = (a * _sig(a) * b).astype(bf16)
        s_ref[...] = s
        st_ref[...] = s.T

    ospec = pl.BlockSpec((FF_TM, FF_T), lambda i, j: (i, j))
    osd = jax.ShapeDtypeStruct((S, D_FF), bf16)
    return _call(
        body, name="ffn_in", grid=(S // FF_TM, FF_NB),
        in_specs=[pl.BlockSpec((FF_TM, D), lambda i, j: (i, 0)), pl.BlockSpec((D, FF_T), lambda i, j: (0, j)),
                  pl.BlockSpec((D, FF_T), lambda i, j: (0, j + FF_NB))],
        out_specs=[ospec] * 3 + [pl.BlockSpec((FF_T, FF_TM), lambda i, j: (j, i))],
        out_shape=[osd] * 3 + [jax.ShapeDtypeStruct((D_FF, S), bf16)], sem=("parallel", "parallel"),
        args=(h, w_in, w_in), side=side)


def ffn_du(dy, w_out_t, ua, ub):
    S = dy.shape[0]
    tm = 512

    def body(dy_ref, w_ref, a_ref, b_ref, du_ref):
        dyv = dy_ref[...]
        for j in range(FF_NB):
            cols = slice(j * FF_T, (j + 1) * FF_T)
            ds = _dot(dyv, w_ref[:, cols], NN)
            a, b = a_ref[:, cols].astype(f32), b_ref[:, cols].astype(f32)
            sg = _sig(a)
            du_ref[:, cols] = (ds * b * sg * (1.0 + a * (1.0 - sg))).astype(bf16)
            du_ref[:, D_FF + j * FF_T:D_FF + (j + 1) * FF_T] = (ds * a * sg).astype(bf16)

    half = pl.BlockSpec((tm, D_FF), lambda i: (i, 0))
    return pl.pallas_call(
        body, name="ffn_du", grid=(S // tm,),
        in_specs=[pl.BlockSpec((tm, D), lambda i: (i, 0)), pl.BlockSpec((D, D_FF), lambda i: (0, 0)), half, half],
        out_specs=pl.BlockSpec((tm, 2 * D_FF), lambda i: (i, 0)),
        out_shape=jax.ShapeDtypeStruct((S, 2 * D_FF), bf16), compiler_params=_cp(("parallel",)),
    )(dy, w_out_t, ua, ub)


TR = 512


def _row_spec(cols=D):
    return pl.BlockSpec((TR, cols), lambda i: (i, 0))


def _vec_spec(cols=D):
    return pl.BlockSpec((1, cols), lambda i: (0, 0))


def prenorm(x, a_vec, sh_vec):
    S = x.shape[0]

    def body(x_ref, a_ref, s_ref, h_ref, ht_ref):
        xv = x_ref[...]
        rstd = lax.rsqrt(jnp.mean(xv * xv, axis=1, keepdims=True) + EPS)
        h = (xv * rstd * a_ref[...] + s_ref[...]).astype(bf16)
        h_ref[...] = h
        ht_ref[...] = h.T

    return pl.pallas_call(
        body, name="prenorm", grid=(S // TR,),
        in_specs=[_row_spec(), _vec_spec(), _vec_spec()],
        out_specs=[_row_spec(), pl.BlockSpec((D, TR), lambda i: (0, i))],
        out_shape=[jax.ShapeDtypeStruct((S, D), bf16), jax.ShapeDtypeStruct((D, S), bf16)],
        compiler_params=_cp(("parallel",)),
    )(x, a_vec, sh_vec)


def postnorm(x, y, b_vec):
    S = x.shape[0]

    def body(x_ref, y_ref, b_ref, o_ref):
        yv = y_ref[...]
        rstd = lax.rsqrt(jnp.mean(yv * yv, axis=1, keepdims=True) + EPS)
        o_ref[...] = x_ref[...] + b_ref[...] * (yv * rstd)

    return pl.pallas_call(
        body, name="postnorm", grid=(S // TR,),
        in_specs=[_row_spec(), _row_spec(), _vec_spec()], out_specs=_row_spec(),
        out_shape=jax.ShapeDtypeStruct((S, D), f32), compiler_params=_cp(("parallel",)),
    )(x, y, b_vec)


def post_bwd(dout, y, b_vec):
    S = dout.shape[0]

    def body(d_ref, y_ref, b_ref, dy_ref, db_ref):
        i = pl.program_id(0)
        yv, dv = y_ref[...], d_ref[...]
        rstd = lax.rsqrt(jnp.mean(yv * yv, axis=1, keepdims=True) + EPS)
        yh = yv * rstd
        dyh = dv * b_ref[...]
        dy_ref[...] = (rstd * (dyh - yh * jnp.mean(dyh * yh, axis=1, keepdims=True))).astype(bf16)
        part = jnp.sum(dv * yh, axis=0, keepdims=True)

        @pl.when(i == 0)
        def _():
            db_ref[...] = part

        @pl.when(i > 0)
        def _():
            db_ref[...] += part

    return pl.pallas_call(
        body, name="post_bwd", grid=(S // TR,),
        in_specs=[_row_spec(), _row_spec(), _vec_spec()], out_specs=[_row_spec(), _vec_spec()],
        out_shape=[jax.ShapeDtypeStruct((S, D), bf16), jax.ShapeDtypeStruct((1, D), f32)],
        compiler_params=_cp(("arbitrary",)),
    )(dout, y, b_vec)


def pre_bwd(dout, dh, x, a_vec):
    S = dout.shape[0]

    def body(d_ref, dh_ref, x_ref, a_ref, dx_ref, ds_ref, da_ref):
        i = pl.program_id(0)
        xv, dhv = x_ref[...], dh_ref[...]
        rstd = lax.rsqrt(jnp.mean(xv * xv, axis=1, keepdims=True) + EPS)
        n1 = xv * rstd
        dn = dhv * a_ref[...]
        dx_ref[...] = d_ref[...] + rstd * (dn - n1 * jnp.mean(dn * n1, axis=1, keepdims=True))
        p_s = jnp.sum(dhv, axis=0, keepdims=True)
        p_a = jnp.sum(dhv * n1, axis=0, keepdims=True)

        @pl.when(i == 0)
        def _():
            ds_ref[...] = p_s
            da_ref[...] = p_a

        @pl.when(i > 0)
        def _():
            ds_ref[...] += p_s
            da_ref[...] += p_a

    return pl.pallas_call(
        body, name="pre_bwd", grid=(S // TR,),
        in_specs=[_row_spec(), _row_spec(), _row_spec(), _vec_spec()],
        out_specs=[_row_spec(), _vec_spec(), _vec_spec()],
        out_shape=[jax.ShapeDtypeStruct((S, D), f32), jax.ShapeDtypeStruct((1, D), f32), jax.ShapeDtypeStruct((1, D), f32)],
        compiler_params=_cp(("arbitrary",)),
    )(dout, dh, x, a_vec)


def loss_grad(y, tgt):
    S = y.shape[0]

    def body(y_ref, t_ref, dy_ref, l_ref):
        i = pl.program_id(0)
        e = y_ref[...] - t_ref[...]
        dy_ref[...] = e * (1.0 / D)
        part = jnp.sum(jnp.sum(e * e, axis=1, keepdims=True), axis=0, keepdims=True) * (0.5 / D)
        part = jnp.broadcast_to(part, (8, LANE))

        @pl.when(i == 0)
        def _():
            l_ref[...] = part

        @pl.when(i > 0)
        def _():
            l_ref[...] += part

    return pl.pallas_call(
        body, name="loss_grad", grid=(S // TR,),
        in_specs=[_row_spec(), _row_spec()],
        out_specs=[_row_spec(), pl.BlockSpec((8, LANE), lambda i: (0, 0))],
        out_shape=[jax.ShapeDtypeStruct((S, D), f32), jax.ShapeDtypeStruct((8, LANE), f32)],
        compiler_params=_cp(("arbitrary",)),
    )(y, tgt)


G_NB = D // LANE
G_TR = 2048
G_OFF = O_GATE // LANE


def gate_merge(u, pa, pb, pc):
    S = u.shape[0]

    def body(g0, g1, g2, a, b, c, o_ref, ot_ref):
        m = (_sig(g0[...]) * a[...] + _sig(g1[...]) * b[...] + _sig(g2[...]) * c[...]).astype(bf16)
        o_ref[...] = m
        ot_ref[...] = m.T

    gs = [pl.BlockSpec((G_TR, LANE), functools.partial(lambda i, j, k: (i, G_OFF + G_NB * k + j), k=k)) for k in range(3)]
    ps = pl.BlockSpec((G_TR, LANE), lambda i, j: (i, j))
    return pl.pallas_call(
        body, name="gate_merge", grid=(S // G_TR, G_NB),
        in_specs=gs + [ps, ps, ps], out_specs=[ps, pl.BlockSpec((LANE, G_TR), lambda i, j: (j, i))],
        out_shape=[jax.ShapeDtypeStruct((S, D), bf16), jax.ShapeDtypeStruct((D, S), bf16)],
        compiler_params=_cp(("parallel", "parallel")),
    )(u, u, u, pa, pb, pc)


def gate_bwd(dm, u, pa, pb, pc):
    S = u.shape[0]

    def body(dm_ref, g0, g1, g2, a, b, c, da, db, dc, dg0, dg1, dg2):
        d = dm_ref[...]
        for g, p, dp, dg in ((g0, a, da, dg0), (g1, b, db, dg1), (g2, c, dc, dg2)):
            s = _sig(g[...])
            dp[...] = (d * s).astype(bf16)
            dg[...] = (d * p[...] * s * (1.0 - s)).astype(bf16)

    gs = [pl.BlockSpec((G_TR, LANE), functools.partial(lambda i, j, k: (i, G_OFF + G_NB * k + j), k=k)) for k in range(3)]
    ps = pl.BlockSpec((G_TR, LANE), lambda i, j: (i, j))
    osd = jax.ShapeDtypeStruct((S, D), bf16)
    return pl.pallas_call(
        body, name="gate_bwd", grid=(S // G_TR, G_NB),
        in_specs=[ps] + gs + [ps, ps, ps], out_specs=[ps] * 6, out_shape=[osd] * 6,
        compiler_params=_cp(("parallel", "parallel")),
    )(dm, u, u, u, pa, pb, pc)


A_TB = 512
A_NCH = A_TB // A_CHUNK
A_NSUB = A_CHUNK // A_SUB
A_HP = 6
A_KW, A_VW = A_HP * A_K, A_HP * A_V


def _hgrn_gates(qr, fr, lbh):
    sq = _sig(qr)
    sig = _sig(fr)
    f = lbh + (1.0 - lbh) * sig
    logf = jnp.log(jnp.maximum(f, TINY))
    k = (1.0 - lbh) * (1.0 - sig)
    return qr * sq, sq, sig, f, logf, k


def _hgrn_intra(qf, k, b, causal):
    qts, kts, eqs, eks, blocks = [], [], [], [], []
    for sb in range(A_NSUB):
        rs = sb * A_SUB
        r = b[rs - 1:rs, :] if sb else jnp.zeros((1, A_K), f32)
        eq = jnp.exp(b[rs:rs + A_SUB, :] - r)
        ek = jnp.exp(jnp.minimum(r - b, A_CLAMP))
        qt = (qf[rs:rs + A_SUB, :] * eq).astype(bf16)
        kt = (k * ek).astype(bf16)
        blocks.append(_dot(qt, kt, NT))
        qts.append(qt), kts.append(kt), eqs.append(eq), eks.append(ek)
    a = jnp.where(causal, jnp.concatenate(blocks, axis=0), 0.0)
    return a, qts, kts, eqs, eks


def _tri():
    r = lax.broadcasted_iota(jnp.int32, (A_CHUNK, A_CHUNK), 0)
    c = lax.broadcasted_iota(jnp.int32, (A_CHUNK, A_CHUNK), 1)
    return r >= c


def _hgrn_in_specs(rev_nb=None):
    def im(col):
        if rev_nb is None:
            return lambda p, i: (i, col + p)
        return lambda p, i: (rev_nb - 1 - i, col + p)
    return [pl.BlockSpec((A_TB, A_KW), im(O_AQ // A_KW)), pl.BlockSpec((A_TB, A_KW), im(O_AF // A_KW)),
            pl.BlockSpec((A_TB, A_VW), im(O_AI // A_VW)), pl.BlockSpec((A_TB, A_VW), im(O_AG // A_VW)),
            pl.BlockSpec((1, A_KW), lambda p, i: (0, p)), pl.BlockSpec((1, A_V), lambda p, i: (0, 0))]


def hgrn_fwd(u, lb, ng, side=None):
    S = u.shape[0]
    nb = S // A_TB

    def body(q_ref, f_ref, i_ref, g_ref, lb_ref, ng_ref, o_ref, ya_ref, st_ref, state):
        @pl.when(pl.program_id(1) == 0)
        def _():
            state[...] = jnp.zeros_like(state)

        causal = _tri()
        tri = causal.astype(f32)

        def chunk(n, carry):
            rows = pl.ds(pl.multiple_of(n * A_CHUNK, A_CHUNK), A_CHUNK)
            o_parts, y_parts = [], []
            for hh in range(A_HP):
                ks = slice(hh * A_K, (hh + 1) * A_K)
                vs = slice(hh * A_V, (hh + 1) * A_V)
                qf, _, _, _, logf, k = _hgrn_gates(q_ref[rows, ks], f_ref[rows, ks], lb_ref[:, ks])
                vi = i_ref[rows, vs].astype(bf16)
                gg = g_ref[rows, vs]
                b = _dot(tri, logf, NN, precision=lax.Precision.HIGHEST)
                s0 = state[hh]
                st_ref[n, hh] = s0
                o = _dot((qf * jnp.exp(b)).astype(bf16), s0.astype(bf16), NT)
                a, _, _, _, _ = _hgrn_intra(qf, k, b, causal)
                o = o + _dot(a.astype(bf16), vi, NN)
                bend = b[A_CHUNK - 1:A_CHUNK, :]
                ke = (k * jnp.exp(bend - b)).astype(bf16)
                state[hh] = s0 * jnp.exp(bend) + _dot(vi, ke, TN)
                rstd = lax.rsqrt(jnp.mean(o * o, axis=1, keepdims=True) + EPS)
                o_parts.append(o)
                y_parts.append(o * rstd * ng_ref[...] * (gg * _sig(gg)))
            o_ref[rows, :] = jnp.concatenate(o_parts, axis=1)
            ya_ref[rows, :] = jnp.concatenate(y_parts, axis=1).astype(bf16)
            return carry

        lax.fori_loop(0, A_NCH, chunk, 0)

    return _call(
        body, name="hgrn_fwd", grid=(A_HEADS // A_HP, nb),
        in_specs=_hgrn_in_specs(),
        out_specs=[pl.BlockSpec((A_TB, A_VW), lambda p, i: (i, p)), pl.BlockSpec((A_TB, A_VW), lambda p, i: (i, p)),
                   pl.BlockSpec((A_NCH, A_HP, A_V, A_K), lambda p, i: (i, p, 0, 0))],
        out_shape=[jax.ShapeDtypeStruct((S, 384), f32), jax.ShapeDtypeStruct((S, 384), bf16),
                   jax.ShapeDtypeStruct((S // A_CHUNK, A_HEADS, A_V, A_K), f32)],
        scratch_shapes=[pltpu.VMEM((A_HP, A_V, A_K), f32)],
        sem=("parallel", "arbitrary"), args=(u, u, u, u, lb, ng), side=side)


def hgrn_bwd(u, lb, ng, o, st, dya, side=None):
    S = u.shape[0]
    nb = S // A_TB

    def body(q_ref, f_ref, i_ref, g_ref, lb_ref, ng_ref, o_ref, st_ref, dy_ref,
             dq_ref, df_ref, di_ref, dg_ref, dlb_ref, dng_ref, dstate):
        @pl.when(pl.program_id(1) == 0)
        def _():
            dstate[...] = jnp.zeros_like(dstate)
            dlb_ref[...] = jnp.zeros_like(dlb_ref)
            dng_ref[...] = jnp.zeros_like(dng_ref)

        causal = _tri()
        tri = causal.astype(f32)

        def chunk(it, carry):
            n = A_NCH - 1 - it
            rows = pl.ds(pl.multiple_of(n * A_CHUNK, A_CHUNK), A_CHUNK)
            dq_p, df_p, di_p, dg_p, dlb_p = [], [], [], [], []
            dng_acc = jnp.zeros((1, A_V), f32)
            for hh in range(A_HP):
                ks = slice(hh * A_K, (hh + 1) * A_K)
                vs = slice(hh * A_V, (hh + 1) * A_V)
                lbh = lb_ref[:, ks]
                qr = q_ref[rows, ks]
                qf, sq, sig, f, logf, k = _hgrn_gates(qr, f_ref[rows, ks], lbh)
                vi = i_ref[rows, vs].astype(bf16)
                gg = g_ref[rows, vs]
                b = _dot(tri, logf, NN, precision=lax.Precision.HIGHEST)
                eb = jnp.exp(b)
                bend = b[A_CHUNK - 1:A_CHUNK, :]
                eend = jnp.exp(bend)
                ekend = jnp.exp(bend - b)
                qe = (qf * eb).astype(bf16)
                ke = (k * ekend).astype(bf16)
                s0 = st_ref[n, hh]
                dsend = dstate[hh]
                ov = o_ref[rows, vs]
                dy = dy_ref[rows, vs]
                rstd = lax.rsqrt(jnp.mean(ov * ov, axis=1, keepdims=True) + EPS)
                oh = ov * rstd
                sg = _sig(gg)
                d_on = dy * (gg * sg)
                dg_p.append(dy * oh * ng_ref[...] * (sg * (1.0 + gg * (1.0 - sg))))
                dng_acc = dng_acc + jnp.sum(d_on * oh, axis=0, keepdims=True)
                doh = d_on * ng_ref[...]
                do = (rstd * (doh - oh * jnp.mean(doh * oh, axis=1, keepdims=True))).astype(bf16)
                a, qts, kts, eqs, eks = _hgrn_intra(qf, k, b, causal)
                da = jnp.where(causal, _dot(do, vi, NT), 0.0).astype(bf16)
                dsb = dsend.astype(bf16)
                dv = _dot(a.astype(bf16), do, TN) + _dot(ke, dsb, NT)
                dq = _dot(do, s0.astype(bf16), NN) * eb
                dk_state = _dot(vi, dsb, NN) * ekend
                dk = dk_state
                dq_i = []
                for sb in range(A_NSUB):
                    da_sb = da[sb * A_SUB:(sb + 1) * A_SUB, :]
                    dq_i.append(_dot(da_sb, kts[sb], NN) * eqs[sb])
                    dk = dk + _dot(da_sb, qts[sb], TN) * eks[sb]
                dq = dq + jnp.concatenate(dq_i, axis=0)
                db = qf * dq - k * dk
                extra = jnp.sum(k * dk_state, axis=0, keepdims=True) + eend * jnp.sum(s0 * dsend, axis=0, keepdims=True)
                dlogf = _dot(tri, db, TN, precision=lax.Precision.HIGHEST) + extra
                dstate[hh] = _dot(do, qe, TN) + eend * dsend
                d_pre = jnp.where(f > TINY, dlogf / f, 0.0) - dk
                dlb_p.append(jnp.sum((1.0 - sig) * d_pre, axis=0, keepdims=True))
                df_p.append((1.0 - lbh) * d_pre * sig * (1.0 - sig))
                dq_p.append(dq * (sq * (1.0 + qr * (1.0 - sq))))
                di_p.append(dv)
            dq_ref[rows, :] = jnp.concatenate(dq_p, axis=1).astype(bf16)
            df_ref[rows, :] = jnp.concatenate(df_p, axis=1).astype(bf16)
            di_ref[rows, :] = jnp.concatenate(di_p, axis=1).astype(bf16)
            dg_ref[rows, :] = jnp.concatenate(dg_p, axis=1).astype(bf16)
            dlb_ref[...] += jnp.concatenate(dlb_p, axis=1)
            dng_ref[0] += dng_acc
            return carry

        lax.fori_loop(0, A_NCH, chunk, 0)

    rev = lambda p, i: (nb - 1 - i, p)
    return _call(
        body, name="hgrn_bwd", grid=(A_HEADS // A_HP, nb),
        in_specs=_hgrn_in_specs(nb) + [pl.BlockSpec((A_TB, A_VW), rev),
                                       pl.BlockSpec((A_NCH, A_HP, A_V, A_K), lambda p, i: (nb - 1 - i, p, 0, 0)),
                                       pl.BlockSpec((A_TB, A_VW), rev)],
        out_specs=[pl.BlockSpec((A_TB, A_KW), rev), pl.BlockSpec((A_TB, A_KW), rev),
                   pl.BlockSpec((A_TB, A_VW), rev), pl.BlockSpec((A_TB, A_VW), rev),
                   pl.BlockSpec((1, A_KW), lambda p, i: (0, p)), pl.BlockSpec((1, 1, A_V), lambda p, i: (p, 0, 0))],
        out_shape=[jax.ShapeDtypeStruct((S, 768), bf16), jax.ShapeDtypeStruct((S, 768), bf16),
                   jax.ShapeDtypeStruct((S, 384), bf16), jax.ShapeDtypeStruct((S, 384), bf16),
                   jax.ShapeDtypeStruct((1, 768), f32), jax.ShapeDtypeStruct((A_HEADS // A_HP, 1, A_V), f32)],
        scratch_shapes=[pltpu.VMEM((A_HP, A_V, A_K), f32)],
        sem=("parallel", "arbitrary"), args=(u, u, u, u, lb, ng, o, st, dya), side=side)


B_TK = 128
SCALE = HD ** -0.5


def _split(x):
    hi = x.astype(bf16)
    return hi, (x - hi.astype(f32)).astype(bf16)


def _dot2(x, m, dn):
    hi, lo = _split(x)
    return _dot(hi, m, dn) + _dot(lo, m, dn)


def _sb_block(qs, kh, mask, m_gt, c):
    z = _dot(qs, kh, NT)
    sp = jnp.maximum(z, 0.0) + jnp.log(1.0 + jnp.exp(-jnp.abs(z)))
    lneg = jnp.where(mask, -sp, 0.0)
    lsz = z - sp
    suf = _dot2(lneg, m_gt, NN) + c
    a = jnp.where(mask, jnp.exp(lsz + suf), 0.0)
    return lneg, lsz, a


def _sb_masks(tq, i, jj):
    t_idx = i * tq + lax.broadcasted_iota(jnp.int32, (tq, B_TK), 0)
    s_idx = jj * B_TK + lax.broadcasted_iota(jnp.int32, (tq, B_TK), 1)
    return s_idx < t_idx


def _sb_tri(strict):
    r = lax.broadcasted_iota(jnp.int32, (B_TK, B_TK), 0)
    c = lax.broadcasted_iota(jnp.int32, (B_TK, B_TK), 1)
    return (r > c if strict else r >= c).astype(bf16)


B_DEAD = -88.0


def _sb_walk(nkb, step, init):
    def cond(state):
        it, alive, _ = state
        return jnp.logical_and(it < nkb, alive)

    def body(state):
        it, _, carry = state
        carry = step(it, carry)
        top = jnp.max(functools.reduce(jnp.maximum, [h[1] for h in carry]))
        return it + 1, top > B_DEAD, carry

    return lax.while_loop(cond, body, (jnp.int32(0), jnp.bool_(True), init))[2]


B_HP = 6
B_W = B_HP * HD


def sb_fwd(u, side=None):
    S = u.shape[0]
    tq = 128

    def body(q_ref, k_ref, v_ref, o_ref):
        i = pl.program_id(1)
        nkb = (i + 1) * (tq // B_TK)
        m_gt = _sb_tri(True)
        qs = [(q_ref[:, hh * HD:(hh + 1) * HD] * SCALE).astype(bf16) for hh in range(B_HP)]

        def step(it, carry):
            jj = nkb - 1 - it
            rows = pl.ds(pl.multiple_of(jj * B_TK, B_TK), B_TK)
            mask = _sb_masks(tq, i, jj)
            kb, vb = k_ref[rows, :], v_ref[rows, :]
            out = []
            for hh in range(B_HP):
                acc, c = carry[hh]
                kh = kb[:, hh * HD:(hh + 1) * HD].astype(bf16)
                vh = vb[:, hh * HD:(hh + 1) * HD].astype(bf16)
                lneg, _, a = _sb_block(qs[hh], kh, mask, m_gt, c)
                out.append((acc + _dot2(a, vh, NN), c + jnp.sum(lneg, axis=1, keepdims=True)))
            return tuple(out)

        z0 = (jnp.zeros((tq, HD), f32), jnp.zeros((tq, 1), f32))
        res = _sb_walk(nkb, step, (z0,) * B_HP)
        o_ref[...] = jnp.concatenate([r[0] for r in res], axis=1)

    outs, souts = _call(
        body, name="sb_fwd", grid=(B_HEADS // B_HP, S // tq),
        in_specs=[pl.BlockSpec((tq, B_W), lambda p, i: (i, O_BQ // B_W + p)),
                  pl.BlockSpec((S, B_W), lambda p, i: (0, O_BK // B_W + p)),
                  pl.BlockSpec((S, B_W), lambda p, i: (0, O_BV // B_W + p))],
        out_specs=[pl.BlockSpec((tq, B_W), lambda p, i: (i, p))],
        out_shape=[jax.ShapeDtypeStruct((S, 384), f32)],
        sem=("parallel", "arbitrary"), args=(u, u, u), side=side)
    return outs[0], souts


def sb_bwd(u, yb, dyb, side=None):
    S = u.shape[0]
    tq = 128
    nq = S // tq

    def body(q_ref, k_ref, v_ref, y_ref, dy_ref, dq_ref, dk_out, dv_out, dk_ref, dv_ref, out_sem):
        p, i = pl.program_id(0), pl.program_id(1)

        @pl.when(i == 0)
        def _():
            dk_ref[...] = jnp.zeros_like(dk_ref)
            dv_ref[...] = jnp.zeros_like(dv_ref)

        nkb = (i + 1) * (tq // B_TK)
        m_gt = _sb_tri(True)
        m_ge = _sb_tri(False)
        qs, dos, tot = [], [], []
        for hh in range(B_HP):
            hs = slice(hh * HD, (hh + 1) * HD)
            qs.append((q_ref[:, hs] * SCALE).astype(bf16))
            dob = dy_ref[:, hs].astype(bf16)
            dos.append(dob)
            tot.append(jnp.sum(dob.astype(f32) * y_ref[:, hs], axis=1, keepdims=True))

        def step(it, carry):
            jj = nkb - 1 - it
            rows = pl.ds(pl.multiple_of(jj * B_TK, B_TK), B_TK)
            mask = _sb_masks(tq, i, jj)
            kb, vb = k_ref[rows, :], v_ref[rows, :]
            out, dk_p, dv_p = [], [], []
            for hh in range(B_HP):
                dq, c, cg = carry[hh]
                kh = kb[:, hh * HD:(hh + 1) * HD].astype(bf16)
                vh = vb[:, hh * HD:(hh + 1) * HD].astype(bf16)
                lneg, lsz, a = _sb_block(qs[hh], kh, mask, m_gt, c)
                g = a * _dot(dos[hh], vh, NT)
                pre = tot[hh] - cg - _dot2(g, m_ge, NN)
                beta = jnp.exp(lsz)
                dz = jnp.where(mask, g * (1.0 - beta) - beta * pre, 0.0).astype(bf16)
                dk_p.append(_dot(dz, qs[hh], TN))
                dv_p.append(_dot(a.astype(bf16), dos[hh], TN))
                out.append((dq + _dot(dz, kh, NN), c + jnp.sum(lneg, axis=1, keepdims=True),
                            cg + jnp.sum(g, axis=1, keepdims=True)))
            dk_ref[rows, :] += jnp.concatenate(dk_p, axis=1)
            dv_ref[rows, :] += jnp.concatenate(dv_p, axis=1)
            return tuple(out)

        z0 = (jnp.zeros((tq, HD), f32), jnp.zeros((tq, 1), f32), jnp.zeros((tq, 1), f32))
        res = _sb_walk(nkb, step, (z0,) * B_HP)
        dq_ref[...] = jnp.concatenate([r[0] for r in res], axis=1) * SCALE

        @pl.when(i == nq - 1)
        def _():
            cols = pl.ds(pl.multiple_of(p * B_W, LANE), B_W)
            ck = pltpu.make_async_copy(dk_ref, dk_out.at[:, cols], out_sem.at[0])
            cv = pltpu.make_async_copy(dv_ref, dv_out.at[:, cols], out_sem.at[1])
            ck.start()
            cv.start()
            ck.wait()
            cv.wait()

    row = pl.BlockSpec((tq, B_W), lambda p, i: (i, p))
    hbm = pl.BlockSpec(memory_space=pl.ANY)
    osd = jax.ShapeDtypeStruct((S, 384), f32)
    return _call(
        body, name="sb_bwd", grid=(B_HEADS // B_HP, nq),
        in_specs=[pl.BlockSpec((tq, B_W), lambda p, i: (i, O_BQ // B_W + p)),
                  pl.BlockSpec((S, B_W), lambda p, i: (0, O_BK // B_W + p)),
                  pl.BlockSpec((S, B_W), lambda p, i: (0, O_BV // B_W + p)), row, row],
        out_specs=[row, hbm, hbm], out_shape=[osd, osd, osd],
        scratch_shapes=[pltpu.VMEM((S, B_W), f32), pltpu.VMEM((S, B_W), f32), pltpu.SemaphoreType.DMA((2,))],
        sem=("parallel", "arbitrary"), args=(u, u, u, yb, dyb), side=side)


def _dil_rows(i, rho, r):
    if r == 1:
        return pl.ds(pl.multiple_of(i * C_BLK, C_BLK), C_BLK)
    return pl.ds(i * (C_BLK * r) + rho, C_BLK, stride=r)


def _dil_scores(qs, kc, kp, i, slope_r):
    qi = lax.broadcasted_iota(jnp.int32, (C_BLK, C_BLK), 0)
    kj = lax.broadcasted_iota(jnp.int32, (C_BLK, C_BLK), 1)
    d_c = qi - kj
    d_p = d_c + C_BLK
    ok_c = d_c >= 0
    ok_p = jnp.logical_and(d_c <= 0, i > 0)
    s_c = jnp.where(ok_c, _dot(qs, kc, NT) - slope_r * d_c.astype(f32), NEG_BIG)
    s_p = jnp.where(ok_p, _dot(qs, kp, NT) - slope_r * d_p.astype(f32), NEG_BIG)
    return s_c, s_p, ok_c, ok_p


def _dil_slope(g, r, hh):
    pair = pl.program_id(0)
    return jnp.where(pair == 0, C_SLOPES[4 * g + hh] * r, C_SLOPES[4 * g + 2 + hh] * r).astype(f32)


def _dil_u_specs(g, S):
    def im(off):
        return lambda p, rho: (0, (off + g * 256) // LANE + p)
    return [pl.BlockSpec((S, LANE), im(O_CQ)), pl.BlockSpec((S, LANE), im(O_CK)), pl.BlockSpec((S, LANE), im(O_CV))]


def dil_fwd(u, g, side=None):
    S = u.shape[0]
    r = C_GROUPS[g][1]
    nbk = S // r // C_BLK

    def body(q_ref, k_ref, v_ref, o_ref, l_ref):
        rho = pl.program_id(1)

        def step(i, carry):
            rc = _dil_rows(i, rho, r)
            rp = _dil_rows(jnp.maximum(i - 1, 0), rho, r)
            q2, kc2, kp2, vc2, vp2 = q_ref[rc, :], k_ref[rc, :], k_ref[rp, :], v_ref[rc, :], v_ref[rp, :]
            o_p, l_p = [], []
            for hh in range(2):
                hs = slice(hh * HD, (hh + 1) * HD)
                qs = (q2[:, hs] * SCALE).astype(bf16)
                kc, kp = kc2[:, hs].astype(bf16), kp2[:, hs].astype(bf16)
                vc, vp = vc2[:, hs].astype(bf16), vp2[:, hs].astype(bf16)
                s_c, s_p, _, _ = _dil_scores(qs, kc, kp, i, _dil_slope(g, r, hh))
                m = jnp.maximum(jnp.max(s_c, axis=1, keepdims=True), jnp.max(s_p, axis=1, keepdims=True))
                p_c, p_p = jnp.exp(s_c - m), jnp.exp(s_p - m)
                den = jnp.sum(p_c, axis=1, keepdims=True) + jnp.sum(p_p, axis=1, keepdims=True)
                o_p.append((_dot(p_c.astype(bf16), vc, NN) + _dot(p_p.astype(bf16), vp, NN)) / den)
                l_p.append(jnp.broadcast_to(m + jnp.log(den), (C_BLK, HD)))
            o_ref[rc, :] = jnp.concatenate(o_p, axis=1)
            l_ref[rc, :] = jnp.concatenate(l_p, axis=1)
            return carry

        lax.fori_loop(0, nbk, step, 0, unroll=2)

    ospec = pl.BlockSpec((S, LANE), lambda p, rho: (0, p))
    osd = jax.ShapeDtypeStruct((S, 256), f32)
    return _call(
        body, name=f"dil_fwd{g}", grid=(2, r),
        in_specs=_dil_u_specs(g, S), out_specs=[ospec, ospec], out_shape=[osd, osd],
        sem=("parallel", "arbitrary"), args=(u, u, u), side=side)


def dil_merge(os_, ls_):
    S = os_[0].shape[0]

    def body(o0, o1, o2, l0, l1, l2, y_ref, lse_ref):
        a, b, c = l0[...], l1[...], l2[...]
        m = jnp.maximum(jnp.maximum(a, b), c)
        ea, eb, ec = jnp.exp(a - m), jnp.exp(b - m), jnp.exp(c - m)
        den = ea + eb + ec
        y_ref[...] = (ea * o0[...] + eb * o1[...] + ec * o2[...]) / den
        lse_ref[...] = m + jnp.log(den)

    spec = pl.BlockSpec((512, 256), lambda i: (i, 0))
    osd = jax.ShapeDtypeStruct((S, 256), f32)
    return pl.pallas_call(
        body, name="dil_merge", grid=(S // 512,), in_specs=[spec] * 6, out_specs=[spec, spec],
        out_shape=[osd, osd], compiler_params=_cp(("parallel",)),
    )(*os_, *ls_)


def dil_bwd(u, g, dyc, yc, lse):
    S = u.shape[0]
    r = C_GROUPS[g][1]
    nbk = S // r // C_BLK

    def body(q_ref, k_ref, v_ref, dy_ref, y_ref, l_ref, dq_ref, dk_ref, dv_ref):
        rho = pl.program_id(1)

        @pl.when(rho == 0)
        def _():
            dk_ref[...] = jnp.zeros_like(dk_ref)
            dv_ref[...] = jnp.zeros_like(dv_ref)

        def step(i, carry):
            rc = _dil_rows(i, rho, r)
            rp = _dil_rows(jnp.maximum(i - 1, 0), rho, r)
            q2, kc2, kp2, vc2, vp2 = q_ref[rc, :], k_ref[rc, :], k_ref[rp, :], v_ref[rc, :], v_ref[rp, :]
            dy2, y2, l2 = dy_ref[rc, :], y_ref[rc, :], l_ref[rc, :]
            dq_p, dkc_p, dkp_p, dvc_p, dvp_p = [], [], [], [], []
            for hh in range(2):
                hs = slice(hh * HD, (hh + 1) * HD)
                qs = (q2[:, hs] * SCALE).astype(bf16)
                kc, kp = kc2[:, hs].astype(bf16), kp2[:, hs].astype(bf16)
                vc, vp = vc2[:, hs].astype(bf16), vp2[:, hs].astype(bf16)
                dy = dy2[:, hs]
                dyb = dy.astype(bf16)
                s_c, s_p, ok_c, ok_p = _dil_scores(qs, kc, kp, i, _dil_slope(g, r, hh))
                lrow = l2[:, hh * HD:hh * HD + 1]
                delta = jnp.sum(dy * y2[:, hs], axis=1, keepdims=True)
                pi_c = jnp.where(ok_c, jnp.exp(s_c - lrow), 0.0)
                pi_p = jnp.where(ok_p, jnp.exp(s_p - lrow), 0.0)
                ds_c = (pi_c * (_dot(dyb, vc, NT) - delta)).astype(bf16)
                ds_p = (pi_p * (_dot(dyb, vp, NT) - delta)).astype(bf16)
                dq_p.append((_dot(ds_c, kc, NN) + _dot(ds_p, kp, NN)) * SCALE)
                dkc_p.append(_dot(ds_c, qs, TN))
                dkp_p.append(_dot(ds_p, qs, TN))
                dvc_p.append(_dot(pi_c.astype(bf16), dyb, TN))
                dvp_p.append(_dot(pi_p.astype(bf16), dyb, TN))
            dq_ref[rc, :] = jnp.concatenate(dq_p, axis=1)
            dk_ref[rc, :] += jnp.concatenate(dkc_p, axis=1)
            dv_ref[rc, :] += jnp.concatenate(dvc_p, axis=1)
            dk_ref[rp, :] += jnp.concatenate(dkp_p, axis=1)
            dv_ref[rp, :] += jnp.concatenate(dvp_p, axis=1)
            return carry

        lax.fori_loop(0, nbk, step, 0, unroll=2)

    ospec = pl.BlockSpec((S, LANE), lambda p, rho: (0, p))
    osd = jax.ShapeDtypeStruct((S, 256), f32)
    return pl.pallas_call(
        body, name=f"dil_bwd{g}", grid=(2, r),
        in_specs=_dil_u_specs(g, S) + [ospec, ospec, ospec], out_specs=[ospec] * 3, out_shape=[osd] * 3,
        compiler_params=_cp(("parallel", "arbitrary")),
    )(u, u, u, dyc, yc, lse)


def _rows_tile(rows):
    return _tile(rows, (256, 176, 128, 64, 32, 16, 8))


def cast_bf16(w):
    shape = w.shape
    w2 = w.reshape(-1, shape[-1])
    rows, cols = w2.shape
    tr = _rows_tile(rows)

    def body(x_ref, o_ref):
        o_ref[...] = x_ref[...].astype(bf16)

    spec = pl.BlockSpec((tr, cols), lambda i: (i, 0))
    out = pl.pallas_call(
        body, name="cast_bf16", grid=(rows // tr,), in_specs=[spec], out_specs=spec,
        out_shape=jax.ShapeDtypeStruct((rows, cols), bf16), compiler_params=_cp(("parallel",)),
    )(w2)
    return out.reshape(shape)


BC1 = 1.0 - ADAM_B1 ** ADAM_STEP
BC2 = 1.0 - ADAM_B2 ** ADAM_STEP


def _adam_math(w, g, m, v):
    m2 = ADAM_B1 * m + (1.0 - ADAM_B1) * g
    v2 = ADAM_B2 * v + (1.0 - ADAM_B2) * (g * g)
    delta = -ADAM_LR * ((m2 / BC1) / (jnp.sqrt(v2 / BC2) + ADAM_EPS) + ADAM_WD * w)
    return delta, m2, v2


def adam(w, g, m, v):
    shape = w.shape
    r2 = lambda t: t.reshape(-1, shape[-1])
    rows, cols = r2(w).shape
    tr = _rows_tile(rows)

    def body(w_ref, g_ref, m_ref, v_ref, d_ref, m2_ref, v2_ref):
        d_ref[...], m2_ref[...], v2_ref[...] = _adam_math(w_ref[...], g_ref[...], m_ref[...], v_ref[...])

    spec = pl.BlockSpec((tr, cols), lambda i: (i, 0))
    osd = jax.ShapeDtypeStruct((rows, cols), f32)
    outs = pl.pallas_call(
        body, name="adam", grid=(rows // tr,), in_specs=[spec] * 4, out_specs=[spec] * 3, out_shape=[osd] * 3,
        compiler_params=_cp(("parallel",)),
    )(r2(w), r2(g), r2(m), r2(v))
    return [o.reshape(shape) for o in outs]


def adam_layers(w, gs, m, v):
    depth, rows, cols = w.shape
    tr = _tile(rows, (128, 64, 32, 16, 8))
    nb = rows // tr

    def body(w_ref, m_ref, v_ref, *rest):
        g_refs, (g_out, d_ref, m2_ref, v2_ref) = rest[:depth], rest[depth:]
        for k in range(depth):
            @pl.when(pl.program_id(0) == k)
            def _(k=k):
                g = g_refs[k][...]
                g_out[0] = g
                d_ref[0], m2_ref[0], v2_ref[0] = _adam_math(w_ref[0], g, m_ref[0], v_ref[0])

    def g_spec(k):
        return pl.BlockSpec((tr, cols), lambda l, i: (jnp.where(l < k, 0, jnp.where(l > k, nb - 1, i)), 0))

    wspec = pl.BlockSpec((1, tr, cols), lambda l, i: (l, i, 0))
    osd = jax.ShapeDtypeStruct(w.shape, f32)
    return pl.pallas_call(
        body, name="adam_layers", grid=(depth, nb), in_specs=[wspec] * 3 + [g_spec(k) for k in range(depth)],
        out_specs=[wspec] * 4, out_shape=[osd] * 4, compiler_params=_cp(("arbitrary", "arbitrary")),
    )(w, m, v, *gs)


ADA_N = 9 * D // 4
ADA_TN = 384


def ada_fwd(c_all, w_ada):
    def body(c_ref, w_ref, o_ref):
        cv = c_ref[...]
        o_ref[0] = _dot((cv * _sig(cv)).astype(bf16), w_ref[0].astype(bf16), NN)

    return pl.pallas_call(
        body, name="ada_fwd", grid=(DEPTH, ADA_N // ADA_TN),
        in_specs=[pl.BlockSpec((8, D), lambda l, j: (0, 0)), pl.BlockSpec((1, D, ADA_TN), lambda l, j: (l, 0, j))],
        out_specs=pl.BlockSpec((1, 8, ADA_TN), lambda l, j: (l, 0, j)),
        out_shape=jax.ShapeDtypeStruct((DEPTH, 8, ADA_N), f32), compiler_params=_cp(("parallel", "parallel")),
    )(c_all, w_ada)


def ada_bwd_adam(c_all, dm, w, m, v, side=None):
    tr = 128

    def body(c_ref, dm_ref, w_ref, m_ref, v_ref, g_ref, d_ref, m2_ref, v2_ref):
        cv = c_ref[...]
        g = _dot((cv * _sig(cv)).astype(bf16), dm_ref[0].astype(bf16), TN)
        g_ref[0] = g
        d_ref[0], m2_ref[0], v2_ref[0] = _adam_math(w_ref[0], g, m_ref[0], v_ref[0])

    wspec = pl.BlockSpec((1, tr, ADA_N), lambda l, i: (l, i, 0))
    osd = jax.ShapeDtypeStruct((DEPTH, D, ADA_N), f32)
    return _call(
        body, name="ada_bwd_adam", grid=(DEPTH, D // tr),
        in_specs=[pl.BlockSpec((8, tr), lambda l, i: (0, i)), pl.BlockSpec((1, 8, ADA_N), lambda l, i: (l, 0, 0)),
                  wspec, wspec, wspec],
        out_specs=[wspec] * 4, out_shape=[osd] * 4, sem=("parallel", "parallel"), args=(c_all, dm, w, m, v), side=side)


def _lb_probs(x):
    mx = jnp.max(x, axis=0, keepdims=True)
    e = jnp.exp(x - mx)
    return e / jnp.sum(e, axis=0, keepdims=True)


def lb_fwd(logits):
    def body(x_ref, o_ref):
        p = _lb_probs(x_ref[...])
        rows = [jnp.zeros((1, 768), f32)]
        for l in range(1, DEPTH):
            rows.append(rows[-1] + p[l:l + 1, :])
        o_ref[...] = jnp.concatenate(rows, axis=0)

    return pl.pallas_call(body, name="lb_fwd", out_shape=jax.ShapeDtypeStruct((DEPTH, 768), f32))(logits)


def lb_bwd(logits, dlb):
    def body(x_ref, d_ref, o_ref):
        p = _lb_probs(x_ref[...])
        d = d_ref[...]
        rows = [jnp.zeros((1, 768), f32)] * DEPTH
        acc = jnp.zeros((1, 768), f32)
        for l in range(DEPTH - 1, 0, -1):
            acc = acc + d[l:l + 1, :]
            rows[l] = acc
        dp = jnp.concatenate(rows, axis=0)
        o_ref[...] = p * (dp - jnp.sum(p * dp, axis=0, keepdims=True))

    return pl.pallas_call(body, name="lb_bwd", out_shape=jax.ShapeDtypeStruct((DEPTH, 768), f32))(logits, dlb)


def sum_slots(x):
    n, rows, cols = x.shape
    tr = _rows_tile(rows)

    def body(x_ref, o_ref):
        acc = x_ref[0]
        for j in range(1, n):
            acc = acc + x_ref[j]
        o_ref[...] = acc

    return pl.pallas_call(
        body, name="sum_slots", grid=(rows // tr,),
        in_specs=[pl.BlockSpec((n, tr, cols), lambda i: (0, i, 0))], out_specs=pl.BlockSpec((tr, cols), lambda i: (i, 0)),
        out_shape=jax.ShapeDtypeStruct((rows, cols), f32), compiler_params=_cp(("parallel",)),
    )(x)


ANY = pl.BlockSpec(memory_space=pl.ANY)
CHIP_FLIPS = ((1, 0), (0, 1), (1, 1))
DEV_FLIPS = tuple((a, b, d) for a in (0, 1) for b in (0, 1) for d in (0, 1))[1:]


def _me():
    return lax.axis_index("x"), lax.axis_index("y"), lax.axis_index("c")


def _flip(v, f):
    return 1 - v if f else v


def _comm_call(body, name, ins, out_shapes, n_remote, n_local):
    return pl.pallas_call(
        body, name=name, in_specs=[ANY] * len(ins), out_specs=[ANY] * len(out_shapes), out_shape=out_shapes,
        scratch_shapes=[pltpu.SemaphoreType.DMA((n_remote,)), pltpu.SemaphoreType.DMA((n_remote,)),
                        pltpu.SemaphoreType.DMA((max(n_local, 1),))],
    )(*ins)


def run_plan(plan, name):
    ni, no = len(plan.ins), len(plan.outs)

    def body(*refs):
        ins, outs, sems = refs[:ni], refs[ni:ni + no], refs[ni + no:]
        plan.start(ins, outs, *sems)
        plan.wait(ins, outs, *sems)

    return pl.pallas_call(body, name=name, in_specs=[ANY] * ni, out_specs=[ANY] * no, out_shape=list(plan.outs),
                          scratch_shapes=plan.sems())(*plan.ins)


def gather_chips_plan(arrs, layer=None):
    n = len(arrs)
    layers = list(layer) if isinstance(layer, (list, tuple)) else [layer] * n
    shapes = [a.shape if l is None else a.shape[1:] for a, l in zip(arrs, layers)]

    def copies(ins, outs, send, recv, loc):
        x, y, c = _me()
        mine = 2 * x + y
        srcs = [r if l is None else r.at[l] for r, l in zip(ins, layers)]
        locs = [pltpu.make_async_copy(srcs[a], outs[a].at[mine], loc.at[a]) for a in range(n)]

        def remote(a, k, slot):
            fx, fy = CHIP_FLIPS[k]
            return pltpu.make_async_remote_copy(srcs[a], outs[a].at[slot], send.at[3 * a + k], recv.at[3 * a + k],
                                                device_id=(_flip(x, fx), _flip(y, fy), c), device_id_type=MESH)

        peers = [2 * _flip(x, fx) + _flip(y, fy) for fx, fy in CHIP_FLIPS]
        return locs, remote, mine, peers

    def start(ins, outs, send, recv, loc):
        locs, remote, mine, _ = copies(ins, outs, send, recv, loc)
        for cp in locs:
            cp.start()
        for a in range(n):
            for k in range(3):
                remote(a, k, mine).start()

    def wait(ins, outs, send, recv, loc):
        locs, remote, _, peers = copies(ins, outs, send, recv, loc)
        for a in range(n):
            for k in range(3):
                cp = remote(a, k, peers[k])
                cp.wait_recv()
                cp.wait_send()
        for cp in locs:
            cp.wait()

    outs = [jax.ShapeDtypeStruct((4,) + tuple(s), a.dtype) for s, a in zip(shapes, arrs)]
    return Plan(list(arrs), outs, 3 * n, n, start, wait)


def all_gather_chips(arrs, layer=None, name="ag4"):
    return run_plan(gather_chips_plan(arrs, layer), name)


def all_gather_devs(arr, name="ag8"):
    def body(in_ref, out_ref, send, recv, loc):
        x, y, c = _me()
        mine = 4 * x + 2 * y + c
        lc = pltpu.make_async_copy(in_ref, out_ref.at[mine], loc.at[0])
        lc.start()

        def remote(k, slot):
            fx, fy, fc = DEV_FLIPS[k]
            return pltpu.make_async_remote_copy(in_ref, out_ref.at[slot], send.at[k], recv.at[k],
                                                device_id=(_flip(x, fx), _flip(y, fy), _flip(c, fc)), device_id_type=MESH)

        for k in range(7):
            remote(k, mine).start()
        for k, (fx, fy, fc) in enumerate(DEV_FLIPS):
            cp = remote(k, 4 * _flip(x, fx) + 2 * _flip(y, fy) + _flip(c, fc))
            cp.wait_recv()
            cp.wait_send()
        lc.wait()

    return _comm_call(body, name, [arr], [jax.ShapeDtypeStruct((8,) + arr.shape, arr.dtype)], 7, 1)[0]


def _rows_of(which, rows):
    return pl.ds(pl.multiple_of(which * rows, 16), rows)


def dev_exchange_plan(parts):
    n = len(parts)

    def copies(ins, outs, send, recv, loc):
        x, y, c = _me()
        mine = 4 * x + 2 * y + c

        def piece(a, px, py, pc):
            rows = ins[a].shape[1] // 2
            return ins[a].at[2 * px + py, _rows_of(pc, rows), :]

        locs = [pltpu.make_async_copy(piece(a, x, y, c), outs[a].at[mine], loc.at[a]) for a in range(n)]

        def remote(a, k, slot):
            fx, fy, fc = DEV_FLIPS[k]
            px, py, pc = _flip(x, fx), _flip(y, fy), _flip(c, fc)
            return pltpu.make_async_remote_copy(piece(a, px, py, pc), outs[a].at[slot], send.at[7 * a + k], recv.at[7 * a + k],
                                                device_id=(px, py, pc), device_id_type=MESH)

        peers = [4 * _flip(x, fx) + 2 * _flip(y, fy) + _flip(c, fc) for fx, fy, fc in DEV_FLIPS]
        return locs, remote, mine, peers

    def start(ins, outs, send, recv, loc):
        locs, remote, mine, _ = copies(ins, outs, send, recv, loc)
        for cp in locs:
            cp.start()
        for a in range(n):
            for k in range(7):
                remote(a, k, mine).start()

    def wait(ins, outs, send, recv, loc):
        locs, remote, _, peers = copies(ins, outs, send, recv, loc)
        for a in range(n):
            for k in range(7):
                cp = remote(a, k, peers[k])
                cp.wait_recv()
                cp.wait_send()
        for cp in locs:
            cp.wait()

    outs = [jax.ShapeDtypeStruct((8, p.shape[1] // 2, p.shape[2]), p.dtype) for p in parts]
    return Plan(list(parts), outs, 7 * n, n, start, wait)


def sum_share(slots, name="rs_sum"):
    n, r, cols = slots.shape
    tr = _tile(r, (128, 176, 64))
    steps = r // tr

    def body(s_ref, g_ref, buf, send, loc, recv):
        i = pl.program_id(0)
        x, y, c = _me()
        slot = i % 2

        def copies(step, sl):
            rows = pl.ds(pl.multiple_of(c * r + step * tr, 8), tr)
            rem = pltpu.make_async_remote_copy(buf.at[sl], g_ref.at[rows, :], send.at[sl], recv.at[0],
                                               device_id=(x, y, 1 - c), device_id_type=MESH)
            return rem, pltpu.make_async_copy(buf.at[sl], g_ref.at[rows, :], loc.at[sl])

        @pl.when(i >= 2)
        def _():
            rem, lc = copies(i - 2, slot)
            rem.wait_send()
            lc.wait()

        acc = s_ref[0].astype(f32)
        for j in range(1, n):
            acc = acc + s_ref[j].astype(f32)
        buf[slot] = acc
        rem, lc = copies(i, slot)
        rem.start()
        lc.start()

        @pl.when(i == steps - 1)
        def _():
            for back in range(min(2, steps)):
                rem, lc = copies(i - back, (i - back) % 2)
                rem.wait_send()
                lc.wait()
            other = g_ref.at[pl.ds(pl.multiple_of((1 - c) * r, 8), r), :]
            pltpu.make_async_remote_copy(other, other, send.at[0], recv.at[0],
                                         device_id=(x, y, 1 - c), device_id_type=MESH).wait_recv()

    return pl.pallas_call(
        body, name=name, grid=(steps,),
        in_specs=[pl.BlockSpec((n, tr, cols), lambda i: (0, i, 0))], out_specs=ANY,
        out_shape=jax.ShapeDtypeStruct((2 * r, cols), f32),
        scratch_shapes=[pltpu.VMEM((2, tr, cols), f32), pltpu.SemaphoreType.DMA((2,)), pltpu.SemaphoreType.DMA((2,)),
                        pltpu.SemaphoreType.DMA((1,))],
        compiler_params=_cp(("arbitrary",)),
    )(slots)


BIG = ("ffn1_w_in", "ffn1_w_out", "w_in", "w_branch_a", "w_branch_b", "w_branch_c", "w_out", "ffn2_w_in", "ffn2_w_out")
ROW_SHARDED = ("ffn1_w_out", "w_out", "ffn2_w_out")
RES_W = (0.5, 1.0, 0.5)


def _full_weight(name, g):
    if name in ROW_SHARDED:
        return g.reshape(4 * g.shape[1], g.shape[2])
    return jnp.concatenate([g[0], g[1], g[2], g[3]], axis=1)


def _by_shard(name, dw):
    if name in ROW_SHARDED:
        return dw.reshape(4, dw.shape[0] // 4, dw.shape[1])
    return dw.reshape(dw.shape[0], 4, dw.shape[1] // 4).transpose(1, 0, 2)


def _full_weight_t(name, g):
    if name in ROW_SHARDED:
        return g.reshape(4 * g.shape[1], g.shape[2]).T
    return g.transpose(0, 2, 1).reshape(4 * g.shape[2], g.shape[1])


def _ffn_fwd(x, w_in, w_out, a_vec, sh_vec, b_vec, plans=(None, None)):
    h, h_t = prenorm(x, a_vec, sh_vec)
    (ua, ub, s, s_t), side0 = ffn_in_swiglu(h, w_in, side=plans[0])
    y = mm(s, w_out, name="ffn_out", side=plans[1])
    side1 = None
    if plans[1] is not None:
        y, side1 = y
    return postnorm(x, y, b_vec), (x, h_t, ua, ub, s_t, y), (side0, side1)


def _ffn_bwd(dout, saved, w_in_t, w_out_t, a_vec, b_vec, plans=None):
    x, h_t, ua, ub, s_t, y = saved
    riders = plans or (None, None, None)
    dy, db = post_bwd(dout, y, b_vec)
    dw_out = mm(s_t, dy, out_dtype=bf16, name="ffn_dwo", side=riders[0])
    du = ffn_du(dy, w_out_t, ua, ub)
    dh = mm(du, w_in_t, name="ffn_dh", side=riders[1])
    sides = []
    if plans:
        (dw_out, s0), (dh, s1) = dw_out, dh
        sides = [s0, s1]
    dw_in = mm(h_t, du, out_dtype=bf16, name="ffn_dwi", side=riders[2])
    if plans:
        dw_in, s2 = dw_in
        sides.append(s2)
    dx, dsh, da = pre_bwd(dout, dh, x, a_vec)
    return (dx, dw_in, dw_out, dsh, da, db) + ((sides,) if plans else ())


def _mix_fwd(x, w, lb, ng, a_vec, sh_vec, b_vec, plans=(None,) * 6):
    h, h_t = prenorm(x, a_vec, sh_vec)
    u = mm(h, w["w_in"], name="mix_in", side=plans[0])
    side0 = None
    if plans[0] is not None:
        u, side0 = u
    (o, ya, st), side1 = hgrn_fwd(u, lb, ng, side=plans[1])
    yb, side2 = sb_fwd(u, side=plans[2])
    groups, dil_sides = [], []
    for g in range(3):
        og, sg = dil_fwd(u, g, side=plans[3 + g])
        groups.append(og)
        dil_sides.append(sg)
    yc, lse = dil_merge([o_ for o_, _ in groups], [l_ for _, l_ in groups])
    pa = mm(ya, w["w_branch_a"], name="mix_pa")
    pb = mm(yb, w["w_branch_b"], name="mix_pb")
    pc = mm(yc, w["w_branch_c"], name="mix_pc")
    merged, merged_t = gate_merge(u, pa, pb, pc)
    z = mm(merged, w["w_out"], name="mix_out")
    return (postnorm(x, z, b_vec), (x, h_t, u, o, ya, st, yb, yc, lse, pa, pb, pc, merged_t, z),
            (side0, side1, side2, *dil_sides))


def _mix_bwd(dout, saved, wt, lb, ng, a_vec, b_vec, plans=(None,) * 3):
    x, h_t, u, o, ya, st, yb, yc, lse, pa, pb, pc, merged_t, z = saved
    dz, db = post_bwd(dout, z, b_vec)
    dmerged = mm(dz, wt["w_out"], name="mix_dm")
    dw_out = mm(merged_t, dz, out_dtype=bf16, name="mix_dwo")
    dpa, dpb, dpc, dg0, dg1, dg2 = gate_bwd(dmerged, u, pa, pb, pc)
    dya = mm(dpa, wt["w_branch_a"], name="mix_dya")
    dyb = mm(dpb, wt["w_branch_b"], name="mix_dyb")
    dyc = mm(dpc, wt["w_branch_c"], name="mix_dyc")
    dw_a = mm(ya.T, dpa, out_dtype=bf16, name="mix_dwa")
    dw_b = mm(yb.astype(bf16).T, dpb, out_dtype=bf16, name="mix_dwb")
    dw_c = mm(yc.astype(bf16).T, dpc, out_dtype=bf16, name="mix_dwc")
    (daq, daf, dai, dag, dlb, dng), side0 = hgrn_bwd(u, lb, ng, o, st, dya, side=plans[0])
    (dbq, dbk, dbv), side1 = sb_bwd(u, yb, dyb, side=plans[1])
    dc = [dil_bwd(u, g, dyc, yc, lse) for g in range(3)]
    du = jnp.concatenate(
        [daq, daf, dai, dag] + [t.astype(bf16) for t in (dbq, dbk, dbv)]
        + [dc[g][j].astype(bf16) for j in range(3) for g in range(3)] + [dg0, dg1, dg2], axis=1)
    dh = mm(du, wt["w_in"], name="mix_dh")
    dw_in = mm(h_t, du, out_dtype=bf16, name="mix_dwi", side=plans[2])
    side2 = None
    if plans[2] is not None:
        dw_in, side2 = dw_in
    dx, dsh, da = pre_bwd(dout, dh, x, a_vec)
    grads = {"w_in": dw_in, "w_out": dw_out, "w_branch_a": dw_a, "w_branch_b": dw_b, "w_branch_c": dw_c}
    return dx, grads, dlb, jnp.sum(dng, axis=0), dsh, da, db, (side0, side1, side2)


NEXT_RIDERS = (("ffn1_w_out",), ("w_out", "w_branch_a", "w_branch_b", "w_branch_c"), ("ffn2_w_out",), ("ffn1_w_in",),
               ("w_in",), (), (), (), ("ffn2_w_in",), ())
FIRST = ("ffn1_w_in", "ffn1_w_out", "w_in", "w_out", "w_branch_a", "w_branch_b", "w_branch_c")
LATE_RIDERS = ((), (), (), (), (), ("ffn2_w_in",), ("ffn2_w_out",), (), (), ())
BWD_RIDERS = (("ffn1_w_in", "ffn2_w_out", "w_branch_a", "w_branch_b", "w_branch_c"), ("w_in", "ffn1_w_out", "w_out"),
              ("ffn2_w_in",))
LAST_RIDERS = (("ffn2_w_out", "w_out", "w_branch_a", "w_branch_b", "w_branch_c"), ("ffn2_w_in",), ("w_in",))
TAIL = ("ffn1_w_in", "ffn1_w_out")


def kernel(x, c, w_ada, b_ada, norm_g, ffn1_w_in, ffn1_w_out, w_in, hgrn_lb_logits, hgrn_norm_g, w_branch_a, w_branch_b, w_branch_c, w_out, ffn2_w_in, ffn2_w_out, loss_target, m_w_ada, m_b_ada, m_norm_g, m_ffn1_w_in, m_ffn1_w_out, m_w_in, m_hgrn_lb_logits, m_hgrn_norm_g, m_w_branch_a, m_w_branch_b, m_w_branch_c, m_w_out, m_ffn2_w_in, m_ffn2_w_out, v_w_ada, v_b_ada, v_norm_g, v_ffn1_w_in, v_ffn1_w_out, v_w_in, v_hgrn_lb_logits, v_hgrn_norm_g, v_w_branch_a, v_w_branch_b, v_w_branch_c, v_w_out, v_ffn2_w_in, v_ffn2_w_out):
    weights = dict(w_ada=w_ada, b_ada=b_ada, norm_g=norm_g, ffn1_w_in=ffn1_w_in, ffn1_w_out=ffn1_w_out, w_in=w_in,
                   hgrn_lb_logits=hgrn_lb_logits, hgrn_norm_g=hgrn_norm_g, w_branch_a=w_branch_a, w_branch_b=w_branch_b,
                   w_branch_c=w_branch_c, w_out=w_out, ffn2_w_in=ffn2_w_in, ffn2_w_out=ffn2_w_out)
    mom = dict(w_ada=m_w_ada, b_ada=m_b_ada, norm_g=m_norm_g, ffn1_w_in=m_ffn1_w_in, ffn1_w_out=m_ffn1_w_out, w_in=m_w_in,
               hgrn_lb_logits=m_hgrn_lb_logits, hgrn_norm_g=m_hgrn_norm_g, w_branch_a=m_w_branch_a, w_branch_b=m_w_branch_b,
               w_branch_c=m_w_branch_c, w_out=m_w_out, ffn2_w_in=m_ffn2_w_in, ffn2_w_out=m_ffn2_w_out)
    var = dict(w_ada=v_w_ada, b_ada=v_b_ada, norm_g=v_norm_g, ffn1_w_in=v_ffn1_w_in, ffn1_w_out=v_ffn1_w_out, w_in=v_w_in,
               hgrn_lb_logits=v_hgrn_lb_logits, hgrn_norm_g=v_hgrn_norm_g, w_branch_a=v_w_branch_a, w_branch_b=v_w_branch_b,
               w_branch_c=v_w_branch_c, w_out=v_w_out, ffn2_w_in=v_ffn2_w_in, ffn2_w_out=v_ffn2_w_out)
    order = list(weights)
    xi, yi, ci = _me()
    chip = 2 * xi + yi
    dev = 4 * xi + 2 * yi + ci
    xs = x[0]

    c_all = all_gather_devs(c, name="ag8_c").reshape(8, D)
    mod_sh = all_gather_chips([ada_fwd(c_all, w_ada)], name="ag4_mod")[0]
    mod_all = mod_sh.transpose(1, 2, 0, 3).reshape(DEPTH, 8, 9 * D)
    mod = lax.dynamic_index_in_dim(mod_all, dev, axis=1, keepdims=False) + b_ada
    mod = mod.reshape(DEPTH, 3, 3, D)
    ng_all = all_gather_chips([norm_g.reshape(DEPTH * 6, D // 4)], name="ag4_norm")[0]
    ng_all = ng_all.reshape(4, DEPTH, 6, D // 4).transpose(1, 2, 0, 3).reshape(DEPTH, 6, D)
    lb_all = lb_fwd(hgrn_lb_logits)
    w16 = {n: cast_bf16(weights[n]) for n in BIG}

    def vecs(l, i):
        shift, scale, gate = mod[l, i, 0][None], mod[l, i, 1][None], mod[l, i, 2][None]
        g_pre, g_post = ng_all[l, 2 * i][None], ng_all[l, 2 * i + 1][None]
        return g_pre * (1.0 + scale), shift, RES_W[i] * gate * g_post

    saved, full = [], []
    gathered = dict(zip(FIRST, all_gather_chips([w16[n] for n in FIRST], layer=0, name="ag4_w0")))
    for l in range(DEPTH):
        riders = [[(n, l) for n in (LATE_RIDERS[h] if l == 0 else ())]
                  + [(n, l + 1) for n in (NEXT_RIDERS[h] if l + 1 < DEPTH else ())] for h in range(len(NEXT_RIDERS))]
        plans = [gather_chips_plan([w16[n] for n, _ in it], layer=[ll for _, ll in it]) if it else None for it in riders]
        coming = {}

        def absorb(hosts, sides):
            for h, outs in zip(hosts, sides):
                for (n, ll), g in zip(riders[h], outs or ()):
                    (gathered if ll == l else coming)[n] = g

        full_w = lambda n: _full_weight(n, gathered[n])
        lb, ng = lb_all[l][None], hgrn_norm_g[l][None]
        xs, s1, sides = _ffn_fwd(xs, full_w("ffn1_w_in"), full_w("ffn1_w_out"), *vecs(l, 0), plans=plans[0:2])
        absorb((0, 1), sides)
        w = {n: full_w(n) for n in ("w_in", "w_branch_a", "w_branch_b", "w_branch_c", "w_out")}
        xs, s2, sides = _mix_fwd(xs, w, lb, ng, *vecs(l, 1), plans=plans[2:8])
        absorb(range(2, 8), sides)
        xs, s3, sides = _ffn_fwd(xs, full_w("ffn2_w_in"), full_w("ffn2_w_out"), *vecs(l, 2), plans=plans[8:10])
        absorb((8, 9), sides)
        saved.append((s1, s2, s3))
        full.append({n: _full_weight_t(n, gathered[n]) for n in BIG})
        gathered = coming

    dx, loss_part = loss_grad(xs, loss_target[0])
    loss = lax.psum(loss_part[0, 0], ("x", "y", "c"))

    big_grads = {n: [None] * DEPTH for n in BIG}
    d_mod, d_ng, d_lb, d_hng = [None] * DEPTH, [None] * DEPTH, [None] * DEPTH, [None] * DEPTH
    pending = None
    for l in reversed(range(DEPTH)):
        wt = full[l]
        s1, s2, s3 = saved[l]
        lb, ng = lb_all[l][None], hgrn_norm_g[l][None]
        rows_mod, rows_ng = [None] * 9, [None] * 6

        def vec_grads(i, dsh, da, db):
            scale, gate = mod[l, i, 1][None], mod[l, i, 2][None]
            g_pre, g_post = ng_all[l, 2 * i][None], ng_all[l, 2 * i + 1][None]
            rows_mod[3 * i], rows_mod[3 * i + 1], rows_mod[3 * i + 2] = dsh, g_pre * da, RES_W[i] * g_post * db
            rows_ng[2 * i], rows_ng[2 * i + 1] = (1.0 + scale) * da, RES_W[i] * gate * db

        a3, _, b3 = vecs(l, 2)
        dx, dwi, dwo, dsh, da, db = _ffn_bwd(dx, s3, wt["ffn2_w_in"], wt["ffn2_w_out"], a3, b3)
        vec_grads(2, dsh, da, db)
        grads = {"ffn2_w_in": dwi, "ffn2_w_out": dwo}
        a2, _, b2 = vecs(l, 1)
        plans = (None,) * len(BWD_RIDERS)
        if pending is not None:
            plans = tuple(dev_exchange_plan([pending[n] for n in names]) for names in BWD_RIDERS)
        dx, gmix, dlb, dhng, dsh, da, db, sides = _mix_bwd(dx, s2, wt, lb, ng, a2, b2, plans=plans)
        if pending is not None:
            for names, outs in zip(BWD_RIDERS, sides):
                for n, slots in zip(names, outs):
                    big_grads[n][l + 1] = sum_share(slots)
        vec_grads(1, dsh, da, db)
        grads.update(gmix)
        a1, _, b1 = vecs(l, 0)
        if l > 0:
            dx, dwi, dwo, dsh, da, db = _ffn_bwd(dx, s1, wt["ffn1_w_in"], wt["ffn1_w_out"], a1, b1)
        else:
            ready = {n: _by_shard(n, grads[n]) for names in LAST_RIDERS for n in names}
            plans = tuple(dev_exchange_plan([ready[n] for n in names]) for names in LAST_RIDERS)
            dx, dwi, dwo, dsh, da, db, sides = _ffn_bwd(dx, s1, wt["ffn1_w_in"], wt["ffn1_w_out"], a1, b1, plans=plans)
            for names, outs in zip(LAST_RIDERS, sides):
                for n, slots in zip(names, outs):
                    big_grads[n][0] = sum_share(slots)
        vec_grads(0, dsh, da, db)
        grads.update({"ffn1_w_in": dwi, "ffn1_w_out": dwo})
        pending = {n: _by_shard(n, grads[n]) for n in (BIG if l > 0 else TAIL)}
        d_mod[l] = jnp.concatenate(rows_mod, axis=1)
        d_ng[l] = jnp.concatenate(rows_ng, axis=0)
        d_lb[l], d_hng[l] = dlb, dhng

    n_small = 6 * D * DEPTH + 768 * DEPTH + A_V * DEPTH + 9 * D * DEPTH
    pad = -n_small % (512 * LANE)
    flat = jnp.concatenate([jnp.stack(d_ng).reshape(-1), jnp.concatenate(d_lb, axis=0).reshape(-1),
                            jnp.concatenate(d_hng, axis=0).reshape(-1), jnp.concatenate(d_mod, axis=0).reshape(-1),
                            jnp.zeros((pad,), f32)])
    small_all = all_gather_devs(flat.reshape(-1, LANE), name="ag8_small")
    total = sum_slots(small_all).reshape(-1)
    o1 = 6 * D * DEPTH
    o2 = o1 + 768 * DEPTH
    o3 = o2 + A_V * DEPTH
    g_ng_full = total[:o1].reshape(DEPTH, 6, D)
    g_lb_all = total[o1:o2].reshape(DEPTH, 768)
    g_small = {
        "norm_g": lax.dynamic_slice_in_dim(g_ng_full, chip * (D // 4), D // 4, axis=2),
        "hgrn_lb_logits": lb_bwd(hgrn_lb_logits, g_lb_all),
        "hgrn_norm_g": total[o2:o3].reshape(DEPTH, A_V),
        "b_ada": total[o3:n_small].reshape(DEPTH, 9 * D),
    }
    dmod_all = small_all.reshape(8, -1)[:, o3:n_small].reshape(8, DEPTH, 9 * D).transpose(1, 0, 2)
    dm_sh = lax.dynamic_slice_in_dim(dmod_all, chip * ADA_N, ADA_N, axis=2)

    out_g, out_d, out_m, out_v = {}, {}, {}, {}
    ada_outs, slots = ada_bwd_adam(c_all, dm_sh, w_ada, m_w_ada, v_w_ada, side=dev_exchange_plan([pending[n] for n in TAIL]))
    out_g["w_ada"], out_d["w_ada"], out_m["w_ada"], out_v["w_ada"] = ada_outs
    for n, s in zip(TAIL, slots):
        big_grads[n][0] = sum_share(s)
    for n in BIG:
        out_g[n], out_d[n], out_m[n], out_v[n] = adam_layers(weights[n], big_grads[n], mom[n], var[n])
    out_g.update(g_small)
    for n in g_small:
        out_d[n], out_m[n], out_v[n] = adam(weights[n], out_g[n], mom[n], var[n])
    return (loss, dx[None], *[out_g[n] for n in order], *[out_d[n] for n in order],
            *[out_m[n] for n in order], *[out_v[n] for n in order])
```

```python
import functools
import math

import jax
import jax.numpy as jnp
from jax import lax
from jax.experimental import pallas as pl
from jax.experimental.pallas import tpu as pltpu

f32, bf16 = jnp.float32, jnp.bfloat16

D = 1024
DEPTH = 4
D_FF = 2816
EPS = 1e-6
NEG_BIG = -1e30
TINY = 1e-30
A_HEADS, A_K, A_V, A_CHUNK = 6, 128, 64, 64
A_SUB = 16
A_CLAMP = 80.0
B_HEADS, HD = 6, 64
C_GROUPS = ((128, 1), (512, 4), (2048, 16))
C_BLK = 128
IN_COLS = 8832
O_AQ, O_AF, O_AI, O_AG = 0, 768, 1536, 1920
O_BQ, O_BK, O_BV = 2304, 2688, 3072
O_CQ, O_CK, O_CV = 3456, 4224, 4992
O_GATE = 5760
LANE = 128
ADAM_LR, ADAM_B1, ADAM_B2, ADAM_EPS, ADAM_WD, ADAM_STEP = 0.001, 0.9, 0.999, 1e-08, 0.01, 10
MESH = pl.DeviceIdType.MESH
VMEM_LIMIT = 56 * 1024 * 1024


def _alibi_slopes(n):
    def pow2(m):
        start = 2.0 ** (-8.0 / m)
        return [start ** (i + 1) for i in range(m)]
    if math.log2(n).is_integer():
        s = pow2(n)
    else:
        c = 2 ** int(math.floor(math.log2(n)))
        s = pow2(c) + pow2(2 * c)[0::2][: n - c]
    return sorted(s, reverse=True)


C_SLOPES = _alibi_slopes(12)


def _tile(n, prefs):
    for p in prefs:
        if n % p == 0:
            return p
    return n


def _cp(sem):
    return pltpu.CompilerParams(dimension_semantics=sem, vmem_limit_bytes=VMEM_LIMIT)


def _sig(x):
    return 1.0 / (1.0 + jnp.exp(-x))


def _dot(a, b, dn, precision=None):
    return lax.dot_general(a, b, (dn, ((), ())), preferred_element_type=f32, precision=precision)


NN = ((1,), (0,))
NT = ((1,), (1,))
TN = ((0,), (0,))


class Plan:
    def __init__(self, ins, outs, n_remote, n_local, start, wait):
        self.ins, self.outs, self.n_remote, self.n_local, self.start, self.wait = ins, outs, n_remote, n_local, start, wait

    def sems(self):
        return [pltpu.SemaphoreType.DMA((self.n_remote,)), pltpu.SemaphoreType.DMA((self.n_remote,)),
                pltpu.SemaphoreType.DMA((max(self.n_local, 1),))]


def _call(body, *, name, grid, in_specs, out_specs, out_shape, sem, args, scratch_shapes=(), side=None):
    if side is None:
        return pl.pallas_call(body, name=name, grid=grid, in_specs=in_specs, out_specs=out_specs, out_shape=out_shape,
                              scratch_shapes=list(scratch_shapes), compiler_params=_cp(sem))(*args), None
    any_spec = pl.BlockSpec(memory_space=pl.ANY)
    n_in, n_out, n_scr = len(in_specs), len(out_specs), len(scratch_shapes)
    s_in, s_out = len(side.ins), len(side.outs)

    def hosted(*refs):
        ins, rest = refs[:n_in], refs[n_in:]
        sins, rest = rest[:s_in], rest[s_in:]
        outs, rest = rest[:n_out], rest[n_out:]
        souts, rest = rest[:s_out], rest[s_out:]
        scr, sems = rest[:n_scr], rest[n_scr:]
        pids = [pl.program_id(d) for d in range(len(grid))]
        first = functools.reduce(jnp.logical_and, [p == 0 for p in pids])
        last = functools.reduce(jnp.logical_and, [p == g - 1 for p, g in zip(pids, grid)])

        @pl.when(first)
        def _():
            side.start(sins, souts, *sems)

        body(*ins, *outs, *scr)

        @pl.when(last)
        def _():
            side.wait(sins, souts, *sems)

    res = pl.pallas_call(
        hosted, name=name, grid=grid, in_specs=list(in_specs) + [any_spec] * s_in,
        out_specs=list(out_specs) + [any_spec] * s_out, out_shape=list(out_shape) + list(side.outs),
        scratch_shapes=list(scratch_shapes) + side.sems(), compiler_params=_cp(("arbitrary",) * len(grid)),
    )(*args, *side.ins)
    return res[:n_out], res[n_out:]


MM_TILES = {
    (4096, 1024, 2816): (1024, 512, 2816),
    (4096, 1024, 5632): (512, 512, 5632),
    (1024, 5632, 4096): (512, 512, 4096),
    (2816, 1024, 4096): (704, 512, 4096),
    (4096, 8832, 1024): (512, 2944, 1024),
    (4096, 1024, 8832): (1024, 512, 2944),
    (1024, 8832, 4096): (512, 2944, 1024),
    (1024, 1024, 4096): (512, 512, 4096),
    (384, 1024, 4096): (384, 512, 4096),
    (256, 1024, 4096): (256, 512, 4096),
}


def mm(a, b, *, out_dtype=f32, name="mm", side=None):
    M, K = a.shape
    K2, N = b.shape
    assert K == K2, (a.shape, b.shape)
    tm, tn, tk = MM_TILES.get((M, N, K), (_tile(M, (1024, 704, 512, 384, 256, 128)), _tile(N, (512, 384, 256, 128)),
                                          _tile(K, (1024, 512, 1408, 384, 256, 128))))
    nk = K // tk

    def body(a_ref, b_ref, o_ref, *acc):
        p = _dot(a_ref[...].astype(bf16), b_ref[...].astype(bf16), NN)
        if nk == 1:
            o_ref[...] = p.astype(out_dtype)
            return
        acc_ref, = acc
        k = pl.program_id(2)

        @pl.when(k == 0)
        def _():
            acc_ref[...] = p

        @pl.when(k > 0)
        def _():
            acc_ref[...] += p

        @pl.when(k == nk - 1)
        def _():
            o_ref[...] = acc_ref[...].astype(out_dtype)

    outs, souts = _call(
        body, name=name, grid=(M // tm, N // tn, nk),
        in_specs=[pl.BlockSpec((tm, tk), lambda i, j, k: (i, k)), pl.BlockSpec((tk, tn), lambda i, j, k: (k, j))],
        out_specs=[pl.BlockSpec((tm, tn), lambda i, j, k: (i, j))],
        out_shape=[jax.ShapeDtypeStruct((M, N), out_dtype)],
        scratch_shapes=[pltpu.VMEM((tm, tn), f32)] if nk > 1 else [],
        sem=("parallel", "parallel", "arbitrary"), args=(a, b), side=side)
    return outs[0] if side is None else (outs[0], souts)


FF_TM = 1024
FF_T = 256
FF_NB = D_FF // FF_T


def ffn_in_swiglu(h, w_in, side=None):
    S = h.shape[0]

    def body(h_ref, wa_ref, wb_ref, a_ref, b_ref, s_ref, st_ref):
        hv = h_ref[...]
        a = _dot(hv, wa_ref[...], NN)
        b = _dot(hv, wb_ref[...], NN)
        a_ref[...] = a.astype(bf16)
        b_ref[...] = b.astype(bf16)
        s = (a * _sig(a) * b).astype(bf16)
        s_ref[...] = s
        st_ref[...] = s.T

    ospec = pl.BlockSpec((FF_TM, FF_T), lambda i, j: (i, j))
    osd = jax.ShapeDtypeStruct((S, D_FF), bf16)
    return _call(
        body, name="ffn_in", grid=(S // FF_TM, FF_NB),
        in_specs=[pl.BlockSpec((FF_TM, D), lambda i, j: (i, 0)), pl.BlockSpec((D, FF_T), lambda i, j: (0, j)),
                  pl.BlockSpec((D, FF_T), lambda i, j: (0, j + FF_NB))],
        out_specs=[ospec] * 3 + [pl.BlockSpec((FF_T, FF_TM), lambda i, j: (j, i))],
        out_shape=[osd] * 3 + [jax.ShapeDtypeStruct((D_FF, S), bf16)], sem=("parallel", "parallel"),
        args=(h, w_in, w_in), side=side)


def ffn_du(dy, w_out_t, ua, ub):
    S = dy.shape[0]
    tm = 512

    def body(dy_ref, w_ref, a_ref, b_ref, du_ref):
        dyv = dy_ref[...]
        for j in range(FF_NB):
            cols = slice(j * FF_T, (j + 1) * FF_T)
            ds = _dot(dyv, w_ref[:, cols], NN)
            a, b = a_ref[:, cols].astype(f32), b_ref[:, cols].astype(f32)
            sg = _sig(a)
            du_ref[:, cols] = (ds * b * sg * (1.0 + a * (1.0 - sg))).astype(bf16)
            du_ref[:, D_FF + j * FF_T:D_FF + (j + 1) * FF_T] = (ds * a * sg).astype(bf16)

    half = pl.BlockSpec((tm, D_FF), lambda i: (i, 0))
    return pl.pallas_call(
        body, name="ffn_du", grid=(S // tm,),
        in_specs=[pl.BlockSpec((tm, D), lambda i: (i, 0)), pl.BlockSpec((D, D_FF), lambda i: (0, 0)), half, half],
        out_specs=pl.BlockSpec((tm, 2 * D_FF), lambda i: (i, 0)),
        out_shape=jax.ShapeDtypeStruct((S, 2 * D_FF), bf16), compiler_params=_cp(("parallel",)),
    )(dy, w_out_t, ua, ub)


TR = 512


def _row_spec(cols=D):
    return pl.BlockSpec((TR, cols), lambda i: (i, 0))


def _vec_spec(cols=D):
    return pl.BlockSpec((1, cols), lambda i: (0, 0))


def prenorm(x, a_vec, sh_vec):
    S = x.shape[0]

    def body(x_ref, a_ref, s_ref, h_ref, ht_ref):
        xv = x_ref[...]
        rstd = lax.rsqrt(jnp.mean(xv * xv, axis=1, keepdims=True) + EPS)
        h = (xv * rstd * a_ref[...] + s_ref[...]).astype(bf16)
        h_ref[...] = h
        ht_ref[...] = h.T

    return pl.pallas_call(
        body, name="prenorm", grid=(S // TR,),
        in_specs=[_row_spec(), _vec_spec(), _vec_spec()],
        out_specs=[_row_spec(), pl.BlockSpec((D, TR), lambda i: (0, i))],
        out_shape=[jax.ShapeDtypeStruct((S, D), bf16), jax.ShapeDtypeStruct((D, S), bf16)],
        compiler_params=_cp(("parallel",)),
    )(x, a_vec, sh_vec)


def postnorm(x, y, b_vec):
    S = x.shape[0]

    def body(x_ref, y_ref, b_ref, o_ref):
        yv = y_ref[...]
        rstd = lax.rsqrt(jnp.mean(yv * yv, axis=1, keepdims=True) + EPS)
        o_ref[...] = x_ref[...] + b_ref[...] * (yv * rstd)

    return pl.pallas_call(
        body, name="postnorm", grid=(S // TR,),
        in_specs=[_row_spec(), _row_spec(), _vec_spec()], out_specs=_row_spec(),
        out_shape=jax.ShapeDtypeStruct((S, D), f32), compiler_params=_cp(("parallel",)),
    )(x, y, b_vec)


def post_bwd(dout, y, b_vec):
    S = dout.shape[0]

    def body(d_ref, y_ref, b_ref, dy_ref, db_ref):
        i = pl.program_id(0)
        yv, dv = y_ref[...], d_ref[...]
        rstd = lax.rsqrt(jnp.mean(yv * yv, axis=1, keepdims=True) + EPS)
        yh = yv * rstd
        dyh = dv * b_ref[...]
        dy_ref[...] = (rstd * (dyh - yh * jnp.mean(dyh * yh, axis=1, keepdims=True))).astype(bf16)
        part = jnp.sum(dv * yh, axis=0, keepdims=True)

        @pl.when(i == 0)
        def _():
            db_ref[...] = part

        @pl.when(i > 0)
        def _():
            db_ref[...] += part

    return pl.pallas_call(
        body, name="post_bwd", grid=(S // TR,),
        in_specs=[_row_spec(), _row_spec(), _vec_spec()], out_specs=[_row_spec(), _vec_spec()],
        out_shape=[jax.ShapeDtypeStruct((S, D), bf16), jax.ShapeDtypeStruct((1, D), f32)],
        compiler_params=_cp(("arbitrary",)),
    )(dout, y, b_vec)


def pre_bwd(dout, dh, x, a_vec):
    S = dout.shape[0]

    def body(d_ref, dh_ref, x_ref, a_ref, dx_ref, ds_ref, da_ref):
        i = pl.program_id(0)
        xv, dhv = x_ref[...], dh_ref[...]
        rstd = lax.rsqrt(jnp.mean(xv * xv, axis=1, keepdims=True) + EPS)
        n1 = xv * rstd
        dn = dhv * a_ref[...]
        dx_ref[...] = d_ref[...] + rstd * (dn - n1 * jnp.mean(dn * n1, axis=1, keepdims=True))
        p_s = jnp.sum(dhv, axis=0, keepdims=True)
        p_a = jnp.sum(dhv * n1, axis=0, keepdims=True)

        @pl.when(i == 0)
        def _():
            ds_ref[...] = p_s
            da_ref[...] = p_a

        @pl.when(i > 0)
        def _():
            ds_ref[...] += p_s
            da_ref[...] += p_a

    return pl.pallas_call(
        body, name="pre_bwd", grid=(S // TR,),
        in_specs=[_row_spec(), _row_spec(), _row_spec(), _vec_spec()],
        out_specs=[_row_spec(), _vec_spec(), _vec_spec()],
        out_shape=[jax.ShapeDtypeStruct((S, D), f32), jax.ShapeDtypeStruct((1, D), f32), jax.ShapeDtypeStruct((1, D), f32)],
        compiler_params=_cp(("arbitrary",)),
    )(dout, dh, x, a_vec)


def loss_grad(y, tgt):
    S = y.shape[0]

    def body(y_ref, t_ref, dy_ref, l_ref):
        i = pl.program_id(0)
        e = y_ref[...] - t_ref[...]
        dy_ref[...] = e * (1.0 / D)
        part = jnp.sum(jnp.sum(e * e, axis=1, keepdims=True), axis=0, keepdims=True) * (0.5 / D)
        part = jnp.broadcast_to(part, (8, LANE))

        @pl.when(i == 0)
        def _():
            l_ref[...] = part

        @pl.when(i > 0)
        def _():
            l_ref[...] += part

    return pl.pallas_call(
        body, name="loss_grad", grid=(S // TR,),
        in_specs=[_row_spec(), _row_spec()],
        out_specs=[_row_spec(), pl.BlockSpec((8, LANE), lambda i: (0, 0))],
        out_shape=[jax.ShapeDtypeStruct((S, D), f32), jax.ShapeDtypeStruct((8, LANE), f32)],
        compiler_params=_cp(("arbitrary",)),
    )(y, tgt)


G_NB = D // LANE
G_TR = 2048
G_OFF = O_GATE // LANE


def gate_merge(u, pa, pb, pc):
    S = u.shape[0]

    def body(g0, g1, g2, a, b, c, o_ref, ot_ref):
        m = (_sig(g0[...]) * a[...] + _sig(g1[...]) * b[...] + _sig(g2[...]) * c[...]).astype(bf16)
        o_ref[...] = m
        ot_ref[...] = m.T

    gs = [pl.BlockSpec((G_TR, LANE), functools.partial(lambda i, j, k: (i, G_OFF + G_NB * k + j), k=k)) for k in range(3)]
    ps = pl.BlockSpec((G_TR, LANE), lambda i, j: (i, j))
    return pl.pallas_call(
        body, name="gate_merge", grid=(S // G_TR, G_NB),
        in_specs=gs + [ps, ps, ps], out_specs=[ps, pl.BlockSpec((LANE, G_TR), lambda i, j: (j, i))],
        out_shape=[jax.ShapeDtypeStruct((S, D), bf16), jax.ShapeDtypeStruct((D, S), bf16)],
        compiler_params=_cp(("parallel", "parallel")),
    )(u, u, u, pa, pb, pc)


def gate_bwd(dm, u, pa, pb, pc):
    S = u.shape[0]

    def body(dm_ref, g0, g1, g2, a, b, c, da, db, dc, dg0, dg1, dg2):
        d = dm_ref[...]
        for g, p, dp, dg in ((g0, a, da, dg0), (g1, b, db, dg1), (g2, c, dc, dg2)):
            s = _sig(g[...])
            dp[...] = (d * s).astype(bf16)
            dg[...] = (d * p[...] * s * (1.0 - s)).astype(bf16)

    gs = [pl.BlockSpec((G_TR, LANE), functools.partial(lambda i, j, k: (i, G_OFF + G_NB * k + j), k=k)) for k in range(3)]
    ps = pl.BlockSpec((G_TR, LANE), lambda i, j: (i, j))
    osd = jax.ShapeDtypeStruct((S, D), bf16)
    return pl.pallas_call(
        body, name="gate_bwd", grid=(S // G_TR, G_NB),
        in_specs=[ps] + gs + [ps, ps, ps], out_specs=[ps] * 6, out_shape=[osd] * 6,
        compiler_params=_cp(("parallel", "parallel")),
    )(dm, u, u, u, pa, pb, pc)


A_TB = 512
A_NCH = A_TB // A_CHUNK
A_NSUB = A_CHUNK // A_SUB
A_HP = 6
A_KW, A_VW = A_HP * A_K, A_HP * A_V


def _hgrn_gates(qr, fr, lbh):
    sq = _sig(qr)
    sig = _sig(fr)
    f = lbh + (1.0 - lbh) * sig
    logf = jnp.log(jnp.maximum(f, TINY))
    k = (1.0 - lbh) * (1.0 - sig)
    return qr * sq, sq, sig, f, logf, k


def _hgrn_intra(qf, k, b, causal):
    qts, kts, eqs, eks, blocks = [], [], [], [], []
    for sb in range(A_NSUB):
        rs = sb * A_SUB
        r = b[rs - 1:rs, :] if sb else jnp.zeros((1, A_K), f32)
        eq = jnp.exp(b[rs:rs + A_SUB, :] - r)
        ek = jnp.exp(jnp.minimum(r - b, A_CLAMP))
        qt = (qf[rs:rs + A_SUB, :] * eq).astype(bf16)
        kt = (k * ek).astype(bf16)
        blocks.append(_dot(qt, kt, NT))
        qts.append(qt), kts.append(kt), eqs.append(eq), eks.append(ek)
    a = jnp.where(causal, jnp.concatenate(blocks, axis=0), 0.0)
    return a, qts, kts, eqs, eks


def _tri():
    r = lax.broadcasted_iota(jnp.int32, (A_CHUNK, A_CHUNK), 0)
    c = lax.broadcasted_iota(jnp.int32, (A_CHUNK, A_CHUNK), 1)
    return r >= c


def _hgrn_in_specs(rev_nb=None):
    def im(col):
        if rev_nb is None:
            return lambda p, i: (i, col + p)
        return lambda p, i: (rev_nb - 1 - i, col + p)
    return [pl.BlockSpec((A_TB, A_KW), im(O_AQ // A_KW)), pl.BlockSpec((A_TB, A_KW), im(O_AF // A_KW)),
            pl.BlockSpec((A_TB, A_VW), im(O_AI // A_VW)), pl.BlockSpec((A_TB, A_VW), im(O_AG // A_VW)),
            pl.BlockSpec((1, A_KW), lambda p, i: (0, p)), pl.BlockSpec((1, A_V), lambda p, i: (0, 0))]


def hgrn_fwd(u, lb, ng, side=None):
    S = u.shape[0]
    nb = S // A_TB

    def body(q_ref, f_ref, i_ref, g_ref, lb_ref, ng_ref, o_ref, ya_ref, st_ref, state):
        @pl.when(pl.program_id(1) == 0)
        def _():
            state[...] = jnp.zeros_like(state)

        causal = _tri()
        tri = causal.astype(f32)

        def chunk(n, carry):
            rows = pl.ds(pl.multiple_of(n * A_CHUNK, A_CHUNK), A_CHUNK)
            o_parts, y_parts = [], []
            for hh in range(A_HP):
                ks = slice(hh * A_K, (hh + 1) * A_K)
                vs = slice(hh * A_V, (hh + 1) * A_V)
                qf, _, _, _, logf, k = _hgrn_gates(q_ref[rows, ks], f_ref[rows, ks], lb_ref[:, ks])
                vi = i_ref[rows, vs].astype(bf16)
                gg = g_ref[rows, vs]
                b = _dot(tri, logf, NN, precision=lax.Precision.HIGHEST)
                s0 = state[hh]
                st_ref[n, hh] = s0
                o = _dot((qf * jnp.exp(b)).astype(bf16), s0.astype(bf16), NT)
                a, _, _, _, _ = _hgrn_intra(qf, k, b, causal)
                o = o + _dot(a.astype(bf16), vi, NN)
                bend = b[A_CHUNK - 1:A_CHUNK, :]
                ke = (k * jnp.exp(bend - b)).astype(bf16)
                state[hh] = s0 * jnp.exp(bend) + _dot(vi, ke, TN)
                rstd = lax.rsqrt(jnp.mean(o * o, axis=1, keepdims=True) + EPS)
                o_parts.append(o)
                y_parts.append(o * rstd * ng_ref[...] * (gg * _sig(gg)))
            o_ref[rows, :] = jnp.concatenate(o_parts, axis=1)
            ya_ref[rows, :] = jnp.concatenate(y_parts, axis=1).astype(bf16)
            return carry

        lax.fori_loop(0, A_NCH, chunk, 0)

    return _call(
        body, name="hgrn_fwd", grid=(A_HEADS // A_HP, nb),
        in_specs=_hgrn_in_specs(),
        out_specs=[pl.BlockSpec((A_TB, A_VW), lambda p, i: (i, p)), pl.BlockSpec((A_TB, A_VW), lambda p, i: (i, p)),
                   pl.BlockSpec((A_NCH, A_HP, A_V, A_K), lambda p, i: (i, p, 0, 0))],
        out_shape=[jax.ShapeDtypeStruct((S, 384), f32), jax.ShapeDtypeStruct((S, 384), bf16),
                   jax.ShapeDtypeStruct((S // A_CHUNK, A_HEADS, A_V, A_K), f32)],
        scratch_shapes=[pltpu.VMEM((A_HP, A_V, A_K), f32)],
        sem=("parallel", "arbitrary"), args=(u, u, u, u, lb, ng), side=side)


def hgrn_bwd(u, lb, ng, o, st, dya, side=None):
    S = u.shape[0]
    nb = S // A_TB

    def body(q_ref, f_ref, i_ref, g_ref, lb_ref, ng_ref, o_ref, st_ref, dy_ref,
             dq_ref, df_ref, di_ref, dg_ref, dlb_ref, dng_ref, dstate):
        @pl.when(pl.program_id(1) == 0)
        def _():
            dstate[...] = jnp.zeros_like(dstate)
            dlb_ref[...] = jnp.zeros_like(dlb_ref)
            dng_ref[...] = jnp.zeros_like(dng_ref)

        causal = _tri()
        tri = causal.astype(f32)

        def chunk(it, carry):
            n = A_NCH - 1 - it
            rows = pl.ds(pl.multiple_of(n * A_CHUNK, A_CHUNK), A_CHUNK)
            dq_p, df_p, di_p, dg_p, dlb_p = [], [], [], [], []
            dng_acc = jnp.zeros((1, A_V), f32)
            for hh in range(A_HP):
                ks = slice(hh * A_K, (hh + 1) * A_K)
                vs = slice(hh * A_V, (hh + 1) * A_V)
                lbh = lb_ref[:, ks]
                qr = q_ref[rows, ks]
                qf, sq, sig, f, logf, k = _hgrn_gates(qr, f_ref[rows, ks], lbh)
                vi = i_ref[rows, vs].astype(bf16)
                gg = g_ref[rows, vs]
                b = _dot(tri, logf, NN, precision=lax.Precision.HIGHEST)
                eb = jnp.exp(b)
                bend = b[A_CHUNK - 1:A_CHUNK, :]
                eend = jnp.exp(bend)
                ekend = jnp.exp(bend - b)
                qe = (qf * eb).astype(bf16)
                ke = (k * ekend).astype(bf16)
                s0 = st_ref[n, hh]
                dsend = dstate[hh]
                ov = o_ref[rows, vs]
                dy = dy_ref[rows, vs]
                rstd = lax.rsqrt(jnp.mean(ov * ov, axis=1, keepdims=True) + EPS)
                oh = ov * rstd
                sg = _sig(gg)
                d_on = dy * (gg * sg)
                dg_p.append(dy * oh * ng_ref[...] * (sg * (1.0 + gg * (1.0 - sg))))
                dng_acc = dng_acc + jnp.sum(d_on * oh, axis=0, keepdims=True)
                doh = d_on * ng_ref[...]
                do = (rstd * (doh - oh * jnp.mean(doh * oh, axis=1, keepdims=True))).astype(bf16)
                a, qts, kts, eqs, eks = _hgrn_intra(qf, k, b, causal)
                da = jnp.where(causal, _dot(do, vi, NT), 0.0).astype(bf16)
                dsb = dsend.astype(bf16)
                dv = _dot(a.astype(bf16), do, TN) + _dot(ke, dsb, NT)
                dq = _dot(do, s0.astype(bf16), NN) * eb
                dk_state = _dot(vi, dsb, NN) * ekend
                dk = dk_state
                dq_i = []
                for sb in range(A_NSUB):
                    da_sb = da[sb * A_SUB:(sb + 1) * A_SUB, :]
                    dq_i.append(_dot(da_sb, kts[sb], NN) * eqs[sb])
                    dk = dk + _dot(da_sb, qts[sb], TN) * eks[sb]
                dq = dq + jnp.concatenate(dq_i, axis=0)
                db = qf * dq - k * dk
                extra = jnp.sum(k * dk_state, axis=0, keepdims=True) + eend * jnp.sum(s0 * dsend, axis=0, keepdims=True)
                dlogf = _dot(tri, db, TN, precision=lax.Precision.HIGHEST) + extra
                dstate[hh] = _dot(do, qe, TN) + eend * dsend
                d_pre = jnp.where(f > TINY, dlogf / f, 0.0) - dk
                dlb_p.append(jnp.sum((1.0 - sig) * d_pre, axis=0, keepdims=True))
                df_p.append((1.0 - lbh) * d_pre * sig * (1.0 - sig))
                dq_p.append(dq * (sq * (1.0 + qr * (1.0 - sq))))
                di_p.append(dv)
            dq_ref[rows, :] = jnp.concatenate(dq_p, axis=1).astype(bf16)
            df_ref[rows, :] = jnp.concatenate(df_p, axis=1).astype(bf16)
            di_ref[rows, :] = jnp.concatenate(di_p, axis=1).astype(bf16)
            dg_ref[rows, :] = jnp.concatenate(dg_p, axis=1).astype(bf16)
            dlb_ref[...] += jnp.concatenate(dlb_p, axis=1)
            dng_ref[0] += dng_acc
            return carry

        lax.fori_loop(0, A_NCH, chunk, 0)

    rev = lambda p, i: (nb - 1 - i, p)
    return _call(
        body, name="hgrn_bwd", grid=(A_HEADS // A_HP, nb),
        in_specs=_hgrn_in_specs(nb) + [pl.BlockSpec((A_TB, A_VW), rev),
                                       pl.BlockSpec((A_NCH, A_HP, A_V, A_K), lambda p, i: (nb - 1 - i, p, 0, 0)),
                                       pl.BlockSpec((A_TB, A_VW), rev)],
        out_specs=[pl.BlockSpec((A_TB, A_KW), rev), pl.BlockSpec((A_TB, A_KW), rev),
                   pl.BlockSpec((A_TB, A_VW), rev), pl.BlockSpec((A_TB, A_VW), rev),
                   pl.BlockSpec((1, A_KW), lambda p, i: (0, p)), pl.BlockSpec((1, 1, A_V), lambda p, i: (p, 0, 0))],
        out_shape=[jax.ShapeDtypeStruct((S, 768), bf16), jax.ShapeDtypeStruct((S, 768), bf16),
                   jax.ShapeDtypeStruct((S, 384), bf16), jax.ShapeDtypeStruct((S, 384), bf16),
                   jax.ShapeDtypeStruct((1, 768), f32), jax.ShapeDtypeStruct((A_HEADS // A_HP, 1, A_V), f32)],
        scratch_shapes=[pltpu.VMEM((A_HP, A_V, A_K), f32)],
        sem=("parallel", "arbitrary"), args=(u, u, u, u, lb, ng, o, st, dya), side=side)


B_TK = 128
SCALE = HD ** -0.5


def _split(x):
    hi = x.astype(bf16)
    return hi, (x - hi.astype(f32)).astype(bf16)


def _dot2(x, m, dn):
    hi, lo = _split(x)
    return _dot(hi, m, dn) + _dot(lo, m, dn)


def _sb_block(qs, kh, mask, m_gt, c):
    z = _dot(qs, kh, NT)
    sp = jnp.maximum(z, 0.0) + jnp.log(1.0 + jnp.exp(-jnp.abs(z)))
    lneg = jnp.where(mask, -sp, 0.0)
    lsz = z - sp
    suf = _dot2(lneg, m_gt, NN) + c
    a = jnp.where(mask, jnp.exp(lsz + suf), 0.0)
    return lneg, lsz, a


def _sb_masks(tq, i, jj):
    t_idx = i * tq + lax.broadcasted_iota(jnp.int32, (tq, B_TK), 0)
    s_idx = jj * B_TK + lax.broadcasted_iota(jnp.int32, (tq, B_TK), 1)
    return s_idx < t_idx


def _sb_tri(strict):
    r = lax.broadcasted_iota(jnp.int32, (B_TK, B_TK), 0)
    c = lax.broadcasted_iota(jnp.int32, (B_TK, B_TK), 1)
    return (r > c if strict else r >= c).astype(bf16)


B_DEAD = -88.0


def _sb_walk(nkb, step, init):
    def cond(state):
        it, alive, _ = state
        return jnp.logical_and(it < nkb, alive)

    def body(state):
        it, _, carry = state
        carry = step(it, carry)
        top = jnp.max(functools.reduce(jnp.maximum, [h[1] for h in carry]))
        return it + 1, top > B_DEAD, carry

    return lax.while_loop(cond, body, (jnp.int32(0), jnp.bool_(True), init))[2]


B_HP = 6
B_W = B_HP * HD


def sb_fwd(u, side=None):
    S = u.shape[0]
    tq = 128

    def body(q_ref, k_ref, v_ref, o_ref):
        i = pl.program_id(1)
        nkb = (i + 1) * (tq // B_TK)
        m_gt = _sb_tri(True)
        qs = [(q_ref[:, hh * HD:(hh + 1) * HD] * SCALE).astype(bf16) for hh in range(B_HP)]

        def step(it, carry):
            jj = nkb - 1 - it
            rows = pl.ds(pl.multiple_of(jj * B_TK, B_TK), B_TK)
            mask = _sb_masks(tq, i, jj)
            kb, vb = k_ref[rows, :], v_ref[rows, :]
            out = []
            for hh in range(B_HP):
                acc, c = carry[hh]
                kh = kb[:, hh * HD:(hh + 1) * HD].astype(bf16)
                vh = vb[:, hh * HD:(hh + 1) * HD].astype(bf16)
                lneg, _, a = _sb_block(qs[hh], kh, mask, m_gt, c)
                out.append((acc + _dot2(a, vh, NN), c + jnp.sum(lneg, axis=1, keepdims=True)))
            return tuple(out)

        z0 = (jnp.zeros((tq, HD), f32), jnp.zeros((tq, 1), f32))
        res = _sb_walk(nkb, step, (z0,) * B_HP)
        o_ref[...] = jnp.concatenate([r[0] for r in res], axis=1)

    outs, souts = _call(
        body, name="sb_fwd", grid=(B_HEADS // B_HP, S // tq),
        in_specs=[pl.BlockSpec((tq, B_W), lambda p, i: (i, O_BQ // B_W + p)),
                  pl.BlockSpec((S, B_W), lambda p, i: (0, O_BK // B_W + p)),
                  pl.BlockSpec((S, B_W), lambda p, i: (0, O_BV // B_W + p))],
        out_specs=[pl.BlockSpec((tq, B_W), lambda p, i: (i, p))],
        out_shape=[jax.ShapeDtypeStruct((S, 384), f32)],
        sem=("parallel", "arbitrary"), args=(u, u, u), side=side)
    return outs[0], souts


def sb_bwd(u, yb, dyb, side=None):
    S = u.shape[0]
    tq = 128
    nq = S // tq

    def body(q_ref, k_ref, v_ref, y_ref, dy_ref, dq_ref, dk_out, dv_out, dk_ref, dv_ref, out_sem):
        p, i = pl.program_id(0), pl.program_id(1)

        @pl.when(i == 0)
        def _():
            dk_ref[...] = jnp.zeros_like(dk_ref)
            dv_ref[...] = jnp.zeros_like(dv_ref)

        nkb = (i + 1) * (tq // B_TK)
        m_gt = _sb_tri(True)
        m_ge = _sb_tri(False)
        qs, dos, tot = [], [], []
        for hh in range(B_HP):
            hs = slice(hh * HD, (hh + 1) * HD)
            qs.append((q_ref[:, hs] * SCALE).astype(bf16))
            dob = dy_ref[:, hs].astype(bf16)
            dos.append(dob)
            tot.append(jnp.sum(dob.astype(f32) * y_ref[:, hs], axis=1, keepdims=True))

        def step(it, carry):
            jj = nkb - 1 - it
            rows = pl.ds(pl.multiple_of(jj * B_TK, B_TK), B_TK)
            mask = _sb_masks(tq, i, jj)
            kb, vb = k_ref[rows, :], v_ref[rows, :]
            out, dk_p, dv_p = [], [], []
            for hh in range(B_HP):
                dq, c, cg = carry[hh]
                kh = kb[:, hh * HD:(hh + 1) * HD].astype(bf16)
                vh = vb[:, hh * HD:(hh + 1) * HD].astype(bf16)
                lneg, lsz, a = _sb_block(qs[hh], kh, mask, m_gt, c)
                g = a * _dot(dos[hh], vh, NT)
                pre = tot[hh] - cg - _dot2(g, m_ge, NN)
                beta = jnp.exp(lsz)
                dz = jnp.where(mask, g * (1.0 - beta) - beta * pre, 0.0).astype(bf16)
                dk_p.append(_dot(dz, qs[hh], TN))
                dv_p.append(_dot(a.astype(bf16), dos[hh], TN))
                out.append((dq + _dot(dz, kh, NN), c + jnp.sum(lneg, axis=1, keepdims=True),
                            cg + jnp.sum(g, axis=1, keepdims=True)))
            dk_ref[rows, :] += jnp.concatenate(dk_p, axis=1)
            dv_ref[rows, :] += jnp.concatenate(dv_p, axis=1)
            return tuple(out)

        z0 = (jnp.zeros((tq, HD), f32), jnp.zeros((tq, 1), f32), jnp.zeros((tq, 1), f32))
        res = _sb_walk(nkb, step, (z0,) * B_HP)
        dq_ref[...] = jnp.concatenate([r[0] for r in res], axis=1) * SCALE

        @pl.when(i == nq - 1)
        def _():
            cols = pl.ds(pl.multiple_of(p * B_W, LANE), B_W)
            ck = pltpu.make_async_copy(dk_ref, dk_out.at[:, cols], out_sem.at[0])
            cv = pltpu.make_async_copy(dv_ref, dv_out.at[:, cols], out_sem.at[1])
            ck.start()
            cv.start()
            ck.wait()
            cv.wait()

    row = pl.BlockSpec((tq, B_W), lambda p, i: (i, p))
    hbm = pl.BlockSpec(memory_space=pl.ANY)
    osd = jax.ShapeDtypeStruct((S, 384), f32)
    return _call(
        body, name="sb_bwd", grid=(B_HEADS // B_HP, nq),
        in_specs=[pl.BlockSpec((tq, B_W), lambda p, i: (i, O_BQ // B_W + p)),
                  pl.BlockSpec((S, B_W), lambda p, i: (0, O_BK // B_W + p)),
                  pl.BlockSpec((S, B_W), lambda p, i: (0, O_BV // B_W + p)), row, row],
        out_specs=[row, hbm, hbm], out_shape=[osd, osd, osd],
        scratch_shapes=[pltpu.VMEM((S, B_W), f32), pltpu.VMEM((S, B_W), f32), pltpu.SemaphoreType.DMA((2,))],
        sem=("parallel", "arbitrary"), args=(u, u, u, yb, dyb), side=side)


def _dil_rows(i, rho, r):
    if r == 1:
        return pl.ds(pl.multiple_of(i * C_BLK, C_BLK), C_BLK)
    return pl.ds(i * (C_BLK * r) + rho, C_BLK, stride=r)


def _dil_scores(qs, kc, kp, i, slope_r):
    qi = lax.broadcasted_iota(jnp.int32, (C_BLK, C_BLK), 0)
    kj = lax.broadcasted_iota(jnp.int32, (C_BLK, C_BLK), 1)
    d_c = qi - kj
    d_p = d_c + C_BLK
    ok_c = d_c >= 0
    ok_p = jnp.logical_and(d_c <= 0, i > 0)
    s_c = jnp.where(ok_c, _dot(qs, kc, NT) - slope_r * d_c.astype(f32), NEG_BIG)
    s_p = jnp.where(ok_p, _dot(qs, kp, NT) - slope_r * d_p.astype(f32), NEG_BIG)
    return s_c, s_p, ok_c, ok_p


def _dil_slope(g, r, hh):
    pair = pl.program_id(0)
    return jnp.where(pair == 0, C_SLOPES[4 * g + hh] * r, C_SLOPES[4 * g + 2 + hh] * r).astype(f32)


def _dil_u_specs(g, S):
    def im(off):
        return lambda p, rho: (0, (off + g * 256) // LANE + p)
    return [pl.BlockSpec((S, LANE), im(O_CQ)), pl.BlockSpec((S, LANE), im(O_CK)), pl.BlockSpec((S, LANE), im(O_CV))]


def dil_fwd(u, g, side=None):
    S = u.shape[0]
    r = C_GROUPS[g][1]
    nbk = S // r // C_BLK

    def body(q_ref, k_ref, v_ref, o_ref, l_ref):
        rho = pl.program_id(1)

        def step(i, carry):
            rc = _dil_rows(i, rho, r)
            rp = _dil_rows(jnp.maximum(i - 1, 0), rho, r)
            q2, kc2, kp2, vc2, vp2 = q_ref[rc, :], k_ref[rc, :], k_ref[rp, :], v_ref[rc, :], v_ref[rp, :]
            o_p, l_p = [], []
            for hh in range(2):
                hs = slice(hh * HD, (hh + 1) * HD)
                qs = (q2[:, hs] * SCALE).astype(bf16)
                kc, kp = kc2[:, hs].astype(bf16), kp2[:, hs].astype(bf16)
                vc, vp = vc2[:, hs].astype(bf16), vp2[:, hs].astype(bf16)
                s_c, s_p, _, _ = _dil_scores(qs, kc, kp, i, _dil_slope(g, r, hh))
                m = jnp.maximum(jnp.max(s_c, axis=1, keepdims=True), jnp.max(s_p, axis=1, keepdims=True))
                p_c, p_p = jnp.exp(s_c - m), jnp.exp(s_p - m)
                den = jnp.sum(p_c, axis=1, keepdims=True) + jnp.sum(p_p, axis=1, keepdims=True)
                o_p.append((_dot(p_c.astype(bf16), vc, NN) + _dot(p_p.astype(bf16), vp, NN)) / den)
                l_p.append(jnp.broadcast_to(m + jnp.log(den), (C_BLK, HD)))
            o_ref[rc, :] = jnp.concatenate(o_p, axis=1)
            l_ref[rc, :] = jnp.concatenate(l_p, axis=1)
            return carry

        lax.fori_loop(0, nbk, step, 0, unroll=2)

    ospec = pl.BlockSpec((S, LANE), lambda p, rho: (0, p))
    osd = jax.ShapeDtypeStruct((S, 256), f32)
    return _call(
        body, name=f"dil_fwd{g}", grid=(2, r),
        in_specs=_dil_u_specs(g, S), out_specs=[ospec, ospec], out_shape=[osd, osd],
        sem=("parallel", "arbitrary"), args=(u, u, u), side=side)


def dil_merge(os_, ls_):
    S = os_[0].shape[0]

    def body(o0, o1, o2, l0, l1, l2, y_ref, lse_ref):
        a, b, c = l0[...], l1[...], l2[...]
        m = jnp.maximum(jnp.maximum(a, b), c)
        ea, eb, ec = jnp.exp(a - m), jnp.exp(b - m), jnp.exp(c - m)
        den = ea + eb + ec
        y_ref[...] = (ea * o0[...] + eb * o1[...] + ec * o2[...]) / den
        lse_ref[...] = m + jnp.log(den)

    spec = pl.BlockSpec((512, 256), lambda i: (i, 0))
    osd = jax.ShapeDtypeStruct((S, 256), f32)
    return pl.pallas_call(
        body, name="dil_merge", grid=(S // 512,), in_specs=[spec] * 6, out_specs=[spec, spec],
        out_shape=[osd, osd], compiler_params=_cp(("parallel",)),
    )(*os_, *ls_)


def dil_bwd(u, g, dyc, yc, lse):
    S = u.shape[0]
    r = C_GROUPS[g][1]
    nbk = S // r // C_BLK

    def body(q_ref, k_ref, v_ref, dy_ref, y_ref, l_ref, dq_ref, dk_ref, dv_ref):
        rho = pl.program_id(1)

        @pl.when(rho == 0)
        def _():
            dk_ref[...] = jnp.zeros_like(dk_ref)
            dv_ref[...] = jnp.zeros_like(dv_ref)

        def step(i, carry):
            rc = _dil_rows(i, rho, r)
            rp = _dil_rows(jnp.maximum(i - 1, 0), rho, r)
            q2, kc2, kp2, vc2, vp2 = q_ref[rc, :], k_ref[rc, :], k_ref[rp, :], v_ref[rc, :], v_ref[rp, :]
            dy2, y2, l2 = dy_ref[rc, :], y_ref[rc, :], l_ref[rc, :]
            dq_p, dkc_p, dkp_p, dvc_p, dvp_p = [], [], [], [], []
            for hh in range(2):
                hs = slice(hh * HD, (hh + 1) * HD)
                qs = (q2[:, hs] * SCALE).astype(bf16)
                kc, kp = kc2[:, hs].astype(bf16), kp2[:, hs].astype(bf16)
                vc, vp = vc2[:, hs].astype(bf16), vp2[:, hs].astype(bf16)
                dy = dy2[:, hs]
                dyb = dy.astype(bf16)
                s_c, s_p, ok_c, ok_p = _dil_scores(qs, kc, kp, i, _dil_slope(g, r, hh))
                lrow = l2[:, hh * HD:hh * HD + 1]
                delta = jnp.sum(dy * y2[:, hs], axis=1, keepdims=True)
                pi_c = jnp.where(ok_c, jnp.exp(s_c - lrow), 0.0)
                pi_p = jnp.where(ok_p, jnp.exp(s_p - lrow), 0.0)
                ds_c = (pi_c * (_dot(dyb, vc, NT) - delta)).astype(bf16)
                ds_p = (pi_p * (_dot(dyb, vp, NT) - delta)).astype(bf16)
                dq_p.append((_dot(ds_c, kc, NN) + _dot(ds_p, kp, NN)) * SCALE)
                dkc_p.append(_dot(ds_c, qs, TN))
                dkp_p.append(_dot(ds_p, qs, TN))
                dvc_p.append(_dot(pi_c.astype(bf16), dyb, TN))
                dvp_p.append(_dot(pi_p.astype(bf16), dyb, TN))
            dq_ref[rc, :] = jnp.concatenate(dq_p, axis=1)
            dk_ref[rc, :] += jnp.concatenate(dkc_p, axis=1)
            dv_ref[rc, :] += jnp.concatenate(dvc_p, axis=1)
            dk_ref[rp, :] += jnp.concatenate(dkp_p, axis=1)
            dv_ref[rp, :] += jnp.concatenate(dvp_p, axis=1)
            return carry

        lax.fori_loop(0, nbk, step, 0, unroll=2)

    ospec = pl.BlockSpec((S, LANE), lambda p, rho: (0, p))
    osd = jax.ShapeDtypeStruct((S, 256), f32)
    return pl.pallas_call(
        body, name=f"dil_bwd{g}", grid=(2, r),
        in_specs=_dil_u_specs(g, S) + [ospec, ospec, ospec], out_specs=[ospec] * 3, out_shape=[osd] * 3,
        compiler_params=_cp(("parallel", "arbitrary")),
    )(u, u, u, dyc, yc, lse)


def _rows_tile(rows):
    return _tile(rows, (256, 176, 128, 64, 32, 16, 8))


def cast_bf16(w):
    shape = w.shape
    w2 = w.reshape(-1, shape[-1])
    rows, cols = w2.shape
    tr = _rows_tile(rows)

    def body(x_ref, o_ref):
        o_ref[...] = x_ref[...].astype(bf16)

    spec = pl.BlockSpec((tr, cols), lambda i: (i, 0))
    out = pl.pallas_call(
        body, name="cast_bf16", grid=(rows // tr,), in_specs=[spec], out_specs=spec,
        out_shape=jax.ShapeDtypeStruct((rows, cols), bf16), compiler_params=_cp(("parallel",)),
    )(w2)
    return out.reshape(shape)


BC1 = 1.0 - ADAM_B1 ** ADAM_STEP
BC2 = 1.0 - ADAM_B2 ** ADAM_STEP


def _adam_math(w, g, m, v):
    m2 = ADAM_B1 * m + (1.0 - ADAM_B1) * g
    v2 = ADAM_B2 * v + (1.0 - ADAM_B2) * (g * g)
    delta = -ADAM_LR * ((m2 / BC1) / (jnp.sqrt(v2 / BC2) + ADAM_EPS) + ADAM_WD * w)
    return delta, m2, v2


def adam(w, g, m, v):
    shape = w.shape
    r2 = lambda t: t.reshape(-1, shape[-1])
    rows, cols = r2(w).shape
    tr = _rows_tile(rows)

    def body(w_ref, g_ref, m_ref, v_ref, d_ref, m2_ref, v2_ref):
        d_ref[...], m2_ref[...], v2_ref[...] = _adam_math(w_ref[...], g_ref[...], m_ref[...], v_ref[...])

    spec = pl.BlockSpec((tr, cols), lambda i: (i, 0))
    osd = jax.ShapeDtypeStruct((rows, cols), f32)
    outs = pl.pallas_call(
        body, name="adam", grid=(rows // tr,), in_specs=[spec] * 4, out_specs=[spec] * 3, out_shape=[osd] * 3,
        compiler_params=_cp(("parallel",)),
    )(r2(w), r2(g), r2(m), r2(v))
    return [o.reshape(shape) for o in outs]


def adam_layers(w, gs, m, v):
    depth, rows, cols = w.shape
    tr = _tile(rows, (128, 64, 32, 16, 8))
    nb = rows // tr

    def body(w_ref, m_ref, v_ref, *rest):
        g_refs, (g_out, d_ref, m2_ref, v2_ref) = rest[:depth], rest[depth:]
        for k in range(depth):
            @pl.when(pl.program_id(0) == k)
            def _(k=k):
                g = g_refs[k][...]
                g_out[0] = g
                d_ref[0], m2_ref[0], v2_ref[0] = _adam_math(w_ref[0], g, m_ref[0], v_ref[0])

    def g_spec(k):
        return pl.BlockSpec((tr, cols), lambda l, i: (jnp.where(l < k, 0, jnp.where(l > k, nb - 1, i)), 0))

    wspec = pl.BlockSpec((1, tr, cols), lambda l, i: (l, i, 0))
    osd = jax.ShapeDtypeStruct(w.shape, f32)
    return pl.pallas_call(
        body, name="adam_layers", grid=(depth, nb), in_specs=[wspec] * 3 + [g_spec(k) for k in range(depth)],
        out_specs=[wspec] * 4, out_shape=[osd] * 4, compiler_params=_cp(("arbitrary", "arbitrary")),
    )(w, m, v, *gs)


ADA_N = 9 * D // 4
ADA_TN = 384


def ada_fwd(c_all, w_ada):
    def body(c_ref, w_ref, o_ref):
        cv = c_ref[...]
        o_ref[0] = _dot((cv * _sig(cv)).astype(bf16), w_ref[0].astype(bf16), NN)

    return pl.pallas_call(
        body, name="ada_fwd", grid=(DEPTH, ADA_N // ADA_TN),
        in_specs=[pl.BlockSpec((8, D), lambda l, j: (0, 0)), pl.BlockSpec((1, D, ADA_TN), lambda l, j: (l, 0, j))],
        out_specs=pl.BlockSpec((1, 8, ADA_TN), lambda l, j: (l, 0, j)),
        out_shape=jax.ShapeDtypeStruct((DEPTH, 8, ADA_N), f32), compiler_params=_cp(("parallel", "parallel")),
    )(c_all, w_ada)


def ada_bwd_adam(c_all, dm, w, m, v, side=None):
    tr = 128

    def body(c_ref, dm_ref, w_ref, m_ref, v_ref, g_ref, d_ref, m2_ref, v2_ref):
        cv = c_ref[...]
        g = _dot((cv * _sig(cv)).astype(bf16), dm_ref[0].astype(bf16), TN)
        g_ref[0] = g
        d_ref[0], m2_ref[0], v2_ref[0] = _adam_math(w_ref[0], g, m_ref[0], v_ref[0])

    wspec = pl.BlockSpec((1, tr, ADA_N), lambda l, i: (l, i, 0))
    osd = jax.ShapeDtypeStruct((DEPTH, D, ADA_N), f32)
    return _call(
        body, name="ada_bwd_adam", grid=(DEPTH, D // tr),
        in_specs=[pl.BlockSpec((8, tr), lambda l, i: (0, i)), pl.BlockSpec((1, 8, ADA_N), lambda l, i: (l, 0, 0)),
                  wspec, wspec, wspec],
        out_specs=[wspec] * 4, out_shape=[osd] * 4, sem=("parallel", "parallel"), args=(c_all, dm, w, m, v), side=side)


def _lb_probs(x):
    mx = jnp.max(x, axis=0, keepdims=True)
    e = jnp.exp(x - mx)
    return e / jnp.sum(e, axis=0, keepdims=True)


def lb_fwd(logits):
    def body(x_ref, o_ref):
        p = _lb_probs(x_ref[...])
        rows = [jnp.zeros((1, 768), f32)]
        for l in range(1, DEPTH):
            rows.append(rows[-1] + p[l:l + 1, :])
        o_ref[...] = jnp.concatenate(rows, axis=0)

    return pl.pallas_call(body, name="lb_fwd", out_shape=jax.ShapeDtypeStruct((DEPTH, 768), f32))(logits)


def lb_bwd(logits, dlb):
    def body(x_ref, d_ref, o_ref):
        p = _lb_probs(x_ref[...])
        d = d_ref[...]
        rows = [jnp.zeros((1, 768), f32)] * DEPTH
        acc = jnp.zeros((1, 768), f32)
        for l in range(DEPTH - 1, 0, -1):
            acc = acc + d[l:l + 1, :]
            rows[l] = acc
        dp = jnp.concatenate(rows, axis=0)
        o_ref[...] = p * (dp - jnp.sum(p * dp, axis=0, keepdims=True))

    return pl.pallas_call(body, name="lb_bwd", out_shape=jax.ShapeDtypeStruct((DEPTH, 768), f32))(logits, dlb)


def sum_slots(x):
    n, rows, cols = x.shape
    tr = _rows_tile(rows)

    def body(x_ref, o_ref):
        acc = x_ref[0]
        for j in range(1, n):
            acc = acc + x_ref[j]
        o_ref[...] = acc

    return pl.pallas_call(
        body, name="sum_slots", grid=(rows // tr,),
        in_specs=[pl.BlockSpec((n, tr, cols), lambda i: (0, i, 0))], out_specs=pl.BlockSpec((tr, cols), lambda i: (i, 0)),
        out_shape=jax.ShapeDtypeStruct((rows, cols), f32), compiler_params=_cp(("parallel",)),
    )(x)


ANY = pl.BlockSpec(memory_space=pl.ANY)
CHIP_FLIPS = ((1, 0), (0, 1), (1, 1))
DEV_FLIPS = tuple((a, b, d) for a in (0, 1) for b in (0, 1) for d in (0, 1))[1:]


def _me():
    return lax.axis_index("x"), lax.axis_index("y"), lax.axis_index("c")


def _flip(v, f):
    return 1 - v if f else v


def _comm_call(body, name, ins, out_shapes, n_remote, n_local):
    return pl.pallas_call(
        body, name=name, in_specs=[ANY] * len(ins), out_specs=[ANY] * len(out_shapes), out_shape=out_shapes,
        scratch_shapes=[pltpu.SemaphoreType.DMA((n_remote,)), pltpu.SemaphoreType.DMA((n_remote,)),
                        pltpu.SemaphoreType.DMA((max(n_local, 1),))],
    )(*ins)


def run_plan(plan, name):
    ni, no = len(plan.ins), len(plan.outs)

    def body(*refs):
        ins, outs, sems = refs[:ni], refs[ni:ni + no], refs[ni + no:]
        plan.start(ins, outs, *sems)
        plan.wait(ins, outs, *sems)

    return pl.pallas_call(body, name=name, in_specs=[ANY] * ni, out_specs=[ANY] * no, out_shape=list(plan.outs),
                          scratch_shapes=plan.sems())(*plan.ins)


def gather_chips_plan(arrs, layer=None):
    n = len(arrs)
    layers = list(layer) if isinstance(layer, (list, tuple)) else [layer] * n
    shapes = [a.shape if l is None else a.shape[1:] for a, l in zip(arrs, layers)]

    def copies(ins, outs, send, recv, loc):
        x, y, c = _me()
        mine = 2 * x + y
        srcs = [r if l is None else r.at[l] for r, l in zip(ins, layers)]
        locs = [pltpu.make_async_copy(srcs[a], outs[a].at[mine], loc.at[a]) for a in range(n)]

        def remote(a, k, slot):
            fx, fy = CHIP_FLIPS[k]
            return pltpu.make_async_remote_copy(srcs[a], outs[a].at[slot], send.at[3 * a + k], recv.at[3 * a + k],
                                                device_id=(_flip(x, fx), _flip(y, fy), c), device_id_type=MESH)

        peers = [2 * _flip(x, fx) + _flip(y, fy) for fx, fy in CHIP_FLIPS]
        return locs, remote, mine, peers

    def start(ins, outs, send, recv, loc):
        locs, remote, mine, _ = copies(ins, outs, send, recv, loc)
        for cp in locs:
            cp.start()
        for a in range(n):
            for k in range(3):
                remote(a, k, mine).start()

    def wait(ins, outs, send, recv, loc):
        locs, remote, _, peers = copies(ins, outs, send, recv, loc)
        for a in range(n):
            for k in range(3):
                cp = remote(a, k, peers[k])
                cp.wait_recv()
                cp.wait_send()
        for cp in locs:
            cp.wait()

    outs = [jax.ShapeDtypeStruct((4,) + tuple(s), a.dtype) for s, a in zip(shapes, arrs)]
    return Plan(list(arrs), outs, 3 * n, n, start, wait)


def all_gather_chips(arrs, layer=None, name="ag4"):
    return run_plan(gather_chips_plan(arrs, layer), name)


def all_gather_devs(arr, name="ag8"):
    def body(in_ref, out_ref, send, recv, loc):
        x, y, c = _me()
        mine = 4 * x + 2 * y + c
        lc = pltpu.make_async_copy(in_ref, out_ref.at[mine], loc.at[0])
        lc.start()

        def remote(k, slot):
            fx, fy, fc = DEV_FLIPS[k]
            return pltpu.make_async_remote_copy(in_ref, out_ref.at[slot], send.at[k], recv.at[k],
                                                device_id=(_flip(x, fx), _flip(y, fy), _flip(c, fc)), device_id_type=MESH)

        for k in range(7):
            remote(k, mine).start()
        for k, (fx, fy, fc) in enumerate(DEV_FLIPS):
            cp = remote(k, 4 * _flip(x, fx) + 2 * _flip(y, fy) + _flip(c, fc))
            cp.wait_recv()
            cp.wait_send()
        lc.wait()

    return _comm_call(body, name, [arr], [jax.ShapeDtypeStruct((8,) + arr.shape, arr.dtype)], 7, 1)[0]


def _rows_of(which, rows):
    return pl.ds(pl.multiple_of(which * rows, 16), rows)


def dev_exchange_plan(parts):
    n = len(parts)

    def copies(ins, outs, send, recv, loc):
        x, y, c = _me()
        mine = 4 * x + 2 * y + c

        def piece(a, px, py, pc):
            rows = ins[a].shape[1] // 2
            return ins[a].at[2 * px + py, _rows_of(pc, rows), :]

        locs = [pltpu.make_async_copy(piece(a, x, y, c), outs[a].at[mine], loc.at[a]) for a in range(n)]

        def remote(a, k, slot):
            fx, fy, fc = DEV_FLIPS[k]
            px, py, pc = _flip(x, fx), _flip(y, fy), _flip(c, fc)
            return pltpu.make_async_remote_copy(piece(a, px, py, pc), outs[a].at[slot], send.at[7 * a + k], recv.at[7 * a + k],
                                                device_id=(px, py, pc), device_id_type=MESH)

        peers = [4 * _flip(x, fx) + 2 * _flip(y, fy) + _flip(c, fc) for fx, fy, fc in DEV_FLIPS]
        return locs, remote, mine, peers

    def start(ins, outs, send, recv, loc):
        locs, remote, mine, _ = copies(ins, outs, send, recv, loc)
        for cp in locs:
            cp.start()
        for a in range(n):
            for k in range(7):
                remote(a, k, mine).start()

    def wait(ins, outs, send, recv, loc):
        locs, remote, _, peers = copies(ins, outs, send, recv, loc)
        for a in range(n):
            for k in range(7):
                cp = remote(a, k, peers[k])
                cp.wait_recv()
                cp.wait_send()
        for cp in locs:
            cp.wait()

    outs = [jax.ShapeDtypeStruct((8, p.shape[1] // 2, p.shape[2]), p.dtype) for p in parts]
    return Plan(list(parts), outs, 7 * n, n, start, wait)


def sum_share(slots, name="rs_sum"):
    n, r, cols = slots.shape
    tr = _tile(r, (128, 176, 64))
    steps = r // tr

    def body(s_ref, g_ref, buf, send, loc, recv):
        i = pl.program_id(0)
        x, y, c = _me()
        slot = i % 2

        def copies(step, sl):
            rows = pl.ds(pl.multiple_of(c * r + step * tr, 8), tr)
            rem = pltpu.make_async_remote_copy(buf.at[sl], g_ref.at[rows, :], send.at[sl], recv.at[0],
                                               device_id=(x, y, 1 - c), device_id_type=MESH)
            return rem, pltpu.make_async_copy(buf.at[sl], g_ref.at[rows, :], loc.at[sl])

        @pl.when(i >= 2)
        def _():
            rem, lc = copies(i - 2, slot)
            rem.wait_send()
            lc.wait()

        acc = s_ref[0].astype(f32)
        for j in range(1, n):
            acc = acc + s_ref[j].astype(f32)
        buf[slot] = acc
        rem, lc = copies(i, slot)
        rem.start()
        lc.start()

        @pl.when(i == steps - 1)
        def _():
            for back in range(min(2, steps)):
                rem, lc = copies(i - back, (i - back) % 2)
                rem.wait_send()
                lc.wait()
            other = g_ref.at[pl.ds(pl.multiple_of((1 - c) * r, 8), r), :]
            pltpu.make_async_remote_copy(other, other, send.at[0], recv.at[0],
                                         device_id=(x, y, 1 - c), device_id_type=MESH).wait_recv()

    return pl.pallas_call(
        body, name=name, grid=(steps,),
        in_specs=[pl.BlockSpec((n, tr, cols), lambda i: (0, i, 0))], out_specs=ANY,
        out_shape=jax.ShapeDtypeStruct((2 * r, cols), f32),
        scratch_shapes=[pltpu.VMEM((2, tr, cols), f32), pltpu.SemaphoreType.DMA((2,)), pltpu.SemaphoreType.DMA((2,)),
                        pltpu.SemaphoreType.DMA((1,))],
        compiler_params=_cp(("arbitrary",)),
    )(slots)


BIG = ("ffn1_w_in", "ffn1_w_out", "w_in", "w_branch_a", "w_branch_b", "w_branch_c", "w_out", "ffn2_w_in", "ffn2_w_out")
ROW_SHARDED = ("ffn1_w_out", "w_out", "ffn2_w_out")
RES_W = (0.5, 1.0, 0.5)


def _full_weight(name, g):
    if name in ROW_SHARDED:
        return g.reshape(4 * g.shape[1], g.shape[2])
    return jnp.concatenate([g[0], g[1], g[2], g[3]], axis=1)


def _by_shard(name, dw):
    if name in ROW_SHARDED:
        return dw.reshape(4, dw.shape[0] // 4, dw.shape[1])
    return dw.reshape(dw.shape[0], 4, dw.shape[1] // 4).transpose(1, 0, 2)


def _full_weight_t(name, g):
    if name in ROW_SHARDED:
        return g.reshape(4 * g.shape[1], g.shape[2]).T
    return g.transpose(0, 2, 1).reshape(4 * g.shape[2], g.shape[1])


def _ffn_fwd(x, w_in, w_out, a_vec, sh_vec, b_vec, plans=(None, None)):
    h, h_t = prenorm(x, a_vec, sh_vec)
    (ua, ub, s, s_t), side0 = ffn_in_swiglu(h, w_in, side=plans[0])
    y = mm(s, w_out, name="ffn_out", side=plans[1])
    side1 = None
    if plans[1] is not None:
        y, side1 = y
    return postnorm(x, y, b_vec), (x, h_t, ua, ub, s_t, y), (side0, side1)


def _ffn_bwd(dout, saved, w_in_t, w_out_t, a_vec, b_vec, plans=None):
    x, h_t, ua, ub, s_t, y = saved
    riders = plans or (None, None, None)
    dy, db = post_bwd(dout, y, b_vec)
    dw_out = mm(s_t, dy, out_dtype=bf16, name="ffn_dwo", side=riders[0])
    du = ffn_du(dy, w_out_t, ua, ub)
    dh = mm(du, w_in_t, name="ffn_dh", side=riders[1])
    sides = []
    if plans:
        (dw_out, s0), (dh, s1) = dw_out, dh
        sides = [s0, s1]
    dw_in = mm(h_t, du, out_dtype=bf16, name="ffn_dwi", side=riders[2])
    if plans:
        dw_in, s2 = dw_in
        sides.append(s2)
    dx, dsh, da = pre_bwd(dout, dh, x, a_vec)
    return (dx, dw_in, dw_out, dsh, da, db) + ((sides,) if plans else ())


def _mix_fwd(x, w, lb, ng, a_vec, sh_vec, b_vec, plans=(None,) * 6):
    h, h_t = prenorm(x, a_vec, sh_vec)
    u = mm(h, w["w_in"], name="mix_in", side=plans[0])
    side0 = None
    if plans[0] is not None:
        u, side0 = u
    (o, ya, st), side1 = hgrn_fwd(u, lb, ng, side=plans[1])
    yb, side2 = sb_fwd(u, side=plans[2])
    groups, dil_sides = [], []
    for g in range(3):
        og, sg = dil_fwd(u, g, side=plans[3 + g])
        groups.append(og)
        dil_sides.append(sg)
    yc, lse = dil_merge([o_ for o_, _ in groups], [l_ for _, l_ in groups])
    pa = mm(ya, w["w_branch_a"], name="mix_pa")
    pb = mm(yb, w["w_branch_b"], name="mix_pb")
    pc = mm(yc, w["w_branch_c"], name="mix_pc")
    merged, merged_t = gate_merge(u, pa, pb, pc)
    z = mm(merged, w["w_out"], name="mix_out")
    return (postnorm(x, z, b_vec), (x, h_t, u, o, ya, st, yb, yc, lse, pa, pb, pc, merged_t, z),
            (side0, side1, side2, *dil_sides))


def _mix_bwd(dout, saved, wt, lb, ng, a_vec, b_vec, plans=(None,) * 3):
    x, h_t, u, o, ya, st, yb, yc, lse, pa, pb, pc, merged_t, z = saved
    dz, db = post_bwd(dout, z, b_vec)
    dmerged = mm(dz, wt["w_out"], name="mix_dm")
    dw_out = mm(merged_t, dz, out_dtype=bf16, name="mix_dwo")
    dpa, dpb, dpc, dg0, dg1, dg2 = gate_bwd(dmerged, u, pa, pb, pc)
    dya = mm(dpa, wt["w_branch_a"], name="mix_dya")
    dyb = mm(dpb, wt["w_branch_b"], name="mix_dyb")
    dyc = mm(dpc, wt["w_branch_c"], name="mix_dyc")
    dw_a = mm(ya.T, dpa, out_dtype=bf16, name="mix_dwa")
    dw_b = mm(yb.astype(bf16).T, dpb, out_dtype=bf16, name="mix_dwb")
    dw_c = mm(yc.astype(bf16).T, dpc, out_dtype=bf16, name="mix_dwc")
    (daq, daf, dai, dag, dlb, dng), side0 = hgrn_bwd(u, lb, ng, o, st, dya, side=plans[0])
    (dbq, dbk, dbv), side1 = sb_bwd(u, yb, dyb, side=plans[1])
    dc = [dil_bwd(u, g, dyc, yc, lse) for g in range(3)]
    du = jnp.concatenate(
        [daq, daf, dai, dag] + [t.astype(bf16) for t in (dbq, dbk, dbv)]
        + [dc[g][j].astype(bf16) for j in range(3) for g in range(3)] + [dg0, dg1, dg2], axis=1)
    dh = mm(du, wt["w_in"], name="mix_dh")
    dw_in = mm(h_t, du, out_dtype=bf16, name="mix_dwi", side=plans[2])
    side2 = None
    if plans[2] is not None:
        dw_in, side2 = dw_in
    dx, dsh, da = pre_bwd(dout, dh, x, a_vec)
    grads = {"w_in": dw_in, "w_out": dw_out, "w_branch_a": dw_a, "w_branch_b": dw_b, "w_branch_c": dw_c}
    return dx, grads, dlb, jnp.sum(dng, axis=0), dsh, da, db, (side0, side1, side2)


NEXT_RIDERS = (("ffn1_w_out",), (), ("ffn2_w_in",), ("ffn1_w_in",), ("w_in",), (), (),
               ("w_out", "w_branch_a", "w_branch_b", "w_branch_c"), ("ffn2_w_out",), ())
FIRST = ("ffn1_w_in", "ffn1_w_out", "w_in", "w_out", "w_branch_a", "w_branch_b", "w_branch_c")
LATE_RIDERS = ((), (), (), (), (), ("ffn2_w_in",), ("ffn2_w_out",), (), (), ())
BWD_RIDERS = (("ffn1_w_in", "ffn2_w_out", "w_branch_a", "w_branch_b", "w_branch_c"), ("w_in", "ffn1_w_out", "w_out"),
              ("ffn2_w_in",))
LAST_RIDERS = (("ffn2_w_out", "w_out", "w_branch_a", "w_branch_b", "w_branch_c"), ("ffn2_w_in",), ("w_in",))
TAIL = ("ffn1_w_in", "ffn1_w_out")


def kernel(x, c, w_ada, b_ada, norm_g, ffn1_w_in, ffn1_w_out, w_in, hgrn_lb_logits, hgrn_norm_g, w_branch_a, w_branch_b, w_branch_c, w_out, ffn2_w_in, ffn2_w_out, loss_target, m_w_ada, m_b_ada, m_norm_g, m_ffn1_w_in, m_ffn1_w_out, m_w_in, m_hgrn_lb_logits, m_hgrn_norm_g, m_w_branch_a, m_w_branch_b, m_w_branch_c, m_w_out, m_ffn2_w_in, m_ffn2_w_out, v_w_ada, v_b_ada, v_norm_g, v_ffn1_w_in, v_ffn1_w_out, v_w_in, v_hgrn_lb_logits, v_hgrn_norm_g, v_w_branch_a, v_w_branch_b, v_w_branch_c, v_w_out, v_ffn2_w_in, v_ffn2_w_out):
    weights = dict(w_ada=w_ada, b_ada=b_ada, norm_g=norm_g, ffn1_w_in=ffn1_w_in, ffn1_w_out=ffn1_w_out, w_in=w_in,
                   hgrn_lb_logits=hgrn_lb_logits, hgrn_norm_g=hgrn_norm_g, w_branch_a=w_branch_a, w_branch_b=w_branch_b,
                   w_branch_c=w_branch_c, w_out=w_out, ffn2_w_in=ffn2_w_in, ffn2_w_out=ffn2_w_out)
    mom = dict(w_ada=m_w_ada, b_ada=m_b_ada, norm_g=m_norm_g, ffn1_w_in=m_ffn1_w_in, ffn1_w_out=m_ffn1_w_out, w_in=m_w_in,
               hgrn_lb_logits=m_hgrn_lb_logits, hgrn_norm_g=m_hgrn_norm_g, w_branch_a=m_w_branch_a, w_branch_b=m_w_branch_b,
               w_branch_c=m_w_branch_c, w_out=m_w_out, ffn2_w_in=m_ffn2_w_in, ffn2_w_out=m_ffn2_w_out)
    var = dict(w_ada=v_w_ada, b_ada=v_b_ada, norm_g=v_norm_g, ffn1_w_in=v_ffn1_w_in, ffn1_w_out=v_ffn1_w_out, w_in=v_w_in,
               hgrn_lb_logits=v_hgrn_lb_logits, hgrn_norm_g=v_hgrn_norm_g, w_branch_a=v_w_branch_a, w_branch_b=v_w_branch_b,
               w_branch_c=v_w_branch_c, w_out=v_w_out, ffn2_w_in=v_ffn2_w_in, ffn2_w_out=v_ffn2_w_out)
    order = list(weights)
    xi, yi, ci = _me()
    chip = 2 * xi + yi
    dev = 4 * xi + 2 * yi + ci
    xs = x[0]

    c_all = all_gather_devs(c, name="ag8_c").reshape(8, D)
    mod_sh = all_gather_chips([ada_fwd(c_all, w_ada)], name="ag4_mod")[0]
    mod_all = mod_sh.transpose(1, 2, 0, 3).reshape(DEPTH, 8, 9 * D)
    mod = lax.dynamic_index_in_dim(mod_all, dev, axis=1, keepdims=False) + b_ada
    mod = mod.reshape(DEPTH, 3, 3, D)
    ng_all = all_gather_chips([norm_g.reshape(DEPTH * 6, D // 4)], name="ag4_norm")[0]
    ng_all = ng_all.reshape(4, DEPTH, 6, D // 4).transpose(1, 2, 0, 3).reshape(DEPTH, 6, D)
    lb_all = lb_fwd(hgrn_lb_logits)
    w16 = {n: cast_bf16(weights[n]) for n in BIG}

    def vecs(l, i):
        shift, scale, gate = mod[l, i, 0][None], mod[l, i, 1][None], mod[l, i, 2][None]
        g_pre, g_post = ng_all[l, 2 * i][None], ng_all[l, 2 * i + 1][None]
        return g_pre * (1.0 + scale), shift, RES_W[i] * gate * g_post

    saved, full = [], []
    gathered = dict(zip(FIRST, all_gather_chips([w16[n] for n in FIRST], layer=0, name="ag4_w0")))
    for l in range(DEPTH):
        riders = [[(n, l) for n in (LATE_RIDERS[h] if l == 0 else ())]
                  + [(n, l + 1) for n in (NEXT_RIDERS[h] if l + 1 < DEPTH else ())] for h in range(len(NEXT_RIDERS))]
        plans = [gather_chips_plan([w16[n] for n, _ in it], layer=[ll for _, ll in it]) if it else None for it in riders]
        coming = {}

        def absorb(hosts, sides):
            for h, outs in zip(hosts, sides):
                for (n, ll), g in zip(riders[h], outs or ()):
                    (gathered if ll == l else coming)[n] = g

        full_w = lambda n: _full_weight(n, gathered[n])
        lb, ng = lb_all[l][None], hgrn_norm_g[l][None]
        xs, s1, sides = _ffn_fwd(xs, full_w("ffn1_w_in"), full_w("ffn1_w_out"), *vecs(l, 0), plans=plans[0:2])
        absorb((0, 1), sides)
        w = {n: full_w(n) for n in ("w_in", "w_branch_a", "w_branch_b", "w_branch_c", "w_out")}
        xs, s2, sides = _mix_fwd(xs, w, lb, ng, *vecs(l, 1), plans=plans[2:8])
        absorb(range(2, 8), sides)
        xs, s3, sides = _ffn_fwd(xs, full_w("ffn2_w_in"), full_w("ffn2_w_out"), *vecs(l, 2), plans=plans[8:10])
        absorb((8, 9), sides)
        saved.append((s1, s2, s3))
        full.append({n: _full_weight_t(n, gathered[n]) for n in BIG})
        gathered = coming

    dx, loss_part = loss_grad(xs, loss_target[0])
    loss = lax.psum(loss_part[0, 0], ("x", "y", "c"))

    big_grads = {n: [None] * DEPTH for n in BIG}
    d_mod, d_ng, d_lb, d_hng = [None] * DEPTH, [None] * DEPTH, [None] * DEPTH, [None] * DEPTH
    pending = None
    for l in reversed(range(DEPTH)):
        wt = full[l]
        s1, s2, s3 = saved[l]
        lb, ng = lb_all[l][None], hgrn_norm_g[l][None]
        rows_mod, rows_ng = [None] * 9, [None] * 6

        def vec_grads(i, dsh, da, db):
            scale, gate = mod[l, i, 1][None], mod[l, i, 2][None]
            g_pre, g_post = ng_all[l, 2 * i][None], ng_all[l, 2 * i + 1][None]
            rows_mod[3 * i], rows_mod[3 * i + 1], rows_mod[3 * i + 2] = dsh, g_pre * da, RES_W[i] * g_post * db
            rows_ng[2 * i], rows_ng[2 * i + 1] = (1.0 + scale) * da, RES_W[i] * gate * db

        a3, _, b3 = vecs(l, 2)
        dx, dwi, dwo, dsh, da, db = _ffn_bwd(dx, s3, wt["ffn2_w_in"], wt["ffn2_w_out"], a3, b3)
        vec_grads(2, dsh, da, db)
        grads = {"ffn2_w_in": dwi, "ffn2_w_out": dwo}
        a2, _, b2 = vecs(l, 1)
        plans = (None,) * len(BWD_RIDERS)
        if pending is not None:
            plans = tuple(dev_exchange_plan([pending[n] for n in names]) for names in BWD_RIDERS)
        dx, gmix, dlb, dhng, dsh, da, db, sides = _mix_bwd(dx, s2, wt, lb, ng, a2, b2, plans=plans)
        if pending is not None:
            for names, outs in zip(BWD_RIDERS, sides):
                for n, slots in zip(names, outs):
                    big_grads[n][l + 1] = sum_share(slots)
        vec_grads(1, dsh, da, db)
        grads.update(gmix)
        a1, _, b1 = vecs(l, 0)
        if l > 0:
            dx, dwi, dwo, dsh, da, db = _ffn_bwd(dx, s1, wt["ffn1_w_in"], wt["ffn1_w_out"], a1, b1)
        else:
            ready = {n: _by_shard(n, grads[n]) for names in LAST_RIDERS for n in names}
            plans = tuple(dev_exchange_plan([ready[n] for n in names]) for names in LAST_RIDERS)
            dx, dwi, dwo, dsh, da, db, sides = _ffn_bwd(dx, s1, wt["ffn1_w_in"], wt["ffn1_w_out"], a1, b1, plans=plans)
            for names, outs in zip(LAST_RIDERS, sides):
                for n, slots in zip(names, outs):
                    big_grads[n][0] = sum_share(slots)
        vec_grads(0, dsh, da, db)
        grads.update({"ffn1_w_in": dwi, "ffn1_w_out": dwo})
        pending = {n: _by_shard(n, grads[n]) for n in (BIG if l > 0 else TAIL)}
        d_mod[l] = jnp.concatenate(rows_mod, axis=1)
        d_ng[l] = jnp.concatenate(rows_ng, axis=0)
        d_lb[l], d_hng[l] = dlb, dhng

    n_small = 6 * D * DEPTH + 768 * DEPTH + A_V * DEPTH + 9 * D * DEPTH
    pad = -n_small % (512 * LANE)
    flat = jnp.concatenate([jnp.stack(d_ng).reshape(-1), jnp.concatenate(d_lb, axis=0).reshape(-1),
                            jnp.concatenate(d_hng, axis=0).reshape(-1), jnp.concatenate(d_mod, axis=0).reshape(-1),
                            jnp.zeros((pad,), f32)])
    small_all = all_gather_devs(flat.reshape(-1, LANE), name="ag8_small")
    total = sum_slots(small_all).reshape(-1)
    o1 = 6 * D * DEPTH
    o2 = o1 + 768 * DEPTH
    o3 = o2 + A_V * DEPTH
    g_ng_full = total[:o1].reshape(DEPTH, 6, D)
    g_lb_all = total[o1:o2].reshape(DEPTH, 768)
    g_small = {
        "norm_g": lax.dynamic_slice_in_dim(g_ng_full, chip * (D // 4), D // 4, axis=2),
        "hgrn_lb_logits": lb_bwd(hgrn_lb_logits, g_lb_all),
        "hgrn_norm_g": total[o2:o3].reshape(DEPTH, A_V),
        "b_ada": total[o3:n_small].reshape(DEPTH, 9 * D),
    }
    dmod_all = small_all.reshape(8, -1)[:, o3:n_small].reshape(8, DEPTH, 9 * D).transpose(1, 0, 2)
    dm_sh = lax.dynamic_slice_in_dim(dmod_all, chip * ADA_N, ADA_N, axis=2)

    out_g, out_d, out_m, out_v = {}, {}, {}, {}
    ada_outs, slots = ada_bwd_adam(c_all, dm_sh, w_ada, m_w_ada, v_w_ada, side=dev_exchange_plan([pending[n] for n in TAIL]))
    out_g["w_ada"], out_d["w_ada"], out_m["w_ada"], out_v["w_ada"] = ada_outs
    for n, s in zip(TAIL, slots):
        big_grads[n][0] = sum_share(s)
    for n in BIG:
        out_g[n], out_d[n], out_m[n], out_v[n] = adam_layers(weights[n], big_grads[n], mom[n], var[n])
    out_g.update(g_small)
    for n in g_small:
        out_d[n], out_m[n], out_v[n] = adam(weights[n], out_g[n], mom[n], var[n])
    return (loss, dx[None], *[out_g[n] for n in order], *[out_d[n] for n in order],
            *[out_m[n] for n in order], *[out_v[n] for n in order])
```

```python
import functools
import math

import jax
import jax.numpy as jnp
from jax import lax
from jax.experimental import pallas as pl
from jax.experimental.pallas import tpu as pltpu

f32, bf16 = jnp.float32, jnp.bfloat16

D = 1024
DEPTH = 4
D_FF = 2816
EPS = 1e-6
NEG_BIG = -1e30
TINY = 1e-30
A_HEADS, A_K, A_V, A_CHUNK = 6, 128, 64, 64
A_SUB = 16
A_CLAMP = 80.0
B_HEADS, HD = 6, 64
C_GROUPS = ((128, 1), (512, 4), (2048, 16))
C_BLK = 128
IN_COLS = 8832
O_AQ, O_AF, O_AI, O_AG = 0, 768, 1536, 1920
O_BQ, O_BK, O_BV = 2304, 2688, 3072
O_CQ, O_CK, O_CV = 3456, 4224, 4992
O_GATE = 5760
LANE = 128
ADAM_LR, ADAM_B1, ADAM_B2, ADAM_EPS, ADAM_WD, ADAM_STEP = 0.001, 0.9, 0.999, 1e-08, 0.01, 10
MESH = pl.DeviceIdType.MESH
VMEM_LIMIT = 56 * 1024 * 1024


def _alibi_slopes(n):
    def pow2(m):
        start = 2.0 ** (-8.0 / m)
        return [start ** (i + 1) for i in range(m)]
    if math.log2(n).is_integer():
        s = pow2(n)
    else:
        c = 2 ** int(math.floor(math.log2(n)))
        s = pow2(c) + pow2(2 * c)[0::2][: n - c]
    return sorted(s, reverse=True)


C_SLOPES = _alibi_slopes(12)


def _tile(n, prefs):
    for p in prefs:
        if n % p == 0:
            return p
    return n


def _cp(sem):
    return pltpu.CompilerParams(dimension_semantics=sem, vmem_limit_bytes=VMEM_LIMIT)


def _sig(x):
    return 1.0 / (1.0 + jnp.exp(-x))


def _dot(a, b, dn, precision=None):
    return lax.dot_general(a, b, (dn, ((), ())), preferred_element_type=f32, precision=precision)


NN = ((1,), (0,))
NT = ((1,), (1,))
TN = ((0,), (0,))


class Plan:
    def __init__(self, ins, outs, n_remote, n_local, start, wait):
        self.ins, self.outs, self.n_remote, self.n_local, self.start, self.wait = ins, outs, n_remote, n_local, start, wait

    def sems(self):
        return [pltpu.SemaphoreType.DMA((self.n_remote,)), pltpu.SemaphoreType.DMA((self.n_remote,)),
                pltpu.SemaphoreType.DMA((max(self.n_local, 1),))]


def _call(body, *, name, grid, in_specs, out_specs, out_shape, sem, args, scratch_shapes=(), side=None):
    if side is None:
        return pl.pallas_call(body, name=name, grid=grid, in_specs=in_specs, out_specs=out_specs, out_shape=out_shape,
                              scratch_shapes=list(scratch_shapes), compiler_params=_cp(sem))(*args), None
    any_spec = pl.BlockSpec(memory_space=pl.ANY)
    n_in, n_out, n_scr = len(in_specs), len(out_specs), len(scratch_shapes)
    s_in, s_out = len(side.ins), len(side.outs)

    def hosted(*refs):
        ins, rest = refs[:n_in], refs[n_in:]
        sins, rest = rest[:s_in], rest[s_in:]
        outs, rest = rest[:n_out], rest[n_out:]
        souts, rest = rest[:s_out], rest[s_out:]
        scr, sems = rest[:n_scr], rest[n_scr:]
        pids = [pl.program_id(d) for d in range(len(grid))]
        first = functools.reduce(jnp.logical_and, [p == 0 for p in pids])
        last = functools.reduce(jnp.logical_and, [p == g - 1 for p, g in zip(pids, grid)])

        @pl.when(first)
        def _():
            side.start(sins, souts, *sems)

        body(*ins, *outs, *scr)

        @pl.when(last)
        def _():
            side.wait(sins, souts, *sems)

    res = pl.pallas_call(
        hosted, name=name, grid=grid, in_specs=list(in_specs) + [any_spec] * s_in,
        out_specs=list(out_specs) + [any_spec] * s_out, out_shape=list(out_shape) + list(side.outs),
        scratch_shapes=list(scratch_shapes) + side.sems(), compiler_params=_cp(("arbitrary",) * len(grid)),
    )(*args, *side.ins)
    return res[:n_out], res[n_out:]


MM_TILES = {
    (4096, 1024, 2816): (1024, 512, 2816),
    (4096, 1024, 5632): (512, 512, 5632),
    (1024, 5632, 4096): (512, 512, 4096),
    (2816, 1024, 4096): (704, 512, 4096),
    (4096, 8832, 1024): (1024, 2944, 1024),
    (4096, 1024, 8832): (1024, 512, 2944),
    (1024, 8832, 4096): (512, 2944, 1024),
    (1024, 1024, 4096): (512, 512, 4096),
    (384, 1024, 4096): (384, 512, 4096),
    (256, 1024, 4096): (256, 512, 4096),
}


def mm(a, b, *, out_dtype=f32, name="mm", side=None):
    M, K = a.shape
    K2, N = b.shape
    assert K == K2, (a.shape, b.shape)
    tm, tn, tk = MM_TILES.get((M, N, K), (_tile(M, (1024, 704, 512, 384, 256, 128)), _tile(N, (512, 384, 256, 128)),
                                          _tile(K, (1024, 512, 1408, 384, 256, 128))))
    nk = K // tk

    def body(a_ref, b_ref, o_ref, *acc):
        p = _dot(a_ref[...].astype(bf16), b_ref[...].astype(bf16), NN)
        if nk == 1:
            o_ref[...] = p.astype(out_dtype)
            return
        acc_ref, = acc
        k = pl.program_id(2)

        @pl.when(k == 0)
        def _():
            acc_ref[...] = p

        @pl.when(k > 0)
        def _():
            acc_ref[...] += p

        @pl.when(k == nk - 1)
        def _():
            o_ref[...] = acc_ref[...].astype(out_dtype)

    outs, souts = _call(
        body, name=name, grid=(M // tm, N // tn, nk),
        in_specs=[pl.BlockSpec((tm, tk), lambda i, j, k: (i, k)), pl.BlockSpec((tk, tn), lambda i, j, k: (k, j))],
        out_specs=[pl.BlockSpec((tm, tn), lambda i, j, k: (i, j))],
        out_shape=[jax.ShapeDtypeStruct((M, N), out_dtype)],
        scratch_shapes=[pltpu.VMEM((tm, tn), f32)] if nk > 1 else [],
        sem=("parallel", "parallel", "arbitrary"), args=(a, b), side=side)
    return outs[0] if side is None else (outs[0], souts)


FF_TM = 1024
FF_T = 256
FF_NB = D_FF // FF_T


def ffn_in_swiglu(h, w_in, side=None):
    S = h.shape[0]

    def body(h_ref, wa_ref, wb_ref, a_ref, b_ref, s_ref, st_ref):
        hv = h_ref[...]
        a = _dot(hv, wa_ref[...], NN)
        b = _dot(hv, wb_ref[...], NN)
        a_ref[...] = a.astype(bf16)
        b_ref[...] = b.astype(bf16)
        s = (a * _sig(a) * b).astype(bf16)
        s_ref[...] = s
        st_ref[...] = s.T

    ospec = pl.BlockSpec((FF_TM, FF_T), lambda i, j: (i, j))
    osd = jax.ShapeDtypeStruct((S, D_FF), bf16)
    return _call(
        body, name="ffn_in", grid=(S // FF_TM, FF_NB),
        in_specs=[pl.BlockSpec((FF_TM, D), lambda i, j: (i, 0)), pl.BlockSpec((D, FF_T), lambda i, j: (0, j)),
                  pl.BlockSpec((D, FF_T), lambda i, j: (0, j + FF_NB))],
        out_specs=[ospec] * 3 + [pl.BlockSpec((FF_T, FF_TM), lambda i, j: (j, i))],
        out_shape=[osd] * 3 + [jax.ShapeDtypeStruct((D_FF, S), bf16)], sem=("parallel", "parallel"),
        args=(h, w_in, w_in), side=side)


def ffn_du(dy, w_out_t, ua, ub):
    S = dy.shape[0]
    tm = 512

    def body(dy_ref, w_ref, a_ref, b_ref, du_ref):
        dyv = dy_ref[...]
        for j in range(FF_NB):
            cols = slice(j * FF_T, (j + 1) * FF_T)
            ds = _dot(dyv, w_ref[:, cols], NN)
            a, b = a_ref[:, cols].astype(f32), b_ref[:, cols].astype(f32)
            sg = _sig(a)
            du_ref[:, cols] = (ds * b * sg * (1.0 + a * (1.0 - sg))).astype(bf16)
            du_ref[:, D_FF + j * FF_T:D_FF + (j + 1) * FF_T] = (ds * a * sg).astype(bf16)

    half = pl.BlockSpec((tm, D_FF), lambda i: (i, 0))
    return pl.pallas_call(
        body, name="ffn_du", grid=(S // tm,),
        in_specs=[pl.BlockSpec((tm, D), lambda i: (i, 0)), pl.BlockSpec((D, D_FF), lambda i: (0, 0)), half, half],
        out_specs=pl.BlockSpec((tm, 2 * D_FF), lambda i: (i, 0)),
        out_shape=jax.ShapeDtypeStruct((S, 2 * D_FF), bf16), compiler_params=_cp(("parallel",)),
    )(dy, w_out_t, ua, ub)


TR = 512


def _row_spec(cols=D):
    return pl.BlockSpec((TR, cols), lambda i: (i, 0))


def _vec_spec(cols=D):
    return pl.BlockSpec((1, cols), lambda i: (0, 0))


def prenorm(x, a_vec, sh_vec):
    S = x.shape[0]

    def body(x_ref, a_ref, s_ref, h_ref, ht_ref):
        xv = x_ref[...]
        rstd = lax.rsqrt(jnp.mean(xv * xv, axis=1, keepdims=True) + EPS)
        h = (xv * rstd * a_ref[...] + s_ref[...]).astype(bf16)
        h_ref[...] = h
        ht_ref[...] = h.T

    return pl.pallas_call(
        body, name="prenorm", grid=(S // TR,),
        in_specs=[_row_spec(), _vec_spec(), _vec_spec()],
        out_specs=[_row_spec(), pl.BlockSpec((D, TR), lambda i: (0, i))],
        out_shape=[jax.ShapeDtypeStruct((S, D), bf16), jax.ShapeDtypeStruct((D, S), bf16)],
        compiler_params=_cp(("parallel",)),
    )(x, a_vec, sh_vec)


def postnorm(x, y, b_vec):
    S = x.shape[0]

    def body(x_ref, y_ref, b_ref, o_ref):
        yv = y_ref[...]
        rstd = lax.rsqrt(jnp.mean(yv * yv, axis=1, keepdims=True) + EPS)
        o_ref[...] = x_ref[...] + b_ref[...] * (yv * rstd)

    return pl.pallas_call(
        body, name="postnorm", grid=(S // TR,),
        in_specs=[_row_spec(), _row_spec(), _vec_spec()], out_specs=_row_spec(),
        out_shape=jax.ShapeDtypeStruct((S, D), f32), compiler_params=_cp(("parallel",)),
    )(x, y, b_vec)


def post_bwd(dout, y, b_vec):
    S = dout.shape[0]

    def body(d_ref, y_ref, b_ref, dy_ref, db_ref):
        i = pl.program_id(0)
        yv, dv = y_ref[...], d_ref[...]
        rstd = lax.rsqrt(jnp.mean(yv * yv, axis=1, keepdims=True) + EPS)
        yh = yv * rstd
        dyh = dv * b_ref[...]
        dy_ref[...] = (rstd * (dyh - yh * jnp.mean(dyh * yh, axis=1, keepdims=True))).astype(bf16)
        part = jnp.sum(dv * yh, axis=0, keepdims=True)

        @pl.when(i == 0)
        def _():
            db_ref[...] = part

        @pl.when(i > 0)
        def _():
            db_ref[...] += part

    return pl.pallas_call(
        body, name="post_bwd", grid=(S // TR,),
        in_specs=[_row_spec(), _row_spec(), _vec_spec()], out_specs=[_row_spec(), _vec_spec()],
        out_shape=[jax.ShapeDtypeStruct((S, D), bf16), jax.ShapeDtypeStruct((1, D), f32)],
        compiler_params=_cp(("arbitrary",)),
    )(dout, y, b_vec)


def pre_bwd(dout, dh, x, a_vec):
    S = dout.shape[0]

    def body(d_ref, dh_ref, x_ref, a_ref, dx_ref, ds_ref, da_ref):
        i = pl.program_id(0)
        xv, dhv = x_ref[...], dh_ref[...]
        rstd = lax.rsqrt(jnp.mean(xv * xv, axis=1, keepdims=True) + EPS)
        n1 = xv * rstd
        dn = dhv * a_ref[...]
        dx_ref[...] = d_ref[...] + rstd * (dn - n1 * jnp.mean(dn * n1, axis=1, keepdims=True))
        p_s = jnp.sum(dhv, axis=0, keepdims=True)
        p_a = jnp.sum(dhv * n1, axis=0, keepdims=True)

        @pl.when(i == 0)
        def _():
            ds_ref[...] = p_s
            da_ref[...] = p_a

        @pl.when(i > 0)
        def _():
            ds_ref[...] += p_s
            da_ref[...] += p_a

    return pl.pallas_call(
        body, name="pre_bwd", grid=(S // TR,),
        in_specs=[_row_spec(), _row_spec(), _row_spec(), _vec_spec()],
        out_specs=[_row_spec(), _vec_spec(), _vec_spec()],
        out_shape=[jax.ShapeDtypeStruct((S, D), f32), jax.ShapeDtypeStruct((1, D), f32), jax.ShapeDtypeStruct((1, D), f32)],
        compiler_params=_cp(("arbitrary",)),
    )(dout, dh, x, a_vec)


def loss_grad(y, tgt):
    S = y.shape[0]

    def body(y_ref, t_ref, dy_ref, l_ref):
        i = pl.program_id(0)
        e = y_ref[...] - t_ref[...]
        dy_ref[...] = e * (1.0 / D)
        part = jnp.sum(jnp.sum(e * e, axis=1, keepdims=True), axis=0, keepdims=True) * (0.5 / D)
        part = jnp.broadcast_to(part, (8, LANE))

        @pl.when(i == 0)
        def _():
            l_ref[...] = part

        @pl.when(i > 0)
        def _():
            l_ref[...] += part

    return pl.pallas_call(
        body, name="loss_grad", grid=(S // TR,),
        in_specs=[_row_spec(), _row_spec()],
        out_specs=[_row_spec(), pl.BlockSpec((8, LANE), lambda i: (0, 0))],
        out_shape=[jax.ShapeDtypeStruct((S, D), f32), jax.ShapeDtypeStruct((8, LANE), f32)],
        compiler_params=_cp(("arbitrary",)),
    )(y, tgt)


G_NB = D // LANE
G_TR = 2048
G_OFF = O_GATE // LANE


def gate_merge(u, pa, pb, pc):
    S = u.shape[0]

    def body(g0, g1, g2, a, b, c, o_ref, ot_ref):
        m = (_sig(g0[...]) * a[...] + _sig(g1[...]) * b[...] + _sig(g2[...]) * c[...]).astype(bf16)
        o_ref[...] = m
        ot_ref[...] = m.T

    gs = [pl.BlockSpec((G_TR, LANE), functools.partial(lambda i, j, k: (i, G_OFF + G_NB * k + j), k=k)) for k in range(3)]
    ps = pl.BlockSpec((G_TR, LANE), lambda i, j: (i, j))
    return pl.pallas_call(
        body, name="gate_merge", grid=(S // G_TR, G_NB),
        in_specs=gs + [ps, ps, ps], out_specs=[ps, pl.BlockSpec((LANE, G_TR), lambda i, j: (j, i))],
        out_shape=[jax.ShapeDtypeStruct((S, D), bf16), jax.ShapeDtypeStruct((D, S), bf16)],
        compiler_params=_cp(("parallel", "parallel")),
    )(u, u, u, pa, pb, pc)


def gate_bwd(dm, u, pa, pb, pc):
    S = u.shape[0]

    def body(dm_ref, g0, g1, g2, a, b, c, da, db, dc, dg0, dg1, dg2):
        d = dm_ref[...]
        for g, p, dp, dg in ((g0, a, da, dg0), (g1, b, db, dg1), (g2, c, dc, dg2)):
            s = _sig(g[...])
            dp[...] = (d * s).astype(bf16)
            dg[...] = (d * p[...] * s * (1.0 - s)).astype(bf16)

    gs = [pl.BlockSpec((G_TR, LANE), functools.partial(lambda i, j, k: (i, G_OFF + G_NB * k + j), k=k)) for k in range(3)]
    ps = pl.BlockSpec((G_TR, LANE), lambda i, j: (i, j))
    osd = jax.ShapeDtypeStruct((S, D), bf16)
    return pl.pallas_call(
        body, name="gate_bwd", grid=(S // G_TR, G_NB),
        in_specs=[ps] + gs + [ps, ps, ps], out_specs=[ps] * 6, out_shape=[osd] * 6,
        compiler_params=_cp(("parallel", "parallel")),
    )(dm, u, u, u, pa, pb, pc)


A_TB = 512
A_NCH = A_TB // A_CHUNK
A_NSUB = A_CHUNK // A_SUB
A_HP = 6
A_KW, A_VW = A_HP * A_K, A_HP * A_V


def _hgrn_gates(qr, fr, lbh):
    sq = _sig(qr)
    sig = _sig(fr)
    f = lbh + (1.0 - lbh) * sig
    logf = jnp.log(jnp.maximum(f, TINY))
    k = (1.0 - lbh) * (1.0 - sig)
    return qr * sq, sq, sig, f, logf, k


def _hgrn_intra(qf, k, b, causal):
    qts, kts, eqs, eks, blocks = [], [], [], [], []
    for sb in range(A_NSUB):
        rs = sb * A_SUB
        r = b[rs - 1:rs, :] if sb else jnp.zeros((1, A_K), f32)
        eq = jnp.exp(b[rs:rs + A_SUB, :] - r)
        ek = jnp.exp(jnp.minimum(r - b, A_CLAMP))
        qt = (qf[rs:rs + A_SUB, :] * eq).astype(bf16)
        kt = (k * ek).astype(bf16)
        blocks.append(_dot(qt, kt, NT))
        qts.append(qt), kts.append(kt), eqs.append(eq), eks.append(ek)
    a = jnp.where(causal, jnp.concatenate(blocks, axis=0), 0.0)
    return a, qts, kts, eqs, eks


def _tri():
    r = lax.broadcasted_iota(jnp.int32, (A_CHUNK, A_CHUNK), 0)
    c = lax.broadcasted_iota(jnp.int32, (A_CHUNK, A_CHUNK), 1)
    return r >= c


def _hgrn_in_specs(rev_nb=None):
    def im(col):
        if rev_nb is None:
            return lambda p, i: (i, col + p)
        return lambda p, i: (rev_nb - 1 - i, col + p)
    return [pl.BlockSpec((A_TB, A_KW), im(O_AQ // A_KW)), pl.BlockSpec((A_TB, A_KW), im(O_AF // A_KW)),
            pl.BlockSpec((A_TB, A_VW), im(O_AI // A_VW)), pl.BlockSpec((A_TB, A_VW), im(O_AG // A_VW)),
            pl.BlockSpec((1, A_KW), lambda p, i: (0, p)), pl.BlockSpec((1, A_V), lambda p, i: (0, 0))]


def hgrn_fwd(u, lb, ng, side=None):
    S = u.shape[0]
    nb = S // A_TB

    def body(q_ref, f_ref, i_ref, g_ref, lb_ref, ng_ref, o_ref, ya_ref, st_ref, state):
        @pl.when(pl.program_id(1) == 0)
        def _():
            state[...] = jnp.zeros_like(state)

        causal = _tri()
        tri = causal.astype(f32)

        def chunk(n, carry):
            rows = pl.ds(pl.multiple_of(n * A_CHUNK, A_CHUNK), A_CHUNK)
            o_parts, y_parts = [], []
            for hh in range(A_HP):
                ks = slice(hh * A_K, (hh + 1) * A_K)
                vs = slice(hh * A_V, (hh + 1) * A_V)
                qf, _, _, _, logf, k = _hgrn_gates(q_ref[rows, ks], f_ref[rows, ks], lb_ref[:, ks])
                vi = i_ref[rows, vs].astype(bf16)
                gg = g_ref[rows, vs]
                b = _dot(tri, logf, NN, precision=lax.Precision.HIGHEST)
                s0 = state[hh]
                st_ref[n, hh] = s0
                o = _dot((qf * jnp.exp(b)).astype(bf16), s0.astype(bf16), NT)
                a, _, _, _, _ = _hgrn_intra(qf, k, b, causal)
                o = o + _dot(a.astype(bf16), vi, NN)
                bend = b[A_CHUNK - 1:A_CHUNK, :]
                ke = (k * jnp.exp(bend - b)).astype(bf16)
                state[hh] = s0 * jnp.exp(bend) + _dot(vi, ke, TN)
                rstd = lax.rsqrt(jnp.mean(o * o, axis=1, keepdims=True) + EPS)
                o_parts.append(o)
                y_parts.append(o * rstd * ng_ref[...] * (gg * _sig(gg)))
            o_ref[rows, :] = jnp.concatenate(o_parts, axis=1)
            ya_ref[rows, :] = jnp.concatenate(y_parts, axis=1).astype(bf16)
            return carry

        lax.fori_loop(0, A_NCH, chunk, 0)

    return _call(
        body, name="hgrn_fwd", grid=(A_HEADS // A_HP, nb),
        in_specs=_hgrn_in_specs(),
        out_specs=[pl.BlockSpec((A_TB, A_VW), lambda p, i: (i, p)), pl.BlockSpec((A_TB, A_VW), lambda p, i: (i, p)),
                   pl.BlockSpec((A_NCH, A_HP, A_V, A_K), lambda p, i: (i, p, 0, 0))],
        out_shape=[jax.ShapeDtypeStruct((S, 384), f32), jax.ShapeDtypeStruct((S, 384), bf16),
                   jax.ShapeDtypeStruct((S // A_CHUNK, A_HEADS, A_V, A_K), f32)],
        scratch_shapes=[pltpu.VMEM((A_HP, A_V, A_K), f32)],
        sem=("parallel", "arbitrary"), args=(u, u, u, u, lb, ng), side=side)


def hgrn_bwd(u, lb, ng, o, st, dya, side=None):
    S = u.shape[0]
    nb = S // A_TB

    def body(q_ref, f_ref, i_ref, g_ref, lb_ref, ng_ref, o_ref, st_ref, dy_ref,
             dq_ref, df_ref, di_ref, dg_ref, dlb_ref, dng_ref, dstate):
        @pl.when(pl.program_id(1) == 0)
        def _():
            dstate[...] = jnp.zeros_like(dstate)
            dlb_ref[...] = jnp.zeros_like(dlb_ref)
            dng_ref[...] = jnp.zeros_like(dng_ref)

        causal = _tri()
        tri = causal.astype(f32)

        def chunk(it, carry):
            n = A_NCH - 1 - it
            rows = pl.ds(pl.multiple_of(n * A_CHUNK, A_CHUNK), A_CHUNK)
            dq_p, df_p, di_p, dg_p, dlb_p = [], [], [], [], []
            dng_acc = jnp.zeros((1, A_V), f32)
            for hh in range(A_HP):
                ks = slice(hh * A_K, (hh + 1) * A_K)
                vs = slice(hh * A_V, (hh + 1) * A_V)
                lbh = lb_ref[:, ks]
                qr = q_ref[rows, ks]
                qf, sq, sig, f, logf, k = _hgrn_gates(qr, f_ref[rows, ks], lbh)
                vi = i_ref[rows, vs].astype(bf16)
                gg = g_ref[rows, vs]
                b = _dot(tri, logf, NN, precision=lax.Precision.HIGHEST)
                eb = jnp.exp(b)
                bend = b[A_CHUNK - 1:A_CHUNK, :]
                eend = jnp.exp(bend)
                ekend = jnp.exp(bend - b)
                qe = (qf * eb).astype(bf16)
                ke = (k * ekend).astype(bf16)
                s0 = st_ref[n, hh]
                dsend = dstate[hh]
                ov = o_ref[rows, vs]
                dy = dy_ref[rows, vs]
                rstd = lax.rsqrt(jnp.mean(ov * ov, axis=1, keepdims=True) + EPS)
                oh = ov * rstd
                sg = _sig(gg)
                d_on = dy * (gg * sg)
                dg_p.append(dy * oh * ng_ref[...] * (sg * (1.0 + gg * (1.0 - sg))))
                dng_acc = dng_acc + jnp.sum(d_on * oh, axis=0, keepdims=True)
                doh = d_on * ng_ref[...]
                do = (rstd * (doh - oh * jnp.mean(doh * oh, axis=1, keepdims=True))).astype(bf16)
                a, qts, kts, eqs, eks = _hgrn_intra(qf, k, b, causal)
                da = jnp.where(causal, _dot(do, vi, NT), 0.0).astype(bf16)
                dsb = dsend.astype(bf16)
                dv = _dot(a.astype(bf16), do, TN) + _dot(ke, dsb, NT)
                dq = _dot(do, s0.astype(bf16), NN) * eb
                dk_state = _dot(vi, dsb, NN) * ekend
                dk = dk_state
                dq_i = []
                for sb in range(A_NSUB):
                    da_sb = da[sb * A_SUB:(sb + 1) * A_SUB, :]
                    dq_i.append(_dot(da_sb, kts[sb], NN) * eqs[sb])
                    dk = dk + _dot(da_sb, qts[sb], TN) * eks[sb]
                dq = dq + jnp.concatenate(dq_i, axis=0)
                db = qf * dq - k * dk
                extra = jnp.sum(k * dk_state, axis=0, keepdims=True) + eend * jnp.sum(s0 * dsend, axis=0, keepdims=True)
                dlogf = _dot(tri, db, TN, precision=lax.Precision.HIGHEST) + extra
                dstate[hh] = _dot(do, qe, TN) + eend * dsend
                d_pre = jnp.where(f > TINY, dlogf / f, 0.0) - dk
                dlb_p.append(jnp.sum((1.0 - sig) * d_pre, axis=0, keepdims=True))
                df_p.append((1.0 - lbh) * d_pre * sig * (1.0 - sig))
                dq_p.append(dq * (sq * (1.0 + qr * (1.0 - sq))))
                di_p.append(dv)
            dq_ref[rows, :] = jnp.concatenate(dq_p, axis=1).astype(bf16)
            df_ref[rows, :] = jnp.concatenate(df_p, axis=1).astype(bf16)
            di_ref[rows, :] = jnp.concatenate(di_p, axis=1).astype(bf16)
            dg_ref[rows, :] = jnp.concatenate(dg_p, axis=1).astype(bf16)
            dlb_ref[...] += jnp.concatenate(dlb_p, axis=1)
            dng_ref[0] += dng_acc
            return carry

        lax.fori_loop(0, A_NCH, chunk, 0)

    rev = lambda p, i: (nb - 1 - i, p)
    return _call(
        body, name="hgrn_bwd", grid=(A_HEADS // A_HP, nb),
        in_specs=_hgrn_in_specs(nb) + [pl.BlockSpec((A_TB, A_VW), rev),
                                       pl.BlockSpec((A_NCH, A_HP, A_V, A_K), lambda p, i: (nb - 1 - i, p, 0, 0)),
                                       pl.BlockSpec((A_TB, A_VW), rev)],
        out_specs=[pl.BlockSpec((A_TB, A_KW), rev), pl.BlockSpec((A_TB, A_KW), rev),
                   pl.BlockSpec((A_TB, A_VW), rev), pl.BlockSpec((A_TB, A_VW), rev),
                   pl.BlockSpec((1, A_KW), lambda p, i: (0, p)), pl.BlockSpec((1, 1, A_V), lambda p, i: (p, 0, 0))],
        out_shape=[jax.ShapeDtypeStruct((S, 768), bf16), jax.ShapeDtypeStruct((S, 768), bf16),
                   jax.ShapeDtypeStruct((S, 384), bf16), jax.ShapeDtypeStruct((S, 384), bf16),
                   jax.ShapeDtypeStruct((1, 768), f32), jax.ShapeDtypeStruct((A_HEADS // A_HP, 1, A_V), f32)],
        scratch_shapes=[pltpu.VMEM((A_HP, A_V, A_K), f32)],
        sem=("parallel", "arbitrary"), args=(u, u, u, u, lb, ng, o, st, dya), side=side)


B_TK = 128
SCALE = HD ** -0.5


def _split(x):
    hi = x.astype(bf16)
    return hi, (x - hi.astype(f32)).astype(bf16)


def _dot2(x, m, dn):
    hi, lo = _split(x)
    return _dot(hi, m, dn) + _dot(lo, m, dn)


def _sb_block(qs, kh, mask, m_gt, c):
    z = _dot(qs, kh, NT)
    sp = jnp.maximum(z, 0.0) + jnp.log(1.0 + jnp.exp(-jnp.abs(z)))
    lneg = jnp.where(mask, -sp, 0.0)
    lsz = z - sp
    suf = _dot2(lneg, m_gt, NN) + c
    a = jnp.where(mask, jnp.exp(lsz + suf), 0.0)
    return lneg, lsz, a


def _sb_masks(tq, i, jj):
    t_idx = i * tq + lax.broadcasted_iota(jnp.int32, (tq, B_TK), 0)
    s_idx = jj * B_TK + lax.broadcasted_iota(jnp.int32, (tq, B_TK), 1)
    return s_idx < t_idx


def _sb_tri(strict):
    r = lax.broadcasted_iota(jnp.int32, (B_TK, B_TK), 0)
    c = lax.broadcasted_iota(jnp.int32, (B_TK, B_TK), 1)
    return (r > c if strict else r >= c).astype(bf16)


B_DEAD = -88.0


def _sb_walk(nkb, step, init):
    def cond(state):
        it, alive, _ = state
        return jnp.logical_and(it < nkb, alive)

    def body(state):
        it, _, carry = state
        carry = step(it, carry)
        top = jnp.max(functools.reduce(jnp.maximum, [h[1] for h in carry]))
        return it + 1, top > B_DEAD, carry

    return lax.while_loop(cond, body, (jnp.int32(0), jnp.bool_(True), init))[2]


B_HP = 6
B_W = B_HP * HD


def sb_fwd(u, side=None):
    S = u.shape[0]
    tq = 128

    def body(q_ref, k_ref, v_ref, o_ref):
        i = pl.program_id(1)
        nkb = (i + 1) * (tq // B_TK)
        m_gt = _sb_tri(True)
        qs = [(q_ref[:, hh * HD:(hh + 1) * HD] * SCALE).astype(bf16) for hh in range(B_HP)]

        def step(it, carry):
            jj = nkb - 1 - it
            rows = pl.ds(pl.multiple_of(jj * B_TK, B_TK), B_TK)
            mask = _sb_masks(tq, i, jj)
            kb, vb = k_ref[rows, :], v_ref[rows, :]
            out = []
            for hh in range(B_HP):
                acc, c = carry[hh]
                kh = kb[:, hh * HD:(hh + 1) * HD].astype(bf16)
                vh = vb[:, hh * HD:(hh + 1) * HD].astype(bf16)
                lneg, _, a = _sb_block(qs[hh], kh, mask, m_gt, c)
                out.append((acc + _dot2(a, vh, NN), c + jnp.sum(lneg, axis=1, keepdims=True)))
            return tuple(out)

        z0 = (jnp.zeros((tq, HD), f32), jnp.zeros((tq, 1), f32))
        res = _sb_walk(nkb, step, (z0,) * B_HP)
        o_ref[...] = jnp.concatenate([r[0] for r in res], axis=1)

    outs, souts = _call(
        body, name="sb_fwd", grid=(B_HEADS // B_HP, S // tq),
        in_specs=[pl.BlockSpec((tq, B_W), lambda p, i: (i, O_BQ // B_W + p)),
                  pl.BlockSpec((S, B_W), lambda p, i: (0, O_BK // B_W + p)),
                  pl.BlockSpec((S, B_W), lambda p, i: (0, O_BV // B_W + p))],
        out_specs=[pl.BlockSpec((tq, B_W), lambda p, i: (i, p))],
        out_shape=[jax.ShapeDtypeStruct((S, 384), f32)],
        sem=("parallel", "arbitrary"), args=(u, u, u), side=side)
    return outs[0], souts


def sb_bwd(u, yb, dyb, side=None):
    S = u.shape[0]
    tq = 128
    nq = S // tq

    def body(q_ref, k_ref, v_ref, y_ref, dy_ref, dq_ref, dk_out, dv_out, dk_ref, dv_ref, out_sem):
        p, i = pl.program_id(0), pl.program_id(1)

        @pl.when(i == 0)
        def _():
            dk_ref[...] = jnp.zeros_like(dk_ref)
            dv_ref[...] = jnp.zeros_like(dv_ref)

        nkb = (i + 1) * (tq // B_TK)
        m_gt = _sb_tri(True)
        m_ge = _sb_tri(False)
        qs, dos, tot = [], [], []
        for hh in range(B_HP):
            hs = slice(hh * HD, (hh + 1) * HD)
            qs.append((q_ref[:, hs] * SCALE).astype(bf16))
            dob = dy_ref[:, hs].astype(bf16)
            dos.append(dob)
            tot.append(jnp.sum(dob.astype(f32) * y_ref[:, hs], axis=1, keepdims=True))

        def step(it, carry):
            jj = nkb - 1 - it
            rows = pl.ds(pl.multiple_of(jj * B_TK, B_TK), B_TK)
            mask = _sb_masks(tq, i, jj)
            kb, vb = k_ref[rows, :], v_ref[rows, :]
            out, dk_p, dv_p = [], [], []
            for hh in range(B_HP):
                dq, c, cg = carry[hh]
                kh = kb[:, hh * HD:(hh + 1) * HD].astype(bf16)
                vh = vb[:, hh * HD:(hh + 1) * HD].astype(bf16)
                lneg, lsz, a = _sb_block(qs[hh], kh, mask, m_gt, c)
                g = a * _dot(dos[hh], vh, NT)
                pre = tot[hh] - cg - _dot2(g, m_ge, NN)
                beta = jnp.exp(lsz)
                dz = jnp.where(mask, g * (1.0 - beta) - beta * pre, 0.0).astype(bf16)
                dk_p.append(_dot(dz, qs[hh], TN))
                dv_p.append(_dot(a.astype(bf16), dos[hh], TN))
                out.append((dq + _dot(dz, kh, NN), c + jnp.sum(lneg, axis=1, keepdims=True),
                            cg + jnp.sum(g, axis=1, keepdims=True)))
            dk_ref[rows, :] += jnp.concatenate(dk_p, axis=1)
            dv_ref[rows, :] += jnp.concatenate(dv_p, axis=1)
            return tuple(out)

        z0 = (jnp.zeros((tq, HD), f32), jnp.zeros((tq, 1), f32), jnp.zeros((tq, 1), f32))
        res = _sb_walk(nkb, step, (z0,) * B_HP)
        dq_ref[...] = jnp.concatenate([r[0] for r in res], axis=1) * SCALE

        @pl.when(i == nq - 1)
        def _():
            cols = pl.ds(pl.multiple_of(p * B_W, LANE), B_W)
            ck = pltpu.make_async_copy(dk_ref, dk_out.at[:, cols], out_sem.at[0])
            cv = pltpu.make_async_copy(dv_ref, dv_out.at[:, cols], out_sem.at[1])
            ck.start()
            cv.start()
            ck.wait()
            cv.wait()

    row = pl.BlockSpec((tq, B_W), lambda p, i: (i, p))
    hbm = pl.BlockSpec(memory_space=pl.ANY)
    osd = jax.ShapeDtypeStruct((S, 384), f32)
    return _call(
        body, name="sb_bwd", grid=(B_HEADS // B_HP, nq),
        in_specs=[pl.BlockSpec((tq, B_W), lambda p, i: (i, O_BQ // B_W + p)),
                  pl.BlockSpec((S, B_W), lambda p, i: (0, O_BK // B_W + p)),
                  pl.BlockSpec((S, B_W), lambda p, i: (0, O_BV // B_W + p)), row, row],
        out_specs=[row, hbm, hbm], out_shape=[osd, osd, osd],
        scratch_shapes=[pltpu.VMEM((S, B_W), f32), pltpu.VMEM((S, B_W), f32), pltpu.SemaphoreType.DMA((2,))],
        sem=("parallel", "arbitrary"), args=(u, u, u, yb, dyb), side=side)


def _dil_rows(i, rho, r):
    if r == 1:
        return pl.ds(pl.multiple_of(i * C_BLK, C_BLK), C_BLK)
    return pl.ds(i * (C_BLK * r) + rho, C_BLK, stride=r)


def _dil_scores(qs, kc, kp, i, slope_r):
    qi = lax.broadcasted_iota(jnp.int32, (C_BLK, C_BLK), 0)
    kj = lax.broadcasted_iota(jnp.int32, (C_BLK, C_BLK), 1)
    d_c = qi - kj
    d_p = d_c + C_BLK
    ok_c = d_c >= 0
    ok_p = jnp.logical_and(d_c <= 0, i > 0)
    s_c = jnp.where(ok_c, _dot(qs, kc, NT) - slope_r * d_c.astype(f32), NEG_BIG)
    s_p = jnp.where(ok_p, _dot(qs, kp, NT) - slope_r * d_p.astype(f32), NEG_BIG)
    return s_c, s_p, ok_c, ok_p


def _dil_slope(g, r, hh):
    pair = pl.program_id(0)
    return jnp.where(pair == 0, C_SLOPES[4 * g + hh] * r, C_SLOPES[4 * g + 2 + hh] * r).astype(f32)


def _dil_u_specs(g, S):
    def im(off):
        return lambda p, rho: (0, (off + g * 256) // LANE + p)
    return [pl.BlockSpec((S, LANE), im(O_CQ)), pl.BlockSpec((S, LANE), im(O_CK)), pl.BlockSpec((S, LANE), im(O_CV))]


def dil_fwd(u, g, side=None):
    S = u.shape[0]
    r = C_GROUPS[g][1]
    nbk = S // r // C_BLK

    def body(q_ref, k_ref, v_ref, o_ref, l_ref):
        rho = pl.program_id(1)

        def step(i, carry):
            rc = _dil_rows(i, rho, r)
            rp = _dil_rows(jnp.maximum(i - 1, 0), rho, r)
            q2, kc2, kp2, vc2, vp2 = q_ref[rc, :], k_ref[rc, :], k_ref[rp, :], v_ref[rc, :], v_ref[rp, :]
            o_p, l_p = [], []
            for hh in range(2):
                hs = slice(hh * HD, (hh + 1) * HD)
                qs = (q2[:, hs] * SCALE).astype(bf16)
                kc, kp = kc2[:, hs].astype(bf16), kp2[:, hs].astype(bf16)
                vc, vp = vc2[:, hs].astype(bf16), vp2[:, hs].astype(bf16)
                s_c, s_p, _, _ = _dil_scores(qs, kc, kp, i, _dil_slope(g, r, hh))
                m = jnp.maximum(jnp.max(s_c, axis=1, keepdims=True), jnp.max(s_p, axis=1, keepdims=True))
                p_c, p_p = jnp.exp(s_c - m), jnp.exp(s_p - m)
                den = jnp.sum(p_c, axis=1, keepdims=True) + jnp.sum(p_p, axis=1, keepdims=True)
                o_p.append((_dot(p_c.astype(bf16), vc, NN) + _dot(p_p.astype(bf16), vp, NN)) / den)
                l_p.append(jnp.broadcast_to(m + jnp.log(den), (C_BLK, HD)))
            o_ref[rc, :] = jnp.concatenate(o_p, axis=1)
            l_ref[rc, :] = jnp.concatenate(l_p, axis=1)
            return carry

        lax.fori_loop(0, nbk, step, 0, unroll=2)

    ospec = pl.BlockSpec((S, LANE), lambda p, rho: (0, p))
    osd = jax.ShapeDtypeStruct((S, 256), f32)
    return _call(
        body, name=f"dil_fwd{g}", grid=(2, r),
        in_specs=_dil_u_specs(g, S), out_specs=[ospec, ospec], out_shape=[osd, osd],
        sem=("parallel", "arbitrary"), args=(u, u, u), side=side)


def dil_merge(os_, ls_):
    S = os_[0].shape[0]

    def body(o0, o1, o2, l0, l1, l2, y_ref, lse_ref):
        a, b, c = l0[...], l1[...], l2[...]
        m = jnp.maximum(jnp.maximum(a, b), c)
        ea, eb, ec = jnp.exp(a - m), jnp.exp(b - m), jnp.exp(c - m)
        den = ea + eb + ec
        y_ref[...] = (ea * o0[...] + eb * o1[...] + ec * o2[...]) / den
        lse_ref[...] = m + jnp.log(den)

    spec = pl.BlockSpec((512, 256), lambda i: (i, 0))
    osd = jax.ShapeDtypeStruct((S, 256), f32)
    return pl.pallas_call(
        body, name="dil_merge", grid=(S // 512,), in_specs=[spec] * 6, out_specs=[spec, spec],
        out_shape=[osd, osd], compiler_params=_cp(("parallel",)),
    )(*os_, *ls_)


def dil_bwd(u, g, dyc, yc, lse):
    S = u.shape[0]
    r = C_GROUPS[g][1]
    nbk = S // r // C_BLK

    def body(q_ref, k_ref, v_ref, dy_ref, y_ref, l_ref, dq_ref, dk_ref, dv_ref):
        rho = pl.program_id(1)

        @pl.when(rho == 0)
        def _():
            dk_ref[...] = jnp.zeros_like(dk_ref)
            dv_ref[...] = jnp.zeros_like(dv_ref)

        def step(i, carry):
            rc = _dil_rows(i, rho, r)
            rp = _dil_rows(jnp.maximum(i - 1, 0), rho, r)
            q2, kc2, kp2, vc2, vp2 = q_ref[rc, :], k_ref[rc, :], k_ref[rp, :], v_ref[rc, :], v_ref[rp, :]
            dy2, y2, l2 = dy_ref[rc, :], y_ref[rc, :], l_ref[rc, :]
            dq_p, dkc_p, dkp_p, dvc_p, dvp_p = [], [], [], [], []
            for hh in range(2):
                hs = slice(hh * HD, (hh + 1) * HD)
                qs = (q2[:, hs] * SCALE).astype(bf16)
                kc, kp = kc2[:, hs].astype(bf16), kp2[:, hs].astype(bf16)
                vc, vp = vc2[:, hs].astype(bf16), vp2[:, hs].astype(bf16)
                dy = dy2[:, hs]
                dyb = dy.astype(bf16)
                s_c, s_p, ok_c, ok_p = _dil_scores(qs, kc, kp, i, _dil_slope(g, r, hh))
                lrow = l2[:, hh * HD:hh * HD + 1]
                delta = jnp.sum(dy * y2[:, hs], axis=1, keepdims=True)
                pi_c = jnp.where(ok_c, jnp.exp(s_c - lrow), 0.0)
                pi_p = jnp.where(ok_p, jnp.exp(s_p - lrow), 0.0)
                ds_c = (pi_c * (_dot(dyb, vc, NT) - delta)).astype(bf16)
                ds_p = (pi_p * (_dot(dyb, vp, NT) - delta)).astype(bf16)
                dq_p.append((_dot(ds_c, kc, NN) + _dot(ds_p, kp, NN)) * SCALE)
                dkc_p.append(_dot(ds_c, qs, TN))
                dkp_p.append(_dot(ds_p, qs, TN))
                dvc_p.append(_dot(pi_c.astype(bf16), dyb, TN))
                dvp_p.append(_dot(pi_p.astype(bf16), dyb, TN))
            dq_ref[rc, :] = jnp.concatenate(dq_p, axis=1)
            dk_ref[rc, :] += jnp.concatenate(dkc_p, axis=1)
            dv_ref[rc, :] += jnp.concatenate(dvc_p, axis=1)
            dk_ref[rp, :] += jnp.concatenate(dkp_p, axis=1)
            dv_ref[rp, :] += jnp.concatenate(dvp_p, axis=1)
            return carry

        lax.fori_loop(0, nbk, step, 0, unroll=2)

    ospec = pl.BlockSpec((S, LANE), lambda p, rho: (0, p))
    osd = jax.ShapeDtypeStruct((S, 256), f32)
    return pl.pallas_call(
        body, name=f"dil_bwd{g}", grid=(2, r),
        in_specs=_dil_u_specs(g, S) + [ospec, ospec, ospec], out_specs=[ospec] * 3, out_shape=[osd] * 3,
        compiler_params=_cp(("parallel", "arbitrary")),
    )(u, u, u, dyc, yc, lse)


def _rows_tile(rows):
    return _tile(rows, (256, 176, 128, 64, 32, 16, 8))


def cast_bf16(w):
    shape = w.shape
    w2 = w.reshape(-1, shape[-1])
    rows, cols = w2.shape
    tr = _rows_tile(rows)

    def body(x_ref, o_ref):
        o_ref[...] = x_ref[...].astype(bf16)

    spec = pl.BlockSpec((tr, cols), lambda i: (i, 0))
    out = pl.pallas_call(
        body, name="cast_bf16", grid=(rows // tr,), in_specs=[spec], out_specs=spec,
        out_shape=jax.ShapeDtypeStruct((rows, cols), bf16), compiler_params=_cp(("parallel",)),
    )(w2)
    return out.reshape(shape)


BC1 = 1.0 - ADAM_B1 ** ADAM_STEP
BC2 = 1.0 - ADAM_B2 ** ADAM_STEP


def _adam_math(w, g, m, v):
    m2 = ADAM_B1 * m + (1.0 - ADAM_B1) * g
    v2 = ADAM_B2 * v + (1.0 - ADAM_B2) * (g * g)
    delta = -ADAM_LR * ((m2 / BC1) / (jnp.sqrt(v2 / BC2) + ADAM_EPS) + ADAM_WD * w)
    return delta, m2, v2


def adam(w, g, m, v):
    shape = w.shape
    r2 = lambda t: t.reshape(-1, shape[-1])
    rows, cols = r2(w).shape
    tr = _rows_tile(rows)

    def body(w_ref, g_ref, m_ref, v_ref, d_ref, m2_ref, v2_ref):
        d_ref[...], m2_ref[...], v2_ref[...] = _adam_math(w_ref[...], g_ref[...], m_ref[...], v_ref[...])

    spec = pl.BlockSpec((tr, cols), lambda i: (i, 0))
    osd = jax.ShapeDtypeStruct((rows, cols), f32)
    outs = pl.pallas_call(
        body, name="adam", grid=(rows // tr,), in_specs=[spec] * 4, out_specs=[spec] * 3, out_shape=[osd] * 3,
        compiler_params=_cp(("parallel",)),
    )(r2(w), r2(g), r2(m), r2(v))
    return [o.reshape(shape) for o in outs]


def adam_layers(w, gs, m, v):
    depth, rows, cols = w.shape
    tr = _tile(rows, (128, 64, 32, 16, 8))
    nb = rows // tr

    def body(w_ref, m_ref, v_ref, *rest):
        g_refs, (g_out, d_ref, m2_ref, v2_ref) = rest[:depth], rest[depth:]
        for k in range(depth):
            @pl.when(pl.program_id(0) == k)
            def _(k=k):
                g = g_refs[k][...]
                g_out[0] = g
                d_ref[0], m2_ref[0], v2_ref[0] = _adam_math(w_ref[0], g, m_ref[0], v_ref[0])

    def g_spec(k):
        return pl.BlockSpec((tr, cols), lambda l, i: (jnp.where(l < k, 0, jnp.where(l > k, nb - 1, i)), 0))

    wspec = pl.BlockSpec((1, tr, cols), lambda l, i: (l, i, 0))
    osd = jax.ShapeDtypeStruct(w.shape, f32)
    return pl.pallas_call(
        body, name="adam_layers", grid=(depth, nb), in_specs=[wspec] * 3 + [g_spec(k) for k in range(depth)],
        out_specs=[wspec] * 4, out_shape=[osd] * 4, compiler_params=_cp(("arbitrary", "arbitrary")),
    )(w, m, v, *gs)


ADA_N = 9 * D // 4
ADA_TN = 384


def ada_fwd(c_all, w_ada):
    def body(c_ref, w_ref, o_ref):
        cv = c_ref[...]
        o_ref[0] = _dot((cv * _sig(cv)).astype(bf16), w_ref[0].astype(bf16), NN)

    return pl.pallas_call(
        body, name="ada_fwd", grid=(DEPTH, ADA_N // ADA_TN),
        in_specs=[pl.BlockSpec((8, D), lambda l, j: (0, 0)), pl.BlockSpec((1, D, ADA_TN), lambda l, j: (l, 0, j))],
        out_specs=pl.BlockSpec((1, 8, ADA_TN), lambda l, j: (l, 0, j)),
        out_shape=jax.ShapeDtypeStruct((DEPTH, 8, ADA_N), f32), compiler_params=_cp(("parallel", "parallel")),
    )(c_all, w_ada)


def ada_bwd_adam(c_all, dm, w, m, v, side=None):
    tr = 128

    def body(c_ref, dm_ref, w_ref, m_ref, v_ref, g_ref, d_ref, m2_ref, v2_ref):
        cv = c_ref[...]
        g = _dot((cv * _sig(cv)).astype(bf16), dm_ref[0].astype(bf16), TN)
        g_ref[0] = g
        d_ref[0], m2_ref[0], v2_ref[0] = _adam_math(w_ref[0], g, m_ref[0], v_ref[0])

    wspec = pl.BlockSpec((1, tr, ADA_N), lambda l, i: (l, i, 0))
    osd = jax.ShapeDtypeStruct((DEPTH, D, ADA_N), f32)
    return _call(
        body, name="ada_bwd_adam", grid=(DEPTH, D // tr),
        in_specs=[pl.BlockSpec((8, tr), lambda l, i: (0, i)), pl.BlockSpec((1, 8, ADA_N), lambda l, i: (l, 0, 0)),
                  wspec, wspec, wspec],
        out_specs=[wspec] * 4, out_shape=[osd] * 4, sem=("parallel", "parallel"), args=(c_all, dm, w, m, v), side=side)


def _lb_probs(x):
    mx = jnp.max(x, axis=0, keepdims=True)
    e = jnp.exp(x - mx)
    return e / jnp.sum(e, axis=0, keepdims=True)


def lb_fwd(logits):
    def body(x_ref, o_ref):
        p = _lb_probs(x_ref[...])
        rows = [jnp.zeros((1, 768), f32)]
        for l in range(1, DEPTH):
            rows.append(rows[-1] + p[l:l + 1, :])
        o_ref[...] = jnp.concatenate(rows, axis=0)

    return pl.pallas_call(body, name="lb_fwd", out_shape=jax.ShapeDtypeStruct((DEPTH, 768), f32))(logits)


def lb_bwd(logits, dlb):
    def body(x_ref, d_ref, o_ref):
        p = _lb_probs(x_ref[...])
        d = d_ref[...]
        rows = [jnp.zeros((1, 768), f32)] * DEPTH
        acc = jnp.zeros((1, 768), f32)
        for l in range(DEPTH - 1, 0, -1):
            acc = acc + d[l:l + 1, :]
            rows[l] = acc
        dp = jnp.concatenate(rows, axis=0)
        o_ref[...] = p * (dp - jnp.sum(p * dp, axis=0, keepdims=True))

    return pl.pallas_call(body, name="lb_bwd", out_shape=jax.ShapeDtypeStruct((DEPTH, 768), f32))(logits, dlb)


def sum_slots(x):
    n, rows, cols = x.shape
    tr = _rows_tile(rows)

    def body(x_ref, o_ref):
        acc = x_ref[0]
        for j in range(1, n):
            acc = acc + x_ref[j]
        o_ref[...] = acc

    return pl.pallas_call(
        body, name="sum_slots", grid=(rows // tr,),
        in_specs=[pl.BlockSpec((n, tr, cols), lambda i: (0, i, 0))], out_specs=pl.BlockSpec((tr, cols), lambda i: (i, 0)),
        out_shape=jax.ShapeDtypeStruct((rows, cols), f32), compiler_params=_cp(("parallel",)),
    )(x)


ANY = pl.BlockSpec(memory_space=pl.ANY)
CHIP_FLIPS = ((1, 0), (0, 1), (1, 1))
DEV_FLIPS = tuple((a, b, d) for a in (0, 1) for b in (0, 1) for d in (0, 1))[1:]


def _me():
    return lax.axis_index("x"), lax.axis_index("y"), lax.axis_index("c")


def _flip(v, f):
    return 1 - v if f else v


def _comm_call(body, name, ins, out_shapes, n_remote, n_local):
    return pl.pallas_call(
        body, name=name, in_specs=[ANY] * len(ins), out_specs=[ANY] * len(out_shapes), out_shape=out_shapes,
        scratch_shapes=[pltpu.SemaphoreType.DMA((n_remote,)), pltpu.SemaphoreType.DMA((n_remote,)),
                        pltpu.SemaphoreType.DMA((max(n_local, 1),))],
    )(*ins)


def run_plan(plan, name):
    ni, no = len(plan.ins), len(plan.outs)

    def body(*refs):
        ins, outs, sems = refs[:ni], refs[ni:ni + no], refs[ni + no:]
        plan.start(ins, outs, *sems)
        plan.wait(ins, outs, *sems)

    return pl.pallas_call(body, name=name, in_specs=[ANY] * ni, out_specs=[ANY] * no, out_shape=list(plan.outs),
                          scratch_shapes=plan.sems())(*plan.ins)


def gather_chips_plan(arrs, layer=None):
    n = len(arrs)
    layers = list(layer) if isinstance(layer, (list, tuple)) else [layer] * n
    shapes = [a.shape if l is None else a.shape[1:] for a, l in zip(arrs, layers)]

    def copies(ins, outs, send, recv, loc):
        x, y, c = _me()
        mine = 2 * x + y
        srcs = [r if l is None else r.at[l] for r, l in zip(ins, layers)]
        locs = [pltpu.make_async_copy(srcs[a], outs[a].at[mine], loc.at[a]) for a in range(n)]

        def remote(a, k, slot):
            fx, fy = CHIP_FLIPS[k]
            return pltpu.make_async_remote_copy(srcs[a], outs[a].at[slot], send.at[3 * a + k], recv.at[3 * a + k],
                                                device_id=(_flip(x, fx), _flip(y, fy), c), device_id_type=MESH)

        peers = [2 * _flip(x, fx) + _flip(y, fy) for fx, fy in CHIP_FLIPS]
        return locs, remote, mine, peers

    def start(ins, outs, send, recv, loc):
        locs, remote, mine, _ = copies(ins, outs, send, recv, loc)
        for cp in locs:
            cp.start()
        for a in range(n):
            for k in range(3):
                remote(a, k, mine).start()

    def wait(ins, outs, send, recv, loc):
        locs, remote, _, peers = copies(ins, outs, send, recv, loc)
        for a in range(n):
            for k in range(3):
                cp = remote(a, k, peers[k])
                cp.wait_recv()
                cp.wait_send()
        for cp in locs:
            cp.wait()

    outs = [jax.ShapeDtypeStruct((4,) + tuple(s), a.dtype) for s, a in zip(shapes, arrs)]
    return Plan(list(arrs), outs, 3 * n, n, start, wait)


def all_gather_chips(arrs, layer=None, name="ag4"):
    return run_plan(gather_chips_plan(arrs, layer), name)


def all_gather_devs(arr, name="ag8"):
    def body(in_ref, out_ref, send, recv, loc):
        x, y, c = _me()
        mine = 4 * x + 2 * y + c
        lc = pltpu.make_async_copy(in_ref, out_ref.at[mine], loc.at[0])
        lc.start()

        def remote(k, slot):
            fx, fy, fc = DEV_FLIPS[k]
            return pltpu.make_async_remote_copy(in_ref, out_ref.at[slot], send.at[k], recv.at[k],
                                                device_id=(_flip(x, fx), _flip(y, fy), _flip(c, fc)), device_id_type=MESH)

        for k in range(7):
            remote(k, mine).start()
        for k, (fx, fy, fc) in enumerate(DEV_FLIPS):
            cp = remote(k, 4 * _flip(x, fx) + 2 * _flip(y, fy) + _flip(c, fc))
            cp.wait_recv()
            cp.wait_send()
        lc.wait()

    return _comm_call(body, name, [arr], [jax.ShapeDtypeStruct((8,) + arr.shape, arr.dtype)], 7, 1)[0]


def _rows_of(which, rows):
    return pl.ds(pl.multiple_of(which * rows, 16), rows)


def dev_exchange_plan(parts):
    n = len(parts)

    def copies(ins, outs, send, recv, loc):
        x, y, c = _me()
        mine = 4 * x + 2 * y + c

        def piece(a, px, py, pc):
            rows = ins[a].shape[1] // 2
            return ins[a].at[2 * px + py, _rows_of(pc, rows), :]

        locs = [pltpu.make_async_copy(piece(a, x, y, c), outs[a].at[mine], loc.at[a]) for a in range(n)]

        def remote(a, k, slot):
            fx, fy, fc = DEV_FLIPS[k]
            px, py, pc = _flip(x, fx), _flip(y, fy), _flip(c, fc)
            return pltpu.make_async_remote_copy(piece(a, px, py, pc), outs[a].at[slot], send.at[7 * a + k], recv.at[7 * a + k],
                                                device_id=(px, py, pc), device_id_type=MESH)

        peers = [4 * _flip(x, fx) + 2 * _flip(y, fy) + _flip(c, fc) for fx, fy, fc in DEV_FLIPS]
        return locs, remote, mine, peers

    def start(ins, outs, send, recv, loc):
        locs, remote, mine, _ = copies(ins, outs, send, recv, loc)
        for cp in locs:
            cp.start()
        for a in range(n):
            for k in range(7):
                remote(a, k, mine).start()

    def wait(ins, outs, send, recv, loc):
        locs, remote, _, peers = copies(ins, outs, send, recv, loc)
        for a in range(n):
            for k in range(7):
                cp = remote(a, k, peers[k])
                cp.wait_recv()
                cp.wait_send()
        for cp in locs:
            cp.wait()

    outs = [jax.ShapeDtypeStruct((8, p.shape[1] // 2, p.shape[2]), p.dtype) for p in parts]
    return Plan(list(parts), outs, 7 * n, n, start, wait)


def sum_share(slots, name="rs_sum"):
    n, r, cols = slots.shape
    tr = _tile(r, (128, 176, 64))
    steps = r // tr

    def body(s_ref, g_ref, buf, send, loc, recv):
        i = pl.program_id(0)
        x, y, c = _me()
        slot = i % 2

        def copies(step, sl):
            rows = pl.ds(pl.multiple_of(c * r + step * tr, 8), tr)
            rem = pltpu.make_async_remote_copy(buf.at[sl], g_ref.at[rows, :], send.at[sl], recv.at[0],
                                               device_id=(x, y, 1 - c), device_id_type=MESH)
            return rem, pltpu.make_async_copy(buf.at[sl], g_ref.at[rows, :], loc.at[sl])

        @pl.when(i >= 2)
        def _():
            rem, lc = copies(i - 2, slot)
            rem.wait_send()
            lc.wait()

        acc = s_ref[0].astype(f32)
        for j in range(1, n):
            acc = acc + s_ref[j].astype(f32)
        buf[slot] = acc
        rem, lc = copies(i, slot)
        rem.start()
        lc.start()

        @pl.when(i == steps - 1)
        def _():
            for back in range(min(2, steps)):
                rem, lc = copies(i - back, (i - back) % 2)
                rem.wait_send()
                lc.wait()
            other = g_ref.at[pl.ds(pl.multiple_of((1 - c) * r, 8), r), :]
            pltpu.make_async_remote_copy(other, other, send.at[0], recv.at[0],
                                         device_id=(x, y, 1 - c), device_id_type=MESH).wait_recv()

    return pl.pallas_call(
        body, name=name, grid=(steps,),
        in_specs=[pl.BlockSpec((n, tr, cols), lambda i: (0, i, 0))], out_specs=ANY,
        out_shape=jax.ShapeDtypeStruct((2 * r, cols), f32),
        scratch_shapes=[pltpu.VMEM((2, tr, cols), f32), pltpu.SemaphoreType.DMA((2,)), pltpu.SemaphoreType.DMA((2,)),
                        pltpu.SemaphoreType.DMA((1,))],
        compiler_params=_cp(("arbitrary",)),
    )(slots)


BIG = ("ffn1_w_in", "ffn1_w_out", "w_in", "w_branch_a", "w_branch_b", "w_branch_c", "w_out", "ffn2_w_in", "ffn2_w_out")
ROW_SHARDED = ("ffn1_w_out", "w_out", "ffn2_w_out")
RES_W = (0.5, 1.0, 0.5)


def _full_weight(name, g):
    if name in ROW_SHARDED:
        return g.reshape(4 * g.shape[1], g.shape[2])
    return jnp.concatenate([g[0], g[1], g[2], g[3]], axis=1)


def _by_shard(name, dw):
    if name in ROW_SHARDED:
        return dw.reshape(4, dw.shape[0] // 4, dw.shape[1])
    return dw.reshape(dw.shape[0], 4, dw.shape[1] // 4).transpose(1, 0, 2)


def _full_weight_t(name, g):
    if name in ROW_SHARDED:
        return g.reshape(4 * g.shape[1], g.shape[2]).T
    return g.transpose(0, 2, 1).reshape(4 * g.shape[2], g.shape[1])


def _ffn_fwd(x, w_in, w_out, a_vec, sh_vec, b_vec, plans=(None, None)):
    h, h_t = prenorm(x, a_vec, sh_vec)
    (ua, ub, s, s_t), side0 = ffn_in_swiglu(h, w_in, side=plans[0])
    y = mm(s, w_out, name="ffn_out", side=plans[1])
    side1 = None
    if plans[1] is not None:
        y, side1 = y
    return postnorm(x, y, b_vec), (x, h_t, ua, ub, s_t, y), (side0, side1)


def _ffn_bwd(dout, saved, w_in_t, w_out_t, a_vec, b_vec, plans=None):
    x, h_t, ua, ub, s_t, y = saved
    riders = plans or (None, None, None)
    dy, db = post_bwd(dout, y, b_vec)
    dw_out = mm(s_t, dy, out_dtype=bf16, name="ffn_dwo", side=riders[0])
    du = ffn_du(dy, w_out_t, ua, ub)
    dh = mm(du, w_in_t, name="ffn_dh", side=riders[1])
    sides = []
    if plans:
        (dw_out, s0), (dh, s1) = dw_out, dh
        sides = [s0, s1]
    dw_in = mm(h_t, du, out_dtype=bf16, name="ffn_dwi", side=riders[2])
    if plans:
        dw_in, s2 = dw_in
        sides.append(s2)
    dx, dsh, da = pre_bwd(dout, dh, x, a_vec)
    return (dx, dw_in, dw_out, dsh, da, db) + ((sides,) if plans else ())


def _mix_fwd(x, w, lb, ng, a_vec, sh_vec, b_vec, plans=(None,) * 6):
    h, h_t = prenorm(x, a_vec, sh_vec)
    u = mm(h, w["w_in"], name="mix_in", side=plans[0])
    side0 = None
    if plans[0] is not None:
        u, side0 = u
    (o, ya, st), side1 = hgrn_fwd(u, lb, ng, side=plans[1])
    yb, side2 = sb_fwd(u, side=plans[2])
    groups, dil_sides = [], []
    for g in range(3):
        og, sg = dil_fwd(u, g, side=plans[3 + g])
        groups.append(og)
        dil_sides.append(sg)
    yc, lse = dil_merge([o_ for o_, _ in groups], [l_ for _, l_ in groups])
    pa = mm(ya, w["w_branch_a"], name="mix_pa")
    pb = mm(yb, w["w_branch_b"], name="mix_pb")
    pc = mm(yc, w["w_branch_c"], name="mix_pc")
    merged, merged_t = gate_merge(u, pa, pb, pc)
    z = mm(merged, w["w_out"], name="mix_out")
    return (postnorm(x, z, b_vec), (x, h_t, u, o, ya, st, yb, yc, lse, pa, pb, pc, merged_t, z),
            (side0, side1, side2, *dil_sides))


def _mix_bwd(dout, saved, wt, lb, ng, a_vec, b_vec, plans=(None,) * 3):
    x, h_t, u, o, ya, st, yb, yc, lse, pa, pb, pc, merged_t, z = saved
    dz, db = post_bwd(dout, z, b_vec)
    dmerged = mm(dz, wt["w_out"], name="mix_dm")
    dw_out = mm(merged_t, dz, out_dtype=bf16, name="mix_dwo")
    dpa, dpb, dpc, dg0, dg1, dg2 = gate_bwd(dmerged, u, pa, pb, pc)
    dya = mm(dpa, wt["w_branch_a"], name="mix_dya")
    dyb = mm(dpb, wt["w_branch_b"], name="mix_dyb")
    dyc = mm(dpc, wt["w_branch_c"], name="mix_dyc")
    dw_a = mm(ya.T, dpa, out_dtype=bf16, name="mix_dwa")
    dw_b = mm(yb.astype(bf16).T, dpb, out_dtype=bf16, name="mix_dwb")
    dw_c = mm(yc.astype(bf16).T, dpc, out_dtype=bf16, name="mix_dwc")
    (daq, daf, dai, dag, dlb, dng), side0 = hgrn_bwd(u, lb, ng, o, st, dya, side=plans[0])
    (dbq, dbk, dbv), side1 = sb_bwd(u, yb, dyb, side=plans[1])
    dc = [dil_bwd(u, g, dyc, yc, lse) for g in range(3)]
    du = jnp.concatenate(
        [daq, daf, dai, dag] + [t.astype(bf16) for t in (dbq, dbk, dbv)]
        + [dc[g][j].astype(bf16) for j in range(3) for g in range(3)] + [dg0, dg1, dg2], axis=1)
    dh = mm(du, wt["w_in"], name="mix_dh")
    dw_in = mm(h_t, du, out_dtype=bf16, name="mix_dwi", side=plans[2])
    side2 = None
    if plans[2] is not None:
        dw_in, side2 = dw_in
    dx, dsh, da = pre_bwd(dout, dh, x, a_vec)
    grads = {"w_in": dw_in, "w_out": dw_out, "w_branch_a": dw_a, "w_branch_b": dw_b, "w_branch_c": dw_c}
    return dx, grads, dlb, jnp.sum(dng, axis=0), dsh, da, db, (side0, side1, side2)


NEXT_RIDERS = (("ffn1_w_out",), (), ("ffn2_w_in",), ("ffn1_w_in",), ("w_in",), (), (),
               ("w_out", "w_branch_a", "w_branch_b", "w_branch_c"), ("ffn2_w_out",), ())
FIRST = ("ffn1_w_in", "ffn1_w_out", "w_in", "w_out", "w_branch_a", "w_branch_b", "w_branch_c")
LATE_RIDERS = ((), (), (), (), (), ("ffn2_w_in",), ("ffn2_w_out",), (), (), ())
BWD_RIDERS = (("ffn1_w_in", "ffn2_w_out", "w_branch_a", "w_branch_b", "w_branch_c"), ("w_in", "ffn1_w_out", "w_out"),
              ("ffn2_w_in",))
LAST_RIDERS = (("ffn2_w_out", "w_out", "w_branch_a", "w_branch_b", "w_branch_c"), ("ffn2_w_in",), ("w_in",))
TAIL = ("ffn1_w_in", "ffn1_w_out")


def kernel(x, c, w_ada, b_ada, norm_g, ffn1_w_in, ffn1_w_out, w_in, hgrn_lb_logits, hgrn_norm_g, w_branch_a, w_branch_b, w_branch_c, w_out, ffn2_w_in, ffn2_w_out, loss_target, m_w_ada, m_b_ada, m_norm_g, m_ffn1_w_in, m_ffn1_w_out, m_w_in, m_hgrn_lb_logits, m_hgrn_norm_g, m_w_branch_a, m_w_branch_b, m_w_branch_c, m_w_out, m_ffn2_w_in, m_ffn2_w_out, v_w_ada, v_b_ada, v_norm_g, v_ffn1_w_in, v_ffn1_w_out, v_w_in, v_hgrn_lb_logits, v_hgrn_norm_g, v_w_branch_a, v_w_branch_b, v_w_branch_c, v_w_out, v_ffn2_w_in, v_ffn2_w_out):
    weights = dict(w_ada=w_ada, b_ada=b_ada, norm_g=norm_g, ffn1_w_in=ffn1_w_in, ffn1_w_out=ffn1_w_out, w_in=w_in,
                   hgrn_lb_logits=hgrn_lb_logits, hgrn_norm_g=hgrn_norm_g, w_branch_a=w_branch_a, w_branch_b=w_branch_b,
                   w_branch_c=w_branch_c, w_out=w_out, ffn2_w_in=ffn2_w_in, ffn2_w_out=ffn2_w_out)
    mom = dict(w_ada=m_w_ada, b_ada=m_b_ada, norm_g=m_norm_g, ffn1_w_in=m_ffn1_w_in, ffn1_w_out=m_ffn1_w_out, w_in=m_w_in,
               hgrn_lb_logits=m_hgrn_lb_logits, hgrn_norm_g=m_hgrn_norm_g, w_branch_a=m_w_branch_a, w_branch_b=m_w_branch_b,
               w_branch_c=m_w_branch_c, w_out=m_w_out, ffn2_w_in=m_ffn2_w_in, ffn2_w_out=m_ffn2_w_out)
    var = dict(w_ada=v_w_ada, b_ada=v_b_ada, norm_g=v_norm_g, ffn1_w_in=v_ffn1_w_in, ffn1_w_out=v_ffn1_w_out, w_in=v_w_in,
               hgrn_lb_logits=v_hgrn_lb_logits, hgrn_norm_g=v_hgrn_norm_g, w_branch_a=v_w_branch_a, w_branch_b=v_w_branch_b,
               w_branch_c=v_w_branch_c, w_out=v_w_out, ffn2_w_in=v_ffn2_w_in, ffn2_w_out=v_ffn2_w_out)
    order = list(weights)
    xi, yi, ci = _me()
    chip = 2 * xi + yi
    dev = 4 * xi + 2 * yi + ci
    xs = x[0]

    c_all = all_gather_devs(c, name="ag8_c").reshape(8, D)
    mod_sh = all_gather_chips([ada_fwd(c_all, w_ada)], name="ag4_mod")[0]
    mod_all = mod_sh.transpose(1, 2, 0, 3).reshape(DEPTH, 8, 9 * D)
    mod = lax.dynamic_index_in_dim(mod_all, dev, axis=1, keepdims=False) + b_ada
    mod = mod.reshape(DEPTH, 3, 3, D)
    ng_all = all_gather_chips([norm_g.reshape(DEPTH * 6, D // 4)], name="ag4_norm")[0]
    ng_all = ng_all.reshape(4, DEPTH, 6, D // 4).transpose(1, 2, 0, 3).reshape(DEPTH, 6, D)
    lb_all = lb_fwd(hgrn_lb_logits)
    w16 = {n: cast_bf16(weights[n]) for n in BIG}

    def vecs(l, i):
        shift, scale, gate = mod[l, i, 0][None], mod[l, i, 1][None], mod[l, i, 2][None]
        g_pre, g_post = ng_all[l, 2 * i][None], ng_all[l, 2 * i + 1][None]
        return g_pre * (1.0 + scale), shift, RES_W[i] * gate * g_post

    saved, full = [], []
    gathered = dict(zip(FIRST, all_gather_chips([w16[n] for n in FIRST], layer=0, name="ag4_w0")))
    for l in range(DEPTH):
        riders = [[(n, l) for n in (LATE_RIDERS[h] if l == 0 else ())]
                  + [(n, l + 1) for n in (NEXT_RIDERS[h] if l + 1 < DEPTH else ())] for h in range(len(NEXT_RIDERS))]
        plans = [gather_chips_plan([w16[n] for n, _ in it], layer=[ll for _, ll in it]) if it else None for it in riders]
        coming = {}

        def absorb(hosts, sides):
            for h, outs in zip(hosts, sides):
                for (n, ll), g in zip(riders[h], outs or ()):
                    (gathered if ll == l else coming)[n] = g

        full_w = lambda n: _full_weight(n, gathered[n])
        lb, ng = lb_all[l][None], hgrn_norm_g[l][None]
        xs, s1, sides = _ffn_fwd(xs, full_w("ffn1_w_in"), full_w("ffn1_w_out"), *vecs(l, 0), plans=plans[0:2])
        absorb((0, 1), sides)
        w = {n: full_w(n) for n in ("w_in", "w_branch_a", "w_branch_b", "w_branch_c", "w_out")}
        xs, s2, sides = _mix_fwd(xs, w, lb, ng, *vecs(l, 1), plans=plans[2:8])
        absorb(range(2, 8), sides)
        xs, s3, sides = _ffn_fwd(xs, full_w("ffn2_w_in"), full_w("ffn2_w_out"), *vecs(l, 2), plans=plans[8:10])
        absorb((8, 9), sides)
        saved.append((s1, s2, s3))
        full.append({n: _full_weight_t(n, gathered[n]) for n in BIG})
        gathered = coming

    dx, loss_part = loss_grad(xs, loss_target[0])
    loss = lax.psum(loss_part[0, 0], ("x", "y", "c"))

    big_grads = {n: [None] * DEPTH for n in BIG}
    d_mod, d_ng, d_lb, d_hng = [None] * DEPTH, [None] * DEPTH, [None] * DEPTH, [None] * DEPTH
    pending = None
    for l in reversed(range(DEPTH)):
        wt = full[l]
        s1, s2, s3 = saved[l]
        lb, ng = lb_all[l][None], hgrn_norm_g[l][None]
        rows_mod, rows_ng = [None] * 9, [None] * 6

        def vec_grads(i, dsh, da, db):
            scale, gate = mod[l, i, 1][None], mod[l, i, 2][None]
            g_pre, g_post = ng_all[l, 2 * i][None], ng_all[l, 2 * i + 1][None]
            rows_mod[3 * i], rows_mod[3 * i + 1], rows_mod[3 * i + 2] = dsh, g_pre * da, RES_W[i] * g_post * db
            rows_ng[2 * i], rows_ng[2 * i + 1] = (1.0 + scale) * da, RES_W[i] * gate * db

        a3, _, b3 = vecs(l, 2)
        dx, dwi, dwo, dsh, da, db = _ffn_bwd(dx, s3, wt["ffn2_w_in"], wt["ffn2_w_out"], a3, b3)
        vec_grads(2, dsh, da, db)
        grads = {"ffn2_w_in": dwi, "ffn2_w_out": dwo}
        a2, _, b2 = vecs(l, 1)
        plans = (None,) * len(BWD_RIDERS)
        if pending is not None:
            plans = tuple(dev_exchange_plan([pending[n] for n in names]) for names in BWD_RIDERS)
        dx, gmix, dlb, dhng, dsh, da, db, sides = _mix_bwd(dx, s2, wt, lb, ng, a2, b2, plans=plans)
        if pending is not None:
            for names, outs in zip(BWD_RIDERS, sides):
                for n, slots in zip(names, outs):
                    big_grads[n][l + 1] = sum_share(slots)
        vec_grads(1, dsh, da, db)
        grads.update(gmix)
        a1, _, b1 = vecs(l, 0)
        if l > 0:
            dx, dwi, dwo, dsh, da, db = _ffn_bwd(dx, s1, wt["ffn1_w_in"], wt["ffn1_w_out"], a1, b1)
        else:
            ready = {n: _by_shard(n, grads[n]) for names in LAST_RIDERS for n in names}
            plans = tuple(dev_exchange_plan([ready[n] for n in names]) for names in LAST_RIDERS)
            dx, dwi, dwo, dsh, da, db, sides = _ffn_bwd(dx, s1, wt["ffn1_w_in"], wt["ffn1_w_out"], a1, b1, plans=plans)
            for names, outs in zip(LAST_RIDERS, sides):
                for n, slots in zip(names, outs):
                    big_grads[n][0] = sum_share(slots)
        vec_grads(0, dsh, da, db)
        grads.update({"ffn1_w_in": dwi, "ffn1_w_out": dwo})
        pending = {n: _by_shard(n, grads[n]) for n in (BIG if l > 0 else TAIL)}
        d_mod[l] = jnp.concatenate(rows_mod, axis=1)
        d_ng[l] = jnp.concatenate(rows_ng, axis=0)
        d_lb[l], d_hng[l] = dlb, dhng

    n_small = 6 * D * DEPTH + 768 * DEPTH + A_V * DEPTH + 9 * D * DEPTH
    pad = -n_small % (512 * LANE)
    flat = jnp.concatenate([jnp.stack(d_ng).reshape(-1), jnp.concatenate(d_lb, axis=0).reshape(-1),
                            jnp.concatenate(d_hng, axis=0).reshape(-1), jnp.concatenate(d_mod, axis=0).reshape(-1),
                            jnp.zeros((pad,), f32)])
    small_all = all_gather_devs(flat.reshape(-1, LANE), name="ag8_small")
    total = sum_slots(small_all).reshape(-1)
    o1 = 6 * D * DEPTH
    o2 = o1 + 768 * DEPTH
    o3 = o2 + A_V * DEPTH
    g_ng_full = total[:o1].reshape(DEPTH, 6, D)
    g_lb_all = total[o1:o2].reshape(DEPTH, 768)
    g_small = {
        "norm_g": lax.dynamic_slice_in_dim(g_ng_full, chip * (D // 4), D // 4, axis=2),
        "hgrn_lb_logits": lb_bwd(hgrn_lb_logits, g_lb_all),
        "hgrn_norm_g": total[o2:o3].reshape(DEPTH, A_V),
        "b_ada": total[o3:n_small].reshape(DEPTH, 9 * D),
    }
    dmod_all = small_all.reshape(8, -1)[:, o3:n_small].reshape(8, DEPTH, 9 * D).transpose(1, 0, 2)
    dm_sh = lax.dynamic_slice_in_dim(dmod_all, chip * ADA_N, ADA_N, axis=2)

    out_g, out_d, out_m, out_v = {}, {}, {}, {}
    ada_outs, slots = ada_bwd_adam(c_all, dm_sh, w_ada, m_w_ada, v_w_ada, side=dev_exchange_plan([pending[n] for n in TAIL]))
    out_g["w_ada"], out_d["w_ada"], out_m["w_ada"], out_v["w_ada"] = ada_outs
    for n, s in zip(TAIL, slots):
        big_grads[n][0] = sum_share(s)
    for n in BIG:
        out_g[n], out_d[n], out_m[n], out_v[n] = adam_layers(weights[n], big_grads[n], mom[n], var[n])
    out_g.update(g_small)
    for n in g_small:
        out_d[n], out_m[n], out_v[n] = adam(weights[n], out_g[n], mom[n], var[n])
    return (loss, dx[None], *[out_g[n] for n in order], *[out_d[n] for n in order],
            *[out_m[n] for n in order], *[out_v[n] for n in order])
```

```python
import functools
import math

import jax
import jax.numpy as jnp
from jax import lax
from jax.experimental import pallas as pl
from jax.experimental.pallas import tpu as pltpu

f32, bf16 = jnp.float32, jnp.bfloat16

D = 1024
DEPTH = 4
D_FF = 2816
EPS = 1e-6
NEG_BIG = -1e30
TINY = 1e-30
A_HEADS, A_K, A_V, A_CHUNK = 6, 128, 64, 64
A_SUB = 16
A_CLAMP = 80.0
B_HEADS, HD = 6, 64
C_GROUPS = ((128, 1), (512, 4), (2048, 16))
C_BLK = 128
IN_COLS = 8832
O_AQ, O_AF, O_AI, O_AG = 0, 768, 1536, 1920
O_BQ, O_BK, O_BV = 2304, 2688, 3072
O_CQ, O_CK, O_CV = 3456, 4224, 4992
O_GATE = 5760
LANE = 128
ADAM_LR, ADAM_B1, ADAM_B2, ADAM_EPS, ADAM_WD, ADAM_STEP = 0.001, 0.9, 0.999, 1e-08, 0.01, 10
MESH = pl.DeviceIdType.MESH
VMEM_LIMIT = 56 * 1024 * 1024


def _alibi_slopes(n):
    def pow2(m):
        start = 2.0 ** (-8.0 / m)
        return [start ** (i + 1) for i in range(m)]
    if math.log2(n).is_integer():
        s = pow2(n)
    else:
        c = 2 ** int(math.floor(math.log2(n)))
        s = pow2(c) + pow2(2 * c)[0::2][: n - c]
    return sorted(s, reverse=True)


C_SLOPES = _alibi_slopes(12)


def _tile(n, prefs):
    for p in prefs:
        if n % p == 0:
            return p
    return n


def _cp(sem):
    return pltpu.CompilerParams(dimension_semantics=sem, vmem_limit_bytes=VMEM_LIMIT)


def _sig(x):
    return 1.0 / (1.0 + jnp.exp(-x))


def _dot(a, b, dn, precision=None):
    return lax.dot_general(a, b, (dn, ((), ())), preferred_element_type=f32, precision=precision)


NN = ((1,), (0,))
NT = ((1,), (1,))
TN = ((0,), (0,))


class Plan:
    def __init__(self, ins, outs, n_remote, n_local, start, wait):
        self.ins, self.outs, self.n_remote, self.n_local, self.start, self.wait = ins, outs, n_remote, n_local, start, wait

    def sems(self):
        return [pltpu.SemaphoreType.DMA((self.n_remote,)), pltpu.SemaphoreType.DMA((self.n_remote,)),
                pltpu.SemaphoreType.DMA((max(self.n_local, 1),))]


def _call(body, *, name, grid, in_specs, out_specs, out_shape, sem, args, scratch_shapes=(), side=None):
    if side is None:
        return pl.pallas_call(body, name=name, grid=grid, in_specs=in_specs, out_specs=out_specs, out_shape=out_shape,
                              scratch_shapes=list(scratch_shapes), compiler_params=_cp(sem))(*args), None
    any_spec = pl.BlockSpec(memory_space=pl.ANY)
    n_in, n_out, n_scr = len(in_specs), len(out_specs), len(scratch_shapes)
    s_in, s_out = len(side.ins), len(side.outs)

    def hosted(*refs):
        ins, rest = refs[:n_in], refs[n_in:]
        sins, rest = rest[:s_in], rest[s_in:]
        outs, rest = rest[:n_out], rest[n_out:]
        souts, rest = rest[:s_out], rest[s_out:]
        scr, sems = rest[:n_scr], rest[n_scr:]
        pids = [pl.program_id(d) for d in range(len(grid))]
        first = functools.reduce(jnp.logical_and, [p == 0 for p in pids])
        last = functools.reduce(jnp.logical_and, [p == g - 1 for p, g in zip(pids, grid)])

        @pl.when(first)
        def _():
            side.start(sins, souts, *sems)

        body(*ins, *outs, *scr)

        @pl.when(last)
        def _():
            side.wait(sins, souts, *sems)

    res = pl.pallas_call(
        hosted, name=name, grid=grid, in_specs=list(in_specs) + [any_spec] * s_in,
        out_specs=list(out_specs) + [any_spec] * s_out, out_shape=list(out_shape) + list(side.outs),
        scratch_shapes=list(scratch_shapes) + side.sems(), compiler_params=_cp(("arbitrary",) * len(grid)),
    )(*args, *side.ins)
    return res[:n_out], res[n_out:]


MM_TILES = {
    (4096, 1024, 2816): (1024, 512, 2816),
    (4096, 1024, 5632): (512, 512, 5632),
    (1024, 5632, 4096): (512, 512, 4096),
    (2816, 1024, 4096): (704, 512, 4096),
    (4096, 8832, 1024): (1024, 2944, 1024),
    (4096, 1024, 8832): (1024, 512, 2944),
    (1024, 8832, 4096): (512, 2944, 1024),
    (1024, 1024, 4096): (512, 512, 4096),
    (384, 1024, 4096): (384, 512, 4096),
    (256, 1024, 4096): (256, 512, 4096),
}


def mm(a, b, *, out_dtype=f32, name="mm", side=None):
    M, K = a.shape
    K2, N = b.shape
    assert K == K2, (a.shape, b.shape)
    tm, tn, tk = MM_TILES.get((M, N, K), (_tile(M, (1024, 704, 512, 384, 256, 128)), _tile(N, (512, 384, 256, 128)),
                                          _tile(K, (1024, 512, 1408, 384, 256, 128))))
    nk = K // tk

    def body(a_ref, b_ref, o_ref, *acc):
        p = _dot(a_ref[...].astype(bf16), b_ref[...].astype(bf16), NN)
        if nk == 1:
            o_ref[...] = p.astype(out_dtype)
            return
        acc_ref, = acc
        k = pl.program_id(2)

        @pl.when(k == 0)
        def _():
            acc_ref[...] = p

        @pl.when(k > 0)
        def _():
            acc_ref[...] += p

        @pl.when(k == nk - 1)
        def _():
            o_ref[...] = acc_ref[...].astype(out_dtype)

    outs, souts = _call(
        body, name=name, grid=(M // tm, N // tn, nk),
        in_specs=[pl.BlockSpec((tm, tk), lambda i, j, k: (i, k)), pl.BlockSpec((tk, tn), lambda i, j, k: (k, j))],
        out_specs=[pl.BlockSpec((tm, tn), lambda i, j, k: (i, j))],
        out_shape=[jax.ShapeDtypeStruct((M, N), out_dtype)],
        scratch_shapes=[pltpu.VMEM((tm, tn), f32)] if nk > 1 else [],
        sem=("parallel", "parallel", "arbitrary"), args=(a, b), side=side)
    return outs[0] if side is None else (outs[0], souts)


FF_TM = 1024
FF_T = 256
FF_NB = D_FF // FF_T


def ffn_in_swiglu(h, w_in, side=None):
    S = h.shape[0]

    def body(h_ref, wa_ref, wb_ref, a_ref, b_ref, s_ref, st_ref):
        hv = h_ref[...]
        a = _dot(hv, wa_ref[...], NN)
        b = _dot(hv, wb_ref[...], NN)
        a_ref[...] = a.astype(bf16)
        b_ref[...] = b.astype(bf16)
        s = (a * _sig(a) * b).astype(bf16)
        s_ref[...] = s
        st_ref[...] = s.T

    ospec = pl.BlockSpec((FF_TM, FF_T), lambda i, j: (i, j))
    osd = jax.ShapeDtypeStruct((S, D_FF), bf16)
    return _call(
        body, name="ffn_in", grid=(S // FF_TM, FF_NB),
        in_specs=[pl.BlockSpec((FF_TM, D), lambda i, j: (i, 0)), pl.BlockSpec((D, FF_T), lambda i, j: (0, j)),
                  pl.BlockSpec((D, FF_T), lambda i, j: (0, j + FF_NB))],
        out_specs=[ospec] * 3 + [pl.BlockSpec((FF_T, FF_TM), lambda i, j: (j, i))],
        out_shape=[osd] * 3 + [jax.ShapeDtypeStruct((D_FF, S), bf16)], sem=("parallel", "parallel"),
        args=(h, w_in, w_in), side=side)


def ffn_du(dy, w_out_t, ua, ub):
    S = dy.shape[0]
    tm = 512

    def body(dy_ref, w_ref, a_ref, b_ref, du_ref):
        dyv = dy_ref[...]
        for j in range(FF_NB):
            cols = slice(j * FF_T, (j + 1) * FF_T)
            ds = _dot(dyv, w_ref[:, cols], NN)
            a, b = a_ref[:, cols].astype(f32), b_ref[:, cols].astype(f32)
            sg = _sig(a)
            du_ref[:, cols] = (ds * b * sg * (1.0 + a * (1.0 - sg))).astype(bf16)
            du_ref[:, D_FF + j * FF_T:D_FF + (j + 1) * FF_T] = (ds * a * sg).astype(bf16)

    half = pl.BlockSpec((tm, D_FF), lambda i: (i, 0))
    return pl.pallas_call(
        body, name="ffn_du", grid=(S // tm,),
        in_specs=[pl.BlockSpec((tm, D), lambda i: (i, 0)), pl.BlockSpec((D, D_FF), lambda i: (0, 0)), half, half],
        out_specs=pl.BlockSpec((tm, 2 * D_FF), lambda i: (i, 0)),
        out_shape=jax.ShapeDtypeStruct((S, 2 * D_FF), bf16), compiler_params=_cp(("parallel",)),
    )(dy, w_out_t, ua, ub)


TR = 512


def _row_spec(cols=D):
    return pl.BlockSpec((TR, cols), lambda i: (i, 0))


def _vec_spec(cols=D):
    return pl.BlockSpec((1, cols), lambda i: (0, 0))


def prenorm(x, a_vec, sh_vec):
    S = x.shape[0]

    def body(x_ref, a_ref, s_ref, h_ref, ht_ref):
        xv = x_ref[...]
        rstd = lax.rsqrt(jnp.mean(xv * xv, axis=1, keepdims=True) + EPS)
        h = (xv * rstd * a_ref[...] + s_ref[...]).astype(bf16)
        h_ref[...] = h
        ht_ref[...] = h.T

    return pl.pallas_call(
        body, name="prenorm", grid=(S // TR,),
        in_specs=[_row_spec(), _vec_spec(), _vec_spec()],
        out_specs=[_row_spec(), pl.BlockSpec((D, TR), lambda i: (0, i))],
        out_shape=[jax.ShapeDtypeStruct((S, D), bf16), jax.ShapeDtypeStruct((D, S), bf16)],
        compiler_params=_cp(("parallel",)),
    )(x, a_vec, sh_vec)


def postnorm(x, y, b_vec):
    S = x.shape[0]

    def body(x_ref, y_ref, b_ref, o_ref):
        yv = y_ref[...]
        rstd = lax.rsqrt(jnp.mean(yv * yv, axis=1, keepdims=True) + EPS)
        o_ref[...] = x_ref[...] + b_ref[...] * (yv * rstd)

    return pl.pallas_call(
        body, name="postnorm", grid=(S // TR,),
        in_specs=[_row_spec(), _row_spec(), _vec_spec()], out_specs=_row_spec(),
        out_shape=jax.ShapeDtypeStruct((S, D), f32), compiler_params=_cp(("parallel",)),
    )(x, y, b_vec)


def post_bwd(dout, y, b_vec):
    S = dout.shape[0]

    def body(d_ref, y_ref, b_ref, dy_ref, db_ref):
        i = pl.program_id(0)
        yv, dv = y_ref[...], d_ref[...]
        rstd = lax.rsqrt(jnp.mean(yv * yv, axis=1, keepdims=True) + EPS)
        yh = yv * rstd
        dyh = dv * b_ref[...]
        dy_ref[...] = (rstd * (dyh - yh * jnp.mean(dyh * yh, axis=1, keepdims=True))).astype(bf16)
        part = jnp.sum(dv * yh, axis=0, keepdims=True)

        @pl.when(i == 0)
        def _():
            db_ref[...] = part

        @pl.when(i > 0)
        def _():
            db_ref[...] += part

    return pl.pallas_call(
        body, name="post_bwd", grid=(S // TR,),
        in_specs=[_row_spec(), _row_spec(), _vec_spec()], out_specs=[_row_spec(), _vec_spec()],
        out_shape=[jax.ShapeDtypeStruct((S, D), bf16), jax.ShapeDtypeStruct((1, D), f32)],
        compiler_params=_cp(("arbitrary",)),
    )(dout, y, b_vec)


def pre_bwd(dout, dh, x, a_vec):
    S = dout.shape[0]

    def body(d_ref, dh_ref, x_ref, a_ref, dx_ref, ds_ref, da_ref):
        i = pl.program_id(0)
        xv, dhv = x_ref[...], dh_ref[...]
        rstd = lax.rsqrt(jnp.mean(xv * xv, axis=1, keepdims=True) + EPS)
        n1 = xv * rstd
        dn = dhv * a_ref[...]
        dx_ref[...] = d_ref[...] + rstd * (dn - n1 * jnp.mean(dn * n1, axis=1, keepdims=True))
        p_s = jnp.sum(dhv, axis=0, keepdims=True)
        p_a = jnp.sum(dhv * n1, axis=0, keepdims=True)

        @pl.when(i == 0)
        def _():
            ds_ref[...] = p_s
            da_ref[...] = p_a

        @pl.when(i > 0)
        def _():
            ds_ref[...] += p_s
            da_ref[...] += p_a

    return pl.pallas_call(
        body, name="pre_bwd", grid=(S // TR,),
        in_specs=[_row_spec(), _row_spec(), _row_spec(), _vec_spec()],
        out_specs=[_row_spec(), _vec_spec(), _vec_spec()],
        out_shape=[jax.ShapeDtypeStruct((S, D), f32), jax.ShapeDtypeStruct((1, D), f32), jax.ShapeDtypeStruct((1, D), f32)],
        compiler_params=_cp(("arbitrary",)),
    )(dout, dh, x, a_vec)


def loss_grad(y, tgt):
    S = y.shape[0]

    def body(y_ref, t_ref, dy_ref, l_ref):
        i = pl.program_id(0)
        e = y_ref[...] - t_ref[...]
        dy_ref[...] = e * (1.0 / D)
        part = jnp.sum(jnp.sum(e * e, axis=1, keepdims=True), axis=0, keepdims=True) * (0.5 / D)
        part = jnp.broadcast_to(part, (8, LANE))

        @pl.when(i == 0)
        def _():
            l_ref[...] = part

        @pl.when(i > 0)
        def _():
            l_ref[...] += part

    return pl.pallas_call(
        body, name="loss_grad", grid=(S // TR,),
        in_specs=[_row_spec(), _row_spec()],
        out_specs=[_row_spec(), pl.BlockSpec((8, LANE), lambda i: (0, 0))],
        out_shape=[jax.ShapeDtypeStruct((S, D), f32), jax.ShapeDtypeStruct((8, LANE), f32)],
        compiler_params=_cp(("arbitrary",)),
    )(y, tgt)


G_NB = D // LANE
G_TR = 2048
G_OFF = O_GATE // LANE


def gate_merge(u, pa, pb, pc):
    S = u.shape[0]

    def body(g0, g1, g2, a, b, c, o_ref, ot_ref):
        m = (_sig(g0[...]) * a[...] + _sig(g1[...]) * b[...] + _sig(g2[...]) * c[...]).astype(bf16)
        o_ref[...] = m
        ot_ref[...] = m.T

    gs = [pl.BlockSpec((G_TR, LANE), functools.partial(lambda i, j, k: (i, G_OFF + G_NB * k + j), k=k)) for k in range(3)]
    ps = pl.BlockSpec((G_TR, LANE), lambda i, j: (i, j))
    return pl.pallas_call(
        body, name="gate_merge", grid=(S // G_TR, G_NB),
        in_specs=gs + [ps, ps, ps], out_specs=[ps, pl.BlockSpec((LANE, G_TR), lambda i, j: (j, i))],
        out_shape=[jax.ShapeDtypeStruct((S, D), bf16), jax.ShapeDtypeStruct((D, S), bf16)],
        compiler_params=_cp(("parallel", "parallel")),
    )(u, u, u, pa, pb, pc)


def gate_bwd(dm, u, pa, pb, pc):
    S = u.shape[0]

    def body(dm_ref, g0, g1, g2, a, b, c, da, db, dc, dg0, dg1, dg2):
        d = dm_ref[...]
        for g, p, dp, dg in ((g0, a, da, dg0), (g1, b, db, dg1), (g2, c, dc, dg2)):
            s = _sig(g[...])
            dp[...] = (d * s).astype(bf16)
            dg[...] = (d * p[...] * s * (1.0 - s)).astype(bf16)

    gs = [pl.BlockSpec((G_TR, LANE), functools.partial(lambda i, j, k: (i, G_OFF + G_NB * k + j), k=k)) for k in range(3)]
    ps = pl.BlockSpec((G_TR, LANE), lambda i, j: (i, j))
    osd = jax.ShapeDtypeStruct((S, D), bf16)
    return pl.pallas_call(
        body, name="gate_bwd", grid=(S // G_TR, G_NB),
        in_specs=[ps] + gs + [ps, ps, ps], out_specs=[ps] * 6, out_shape=[osd] * 6,
        compiler_params=_cp(("parallel", "parallel")),
    )(dm, u, u, u, pa, pb, pc)


A_TB = 512
A_NCH = A_TB // A_CHUNK
A_NSUB = A_CHUNK // A_SUB
A_HP = 6
A_KW, A_VW = A_HP * A_K, A_HP * A_V


def _hgrn_gates(qr, fr, lbh):
    sq = _sig(qr)
    sig = _sig(fr)
    f = lbh + (1.0 - lbh) * sig
    logf = jnp.log(jnp.maximum(f, TINY))
    k = (1.0 - lbh) * (1.0 - sig)
    return qr * sq, sq, sig, f, logf, k


def _hgrn_intra(qf, k, b, causal):
    qts, kts, eqs, eks, blocks = [], [], [], [], []
    for sb in range(A_NSUB):
        rs = sb * A_SUB
        r = b[rs - 1:rs, :] if sb else jnp.zeros((1, A_K), f32)
        eq = jnp.exp(b[rs:rs + A_SUB, :] - r)
        ek = jnp.exp(jnp.minimum(r - b, A_CLAMP))
        qt = (qf[rs:rs + A_SUB, :] * eq).astype(bf16)
        kt = (k * ek).astype(bf16)
        blocks.append(_dot(qt, kt, NT))
        qts.append(qt), kts.append(kt), eqs.append(eq), eks.append(ek)
    a = jnp.where(causal, jnp.concatenate(blocks, axis=0), 0.0)
    return a, qts, kts, eqs, eks


def _tri_sum(tri, x, dn):
    t = tri.astype(bf16)
    hi = x.astype(bf16)
    rest = x - hi.astype(f32)
    mid = rest.astype(bf16)
    lo = (rest - mid.astype(f32)).astype(bf16)
    return _dot(t, hi, dn) + _dot(t, mid, dn) + _dot(t, lo, dn)


def _tri():
    r = lax.broadcasted_iota(jnp.int32, (A_CHUNK, A_CHUNK), 0)
    c = lax.broadcasted_iota(jnp.int32, (A_CHUNK, A_CHUNK), 1)
    return r >= c


def _hgrn_in_specs(rev_nb=None):
    def im(col):
        if rev_nb is None:
            return lambda p, i: (i, col + p)
        return lambda p, i: (rev_nb - 1 - i, col + p)
    return [pl.BlockSpec((A_TB, A_KW), im(O_AQ // A_KW)), pl.BlockSpec((A_TB, A_KW), im(O_AF // A_KW)),
            pl.BlockSpec((A_TB, A_VW), im(O_AI // A_VW)), pl.BlockSpec((A_TB, A_VW), im(O_AG // A_VW)),
            pl.BlockSpec((1, A_KW), lambda p, i: (0, p)), pl.BlockSpec((1, A_V), lambda p, i: (0, 0))]


def hgrn_fwd(u, lb, ng, side=None):
    S = u.shape[0]
    nb = S // A_TB

    def body(q_ref, f_ref, i_ref, g_ref, lb_ref, ng_ref, o_ref, ya_ref, st_ref, state):
        @pl.when(pl.program_id(1) == 0)
        def _():
            state[...] = jnp.zeros_like(state)

        causal = _tri()
        tri = causal.astype(f32)

        def chunk(n, carry):
            rows = pl.ds(pl.multiple_of(n * A_CHUNK, A_CHUNK), A_CHUNK)
            o_parts, y_parts = [], []
            for hh in range(A_HP):
                ks = slice(hh * A_K, (hh + 1) * A_K)
                vs = slice(hh * A_V, (hh + 1) * A_V)
                qf, _, _, _, logf, k = _hgrn_gates(q_ref[rows, ks], f_ref[rows, ks], lb_ref[:, ks])
                vi = i_ref[rows, vs].astype(bf16)
                gg = g_ref[rows, vs]
                b = _tri_sum(tri, logf, NN)
                s0 = state[hh]
                st_ref[n, hh] = s0
                o = _dot((qf * jnp.exp(b)).astype(bf16), s0.astype(bf16), NT)
                a, _, _, _, _ = _hgrn_intra(qf, k, b, causal)
                o = o + _dot(a.astype(bf16), vi, NN)
                bend = b[A_CHUNK - 1:A_CHUNK, :]
                ke = (k * jnp.exp(bend - b)).astype(bf16)
                state[hh] = s0 * jnp.exp(bend) + _dot(vi, ke, TN)
                rstd = lax.rsqrt(jnp.mean(o * o, axis=1, keepdims=True) + EPS)
                o_parts.append(o)
                y_parts.append(o * rstd * ng_ref[...] * (gg * _sig(gg)))
            o_ref[rows, :] = jnp.concatenate(o_parts, axis=1)
            ya_ref[rows, :] = jnp.concatenate(y_parts, axis=1).astype(bf16)
            return carry

        lax.fori_loop(0, A_NCH, chunk, 0)

    return _call(
        body, name="hgrn_fwd", grid=(A_HEADS // A_HP, nb),
        in_specs=_hgrn_in_specs(),
        out_specs=[pl.BlockSpec((A_TB, A_VW), lambda p, i: (i, p)), pl.BlockSpec((A_TB, A_VW), lambda p, i: (i, p)),
                   pl.BlockSpec((A_NCH, A_HP, A_V, A_K), lambda p, i: (i, p, 0, 0))],
        out_shape=[jax.ShapeDtypeStruct((S, 384), f32), jax.ShapeDtypeStruct((S, 384), bf16),
                   jax.ShapeDtypeStruct((S // A_CHUNK, A_HEADS, A_V, A_K), f32)],
        scratch_shapes=[pltpu.VMEM((A_HP, A_V, A_K), f32)],
        sem=("parallel", "arbitrary"), args=(u, u, u, u, lb, ng), side=side)


def hgrn_bwd(u, lb, ng, o, st, dya, side=None):
    S = u.shape[0]
    nb = S // A_TB

    def body(q_ref, f_ref, i_ref, g_ref, lb_ref, ng_ref, o_ref, st_ref, dy_ref,
             dq_ref, df_ref, di_ref, dg_ref, dlb_ref, dng_ref, dstate):
        @pl.when(pl.program_id(1) == 0)
        def _():
            dstate[...] = jnp.zeros_like(dstate)
            dlb_ref[...] = jnp.zeros_like(dlb_ref)
            dng_ref[...] = jnp.zeros_like(dng_ref)

        causal = _tri()
        tri = causal.astype(f32)

        def chunk(it, carry):
            n = A_NCH - 1 - it
            rows = pl.ds(pl.multiple_of(n * A_CHUNK, A_CHUNK), A_CHUNK)
            dq_p, df_p, di_p, dg_p, dlb_p = [], [], [], [], []
            dng_acc = jnp.zeros((1, A_V), f32)
            for hh in range(A_HP):
                ks = slice(hh * A_K, (hh + 1) * A_K)
                vs = slice(hh * A_V, (hh + 1) * A_V)
                lbh = lb_ref[:, ks]
                qr = q_ref[rows, ks]
                qf, sq, sig, f, logf, k = _hgrn_gates(qr, f_ref[rows, ks], lbh)
                vi = i_ref[rows, vs].astype(bf16)
                gg = g_ref[rows, vs]
                b = _tri_sum(tri, logf, NN)
                eb = jnp.exp(b)
                bend = b[A_CHUNK - 1:A_CHUNK, :]
                eend = jnp.exp(bend)
                ekend = jnp.exp(bend - b)
                qe = (qf * eb).astype(bf16)
                ke = (k * ekend).astype(bf16)
                s0 = st_ref[n, hh]
                dsend = dstate[hh]
                ov = o_ref[rows, vs]
                dy = dy_ref[rows, vs]
                rstd = lax.rsqrt(jnp.mean(ov * ov, axis=1, keepdims=True) + EPS)
                oh = ov * rstd
                sg = _sig(gg)
                d_on = dy * (gg * sg)
                dg_p.append(dy * oh * ng_ref[...] * (sg * (1.0 + gg * (1.0 - sg))))
                dng_acc = dng_acc + jnp.sum(d_on * oh, axis=0, keepdims=True)
                doh = d_on * ng_ref[...]
                do = (rstd * (doh - oh * jnp.mean(doh * oh, axis=1, keepdims=True))).astype(bf16)
                a, qts, kts, eqs, eks = _hgrn_intra(qf, k, b, causal)
                da = jnp.where(causal, _dot(do, vi, NT), 0.0).astype(bf16)
                dsb = dsend.astype(bf16)
                dv = _dot(a.astype(bf16), do, TN) + _dot(ke, dsb, NT)
                dq = _dot(do, s0.astype(bf16), NN) * eb
                dk_state = _dot(vi, dsb, NN) * ekend
                dk = dk_state
                dq_i = []
                for sb in range(A_NSUB):
                    da_sb = da[sb * A_SUB:(sb + 1) * A_SUB, :]
                    dq_i.append(_dot(da_sb, kts[sb], NN) * eqs[sb])
                    dk = dk + _dot(da_sb, qts[sb], TN) * eks[sb]
                dq = dq + jnp.concatenate(dq_i, axis=0)
                db = qf * dq - k * dk
                extra = jnp.sum(k * dk_state, axis=0, keepdims=True) + eend * jnp.sum(s0 * dsend, axis=0, keepdims=True)
                dlogf = _tri_sum(tri, db, TN) + extra
                dstate[hh] = _dot(do, qe, TN) + eend * dsend
                d_pre = jnp.where(f > TINY, dlogf / f, 0.0) - dk
                dlb_p.append(jnp.sum((1.0 - sig) * d_pre, axis=0, keepdims=True))
                df_p.append((1.0 - lbh) * d_pre * sig * (1.0 - sig))
                dq_p.append(dq * (sq * (1.0 + qr * (1.0 - sq))))
                di_p.append(dv)
            dq_ref[rows, :] = jnp.concatenate(dq_p, axis=1).astype(bf16)
            df_ref[rows, :] = jnp.concatenate(df_p, axis=1).astype(bf16)
            di_ref[rows, :] = jnp.concatenate(di_p, axis=1).astype(bf16)
            dg_ref[rows, :] = jnp.concatenate(dg_p, axis=1).astype(bf16)
            dlb_ref[...] += jnp.concatenate(dlb_p, axis=1)
            dng_ref[0] += dng_acc
            return carry

        lax.fori_loop(0, A_NCH, chunk, 0)

    rev = lambda p, i: (nb - 1 - i, p)
    return _call(
        body, name="hgrn_bwd", grid=(A_HEADS // A_HP, nb),
        in_specs=_hgrn_in_specs(nb) + [pl.BlockSpec((A_TB, A_VW), rev),
                                       pl.BlockSpec((A_NCH, A_HP, A_V, A_K), lambda p, i: (nb - 1 - i, p, 0, 0)),
                                       pl.BlockSpec((A_TB, A_VW), rev)],
        out_specs=[pl.BlockSpec((A_TB, A_KW), rev), pl.BlockSpec((A_TB, A_KW), rev),
                   pl.BlockSpec((A_TB, A_VW), rev), pl.BlockSpec((A_TB, A_VW), rev),
                   pl.BlockSpec((1, A_KW), lambda p, i: (0, p)), pl.BlockSpec((1, 1, A_V), lambda p, i: (p, 0, 0))],
        out_shape=[jax.ShapeDtypeStruct((S, 768), bf16), jax.ShapeDtypeStruct((S, 768), bf16),
                   jax.ShapeDtypeStruct((S, 384), bf16), jax.ShapeDtypeStruct((S, 384), bf16),
                   jax.ShapeDtypeStruct((1, 768), f32), jax.ShapeDtypeStruct((A_HEADS // A_HP, 1, A_V), f32)],
        scratch_shapes=[pltpu.VMEM((A_HP, A_V, A_K), f32)],
        sem=("parallel", "arbitrary"), args=(u, u, u, u, lb, ng, o, st, dya), side=side)


B_TK = 128
SCALE = HD ** -0.5


def _split(x):
    hi = x.astype(bf16)
    return hi, (x - hi.astype(f32)).astype(bf16)


def _dot2(x, m, dn):
    hi, lo = _split(x)
    return _dot(hi, m, dn) + _dot(lo, m, dn)


def _sb_block(qs, kh, mask, m_gt, c):
    z = _dot(qs, kh, NT)
    sp = jnp.maximum(z, 0.0) + jnp.log(1.0 + jnp.exp(-jnp.abs(z)))
    lneg = jnp.where(mask, -sp, 0.0)
    lsz = z - sp
    suf = _dot2(lneg, m_gt, NN) + c
    a = jnp.where(mask, jnp.exp(lsz + suf), 0.0)
    return lneg, lsz, a


def _sb_masks(tq, i, jj):
    t_idx = i * tq + lax.broadcasted_iota(jnp.int32, (tq, B_TK), 0)
    s_idx = jj * B_TK + lax.broadcasted_iota(jnp.int32, (tq, B_TK), 1)
    return s_idx < t_idx


def _sb_tri(strict):
    r = lax.broadcasted_iota(jnp.int32, (B_TK, B_TK), 0)
    c = lax.broadcasted_iota(jnp.int32, (B_TK, B_TK), 1)
    return (r > c if strict else r >= c).astype(bf16)


B_DEAD = -88.0


def _sb_walk(nkb, step, init):
    def cond(state):
        it, alive, _ = state
        return jnp.logical_and(it < nkb, alive)

    def body(state):
        it, _, carry = state
        carry = step(it, carry)
        top = jnp.max(functools.reduce(jnp.maximum, [h[1] for h in carry]))
        return it + 1, top > B_DEAD, carry

    return lax.while_loop(cond, body, (jnp.int32(0), jnp.bool_(True), init))[2]


B_HP = 6
B_W = B_HP * HD


def sb_fwd(u, side=None):
    S = u.shape[0]
    tq = 128

    def body(q_ref, k_ref, v_ref, o_ref):
        i = pl.program_id(1)
        nkb = (i + 1) * (tq // B_TK)
        m_gt = _sb_tri(True)
        qs = [(q_ref[:, hh * HD:(hh + 1) * HD] * SCALE).astype(bf16) for hh in range(B_HP)]

        def step(it, carry):
            jj = nkb - 1 - it
            rows = pl.ds(pl.multiple_of(jj * B_TK, B_TK), B_TK)
            mask = _sb_masks(tq, i, jj)
            kb, vb = k_ref[rows, :], v_ref[rows, :]
            out = []
            for hh in range(B_HP):
                acc, c = carry[hh]
                kh = kb[:, hh * HD:(hh + 1) * HD].astype(bf16)
                vh = vb[:, hh * HD:(hh + 1) * HD].astype(bf16)
                lneg, _, a = _sb_block(qs[hh], kh, mask, m_gt, c)
                out.append((acc + _dot2(a, vh, NN), c + jnp.sum(lneg, axis=1, keepdims=True)))
            return tuple(out)

        z0 = (jnp.zeros((tq, HD), f32), jnp.zeros((tq, 1), f32))
        res = _sb_walk(nkb, step, (z0,) * B_HP)
        o_ref[...] = jnp.concatenate([r[0] for r in res], axis=1)

    outs, souts = _call(
        body, name="sb_fwd", grid=(B_HEADS // B_HP, S // tq),
        in_specs=[pl.BlockSpec((tq, B_W), lambda p, i: (i, O_BQ // B_W + p)),
                  pl.BlockSpec((S, B_W), lambda p, i: (0, O_BK // B_W + p)),
                  pl.BlockSpec((S, B_W), lambda p, i: (0, O_BV // B_W + p))],
        out_specs=[pl.BlockSpec((tq, B_W), lambda p, i: (i, p))],
        out_shape=[jax.ShapeDtypeStruct((S, 384), f32)],
        sem=("parallel", "arbitrary"), args=(u, u, u), side=side)
    return outs[0], souts


def sb_bwd(u, yb, dyb, side=None):
    S = u.shape[0]
    tq = 128
    nq = S // tq

    def body(q_ref, k_ref, v_ref, y_ref, dy_ref, dq_ref, dk_out, dv_out, dk_ref, dv_ref, out_sem):
        p, i = pl.program_id(0), pl.program_id(1)

        @pl.when(i == 0)
        def _():
            dk_ref[...] = jnp.zeros_like(dk_ref)
            dv_ref[...] = jnp.zeros_like(dv_ref)

        nkb = (i + 1) * (tq // B_TK)
        m_gt = _sb_tri(True)
        m_ge = _sb_tri(False)
        qs, dos, tot = [], [], []
        for hh in range(B_HP):
            hs = slice(hh * HD, (hh + 1) * HD)
            qs.append((q_ref[:, hs] * SCALE).astype(bf16))
            dob = dy_ref[:, hs].astype(bf16)
            dos.append(dob)
            tot.append(jnp.sum(dob.astype(f32) * y_ref[:, hs], axis=1, keepdims=True))

        def step(it, carry):
            jj = nkb - 1 - it
            rows = pl.ds(pl.multiple_of(jj * B_TK, B_TK), B_TK)
            mask = _sb_masks(tq, i, jj)
            kb, vb = k_ref[rows, :], v_ref[rows, :]
            out, dk_p, dv_p = [], [], []
            for hh in range(B_HP):
                dq, c, cg = carry[hh]
                kh = kb[:, hh * HD:(hh + 1) * HD].astype(bf16)
                vh = vb[:, hh * HD:(hh + 1) * HD].astype(bf16)
                lneg, lsz, a = _sb_block(qs[hh], kh, mask, m_gt, c)
                g = a * _dot(dos[hh], vh, NT)
                pre = tot[hh] - cg - _dot2(g, m_ge, NN)
                beta = jnp.exp(lsz)
                dz = jnp.where(mask, g * (1.0 - beta) - beta * pre, 0.0).astype(bf16)
                dk_p.append(_dot(dz, qs[hh], TN))
                dv_p.append(_dot(a.astype(bf16), dos[hh], TN))
                out.append((dq + _dot(dz, kh, NN), c + jnp.sum(lneg, axis=1, keepdims=True),
                            cg + jnp.sum(g, axis=1, keepdims=True)))
            dk_ref[rows, :] += jnp.concatenate(dk_p, axis=1)
            dv_ref[rows, :] += jnp.concatenate(dv_p, axis=1)
            return tuple(out)

        z0 = (jnp.zeros((tq, HD), f32), jnp.zeros((tq, 1), f32), jnp.zeros((tq, 1), f32))
        res = _sb_walk(nkb, step, (z0,) * B_HP)
        dq_ref[...] = jnp.concatenate([r[0] for r in res], axis=1) * SCALE

        @pl.when(i == nq - 1)
        def _():
            cols = pl.ds(pl.multiple_of(p * B_W, LANE), B_W)
            ck = pltpu.make_async_copy(dk_ref, dk_out.at[:, cols], out_sem.at[0])
            cv = pltpu.make_async_copy(dv_ref, dv_out.at[:, cols], out_sem.at[1])
            ck.start()
            cv.start()
            ck.wait()
            cv.wait()

    row = pl.BlockSpec((tq, B_W), lambda p, i: (i, p))
    hbm = pl.BlockSpec(memory_space=pl.ANY)
    osd = jax.ShapeDtypeStruct((S, 384), f32)
    return _call(
        body, name="sb_bwd", grid=(B_HEADS // B_HP, nq),
        in_specs=[pl.BlockSpec((tq, B_W), lambda p, i: (i, O_BQ // B_W + p)),
                  pl.BlockSpec((S, B_W), lambda p, i: (0, O_BK // B_W + p)),
                  pl.BlockSpec((S, B_W), lambda p, i: (0, O_BV // B_W + p)), row, row],
        out_specs=[row, hbm, hbm], out_shape=[osd, osd, osd],
        scratch_shapes=[pltpu.VMEM((S, B_W), f32), pltpu.VMEM((S, B_W), f32), pltpu.SemaphoreType.DMA((2,))],
        sem=("parallel", "arbitrary"), args=(u, u, u, yb, dyb), side=side)


def _dil_rows(i, rho, r):
    if r == 1:
        return pl.ds(pl.multiple_of(i * C_BLK, C_BLK), C_BLK)
    return pl.ds(i * (C_BLK * r) + rho, C_BLK, stride=r)


def _dil_scores(qs, kc, kp, i, slope_r):
    qi = lax.broadcasted_iota(jnp.int32, (C_BLK, C_BLK), 0)
    kj = lax.broadcasted_iota(jnp.int32, (C_BLK, C_BLK), 1)
    d_c = qi - kj
    d_p = d_c + C_BLK
    ok_c = d_c >= 0
    ok_p = jnp.logical_and(d_c <= 0, i > 0)
    s_c = jnp.where(ok_c, _dot(qs, kc, NT) - slope_r * d_c.astype(f32), NEG_BIG)
    s_p = jnp.where(ok_p, _dot(qs, kp, NT) - slope_r * d_p.astype(f32), NEG_BIG)
    return s_c, s_p, ok_c, ok_p


def _dil_slope(g, r, hh):
    pair = pl.program_id(0)
    return jnp.where(pair == 0, C_SLOPES[4 * g + hh] * r, C_SLOPES[4 * g + 2 + hh] * r).astype(f32)


def _dil_u_specs(g, S):
    def im(off):
        return lambda p, rho: (0, (off + g * 256) // LANE + p)
    return [pl.BlockSpec((S, LANE), im(O_CQ)), pl.BlockSpec((S, LANE), im(O_CK)), pl.BlockSpec((S, LANE), im(O_CV))]


def dil_fwd(u, g, side=None):
    S = u.shape[0]
    r = C_GROUPS[g][1]
    nbk = S // r // C_BLK

    def body(q_ref, k_ref, v_ref, o_ref, l_ref):
        rho = pl.program_id(1)

        def step(i, carry):
            rc = _dil_rows(i, rho, r)
            rp = _dil_rows(jnp.maximum(i - 1, 0), rho, r)
            q2, kc2, kp2, vc2, vp2 = q_ref[rc, :], k_ref[rc, :], k_ref[rp, :], v_ref[rc, :], v_ref[rp, :]
            o_p, l_p = [], []
            for hh in range(2):
                hs = slice(hh * HD, (hh + 1) * HD)
                qs = (q2[:, hs] * SCALE).astype(bf16)
                kc, kp = kc2[:, hs].astype(bf16), kp2[:, hs].astype(bf16)
                vc, vp = vc2[:, hs].astype(bf16), vp2[:, hs].astype(bf16)
                s_c, s_p, _, _ = _dil_scores(qs, kc, kp, i, _dil_slope(g, r, hh))
                m = jnp.maximum(jnp.max(s_c, axis=1, keepdims=True), jnp.max(s_p, axis=1, keepdims=True))
                p_c, p_p = jnp.exp(s_c - m), jnp.exp(s_p - m)
                den = jnp.sum(p_c, axis=1, keepdims=True) + jnp.sum(p_p, axis=1, keepdims=True)
                o_p.append((_dot(p_c.astype(bf16), vc, NN) + _dot(p_p.astype(bf16), vp, NN)) / den)
                l_p.append(jnp.broadcast_to(m + jnp.log(den), (C_BLK, HD)))
            o_ref[rc, :] = jnp.concatenate(o_p, axis=1)
            l_ref[rc, :] = jnp.concatenate(l_p, axis=1)
            return carry

        lax.fori_loop(0, nbk, step, 0, unroll=2)

    ospec = pl.BlockSpec((S, LANE), lambda p, rho: (0, p))
    osd = jax.ShapeDtypeStruct((S, 256), f32)
    return _call(
        body, name=f"dil_fwd{g}", grid=(2, r),
        in_specs=_dil_u_specs(g, S), out_specs=[ospec, ospec], out_shape=[osd, osd],
        sem=("parallel", "arbitrary"), args=(u, u, u), side=side)


def dil_merge(os_, ls_):
    S = os_[0].shape[0]

    def body(o0, o1, o2, l0, l1, l2, y_ref, lse_ref):
        a, b, c = l0[...], l1[...], l2[...]
        m = jnp.maximum(jnp.maximum(a, b), c)
        ea, eb, ec = jnp.exp(a - m), jnp.exp(b - m), jnp.exp(c - m)
        den = ea + eb + ec
        y_ref[...] = (ea * o0[...] + eb * o1[...] + ec * o2[...]) / den
        lse_ref[...] = m + jnp.log(den)

    spec = pl.BlockSpec((512, 256), lambda i: (i, 0))
    osd = jax.ShapeDtypeStruct((S, 256), f32)
    return pl.pallas_call(
        body, name="dil_merge", grid=(S // 512,), in_specs=[spec] * 6, out_specs=[spec, spec],
        out_shape=[osd, osd], compiler_params=_cp(("parallel",)),
    )(*os_, *ls_)


def dil_bwd(u, g, dyc, yc, lse):
    S = u.shape[0]
    r = C_GROUPS[g][1]
    nbk = S // r // C_BLK

    def body(q_ref, k_ref, v_ref, dy_ref, y_ref, l_ref, dq_ref, dk_ref, dv_ref):
        rho = pl.program_id(1)

        @pl.when(rho == 0)
        def _():
            dk_ref[...] = jnp.zeros_like(dk_ref)
            dv_ref[...] = jnp.zeros_like(dv_ref)

        def step(i, carry):
            rc = _dil_rows(i, rho, r)
            rp = _dil_rows(jnp.maximum(i - 1, 0), rho, r)
            q2, kc2, kp2, vc2, vp2 = q_ref[rc, :], k_ref[rc, :], k_ref[rp, :], v_ref[rc, :], v_ref[rp, :]
            dy2, y2, l2 = dy_ref[rc, :], y_ref[rc, :], l_ref[rc, :]
            dq_p, dkc_p, dkp_p, dvc_p, dvp_p = [], [], [], [], []
            for hh in range(2):
                hs = slice(hh * HD, (hh + 1) * HD)
                qs = (q2[:, hs] * SCALE).astype(bf16)
                kc, kp = kc2[:, hs].astype(bf16), kp2[:, hs].astype(bf16)
                vc, vp = vc2[:, hs].astype(bf16), vp2[:, hs].astype(bf16)
                dy = dy2[:, hs]
                dyb = dy.astype(bf16)
                s_c, s_p, ok_c, ok_p = _dil_scores(qs, kc, kp, i, _dil_slope(g, r, hh))
                lrow = l2[:, hh * HD:hh * HD + 1]
                delta = jnp.sum(dy * y2[:, hs], axis=1, keepdims=True)
                pi_c = jnp.where(ok_c, jnp.exp(s_c - lrow), 0.0)
                pi_p = jnp.where(ok_p, jnp.exp(s_p - lrow), 0.0)
                ds_c = (pi_c * (_dot(dyb, vc, NT) - delta)).astype(bf16)
                ds_p = (pi_p * (_dot(dyb, vp, NT) - delta)).astype(bf16)
                dq_p.append((_dot(ds_c, kc, NN) + _dot(ds_p, kp, NN)) * SCALE)
                dkc_p.append(_dot(ds_c, qs, TN))
                dkp_p.append(_dot(ds_p, qs, TN))
                dvc_p.append(_dot(pi_c.astype(bf16), dyb, TN))
                dvp_p.append(_dot(pi_p.astype(bf16), dyb, TN))
            dq_ref[rc, :] = jnp.concatenate(dq_p, axis=1)
            dk_ref[rc, :] += jnp.concatenate(dkc_p, axis=1)
            dv_ref[rc, :] += jnp.concatenate(dvc_p, axis=1)
            dk_ref[rp, :] += jnp.concatenate(dkp_p, axis=1)
            dv_ref[rp, :] += jnp.concatenate(dvp_p, axis=1)
            return carry

        lax.fori_loop(0, nbk, step, 0, unroll=2)

    ospec = pl.BlockSpec((S, LANE), lambda p, rho: (0, p))
    osd = jax.ShapeDtypeStruct((S, 256), f32)
    return pl.pallas_call(
        body, name=f"dil_bwd{g}", grid=(2, r),
        in_specs=_dil_u_specs(g, S) + [ospec, ospec, ospec], out_specs=[ospec] * 3, out_shape=[osd] * 3,
        compiler_params=_cp(("parallel", "arbitrary")),
    )(u, u, u, dyc, yc, lse)


def _rows_tile(rows):
    return _tile(rows, (256, 176, 128, 64, 32, 16, 8))


def cast_bf16(w):
    shape = w.shape
    w2 = w.reshape(-1, shape[-1])
    rows, cols = w2.shape
    tr = _rows_tile(rows)

    def body(x_ref, o_ref):
        o_ref[...] = x_ref[...].astype(bf16)

    spec = pl.BlockSpec((tr, cols), lambda i: (i, 0))
    out = pl.pallas_call(
        body, name="cast_bf16", grid=(rows // tr,), in_specs=[spec], out_specs=spec,
        out_shape=jax.ShapeDtypeStruct((rows, cols), bf16), compiler_params=_cp(("parallel",)),
    )(w2)
    return out.reshape(shape)


BC1 = 1.0 - ADAM_B1 ** ADAM_STEP
BC2 = 1.0 - ADAM_B2 ** ADAM_STEP


def _adam_math(w, g, m, v):
    m2 = ADAM_B1 * m + (1.0 - ADAM_B1) * g
    v2 = ADAM_B2 * v + (1.0 - ADAM_B2) * (g * g)
    delta = -ADAM_LR * ((m2 / BC1) / (jnp.sqrt(v2 / BC2) + ADAM_EPS) + ADAM_WD * w)
    return delta, m2, v2


def adam(w, g, m, v):
    shape = w.shape
    r2 = lambda t: t.reshape(-1, shape[-1])
    rows, cols = r2(w).shape
    tr = _rows_tile(rows)

    def body(w_ref, g_ref, m_ref, v_ref, d_ref, m2_ref, v2_ref):
        d_ref[...], m2_ref[...], v2_ref[...] = _adam_math(w_ref[...], g_ref[...], m_ref[...], v_ref[...])

    spec = pl.BlockSpec((tr, cols), lambda i: (i, 0))
    osd = jax.ShapeDtypeStruct((rows, cols), f32)
    outs = pl.pallas_call(
        body, name="adam", grid=(rows // tr,), in_specs=[spec] * 4, out_specs=[spec] * 3, out_shape=[osd] * 3,
        compiler_params=_cp(("parallel",)),
    )(r2(w), r2(g), r2(m), r2(v))
    return [o.reshape(shape) for o in outs]


def adam_layers(w, gs, m, v):
    depth, rows, cols = w.shape
    tr = _tile(rows, (128, 64, 32, 16, 8))
    nb = rows // tr

    def body(w_ref, m_ref, v_ref, *rest):
        g_refs, (g_out, d_ref, m2_ref, v2_ref) = rest[:depth], rest[depth:]
        for k in range(depth):
            @pl.when(pl.program_id(0) == k)
            def _(k=k):
                g = g_refs[k][...]
                g_out[0] = g
                d_ref[0], m2_ref[0], v2_ref[0] = _adam_math(w_ref[0], g, m_ref[0], v_ref[0])

    def g_spec(k):
        return pl.BlockSpec((tr, cols), lambda l, i: (jnp.where(l < k, 0, jnp.where(l > k, nb - 1, i)), 0))

    wspec = pl.BlockSpec((1, tr, cols), lambda l, i: (l, i, 0))
    osd = jax.ShapeDtypeStruct(w.shape, f32)
    return pl.pallas_call(
        body, name="adam_layers", grid=(depth, nb), in_specs=[wspec] * 3 + [g_spec(k) for k in range(depth)],
        out_specs=[wspec] * 4, out_shape=[osd] * 4, compiler_params=_cp(("arbitrary", "arbitrary")),
    )(w, m, v, *gs)


ADA_N = 9 * D // 4
ADA_TN = 384


def ada_fwd(c_all, w_ada):
    def body(c_ref, w_ref, o_ref):
        cv = c_ref[...]
        o_ref[0] = _dot((cv * _sig(cv)).astype(bf16), w_ref[0].astype(bf16), NN)

    return pl.pallas_call(
        body, name="ada_fwd", grid=(DEPTH, ADA_N // ADA_TN),
        in_specs=[pl.BlockSpec((8, D), lambda l, j: (0, 0)), pl.BlockSpec((1, D, ADA_TN), lambda l, j: (l, 0, j))],
        out_specs=pl.BlockSpec((1, 8, ADA_TN), lambda l, j: (l, 0, j)),
        out_shape=jax.ShapeDtypeStruct((DEPTH, 8, ADA_N), f32), compiler_params=_cp(("parallel", "parallel")),
    )(c_all, w_ada)


def ada_bwd_adam(c_all, dm, w, m, v, side=None):
    tr = 128

    def body(c_ref, dm_ref, w_ref, m_ref, v_ref, g_ref, d_ref, m2_ref, v2_ref):
        cv = c_ref[...]
        g = _dot((cv * _sig(cv)).astype(bf16), dm_ref[0].astype(bf16), TN)
        g_ref[0] = g
        d_ref[0], m2_ref[0], v2_ref[0] = _adam_math(w_ref[0], g, m_ref[0], v_ref[0])

    wspec = pl.BlockSpec((1, tr, ADA_N), lambda l, i: (l, i, 0))
    osd = jax.ShapeDtypeStruct((DEPTH, D, ADA_N), f32)
    return _call(
        body, name="ada_bwd_adam", grid=(DEPTH, D // tr),
        in_specs=[pl.BlockSpec((8, tr), lambda l, i: (0, i)), pl.BlockSpec((1, 8, ADA_N), lambda l, i: (l, 0, 0)),
                  wspec, wspec, wspec],
        out_specs=[wspec] * 4, out_shape=[osd] * 4, sem=("parallel", "parallel"), args=(c_all, dm, w, m, v), side=side)


def _lb_probs(x):
    mx = jnp.max(x, axis=0, keepdims=True)
    e = jnp.exp(x - mx)
    return e / jnp.sum(e, axis=0, keepdims=True)


def lb_fwd(logits):
    def body(x_ref, o_ref):
        p = _lb_probs(x_ref[...])
        rows = [jnp.zeros((1, 768), f32)]
        for l in range(1, DEPTH):
            rows.append(rows[-1] + p[l:l + 1, :])
        o_ref[...] = jnp.concatenate(rows, axis=0)

    return pl.pallas_call(body, name="lb_fwd", out_shape=jax.ShapeDtypeStruct((DEPTH, 768), f32))(logits)


def lb_bwd(logits, dlb):
    def body(x_ref, d_ref, o_ref):
        p = _lb_probs(x_ref[...])
        d = d_ref[...]
        rows = [jnp.zeros((1, 768), f32)] * DEPTH
        acc = jnp.zeros((1, 768), f32)
        for l in range(DEPTH - 1, 0, -1):
            acc = acc + d[l:l + 1, :]
            rows[l] = acc
        dp = jnp.concatenate(rows, axis=0)
        o_ref[...] = p * (dp - jnp.sum(p * dp, axis=0, keepdims=True))

    return pl.pallas_call(body, name="lb_bwd", out_shape=jax.ShapeDtypeStruct((DEPTH, 768), f32))(logits, dlb)


def sum_slots(x):
    n, rows, cols = x.shape
    tr = _rows_tile(rows)

    def body(x_ref, o_ref):
        acc = x_ref[0]
        for j in range(1, n):
            acc = acc + x_ref[j]
        o_ref[...] = acc

    return pl.pallas_call(
        body, name="sum_slots", grid=(rows // tr,),
        in_specs=[pl.BlockSpec((n, tr, cols), lambda i: (0, i, 0))], out_specs=pl.BlockSpec((tr, cols), lambda i: (i, 0)),
        out_shape=jax.ShapeDtypeStruct((rows, cols), f32), compiler_params=_cp(("parallel",)),
    )(x)


ANY = pl.BlockSpec(memory_space=pl.ANY)
CHIP_FLIPS = ((1, 0), (0, 1), (1, 1))
DEV_FLIPS = tuple((a, b, d) for a in (0, 1) for b in (0, 1) for d in (0, 1))[1:]


def _me():
    return lax.axis_index("x"), lax.axis_index("y"), lax.axis_index("c")


def _flip(v, f):
    return 1 - v if f else v


def _comm_call(body, name, ins, out_shapes, n_remote, n_local):
    return pl.pallas_call(
        body, name=name, in_specs=[ANY] * len(ins), out_specs=[ANY] * len(out_shapes), out_shape=out_shapes,
        scratch_shapes=[pltpu.SemaphoreType.DMA((n_remote,)), pltpu.SemaphoreType.DMA((n_remote,)),
                        pltpu.SemaphoreType.DMA((max(n_local, 1),))],
    )(*ins)


def run_plan(plan, name):
    ni, no = len(plan.ins), len(plan.outs)

    def body(*refs):
        ins, outs, sems = refs[:ni], refs[ni:ni + no], refs[ni + no:]
        plan.start(ins, outs, *sems)
        plan.wait(ins, outs, *sems)

    return pl.pallas_call(body, name=name, in_specs=[ANY] * ni, out_specs=[ANY] * no, out_shape=list(plan.outs),
                          scratch_shapes=plan.sems())(*plan.ins)


def gather_chips_plan(arrs, layer=None):
    n = len(arrs)
    layers = list(layer) if isinstance(layer, (list, tuple)) else [layer] * n
    shapes = [a.shape if l is None else a.shape[1:] for a, l in zip(arrs, layers)]

    def copies(ins, outs, send, recv, loc):
        x, y, c = _me()
        mine = 2 * x + y
        srcs = [r if l is None else r.at[l] for r, l in zip(ins, layers)]
        locs = [pltpu.make_async_copy(srcs[a], outs[a].at[mine], loc.at[a]) for a in range(n)]

        def remote(a, k, slot):
            fx, fy = CHIP_FLIPS[k]
            return pltpu.make_async_remote_copy(srcs[a], outs[a].at[slot], send.at[3 * a + k], recv.at[3 * a + k],
                                                device_id=(_flip(x, fx), _flip(y, fy), c), device_id_type=MESH)

        peers = [2 * _flip(x, fx) + _flip(y, fy) for fx, fy in CHIP_FLIPS]
        return locs, remote, mine, peers

    def start(ins, outs, send, recv, loc):
        locs, remote, mine, _ = copies(ins, outs, send, recv, loc)
        for cp in locs:
            cp.start()
        for a in range(n):
            for k in range(3):
                remote(a, k, mine).start()

    def wait(ins, outs, send, recv, loc):
        locs, remote, _, peers = copies(ins, outs, send, recv, loc)
        for a in range(n):
            for k in range(3):
                cp = remote(a, k, peers[k])
                cp.wait_recv()
                cp.wait_send()
        for cp in locs:
            cp.wait()

    outs = [jax.ShapeDtypeStruct((4,) + tuple(s), a.dtype) for s, a in zip(shapes, arrs)]
    return Plan(list(arrs), outs, 3 * n, n, start, wait)


def all_gather_chips(arrs, layer=None, name="ag4"):
    return run_plan(gather_chips_plan(arrs, layer), name)


def all_gather_devs(arr, name="ag8"):
    def body(in_ref, out_ref, send, recv, loc):
        x, y, c = _me()
        mine = 4 * x + 2 * y + c
        lc = pltpu.make_async_copy(in_ref, out_ref.at[mine], loc.at[0])
        lc.start()

        def remote(k, slot):
            fx, fy, fc = DEV_FLIPS[k]
            return pltpu.make_async_remote_copy(in_ref, out_ref.at[slot], send.at[k], recv.at[k],
                                                device_id=(_flip(x, fx), _flip(y, fy), _flip(c, fc)), device_id_type=MESH)

        for k in range(7):
            remote(k, mine).start()
        for k, (fx, fy, fc) in enumerate(DEV_FLIPS):
            cp = remote(k, 4 * _flip(x, fx) + 2 * _flip(y, fy) + _flip(c, fc))
            cp.wait_recv()
            cp.wait_send()
        lc.wait()

    return _comm_call(body, name, [arr], [jax.ShapeDtypeStruct((8,) + arr.shape, arr.dtype)], 7, 1)[0]


def _rows_of(which, rows):
    return pl.ds(pl.multiple_of(which * rows, 16), rows)


def dev_exchange_plan(parts):
    n = len(parts)

    def copies(ins, outs, send, recv, loc):
        x, y, c = _me()
        mine = 4 * x + 2 * y + c

        def piece(a, px, py, pc):
            rows = ins[a].shape[1] // 2
            return ins[a].at[2 * px + py, _rows_of(pc, rows), :]

        locs = [pltpu.make_async_copy(piece(a, x, y, c), outs[a].at[mine], loc.at[a]) for a in range(n)]

        def remote(a, k, slot):
            fx, fy, fc = DEV_FLIPS[k]
            px, py, pc = _flip(x, fx), _flip(y, fy), _flip(c, fc)
            return pltpu.make_async_remote_copy(piece(a, px, py, pc), outs[a].at[slot], send.at[7 * a + k], recv.at[7 * a + k],
                                                device_id=(px, py, pc), device_id_type=MESH)

        peers = [4 * _flip(x, fx) + 2 * _flip(y, fy) + _flip(c, fc) for fx, fy, fc in DEV_FLIPS]
        return locs, remote, mine, peers

    def start(ins, outs, send, recv, loc):
        locs, remote, mine, _ = copies(ins, outs, send, recv, loc)
        for cp in locs:
            cp.start()
        for a in range(n):
            for k in range(7):
                remote(a, k, mine).start()

    def wait(ins, outs, send, recv, loc):
        locs, remote, _, peers = copies(ins, outs, send, recv, loc)
        for a in range(n):
            for k in range(7):
                cp = remote(a, k, peers[k])
                cp.wait_recv()
                cp.wait_send()
        for cp in locs:
            cp.wait()

    outs = [jax.ShapeDtypeStruct((8, p.shape[1] // 2, p.shape[2]), p.dtype) for p in parts]
    return Plan(list(parts), outs, 7 * n, n, start, wait)


def sum_share(slots, name="rs_sum"):
    n, r, cols = slots.shape
    tr = _tile(r, (128, 176, 64))
    steps = r // tr

    def body(s_ref, g_ref, buf, send, loc, recv):
        i = pl.program_id(0)
        x, y, c = _me()
        slot = i % 2

        def copies(step, sl):
            rows = pl.ds(pl.multiple_of(c * r + step * tr, 8), tr)
            rem = pltpu.make_async_remote_copy(buf.at[sl], g_ref.at[rows, :], send.at[sl], recv.at[0],
                                               device_id=(x, y, 1 - c), device_id_type=MESH)
            return rem, pltpu.make_async_copy(buf.at[sl], g_ref.at[rows, :], loc.at[sl])

        @pl.when(i >= 2)
        def _():
            rem, lc = copies(i - 2, slot)
            rem.wait_send()
            lc.wait()

        acc = s_ref[0].astype(f32)
        for j in range(1, n):
            acc = acc + s_ref[j].astype(f32)
        buf[slot] = acc
        rem, lc = copies(i, slot)
        rem.start()
        lc.start()

        @pl.when(i == steps - 1)
        def _():
            for back in range(min(2, steps)):
                rem, lc = copies(i - back, (i - back) % 2)
                rem.wait_send()
                lc.wait()
            other = g_ref.at[pl.ds(pl.multiple_of((1 - c) * r, 8), r), :]
            pltpu.make_async_remote_copy(other, other, send.at[0], recv.at[0],
                                         device_id=(x, y, 1 - c), device_id_type=MESH).wait_recv()

    return pl.pallas_call(
        body, name=name, grid=(steps,),
        in_specs=[pl.BlockSpec((n, tr, cols), lambda i: (0, i, 0))], out_specs=ANY,
        out_shape=jax.ShapeDtypeStruct((2 * r, cols), f32),
        scratch_shapes=[pltpu.VMEM((2, tr, cols), f32), pltpu.SemaphoreType.DMA((2,)), pltpu.SemaphoreType.DMA((2,)),
                        pltpu.SemaphoreType.DMA((1,))],
        compiler_params=_cp(("arbitrary",)),
    )(slots)


BIG = ("ffn1_w_in", "ffn1_w_out", "w_in", "w_branch_a", "w_branch_b", "w_branch_c", "w_out", "ffn2_w_in", "ffn2_w_out")
ROW_SHARDED = ("ffn1_w_out", "w_out", "ffn2_w_out")
RES_W = (0.5, 1.0, 0.5)


def _full_weight(name, g):
    if name in ROW_SHARDED:
        return g.reshape(4 * g.shape[1], g.shape[2])
    return jnp.concatenate([g[0], g[1], g[2], g[3]], axis=1)


def _by_shard(name, dw):
    if name in ROW_SHARDED:
        return dw.reshape(4, dw.shape[0] // 4, dw.shape[1])
    return dw.reshape(dw.shape[0], 4, dw.shape[1] // 4).transpose(1, 0, 2)


def _full_weight_t(name, g):
    if name in ROW_SHARDED:
        return g.reshape(4 * g.shape[1], g.shape[2]).T
    return g.transpose(0, 2, 1).reshape(4 * g.shape[2], g.shape[1])


def _ffn_fwd(x, w_in, w_out, a_vec, sh_vec, b_vec, plans=(None, None)):
    h, h_t = prenorm(x, a_vec, sh_vec)
    (ua, ub, s, s_t), side0 = ffn_in_swiglu(h, w_in, side=plans[0])
    y = mm(s, w_out, name="ffn_out", side=plans[1])
    side1 = None
    if plans[1] is not None:
        y, side1 = y
    return postnorm(x, y, b_vec), (x, h_t, ua, ub, s_t, y), (side0, side1)


def _ffn_bwd(dout, saved, w_in_t, w_out_t, a_vec, b_vec, plans=None):
    x, h_t, ua, ub, s_t, y = saved
    riders = plans or (None, None, None)
    dy, db = post_bwd(dout, y, b_vec)
    dw_out = mm(s_t, dy, out_dtype=bf16, name="ffn_dwo", side=riders[0])
    du = ffn_du(dy, w_out_t, ua, ub)
    dh = mm(du, w_in_t, name="ffn_dh", side=riders[1])
    sides = []
    if plans:
        (dw_out, s0), (dh, s1) = dw_out, dh
        sides = [s0, s1]
    dw_in = mm(h_t, du, out_dtype=bf16, name="ffn_dwi", side=riders[2])
    if plans:
        dw_in, s2 = dw_in
        sides.append(s2)
    dx, dsh, da = pre_bwd(dout, dh, x, a_vec)
    return (dx, dw_in, dw_out, dsh, da, db) + ((sides,) if plans else ())


def _mix_fwd(x, w, lb, ng, a_vec, sh_vec, b_vec, plans=(None,) * 6):
    h, h_t = prenorm(x, a_vec, sh_vec)
    u = mm(h, w["w_in"], name="mix_in", side=plans[0])
    side0 = None
    if plans[0] is not None:
        u, side0 = u
    (o, ya, st), side1 = hgrn_fwd(u, lb, ng, side=plans[1])
    yb, side2 = sb_fwd(u, side=plans[2])
    groups, dil_sides = [], []
    for g in range(3):
        og, sg = dil_fwd(u, g, side=plans[3 + g])
        groups.append(og)
        dil_sides.append(sg)
    yc, lse = dil_merge([o_ for o_, _ in groups], [l_ for _, l_ in groups])
    pa = mm(ya, w["w_branch_a"], name="mix_pa")
    pb = mm(yb, w["w_branch_b"], name="mix_pb")
    pc = mm(yc, w["w_branch_c"], name="mix_pc")
    merged, merged_t = gate_merge(u, pa, pb, pc)
    z = mm(merged, w["w_out"], name="mix_out")
    return (postnorm(x, z, b_vec), (x, h_t, u, o, ya, st, yb, yc, lse, pa, pb, pc, merged_t, z),
            (side0, side1, side2, *dil_sides))


def _mix_bwd(dout, saved, wt, lb, ng, a_vec, b_vec, plans=(None,) * 3):
    x, h_t, u, o, ya, st, yb, yc, lse, pa, pb, pc, merged_t, z = saved
    dz, db = post_bwd(dout, z, b_vec)
    dmerged = mm(dz, wt["w_out"], name="mix_dm")
    dw_out = mm(merged_t, dz, out_dtype=bf16, name="mix_dwo")
    dpa, dpb, dpc, dg0, dg1, dg2 = gate_bwd(dmerged, u, pa, pb, pc)
    dya = mm(dpa, wt["w_branch_a"], name="mix_dya")
    dyb = mm(dpb, wt["w_branch_b"], name="mix_dyb")
    dyc = mm(dpc, wt["w_branch_c"], name="mix_dyc")
    dw_a = mm(ya.T, dpa, out_dtype=bf16, name="mix_dwa")
    dw_b = mm(yb.astype(bf16).T, dpb, out_dtype=bf16, name="mix_dwb")
    dw_c = mm(yc.astype(bf16).T, dpc, out_dtype=bf16, name="mix_dwc")
    (daq, daf, dai, dag, dlb, dng), side0 = hgrn_bwd(u, lb, ng, o, st, dya, side=plans[0])
    (dbq, dbk, dbv), side1 = sb_bwd(u, yb, dyb, side=plans[1])
    dc = [dil_bwd(u, g, dyc, yc, lse) for g in range(3)]
    du = jnp.concatenate(
        [daq, daf, dai, dag] + [t.astype(bf16) for t in (dbq, dbk, dbv)]
        + [dc[g][j].astype(bf16) for j in range(3) for g in range(3)] + [dg0, dg1, dg2], axis=1)
    dh = mm(du, wt["w_in"], name="mix_dh")
    dw_in = mm(h_t, du, out_dtype=bf16, name="mix_dwi", side=plans[2])
    side2 = None
    if plans[2] is not None:
        dw_in, side2 = dw_in
    dx, dsh, da = pre_bwd(dout, dh, x, a_vec)
    grads = {"w_in": dw_in, "w_out": dw_out, "w_branch_a": dw_a, "w_branch_b": dw_b, "w_branch_c": dw_c}
    return dx, grads, dlb, jnp.sum(dng, axis=0), dsh, da, db, (side0, side1, side2)


NEXT_RIDERS = (("ffn1_w_out",), (), ("ffn2_w_in",), ("ffn1_w_in",), ("w_in",), (), (),
               ("w_out", "w_branch_a", "w_branch_b", "w_branch_c"), ("ffn2_w_out",), ())
FIRST = ("ffn1_w_in", "ffn1_w_out", "w_in", "w_out", "w_branch_a", "w_branch_b", "w_branch_c")
LATE_RIDERS = ((), (), (), (), (), ("ffn2_w_in",), ("ffn2_w_out",), (), (), ())
BWD_RIDERS = (("ffn1_w_in", "ffn2_w_out", "w_branch_a", "w_branch_b", "w_branch_c"), ("w_in", "ffn1_w_out", "w_out"),
              ("ffn2_w_in",))
LAST_RIDERS = (("ffn2_w_out", "w_out", "w_branch_a", "w_branch_b", "w_branch_c"), ("ffn2_w_in",), ("w_in",))
TAIL = ("ffn1_w_in", "ffn1_w_out")


def kernel(x, c, w_ada, b_ada, norm_g, ffn1_w_in, ffn1_w_out, w_in, hgrn_lb_logits, hgrn_norm_g, w_branch_a, w_branch_b, w_branch_c, w_out, ffn2_w_in, ffn2_w_out, loss_target, m_w_ada, m_b_ada, m_norm_g, m_ffn1_w_in, m_ffn1_w_out, m_w_in, m_hgrn_lb_logits, m_hgrn_norm_g, m_w_branch_a, m_w_branch_b, m_w_branch_c, m_w_out, m_ffn2_w_in, m_ffn2_w_out, v_w_ada, v_b_ada, v_norm_g, v_ffn1_w_in, v_ffn1_w_out, v_w_in, v_hgrn_lb_logits, v_hgrn_norm_g, v_w_branch_a, v_w_branch_b, v_w_branch_c, v_w_out, v_ffn2_w_in, v_ffn2_w_out):
    weights = dict(w_ada=w_ada, b_ada=b_ada, norm_g=norm_g, ffn1_w_in=ffn1_w_in, ffn1_w_out=ffn1_w_out, w_in=w_in,
                   hgrn_lb_logits=hgrn_lb_logits, hgrn_norm_g=hgrn_norm_g, w_branch_a=w_branch_a, w_branch_b=w_branch_b,
                   w_branch_c=w_branch_c, w_out=w_out, ffn2_w_in=ffn2_w_in, ffn2_w_out=ffn2_w_out)
    mom = dict(w_ada=m_w_ada, b_ada=m_b_ada, norm_g=m_norm_g, ffn1_w_in=m_ffn1_w_in, ffn1_w_out=m_ffn1_w_out, w_in=m_w_in,
               hgrn_lb_logits=m_hgrn_lb_logits, hgrn_norm_g=m_hgrn_norm_g, w_branch_a=m_w_branch_a, w_branch_b=m_w_branch_b,
               w_branch_c=m_w_branch_c, w_out=m_w_out, ffn2_w_in=m_ffn2_w_in, ffn2_w_out=m_ffn2_w_out)
    var = dict(w_ada=v_w_ada, b_ada=v_b_ada, norm_g=v_norm_g, ffn1_w_in=v_ffn1_w_in, ffn1_w_out=v_ffn1_w_out, w_in=v_w_in,
               hgrn_lb_logits=v_hgrn_lb_logits, hgrn_norm_g=v_hgrn_norm_g, w_branch_a=v_w_branch_a, w_branch_b=v_w_branch_b,
               w_branch_c=v_w_branch_c, w_out=v_w_out, ffn2_w_in=v_ffn2_w_in, ffn2_w_out=v_ffn2_w_out)
    order = list(weights)
    xi, yi, ci = _me()
    chip = 2 * xi + yi
    dev = 4 * xi + 2 * yi + ci
    xs = x[0]

    c_all = all_gather_devs(c, name="ag8_c").reshape(8, D)
    mod_sh = all_gather_chips([ada_fwd(c_all, w_ada)], name="ag4_mod")[0]
    mod_all = mod_sh.transpose(1, 2, 0, 3).reshape(DEPTH, 8, 9 * D)
    mod = lax.dynamic_index_in_dim(mod_all, dev, axis=1, keepdims=False) + b_ada
    mod = mod.reshape(DEPTH, 3, 3, D)
    ng_all = all_gather_chips([norm_g.reshape(DEPTH * 6, D // 4)], name="ag4_norm")[0]
    ng_all = ng_all.reshape(4, DEPTH, 6, D // 4).transpose(1, 2, 0, 3).reshape(DEPTH, 6, D)
    lb_all = lb_fwd(hgrn_lb_logits)
    w16 = {n: cast_bf16(weights[n]) for n in BIG}

    def vecs(l, i):
        shift, scale, gate = mod[l, i, 0][None], mod[l, i, 1][None], mod[l, i, 2][None]
        g_pre, g_post = ng_all[l, 2 * i][None], ng_all[l, 2 * i + 1][None]
        return g_pre * (1.0 + scale), shift, RES_W[i] * gate * g_post

    saved, full = [], []
    gathered = dict(zip(FIRST, all_gather_chips([w16[n] for n in FIRST], layer=0, name="ag4_w0")))
    for l in range(DEPTH):
        riders = [[(n, l) for n in (LATE_RIDERS[h] if l == 0 else ())]
                  + [(n, l + 1) for n in (NEXT_RIDERS[h] if l + 1 < DEPTH else ())] for h in range(len(NEXT_RIDERS))]
        plans = [gather_chips_plan([w16[n] for n, _ in it], layer=[ll for _, ll in it]) if it else None for it in riders]
        coming = {}

        def absorb(hosts, sides):
            for h, outs in zip(hosts, sides):
                for (n, ll), g in zip(riders[h], outs or ()):
                    (gathered if ll == l else coming)[n] = g

        full_w = lambda n: _full_weight(n, gathered[n])
        lb, ng = lb_all[l][None], hgrn_norm_g[l][None]
        xs, s1, sides = _ffn_fwd(xs, full_w("ffn1_w_in"), full_w("ffn1_w_out"), *vecs(l, 0), plans=plans[0:2])
        absorb((0, 1), sides)
        w = {n: full_w(n) for n in ("w_in", "w_branch_a", "w_branch_b", "w_branch_c", "w_out")}
        xs, s2, sides = _mix_fwd(xs, w, lb, ng, *vecs(l, 1), plans=plans[2:8])
        absorb(range(2, 8), sides)
        xs, s3, sides = _ffn_fwd(xs, full_w("ffn2_w_in"), full_w("ffn2_w_out"), *vecs(l, 2), plans=plans[8:10])
        absorb((8, 9), sides)
        saved.append((s1, s2, s3))
        full.append({n: _full_weight_t(n, gathered[n]) for n in BIG})
        gathered = coming

    dx, loss_part = loss_grad(xs, loss_target[0])
    loss = lax.psum(loss_part[0, 0], ("x", "y", "c"))

    big_grads = {n: [None] * DEPTH for n in BIG}
    d_mod, d_ng, d_lb, d_hng = [None] * DEPTH, [None] * DEPTH, [None] * DEPTH, [None] * DEPTH
    pending = None
    for l in reversed(range(DEPTH)):
        wt = full[l]
        s1, s2, s3 = saved[l]
        lb, ng = lb_all[l][None], hgrn_norm_g[l][None]
        rows_mod, rows_ng = [None] * 9, [None] * 6

        def vec_grads(i, dsh, da, db):
            scale, gate = mod[l, i, 1][None], mod[l, i, 2][None]
            g_pre, g_post = ng_all[l, 2 * i][None], ng_all[l, 2 * i + 1][None]
            rows_mod[3 * i], rows_mod[3 * i + 1], rows_mod[3 * i + 2] = dsh, g_pre * da, RES_W[i] * g_post * db
            rows_ng[2 * i], rows_ng[2 * i + 1] = (1.0 + scale) * da, RES_W[i] * gate * db

        a3, _, b3 = vecs(l, 2)
        dx, dwi, dwo, dsh, da, db = _ffn_bwd(dx, s3, wt["ffn2_w_in"], wt["ffn2_w_out"], a3, b3)
        vec_grads(2, dsh, da, db)
        grads = {"ffn2_w_in": dwi, "ffn2_w_out": dwo}
        a2, _, b2 = vecs(l, 1)
        plans = (None,) * len(BWD_RIDERS)
        if pending is not None:
            plans = tuple(dev_exchange_plan([pending[n] for n in names]) for names in BWD_RIDERS)
        dx, gmix, dlb, dhng, dsh, da, db, sides = _mix_bwd(dx, s2, wt, lb, ng, a2, b2, plans=plans)
        if pending is not None:
            for names, outs in zip(BWD_RIDERS, sides):
                for n, slots in zip(names, outs):
                    big_grads[n][l + 1] = sum_share(slots)
        vec_grads(1, dsh, da, db)
        grads.update(gmix)
        a1, _, b1 = vecs(l, 0)
        if l > 0:
            dx, dwi, dwo, dsh, da, db = _ffn_bwd(dx, s1, wt["ffn1_w_in"], wt["ffn1_w_out"], a1, b1)
        else:
            ready = {n: _by_shard(n, grads[n]) for names in LAST_RIDERS for n in names}
            plans = tuple(dev_exchange_plan([ready[n] for n in names]) for names in LAST_RIDERS)
            dx, dwi, dwo, dsh, da, db, sides = _ffn_bwd(dx, s1, wt["ffn1_w_in"], wt["ffn1_w_out"], a1, b1, plans=plans)
            for names, outs in zip(LAST_RIDERS, sides):
                for n, slots in zip(names, outs):
                    big_grads[n][0] = sum_share(slots)
        vec_grads(0, dsh, da, db)
        grads.update({"ffn1_w_in": dwi, "ffn1_w_out": dwo})
        pending = {n: _by_shard(n, grads[n]) for n in (BIG if l > 0 else TAIL)}
        d_mod[l] = jnp.concatenate(rows_mod, axis=1)
        d_ng[l] = jnp.concatenate(rows_ng, axis=0)
        d_lb[l], d_hng[l] = dlb, dhng

    n_small = 6 * D * DEPTH + 768 * DEPTH + A_V * DEPTH + 9 * D * DEPTH
    pad = -n_small % (512 * LANE)
    flat = jnp.concatenate([jnp.stack(d_ng).reshape(-1), jnp.concatenate(d_lb, axis=0).reshape(-1),
                            jnp.concatenate(d_hng, axis=0).reshape(-1), jnp.concatenate(d_mod, axis=0).reshape(-1),
                            jnp.zeros((pad,), f32)])
    small_all = all_gather_devs(flat.reshape(-1, LANE), name="ag8_small")
    total = sum_slots(small_all).reshape(-1)
    o1 = 6 * D * DEPTH
    o2 = o1 + 768 * DEPTH
    o3 = o2 + A_V * DEPTH
    g_ng_full = total[:o1].reshape(DEPTH, 6, D)
    g_lb_all = total[o1:o2].reshape(DEPTH, 768)
    g_small = {
        "norm_g": lax.dynamic_slice_in_dim(g_ng_full, chip * (D // 4), D // 4, axis=2),
        "hgrn_lb_logits": lb_bwd(hgrn_lb_logits, g_lb_all),
        "hgrn_norm_g": total[o2:o3].reshape(DEPTH, A_V),
        "b_ada": total[o3:n_small].reshape(DEPTH, 9 * D),
    }
    dmod_all = small_all.reshape(8, -1)[:, o3:n_small].reshape(8, DEPTH, 9 * D).transpose(1, 0, 2)
    dm_sh = lax.dynamic_slice_in_dim(dmod_all, chip * ADA_N, ADA_N, axis=2)

    out_g, out_d, out_m, out_v = {}, {}, {}, {}
    ada_outs, slots = ada_bwd_adam(c_all, dm_sh, w_ada, m_w_ada, v_w_ada, side=dev_exchange_plan([pending[n] for n in TAIL]))
    out_g["w_ada"], out_d["w_ada"], out_m["w_ada"], out_v["w_ada"] = ada_outs
    for n, s in zip(TAIL, slots):
        big_grads[n][0] = sum_share(s)
    for n in BIG:
        out_g[n], out_d[n], out_m[n], out_v[n] = adam_layers(weights[n], big_grads[n], mom[n], var[n])
    out_g.update(g_small)
    for n in g_small:
        out_d[n], out_m[n], out_v[n] = adam(weights[n], out_g[n], mom[n], var[n])
    return (loss, dx[None], *[out_g[n] for n in order], *[out_d[n] for n in order],
            *[out_m[n] for n in order], *[out_v[n] for n in order])
```

```python
import functools
import math

import jax
import jax.numpy as jnp
from jax import lax
from jax.experimental import pallas as pl
from jax.experimental.pallas import tpu as pltpu

f32, bf16 = jnp.float32, jnp.bfloat16

D = 1024
DEPTH = 4
D_FF = 2816
EPS = 1e-6
NEG_BIG = -1e30
TINY = 1e-30
A_HEADS, A_K, A_V, A_CHUNK = 6, 128, 64, 64
A_SUB = 16
A_CLAMP = 80.0
B_HEADS, HD = 6, 64
C_GROUPS = ((128, 1), (512, 4), (2048, 16))
C_BLK = 128
IN_COLS = 8832
O_AQ, O_AF, O_AI, O_AG = 0, 768, 1536, 1920
O_BQ, O_BK, O_BV = 2304, 2688, 3072
O_CQ, O_CK, O_CV = 3456, 4224, 4992
O_GATE = 5760
LANE = 128
ADAM_LR, ADAM_B1, ADAM_B2, ADAM_EPS, ADAM_WD, ADAM_STEP = 0.001, 0.9, 0.999, 1e-08, 0.01, 10
MESH = pl.DeviceIdType.MESH
VMEM_LIMIT = 56 * 1024 * 1024


def _alibi_slopes(n):
    def pow2(m):
        start = 2.0 ** (-8.0 / m)
        return [start ** (i + 1) for i in range(m)]
    if math.log2(n).is_integer():
        s = pow2(n)
    else:
        c = 2 ** int(math.floor(math.log2(n)))
        s = pow2(c) + pow2(2 * c)[0::2][: n - c]
    return sorted(s, reverse=True)


C_SLOPES = _alibi_slopes(12)


def _tile(n, prefs):
    for p in prefs:
        if n % p == 0:
            return p
    return n


def _cp(sem):
    return pltpu.CompilerParams(dimension_semantics=sem, vmem_limit_bytes=VMEM_LIMIT)


def _sig(x):
    return 1.0 / (1.0 + jnp.exp(-x))


def _dot(a, b, dn, precision=None):
    return lax.dot_general(a, b, (dn, ((), ())), preferred_element_type=f32, precision=precision)


NN = ((1,), (0,))
NT = ((1,), (1,))
TN = ((0,), (0,))


class Plan:
    def __init__(self, ins, outs, n_remote, n_local, start, wait):
        self.ins, self.outs, self.n_remote, self.n_local, self.start, self.wait = ins, outs, n_remote, n_local, start, wait

    def sems(self):
        return [pltpu.SemaphoreType.DMA((self.n_remote,)), pltpu.SemaphoreType.DMA((self.n_remote,)),
                pltpu.SemaphoreType.DMA((max(self.n_local, 1),))]


def _call(body, *, name, grid, in_specs, out_specs, out_shape, sem, args, scratch_shapes=(), side=None):
    if side is None:
        return pl.pallas_call(body, name=name, grid=grid, in_specs=in_specs, out_specs=out_specs, out_shape=out_shape,
                              scratch_shapes=list(scratch_shapes), compiler_params=_cp(sem))(*args), None
    any_spec = pl.BlockSpec(memory_space=pl.ANY)
    n_in, n_out, n_scr = len(in_specs), len(out_specs), len(scratch_shapes)
    s_in, s_out = len(side.ins), len(side.outs)

    def hosted(*refs):
        ins, rest = refs[:n_in], refs[n_in:]
        sins, rest = rest[:s_in], rest[s_in:]
        outs, rest = rest[:n_out], rest[n_out:]
        souts, rest = rest[:s_out], rest[s_out:]
        scr, sems = rest[:n_scr], rest[n_scr:]
        pids = [pl.program_id(d) for d in range(len(grid))]
        first = functools.reduce(jnp.logical_and, [p == 0 for p in pids])
        last = functools.reduce(jnp.logical_and, [p == g - 1 for p, g in zip(pids, grid)])

        @pl.when(first)
        def _():
            side.start(sins, souts, *sems)

        body(*ins, *outs, *scr)

        @pl.when(last)
        def _():
            side.wait(sins, souts, *sems)

    res = pl.pallas_call(
        hosted, name=name, grid=grid, in_specs=list(in_specs) + [any_spec] * s_in,
        out_specs=list(out_specs) + [any_spec] * s_out, out_shape=list(out_shape) + list(side.outs),
        scratch_shapes=list(scratch_shapes) + side.sems(), compiler_params=_cp(("arbitrary",) * len(grid)),
    )(*args, *side.ins)
    return res[:n_out], res[n_out:]


MM_TILES = {
    (4096, 1024, 2816): (1024, 512, 2816),
    (4096, 1024, 5632): (512, 512, 5632),
    (1024, 5632, 4096): (512, 512, 4096),
    (2816, 1024, 4096): (704, 512, 4096),
    (4096, 8832, 1024): (1024, 2944, 1024),
    (4096, 1024, 8832): (1024, 512, 2944),
    (1024, 8832, 4096): (512, 2944, 1024),
    (1024, 1024, 4096): (512, 512, 4096),
    (384, 1024, 4096): (384, 512, 4096),
    (256, 1024, 4096): (256, 512, 4096),
}


def mm(a, b, *, out_dtype=f32, name="mm", side=None):
    M, K = a.shape
    K2, N = b.shape
    assert K == K2, (a.shape, b.shape)
    tm, tn, tk = MM_TILES.get((M, N, K), (_tile(M, (1024, 704, 512, 384, 256, 128)), _tile(N, (512, 384, 256, 128)),
                                          _tile(K, (1024, 512, 1408, 384, 256, 128))))
    nk = K // tk

    def body(a_ref, b_ref, o_ref, *acc):
        p = _dot(a_ref[...].astype(bf16), b_ref[...].astype(bf16), NN)
        if nk == 1:
            o_ref[...] = p.astype(out_dtype)
            return
        acc_ref, = acc
        k = pl.program_id(2)

        @pl.when(k == 0)
        def _():
            acc_ref[...] = p

        @pl.when(k > 0)
        def _():
            acc_ref[...] += p

        @pl.when(k == nk - 1)
        def _():
            o_ref[...] = acc_ref[...].astype(out_dtype)

    outs, souts = _call(
        body, name=name, grid=(M // tm, N // tn, nk),
        in_specs=[pl.BlockSpec((tm, tk), lambda i, j, k: (i, k)), pl.BlockSpec((tk, tn), lambda i, j, k: (k, j))],
        out_specs=[pl.BlockSpec((tm, tn), lambda i, j, k: (i, j))],
        out_shape=[jax.ShapeDtypeStruct((M, N), out_dtype)],
        scratch_shapes=[pltpu.VMEM((tm, tn), f32)] if nk > 1 else [],
        sem=("parallel", "parallel", "arbitrary"), args=(a, b), side=side)
    return outs[0] if side is None else (outs[0], souts)


FF_TM = 1024
FF_T = 256
FF_NB = D_FF // FF_T


def ffn_in_swiglu(h, w_in, side=None):
    S = h.shape[0]

    def body(h_ref, wa_ref, wb_ref, a_ref, b_ref, s_ref, st_ref):
        hv = h_ref[...]
        a = _dot(hv, wa_ref[...], NN)
        b = _dot(hv, wb_ref[...], NN)
        a_ref[...] = a.astype(bf16)
        b_ref[...] = b.astype(bf16)
        s = (a * _sig(a) * b).astype(bf16)
        s_ref[...] = s
        st_ref[...] = s.T

    ospec = pl.BlockSpec((FF_TM, FF_T), lambda i, j: (i, j))
    osd = jax.ShapeDtypeStruct((S, D_FF), bf16)
    return _call(
        body, name="ffn_in", grid=(S // FF_TM, FF_NB),
        in_specs=[pl.BlockSpec((FF_TM, D), lambda i, j: (i, 0)), pl.BlockSpec((D, FF_T), lambda i, j: (0, j)),
                  pl.BlockSpec((D, FF_T), lambda i, j: (0, j + FF_NB))],
        out_specs=[ospec] * 3 + [pl.BlockSpec((FF_T, FF_TM), lambda i, j: (j, i))],
        out_shape=[osd] * 3 + [jax.ShapeDtypeStruct((D_FF, S), bf16)], sem=("parallel", "parallel"),
        args=(h, w_in, w_in), side=side)


def ffn_du(dy, w_out_t, ua, ub):
    S = dy.shape[0]
    tm = 512

    def body(dy_ref, w_ref, a_ref, b_ref, du_ref):
        dyv = dy_ref[...]
        for j in range(FF_NB):
            cols = slice(j * FF_T, (j + 1) * FF_T)
            ds = _dot(dyv, w_ref[:, cols], NN)
            a, b = a_ref[:, cols].astype(f32), b_ref[:, cols].astype(f32)
            sg = _sig(a)
            du_ref[:, cols] = (ds * b * sg * (1.0 + a * (1.0 - sg))).astype(bf16)
            du_ref[:, D_FF + j * FF_T:D_FF + (j + 1) * FF_T] = (ds * a * sg).astype(bf16)

    half = pl.BlockSpec((tm, D_FF), lambda i: (i, 0))
    return pl.pallas_call(
        body, name="ffn_du", grid=(S // tm,),
        in_specs=[pl.BlockSpec((tm, D), lambda i: (i, 0)), pl.BlockSpec((D, D_FF), lambda i: (0, 0)), half, half],
        out_specs=pl.BlockSpec((tm, 2 * D_FF), lambda i: (i, 0)),
        out_shape=jax.ShapeDtypeStruct((S, 2 * D_FF), bf16), compiler_params=_cp(("parallel",)),
    )(dy, w_out_t, ua, ub)


TR = 512


def _row_spec(cols=D):
    return pl.BlockSpec((TR, cols), lambda i: (i, 0))


def _vec_spec(cols=D):
    return pl.BlockSpec((1, cols), lambda i: (0, 0))


def prenorm(x, a_vec, sh_vec):
    S = x.shape[0]

    def body(x_ref, a_ref, s_ref, h_ref, ht_ref):
        xv = x_ref[...]
        rstd = lax.rsqrt(jnp.mean(xv * xv, axis=1, keepdims=True) + EPS)
        h = (xv * rstd * a_ref[...] + s_ref[...]).astype(bf16)
        h_ref[...] = h
        ht_ref[...] = h.T

    return pl.pallas_call(
        body, name="prenorm", grid=(S // TR,),
        in_specs=[_row_spec(), _vec_spec(), _vec_spec()],
        out_specs=[_row_spec(), pl.BlockSpec((D, TR), lambda i: (0, i))],
        out_shape=[jax.ShapeDtypeStruct((S, D), bf16), jax.ShapeDtypeStruct((D, S), bf16)],
        compiler_params=_cp(("parallel",)),
    )(x, a_vec, sh_vec)


def postnorm(x, y, b_vec):
    S = x.shape[0]

    def body(x_ref, y_ref, b_ref, o_ref):
        yv = y_ref[...]
        rstd = lax.rsqrt(jnp.mean(yv * yv, axis=1, keepdims=True) + EPS)
        o_ref[...] = x_ref[...] + b_ref[...] * (yv * rstd)

    return pl.pallas_call(
        body, name="postnorm", grid=(S // TR,),
        in_specs=[_row_spec(), _row_spec(), _vec_spec()], out_specs=_row_spec(),
        out_shape=jax.ShapeDtypeStruct((S, D), f32), compiler_params=_cp(("parallel",)),
    )(x, y, b_vec)


def post_bwd(dout, y, b_vec):
    S = dout.shape[0]

    def body(d_ref, y_ref, b_ref, dy_ref, db_ref):
        i = pl.program_id(0)
        yv, dv = y_ref[...], d_ref[...]
        rstd = lax.rsqrt(jnp.mean(yv * yv, axis=1, keepdims=True) + EPS)
        yh = yv * rstd
        dyh = dv * b_ref[...]
        dy_ref[...] = (rstd * (dyh - yh * jnp.mean(dyh * yh, axis=1, keepdims=True))).astype(bf16)
        part = jnp.sum(dv * yh, axis=0, keepdims=True)

        @pl.when(i == 0)
        def _():
            db_ref[...] = part

        @pl.when(i > 0)
        def _():
            db_ref[...] += part

    return pl.pallas_call(
        body, name="post_bwd", grid=(S // TR,),
        in_specs=[_row_spec(), _row_spec(), _vec_spec()], out_specs=[_row_spec(), _vec_spec()],
        out_shape=[jax.ShapeDtypeStruct((S, D), bf16), jax.ShapeDtypeStruct((1, D), f32)],
        compiler_params=_cp(("arbitrary",)),
    )(dout, y, b_vec)


def pre_bwd(dout, dh, x, a_vec):
    S = dout.shape[0]

    def body(d_ref, dh_ref, x_ref, a_ref, dx_ref, ds_ref, da_ref):
        i = pl.program_id(0)
        xv, dhv = x_ref[...], dh_ref[...]
        rstd = lax.rsqrt(jnp.mean(xv * xv, axis=1, keepdims=True) + EPS)
        n1 = xv * rstd
        dn = dhv * a_ref[...]
        dx_ref[...] = d_ref[...] + rstd * (dn - n1 * jnp.mean(dn * n1, axis=1, keepdims=True))
        p_s = jnp.sum(dhv, axis=0, keepdims=True)
        p_a = jnp.sum(dhv * n1, axis=0, keepdims=True)

        @pl.when(i == 0)
        def _():
            ds_ref[...] = p_s
            da_ref[...] = p_a

        @pl.when(i > 0)
        def _():
            ds_ref[...] += p_s
            da_ref[...] += p_a

    return pl.pallas_call(
        body, name="pre_bwd", grid=(S // TR,),
        in_specs=[_row_spec(), _row_spec(), _row_spec(), _vec_spec()],
        out_specs=[_row_spec(), _vec_spec(), _vec_spec()],
        out_shape=[jax.ShapeDtypeStruct((S, D), f32), jax.ShapeDtypeStruct((1, D), f32), jax.ShapeDtypeStruct((1, D), f32)],
        compiler_params=_cp(("arbitrary",)),
    )(dout, dh, x, a_vec)


def loss_grad(y, tgt):
    S = y.shape[0]

    def body(y_ref, t_ref, dy_ref, l_ref):
        i = pl.program_id(0)
        e = y_ref[...] - t_ref[...]
        dy_ref[...] = e * (1.0 / D)
        part = jnp.sum(jnp.sum(e * e, axis=1, keepdims=True), axis=0, keepdims=True) * (0.5 / D)
        part = jnp.broadcast_to(part, (8, LANE))

        @pl.when(i == 0)
        def _():
            l_ref[...] = part

        @pl.when(i > 0)
        def _():
            l_ref[...] += part

    return pl.pallas_call(
        body, name="loss_grad", grid=(S // TR,),
        in_specs=[_row_spec(), _row_spec()],
        out_specs=[_row_spec(), pl.BlockSpec((8, LANE), lambda i: (0, 0))],
        out_shape=[jax.ShapeDtypeStruct((S, D), f32), jax.ShapeDtypeStruct((8, LANE), f32)],
        compiler_params=_cp(("arbitrary",)),
    )(y, tgt)


G_NB = D // LANE
G_TR = 2048
G_OFF = O_GATE // LANE


def gate_merge(u, pa, pb, pc):
    S = u.shape[0]

    def body(g0, g1, g2, a, b, c, o_ref, ot_ref):
        m = (_sig(g0[...]) * a[...] + _sig(g1[...]) * b[...] + _sig(g2[...]) * c[...]).astype(bf16)
        o_ref[...] = m
        ot_ref[...] = m.T

    gs = [pl.BlockSpec((G_TR, LANE), functools.partial(lambda i, j, k: (i, G_OFF + G_NB * k + j), k=k)) for k in range(3)]
    ps = pl.BlockSpec((G_TR, LANE), lambda i, j: (i, j))
    return pl.pallas_call(
        body, name="gate_merge", grid=(S // G_TR, G_NB),
        in_specs=gs + [ps, ps, ps], out_specs=[ps, pl.BlockSpec((LANE, G_TR), lambda i, j: (j, i))],
        out_shape=[jax.ShapeDtypeStruct((S, D), bf16), jax.ShapeDtypeStruct((D, S), bf16)],
        compiler_params=_cp(("parallel", "parallel")),
    )(u, u, u, pa, pb, pc)


def gate_bwd(dm, u, pa, pb, pc):
    S = u.shape[0]

    def body(dm_ref, g0, g1, g2, a, b, c, da, db, dc, dg0, dg1, dg2):
        d = dm_ref[...]
        for g, p, dp, dg in ((g0, a, da, dg0), (g1, b, db, dg1), (g2, c, dc, dg2)):
            s = _sig(g[...])
            dp[...] = (d * s).astype(bf16)
            dg[...] = (d * p[...] * s * (1.0 - s)).astype(bf16)

    gs = [pl.BlockSpec((G_TR, LANE), functools.partial(lambda i, j, k: (i, G_OFF + G_NB * k + j), k=k)) for k in range(3)]
    ps = pl.BlockSpec((G_TR, LANE), lambda i, j: (i, j))
    osd = jax.ShapeDtypeStruct((S, D), bf16)
    return pl.pallas_call(
        body, name="gate_bwd", grid=(S // G_TR, G_NB),
        in_specs=[ps] + gs + [ps, ps, ps], out_specs=[ps] * 6, out_shape=[osd] * 6,
        compiler_params=_cp(("parallel", "parallel")),
    )(dm, u, u, u, pa, pb, pc)


A_TB = 512
A_NCH = A_TB // A_CHUNK
A_NSUB = A_CHUNK // A_SUB
A_HP = 6
A_KW, A_VW = A_HP * A_K, A_HP * A_V


def _hgrn_gates(qr, fr, lbh):
    sq = _sig(qr)
    sig = _sig(fr)
    f = lbh + (1.0 - lbh) * sig
    logf = jnp.log(jnp.maximum(f, TINY))
    k = (1.0 - lbh) * (1.0 - sig)
    return qr * sq, sq, sig, f, logf, k


def _hgrn_intra(qf, k, b, causal):
    qts, kts, eqs, eks, blocks = [], [], [], [], []
    for sb in range(A_NSUB):
        rs = sb * A_SUB
        r = b[rs - 1:rs, :] if sb else jnp.zeros((1, A_K), f32)
        eq = jnp.exp(b[rs:rs + A_SUB, :] - r)
        ek = jnp.exp(jnp.minimum(r - b, A_CLAMP))
        qt = (qf[rs:rs + A_SUB, :] * eq).astype(bf16)
        kt = (k * ek).astype(bf16)
        blocks.append(_dot(qt, kt, NT))
        qts.append(qt), kts.append(kt), eqs.append(eq), eks.append(ek)
    a = jnp.where(causal, jnp.concatenate(blocks, axis=0), 0.0)
    return a, qts, kts, eqs, eks


def _tri_sum(tri, x, dn):
    t = tri.astype(bf16)
    hi = x.astype(bf16)
    rest = x - hi.astype(f32)
    mid = rest.astype(bf16)
    lo = (rest - mid.astype(f32)).astype(bf16)
    return _dot(t, hi, dn) + _dot(t, mid, dn) + _dot(t, lo, dn)


def _tri():
    r = lax.broadcasted_iota(jnp.int32, (A_CHUNK, A_CHUNK), 0)
    c = lax.broadcasted_iota(jnp.int32, (A_CHUNK, A_CHUNK), 1)
    return r >= c


def _hgrn_in_specs(rev_nb=None):
    def im(col):
        if rev_nb is None:
            return lambda p, i: (i, col + p)
        return lambda p, i: (rev_nb - 1 - i, col + p)
    return [pl.BlockSpec((A_TB, A_KW), im(O_AQ // A_KW)), pl.BlockSpec((A_TB, A_KW), im(O_AF // A_KW)),
            pl.BlockSpec((A_TB, A_VW), im(O_AI // A_VW)), pl.BlockSpec((A_TB, A_VW), im(O_AG // A_VW)),
            pl.BlockSpec((1, A_KW), lambda p, i: (0, p)), pl.BlockSpec((1, A_V), lambda p, i: (0, 0))]


def hgrn_fwd(u, lb, ng, side=None):
    S = u.shape[0]
    nb = S // A_TB

    def body(q_ref, f_ref, i_ref, g_ref, lb_ref, ng_ref, o_ref, ya_ref, st_ref, state):
        @pl.when(pl.program_id(1) == 0)
        def _():
            state[...] = jnp.zeros_like(state)

        causal = _tri()
        tri = causal.astype(f32)

        def chunk(n, carry):
            rows = pl.ds(pl.multiple_of(n * A_CHUNK, A_CHUNK), A_CHUNK)
            o_parts, y_parts = [], []
            for hh in range(A_HP):
                ks = slice(hh * A_K, (hh + 1) * A_K)
                vs = slice(hh * A_V, (hh + 1) * A_V)
                qf, _, _, _, logf, k = _hgrn_gates(q_ref[rows, ks], f_ref[rows, ks], lb_ref[:, ks])
                vi = i_ref[rows, vs].astype(bf16)
                gg = g_ref[rows, vs]
                b = _tri_sum(tri, logf, NN)
                s0 = state[hh]
                st_ref[n, hh] = s0
                o = _dot((qf * jnp.exp(b)).astype(bf16), s0.astype(bf16), NT)
                a, _, _, _, _ = _hgrn_intra(qf, k, b, causal)
                o = o + _dot(a.astype(bf16), vi, NN)
                bend = b[A_CHUNK - 1:A_CHUNK, :]
                ke = (k * jnp.exp(bend - b)).astype(bf16)
                state[hh] = s0 * jnp.exp(bend) + _dot(vi, ke, TN)
                rstd = lax.rsqrt(jnp.mean(o * o, axis=1, keepdims=True) + EPS)
                o_parts.append(o)
                y_parts.append(o * rstd * ng_ref[...] * (gg * _sig(gg)))
            o_ref[rows, :] = jnp.concatenate(o_parts, axis=1)
            ya_ref[rows, :] = jnp.concatenate(y_parts, axis=1).astype(bf16)
            return carry

        lax.fori_loop(0, A_NCH, chunk, 0)

    return _call(
        body, name="hgrn_fwd", grid=(A_HEADS // A_HP, nb),
        in_specs=_hgrn_in_specs(),
        out_specs=[pl.BlockSpec((A_TB, A_VW), lambda p, i: (i, p)), pl.BlockSpec((A_TB, A_VW), lambda p, i: (i, p)),
                   pl.BlockSpec((A_NCH, A_HP, A_V, A_K), lambda p, i: (i, p, 0, 0))],
        out_shape=[jax.ShapeDtypeStruct((S, 384), f32), jax.ShapeDtypeStruct((S, 384), bf16),
                   jax.ShapeDtypeStruct((S // A_CHUNK, A_HEADS, A_V, A_K), f32)],
        scratch_shapes=[pltpu.VMEM((A_HP, A_V, A_K), f32)],
        sem=("parallel", "arbitrary"), args=(u, u, u, u, lb, ng), side=side)


def hgrn_bwd(u, lb, ng, o, st, dya, side=None):
    S = u.shape[0]
    nb = S // A_TB

    def body(q_ref, f_ref, i_ref, g_ref, lb_ref, ng_ref, o_ref, st_ref, dy_ref,
             dq_ref, df_ref, di_ref, dg_ref, dlb_ref, dng_ref, dstate):
        @pl.when(pl.program_id(1) == 0)
        def _():
            dstate[...] = jnp.zeros_like(dstate)
            dlb_ref[...] = jnp.zeros_like(dlb_ref)
            dng_ref[...] = jnp.zeros_like(dng_ref)

        causal = _tri()
        tri = causal.astype(f32)

        def chunk(it, carry):
            n = A_NCH - 1 - it
            rows = pl.ds(pl.multiple_of(n * A_CHUNK, A_CHUNK), A_CHUNK)
            dq_p, df_p, di_p, dg_p, dlb_p = [], [], [], [], []
            dng_acc = jnp.zeros((1, A_V), f32)
            for hh in range(A_HP):
                ks = slice(hh * A_K, (hh + 1) * A_K)
                vs = slice(hh * A_V, (hh + 1) * A_V)
                lbh = lb_ref[:, ks]
                qr = q_ref[rows, ks]
                qf, sq, sig, f, logf, k = _hgrn_gates(qr, f_ref[rows, ks], lbh)
                vi = i_ref[rows, vs].astype(bf16)
                gg = g_ref[rows, vs]
                b = _tri_sum(tri, logf, NN)
                eb = jnp.exp(b)
                bend = b[A_CHUNK - 1:A_CHUNK, :]
                eend = jnp.exp(bend)
                ekend = jnp.exp(bend - b)
                qe = (qf * eb).astype(bf16)
                ke = (k * ekend).astype(bf16)
                s0 = st_ref[n, hh]
                dsend = dstate[hh]
                ov = o_ref[rows, vs]
                dy = dy_ref[rows, vs]
                rstd = lax.rsqrt(jnp.mean(ov * ov, axis=1, keepdims=True) + EPS)
                oh = ov * rstd
                sg = _sig(gg)
                d_on = dy * (gg * sg)
                dg_p.append(dy * oh * ng_ref[...] * (sg * (1.0 + gg * (1.0 - sg))))
                dng_acc = dng_acc + jnp.sum(d_on * oh, axis=0, keepdims=True)
                doh = d_on * ng_ref[...]
                do = (rstd * (doh - oh * jnp.mean(doh * oh, axis=1, keepdims=True))).astype(bf16)
                a, qts, kts, eqs, eks = _hgrn_intra(qf, k, b, causal)
                da = jnp.where(causal, _dot(do, vi, NT), 0.0).astype(bf16)
                dsb = dsend.astype(bf16)
                dv = _dot(a.astype(bf16), do, TN) + _dot(ke, dsb, NT)
                dq = _dot(do, s0.astype(bf16), NN) * eb
                dk_state = _dot(vi, dsb, NN) * ekend
                dk = dk_state
                dq_i = []
                for sb in range(A_NSUB):
                    da_sb = da[sb * A_SUB:(sb + 1) * A_SUB, :]
                    dq_i.append(_dot(da_sb, kts[sb], NN) * eqs[sb])
                    dk = dk + _dot(da_sb, qts[sb], TN) * eks[sb]
                dq = dq + jnp.concatenate(dq_i, axis=0)
                db = qf * dq - k * dk
                extra = jnp.sum(k * dk_state, axis=0, keepdims=True) + eend * jnp.sum(s0 * dsend, axis=0, keepdims=True)
                dlogf = _tri_sum(tri, db, TN) + extra
                dstate[hh] = _dot(do, qe, TN) + eend * dsend
                d_pre = jnp.where(f > TINY, dlogf / f, 0.0) - dk
                dlb_p.append(jnp.sum((1.0 - sig) * d_pre, axis=0, keepdims=True))
                df_p.append((1.0 - lbh) * d_pre * sig * (1.0 - sig))
                dq_p.append(dq * (sq * (1.0 + qr * (1.0 - sq))))
                di_p.append(dv)
            dq_ref[rows, :] = jnp.concatenate(dq_p, axis=1).astype(bf16)
            df_ref[rows, :] = jnp.concatenate(df_p, axis=1).astype(bf16)
            di_ref[rows, :] = jnp.concatenate(di_p, axis=1).astype(bf16)
            dg_ref[rows, :] = jnp.concatenate(dg_p, axis=1).astype(bf16)
            dlb_ref[...] += jnp.concatenate(dlb_p, axis=1)
            dng_ref[0] += dng_acc
            return carry

        lax.fori_loop(0, A_NCH, chunk, 0)

    rev = lambda p, i: (nb - 1 - i, p)
    return _call(
        body, name="hgrn_bwd", grid=(A_HEADS // A_HP, nb),
        in_specs=_hgrn_in_specs(nb) + [pl.BlockSpec((A_TB, A_VW), rev),
                                       pl.BlockSpec((A_NCH, A_HP, A_V, A_K), lambda p, i: (nb - 1 - i, p, 0, 0)),
                                       pl.BlockSpec((A_TB, A_VW), rev)],
        out_specs=[pl.BlockSpec((A_TB, A_KW), rev), pl.BlockSpec((A_TB, A_KW), rev),
                   pl.BlockSpec((A_TB, A_VW), rev), pl.BlockSpec((A_TB, A_VW), rev),
                   pl.BlockSpec((1, A_KW), lambda p, i: (0, p)), pl.BlockSpec((1, 1, A_V), lambda p, i: (p, 0, 0))],
        out_shape=[jax.ShapeDtypeStruct((S, 768), bf16), jax.ShapeDtypeStruct((S, 768), bf16),
                   jax.ShapeDtypeStruct((S, 384), bf16), jax.ShapeDtypeStruct((S, 384), bf16),
                   jax.ShapeDtypeStruct((1, 768), f32), jax.ShapeDtypeStruct((A_HEADS // A_HP, 1, A_V), f32)],
        scratch_shapes=[pltpu.VMEM((A_HP, A_V, A_K), f32)],
        sem=("parallel", "arbitrary"), args=(u, u, u, u, lb, ng, o, st, dya), side=side)


B_TK = 128
SCALE = HD ** -0.5


def _split(x):
    hi = x.astype(bf16)
    return hi, (x - hi.astype(f32)).astype(bf16)


def _dot2(x, m, dn):
    hi, lo = _split(x)
    return _dot(hi, m, dn) + _dot(lo, m, dn)


def _sb_block(qs, kh, mask, m_gt, c):
    z = _dot(qs, kh, NT)
    sp = jnp.maximum(z, 0.0) + jnp.log(1.0 + jnp.exp(-jnp.abs(z)))
    lneg = jnp.where(mask, -sp, 0.0)
    lsz = z - sp
    suf = _dot2(lneg, m_gt, NN) + c
    a = jnp.where(mask, jnp.exp(lsz + suf), 0.0)
    return lneg, lsz, a


def _sb_masks(tq, i, jj):
    t_idx = i * tq + lax.broadcasted_iota(jnp.int32, (tq, B_TK), 0)
    s_idx = jj * B_TK + lax.broadcasted_iota(jnp.int32, (tq, B_TK), 1)
    return s_idx < t_idx


def _sb_tri(strict):
    r = lax.broadcasted_iota(jnp.int32, (B_TK, B_TK), 0)
    c = lax.broadcasted_iota(jnp.int32, (B_TK, B_TK), 1)
    return (r > c if strict else r >= c).astype(bf16)


B_DEAD = -88.0


def _sb_walk(nkb, step, init):
    def cond(state):
        it, alive, _ = state
        return jnp.logical_and(it < nkb, alive)

    def body(state):
        it, _, carry = state
        carry = step(it, carry)
        top = jnp.max(functools.reduce(jnp.maximum, [h[1] for h in carry]))
        return it + 1, top > B_DEAD, carry

    return lax.while_loop(cond, body, (jnp.int32(0), jnp.bool_(True), init))[2]


B_HP = 6
B_W = B_HP * HD


def sb_fwd(u, side=None):
    S = u.shape[0]
    tq = 64

    def body(q_ref, k_ref, v_ref, o_ref):
        i = pl.program_id(1)
        nkb = ((i + 1) * tq + B_TK - 1) // B_TK
        m_gt = _sb_tri(True)
        qs = [(q_ref[:, hh * HD:(hh + 1) * HD] * SCALE).astype(bf16) for hh in range(B_HP)]

        def step(it, carry):
            jj = nkb - 1 - it
            rows = pl.ds(pl.multiple_of(jj * B_TK, B_TK), B_TK)
            mask = _sb_masks(tq, i, jj)
            kb, vb = k_ref[rows, :], v_ref[rows, :]
            out = []
            for hh in range(B_HP):
                acc, c = carry[hh]
                kh = kb[:, hh * HD:(hh + 1) * HD].astype(bf16)
                vh = vb[:, hh * HD:(hh + 1) * HD].astype(bf16)
                lneg, _, a = _sb_block(qs[hh], kh, mask, m_gt, c)
                out.append((acc + _dot2(a, vh, NN), c + jnp.sum(lneg, axis=1, keepdims=True)))
            return tuple(out)

        z0 = (jnp.zeros((tq, HD), f32), jnp.zeros((tq, 1), f32))
        res = _sb_walk(nkb, step, (z0,) * B_HP)
        o_ref[...] = jnp.concatenate([r[0] for r in res], axis=1)

    outs, souts = _call(
        body, name="sb_fwd", grid=(B_HEADS // B_HP, S // tq),
        in_specs=[pl.BlockSpec((tq, B_W), lambda p, i: (i, O_BQ // B_W + p)),
                  pl.BlockSpec((S, B_W), lambda p, i: (0, O_BK // B_W + p)),
                  pl.BlockSpec((S, B_W), lambda p, i: (0, O_BV // B_W + p))],
        out_specs=[pl.BlockSpec((tq, B_W), lambda p, i: (i, p))],
        out_shape=[jax.ShapeDtypeStruct((S, 384), f32)],
        sem=("parallel", "arbitrary"), args=(u, u, u), side=side)
    return outs[0], souts


def sb_bwd(u, yb, dyb, side=None):
    S = u.shape[0]
    tq = 64
    nq = S // tq

    def body(q_ref, k_ref, v_ref, y_ref, dy_ref, dq_ref, dk_out, dv_out, dk_ref, dv_ref, out_sem):
        p, i = pl.program_id(0), pl.program_id(1)

        @pl.when(i == 0)
        def _():
            dk_ref[...] = jnp.zeros_like(dk_ref)
            dv_ref[...] = jnp.zeros_like(dv_ref)

        nkb = ((i + 1) * tq + B_TK - 1) // B_TK
        m_gt = _sb_tri(True)
        m_ge = _sb_tri(False)
        qs, dos, tot = [], [], []
        for hh in range(B_HP):
            hs = slice(hh * HD, (hh + 1) * HD)
            qs.append((q_ref[:, hs] * SCALE).astype(bf16))
            dob = dy_ref[:, hs].astype(bf16)
            dos.append(dob)
            tot.append(jnp.sum(dob.astype(f32) * y_ref[:, hs], axis=1, keepdims=True))

        def step(it, carry):
            jj = nkb - 1 - it
            rows = pl.ds(pl.multiple_of(jj * B_TK, B_TK), B_TK)
            mask = _sb_masks(tq, i, jj)
            kb, vb = k_ref[rows, :], v_ref[rows, :]
            out, dk_p, dv_p = [], [], []
            for hh in range(B_HP):
                dq, c, cg = carry[hh]
                kh = kb[:, hh * HD:(hh + 1) * HD].astype(bf16)
                vh = vb[:, hh * HD:(hh + 1) * HD].astype(bf16)
                lneg, lsz, a = _sb_block(qs[hh], kh, mask, m_gt, c)
                g = a * _dot(dos[hh], vh, NT)
                pre = tot[hh] - cg - _dot2(g, m_ge, NN)
                beta = jnp.exp(lsz)
                dz = jnp.where(mask, g * (1.0 - beta) - beta * pre, 0.0).astype(bf16)
                dk_p.append(_dot(dz, qs[hh], TN))
                dv_p.append(_dot(a.astype(bf16), dos[hh], TN))
                out.append((dq + _dot(dz, kh, NN), c + jnp.sum(lneg, axis=1, keepdims=True),
                            cg + jnp.sum(g, axis=1, keepdims=True)))
            dk_ref[rows, :] += jnp.concatenate(dk_p, axis=1)
            dv_ref[rows, :] += jnp.concatenate(dv_p, axis=1)
            return tuple(out)

        z0 = (jnp.zeros((tq, HD), f32), jnp.zeros((tq, 1), f32), jnp.zeros((tq, 1), f32))
        res = _sb_walk(nkb, step, (z0,) * B_HP)
        dq_ref[...] = jnp.concatenate([r[0] for r in res], axis=1) * SCALE

        @pl.when(i == nq - 1)
        def _():
            cols = pl.ds(pl.multiple_of(p * B_W, LANE), B_W)
            ck = pltpu.make_async_copy(dk_ref, dk_out.at[:, cols], out_sem.at[0])
            cv = pltpu.make_async_copy(dv_ref, dv_out.at[:, cols], out_sem.at[1])
            ck.start()
            cv.start()
            ck.wait()
            cv.wait()

    row = pl.BlockSpec((tq, B_W), lambda p, i: (i, p))
    hbm = pl.BlockSpec(memory_space=pl.ANY)
    osd = jax.ShapeDtypeStruct((S, 384), f32)
    return _call(
        body, name="sb_bwd", grid=(B_HEADS // B_HP, nq),
        in_specs=[pl.BlockSpec((tq, B_W), lambda p, i: (i, O_BQ // B_W + p)),
                  pl.BlockSpec((S, B_W), lambda p, i: (0, O_BK // B_W + p)),
                  pl.BlockSpec((S, B_W), lambda p, i: (0, O_BV // B_W + p)), row, row],
        out_specs=[row, hbm, hbm], out_shape=[osd, osd, osd],
        scratch_shapes=[pltpu.VMEM((S, B_W), f32), pltpu.VMEM((S, B_W), f32), pltpu.SemaphoreType.DMA((2,))],
        sem=("parallel", "arbitrary"), args=(u, u, u, yb, dyb), side=side)


def _dil_rows(i, rho, r):
    if r == 1:
        return pl.ds(pl.multiple_of(i * C_BLK, C_BLK), C_BLK)
    return pl.ds(i * (C_BLK * r) + rho, C_BLK, stride=r)


def _dil_scores(qs, kc, kp, i, slope_r):
    qi = lax.broadcasted_iota(jnp.int32, (C_BLK, C_BLK), 0)
    kj = lax.broadcasted_iota(jnp.int32, (C_BLK, C_BLK), 1)
    d_c = qi - kj
    d_p = d_c + C_BLK
    ok_c = d_c >= 0
    ok_p = jnp.logical_and(d_c <= 0, i > 0)
    s_c = jnp.where(ok_c, _dot(qs, kc, NT) - slope_r * d_c.astype(f32), NEG_BIG)
    s_p = jnp.where(ok_p, _dot(qs, kp, NT) - slope_r * d_p.astype(f32), NEG_BIG)
    return s_c, s_p, ok_c, ok_p


def _dil_slope(g, r, hh):
    pair = pl.program_id(0)
    return jnp.where(pair == 0, C_SLOPES[4 * g + hh] * r, C_SLOPES[4 * g + 2 + hh] * r).astype(f32)


def _dil_u_specs(g, S):
    def im(off):
        return lambda p, rho: (0, (off + g * 256) // LANE + p)
    return [pl.BlockSpec((S, LANE), im(O_CQ)), pl.BlockSpec((S, LANE), im(O_CK)), pl.BlockSpec((S, LANE), im(O_CV))]


def dil_fwd(u, g, side=None):
    S = u.shape[0]
    r = C_GROUPS[g][1]
    nbk = S // r // C_BLK

    def body(q_ref, k_ref, v_ref, o_ref, l_ref):
        rho = pl.program_id(1)

        def step(i, carry):
            rc = _dil_rows(i, rho, r)
            rp = _dil_rows(jnp.maximum(i - 1, 0), rho, r)
            q2, kc2, kp2, vc2, vp2 = q_ref[rc, :], k_ref[rc, :], k_ref[rp, :], v_ref[rc, :], v_ref[rp, :]
            o_p, l_p = [], []
            for hh in range(2):
                hs = slice(hh * HD, (hh + 1) * HD)
                qs = (q2[:, hs] * SCALE).astype(bf16)
                kc, kp = kc2[:, hs].astype(bf16), kp2[:, hs].astype(bf16)
                vc, vp = vc2[:, hs].astype(bf16), vp2[:, hs].astype(bf16)
                s_c, s_p, _, _ = _dil_scores(qs, kc, kp, i, _dil_slope(g, r, hh))
                m = jnp.maximum(jnp.max(s_c, axis=1, keepdims=True), jnp.max(s_p, axis=1, keepdims=True))
                p_c, p_p = jnp.exp(s_c - m), jnp.exp(s_p - m)
                den = jnp.sum(p_c, axis=1, keepdims=True) + jnp.sum(p_p, axis=1, keepdims=True)
                o_p.append((_dot(p_c.astype(bf16), vc, NN) + _dot(p_p.astype(bf16), vp, NN)) / den)
                l_p.append(jnp.broadcast_to(m + jnp.log(den), (C_BLK, HD)))
            o_ref[rc, :] = jnp.concatenate(o_p, axis=1)
            l_ref[rc, :] = jnp.concatenate(l_p, axis=1)
            return carry

        lax.fori_loop(0, nbk, step, 0, unroll=2)

    ospec = pl.BlockSpec((S, LANE), lambda p, rho: (0, p))
    osd = jax.ShapeDtypeStruct((S, 256), f32)
    return _call(
        body, name=f"dil_fwd{g}", grid=(2, r),
        in_specs=_dil_u_specs(g, S), out_specs=[ospec, ospec], out_shape=[osd, osd],
        sem=("parallel", "arbitrary"), args=(u, u, u), side=side)


def dil_merge(os_, ls_):
    S = os_[0].shape[0]

    def body(o0, o1, o2, l0, l1, l2, y_ref, lse_ref):
        a, b, c = l0[...], l1[...], l2[...]
        m = jnp.maximum(jnp.maximum(a, b), c)
        ea, eb, ec = jnp.exp(a - m), jnp.exp(b - m), jnp.exp(c - m)
        den = ea + eb + ec
        y_ref[...] = (ea * o0[...] + eb * o1[...] + ec * o2[...]) / den
        lse_ref[...] = m + jnp.log(den)

    spec = pl.BlockSpec((512, 256), lambda i: (i, 0))
    osd = jax.ShapeDtypeStruct((S, 256), f32)
    return pl.pallas_call(
        body, name="dil_merge", grid=(S // 512,), in_specs=[spec] * 6, out_specs=[spec, spec],
        out_shape=[osd, osd], compiler_params=_cp(("parallel",)),
    )(*os_, *ls_)


def dil_bwd(u, g, dyc, yc, lse):
    S = u.shape[0]
    r = C_GROUPS[g][1]
    nbk = S // r // C_BLK

    def body(q_ref, k_ref, v_ref, dy_ref, y_ref, l_ref, dq_ref, dk_ref, dv_ref):
        rho = pl.program_id(1)

        @pl.when(rho == 0)
        def _():
            dk_ref[...] = jnp.zeros_like(dk_ref)
            dv_ref[...] = jnp.zeros_like(dv_ref)

        def step(i, carry):
            rc = _dil_rows(i, rho, r)
            rp = _dil_rows(jnp.maximum(i - 1, 0), rho, r)
            q2, kc2, kp2, vc2, vp2 = q_ref[rc, :], k_ref[rc, :], k_ref[rp, :], v_ref[rc, :], v_ref[rp, :]
            dy2, y2, l2 = dy_ref[rc, :], y_ref[rc, :], l_ref[rc, :]
            dq_p, dkc_p, dkp_p, dvc_p, dvp_p = [], [], [], [], []
            for hh in range(2):
                hs = slice(hh * HD, (hh + 1) * HD)
                qs = (q2[:, hs] * SCALE).astype(bf16)
                kc, kp = kc2[:, hs].astype(bf16), kp2[:, hs].astype(bf16)
                vc, vp = vc2[:, hs].astype(bf16), vp2[:, hs].astype(bf16)
                dy = dy2[:, hs]
                dyb = dy.astype(bf16)
                s_c, s_p, ok_c, ok_p = _dil_scores(qs, kc, kp, i, _dil_slope(g, r, hh))
                lrow = l2[:, hh * HD:hh * HD + 1]
                delta = jnp.sum(dy * y2[:, hs], axis=1, keepdims=True)
                pi_c = jnp.where(ok_c, jnp.exp(s_c - lrow), 0.0)
                pi_p = jnp.where(ok_p, jnp.exp(s_p - lrow), 0.0)
                ds_c = (pi_c * (_dot(dyb, vc, NT) - delta)).astype(bf16)
                ds_p = (pi_p * (_dot(dyb, vp, NT) - delta)).astype(bf16)
                dq_p.append((_dot(ds_c, kc, NN) + _dot(ds_p, kp, NN)) * SCALE)
                dkc_p.append(_dot(ds_c, qs, TN))
                dkp_p.append(_dot(ds_p, qs, TN))
                dvc_p.append(_dot(pi_c.astype(bf16), dyb, TN))
                dvp_p.append(_dot(pi_p.astype(bf16), dyb, TN))
            dq_ref[rc, :] = jnp.concatenate(dq_p, axis=1)
            dk_ref[rc, :] += jnp.concatenate(dkc_p, axis=1)
            dv_ref[rc, :] += jnp.concatenate(dvc_p, axis=1)
            dk_ref[rp, :] += jnp.concatenate(dkp_p, axis=1)
            dv_ref[rp, :] += jnp.concatenate(dvp_p, axis=1)
            return carry

        lax.fori_loop(0, nbk, step, 0, unroll=2)

    ospec = pl.BlockSpec((S, LANE), lambda p, rho: (0, p))
    osd = jax.ShapeDtypeStruct((S, 256), f32)
    return pl.pallas_call(
        body, name=f"dil_bwd{g}", grid=(2, r),
        in_specs=_dil_u_specs(g, S) + [ospec, ospec, ospec], out_specs=[ospec] * 3, out_shape=[osd] * 3,
        compiler_params=_cp(("parallel", "arbitrary")),
    )(u, u, u, dyc, yc, lse)


def _rows_tile(rows):
    return _tile(rows, (256, 176, 128, 64, 32, 16, 8))


def cast_bf16(w):
    shape = w.shape
    w2 = w.reshape(-1, shape[-1])
    rows, cols = w2.shape
    tr = _rows_tile(rows)

    def body(x_ref, o_ref):
        o_ref[...] = x_ref[...].astype(bf16)

    spec = pl.BlockSpec((tr, cols), lambda i: (i, 0))
    out = pl.pallas_call(
        body, name="cast_bf16", grid=(rows // tr,), in_specs=[spec], out_specs=spec,
        out_shape=jax.ShapeDtypeStruct((rows, cols), bf16), compiler_params=_cp(("parallel",)),
    )(w2)
    return out.reshape(shape)


BC1 = 1.0 - ADAM_B1 ** ADAM_STEP
BC2 = 1.0 - ADAM_B2 ** ADAM_STEP


def _adam_math(w, g, m, v):
    m2 = ADAM_B1 * m + (1.0 - ADAM_B1) * g
    v2 = ADAM_B2 * v + (1.0 - ADAM_B2) * (g * g)
    delta = -ADAM_LR * ((m2 / BC1) / (jnp.sqrt(v2 / BC2) + ADAM_EPS) + ADAM_WD * w)
    return delta, m2, v2


def adam(w, g, m, v):
    shape = w.shape
    r2 = lambda t: t.reshape(-1, shape[-1])
    rows, cols = r2(w).shape
    tr = _rows_tile(rows)

    def body(w_ref, g_ref, m_ref, v_ref, d_ref, m2_ref, v2_ref):
        d_ref[...], m2_ref[...], v2_ref[...] = _adam_math(w_ref[...], g_ref[...], m_ref[...], v_ref[...])

    spec = pl.BlockSpec((tr, cols), lambda i: (i, 0))
    osd = jax.ShapeDtypeStruct((rows, cols), f32)
    outs = pl.pallas_call(
        body, name="adam", grid=(rows // tr,), in_specs=[spec] * 4, out_specs=[spec] * 3, out_shape=[osd] * 3,
        compiler_params=_cp(("parallel",)),
    )(r2(w), r2(g), r2(m), r2(v))
    return [o.reshape(shape) for o in outs]


def adam_layers(w, gs, m, v):
    depth, rows, cols = w.shape
    tr = _tile(rows, (128, 64, 32, 16, 8))
    nb = rows // tr

    def body(w_ref, m_ref, v_ref, *rest):
        g_refs, (g_out, d_ref, m2_ref, v2_ref) = rest[:depth], rest[depth:]
        for k in range(depth):
            @pl.when(pl.program_id(0) == k)
            def _(k=k):
                g = g_refs[k][...]
                g_out[0] = g
                d_ref[0], m2_ref[0], v2_ref[0] = _adam_math(w_ref[0], g, m_ref[0], v_ref[0])

    def g_spec(k):
        return pl.BlockSpec((tr, cols), lambda l, i: (jnp.where(l < k, 0, jnp.where(l > k, nb - 1, i)), 0))

    wspec = pl.BlockSpec((1, tr, cols), lambda l, i: (l, i, 0))
    osd = jax.ShapeDtypeStruct(w.shape, f32)
    return pl.pallas_call(
        body, name="adam_layers", grid=(depth, nb), in_specs=[wspec] * 3 + [g_spec(k) for k in range(depth)],
        out_specs=[wspec] * 4, out_shape=[osd] * 4, compiler_params=_cp(("arbitrary", "arbitrary")),
    )(w, m, v, *gs)


ADA_N = 9 * D // 4
ADA_TN = 384


def ada_fwd(c_all, w_ada):
    def body(c_ref, w_ref, o_ref):
        cv = c_ref[...]
        o_ref[0] = _dot((cv * _sig(cv)).astype(bf16), w_ref[0].astype(bf16), NN)

    return pl.pallas_call(
        body, name="ada_fwd", grid=(DEPTH, ADA_N // ADA_TN),
        in_specs=[pl.BlockSpec((8, D), lambda l, j: (0, 0)), pl.BlockSpec((1, D, ADA_TN), lambda l, j: (l, 0, j))],
        out_specs=pl.BlockSpec((1, 8, ADA_TN), lambda l, j: (l, 0, j)),
        out_shape=jax.ShapeDtypeStruct((DEPTH, 8, ADA_N), f32), compiler_params=_cp(("parallel", "parallel")),
    )(c_all, w_ada)


def ada_bwd_adam(c_all, dm, w, m, v, side=None):
    tr = 128

    def body(c_ref, dm_ref, w_ref, m_ref, v_ref, g_ref, d_ref, m2_ref, v2_ref):
        cv = c_ref[...]
        g = _dot((cv * _sig(cv)).astype(bf16), dm_ref[0].astype(bf16), TN)
        g_ref[0] = g
        d_ref[0], m2_ref[0], v2_ref[0] = _adam_math(w_ref[0], g, m_ref[0], v_ref[0])

    wspec = pl.BlockSpec((1, tr, ADA_N), lambda l, i: (l, i, 0))
    osd = jax.ShapeDtypeStruct((DEPTH, D, ADA_N), f32)
    return _call(
        body, name="ada_bwd_adam", grid=(DEPTH, D // tr),
        in_specs=[pl.BlockSpec((8, tr), lambda l, i: (0, i)), pl.BlockSpec((1, 8, ADA_N), lambda l, i: (l, 0, 0)),
                  wspec, wspec, wspec],
        out_specs=[wspec] * 4, out_shape=[osd] * 4, sem=("parallel", "parallel"), args=(c_all, dm, w, m, v), side=side)


def _lb_probs(x):
    mx = jnp.max(x, axis=0, keepdims=True)
    e = jnp.exp(x - mx)
    return e / jnp.sum(e, axis=0, keepdims=True)


def lb_fwd(logits):
    def body(x_ref, o_ref):
        p = _lb_probs(x_ref[...])
        rows = [jnp.zeros((1, 768), f32)]
        for l in range(1, DEPTH):
            rows.append(rows[-1] + p[l:l + 1, :])
        o_ref[...] = jnp.concatenate(rows, axis=0)

    return pl.pallas_call(body, name="lb_fwd", out_shape=jax.ShapeDtypeStruct((DEPTH, 768), f32))(logits)


def lb_bwd(logits, dlb):
    def body(x_ref, d_ref, o_ref):
        p = _lb_probs(x_ref[...])
        d = d_ref[...]
        rows = [jnp.zeros((1, 768), f32)] * DEPTH
        acc = jnp.zeros((1, 768), f32)
        for l in range(DEPTH - 1, 0, -1):
            acc = acc + d[l:l + 1, :]
            rows[l] = acc
        dp = jnp.concatenate(rows, axis=0)
        o_ref[...] = p * (dp - jnp.sum(p * dp, axis=0, keepdims=True))

    return pl.pallas_call(body, name="lb_bwd", out_shape=jax.ShapeDtypeStruct((DEPTH, 768), f32))(logits, dlb)


def sum_slots(x):
    n, rows, cols = x.shape
    tr = _rows_tile(rows)

    def body(x_ref, o_ref):
        acc = x_ref[0]
        for j in range(1, n):
            acc = acc + x_ref[j]
        o_ref[...] = acc

    return pl.pallas_call(
        body, name="sum_slots", grid=(rows // tr,),
        in_specs=[pl.BlockSpec((n, tr, cols), lambda i: (0, i, 0))], out_specs=pl.BlockSpec((tr, cols), lambda i: (i, 0)),
        out_shape=jax.ShapeDtypeStruct((rows, cols), f32), compiler_params=_cp(("parallel",)),
    )(x)


ANY = pl.BlockSpec(memory_space=pl.ANY)
CHIP_FLIPS = ((1, 0), (0, 1), (1, 1))
DEV_FLIPS = tuple((a, b, d) for a in (0, 1) for b in (0, 1) for d in (0, 1))[1:]


def _me():
    return lax.axis_index("x"), lax.axis_index("y"), lax.axis_index("c")


def _flip(v, f):
    return 1 - v if f else v


def _comm_call(body, name, ins, out_shapes, n_remote, n_local):
    return pl.pallas_call(
        body, name=name, in_specs=[ANY] * len(ins), out_specs=[ANY] * len(out_shapes), out_shape=out_shapes,
        scratch_shapes=[pltpu.SemaphoreType.DMA((n_remote,)), pltpu.SemaphoreType.DMA((n_remote,)),
                        pltpu.SemaphoreType.DMA((max(n_local, 1),))],
    )(*ins)


def run_plan(plan, name):
    ni, no = len(plan.ins), len(plan.outs)

    def body(*refs):
        ins, outs, sems = refs[:ni], refs[ni:ni + no], refs[ni + no:]
        plan.start(ins, outs, *sems)
        plan.wait(ins, outs, *sems)

    return pl.pallas_call(body, name=name, in_specs=[ANY] * ni, out_specs=[ANY] * no, out_shape=list(plan.outs),
                          scratch_shapes=plan.sems())(*plan.ins)


def gather_chips_plan(arrs, layer=None):
    n = len(arrs)
    layers = list(layer) if isinstance(layer, (list, tuple)) else [layer] * n
    shapes = [a.shape if l is None else a.shape[1:] for a, l in zip(arrs, layers)]

    def copies(ins, outs, send, recv, loc):
        x, y, c = _me()
        mine = 2 * x + y
        srcs = [r if l is None else r.at[l] for r, l in zip(ins, layers)]
        locs = [pltpu.make_async_copy(srcs[a], outs[a].at[mine], loc.at[a]) for a in range(n)]

        def remote(a, k, slot):
            fx, fy = CHIP_FLIPS[k]
            return pltpu.make_async_remote_copy(srcs[a], outs[a].at[slot], send.at[3 * a + k], recv.at[3 * a + k],
                                                device_id=(_flip(x, fx), _flip(y, fy), c), device_id_type=MESH)

        peers = [2 * _flip(x, fx) + _flip(y, fy) for fx, fy in CHIP_FLIPS]
        return locs, remote, mine, peers

    def start(ins, outs, send, recv, loc):
        locs, remote, mine, _ = copies(ins, outs, send, recv, loc)
        for cp in locs:
            cp.start()
        for a in range(n):
            for k in range(3):
                remote(a, k, mine).start()

    def wait(ins, outs, send, recv, loc):
        locs, remote, _, peers = copies(ins, outs, send, recv, loc)
        for a in range(n):
            for k in range(3):
                cp = remote(a, k, peers[k])
                cp.wait_recv()
                cp.wait_send()
        for cp in locs:
            cp.wait()

    outs = [jax.ShapeDtypeStruct((4,) + tuple(s), a.dtype) for s, a in zip(shapes, arrs)]
    return Plan(list(arrs), outs, 3 * n, n, start, wait)


def all_gather_chips(arrs, layer=None, name="ag4"):
    return run_plan(gather_chips_plan(arrs, layer), name)


def all_gather_devs(arr, name="ag8"):
    def body(in_ref, out_ref, send, recv, loc):
        x, y, c = _me()
        mine = 4 * x + 2 * y + c
        lc = pltpu.make_async_copy(in_ref, out_ref.at[mine], loc.at[0])
        lc.start()

        def remote(k, slot):
            fx, fy, fc = DEV_FLIPS[k]
            return pltpu.make_async_remote_copy(in_ref, out_ref.at[slot], send.at[k], recv.at[k],
                                                device_id=(_flip(x, fx), _flip(y, fy), _flip(c, fc)), device_id_type=MESH)

        for k in range(7):
            remote(k, mine).start()
        for k, (fx, fy, fc) in enumerate(DEV_FLIPS):
            cp = remote(k, 4 * _flip(x, fx) + 2 * _flip(y, fy) + _flip(c, fc))
            cp.wait_recv()
            cp.wait_send()
        lc.wait()

    return _comm_call(body, name, [arr], [jax.ShapeDtypeStruct((8,) + arr.shape, arr.dtype)], 7, 1)[0]


def _rows_of(which, rows):
    return pl.ds(pl.multiple_of(which * rows, 16), rows)


def dev_exchange_plan(parts):
    n = len(parts)

    def copies(ins, outs, send, recv, loc):
        x, y, c = _me()
        mine = 4 * x + 2 * y + c

        def piece(a, px, py, pc):
            rows = ins[a].shape[1] // 2
            return ins[a].at[2 * px + py, _rows_of(pc, rows), :]

        locs = [pltpu.make_async_copy(piece(a, x, y, c), outs[a].at[mine], loc.at[a]) for a in range(n)]

        def remote(a, k, slot):
            fx, fy, fc = DEV_FLIPS[k]
            px, py, pc = _flip(x, fx), _flip(y, fy), _flip(c, fc)
            return pltpu.make_async_remote_copy(piece(a, px, py, pc), outs[a].at[slot], send.at[7 * a + k], recv.at[7 * a + k],
                                                device_id=(px, py, pc), device_id_type=MESH)

        peers = [4 * _flip(x, fx) + 2 * _flip(y, fy) + _flip(c, fc) for fx, fy, fc in DEV_FLIPS]
        return locs, remote, mine, peers

    def start(ins, outs, send, recv, loc):
        locs, remote, mine, _ = copies(ins, outs, send, recv, loc)
        for cp in locs:
            cp.start()
        for a in range(n):
            for k in range(7):
                remote(a, k, mine).start()

    def wait(ins, outs, send, recv, loc):
        locs, remote, _, peers = copies(ins, outs, send, recv, loc)
        for a in range(n):
            for k in range(7):
                cp = remote(a, k, peers[k])
                cp.wait_recv()
                cp.wait_send()
        for cp in locs:
            cp.wait()

    outs = [jax.ShapeDtypeStruct((8, p.shape[1] // 2, p.shape[2]), p.dtype) for p in parts]
    return Plan(list(parts), outs, 7 * n, n, start, wait)


def sum_share(slots, name="rs_sum"):
    n, r, cols = slots.shape
    tr = _tile(r, (128, 176, 64))
    steps = r // tr

    def body(s_ref, g_ref, buf, send, loc, recv):
        i = pl.program_id(0)
        x, y, c = _me()
        slot = i % 2

        def copies(step, sl):
            rows = pl.ds(pl.multiple_of(c * r + step * tr, 8), tr)
            rem = pltpu.make_async_remote_copy(buf.at[sl], g_ref.at[rows, :], send.at[sl], recv.at[0],
                                               device_id=(x, y, 1 - c), device_id_type=MESH)
            return rem, pltpu.make_async_copy(buf.at[sl], g_ref.at[rows, :], loc.at[sl])

        @pl.when(i >= 2)
        def _():
            rem, lc = copies(i - 2, slot)
            rem.wait_send()
            lc.wait()

        acc = s_ref[0].astype(f32)
        for j in range(1, n):
            acc = acc + s_ref[j].astype(f32)
        buf[slot] = acc
        rem, lc = copies(i, slot)
        rem.start()
        lc.start()

        @pl.when(i == steps - 1)
        def _():
            for back in range(min(2, steps)):
                rem, lc = copies(i - back, (i - back) % 2)
                rem.wait_send()
                lc.wait()
            other = g_ref.at[pl.ds(pl.multiple_of((1 - c) * r, 8), r), :]
            pltpu.make_async_remote_copy(other, other, send.at[0], recv.at[0],
                                         device_id=(x, y, 1 - c), device_id_type=MESH).wait_recv()

    return pl.pallas_call(
        body, name=name, grid=(steps,),
        in_specs=[pl.BlockSpec((n, tr, cols), lambda i: (0, i, 0))], out_specs=ANY,
        out_shape=jax.ShapeDtypeStruct((2 * r, cols), f32),
        scratch_shapes=[pltpu.VMEM((2, tr, cols), f32), pltpu.SemaphoreType.DMA((2,)), pltpu.SemaphoreType.DMA((2,)),
                        pltpu.SemaphoreType.DMA((1,))],
        compiler_params=_cp(("arbitrary",)),
    )(slots)


BIG = ("ffn1_w_in", "ffn1_w_out", "w_in", "w_branch_a", "w_branch_b", "w_branch_c", "w_out", "ffn2_w_in", "ffn2_w_out")
ROW_SHARDED = ("ffn1_w_out", "w_out", "ffn2_w_out")
RES_W = (0.5, 1.0, 0.5)


def _full_weight(name, g):
    if name in ROW_SHARDED:
        return g.reshape(4 * g.shape[1], g.shape[2])
    return jnp.concatenate([g[0], g[1], g[2], g[3]], axis=1)


def _by_shard(name, dw):
    if name in ROW_SHARDED:
        return dw.reshape(4, dw.shape[0] // 4, dw.shape[1])
    return dw.reshape(dw.shape[0], 4, dw.shape[1] // 4).transpose(1, 0, 2)


def _full_weight_t(name, g):
    if name in ROW_SHARDED:
        return g.reshape(4 * g.shape[1], g.shape[2]).T
    return g.transpose(0, 2, 1).reshape(4 * g.shape[2], g.shape[1])


def _ffn_fwd(x, w_in, w_out, a_vec, sh_vec, b_vec, plans=(None, None)):
    h, h_t = prenorm(x, a_vec, sh_vec)
    (ua, ub, s, s_t), side0 = ffn_in_swiglu(h, w_in, side=plans[0])
    y = mm(s, w_out, name="ffn_out", side=plans[1])
    side1 = None
    if plans[1] is not None:
        y, side1 = y
    return postnorm(x, y, b_vec), (x, h_t, ua, ub, s_t, y), (side0, side1)


def _ffn_bwd(dout, saved, w_in_t, w_out_t, a_vec, b_vec, plans=None):
    x, h_t, ua, ub, s_t, y = saved
    riders = plans or (None, None, None)
    dy, db = post_bwd(dout, y, b_vec)
    dw_out = mm(s_t, dy, out_dtype=bf16, name="ffn_dwo", side=riders[0])
    du = ffn_du(dy, w_out_t, ua, ub)
    dh = mm(du, w_in_t, name="ffn_dh", side=riders[1])
    sides = []
    if plans:
        (dw_out, s0), (dh, s1) = dw_out, dh
        sides = [s0, s1]
    dw_in = mm(h_t, du, out_dtype=bf16, name="ffn_dwi", side=riders[2])
    if plans:
        dw_in, s2 = dw_in
        sides.append(s2)
    dx, dsh, da = pre_bwd(dout, dh, x, a_vec)
    return (dx, dw_in, dw_out, dsh, da, db) + ((sides,) if plans else ())


def _mix_fwd(x, w, lb, ng, a_vec, sh_vec, b_vec, plans=(None,) * 6):
    h, h_t = prenorm(x, a_vec, sh_vec)
    u = mm(h, w["w_in"], name="mix_in", side=plans[0])
    side0 = None
    if plans[0] is not None:
        u, side0 = u
    (o, ya, st), side1 = hgrn_fwd(u, lb, ng, side=plans[1])
    yb, side2 = sb_fwd(u, side=plans[2])
    groups, dil_sides = [], []
    for g in range(3):
        og, sg = dil_fwd(u, g, side=plans[3 + g])
        groups.append(og)
        dil_sides.append(sg)
    yc, lse = dil_merge([o_ for o_, _ in groups], [l_ for _, l_ in groups])
    pa = mm(ya, w["w_branch_a"], name="mix_pa")
    pb = mm(yb, w["w_branch_b"], name="mix_pb")
    pc = mm(yc, w["w_branch_c"], name="mix_pc")
    merged, merged_t = gate_merge(u, pa, pb, pc)
    z = mm(merged, w["w_out"], name="mix_out")
    return (postnorm(x, z, b_vec), (x, h_t, u, o, ya, st, yb, yc, lse, pa, pb, pc, merged_t, z),
            (side0, side1, side2, *dil_sides))


def _mix_bwd(dout, saved, wt, lb, ng, a_vec, b_vec, plans=(None,) * 3):
    x, h_t, u, o, ya, st, yb, yc, lse, pa, pb, pc, merged_t, z = saved
    dz, db = post_bwd(dout, z, b_vec)
    dmerged = mm(dz, wt["w_out"], name="mix_dm")
    dw_out = mm(merged_t, dz, out_dtype=bf16, name="mix_dwo")
    dpa, dpb, dpc, dg0, dg1, dg2 = gate_bwd(dmerged, u, pa, pb, pc)
    dya = mm(dpa, wt["w_branch_a"], name="mix_dya")
    dyb = mm(dpb, wt["w_branch_b"], name="mix_dyb")
    dyc = mm(dpc, wt["w_branch_c"], name="mix_dyc")
    dw_a = mm(ya.T, dpa, out_dtype=bf16, name="mix_dwa")
    dw_b = mm(yb.astype(bf16).T, dpb, out_dtype=bf16, name="mix_dwb")
    dw_c = mm(yc.astype(bf16).T, dpc, out_dtype=bf16, name="mix_dwc")
    (daq, daf, dai, dag, dlb, dng), side0 = hgrn_bwd(u, lb, ng, o, st, dya, side=plans[0])
    (dbq, dbk, dbv), side1 = sb_bwd(u, yb, dyb, side=plans[1])
    dc = [dil_bwd(u, g, dyc, yc, lse) for g in range(3)]
    du = jnp.concatenate(
        [daq, daf, dai, dag] + [t.astype(bf16) for t in (dbq, dbk, dbv)]
        + [dc[g][j].astype(bf16) for j in range(3) for g in range(3)] + [dg0, dg1, dg2], axis=1)
    dh = mm(du, wt["w_in"], name="mix_dh")
    dw_in = mm(h_t, du, out_dtype=bf16, name="mix_dwi", side=plans[2])
    side2 = None
    if plans[2] is not None:
        dw_in, side2 = dw_in
    dx, dsh, da = pre_bwd(dout, dh, x, a_vec)
    grads = {"w_in": dw_in, "w_out": dw_out, "w_branch_a": dw_a, "w_branch_b": dw_b, "w_branch_c": dw_c}
    return dx, grads, dlb, jnp.sum(dng, axis=0), dsh, da, db, (side0, side1, side2)


NEXT_RIDERS = (("ffn1_w_out",), (), ("ffn2_w_in",), ("ffn1_w_in",), ("w_in",), (), (),
               ("w_out", "w_branch_a", "w_branch_b", "w_branch_c"), ("ffn2_w_out",), ())
FIRST = ("ffn1_w_in", "ffn1_w_out", "w_in", "w_out", "w_branch_a", "w_branch_b", "w_branch_c")
LATE_RIDERS = ((), (), (), (), (), ("ffn2_w_in",), ("ffn2_w_out",), (), (), ())
BWD_RIDERS = (("ffn1_w_in", "ffn2_w_out", "w_branch_a", "w_branch_b", "w_branch_c"), ("w_in", "ffn1_w_out", "w_out"),
              ("ffn2_w_in",))
LAST_RIDERS = (("ffn2_w_out", "w_out", "w_branch_a", "w_branch_b", "w_branch_c"), ("ffn2_w_in",), ("w_in",))
TAIL = ("ffn1_w_in", "ffn1_w_out")


def kernel(x, c, w_ada, b_ada, norm_g, ffn1_w_in, ffn1_w_out, w_in, hgrn_lb_logits, hgrn_norm_g, w_branch_a, w_branch_b, w_branch_c, w_out, ffn2_w_in, ffn2_w_out, loss_target, m_w_ada, m_b_ada, m_norm_g, m_ffn1_w_in, m_ffn1_w_out, m_w_in, m_hgrn_lb_logits, m_hgrn_norm_g, m_w_branch_a, m_w_branch_b, m_w_branch_c, m_w_out, m_ffn2_w_in, m_ffn2_w_out, v_w_ada, v_b_ada, v_norm_g, v_ffn1_w_in, v_ffn1_w_out, v_w_in, v_hgrn_lb_logits, v_hgrn_norm_g, v_w_branch_a, v_w_branch_b, v_w_branch_c, v_w_out, v_ffn2_w_in, v_ffn2_w_out):
    weights = dict(w_ada=w_ada, b_ada=b_ada, norm_g=norm_g, ffn1_w_in=ffn1_w_in, ffn1_w_out=ffn1_w_out, w_in=w_in,
                   hgrn_lb_logits=hgrn_lb_logits, hgrn_norm_g=hgrn_norm_g, w_branch_a=w_branch_a, w_branch_b=w_branch_b,
                   w_branch_c=w_branch_c, w_out=w_out, ffn2_w_in=ffn2_w_in, ffn2_w_out=ffn2_w_out)
    mom = dict(w_ada=m_w_ada, b_ada=m_b_ada, norm_g=m_norm_g, ffn1_w_in=m_ffn1_w_in, ffn1_w_out=m_ffn1_w_out, w_in=m_w_in,
               hgrn_lb_logits=m_hgrn_lb_logits, hgrn_norm_g=m_hgrn_norm_g, w_branch_a=m_w_branch_a, w_branch_b=m_w_branch_b,
               w_branch_c=m_w_branch_c, w_out=m_w_out, ffn2_w_in=m_ffn2_w_in, ffn2_w_out=m_ffn2_w_out)
    var = dict(w_ada=v_w_ada, b_ada=v_b_ada, norm_g=v_norm_g, ffn1_w_in=v_ffn1_w_in, ffn1_w_out=v_ffn1_w_out, w_in=v_w_in,
               hgrn_lb_logits=v_hgrn_lb_logits, hgrn_norm_g=v_hgrn_norm_g, w_branch_a=v_w_branch_a, w_branch_b=v_w_branch_b,
               w_branch_c=v_w_branch_c, w_out=v_w_out, ffn2_w_in=v_ffn2_w_in, ffn2_w_out=v_ffn2_w_out)
    order = list(weights)
    xi, yi, ci = _me()
    chip = 2 * xi + yi
    dev = 4 * xi + 2 * yi + ci
    xs = x[0]

    c_all = all_gather_devs(c, name="ag8_c").reshape(8, D)
    mod_sh = all_gather_chips([ada_fwd(c_all, w_ada)], name="ag4_mod")[0]
    mod_all = mod_sh.transpose(1, 2, 0, 3).reshape(DEPTH, 8, 9 * D)
    mod = lax.dynamic_index_in_dim(mod_all, dev, axis=1, keepdims=False) + b_ada
    mod = mod.reshape(DEPTH, 3, 3, D)
    ng_all = all_gather_chips([norm_g.reshape(DEPTH * 6, D // 4)], name="ag4_norm")[0]
    ng_all = ng_all.reshape(4, DEPTH, 6, D // 4).transpose(1, 2, 0, 3).reshape(DEPTH, 6, D)
    lb_all = lb_fwd(hgrn_lb_logits)
    w16 = {n: cast_bf16(weights[n]) for n in BIG}

    def vecs(l, i):
        shift, scale, gate = mod[l, i, 0][None], mod[l, i, 1][None], mod[l, i, 2][None]
        g_pre, g_post = ng_all[l, 2 * i][None], ng_all[l, 2 * i + 1][None]
        return g_pre * (1.0 + scale), shift, RES_W[i] * gate * g_post

    saved, full = [], []
    gathered = dict(zip(FIRST, all_gather_chips([w16[n] for n in FIRST], layer=0, name="ag4_w0")))
    for l in range(DEPTH):
        riders = [[(n, l) for n in (LATE_RIDERS[h] if l == 0 else ())]
                  + [(n, l + 1) for n in (NEXT_RIDERS[h] if l + 1 < DEPTH else ())] for h in range(len(NEXT_RIDERS))]
        plans = [gather_chips_plan([w16[n] for n, _ in it], layer=[ll for _, ll in it]) if it else None for it in riders]
        coming = {}

        def absorb(hosts, sides):
            for h, outs in zip(hosts, sides):
                for (n, ll), g in zip(riders[h], outs or ()):
                    (gathered if ll == l else coming)[n] = g

        full_w = lambda n: _full_weight(n, gathered[n])
        lb, ng = lb_all[l][None], hgrn_norm_g[l][None]
        xs, s1, sides = _ffn_fwd(xs, full_w("ffn1_w_in"), full_w("ffn1_w_out"), *vecs(l, 0), plans=plans[0:2])
        absorb((0, 1), sides)
        w = {n: full_w(n) for n in ("w_in", "w_branch_a", "w_branch_b", "w_branch_c", "w_out")}
        xs, s2, sides = _mix_fwd(xs, w, lb, ng, *vecs(l, 1), plans=plans[2:8])
        absorb(range(2, 8), sides)
        xs, s3, sides = _ffn_fwd(xs, full_w("ffn2_w_in"), full_w("ffn2_w_out"), *vecs(l, 2), plans=plans[8:10])
        absorb((8, 9), sides)
        saved.append((s1, s2, s3))
        full.append({n: _full_weight_t(n, gathered[n]) for n in BIG})
        gathered = coming

    dx, loss_part = loss_grad(xs, loss_target[0])
    loss = lax.psum(loss_part[0, 0], ("x", "y", "c"))

    big_grads = {n: [None] * DEPTH for n in BIG}
    d_mod, d_ng, d_lb, d_hng = [None] * DEPTH, [None] * DEPTH, [None] * DEPTH, [None] * DEPTH
    pending = None
    for l in reversed(range(DEPTH)):
        wt = full[l]
        s1, s2, s3 = saved[l]
        lb, ng = lb_all[l][None], hgrn_norm_g[l][None]
        rows_mod, rows_ng = [None] * 9, [None] * 6

        def vec_grads(i, dsh, da, db):
            scale, gate = mod[l, i, 1][None], mod[l, i, 2][None]
            g_pre, g_post = ng_all[l, 2 * i][None], ng_all[l, 2 * i + 1][None]
            rows_mod[3 * i], rows_mod[3 * i + 1], rows_mod[3 * i + 2] = dsh, g_pre * da, RES_W[i] * g_post * db
            rows_ng[2 * i], rows_ng[2 * i + 1] = (1.0 + scale) * da, RES_W[i] * gate * db

        a3, _, b3 = vecs(l, 2)
        dx, dwi, dwo, dsh, da, db = _ffn_bwd(dx, s3, wt["ffn2_w_in"], wt["ffn2_w_out"], a3, b3)
        vec_grads(2, dsh, da, db)
        grads = {"ffn2_w_in": dwi, "ffn2_w_out": dwo}
        a2, _, b2 = vecs(l, 1)
        plans = (None,) * len(BWD_RIDERS)
        if pending is not None:
            plans = tuple(dev_exchange_plan([pending[n] for n in names]) for names in BWD_RIDERS)
        dx, gmix, dlb, dhng, dsh, da, db, sides = _mix_bwd(dx, s2, wt, lb, ng, a2, b2, plans=plans)
        if pending is not None:
            for names, outs in zip(BWD_RIDERS, sides):
                for n, slots in zip(names, outs):
                    big_grads[n][l + 1] = sum_share(slots)
        vec_grads(1, dsh, da, db)
        grads.update(gmix)
        a1, _, b1 = vecs(l, 0)
        if l > 0:
            dx, dwi, dwo, dsh, da, db = _ffn_bwd(dx, s1, wt["ffn1_w_in"], wt["ffn1_w_out"], a1, b1)
        else:
            ready = {n: _by_shard(n, grads[n]) for names in LAST_RIDERS for n in names}
            plans = tuple(dev_exchange_plan([ready[n] for n in names]) for names in LAST_RIDERS)
            dx, dwi, dwo, dsh, da, db, sides = _ffn_bwd(dx, s1, wt["ffn1_w_in"], wt["ffn1_w_out"], a1, b1, plans=plans)
            for names, outs in zip(LAST_RIDERS, sides):
                for n, slots in zip(names, outs):
                    big_grads[n][0] = sum_share(slots)
        vec_grads(0, dsh, da, db)
        grads.update({"ffn1_w_in": dwi, "ffn1_w_out": dwo})
        pending = {n: _by_shard(n, grads[n]) for n in (BIG if l > 0 else TAIL)}
        d_mod[l] = jnp.concatenate(rows_mod, axis=1)
        d_ng[l] = jnp.concatenate(rows_ng, axis=0)
        d_lb[l], d_hng[l] = dlb, dhng

    n_small = 6 * D * DEPTH + 768 * DEPTH + A_V * DEPTH + 9 * D * DEPTH
    pad = -n_small % (512 * LANE)
    flat = jnp.concatenate([jnp.stack(d_ng).reshape(-1), jnp.concatenate(d_lb, axis=0).reshape(-1),
                            jnp.concatenate(d_hng, axis=0).reshape(-1), jnp.concatenate(d_mod, axis=0).reshape(-1),
                            jnp.zeros((pad,), f32)])
    small_all = all_gather_devs(flat.reshape(-1, LANE), name="ag8_small")
    total = sum_slots(small_all).reshape(-1)
    o1 = 6 * D * DEPTH
    o2 = o1 + 768 * DEPTH
    o3 = o2 + A_V * DEPTH
    g_ng_full = total[:o1].reshape(DEPTH, 6, D)
    g_lb_all = total[o1:o2].reshape(DEPTH, 768)
    g_small = {
        "norm_g": lax.dynamic_slice_in_dim(g_ng_full, chip * (D // 4), D // 4, axis=2),
        "hgrn_lb_logits": lb_bwd(hgrn_lb_logits, g_lb_all),
        "hgrn_norm_g": total[o2:o3].reshape(DEPTH, A_V),
        "b_ada": total[o3:n_small].reshape(DEPTH, 9 * D),
    }
    dmod_all = small_all.reshape(8, -1)[:, o3:n_small].reshape(8, DEPTH, 9 * D).transpose(1, 0, 2)
    dm_sh = lax.dynamic_slice_in_dim(dmod_all, chip * ADA_N, ADA_N, axis=2)

    out_g, out_d, out_m, out_v = {}, {}, {}, {}
    ada_outs, slots = ada_bwd_adam(c_all, dm_sh, w_ada, m_w_ada, v_w_ada, side=dev_exchange_plan([pending[n] for n in TAIL]))
    out_g["w_ada"], out_d["w_ada"], out_m["w_ada"], out_v["w_ada"] = ada_outs
    for n, s in zip(TAIL, slots):
        big_grads[n][0] = sum_share(s)
    for n in BIG:
        out_g[n], out_d[n], out_m[n], out_v[n] = adam_layers(weights[n], big_grads[n], mom[n], var[n])
    out_g.update(g_small)
    for n in g_small:
        out_d[n], out_m[n], out_v[n] = adam(weights[n], out_g[n], mom[n], var[n])
    return (loss, dx[None], *[out_g[n] for n in order], *[out_d[n] for n in order],
            *[out_m[n] for n in order], *[out_v[n] for n in order])
```
